```python
import jax
import jax.numpy as jnp
from jax import lax
import numpy as np


D_MODEL = 1024
BATCH = 4
SEQ = 8192
DEPTH = 1

GRID_W = 64
MEM_LEN = 256
NA_HEADS = 8
NA_HEAD_DIM = 64
NA_WIN_ROWS = 8
NA_WIN_COLS = 16
FT_GROUPS = 4
FT_GROUP_DIM = 128
MEM_HEADS = 4
MEM_HEAD_DIM = 128
NA_WIDTH = NA_HEADS * NA_HEAD_DIM
FT_WIDTH = FT_GROUPS * FT_GROUP_DIM
MEM_WIDTH = MEM_HEADS * MEM_HEAD_DIM
MIX_WIDTH = NA_WIDTH + FT_WIDTH + MEM_WIDTH
IN_WIDTH = 3 * NA_WIDTH + FT_WIDTH + MEM_WIDTH
N_EXPERTS = 32
TOP_K = 4
D_EXPERT = D_MODEL
SWIGLU_LIMIT = 7.0
SWIGLU_ALPHA = 1.702
MOE_BLOCK = 512
EPS = 1e-6

kernel_name = 'hymba_natten_fnet_memory_moe_encoder'


def rmsnorm(x, g):
    xf = x.astype(jnp.float32)
    y = xf * lax.rsqrt(jnp.mean(xf * xf, axis=-1, keepdims=True) + EPS)
    return (y * g.astype(jnp.float32)).astype(x.dtype)


def neighbourhood_attention(q, k, v, rel_bias):
    b, s = q.shape[0], q.shape[1]
    rows = s // GRID_W
    win_r = min(NA_WIN_ROWS, rows)

    def to_grid(t):
        return t.reshape(b, rows, GRID_W, NA_HEADS, NA_HEAD_DIM).transpose(0, 3, 1, 2, 4)

    qg, kg, vg = to_grid(q), to_grid(k), to_grid(v)
    r = np.arange(rows)
    row_start = np.clip(r - win_r // 2, 0, rows - win_r)
    row_idx = row_start[:, None] + np.arange(win_r)[None, :]
    k_rows = kg[:, :, row_idx]
    v_rows = vg[:, :, row_idx]
    c = np.arange(GRID_W)
    col_start = np.clip(c - NA_WIN_COLS // 2, 0, GRID_W - NA_WIN_COLS)
    col_in = (c[None, :] >= col_start[:, None]) & (c[None, :] < col_start[:, None] + NA_WIN_COLS)
    dr = row_idx - r[:, None]
    dc = np.clip(c[None, :] - c[:, None], -(NA_WIN_COLS - 1), NA_WIN_COLS - 1)
    bias = rel_bias[:, dr[:, None, :, None] + (NA_WIN_ROWS - 1), dc[None, :, None, :] + (NA_WIN_COLS - 1)]
    scores = jnp.einsum('bhrqd,bhrjcd->bhrqjc', qg, k_rows).astype(jnp.float32) * (NA_HEAD_DIM ** -0.5)
    scores = jnp.where(col_in[None, None, None, :, None, :], scores + bias[None].astype(jnp.float32), -jnp.inf)
    p = jax.nn.softmax(scores.reshape(b, NA_HEADS, rows, GRID_W, win_r * GRID_W), axis=-1).astype(v.dtype)
    o = jnp.einsum('bhrqk,bhrkd->bhrqd', p, v_rows.reshape(b, NA_HEADS, rows, win_r * GRID_W, NA_HEAD_DIM))
    return o.transpose(0, 2, 3, 1, 4).reshape(b, s, NA_WIDTH)


def fourier_mix(u):
    b, s = u.shape[0], u.shape[1]
    ug = u.astype(jnp.float32).reshape(b, s, FT_GROUPS, FT_GROUP_DIM)
    f = jnp.fft.fft2(ug, axes=(1, 3), norm='ortho')
    return jnp.real(f).reshape(b, s, FT_WIDTH).astype(u.dtype)


def memory_attention(q, mem_n, w_mem_kv):
    b, s = q.shape[0], q.shape[1]
    kv = mem_n @ w_mem_kv
    k = kv[..., :MEM_WIDTH].reshape(b, -1, MEM_HEADS, MEM_HEAD_DIM)
    v = kv[..., MEM_WIDTH:].reshape(b, -1, MEM_HEADS, MEM_HEAD_DIM)
    qh = q.reshape(b, s, MEM_HEADS, MEM_HEAD_DIM)
    scores = jnp.einsum('bshd,bmhd->bhsm', qh, k).astype(jnp.float32) * (MEM_HEAD_DIM ** -0.5)
    p = jax.nn.softmax(scores, axis=-1).astype(v.dtype)
    return jnp.einsum('bhsm,bmhd->bshd', p, v).reshape(b, s, MEM_WIDTH)


def routed_ffn(h, router_w, router_b, w_gu, b_gu, w_down, b_down):
    b, s, d = h.shape
    t = b * s
    hf = h.reshape(t, d)
    logits = (hf @ router_w + router_b).astype(jnp.float32)
    top_val, top_idx = lax.top_k(logits, TOP_K)
    gates = jax.nn.softmax(top_val, axis=-1).astype(h.dtype)
    n_assign = t * TOP_K
    e_flat = top_idx.reshape(-1)
    tok_flat = jnp.repeat(jnp.arange(t, dtype=jnp.int32), TOP_K)
    order = jnp.argsort(e_flat, stable=True)
    e_sorted = e_flat[order]
    tok_sorted = tok_flat[order]
    gate_sorted = gates.reshape(-1)[order]
    counts = jnp.bincount(e_flat, length=N_EXPERTS)
    padded = (counts + MOE_BLOCK - 1) // MOE_BLOCK * MOE_BLOCK
    padded_end = jnp.cumsum(padded)
    group_start = jnp.cumsum(counts) - counts
    dest = (padded_end - padded)[e_sorted] + jnp.arange(n_assign, dtype=jnp.int32) - group_start[e_sorted]
    n_blocks = -(-n_assign // MOE_BLOCK) + N_EXPERTS
    n_slots = n_blocks * MOE_BLOCK
    slot_tok = jnp.zeros((n_slots,), jnp.int32).at[dest].set(tok_sorted)
    slot_gate = jnp.zeros((n_slots,), h.dtype).at[dest].set(gate_sorted)
    block_expert = jnp.minimum(
        jnp.searchsorted(padded_end, jnp.arange(n_blocks, dtype=jnp.int32) * MOE_BLOCK, side='right'),
        N_EXPERTS - 1)

    def expert_block(args):
        tok_b, gate_b, e = args
        xb = hf[tok_b]
        gu = xb @ w_gu[e] + b_gu[e]
        x_glu = jnp.minimum(gu[:, :D_EXPERT], SWIGLU_LIMIT)
        x_lin = jnp.clip(gu[:, D_EXPERT:], -SWIGLU_LIMIT, SWIGLU_LIMIT)
        act = x_glu * jax.nn.sigmoid(SWIGLU_ALPHA * x_glu) * (x_lin + 1.0)
        y = act @ w_down[e] + b_down[e]
        return y * gate_b[:, None]

    y = lax.map(expert_block, (slot_tok.reshape(n_blocks, MOE_BLOCK), slot_gate.reshape(n_blocks, MOE_BLOCK), block_expert))
    out = jax.ops.segment_sum(y.reshape(n_slots, d), slot_tok, num_segments=t)
    return out.reshape(b, s, d)


def setup_inputs(seed: int = 0) -> dict:
    key = jax.random.key(seed)
    ks = jax.random.split(key, 18)

    def nrm(k, shape, scale):
        return jax.random.normal(k, shape, jnp.float32) * scale

    return {
        'x': nrm(ks[0], (BATCH, SEQ, D_MODEL), 1.0),
        'mem': nrm(ks[1], (BATCH, MEM_LEN, D_MODEL), 1.0),
        'g_mix': 1.0 + nrm(ks[2], (DEPTH, D_MODEL), 0.02),
        'g_mem': 1.0 + nrm(ks[3], (DEPTH, D_MODEL), 0.02),
        'w_in': nrm(ks[4], (DEPTH, D_MODEL, IN_WIDTH), D_MODEL ** -0.5),
        'w_mem_kv': nrm(ks[5], (DEPTH, D_MODEL, 2 * MEM_WIDTH), D_MODEL ** -0.5),
        'na_rel_bias': nrm(ks[6], (DEPTH, NA_HEADS, 2 * NA_WIN_ROWS - 1, 2 * NA_WIN_COLS - 1), 0.1),
        'g_grp': 1.0 + nrm(ks[7], (DEPTH, MIX_WIDTH), 0.02),
        'w_out': nrm(ks[8], (DEPTH, MIX_WIDTH, D_MODEL), MIX_WIDTH ** -0.5),
        'g_ffn': 1.0 + nrm(ks[9], (DEPTH, D_MODEL), 0.02),
        'router_w': nrm(ks[10], (DEPTH, D_MODEL, N_EXPERTS), D_MODEL ** -0.5),
        'router_b': nrm(ks[11], (DEPTH, N_EXPERTS), 0.01),
        'w_gu': nrm(ks[12], (DEPTH, N_EXPERTS, D_MODEL, 2 * D_EXPERT), D_MODEL ** -0.5),
        'b_gu': nrm(ks[13], (DEPTH, N_EXPERTS, 2 * D_EXPERT), 0.01),
        'w_down': nrm(ks[14], (DEPTH, N_EXPERTS, D_EXPERT, D_MODEL), D_EXPERT ** -0.5),
        'b_down': nrm(ks[15], (DEPTH, N_EXPERTS, D_MODEL), 0.01),
        'g_final': 1.0 + nrm(ks[16], (D_MODEL,), 0.02),
    }


def reference(x, mem, g_mix, g_mem, w_in, w_mem_kv, na_rel_bias, g_grp, w_out, g_ffn,
              router_w, router_b, w_gu, b_gu, w_down, b_down, g_final):
    splits = [NA_WIDTH, 2 * NA_WIDTH, 3 * NA_WIDTH, 3 * NA_WIDTH + FT_WIDTH]
    for l in range(DEPTH):
        h = rmsnorm(x, g_mix[l])
        proj = h @ w_in[l]
        q_na, k_na, v_na, u_ft, q_mem = jnp.split(proj, splits, axis=-1)
        mem_n = rmsnorm(mem, g_mem[l])
        y_na = neighbourhood_attention(q_na, k_na, v_na, na_rel_bias[l])
        y_ft = fourier_mix(u_ft)
        y_mem = memory_attention(q_mem, mem_n, w_mem_kv[l])
        g = g_grp[l]
        y = jnp.concatenate([
            rmsnorm(y_na, g[:NA_WIDTH]),
            rmsnorm(y_ft, g[NA_WIDTH:NA_WIDTH + FT_WIDTH]),
            rmsnorm(y_mem, g[NA_WIDTH + FT_WIDTH:]),
        ], axis=-1)
        x = x + y @ w_out[l]
        h2 = rmsnorm(x, g_ffn[l])
        x = x + routed_ffn(h2, router_w[l], router_b[l], w_gu[l], b_gu[l], w_down[l], b_down[l])
    return rmsnorm(x, g_final)
```

```python
import functools

import numpy as np
import jax
import jax.numpy as jnp
from jax import lax
from jax.experimental import pallas as pl
from jax.experimental.pallas import tpu as pltpu

f32 = jnp.float32
bf16 = jnp.bfloat16
u32 = jnp.uint32
i32 = jnp.int32

GRID_W = 64
NA_HEADS = 8
NA_HEAD_DIM = 64
NA_WIN_ROWS = 8
NA_WIN_COLS = 16
FT_GROUPS = 4
FT_GROUP_DIM = 128
MEM_HEADS = 4
MEM_HEAD_DIM = 128
NA_WIDTH = NA_HEADS * NA_HEAD_DIM
FT_WIDTH = FT_GROUPS * FT_GROUP_DIM
MEM_WIDTH = MEM_HEADS * MEM_HEAD_DIM
N_EXPERTS = 32
TOP_K = 4
SWIGLU_LIMIT = 7.0
SWIGLU_ALPHA = 1.702
EPS = 1e-6

LANES = 128
SUBLANES = 8
VMEM_LIMIT_BYTES = 56 * 1024 * 1024

TOKEN_TILE = 512
MOE_BLOCK = 512
FT_N1 = 64
FT_N2 = 128
FT_K1_BLOCK = 8
MASK_VALUE = -1e30


def _params(*semantics):
    return pltpu.CompilerParams(dimension_semantics=semantics, vmem_limit_bytes=VMEM_LIMIT_BYTES)


def _rms_scale(x):
    return x * lax.rsqrt(jnp.mean(x * x, axis=-1, keepdims=True) + EPS)


def _softmax_rows(s):
    p = jnp.exp(s - jnp.max(s, axis=-1, keepdims=True))
    return p / jnp.sum(p, axis=-1, keepdims=True)


def _pack_bf16_pairs(x):
    n = x.shape[1] // 2
    bits = pltpu.bitcast(x.astype(bf16).astype(f32), u32)
    return (bits[:, :n] >> 16) | (bits[:, n:] & jnp.uint32(0xFFFF0000))


def _unpack_bf16_pairs(w):
    lo = pltpu.bitcast(w << 16, f32)
    hi = pltpu.bitcast(w & jnp.uint32(0xFFFF0000), f32)
    return jnp.concatenate([lo, hi], axis=1)


def _in_proj_kernel(x_ref, g_ref, w_ref, qna_ref, kna_ref, vna_ref, uft_ref, qmem_ref):
    h = _rms_scale(x_ref[...]) * g_ref[...]
    proj = jnp.dot(h.astype(bf16), w_ref[...], preferred_element_type=f32)
    o = NA_WIDTH
    qna_ref[...] = (proj[:, :o] * (NA_HEAD_DIM ** -0.5)).astype(bf16)
    kna_ref[...] = proj[:, o:2 * o].astype(bf16)
    vna_ref[...] = proj[:, 2 * o:3 * o].astype(bf16)
    uft_ref[...] = proj[:, 3 * o:3 * o + FT_WIDTH]
    qmem_ref[...] = proj[:, 3 * o + FT_WIDTH:].astype(bf16)


def _in_proj(x2d, g_mix, w_in_bf16):
    t, d = x2d.shape
    tm = TOKEN_TILE
    row = lambda w: pl.BlockSpec((tm, w), lambda i: (i, 0))
    return pl.pallas_call(
        _in_proj_kernel,
        grid=(t // tm,),
        in_specs=[row(d), pl.BlockSpec((1, d), lambda i: (0, 0)),
                  pl.BlockSpec(w_in_bf16.shape, lambda i: (0, 0))],
        out_specs=[row(NA_WIDTH), row(NA_WIDTH), row(NA_WIDTH), row(FT_WIDTH), row(MEM_WIDTH)],
        out_shape=[jax.ShapeDtypeStruct((t, NA_WIDTH), bf16)] * 3
        + [jax.ShapeDtypeStruct((t, FT_WIDTH), f32), jax.ShapeDtypeStruct((t, MEM_WIDTH), bf16)],
        compiler_params=_params("parallel"),
        name="in_proj",
    )(x2d, g_mix.reshape(1, d), w_in_bf16)


def _na_bias_table(rel_bias):
    s = np.arange(NA_WIN_ROWS)[:, None]
    j = np.arange(NA_WIN_ROWS)[None, :]
    dr_idx = (j - s) + (NA_WIN_ROWS - 1)
    c = np.arange(GRID_W)
    dc_idx = np.clip(c[None, :] - c[:, None], -(NA_WIN_COLS - 1), NA_WIN_COLS - 1) + (NA_WIN_COLS - 1)
    col_start = np.clip(c - NA_WIN_COLS // 2, 0, GRID_W - NA_WIN_COLS)
    col_in = (c[None, :] >= col_start[:, None]) & (c[None, :] < col_start[:, None] + NA_WIN_COLS)
    tab = rel_bias[:, dr_idx[:, None, :, None], dc_idx[None, :, None, :]]
    tab = jnp.where(col_in[None, None, :, None, :], tab.astype(f32), MASK_VALUE)
    tab = tab.reshape(NA_HEADS // 2, 2, NA_WIN_ROWS, GRID_W, NA_WIN_ROWS * GRID_W)
    return tab.transpose(2, 0, 1, 3, 4).reshape(NA_WIN_ROWS, NA_HEADS // 2, 2 * GRID_W, NA_WIN_ROWS * GRID_W)


def _na_kernel(q_ref, k_ref, v_ref, bias_ref, o_ref):
    rows = q_ref.shape[1] // GRID_W
    win = NA_WIN_ROWS * GRID_W
    first_head = lax.broadcasted_iota(i32, (GRID_W, 2 * NA_HEAD_DIM), 1) < NA_HEAD_DIM

    def body(r, carry):
        row_start = jnp.clip(r - NA_WIN_ROWS // 2, 0, rows - NA_WIN_ROWS)
        q0 = pl.multiple_of(r * GRID_W, GRID_W)
        k0 = pl.multiple_of(row_start * GRID_W, GRID_W)
        q = q_ref[0, pl.ds(q0, GRID_W), :]
        zero = jnp.zeros_like(q)
        qm = jnp.concatenate([jnp.where(first_head, q, zero), jnp.where(first_head, zero, q)], axis=0)
        kw = k_ref[0, pl.ds(k0, win), :]
        vw = v_ref[0, pl.ds(k0, win), :]
        s = lax.dot_general(qm, kw, (((1,), (1,)), ((), ())), preferred_element_type=f32)
        p = _softmax_rows(s + bias_ref[r - row_start, 0])
        o = jnp.dot(p.astype(bf16), vw, preferred_element_type=f32)
        o_ref[0, pl.ds(q0, GRID_W), :] = jnp.where(first_head, o[:GRID_W], o[GRID_W:]).astype(o_ref.dtype)
        return carry

    lax.fori_loop(0, rows, body, 0)


def _neighbourhood_attention(q, k, v, bias_tab):
    b, s, _ = q.shape
    pair = 2 * NA_HEAD_DIM
    qkv_spec = pl.BlockSpec((1, s, pair), lambda bi, hp: (bi, 0, hp))
    return pl.pallas_call(
        _na_kernel,
        grid=(b, NA_HEADS // 2),
        in_specs=[qkv_spec, qkv_spec, qkv_spec,
                  pl.BlockSpec((NA_WIN_ROWS, 1, 2 * GRID_W, NA_WIN_ROWS * GRID_W), lambda bi, hp: (0, hp, 0, 0))],
        out_specs=qkv_spec,
        out_shape=jax.ShapeDtypeStruct((b, s, NA_WIDTH), bf16),
        compiler_params=_params("parallel", "parallel"),
        name="neighbourhood_attention",
    )(q, k, v, bias_tab)


def _ft_tables(seq):
    assert seq == FT_N1 * FT_N2
    n_blk = FT_N2 // SUBLANES
    k1 = np.arange(FT_N1)[:, None, None, None]
    sr = np.arange(SUBLANES)[None, :, None, None]
    n1 = np.arange(FT_N1)[None, None, :, None]
    sc = np.arange(SUBLANES)[None, None, None, :]
    stage1 = np.zeros((n_blk, 2, FT_N1, SUBLANES, FT_N1, SUBLANES), np.float64)
    for blk in range(n_blk):
        n = FT_N2 * n1 + SUBLANES * blk + sr
        ang = 2.0 * np.pi * ((k1 * n) % seq) / seq
        eye = (sr == sc)
        stage1[blk, 0] = np.cos(ang) * eye
        stage1[blk, 1] = -np.sin(ang) * eye
    stage1 = stage1.reshape(n_blk, 2 * FT_N1 * SUBLANES, FT_N1 * SUBLANES)
    a = np.arange(FT_N2)
    ang2 = 2.0 * np.pi * ((a[:, None] * a[None, :]) % FT_N2) / FT_N2
    c2, s2 = np.cos(ang2), np.sin(ang2)
    stage2 = np.block([[c2, s2], [-s2, c2]])
    g = np.arange(FT_GROUP_DIM)
    angc = 2.0 * np.pi * ((g[:, None] * g[None, :]) % FT_GROUP_DIM) / FT_GROUP_DIM
    norm = 1.0 / np.sqrt(seq * FT_GROUP_DIM)
    chan = np.concatenate([np.cos(angc), np.sin(angc)], axis=0) * norm
    return (jnp.asarray(stage1, bf16), jnp.asarray(stage2, bf16), jnp.asarray(chan, bf16))


def _ft_stage1_kernel(u_ref, m_ref, z_ref):
    rows = FT_N1 * SUBLANES
    u = u_ref[0].reshape(rows, FT_WIDTH).astype(bf16)
    z = jnp.dot(m_ref[0], u, preferred_element_type=f32)
    z_ref[0] = z.reshape(2, FT_N1, SUBLANES, FT_WIDTH)


def _ft_stage2_kernel(z_ref, s2_ref, cs_ref, y_ref):
    gd = FT_GROUP_DIM
    for kk in range(FT_K1_BLOCK):
        zz = jnp.concatenate([z_ref[0, 0, kk], z_ref[0, 1, kk]], axis=0).astype(bf16)
        x = jnp.dot(s2_ref[...], zz, preferred_element_type=f32)
        outs = []
        for g in range(FT_GROUPS):
            xg = jnp.concatenate([x[:FT_N2, g * gd:(g + 1) * gd], x[FT_N2:, g * gd:(g + 1) * gd]], axis=1)
            outs.append(jnp.dot(xg.astype(bf16), cs_ref[...], preferred_element_type=f32))
        y_ref[0, kk] = jnp.concatenate(outs, axis=1).astype(y_ref.dtype)


def _fourier_mix(u, tables):
    b, s, c = u.shape
    stage1, stage2, chan = tables
    n_blk = FT_N2 // SUBLANES
    z = pl.pallas_call(
        _ft_stage1_kernel,
        grid=(n_blk, b),
        in_specs=[pl.BlockSpec((1, FT_N1, SUBLANES, c), lambda j, bi: (bi, 0, j, 0)),
                  pl.BlockSpec((1,) + stage1.shape[1:], lambda j, bi: (j, 0, 0))],
        out_specs=pl.BlockSpec((1, 2, FT_N1, SUBLANES, c), lambda j, bi: (bi, 0, 0, j, 0)),
        out_shape=jax.ShapeDtypeStruct((b, 2, FT_N1, FT_N2, c), f32),
        compiler_params=_params("parallel", "parallel"),
        name="fourier_stage1",
    )(u.reshape(b, FT_N1, FT_N2, c), stage1)
    y = pl.pallas_call(
        _ft_stage2_kernel,
        grid=(b, FT_N1 // FT_K1_BLOCK),
        in_specs=[pl.BlockSpec((1, 2, FT_K1_BLOCK, FT_N2, c), lambda bi, kb: (bi, 0, kb, 0, 0)),
                  pl.BlockSpec(stage2.shape, lambda bi, kb: (0, 0)),
                  pl.BlockSpec(chan.shape, lambda bi, kb: (0, 0))],
        out_specs=pl.BlockSpec((1, FT_K1_BLOCK, FT_N2, c), lambda bi, kb: (bi, kb, 0, 0)),
        out_shape=jax.ShapeDtypeStruct((b, FT_N1, FT_N2, c), bf16),
        compiler_params=_params("parallel", "parallel"),
        name="fourier_stage2",
    )(z, stage2, chan)
    return y.transpose(0, 2, 1, 3).reshape(b, s, c)


def _mem_kv_kernel(mem_ref, g_ref, w_ref, k_ref, v_ref):
    mn = _rms_scale(mem_ref[0]) * g_ref[...]
    kv = jnp.dot(mn.astype(bf16), w_ref[...], preferred_element_type=f32)
    k_ref[0] = kv[:, :MEM_WIDTH].astype(bf16)
    v_ref[0] = kv[:, MEM_WIDTH:].astype(bf16)


def _mem_kv(mem, g_mem, w_kv_bf16):
    b, m, d = mem.shape
    kv_spec = pl.BlockSpec((1, m, MEM_WIDTH), lambda bi: (bi, 0, 0))
    return pl.pallas_call(
        _mem_kv_kernel,
        grid=(b,),
        in_specs=[pl.BlockSpec((1, m, d), lambda bi: (bi, 0, 0)), pl.BlockSpec((1, d), lambda bi: (0, 0)),
                  pl.BlockSpec(w_kv_bf16.shape, lambda bi: (0, 0))],
        out_specs=[kv_spec, kv_spec],
        out_shape=[jax.ShapeDtypeStruct((b, m, MEM_WIDTH), bf16)] * 2,
        compiler_params=_params("parallel"),
        name="mem_kv",
    )(mem, g_mem.reshape(1, d), w_kv_bf16)


def _mix_out_kernel(x_ref, yna_ref, yft_ref, qm_ref, km_ref, vm_ref, ggrp_ref, wout_ref, gffn_ref, rw_ref,
                    rb_ref, x1_ref, h2p_ref, eidx_ref, gate_ref, rank_ref, cnt_ref, carry_ref):
    tm = x_ref.shape[0]

    @pl.when(pl.program_id(0) == 0)
    def _():
        carry_ref[...] = jnp.zeros_like(carry_ref)

    q = qm_ref[...]
    km = km_ref[0]
    vm = vm_ref[0]
    heads = []
    for h in range(MEM_HEADS):
        sl = slice(h * MEM_HEAD_DIM, (h + 1) * MEM_HEAD_DIM)
        s = lax.dot_general(q[:, sl], km[:, sl], (((1,), (1,)), ((), ())), preferred_element_type=f32)
        p = _softmax_rows(s * (MEM_HEAD_DIM ** -0.5))
        heads.append(jnp.dot(p.astype(bf16), vm[:, sl], preferred_element_type=f32))
    ymem = jnp.concatenate(heads, axis=1)

    g = ggrp_ref[...]
    a, c = NA_WIDTH, NA_WIDTH + FT_WIDTH
    y = jnp.concatenate([_rms_scale(yna_ref[...].astype(f32)) * g[:, :a],
                         _rms_scale(yft_ref[...].astype(f32)) * g[:, a:c],
                         _rms_scale(ymem) * g[:, c:]], axis=1)
    x1 = x_ref[...] + jnp.dot(y.astype(bf16), wout_ref[...], preferred_element_type=f32)
    x1_ref[...] = x1
    h2 = _rms_scale(x1) * gffn_ref[...]
    h2p_ref[...] = _pack_bf16_pairs(h2)

    logits = jnp.dot(h2, rw_ref[...], preferred_element_type=f32, precision=lax.Precision.HIGHEST) + rb_ref[...]
    lane = lax.broadcasted_iota(i32, (tm, N_EXPERTS), 1).astype(f32)
    vals, idxs, sels = [], [], []
    l = logits
    for _ in range(TOP_K):
        m = jnp.max(l, axis=1, keepdims=True)
        idx = jnp.min(jnp.where(l == m, lane, float(N_EXPERTS)), axis=1, keepdims=True)
        sel = lane == idx
        vals.append(m)
        idxs.append(idx)
        sels.append(sel)
        l = jnp.where(sel, -jnp.inf, l)
    ex = [jnp.exp(v - vals[0]) for v in vals]
    den = ex[0] + ex[1] + ex[2] + ex[3]

    onehot = (sels[0] | sels[1] | sels[2] | sels[3]).astype(f32)
    lower = (lax.broadcasted_iota(i32, (tm, tm), 0) > lax.broadcasted_iota(i32, (tm, tm), 1)).astype(bf16)
    before = jnp.dot(lower, onehot.astype(bf16), preferred_element_type=f32) + carry_ref[...]
    ranks = [jnp.sum(jnp.where(sel, before, 0.0), axis=1, keepdims=True) for sel in sels]
    carry_ref[...] = carry_ref[...] + jnp.sum(onehot, axis=0, keepdims=True)
    cnt_ref[...] = carry_ref[...].astype(i32)

    k_lane = lax.broadcasted_iota(i32, (tm, TOP_K), 1)

    def columns(cols):
        out = jnp.broadcast_to(cols[TOP_K - 1], (tm, TOP_K))
        for k in range(TOP_K - 2, -1, -1):
            out = jnp.where(k_lane == k, cols[k], out)
        return out

    eidx_ref[...] = columns(idxs).astype(i32)
    gate_ref[...] = columns([e / den for e in ex])
    rank_ref[...] = columns(ranks).astype(i32)


def _mix_out(x2d, y_na, y_ft, q_mem, k_mem, v_mem, g_grp, w_out_bf16, g_ffn, router_w, router_b, seq):
    t, d = x2d.shape
    tm = TOKEN_TILE
    steps_per_batch = seq // tm
    m = k_mem.shape[1]
    row = lambda w: pl.BlockSpec((tm, w), lambda i: (i, 0))
    full = lambda a: pl.BlockSpec(a.shape, lambda i: (0,) * a.ndim)
    kv_spec = pl.BlockSpec((1, m, MEM_WIDTH), lambda i: (i // steps_per_batch, 0, 0))
    g_grp2, g_ffn2, rb2 = g_grp.reshape(1, -1), g_ffn.reshape(1, d), router_b.reshape(1, N_EXPERTS)
    return pl.pallas_call(
        _mix_out_kernel,
        grid=(t // tm,),
        in_specs=[row(d), row(NA_WIDTH), row(FT_WIDTH), row(MEM_WIDTH), kv_spec, kv_spec,
                  full(g_grp2), full(w_out_bf16), full(g_ffn2), full(router_w), full(rb2)],
        out_specs=[row(d), row(d // 2), row(TOP_K), row(TOP_K), row(TOP_K),
                   pl.BlockSpec((1, N_EXPERTS), lambda i: (0, 0))],
        out_shape=[jax.ShapeDtypeStruct((t, d), f32), jax.ShapeDtypeStruct((t, d // 2), u32),
                   jax.ShapeDtypeStruct((t, TOP_K), i32), jax.ShapeDtypeStruct((t, TOP_K), f32),
                   jax.ShapeDtypeStruct((t, TOP_K), i32), jax.ShapeDtypeStruct((1, N_EXPERTS), i32)],
        scratch_shapes=[pltpu.VMEM((1, N_EXPERTS), f32)],
        compiler_params=_params("arbitrary"),
        name="mix_out_router",
    )(x2d, y_na, y_ft, q_mem, k_mem, v_mem, g_grp2, w_out_bf16, g_ffn2, router_w, rb2)


def _row_copy(src, src_row, dst, dst_row, sem):
    return pltpu.make_async_copy(src.at[pl.ds(src_row, 1)], dst.at[pl.ds(dst_row, 1)], sem)


def _dispatch_kernel(dest_ref, h_ref, xs_hbm, sem):
    tm = h_ref.shape[0]

    def copies(i):
        return [_row_copy(h_ref, i, xs_hbm, dest_ref[0, 0, TOP_K * i + j], sem) for j in range(TOP_K)]

    def start(i, carry):
        for cp in copies(i):
            cp.start()
        return carry

    def wait(i, carry):
        for cp in copies(i):
            cp.wait()
        return carry

    lax.fori_loop(0, tm, start, 0)
    lax.fori_loop(0, tm, wait, 0)


def _dispatch(h2p, dest, n_slots):
    t, w = h2p.shape
    tm = TOKEN_TILE
    return pl.pallas_call(
        _dispatch_kernel,
        grid=(t // tm,),
        in_specs=[pl.BlockSpec((1, 1, TOP_K * tm), lambda i: (i, 0, 0), memory_space=pltpu.SMEM),
                  pl.BlockSpec((tm, w), lambda i: (i, 0))],
        out_specs=pl.BlockSpec(memory_space=pl.ANY),
        out_shape=jax.ShapeDtypeStruct((n_slots, w), u32),
        scratch_shapes=[pltpu.SemaphoreType.DMA(())],
        compiler_params=_params("arbitrary"),
        name="moe_dispatch",
    )(dest.reshape(t // tm, 1, TOP_K * tm), h2p)


def _expert_kernel(blk_e_ref, blk_cnt_ref, xs_ref, wgu_ref, bgu_ref, wd_ref, bd_ref, y_ref):
    b = pl.program_id(0)
    cnt = blk_cnt_ref[b]
    bm = xs_ref.shape[0]
    de = wd_ref.shape[1]

    @pl.when(cnt > 0)
    def _():
        valid = lax.broadcasted_iota(i32, (bm, 1), 0) < cnt
        x = jnp.where(valid, _unpack_bf16_pairs(xs_ref[...]), 0.0).astype(bf16)
        gu = jnp.dot(x, wgu_ref[0], preferred_element_type=f32) + bgu_ref[0]
        x_glu = jnp.minimum(gu[:, :de], SWIGLU_LIMIT)
        x_lin = jnp.clip(gu[:, de:], -SWIGLU_LIMIT, SWIGLU_LIMIT)
        act = x_glu * (1.0 / (1.0 + jnp.exp(-SWIGLU_ALPHA * x_glu))) * (x_lin + 1.0)
        y = jnp.dot(act.astype(bf16), wd_ref[0], preferred_element_type=f32) + bd_ref[0]
        y_ref[...] = _pack_bf16_pairs(y)

    @pl.when(cnt == 0)
    def _():
        y_ref[...] = jnp.zeros_like(y_ref)


def _experts(xs, blk_e, blk_cnt, w_gu_bf16, b_gu, w_down_bf16, b_down):
    n_slots, w = xs.shape
    bm = MOE_BLOCK
    e, d, de2 = w_gu_bf16.shape
    de = w_down_bf16.shape[1]
    grid_spec = pltpu.PrefetchScalarGridSpec(
        num_scalar_prefetch=2,
        grid=(n_slots // bm,),
        in_specs=[pl.BlockSpec((bm, w), lambda b, be, bc: (b, 0)),
                  pl.BlockSpec((1, d, de2), lambda b, be, bc: (be[b], 0, 0)),
                  pl.BlockSpec((1, 1, de2), lambda b, be, bc: (be[b], 0, 0)),
                  pl.BlockSpec((1, de, d), lambda b, be, bc: (be[b], 0, 0)),
                  pl.BlockSpec((1, 1, d), lambda b, be, bc: (be[b], 0, 0))],
        out_specs=pl.BlockSpec((bm, w), lambda b, be, bc: (b, 0)),
    )
    return pl.pallas_call(
        _expert_kernel,
        grid_spec=grid_spec,
        out_shape=jax.ShapeDtypeStruct((n_slots, w), u32),
        compiler_params=_params("arbitrary"),
        name="moe_experts",
    )(blk_e, blk_cnt, xs, w_gu_bf16, b_gu.reshape(e, 1, de2), w_down_bf16, b_down.reshape(e, 1, d))


def _combine_kernel(dest_ref, x1_ref, gate_ref, gfin_ref, y_hbm, o_ref, buf, sem):
    tm = x1_ref.shape[0]

    def copies(i):
        return [_row_copy(y_hbm, dest_ref[0, 0, TOP_K * i + j], buf.at[j], i, sem) for j in range(TOP_K)]

    def start(i, carry):
        for cp in copies(i):
            cp.start()
        return carry

    def wait(i, carry):
        for cp in copies(i):
            cp.wait()
        return carry

    lax.fori_loop(0, tm, start, 0)
    lax.fori_loop(0, tm, wait, 0)
    acc = x1_ref[...]
    gates = gate_ref[...]
    for j in range(TOP_K):
        acc = acc + gates[:, j:j + 1] * _unpack_bf16_pairs(buf[j])
    o_ref[...] = _rms_scale(acc) * gfin_ref[...]


def _combine(x1, gates, dest, y_slots, g_final):
    t, d = x1.shape
    tm = TOKEN_TILE
    w = y_slots.shape[1]
    return pl.pallas_call(
        _combine_kernel,
        grid=(t // tm,),
        in_specs=[pl.BlockSpec((1, 1, TOP_K * tm), lambda i: (i, 0, 0), memory_space=pltpu.SMEM),
                  pl.BlockSpec((tm, d), lambda i: (i, 0)),
                  pl.BlockSpec((tm, TOP_K), lambda i: (i, 0)),
                  pl.BlockSpec((1, d), lambda i: (0, 0)),
                  pl.BlockSpec(memory_space=pl.ANY)],
        out_specs=pl.BlockSpec((tm, d), lambda i: (i, 0)),
        out_shape=jax.ShapeDtypeStruct((t, d), f32),
        scratch_shapes=[pltpu.VMEM((TOP_K, tm, w), u32), pltpu.SemaphoreType.DMA(())],
        compiler_params=_params("arbitrary"),
        name="moe_combine",
    )(dest.reshape(t // tm, 1, TOP_K * tm), x1, gates, g_final.reshape(1, d), y_slots)


def _slot_layout(counts, eidx, rank, n_blocks):
    bm = MOE_BLOCK
    padded = (counts + bm - 1) // bm * bm
    padded_end = jnp.cumsum(padded)
    start = padded_end - padded
    dest = (start[eidx] + rank).reshape(-1).astype(i32)
    blk_row = jnp.arange(n_blocks, dtype=i32) * bm
    blk_e = jnp.minimum(jnp.searchsorted(padded_end, blk_row, side="right"), N_EXPERTS - 1).astype(i32)
    blk_cnt = jnp.clip(counts[blk_e] - (blk_row - start[blk_e]), 0, bm).astype(i32)
    return dest, blk_e, blk_cnt


def _layer(x2d, mem, seq, g_mix, g_mem, w_in, w_mem_kv, na_rel_bias, g_grp, w_out, g_ffn,
           router_w, router_b, w_gu, b_gu, w_down, b_down):
    t, d = x2d.shape
    b = t // seq
    q_na, k_na, v_na, u_ft, q_mem = _in_proj(x2d, g_mix, w_in.astype(bf16))
    k_mem, v_mem = _mem_kv(mem, g_mem, w_mem_kv.astype(bf16))
    shape3 = lambda a: a.reshape(b, seq, a.shape[-1])
    y_na = _neighbourhood_attention(shape3(q_na), shape3(k_na), shape3(v_na), _na_bias_table(na_rel_bias))
    y_ft = _fourier_mix(shape3(u_ft), _ft_tables(seq))
    x1, h2p, eidx, gates, rank, counts = _mix_out(
        x2d, y_na.reshape(t, -1), y_ft.reshape(t, -1), q_mem, k_mem, v_mem, g_grp, w_out.astype(bf16),
        g_ffn, router_w, router_b, seq)
    n_blocks = (t * TOP_K) // MOE_BLOCK + N_EXPERTS
    dest, blk_e, blk_cnt = _slot_layout(counts[0], eidx, rank, n_blocks)
    xs = _dispatch(h2p, dest, n_blocks * MOE_BLOCK)
    y_slots = _experts(xs, blk_e, blk_cnt, w_gu.astype(bf16), b_gu, w_down.astype(bf16), b_down)
    return x1, gates, dest, y_slots


def kernel(x, mem, g_mix, g_mem, w_in, w_mem_kv, na_rel_bias, g_grp, w_out, g_ffn, router_w, router_b,
           w_gu, b_gu, w_down, b_down, g_final):
    b, seq, d = x.shape
    depth = w_in.shape[0]
    assert depth == 1, "the fused final norm assumes a single layer"
    x2d = x.reshape(b * seq, d)
    x1, gates, dest, y_slots = _layer(
        x2d, mem, seq, g_mix[0], g_mem[0], w_in[0], w_mem_kv[0], na_rel_bias[0], g_grp[0], w_out[0], g_ffn[0],
        router_w[0], router_b[0], w_gu[0], b_gu[0], w_down[0], b_down[0])
    out = _combine(x1, gates, dest, y_slots, g_final)
    return out.reshape(b, seq, d)
```

```python
import functools

import numpy as np
import jax
import jax.numpy as jnp
from jax import lax
from jax.experimental import pallas as pl
from jax.experimental.pallas import tpu as pltpu

f32 = jnp.float32
bf16 = jnp.bfloat16
u32 = jnp.uint32
i32 = jnp.int32

GRID_W = 64
NA_HEADS = 8
NA_HEAD_DIM = 64
NA_WIN_ROWS = 8
NA_WIN_COLS = 16
FT_GROUPS = 4
FT_GROUP_DIM = 128
MEM_HEADS = 4
MEM_HEAD_DIM = 128
NA_WIDTH = NA_HEADS * NA_HEAD_DIM
FT_WIDTH = FT_GROUPS * FT_GROUP_DIM
MEM_WIDTH = MEM_HEADS * MEM_HEAD_DIM
N_EXPERTS = 32
TOP_K = 4
SWIGLU_LIMIT = 7.0
SWIGLU_ALPHA = 1.702
EPS = 1e-6

LANES = 128
SUBLANES = 8
VMEM_LIMIT_BYTES = 56 * 1024 * 1024

TOKEN_TILE = 512
MOE_BLOCK = 512
NA_ROW_UNROLL = 8
FT_N1 = 64
FT_N2 = 128
FT_K1_BLOCK = 8
MASK_VALUE = -1e30


def _params(*semantics):
    return pltpu.CompilerParams(dimension_semantics=semantics, vmem_limit_bytes=VMEM_LIMIT_BYTES)


def _rms_scale(x):
    return x * lax.rsqrt(jnp.mean(x * x, axis=-1, keepdims=True) + EPS)


def _softmax_rows(s):
    p = jnp.exp(s - jnp.max(s, axis=-1, keepdims=True))
    return p / jnp.sum(p, axis=-1, keepdims=True)


def _pack_bf16_pairs(x):
    n = x.shape[1] // 2
    bits = pltpu.bitcast(x.astype(bf16).astype(f32), u32)
    return (bits[:, :n] >> 16) | (bits[:, n:] & jnp.uint32(0xFFFF0000))


def _unpack_bf16_pairs(w):
    lo = pltpu.bitcast(w << 16, f32)
    hi = pltpu.bitcast(w & jnp.uint32(0xFFFF0000), f32)
    return jnp.concatenate([lo, hi], axis=1)


def _in_proj_kernel(x_ref, g_ref, w_ref, qna_ref, kna_ref, vna_ref, uft_ref, qmem_ref):
    h = _rms_scale(x_ref[...]) * g_ref[...]
    proj = jnp.dot(h.astype(bf16), w_ref[...], preferred_element_type=f32)
    o = NA_WIDTH
    qna_ref[...] = (proj[:, :o] * (NA_HEAD_DIM ** -0.5)).astype(bf16)
    kna_ref[...] = proj[:, o:2 * o].astype(bf16)
    vna_ref[...] = proj[:, 2 * o:3 * o].astype(bf16)
    uft_ref[...] = proj[:, 3 * o:3 * o + FT_WIDTH]
    qmem_ref[...] = proj[:, 3 * o + FT_WIDTH:].astype(bf16)


def _in_proj(x2d, g_mix, w_in_bf16):
    t, d = x2d.shape
    tm = TOKEN_TILE
    row = lambda w: pl.BlockSpec((tm, w), lambda i: (i, 0))
    return pl.pallas_call(
        _in_proj_kernel,
        grid=(t // tm,),
        in_specs=[row(d), pl.BlockSpec((1, d), lambda i: (0, 0)),
                  pl.BlockSpec(w_in_bf16.shape, lambda i: (0, 0))],
        out_specs=[row(NA_WIDTH), row(NA_WIDTH), row(NA_WIDTH), row(FT_WIDTH), row(MEM_WIDTH)],
        out_shape=[jax.ShapeDtypeStruct((t, NA_WIDTH), bf16)] * 3
        + [jax.ShapeDtypeStruct((t, FT_WIDTH), f32), jax.ShapeDtypeStruct((t, MEM_WIDTH), bf16)],
        compiler_params=_params("parallel"),
        name="in_proj",
    )(x2d, g_mix.reshape(1, d), w_in_bf16)


def _na_bias_table(rel_bias):
    s = np.arange(NA_WIN_ROWS)[:, None]
    j = np.arange(NA_WIN_ROWS)[None, :]
    dr_idx = (j - s) + (NA_WIN_ROWS - 1)
    c = np.arange(GRID_W)
    dc_idx = np.clip(c[None, :] - c[:, None], -(NA_WIN_COLS - 1), NA_WIN_COLS - 1) + (NA_WIN_COLS - 1)
    col_start = np.clip(c - NA_WIN_COLS // 2, 0, GRID_W - NA_WIN_COLS)
    col_in = (c[None, :] >= col_start[:, None]) & (c[None, :] < col_start[:, None] + NA_WIN_COLS)
    pick_r = jnp.asarray(dr_idx[:, :, None] == np.arange(2 * NA_WIN_ROWS - 1), f32)
    pick_c = jnp.asarray(dc_idx[:, :, None] == np.arange(2 * NA_WIN_COLS - 1), f32)
    tab = jnp.einsum("hab,sja,qcb->hsqjc", rel_bias.astype(f32), pick_r, pick_c, precision=lax.Precision.HIGHEST)
    tab = jnp.where(col_in[None, None, :, None, :], tab, MASK_VALUE)
    tab = tab.reshape(NA_HEADS // 2, 2, NA_WIN_ROWS, GRID_W, NA_WIN_ROWS * GRID_W)
    return tab.transpose(2, 0, 1, 3, 4).reshape(NA_WIN_ROWS, NA_HEADS // 2, 2 * GRID_W, NA_WIN_ROWS * GRID_W)


def _na_kernel(q_ref, k_ref, v_ref, bias_ref, o_ref):
    rows = q_ref.shape[1] // GRID_W
    win = NA_WIN_ROWS * GRID_W
    first_head = lax.broadcasted_iota(i32, (GRID_W, 2 * NA_HEAD_DIM), 1) < NA_HEAD_DIM

    def body(it, carry):
        scores, values, q_offsets = [], [], []
        for u in range(NA_ROW_UNROLL):
            r = it * NA_ROW_UNROLL + u
            row_start = jnp.clip(r - NA_WIN_ROWS // 2, 0, rows - NA_WIN_ROWS)
            q0 = pl.multiple_of(r * GRID_W, GRID_W)
            k0 = pl.multiple_of(row_start * GRID_W, GRID_W)
            q = q_ref[0, pl.ds(q0, GRID_W), :]
            zero = jnp.zeros_like(q)
            qm = jnp.concatenate([jnp.where(first_head, q, zero), jnp.where(first_head, zero, q)], axis=0)
            s = lax.dot_general(qm, k_ref[0, pl.ds(k0, win), :], (((1,), (1,)), ((), ())),
                                preferred_element_type=f32)
            scores.append(s + bias_ref[r - row_start, 0])
            values.append(v_ref[0, pl.ds(k0, win), :])
            q_offsets.append(q0)
        s = jnp.concatenate(scores, axis=0)
        p = jnp.exp(s - jnp.max(s, axis=-1, keepdims=True))
        inv_den = 1.0 / jnp.sum(p, axis=-1, keepdims=True)
        p = p.astype(bf16)
        for u in range(NA_ROW_UNROLL):
            sl = slice(u * 2 * GRID_W, (u + 1) * 2 * GRID_W)
            o = jnp.dot(p[sl], values[u], preferred_element_type=f32) * inv_den[sl]
            o_ref[0, pl.ds(q_offsets[u], GRID_W), :] = jnp.where(
                first_head, o[:GRID_W], o[GRID_W:]).astype(o_ref.dtype)
        return carry

    lax.fori_loop(0, rows // NA_ROW_UNROLL, body, 0)


def _neighbourhood_attention(q, k, v, bias_tab):
    b, s, _ = q.shape
    pair = 2 * NA_HEAD_DIM
    qkv_spec = pl.BlockSpec((1, s, pair), lambda bi, hp: (bi, 0, hp))
    return pl.pallas_call(
        _na_kernel,
        grid=(b, NA_HEADS // 2),
        in_specs=[qkv_spec, qkv_spec, qkv_spec,
                  pl.BlockSpec((NA_WIN_ROWS, 1, 2 * GRID_W, NA_WIN_ROWS * GRID_W), lambda bi, hp: (0, hp, 0, 0))],
        out_specs=qkv_spec,
        out_shape=jax.ShapeDtypeStruct((b, s, NA_WIDTH), bf16),
        compiler_params=_params("parallel", "parallel"),
        name="neighbourhood_attention",
    )(q, k, v, bias_tab)


def _ft_tables(seq):
    assert seq == FT_N1 * FT_N2
    n_blk = FT_N2 // SUBLANES
    k1 = np.arange(FT_N1)[:, None, None, None]
    sr = np.arange(SUBLANES)[None, :, None, None]
    n1 = np.arange(FT_N1)[None, None, :, None]
    sc = np.arange(SUBLANES)[None, None, None, :]
    stage1 = np.zeros((n_blk, 2, FT_N1, SUBLANES, FT_N1, SUBLANES), np.float64)
    for blk in range(n_blk):
        n = FT_N2 * n1 + SUBLANES * blk + sr
        ang = 2.0 * np.pi * ((k1 * n) % seq) / seq
        eye = (sr == sc)
        stage1[blk, 0] = np.cos(ang) * eye
        stage1[blk, 1] = -np.sin(ang) * eye
    stage1 = stage1.reshape(n_blk, 2 * FT_N1 * SUBLANES, FT_N1 * SUBLANES)
    a = np.arange(FT_N2)
    ang2 = 2.0 * np.pi * ((a[:, None] * a[None, :]) % FT_N2) / FT_N2
    c2, s2 = np.cos(ang2), np.sin(ang2)
    stage2 = np.block([[c2, s2], [-s2, c2]])
    g = np.arange(FT_GROUP_DIM)
    angc = 2.0 * np.pi * ((g[:, None] * g[None, :]) % FT_GROUP_DIM) / FT_GROUP_DIM
    norm = 1.0 / np.sqrt(seq * FT_GROUP_DIM)
    chan = np.concatenate([np.cos(angc), np.sin(angc)], axis=0) * norm
    return (jnp.asarray(stage1, bf16), jnp.asarray(stage2, bf16), jnp.asarray(chan, bf16))


def _ft_stage1_kernel(u_ref, m_ref, z_ref):
    rows = FT_N1 * SUBLANES
    u = u_ref[0].reshape(rows, FT_WIDTH).astype(bf16)
    z = jnp.dot(m_ref[0], u, preferred_element_type=f32)
    z_ref[0] = z.reshape(2, FT_N1, SUBLANES, FT_WIDTH)


def _ft_stage2_kernel(z_ref, s2_ref, cs_ref, y_ref):
    gd = FT_GROUP_DIM
    for kk in range(FT_K1_BLOCK):
        zz = jnp.concatenate([z_ref[0, 0, kk], z_ref[0, 1, kk]], axis=0).astype(bf16)
        x = jnp.dot(s2_ref[...], zz, preferred_element_type=f32)
        outs = []
        for g in range(FT_GROUPS):
            xg = jnp.concatenate([x[:FT_N2, g * gd:(g + 1) * gd], x[FT_N2:, g * gd:(g + 1) * gd]], axis=1)
            outs.append(jnp.dot(xg.astype(bf16), cs_ref[...], preferred_element_type=f32))
        y_ref[0, kk] = jnp.concatenate(outs, axis=1).astype(y_ref.dtype)


def _fourier_mix(u, tables):
    b, s, c = u.shape
    stage1, stage2, chan = tables
    n_blk = FT_N2 // SUBLANES
    z = pl.pallas_call(
        _ft_stage1_kernel,
        grid=(n_blk, b),
        in_specs=[pl.BlockSpec((1, FT_N1, SUBLANES, c), lambda j, bi: (bi, 0, j, 0)),
                  pl.BlockSpec((1,) + stage1.shape[1:], lambda j, bi: (j, 0, 0))],
        out_specs=pl.BlockSpec((1, 2, FT_N1, SUBLANES, c), lambda j, bi: (bi, 0, 0, j, 0)),
        out_shape=jax.ShapeDtypeStruct((b, 2, FT_N1, FT_N2, c), f32),
        compiler_params=_params("parallel", "parallel"),
        name="fourier_stage1",
    )(u.reshape(b, FT_N1, FT_N2, c), stage1)
    y = pl.pallas_call(
        _ft_stage2_kernel,
        grid=(b, FT_N1 // FT_K1_BLOCK),
        in_specs=[pl.BlockSpec((1, 2, FT_K1_BLOCK, FT_N2, c), lambda bi, kb: (bi, 0, kb, 0, 0)),
                  pl.BlockSpec(stage2.shape, lambda bi, kb: (0, 0)),
                  pl.BlockSpec(chan.shape, lambda bi, kb: (0, 0))],
        out_specs=pl.BlockSpec((1, FT_K1_BLOCK, FT_N2, c), lambda bi, kb: (bi, kb, 0, 0)),
        out_shape=jax.ShapeDtypeStruct((b, FT_N1, FT_N2, c), bf16),
        compiler_params=_params("parallel", "parallel"),
        name="fourier_stage2",
    )(z, stage2, chan)
    return y.transpose(0, 2, 1, 3).reshape(b, s, c)


def _mem_kv_kernel(mem_ref, g_ref, w_ref, k_ref, v_ref):
    mn = _rms_scale(mem_ref[0]) * g_ref[...]
    kv = jnp.dot(mn.astype(bf16), w_ref[...], preferred_element_type=f32)
    k_ref[0] = kv[:, :MEM_WIDTH].astype(bf16)
    v_ref[0] = kv[:, MEM_WIDTH:].astype(bf16)


def _mem_kv(mem, g_mem, w_kv_bf16):
    b, m, d = mem.shape
    kv_spec = pl.BlockSpec((1, m, MEM_WIDTH), lambda bi: (bi, 0, 0))
    return pl.pallas_call(
        _mem_kv_kernel,
        grid=(b,),
        in_specs=[pl.BlockSpec((1, m, d), lambda bi: (bi, 0, 0)), pl.BlockSpec((1, d), lambda bi: (0, 0)),
                  pl.BlockSpec(w_kv_bf16.shape, lambda bi: (0, 0))],
        out_specs=[kv_spec, kv_spec],
        out_shape=[jax.ShapeDtypeStruct((b, m, MEM_WIDTH), bf16)] * 2,
        compiler_params=_params("parallel"),
        name="mem_kv",
    )(mem, g_mem.reshape(1, d), w_kv_bf16)


def _mix_out_kernel(x_ref, yna_ref, yft_ref, qm_ref, km_ref, vm_ref, ggrp_ref, wout_ref, gffn_ref, rw_ref,
                    rb_ref, x1_ref, h2p_ref, eidx_ref, gate_ref, rank_ref, cnt_ref, carry_ref):
    tm = x_ref.shape[0]

    @pl.when(pl.program_id(0) == 0)
    def _():
        carry_ref[...] = jnp.zeros_like(carry_ref)

    q = qm_ref[...]
    km = km_ref[0]
    vm = vm_ref[0]
    heads = []
    for h in range(MEM_HEADS):
        sl = slice(h * MEM_HEAD_DIM, (h + 1) * MEM_HEAD_DIM)
        s = lax.dot_general(q[:, sl], km[:, sl], (((1,), (1,)), ((), ())), preferred_element_type=f32)
        p = _softmax_rows(s * (MEM_HEAD_DIM ** -0.5))
        heads.append(jnp.dot(p.astype(bf16), vm[:, sl], preferred_element_type=f32))
    ymem = jnp.concatenate(heads, axis=1)

    g = ggrp_ref[...]
    a, c = NA_WIDTH, NA_WIDTH + FT_WIDTH
    y = jnp.concatenate([_rms_scale(yna_ref[...].astype(f32)) * g[:, :a],
                         _rms_scale(yft_ref[...].astype(f32)) * g[:, a:c],
                         _rms_scale(ymem) * g[:, c:]], axis=1)
    x1 = x_ref[...] + jnp.dot(y.astype(bf16), wout_ref[...], preferred_element_type=f32)
    x1_ref[...] = x1
    h2 = _rms_scale(x1) * gffn_ref[...]
    h2p_ref[...] = _pack_bf16_pairs(h2)

    h_hi = h2.astype(bf16)
    h_lo = (h2 - h_hi.astype(f32)).astype(bf16)
    logits = (jnp.dot(h_hi, rw_ref[0], preferred_element_type=f32)
              + jnp.dot(h_hi, rw_ref[1], preferred_element_type=f32)
              + jnp.dot(h_lo, rw_ref[0], preferred_element_type=f32)) + rb_ref[...]
    lane = lax.broadcasted_iota(i32, (tm, N_EXPERTS), 1).astype(f32)
    vals, idxs, sels = [], [], []
    l = logits
    for _ in range(TOP_K):
        m = jnp.max(l, axis=1, keepdims=True)
        idx = jnp.min(jnp.where(l == m, lane, float(N_EXPERTS)), axis=1, keepdims=True)
        sel = lane == idx
        vals.append(m)
        idxs.append(idx)
        sels.append(sel)
        l = jnp.where(sel, -jnp.inf, l)
    ex = [jnp.exp(v - vals[0]) for v in vals]
    den = ex[0] + ex[1] + ex[2] + ex[3]

    onehot = (sels[0] | sels[1] | sels[2] | sels[3]).astype(f32)
    lower = (lax.broadcasted_iota(i32, (tm, tm), 0) > lax.broadcasted_iota(i32, (tm, tm), 1)).astype(bf16)
    before = jnp.dot(lower, onehot.astype(bf16), preferred_element_type=f32) + carry_ref[...]
    ranks = [jnp.sum(jnp.where(sel, before, 0.0), axis=1, keepdims=True) for sel in sels]
    carry_ref[...] = carry_ref[...] + jnp.sum(onehot, axis=0, keepdims=True)
    cnt_ref[...] = carry_ref[...].astype(i32)

    k_lane = lax.broadcasted_iota(i32, (tm, TOP_K), 1)

    def columns(cols):
        out = jnp.broadcast_to(cols[TOP_K - 1], (tm, TOP_K))
        for k in range(TOP_K - 2, -1, -1):
            out = jnp.where(k_lane == k, cols[k], out)
        return out

    eidx_ref[...] = columns(idxs).astype(i32)
    gate_ref[...] = columns([e / den for e in ex])
    rank_ref[...] = columns(ranks).astype(i32)


def _mix_out(x2d, y_na, y_ft, q_mem, k_mem, v_mem, g_grp, w_out_bf16, g_ffn, router_w, router_b, seq):
    t, d = x2d.shape
    tm = TOKEN_TILE
    steps_per_batch = seq // tm
    m = k_mem.shape[1]
    row = lambda w: pl.BlockSpec((tm, w), lambda i: (i, 0))
    full = lambda a: pl.BlockSpec(a.shape, lambda i: (0,) * a.ndim)
    kv_spec = pl.BlockSpec((1, m, MEM_WIDTH), lambda i: (i // steps_per_batch, 0, 0))
    g_grp2, g_ffn2, rb2 = g_grp.reshape(1, -1), g_ffn.reshape(1, d), router_b.reshape(1, N_EXPERTS)
    rw_hi = router_w.astype(bf16)
    router_w = jnp.stack([rw_hi, (router_w - rw_hi.astype(f32)).astype(bf16)])
    return pl.pallas_call(
        _mix_out_kernel,
        grid=(t // tm,),
        in_specs=[row(d), row(NA_WIDTH), row(FT_WIDTH), row(MEM_WIDTH), kv_spec, kv_spec,
                  full(g_grp2), full(w_out_bf16), full(g_ffn2), full(router_w), full(rb2)],
        out_specs=[row(d), row(d // 2), row(TOP_K), row(TOP_K), row(TOP_K),
                   pl.BlockSpec((1, N_EXPERTS), lambda i: (0, 0))],
        out_shape=[jax.ShapeDtypeStruct((t, d), f32), jax.ShapeDtypeStruct((t, d // 2), u32),
                   jax.ShapeDtypeStruct((t, TOP_K), i32), jax.ShapeDtypeStruct((t, TOP_K), f32),
                   jax.ShapeDtypeStruct((t, TOP_K), i32), jax.ShapeDtypeStruct((1, N_EXPERTS), i32)],
        scratch_shapes=[pltpu.VMEM((1, N_EXPERTS), f32)],
        compiler_params=_params("arbitrary"),
        name="mix_out_router",
    )(x2d, y_na, y_ft, q_mem, k_mem, v_mem, g_grp2, w_out_bf16, g_ffn2, router_w, rb2)


def _row_copy(src, src_row, dst, dst_row, sem):
    return pltpu.make_async_copy(src.at[pl.ds(src_row, 1)], dst.at[pl.ds(dst_row, 1)], sem)


def _dispatch_kernel(dest_ref, h_ref, xs_hbm, sem):
    tm = h_ref.shape[0]

    def copies(i):
        return [_row_copy(h_ref, i, xs_hbm, dest_ref[0, 0, TOP_K * i + j], sem) for j in range(TOP_K)]

    def start(i, carry):
        for cp in copies(i):
            cp.start()
        return carry

    def wait(i, carry):
        for cp in copies(i):
            cp.wait()
        return carry

    lax.fori_loop(0, tm, start, 0)
    lax.fori_loop(0, tm, wait, 0)


def _dispatch(h2p, dest, n_slots):
    t, w = h2p.shape
    tm = TOKEN_TILE
    return pl.pallas_call(
        _dispatch_kernel,
        grid=(t // tm,),
        in_specs=[pl.BlockSpec((1, 1, TOP_K * tm), lambda i: (i, 0, 0), memory_space=pltpu.SMEM),
                  pl.BlockSpec((tm, w), lambda i: (i, 0))],
        out_specs=pl.BlockSpec(memory_space=pl.ANY),
        out_shape=jax.ShapeDtypeStruct((n_slots, w), u32),
        scratch_shapes=[pltpu.SemaphoreType.DMA(())],
        compiler_params=_params("arbitrary"),
        name="moe_dispatch",
    )(dest.reshape(t // tm, 1, TOP_K * tm), h2p)


def _expert_kernel(blk_e_ref, blk_cnt_ref, xs_ref, wgu_ref, bgu_ref, wd_ref, bd_ref, y_ref):
    b = pl.program_id(0)
    cnt = blk_cnt_ref[b]
    bm = xs_ref.shape[0]
    de = wd_ref.shape[1]

    @pl.when(cnt > 0)
    def _():
        valid = lax.broadcasted_iota(i32, (bm, 1), 0) < cnt
        x = jnp.where(valid, _unpack_bf16_pairs(xs_ref[...]), 0.0).astype(bf16)
        gu = jnp.dot(x, wgu_ref[0], preferred_element_type=f32) + bgu_ref[0]
        x_glu = jnp.minimum(gu[:, :de], SWIGLU_LIMIT)
        x_lin = jnp.clip(gu[:, de:], -SWIGLU_LIMIT, SWIGLU_LIMIT)
        act = x_glu * (1.0 / (1.0 + jnp.exp(-SWIGLU_ALPHA * x_glu))) * (x_lin + 1.0)
        y = jnp.dot(act.astype(bf16), wd_ref[0], preferred_element_type=f32) + bd_ref[0]
        y_ref[...] = _pack_bf16_pairs(y)

    @pl.when(cnt == 0)
    def _():
        y_ref[...] = jnp.zeros_like(y_ref)


def _experts(xs, blk_e, blk_cnt, w_gu_bf16, b_gu, w_down_bf16, b_down):
    n_slots, w = xs.shape
    bm = MOE_BLOCK
    e, d, de2 = w_gu_bf16.shape
    de = w_down_bf16.shape[1]
    grid_spec = pltpu.PrefetchScalarGridSpec(
        num_scalar_prefetch=2,
        grid=(n_slots // bm,),
        in_specs=[pl.BlockSpec((bm, w), lambda b, be, bc: (b, 0)),
                  pl.BlockSpec((1, d, de2), lambda b, be, bc: (be[b], 0, 0)),
                  pl.BlockSpec((1, 1, de2), lambda b, be, bc: (be[b], 0, 0)),
                  pl.BlockSpec((1, de, d), lambda b, be, bc: (be[b], 0, 0)),
                  pl.BlockSpec((1, 1, d), lambda b, be, bc: (be[b], 0, 0))],
        out_specs=pl.BlockSpec((bm, w), lambda b, be, bc: (b, 0)),
    )
    return pl.pallas_call(
        _expert_kernel,
        grid_spec=grid_spec,
        out_shape=jax.ShapeDtypeStruct((n_slots, w), u32),
        compiler_params=_params("arbitrary"),
        name="moe_experts",
    )(blk_e, blk_cnt, xs, w_gu_bf16, b_gu.reshape(e, 1, de2), w_down_bf16, b_down.reshape(e, 1, d))


def _combine_kernel(dest_ref, x1_ref, gate_ref, gfin_ref, y_hbm, o_ref, buf, sem):
    tm = x1_ref.shape[0]

    def copies(i):
        return [_row_copy(y_hbm, dest_ref[0, 0, TOP_K * i + j], buf.at[j], i, sem) for j in range(TOP_K)]

    def start(i, carry):
        for cp in copies(i):
            cp.start()
        return carry

    def wait(i, carry):
        for cp in copies(i):
            cp.wait()
        return carry

    lax.fori_loop(0, tm, start, 0)
    lax.fori_loop(0, tm, wait, 0)
    acc = x1_ref[...]
    gates = gate_ref[...]
    for j in range(TOP_K):
        acc = acc + gates[:, j:j + 1] * _unpack_bf16_pairs(buf[j])
    o_ref[...] = _rms_scale(acc) * gfin_ref[...]


def _combine(x1, gates, dest, y_slots, g_final):
    t, d = x1.shape
    tm = TOKEN_TILE
    w = y_slots.shape[1]
    return pl.pallas_call(
        _combine_kernel,
        grid=(t // tm,),
        in_specs=[pl.BlockSpec((1, 1, TOP_K * tm), lambda i: (i, 0, 0), memory_space=pltpu.SMEM),
                  pl.BlockSpec((tm, d), lambda i: (i, 0)),
                  pl.BlockSpec((tm, TOP_K), lambda i: (i, 0)),
                  pl.BlockSpec((1, d), lambda i: (0, 0)),
                  pl.BlockSpec(memory_space=pl.ANY)],
        out_specs=pl.BlockSpec((tm, d), lambda i: (i, 0)),
        out_shape=jax.ShapeDtypeStruct((t, d), f32),
        scratch_shapes=[pltpu.VMEM((TOP_K, tm, w), u32), pltpu.SemaphoreType.DMA(())],
        compiler_params=_params("arbitrary"),
        name="moe_combine",
    )(dest.reshape(t // tm, 1, TOP_K * tm), x1, gates, g_final.reshape(1, d), y_slots)


def _slot_layout(counts, eidx, rank, n_blocks):
    bm = MOE_BLOCK
    padded = (counts + bm - 1) // bm * bm
    padded_end = jnp.cumsum(padded)
    start = padded_end - padded
    experts = jnp.arange(N_EXPERTS, dtype=i32)
    lookup = lambda table, idx: jnp.sum(jnp.where(idx[..., None] == experts, table, 0), axis=-1)
    dest = (lookup(start, eidx) + rank).reshape(-1).astype(i32)
    blk_row = jnp.arange(n_blocks, dtype=i32) * bm
    blk_e = jnp.minimum(jnp.sum((padded_end[None, :] <= blk_row[:, None]).astype(i32), axis=1), N_EXPERTS - 1)
    blk_cnt = jnp.clip(lookup(counts, blk_e) - (blk_row - lookup(start, blk_e)), 0, bm).astype(i32)
    return dest, blk_e, blk_cnt


def _layer(x2d, mem, seq, g_mix, g_mem, w_in, w_mem_kv, na_rel_bias, g_grp, w_out, g_ffn,
           router_w, router_b, w_gu, b_gu, w_down, b_down):
    t, d = x2d.shape
    b = t // seq
    q_na, k_na, v_na, u_ft, q_mem = _in_proj(x2d, g_mix, w_in.astype(bf16))
    k_mem, v_mem = _mem_kv(mem, g_mem, w_mem_kv.astype(bf16))
    shape3 = lambda a: a.reshape(b, seq, a.shape[-1])
    y_na = _neighbourhood_attention(shape3(q_na), shape3(k_na), shape3(v_na), _na_bias_table(na_rel_bias))
    y_ft = _fourier_mix(shape3(u_ft), _ft_tables(seq))
    x1, h2p, eidx, gates, rank, counts = _mix_out(
        x2d, y_na.reshape(t, -1), y_ft.reshape(t, -1), q_mem, k_mem, v_mem, g_grp, w_out.astype(bf16),
        g_ffn, router_w, router_b, seq)
    n_blocks = (t * TOP_K) // MOE_BLOCK + N_EXPERTS
    dest, blk_e, blk_cnt = _slot_layout(counts[0], eidx, rank, n_blocks)
    xs = _dispatch(h2p, dest, n_blocks * MOE_BLOCK)
    y_slots = _experts(xs, blk_e, blk_cnt, w_gu.astype(bf16), b_gu, w_down.astype(bf16), b_down)
    return x1, gates, dest, y_slots


def kernel(x, mem, g_mix, g_mem, w_in, w_mem_kv, na_rel_bias, g_grp, w_out, g_ffn, router_w, router_b,
           w_gu, b_gu, w_down, b_down, g_final):
    b, seq, d = x.shape
    depth = w_in.shape[0]
    assert depth == 1, "the fused final norm assumes a single layer"
    x2d = x.reshape(b * seq, d)
    x1, gates, dest, y_slots = _layer(
        x2d, mem, seq, g_mix[0], g_mem[0], w_in[0], w_mem_kv[0], na_rel_bias[0], g_grp[0], w_out[0], g_ffn[0],
        router_w[0], router_b[0], w_gu[0], b_gu[0], w_down[0], b_down[0])
    out = _combine(x1, gates, dest, y_slots, g_final)
    return out.reshape(b, seq, d)
```

```python
import functools

import numpy as np
import jax
import jax.numpy as jnp
from jax import lax
from jax.experimental import pallas as pl
from jax.experimental.pallas import tpu as pltpu
from jax.experimental.pallas import tpu_sc as plsc

f32 = jnp.float32
bf16 = jnp.bfloat16
u32 = jnp.uint32
i32 = jnp.int32

GRID_W = 64
NA_HEADS = 8
NA_HEAD_DIM = 64
NA_WIN_ROWS = 8
NA_WIN_COLS = 16
FT_GROUPS = 4
FT_GROUP_DIM = 128
MEM_HEADS = 4
MEM_HEAD_DIM = 128
NA_WIDTH = NA_HEADS * NA_HEAD_DIM
FT_WIDTH = FT_GROUPS * FT_GROUP_DIM
MEM_WIDTH = MEM_HEADS * MEM_HEAD_DIM
N_EXPERTS = 32
TOP_K = 4
SWIGLU_LIMIT = 7.0
SWIGLU_ALPHA = 1.702
EPS = 1e-6

LANES = 128
SUBLANES = 8
VMEM_LIMIT_BYTES = 56 * 1024 * 1024
SC_CORES = 2
SC_SUBCORES = 16
SC_GATHER_CHUNK = 64

TOKEN_TILE = 512
MOE_BLOCK = 512
NA_ROW_UNROLL = 8
FT_N1 = 64
FT_N2 = 128
FT_K1_BLOCK = 8
MASK_VALUE = -1e30


def _params(*semantics):
    return pltpu.CompilerParams(dimension_semantics=semantics, vmem_limit_bytes=VMEM_LIMIT_BYTES)


def _rms_scale(x):
    return x * lax.rsqrt(jnp.mean(x * x, axis=-1, keepdims=True) + EPS)


def _softmax_rows(s):
    p = jnp.exp(s - jnp.max(s, axis=-1, keepdims=True))
    return p / jnp.sum(p, axis=-1, keepdims=True)


def _pack_bf16_pairs(x):
    n = x.shape[1] // 2
    bits = pltpu.bitcast(x.astype(bf16).astype(f32), u32)
    return (bits[:, :n] >> 16) | (bits[:, n:] & jnp.uint32(0xFFFF0000))


def _unpack_bf16_pairs(w):
    lo = pltpu.bitcast(w << 16, f32)
    hi = pltpu.bitcast(w & jnp.uint32(0xFFFF0000), f32)
    return jnp.concatenate([lo, hi], axis=1)


def _in_proj_kernel(x_ref, g_ref, w_ref, qna_ref, kna_ref, vna_ref, uft_ref, qmem_ref):
    h = _rms_scale(x_ref[...]) * g_ref[...]
    proj = jnp.dot(h.astype(bf16), w_ref[...], preferred_element_type=f32)
    o = NA_WIDTH
    qna_ref[...] = (proj[:, :o] * (NA_HEAD_DIM ** -0.5)).astype(bf16)
    kna_ref[...] = proj[:, o:2 * o].astype(bf16)
    vna_ref[...] = proj[:, 2 * o:3 * o].astype(bf16)
    uft_ref[...] = proj[:, 3 * o:3 * o + FT_WIDTH]
    qmem_ref[...] = proj[:, 3 * o + FT_WIDTH:].astype(bf16)


def _in_proj(x2d, g_mix, w_in_bf16):
    t, d = x2d.shape
    tm = TOKEN_TILE
    row = lambda w: pl.BlockSpec((tm, w), lambda i: (i, 0))
    return pl.pallas_call(
        _in_proj_kernel,
        grid=(t // tm,),
        in_specs=[row(d), pl.BlockSpec((1, d), lambda i: (0, 0)),
                  pl.BlockSpec(w_in_bf16.shape, lambda i: (0, 0))],
        out_specs=[row(NA_WIDTH), row(NA_WIDTH), row(NA_WIDTH), row(FT_WIDTH), row(MEM_WIDTH)],
        out_shape=[jax.ShapeDtypeStruct((t, NA_WIDTH), bf16)] * 3
        + [jax.ShapeDtypeStruct((t, FT_WIDTH), f32), jax.ShapeDtypeStruct((t, MEM_WIDTH), bf16)],
        compiler_params=_params("parallel"),
        name="in_proj",
    )(x2d, g_mix.reshape(1, d), w_in_bf16)


def _na_bias_table(rel_bias):
    s = np.arange(NA_WIN_ROWS)[:, None]
    j = np.arange(NA_WIN_ROWS)[None, :]
    dr_idx = (j - s) + (NA_WIN_ROWS - 1)
    c = np.arange(GRID_W)
    dc_idx = np.clip(c[None, :] - c[:, None], -(NA_WIN_COLS - 1), NA_WIN_COLS - 1) + (NA_WIN_COLS - 1)
    col_start = np.clip(c - NA_WIN_COLS // 2, 0, GRID_W - NA_WIN_COLS)
    col_in = (c[None, :] >= col_start[:, None]) & (c[None, :] < col_start[:, None] + NA_WIN_COLS)
    pick_r = jnp.asarray(dr_idx[:, :, None] == np.arange(2 * NA_WIN_ROWS - 1), f32)
    pick_c = jnp.asarray(dc_idx[:, :, None] == np.arange(2 * NA_WIN_COLS - 1), f32)
    tab = jnp.einsum("hab,sja,qcb->hsqjc", rel_bias.astype(f32), pick_r, pick_c, precision=lax.Precision.HIGHEST)
    tab = jnp.where(col_in[None, None, :, None, :], tab, MASK_VALUE)
    tab = tab.reshape(NA_HEADS // 2, 2, NA_WIN_ROWS, GRID_W, NA_WIN_ROWS * GRID_W)
    return tab.transpose(2, 0, 1, 3, 4).reshape(NA_WIN_ROWS, NA_HEADS // 2, 2 * GRID_W, NA_WIN_ROWS * GRID_W)


def _na_kernel(q_ref, k_ref, v_ref, bias_ref, o_ref):
    rows = q_ref.shape[1] // GRID_W
    win = NA_WIN_ROWS * GRID_W
    first_head = lax.broadcasted_iota(i32, (GRID_W, 2 * NA_HEAD_DIM), 1) < NA_HEAD_DIM

    def body(it, carry):
        scores, values, q_offsets = [], [], []
        for u in range(NA_ROW_UNROLL):
            r = it * NA_ROW_UNROLL + u
            row_start = jnp.clip(r - NA_WIN_ROWS // 2, 0, rows - NA_WIN_ROWS)
            q0 = pl.multiple_of(r * GRID_W, GRID_W)
            k0 = pl.multiple_of(row_start * GRID_W, GRID_W)
            q = q_ref[0, pl.ds(q0, GRID_W), :]
            zero = jnp.zeros_like(q)
            qm = jnp.concatenate([jnp.where(first_head, q, zero), jnp.where(first_head, zero, q)], axis=0)
            s = lax.dot_general(qm, k_ref[0, pl.ds(k0, win), :], (((1,), (1,)), ((), ())),
                                preferred_element_type=f32)
            scores.append(s + bias_ref[r - row_start, 0])
            values.append(v_ref[0, pl.ds(k0, win), :])
            q_offsets.append(q0)
        s = jnp.concatenate(scores, axis=0)
        p = jnp.exp(s - jnp.max(s, axis=-1, keepdims=True))
        inv_den = 1.0 / jnp.sum(p, axis=-1, keepdims=True)
        p = p.astype(bf16)
        for u in range(NA_ROW_UNROLL):
            sl = slice(u * 2 * GRID_W, (u + 1) * 2 * GRID_W)
            o = jnp.dot(p[sl], values[u], preferred_element_type=f32) * inv_den[sl]
            o_ref[0, pl.ds(q_offsets[u], GRID_W), :] = jnp.where(
                first_head, o[:GRID_W], o[GRID_W:]).astype(o_ref.dtype)
        return carry

    lax.fori_loop(0, rows // NA_ROW_UNROLL, body, 0)


def _neighbourhood_attention(q, k, v, bias_tab):
    b, s, _ = q.shape
    pair = 2 * NA_HEAD_DIM
    qkv_spec = pl.BlockSpec((1, s, pair), lambda bi, hp: (bi, 0, hp))
    return pl.pallas_call(
        _na_kernel,
        grid=(b, NA_HEADS // 2),
        in_specs=[qkv_spec, qkv_spec, qkv_spec,
                  pl.BlockSpec((NA_WIN_ROWS, 1, 2 * GRID_W, NA_WIN_ROWS * GRID_W), lambda bi, hp: (0, hp, 0, 0))],
        out_specs=qkv_spec,
        out_shape=jax.ShapeDtypeStruct((b, s, NA_WIDTH), bf16),
        compiler_params=_params("parallel", "parallel"),
        name="neighbourhood_attention",
    )(q, k, v, bias_tab)


def _ft_tables(seq):
    assert seq == FT_N1 * FT_N2
    n_blk = FT_N2 // SUBLANES
    k1 = np.arange(FT_N1)[:, None, None, None]
    sr = np.arange(SUBLANES)[None, :, None, None]
    n1 = np.arange(FT_N1)[None, None, :, None]
    sc = np.arange(SUBLANES)[None, None, None, :]
    stage1 = np.zeros((n_blk, 2, FT_N1, SUBLANES, FT_N1, SUBLANES), np.float64)
    for blk in range(n_blk):
        n = FT_N2 * n1 + SUBLANES * blk + sr
        ang = 2.0 * np.pi * ((k1 * n) % seq) / seq
        eye = (sr == sc)
        stage1[blk, 0] = np.cos(ang) * eye
        stage1[blk, 1] = -np.sin(ang) * eye
    stage1 = stage1.reshape(n_blk, 2 * FT_N1 * SUBLANES, FT_N1 * SUBLANES)
    a = np.arange(FT_N2)
    ang2 = 2.0 * np.pi * ((a[:, None] * a[None, :]) % FT_N2) / FT_N2
    c2, s2 = np.cos(ang2), np.sin(ang2)
    stage2 = np.block([[c2, s2], [-s2, c2]])
    g = np.arange(FT_GROUP_DIM)
    angc = 2.0 * np.pi * ((g[:, None] * g[None, :]) % FT_GROUP_DIM) / FT_GROUP_DIM
    norm = 1.0 / np.sqrt(seq * FT_GROUP_DIM)
    chan = np.concatenate([np.cos(angc), np.sin(angc)], axis=0) * norm
    return (jnp.asarray(stage1, bf16), jnp.asarray(stage2, bf16), jnp.asarray(chan, bf16))


def _ft_stage1_kernel(u_ref, m_ref, z_ref):
    rows = FT_N1 * SUBLANES
    u = u_ref[0].reshape(rows, FT_WIDTH).astype(bf16)
    z = jnp.dot(m_ref[0], u, preferred_element_type=f32)
    z_ref[0] = z.reshape(2, FT_N1, SUBLANES, FT_WIDTH)


def _ft_stage2_kernel(z_ref, s2_ref, cs_ref, y_ref):
    gd = FT_GROUP_DIM
    for kk in range(FT_K1_BLOCK):
        zz = jnp.concatenate([z_ref[0, 0, kk], z_ref[0, 1, kk]], axis=0).astype(bf16)
        x = jnp.dot(s2_ref[...], zz, preferred_element_type=f32)
        outs = []
        for g in range(FT_GROUPS):
            xg = jnp.concatenate([x[:FT_N2, g * gd:(g + 1) * gd], x[FT_N2:, g * gd:(g + 1) * gd]], axis=1)
            outs.append(jnp.dot(xg.astype(bf16), cs_ref[...], preferred_element_type=f32))
        y_ref[0, kk] = jnp.concatenate(outs, axis=1).astype(y_ref.dtype)


def _fourier_mix(u, tables):
    b, s, c = u.shape
    stage1, stage2, chan = tables
    n_blk = FT_N2 // SUBLANES
    z = pl.pallas_call(
        _ft_stage1_kernel,
        grid=(n_blk, b),
        in_specs=[pl.BlockSpec((1, FT_N1, SUBLANES, c), lambda j, bi: (bi, 0, j, 0)),
                  pl.BlockSpec((1,) + stage1.shape[1:], lambda j, bi: (j, 0, 0))],
        out_specs=pl.BlockSpec((1, 2, FT_N1, SUBLANES, c), lambda j, bi: (bi, 0, 0, j, 0)),
        out_shape=jax.ShapeDtypeStruct((b, 2, FT_N1, FT_N2, c), f32),
        compiler_params=_params("parallel", "parallel"),
        name="fourier_stage1",
    )(u.reshape(b, FT_N1, FT_N2, c), stage1)
    y = pl.pallas_call(
        _ft_stage2_kernel,
        grid=(b, FT_N1 // FT_K1_BLOCK),
        in_specs=[pl.BlockSpec((1, 2, FT_K1_BLOCK, FT_N2, c), lambda bi, kb: (bi, 0, kb, 0, 0)),
                  pl.BlockSpec(stage2.shape, lambda bi, kb: (0, 0)),
                  pl.BlockSpec(chan.shape, lambda bi, kb: (0, 0))],
        out_specs=pl.BlockSpec((1, FT_K1_BLOCK, FT_N2, c), lambda bi, kb: (bi, kb, 0, 0)),
        out_shape=jax.ShapeDtypeStruct((b, FT_N1, FT_N2, c), bf16),
        compiler_params=_params("parallel", "parallel"),
        name="fourier_stage2",
    )(z, stage2, chan)
    return y.transpose(0, 2, 1, 3).reshape(b, s, c)


def _mem_kv_kernel(mem_ref, g_ref, w_ref, k_ref, v_ref):
    mn = _rms_scale(mem_ref[0]) * g_ref[...]
    kv = jnp.dot(mn.astype(bf16), w_ref[...], preferred_element_type=f32)
    k_ref[0] = kv[:, :MEM_WIDTH].astype(bf16)
    v_ref[0] = kv[:, MEM_WIDTH:].astype(bf16)


def _mem_kv(mem, g_mem, w_kv_bf16):
    b, m, d = mem.shape
    kv_spec = pl.BlockSpec((1, m, MEM_WIDTH), lambda bi: (bi, 0, 0))
    return pl.pallas_call(
        _mem_kv_kernel,
        grid=(b,),
        in_specs=[pl.BlockSpec((1, m, d), lambda bi: (bi, 0, 0)), pl.BlockSpec((1, d), lambda bi: (0, 0)),
                  pl.BlockSpec(w_kv_bf16.shape, lambda bi: (0, 0))],
        out_specs=[kv_spec, kv_spec],
        out_shape=[jax.ShapeDtypeStruct((b, m, MEM_WIDTH), bf16)] * 2,
        compiler_params=_params("parallel"),
        name="mem_kv",
    )(mem, g_mem.reshape(1, d), w_kv_bf16)


def _mix_out_kernel(x_ref, yna_ref, yft_ref, qm_ref, km_ref, vm_ref, ggrp_ref, wout_ref, gffn_ref, rw_ref,
                    rb_ref, x1_ref, h2p_ref, eidx_ref, gate_ref, rank_ref, cnt_ref, carry_ref):
    tm = x_ref.shape[0]

    @pl.when(pl.program_id(0) == 0)
    def _():
        carry_ref[...] = jnp.zeros_like(carry_ref)

    q = qm_ref[...]
    km = km_ref[0]
    vm = vm_ref[0]
    heads = []
    for h in range(MEM_HEADS):
        sl = slice(h * MEM_HEAD_DIM, (h + 1) * MEM_HEAD_DIM)
        s = lax.dot_general(q[:, sl], km[:, sl], (((1,), (1,)), ((), ())), preferred_element_type=f32)
        p = _softmax_rows(s * (MEM_HEAD_DIM ** -0.5))
        heads.append(jnp.dot(p.astype(bf16), vm[:, sl], preferred_element_type=f32))
    ymem = jnp.concatenate(heads, axis=1)

    g = ggrp_ref[...]
    a, c = NA_WIDTH, NA_WIDTH + FT_WIDTH
    y = jnp.concatenate([_rms_scale(yna_ref[...].astype(f32)) * g[:, :a],
                         _rms_scale(yft_ref[...].astype(f32)) * g[:, a:c],
                         _rms_scale(ymem) * g[:, c:]], axis=1)
    x1 = x_ref[...] + jnp.dot(y.astype(bf16), wout_ref[...], preferred_element_type=f32)
    x1_ref[...] = x1
    h2 = _rms_scale(x1) * gffn_ref[...]
    h2p_ref[...] = _pack_bf16_pairs(h2)

    h_hi = h2.astype(bf16)
    h_lo = (h2 - h_hi.astype(f32)).astype(bf16)
    logits = (jnp.dot(h_hi, rw_ref[0], preferred_element_type=f32)
              + jnp.dot(h_hi, rw_ref[1], preferred_element_type=f32)
              + jnp.dot(h_lo, rw_ref[0], preferred_element_type=f32)) + rb_ref[...]
    lane = lax.broadcasted_iota(i32, (tm, N_EXPERTS), 1).astype(f32)
    vals, idxs, sels = [], [], []
    l = logits
    for _ in range(TOP_K):
        m = jnp.max(l, axis=1, keepdims=True)
        idx = jnp.min(jnp.where(l == m, lane, float(N_EXPERTS)), axis=1, keepdims=True)
        sel = lane == idx
        vals.append(m)
        idxs.append(idx)
        sels.append(sel)
        l = jnp.where(sel, -jnp.inf, l)
    ex = [jnp.exp(v - vals[0]) for v in vals]
    den = ex[0] + ex[1] + ex[2] + ex[3]

    onehot = (sels[0] | sels[1] | sels[2] | sels[3]).astype(f32)
    lower = (lax.broadcasted_iota(i32, (tm, tm), 0) > lax.broadcasted_iota(i32, (tm, tm), 1)).astype(bf16)
    before = jnp.dot(lower, onehot.astype(bf16), preferred_element_type=f32) + carry_ref[...]
    ranks = [jnp.sum(jnp.where(sel, before, 0.0), axis=1, keepdims=True) for sel in sels]
    carry_ref[...] = carry_ref[...] + jnp.sum(onehot, axis=0, keepdims=True)
    cnt_ref[...] = carry_ref[...].astype(i32)

    k_lane = lax.broadcasted_iota(i32, (tm, TOP_K), 1)

    def columns(cols):
        out = jnp.broadcast_to(cols[TOP_K - 1], (tm, TOP_K))
        for k in range(TOP_K - 2, -1, -1):
            out = jnp.where(k_lane == k, cols[k], out)
        return out

    eidx_ref[...] = columns(idxs).astype(i32)
    gate_ref[...] = columns([e / den for e in ex])
    rank_ref[...] = columns(ranks).astype(i32)


def _mix_out(x2d, y_na, y_ft, q_mem, k_mem, v_mem, g_grp, w_out_bf16, g_ffn, router_w, router_b, seq):
    t, d = x2d.shape
    tm = TOKEN_TILE
    steps_per_batch = seq // tm
    m = k_mem.shape[1]
    row = lambda w: pl.BlockSpec((tm, w), lambda i: (i, 0))
    full = lambda a: pl.BlockSpec(a.shape, lambda i: (0,) * a.ndim)
    kv_spec = pl.BlockSpec((1, m, MEM_WIDTH), lambda i: (i // steps_per_batch, 0, 0))
    g_grp2, g_ffn2, rb2 = g_grp.reshape(1, -1), g_ffn.reshape(1, d), router_b.reshape(1, N_EXPERTS)
    rw_hi = router_w.astype(bf16)
    router_w = jnp.stack([rw_hi, (router_w - rw_hi.astype(f32)).astype(bf16)])
    return pl.pallas_call(
        _mix_out_kernel,
        grid=(t // tm,),
        in_specs=[row(d), row(NA_WIDTH), row(FT_WIDTH), row(MEM_WIDTH), kv_spec, kv_spec,
                  full(g_grp2), full(w_out_bf16), full(g_ffn2), full(router_w), full(rb2)],
        out_specs=[row(d), row(d // 2), row(TOP_K), row(TOP_K), row(TOP_K),
                   pl.BlockSpec((1, N_EXPERTS), lambda i: (0, 0))],
        out_shape=[jax.ShapeDtypeStruct((t, d), f32), jax.ShapeDtypeStruct((t, d // 2), u32),
                   jax.ShapeDtypeStruct((t, TOP_K), i32), jax.ShapeDtypeStruct((t, TOP_K), f32),
                   jax.ShapeDtypeStruct((t, TOP_K), i32), jax.ShapeDtypeStruct((1, N_EXPERTS), i32)],
        scratch_shapes=[pltpu.VMEM((1, N_EXPERTS), f32)],
        compiler_params=_params("arbitrary"),
        name="mix_out_router",
    )(x2d, y_na, y_ft, q_mem, k_mem, v_mem, g_grp2, w_out_bf16, g_ffn2, router_w, rb2)


def _row_copy(src, src_row, dst, dst_row, sem):
    return pltpu.make_async_copy(src.at[pl.ds(src_row, 1)], dst.at[pl.ds(dst_row, 1)], sem)


def _dispatch_kernel(dest_ref, h_ref, xs_hbm, sem):
    tm = h_ref.shape[0]

    def copies(i):
        return [_row_copy(h_ref, i, xs_hbm, dest_ref[0, 0, TOP_K * i + j], sem) for j in range(TOP_K)]

    def start(i, carry):
        for cp in copies(i):
            cp.start()
        return carry

    def wait(i, carry):
        for cp in copies(i):
            cp.wait()
        return carry

    lax.fori_loop(0, tm, start, 0)
    lax.fori_loop(0, tm, wait, 0)


def _dispatch(h2p, dest, n_slots):
    t, w = h2p.shape
    tm = TOKEN_TILE
    return pl.pallas_call(
        _dispatch_kernel,
        grid=(t // tm,),
        in_specs=[pl.BlockSpec((1, 1, TOP_K * tm), lambda i: (i, 0, 0), memory_space=pltpu.SMEM),
                  pl.BlockSpec((tm, w), lambda i: (i, 0))],
        out_specs=pl.BlockSpec(memory_space=pl.ANY),
        out_shape=jax.ShapeDtypeStruct((n_slots, w), u32),
        scratch_shapes=[pltpu.SemaphoreType.DMA(())],
        compiler_params=_params("arbitrary"),
        name="moe_dispatch",
    )(dest.reshape(t // tm, 1, TOP_K * tm), h2p)


def _expert_kernel(blk_e_ref, blk_cnt_ref, xs_ref, wgu_ref, bgu_ref, wd_ref, bd_ref, y_ref):
    b = pl.program_id(0)
    cnt = blk_cnt_ref[b]
    bm = xs_ref.shape[0]
    de = wd_ref.shape[1]

    @pl.when(cnt > 0)
    def _():
        valid = lax.broadcasted_iota(i32, (bm, 1), 0) < cnt
        x = jnp.where(valid, _unpack_bf16_pairs(xs_ref[...]), 0.0).astype(bf16)
        gu = jnp.dot(x, wgu_ref[0], preferred_element_type=f32) + bgu_ref[0]
        x_glu = jnp.minimum(gu[:, :de], SWIGLU_LIMIT)
        x_lin = jnp.clip(gu[:, de:], -SWIGLU_LIMIT, SWIGLU_LIMIT)
        act = x_glu * (1.0 / (1.0 + jnp.exp(-SWIGLU_ALPHA * x_glu))) * (x_lin + 1.0)
        y = jnp.dot(act.astype(bf16), wd_ref[0], preferred_element_type=f32) + bd_ref[0]
        y_ref[...] = _pack_bf16_pairs(y)

    @pl.when(cnt == 0)
    def _():
        y_ref[...] = jnp.zeros_like(y_ref)


def _experts(xs, blk_e, blk_cnt, w_gu_bf16, b_gu, w_down_bf16, b_down):
    n_slots, w = xs.shape
    bm = MOE_BLOCK
    e, d, de2 = w_gu_bf16.shape
    de = w_down_bf16.shape[1]
    grid_spec = pltpu.PrefetchScalarGridSpec(
        num_scalar_prefetch=2,
        grid=(n_slots // bm,),
        in_specs=[pl.BlockSpec((bm, w), lambda b, be, bc: (b, 0)),
                  pl.BlockSpec((1, d, de2), lambda b, be, bc: (be[b], 0, 0)),
                  pl.BlockSpec((1, 1, de2), lambda b, be, bc: (be[b], 0, 0)),
                  pl.BlockSpec((1, de, d), lambda b, be, bc: (be[b], 0, 0)),
                  pl.BlockSpec((1, 1, d), lambda b, be, bc: (be[b], 0, 0))],
        out_specs=pl.BlockSpec((bm, w), lambda b, be, bc: (b, 0)),
    )
    return pl.pallas_call(
        _expert_kernel,
        grid_spec=grid_spec,
        out_shape=jax.ShapeDtypeStruct((n_slots, w), u32),
        compiler_params=_params("arbitrary"),
        name="moe_experts",
    )(blk_e, blk_cnt, xs, w_gu_bf16, b_gu.reshape(e, 1, de2), w_down_bf16, b_down.reshape(e, 1, d))


def _sc_gather_rows(table, idx):
    n, w = idx.shape[0], table.shape[1]
    workers = SC_CORES * SC_SUBCORES
    chunk = SC_GATHER_CHUNK
    per = n // workers
    steps = per // chunk
    assert per * workers == n and steps * chunk == per and steps % 2 == 0
    mesh = plsc.VectorSubcoreMesh(core_axis_name="c", subcore_axis_name="s",
                                  num_cores=SC_CORES, num_subcores=SC_SUBCORES)

    def body(table_hbm, idx_hbm, out_hbm, idx_v, rows_v, gsem, wsem):
        base = (lax.axis_index("s") * SC_CORES + lax.axis_index("c")) * per
        pltpu.sync_copy(idx_hbm.at[pl.ds(base, per)], idx_v)

        def gather(c, slot):
            return pltpu.make_async_copy(table_hbm.at[idx_v.at[pl.ds(c * chunk, chunk)]], rows_v.at[slot],
                                         gsem.at[slot])

        def write(c, slot):
            return pltpu.make_async_copy(rows_v.at[slot], out_hbm.at[pl.ds(base + c * chunk, chunk)],
                                         wsem.at[slot])

        gather(0, 0).start()

        @pl.loop(0, steps, step=2)
        def _(c0):
            for slot in range(2):
                c = c0 + slot
                gather(c, slot).wait()
                write(c, slot).start()

                @pl.when(c >= 1)
                def _():
                    write(c - 1, 1 - slot).wait()

                @pl.when(c + 1 < steps)
                def _():
                    gather(c + 1, 1 - slot).start()

        write(steps - 1, 1).wait()

    return pl.kernel(
        body,
        out_type=jax.ShapeDtypeStruct((n, w), table.dtype),
        mesh=mesh,
        scratch_types=[pltpu.VMEM((per,), i32), pltpu.VMEM((2, chunk, w), table.dtype),
                       pltpu.SemaphoreType.DMA((2,)), pltpu.SemaphoreType.DMA((2,))],
        name="sc_gather_rows",
    )(table, idx)


def _combine_kernel(x1_ref, gate_ref, gfin_ref, yg_ref, o_ref):
    acc = x1_ref[...]
    gates = gate_ref[...]
    for j in range(TOP_K):
        acc = acc + gates[:, j:j + 1] * _unpack_bf16_pairs(yg_ref[j])
    o_ref[...] = _rms_scale(acc) * gfin_ref[...]


def _combine(x1, gates, dest, y_slots, g_final):
    t, d = x1.shape
    tm = TOKEN_TILE
    w = y_slots.shape[1]
    yg = _sc_gather_rows(y_slots, dest.reshape(t, TOP_K).T.reshape(-1)).reshape(TOP_K, t, w)
    return pl.pallas_call(
        _combine_kernel,
        grid=(t // tm,),
        in_specs=[pl.BlockSpec((tm, d), lambda i: (i, 0)),
                  pl.BlockSpec((tm, TOP_K), lambda i: (i, 0)),
                  pl.BlockSpec((1, d), lambda i: (0, 0)),
                  pl.BlockSpec((TOP_K, tm, w), lambda i: (0, i, 0))],
        out_specs=pl.BlockSpec((tm, d), lambda i: (i, 0)),
        out_shape=jax.ShapeDtypeStruct((t, d), f32),
        compiler_params=_params("parallel"),
        name="moe_combine",
    )(x1, gates, g_final.reshape(1, d), yg)


def _slot_layout(counts, eidx, rank, n_blocks):
    bm = MOE_BLOCK
    padded = (counts + bm - 1) // bm * bm
    padded_end = jnp.cumsum(padded)
    start = padded_end - padded
    experts = jnp.arange(N_EXPERTS, dtype=i32)
    lookup = lambda table, idx: jnp.sum(jnp.where(idx[..., None] == experts, table, 0), axis=-1)
    dest = (lookup(start, eidx) + rank).reshape(-1).astype(i32)
    blk_row = jnp.arange(n_blocks, dtype=i32) * bm
    blk_e = jnp.minimum(jnp.sum((padded_end[None, :] <= blk_row[:, None]).astype(i32), axis=1), N_EXPERTS - 1)
    blk_cnt = jnp.clip(lookup(counts, blk_e) - (blk_row - lookup(start, blk_e)), 0, bm).astype(i32)
    return dest, blk_e, blk_cnt


def _layer(x2d, mem, seq, g_mix, g_mem, w_in, w_mem_kv, na_rel_bias, g_grp, w_out, g_ffn,
           router_w, router_b, w_gu, b_gu, w_down, b_down):
    t, d = x2d.shape
    b = t // seq
    q_na, k_na, v_na, u_ft, q_mem = _in_proj(x2d, g_mix, w_in.astype(bf16))
    k_mem, v_mem = _mem_kv(mem, g_mem, w_mem_kv.astype(bf16))
    shape3 = lambda a: a.reshape(b, seq, a.shape[-1])
    y_na = _neighbourhood_attention(shape3(q_na), shape3(k_na), shape3(v_na), _na_bias_table(na_rel_bias))
    y_ft = _fourier_mix(shape3(u_ft), _ft_tables(seq))
    x1, h2p, eidx, gates, rank, counts = _mix_out(
        x2d, y_na.reshape(t, -1), y_ft.reshape(t, -1), q_mem, k_mem, v_mem, g_grp, w_out.astype(bf16),
        g_ffn, router_w, router_b, seq)
    n_blocks = (t * TOP_K) // MOE_BLOCK + N_EXPERTS
    dest, blk_e, blk_cnt = _slot_layout(counts[0], eidx, rank, n_blocks)
    xs = _dispatch(h2p, dest, n_blocks * MOE_BLOCK)
    y_slots = _experts(xs, blk_e, blk_cnt, w_gu.astype(bf16), b_gu, w_down.astype(bf16), b_down)
    return x1, gates, dest, y_slots


def kernel(x, mem, g_mix, g_mem, w_in, w_mem_kv, na_rel_bias, g_grp, w_out, g_ffn, router_w, router_b,
           w_gu, b_gu, w_down, b_down, g_final):
    b, seq, d = x.shape
    depth = w_in.shape[0]
    assert depth == 1, "the fused final norm assumes a single layer"
    x2d = x.reshape(b * seq, d)
    x1, gates, dest, y_slots = _layer(
        x2d, mem, seq, g_mix[0], g_mem[0], w_in[0], w_mem_kv[0], na_rel_bias[0], g_grp[0], w_out[0], g_ffn[0],
        router_w[0], router_b[0], w_gu[0], b_gu[0], w_down[0], b_down[0])
    out = _combine(x1, gates, dest, y_slots, g_final)
    return out.reshape(b, seq, d)
```

```python
import functools

import numpy as np
import jax
import jax.numpy as jnp
from jax import lax
from jax.experimental import pallas as pl
from jax.experimental.pallas import tpu as pltpu
from jax.experimental.pallas import tpu_sc as plsc

f32 = jnp.float32
bf16 = jnp.bfloat16
u32 = jnp.uint32
i32 = jnp.int32

GRID_W = 64
NA_HEADS = 8
NA_HEAD_DIM = 64
NA_WIN_ROWS = 8
NA_WIN_COLS = 16
FT_GROUPS = 4
FT_GROUP_DIM = 128
MEM_HEADS = 4
MEM_HEAD_DIM = 128
NA_WIDTH = NA_HEADS * NA_HEAD_DIM
FT_WIDTH = FT_GROUPS * FT_GROUP_DIM
MEM_WIDTH = MEM_HEADS * MEM_HEAD_DIM
N_EXPERTS = 32
TOP_K = 4
SWIGLU_LIMIT = 7.0
SWIGLU_ALPHA = 1.702
EPS = 1e-6

LANES = 128
SUBLANES = 8
VMEM_LIMIT_BYTES = 56 * 1024 * 1024
SC_CORES = 2
SC_SUBCORES = 16
SC_GATHER_CHUNK = 64

TOKEN_TILE = 512
MOE_BLOCK = 512
NA_ROW_UNROLL = 8
FT_N1 = 64
FT_N2 = 128
FT_K1_BLOCK = 8
MASK_VALUE = -1e30


def _params(*semantics):
    return pltpu.CompilerParams(dimension_semantics=semantics, vmem_limit_bytes=VMEM_LIMIT_BYTES)


def _rms_scale(x):
    return x * lax.rsqrt(jnp.mean(x * x, axis=-1, keepdims=True) + EPS)


def _softmax_rows(s):
    p = jnp.exp(s - jnp.max(s, axis=-1, keepdims=True))
    return p / jnp.sum(p, axis=-1, keepdims=True)


def _pack_bf16_pairs(x):
    n = x.shape[1] // 2
    bits = pltpu.bitcast(x.astype(bf16).astype(f32), u32)
    return (bits[:, :n] >> 16) | (bits[:, n:] & jnp.uint32(0xFFFF0000))


def _unpack_bf16_pairs(w):
    lo = pltpu.bitcast(w << 16, f32)
    hi = pltpu.bitcast(w & jnp.uint32(0xFFFF0000), f32)
    return jnp.concatenate([lo, hi], axis=1)


def _in_proj_kernel(x_ref, g_ref, w_ref, qna_ref, kna_ref, vna_ref, uft_ref, qmem_ref):
    h = _rms_scale(x_ref[...]) * g_ref[...]
    proj = jnp.dot(h.astype(bf16), w_ref[...], preferred_element_type=f32)
    o = NA_WIDTH
    qna_ref[...] = (proj[:, :o] * (NA_HEAD_DIM ** -0.5)).astype(bf16)
    kna_ref[...] = proj[:, o:2 * o].astype(bf16)
    vna_ref[...] = proj[:, 2 * o:3 * o].astype(bf16)
    uft_ref[...] = proj[:, 3 * o:3 * o + FT_WIDTH]
    qmem_ref[...] = proj[:, 3 * o + FT_WIDTH:].astype(bf16)


def _in_proj(x2d, g_mix, w_in_bf16):
    t, d = x2d.shape
    tm = TOKEN_TILE
    row = lambda w: pl.BlockSpec((tm, w), lambda i: (i, 0))
    return pl.pallas_call(
        _in_proj_kernel,
        grid=(t // tm,),
        in_specs=[row(d), pl.BlockSpec((1, d), lambda i: (0, 0)),
                  pl.BlockSpec(w_in_bf16.shape, lambda i: (0, 0))],
        out_specs=[row(NA_WIDTH), row(NA_WIDTH), row(NA_WIDTH), row(FT_WIDTH), row(MEM_WIDTH)],
        out_shape=[jax.ShapeDtypeStruct((t, NA_WIDTH), bf16)] * 3
        + [jax.ShapeDtypeStruct((t, FT_WIDTH), f32), jax.ShapeDtypeStruct((t, MEM_WIDTH), bf16)],
        compiler_params=_params("parallel"),
        name="in_proj",
    )(x2d, g_mix.reshape(1, d), w_in_bf16)


def _na_bias_table(rel_bias):
    s = np.arange(NA_WIN_ROWS)[:, None]
    j = np.arange(NA_WIN_ROWS)[None, :]
    dr_idx = (j - s) + (NA_WIN_ROWS - 1)
    c = np.arange(GRID_W)
    dc_idx = np.clip(c[None, :] - c[:, None], -(NA_WIN_COLS - 1), NA_WIN_COLS - 1) + (NA_WIN_COLS - 1)
    col_start = np.clip(c - NA_WIN_COLS // 2, 0, GRID_W - NA_WIN_COLS)
    col_in = (c[None, :] >= col_start[:, None]) & (c[None, :] < col_start[:, None] + NA_WIN_COLS)
    pick_r = jnp.asarray(dr_idx[:, :, None] == np.arange(2 * NA_WIN_ROWS - 1), f32)
    pick_c = jnp.asarray(dc_idx[:, :, None] == np.arange(2 * NA_WIN_COLS - 1), f32)
    tab = jnp.einsum("hab,sja,qcb->hsqjc", rel_bias.astype(f32), pick_r, pick_c, precision=lax.Precision.HIGHEST)
    tab = jnp.where(col_in[None, None, :, None, :], tab, MASK_VALUE)
    tab = tab.reshape(NA_HEADS // 2, 2, NA_WIN_ROWS, GRID_W, NA_WIN_ROWS * GRID_W)
    return tab.transpose(2, 0, 1, 3, 4).reshape(NA_WIN_ROWS, NA_HEADS // 2, 2 * GRID_W, NA_WIN_ROWS * GRID_W)


def _na_kernel(q_ref, k_ref, v_ref, bias_ref, o_ref):
    rows = q_ref.shape[1] // GRID_W
    win = NA_WIN_ROWS * GRID_W
    first_head = lax.broadcasted_iota(i32, (GRID_W, 2 * NA_HEAD_DIM), 1) < NA_HEAD_DIM

    def body(it, carry):
        scores, values, q_offsets = [], [], []
        for u in range(NA_ROW_UNROLL):
            r = it * NA_ROW_UNROLL + u
            row_start = jnp.clip(r - NA_WIN_ROWS // 2, 0, rows - NA_WIN_ROWS)
            q0 = pl.multiple_of(r * GRID_W, GRID_W)
            k0 = pl.multiple_of(row_start * GRID_W, GRID_W)
            q = q_ref[0, pl.ds(q0, GRID_W), :]
            zero = jnp.zeros_like(q)
            qm = jnp.concatenate([jnp.where(first_head, q, zero), jnp.where(first_head, zero, q)], axis=0)
            s = lax.dot_general(qm, k_ref[0, pl.ds(k0, win), :], (((1,), (1,)), ((), ())),
                                preferred_element_type=f32)
            scores.append(s + bias_ref[r - row_start, 0])
            values.append(v_ref[0, pl.ds(k0, win), :])
            q_offsets.append(q0)
        s = jnp.concatenate(scores, axis=0)
        p = jnp.exp(s - jnp.max(s, axis=-1, keepdims=True))
        inv_den = 1.0 / jnp.sum(p, axis=-1, keepdims=True)
        p = p.astype(bf16)
        for u in range(NA_ROW_UNROLL):
            sl = slice(u * 2 * GRID_W, (u + 1) * 2 * GRID_W)
            o = jnp.dot(p[sl], values[u], preferred_element_type=f32) * inv_den[sl]
            o_ref[0, pl.ds(q_offsets[u], GRID_W), :] = jnp.where(
                first_head, o[:GRID_W], o[GRID_W:]).astype(o_ref.dtype)
        return carry

    lax.fori_loop(0, rows // NA_ROW_UNROLL, body, 0)


def _neighbourhood_attention(q, k, v, bias_tab):
    b, s, _ = q.shape
    pair = 2 * NA_HEAD_DIM
    qkv_spec = pl.BlockSpec((1, s, pair), lambda bi, hp: (bi, 0, hp))
    return pl.pallas_call(
        _na_kernel,
        grid=(b, NA_HEADS // 2),
        in_specs=[qkv_spec, qkv_spec, qkv_spec,
                  pl.BlockSpec((NA_WIN_ROWS, 1, 2 * GRID_W, NA_WIN_ROWS * GRID_W), lambda bi, hp: (0, hp, 0, 0))],
        out_specs=qkv_spec,
        out_shape=jax.ShapeDtypeStruct((b, s, NA_WIDTH), bf16),
        compiler_params=_params("parallel", "parallel"),
        name="neighbourhood_attention",
    )(q, k, v, bias_tab)


def _ft_tables(seq):
    assert seq == FT_N1 * FT_N2
    n_blk = FT_N2 // SUBLANES
    k1 = np.arange(FT_N1)[:, None, None, None]
    sr = np.arange(SUBLANES)[None, :, None, None]
    n1 = np.arange(FT_N1)[None, None, :, None]
    sc = np.arange(SUBLANES)[None, None, None, :]
    stage1 = np.zeros((n_blk, 2, FT_N1, SUBLANES, FT_N1, SUBLANES), np.float64)
    for blk in range(n_blk):
        n = FT_N2 * n1 + SUBLANES * blk + sr
        ang = 2.0 * np.pi * ((k1 * n) % seq) / seq
        eye = (sr == sc)
        stage1[blk, 0] = np.cos(ang) * eye
        stage1[blk, 1] = -np.sin(ang) * eye
    stage1 = stage1.reshape(n_blk, 2 * FT_N1 * SUBLANES, FT_N1 * SUBLANES)
    a = np.arange(FT_N2)
    ang2 = 2.0 * np.pi * ((a[:, None] * a[None, :]) % FT_N2) / FT_N2
    c2, s2 = np.cos(ang2), np.sin(ang2)
    stage2 = np.block([[c2, s2], [-s2, c2]])
    g = np.arange(FT_GROUP_DIM)
    angc = 2.0 * np.pi * ((g[:, None] * g[None, :]) % FT_GROUP_DIM) / FT_GROUP_DIM
    norm = 1.0 / np.sqrt(seq * FT_GROUP_DIM)
    chan = np.concatenate([np.cos(angc), np.sin(angc)], axis=0) * norm
    return (jnp.asarray(stage1, bf16), jnp.asarray(stage2, bf16), jnp.asarray(chan, bf16))


def _ft_stage1_kernel(u_ref, m_ref, z_ref):
    rows = FT_N1 * SUBLANES
    u = u_ref[0].reshape(rows, FT_WIDTH).astype(bf16)
    z = jnp.dot(m_ref[0], u, preferred_element_type=f32)
    z_ref[0] = z.reshape(2, FT_N1, SUBLANES, FT_WIDTH)


def _ft_stage2_kernel(z_ref, s2_ref, cs_ref, y_ref):
    gd = FT_GROUP_DIM
    for kk in range(FT_K1_BLOCK):
        zz = jnp.concatenate([z_ref[0, 0, kk], z_ref[0, 1, kk]], axis=0).astype(bf16)
        x = jnp.dot(s2_ref[...], zz, preferred_element_type=f32)
        outs = []
        for g in range(FT_GROUPS):
            xg = jnp.concatenate([x[:FT_N2, g * gd:(g + 1) * gd], x[FT_N2:, g * gd:(g + 1) * gd]], axis=1)
            outs.append(jnp.dot(xg.astype(bf16), cs_ref[...], preferred_element_type=f32))
        y_ref[0, kk] = jnp.concatenate(outs, axis=1).astype(y_ref.dtype)


def _fourier_mix(u, tables):
    b, s, c = u.shape
    stage1, stage2, chan = tables
    n_blk = FT_N2 // SUBLANES
    z = pl.pallas_call(
        _ft_stage1_kernel,
        grid=(n_blk, b),
        in_specs=[pl.BlockSpec((1, FT_N1, SUBLANES, c), lambda j, bi: (bi, 0, j, 0)),
                  pl.BlockSpec((1,) + stage1.shape[1:], lambda j, bi: (j, 0, 0))],
        out_specs=pl.BlockSpec((1, 2, FT_N1, SUBLANES, c), lambda j, bi: (bi, 0, 0, j, 0)),
        out_shape=jax.ShapeDtypeStruct((b, 2, FT_N1, FT_N2, c), f32),
        compiler_params=_params("parallel", "parallel"),
        name="fourier_stage1",
    )(u.reshape(b, FT_N1, FT_N2, c), stage1)
    y = pl.pallas_call(
        _ft_stage2_kernel,
        grid=(b, FT_N1 // FT_K1_BLOCK),
        in_specs=[pl.BlockSpec((1, 2, FT_K1_BLOCK, FT_N2, c), lambda bi, kb: (bi, 0, kb, 0, 0)),
                  pl.BlockSpec(stage2.shape, lambda bi, kb: (0, 0)),
                  pl.BlockSpec(chan.shape, lambda bi, kb: (0, 0))],
        out_specs=pl.BlockSpec((1, FT_K1_BLOCK, FT_N2, c), lambda bi, kb: (bi, kb, 0, 0)),
        out_shape=jax.ShapeDtypeStruct((b, FT_N1, FT_N2, c), bf16),
        compiler_params=_params("parallel", "parallel"),
        name="fourier_stage2",
    )(z, stage2, chan)
    return y.transpose(0, 2, 1, 3).reshape(b, s, c)


def _mem_kv_kernel(mem_ref, g_ref, w_ref, k_ref, v_ref):
    mn = _rms_scale(mem_ref[0]) * g_ref[...]
    kv = jnp.dot(mn.astype(bf16), w_ref[...], preferred_element_type=f32)
    k_ref[0] = kv[:, :MEM_WIDTH].astype(bf16)
    v_ref[0] = kv[:, MEM_WIDTH:].astype(bf16)


def _mem_kv(mem, g_mem, w_kv_bf16):
    b, m, d = mem.shape
    kv_spec = pl.BlockSpec((1, m, MEM_WIDTH), lambda bi: (bi, 0, 0))
    return pl.pallas_call(
        _mem_kv_kernel,
        grid=(b,),
        in_specs=[pl.BlockSpec((1, m, d), lambda bi: (bi, 0, 0)), pl.BlockSpec((1, d), lambda bi: (0, 0)),
                  pl.BlockSpec(w_kv_bf16.shape, lambda bi: (0, 0))],
        out_specs=[kv_spec, kv_spec],
        out_shape=[jax.ShapeDtypeStruct((b, m, MEM_WIDTH), bf16)] * 2,
        compiler_params=_params("parallel"),
        name="mem_kv",
    )(mem, g_mem.reshape(1, d), w_kv_bf16)


def _mix_out_kernel(x_ref, yna_ref, yft_ref, qm_ref, km_ref, vm_ref, ggrp_ref, wout_ref, gffn_ref, rw_ref,
                    rb_ref, x1_ref, h2p_ref, eidx_ref, gate_ref, rank_ref, cnt_ref, carry_ref):
    tm = x_ref.shape[0]

    @pl.when(pl.program_id(0) == 0)
    def _():
        carry_ref[...] = jnp.zeros_like(carry_ref)

    q = qm_ref[...]
    km = km_ref[0]
    vm = vm_ref[0]
    heads = []
    for h in range(MEM_HEADS):
        sl = slice(h * MEM_HEAD_DIM, (h + 1) * MEM_HEAD_DIM)
        s = lax.dot_general(q[:, sl], km[:, sl], (((1,), (1,)), ((), ())), preferred_element_type=f32)
        p = _softmax_rows(s * (MEM_HEAD_DIM ** -0.5))
        heads.append(jnp.dot(p.astype(bf16), vm[:, sl], preferred_element_type=f32))
    ymem = jnp.concatenate(heads, axis=1)

    g = ggrp_ref[...]
    a, c = NA_WIDTH, NA_WIDTH + FT_WIDTH
    y = jnp.concatenate([_rms_scale(yna_ref[...].astype(f32)) * g[:, :a],
                         _rms_scale(yft_ref[...].astype(f32)) * g[:, a:c],
                         _rms_scale(ymem) * g[:, c:]], axis=1)
    x1 = x_ref[...] + jnp.dot(y.astype(bf16), wout_ref[...], preferred_element_type=f32)
    x1_ref[...] = x1
    h2 = _rms_scale(x1) * gffn_ref[...]
    h2p_ref[...] = _pack_bf16_pairs(h2)

    h_hi = h2.astype(bf16)
    h_lo = (h2 - h_hi.astype(f32)).astype(bf16)
    logits = (jnp.dot(h_hi, rw_ref[0], preferred_element_type=f32)
              + jnp.dot(h_hi, rw_ref[1], preferred_element_type=f32)
              + jnp.dot(h_lo, rw_ref[0], preferred_element_type=f32)) + rb_ref[...]
    lane = lax.broadcasted_iota(i32, (tm, N_EXPERTS), 1).astype(f32)
    vals, idxs, sels = [], [], []
    l = logits
    for _ in range(TOP_K):
        m = jnp.max(l, axis=1, keepdims=True)
        idx = jnp.min(jnp.where(l == m, lane, float(N_EXPERTS)), axis=1, keepdims=True)
        sel = lane == idx
        vals.append(m)
        idxs.append(idx)
        sels.append(sel)
        l = jnp.where(sel, -jnp.inf, l)
    ex = [jnp.exp(v - vals[0]) for v in vals]
    den = ex[0] + ex[1] + ex[2] + ex[3]

    onehot = (sels[0] | sels[1] | sels[2] | sels[3]).astype(f32)
    lower = (lax.broadcasted_iota(i32, (tm, tm), 0) > lax.broadcasted_iota(i32, (tm, tm), 1)).astype(bf16)
    before = jnp.dot(lower, onehot.astype(bf16), preferred_element_type=f32) + carry_ref[...]
    ranks = [jnp.sum(jnp.where(sel, before, 0.0), axis=1, keepdims=True) for sel in sels]
    carry_ref[...] = carry_ref[...] + jnp.sum(onehot, axis=0, keepdims=True)
    cnt_ref[...] = carry_ref[...].astype(i32)

    k_lane = lax.broadcasted_iota(i32, (tm, TOP_K), 1)

    def columns(cols):
        out = jnp.broadcast_to(cols[TOP_K - 1], (tm, TOP_K))
        for k in range(TOP_K - 2, -1, -1):
            out = jnp.where(k_lane == k, cols[k], out)
        return out

    eidx_ref[...] = columns(idxs).astype(i32)
    gate_ref[...] = columns([e / den for e in ex])
    rank_ref[...] = columns(ranks).astype(i32)


def _mix_out(x2d, y_na, y_ft, q_mem, k_mem, v_mem, g_grp, w_out_bf16, g_ffn, router_w, router_b, seq):
    t, d = x2d.shape
    tm = TOKEN_TILE
    steps_per_batch = seq // tm
    m = k_mem.shape[1]
    row = lambda w: pl.BlockSpec((tm, w), lambda i: (i, 0))
    full = lambda a: pl.BlockSpec(a.shape, lambda i: (0,) * a.ndim)
    kv_spec = pl.BlockSpec((1, m, MEM_WIDTH), lambda i: (i // steps_per_batch, 0, 0))
    g_grp2, g_ffn2, rb2 = g_grp.reshape(1, -1), g_ffn.reshape(1, d), router_b.reshape(1, N_EXPERTS)
    rw_hi = router_w.astype(bf16)
    router_w = jnp.stack([rw_hi, (router_w - rw_hi.astype(f32)).astype(bf16)])
    return pl.pallas_call(
        _mix_out_kernel,
        grid=(t // tm,),
        in_specs=[row(d), row(NA_WIDTH), row(FT_WIDTH), row(MEM_WIDTH), kv_spec, kv_spec,
                  full(g_grp2), full(w_out_bf16), full(g_ffn2), full(router_w), full(rb2)],
        out_specs=[row(d), row(d // 2), row(TOP_K), row(TOP_K), row(TOP_K),
                   pl.BlockSpec((1, N_EXPERTS), lambda i: (0, 0))],
        out_shape=[jax.ShapeDtypeStruct((t, d), f32), jax.ShapeDtypeStruct((t, d // 2), u32),
                   jax.ShapeDtypeStruct((t, TOP_K), i32), jax.ShapeDtypeStruct((t, TOP_K), f32),
                   jax.ShapeDtypeStruct((t, TOP_K), i32), jax.ShapeDtypeStruct((1, N_EXPERTS), i32)],
        scratch_shapes=[pltpu.VMEM((1, N_EXPERTS), f32)],
        compiler_params=_params("arbitrary"),
        name="mix_out_router",
    )(x2d, y_na, y_ft, q_mem, k_mem, v_mem, g_grp2, w_out_bf16, g_ffn2, router_w, rb2)


def _sc_mesh():
    return plsc.VectorSubcoreMesh(core_axis_name="c", subcore_axis_name="s",
                                  num_cores=SC_CORES, num_subcores=SC_SUBCORES)


def _dispatch(h2p, dest, n_slots):
    t, w = h2p.shape
    workers = SC_CORES * SC_SUBCORES
    chunk = SC_GATHER_CHUNK
    per = t // workers
    steps = per // chunk
    assert per * workers == t and steps * chunk == per and steps % 2 == 0
    idx = dest.reshape(workers, steps, chunk, TOP_K).transpose(0, 3, 1, 2)

    def body(h_hbm, idx_hbm, out_hbm, idx_v, rows_v, rsem, ssem):
        wid = lax.axis_index("s") * SC_CORES + lax.axis_index("c")
        base = wid * per
        pltpu.sync_copy(idx_hbm.at[wid], idx_v)

        def read(c, slot):
            return pltpu.make_async_copy(h_hbm.at[pl.ds(base + c * chunk, chunk)], rows_v.at[slot], rsem.at[slot])

        def scatters(c, slot):
            return [pltpu.make_async_copy(rows_v.at[slot], out_hbm.at[idx_v.at[j, c]], ssem.at[slot])
                    for j in range(TOP_K)]

        read(0, 0).start()

        @pl.loop(0, steps, step=2)
        def _(c0):
            for slot in range(2):
                c = c0 + slot
                read(c, slot).wait()
                for cp in scatters(c, slot):
                    cp.start()

                @pl.when(c >= 1)
                def _():
                    for cp in scatters(c - 1, 1 - slot):
                        cp.wait()

                @pl.when(c + 1 < steps)
                def _():
                    read(c + 1, 1 - slot).start()

        for cp in scatters(steps - 1, 1):
            cp.wait()

    return pl.kernel(
        body,
        out_type=jax.ShapeDtypeStruct((n_slots, w), h2p.dtype),
        mesh=_sc_mesh(),
        scratch_types=[pltpu.VMEM((TOP_K, steps, chunk), i32), pltpu.VMEM((2, chunk, w), h2p.dtype),
                       pltpu.SemaphoreType.DMA((2,)), pltpu.SemaphoreType.DMA((2,))],
        name="sc_dispatch_rows",
    )(h2p, idx)


def _expert_kernel(blk_e_ref, blk_cnt_ref, xs_ref, wgu_ref, bgu_ref, wd_ref, bd_ref, y_ref):
    b = pl.program_id(0)
    cnt = blk_cnt_ref[b]
    bm = xs_ref.shape[0]
    de = wd_ref.shape[1]

    @pl.when(cnt > 0)
    def _():
        valid = lax.broadcasted_iota(i32, (bm, 1), 0) < cnt
        x = jnp.where(valid, _unpack_bf16_pairs(xs_ref[...]), 0.0).astype(bf16)
        gu = jnp.dot(x, wgu_ref[0], preferred_element_type=f32) + bgu_ref[0]
        x_glu = jnp.minimum(gu[:, :de], SWIGLU_LIMIT)
        x_lin = jnp.clip(gu[:, de:], -SWIGLU_LIMIT, SWIGLU_LIMIT)
        act = x_glu * (1.0 / (1.0 + jnp.exp(-SWIGLU_ALPHA * x_glu))) * (x_lin + 1.0)
        y = jnp.dot(act.astype(bf16), wd_ref[0], preferred_element_type=f32) + bd_ref[0]
        y_ref[...] = _pack_bf16_pairs(y)

    @pl.when(cnt == 0)
    def _():
        y_ref[...] = jnp.zeros_like(y_ref)


def _experts(xs, blk_e, blk_cnt, w_gu_bf16, b_gu, w_down_bf16, b_down):
    n_slots, w = xs.shape
    bm = MOE_BLOCK
    e, d, de2 = w_gu_bf16.shape
    de = w_down_bf16.shape[1]
    grid_spec = pltpu.PrefetchScalarGridSpec(
        num_scalar_prefetch=2,
        grid=(n_slots // bm,),
        in_specs=[pl.BlockSpec((bm, w), lambda b, be, bc: (b, 0)),
                  pl.BlockSpec((1, d, de2), lambda b, be, bc: (be[b], 0, 0)),
                  pl.BlockSpec((1, 1, de2), lambda b, be, bc: (be[b], 0, 0)),
                  pl.BlockSpec((1, de, d), lambda b, be, bc: (be[b], 0, 0)),
                  pl.BlockSpec((1, 1, d), lambda b, be, bc: (be[b], 0, 0))],
        out_specs=pl.BlockSpec((bm, w), lambda b, be, bc: (b, 0)),
    )
    return pl.pallas_call(
        _expert_kernel,
        grid_spec=grid_spec,
        out_shape=jax.ShapeDtypeStruct((n_slots, w), u32),
        compiler_params=_params("arbitrary"),
        name="moe_experts",
    )(blk_e, blk_cnt, xs, w_gu_bf16, b_gu.reshape(e, 1, de2), w_down_bf16, b_down.reshape(e, 1, d))


def _sc_gather_rows(table, idx):
    n, w = idx.shape[0], table.shape[1]
    workers = SC_CORES * SC_SUBCORES
    chunk = SC_GATHER_CHUNK
    per = n // workers
    steps = per // chunk
    assert per * workers == n and steps * chunk == per and steps % 2 == 0

    def body(table_hbm, idx_hbm, out_hbm, idx_v, rows_v, gsem, wsem):
        base = (lax.axis_index("s") * SC_CORES + lax.axis_index("c")) * per
        pltpu.sync_copy(idx_hbm.at[pl.ds(base, per)], idx_v)

        def gather(c, slot):
            return pltpu.make_async_copy(table_hbm.at[idx_v.at[pl.ds(c * chunk, chunk)]], rows_v.at[slot],
                                         gsem.at[slot])

        def write(c, slot):
            return pltpu.make_async_copy(rows_v.at[slot], out_hbm.at[pl.ds(base + c * chunk, chunk)],
                                         wsem.at[slot])

        gather(0, 0).start()

        @pl.loop(0, steps, step=2)
        def _(c0):
            for slot in range(2):
                c = c0 + slot
                gather(c, slot).wait()
                write(c, slot).start()

                @pl.when(c >= 1)
                def _():
                    write(c - 1, 1 - slot).wait()

                @pl.when(c + 1 < steps)
                def _():
                    gather(c + 1, 1 - slot).start()

        write(steps - 1, 1).wait()

    return pl.kernel(
        body,
        out_type=jax.ShapeDtypeStruct((n, w), table.dtype),
        mesh=_sc_mesh(),
        scratch_types=[pltpu.VMEM((per,), i32), pltpu.VMEM((2, chunk, w), table.dtype),
                       pltpu.SemaphoreType.DMA((2,)), pltpu.SemaphoreType.DMA((2,))],
        name="sc_gather_rows",
    )(table, idx)


def _combine_kernel(x1_ref, gate_ref, gfin_ref, yg_ref, o_ref):
    acc = x1_ref[...]
    gates = gate_ref[...]
    for j in range(TOP_K):
        acc = acc + gates[:, j:j + 1] * _unpack_bf16_pairs(yg_ref[j])
    o_ref[...] = _rms_scale(acc) * gfin_ref[...]


def _combine(x1, gates, dest, y_slots, g_final):
    t, d = x1.shape
    tm = TOKEN_TILE
    w = y_slots.shape[1]
    yg = _sc_gather_rows(y_slots, dest.reshape(t, TOP_K).T.reshape(-1)).reshape(TOP_K, t, w)
    return pl.pallas_call(
        _combine_kernel,
        grid=(t // tm,),
        in_specs=[pl.BlockSpec((tm, d), lambda i: (i, 0)),
                  pl.BlockSpec((tm, TOP_K), lambda i: (i, 0)),
                  pl.BlockSpec((1, d), lambda i: (0, 0)),
                  pl.BlockSpec((TOP_K, tm, w), lambda i: (0, i, 0))],
        out_specs=pl.BlockSpec((tm, d), lambda i: (i, 0)),
        out_shape=jax.ShapeDtypeStruct((t, d), f32),
        compiler_params=_params("parallel"),
        name="moe_combine",
    )(x1, gates, g_final.reshape(1, d), yg)


def _slot_layout(counts, eidx, rank, n_blocks):
    bm = MOE_BLOCK
    padded = (counts + bm - 1) // bm * bm
    padded_end = jnp.cumsum(padded)
    start = padded_end - padded
    experts = jnp.arange(N_EXPERTS, dtype=i32)
    lookup = lambda table, idx: jnp.sum(jnp.where(idx[..., None] == experts, table, 0), axis=-1)
    dest = (lookup(start, eidx) + rank).reshape(-1).astype(i32)
    blk_row = jnp.arange(n_blocks, dtype=i32) * bm
    blk_e = jnp.minimum(jnp.sum((padded_end[None, :] <= blk_row[:, None]).astype(i32), axis=1), N_EXPERTS - 1)
    blk_cnt = jnp.clip(lookup(counts, blk_e) - (blk_row - lookup(start, blk_e)), 0, bm).astype(i32)
    return dest, blk_e, blk_cnt


def _layer(x2d, mem, seq, g_mix, g_mem, w_in, w_mem_kv, na_rel_bias, g_grp, w_out, g_ffn,
           router_w, router_b, w_gu, b_gu, w_down, b_down):
    t, d = x2d.shape
    b = t // seq
    q_na, k_na, v_na, u_ft, q_mem = _in_proj(x2d, g_mix, w_in.astype(bf16))
    k_mem, v_mem = _mem_kv(mem, g_mem, w_mem_kv.astype(bf16))
    shape3 = lambda a: a.reshape(b, seq, a.shape[-1])
    y_na = _neighbourhood_attention(shape3(q_na), shape3(k_na), shape3(v_na), _na_bias_table(na_rel_bias))
    y_ft = _fourier_mix(shape3(u_ft), _ft_tables(seq))
    x1, h2p, eidx, gates, rank, counts = _mix_out(
        x2d, y_na.reshape(t, -1), y_ft.reshape(t, -1), q_mem, k_mem, v_mem, g_grp, w_out.astype(bf16),
        g_ffn, router_w, router_b, seq)
    n_blocks = (t * TOP_K) // MOE_BLOCK + N_EXPERTS
    dest, blk_e, blk_cnt = _slot_layout(counts[0], eidx, rank, n_blocks)
    xs = _dispatch(h2p, dest, n_blocks * MOE_BLOCK)
    y_slots = _experts(xs, blk_e, blk_cnt, w_gu.astype(bf16), b_gu, w_down.astype(bf16), b_down)
    return x1, gates, dest, y_slots


def kernel(x, mem, g_mix, g_mem, w_in, w_mem_kv, na_rel_bias, g_grp, w_out, g_ffn, router_w, router_b,
           w_gu, b_gu, w_down, b_down, g_final):
    b, seq, d = x.shape
    depth = w_in.shape[0]
    assert depth == 1, "the fused final norm assumes a single layer"
    x2d = x.reshape(b * seq, d)
    x1, gates, dest, y_slots = _layer(
        x2d, mem, seq, g_mix[0], g_mem[0], w_in[0], w_mem_kv[0], na_rel_bias[0], g_grp[0], w_out[0], g_ffn[0],
        router_w[0], router_b[0], w_gu[0], b_gu[0], w_down[0], b_down[0])
    out = _combine(x1, gates, dest, y_slots, g_final)
    return out.reshape(b, seq, d)
```

```python
import functools

import numpy as np
import jax
import jax.numpy as jnp
from jax import lax
from jax.experimental import pallas as pl
from jax.experimental.pallas import tpu as pltpu
from jax.experimental.pallas import tpu_sc as plsc

f32 = jnp.float32
bf16 = jnp.bfloat16
u32 = jnp.uint32
i32 = jnp.int32

GRID_W = 64
NA_HEADS = 8
NA_HEAD_DIM = 64
NA_WIN_ROWS = 8
NA_WIN_COLS = 16
FT_GROUPS = 4
FT_GROUP_DIM = 128
MEM_HEADS = 4
MEM_HEAD_DIM = 128
NA_WIDTH = NA_HEADS * NA_HEAD_DIM
FT_WIDTH = FT_GROUPS * FT_GROUP_DIM
MEM_WIDTH = MEM_HEADS * MEM_HEAD_DIM
N_EXPERTS = 32
TOP_K = 4
SWIGLU_LIMIT = 7.0
SWIGLU_ALPHA = 1.702
EPS = 1e-6

LANES = 128
SUBLANES = 8
VMEM_LIMIT_BYTES = 56 * 1024 * 1024
SC_CORES = 2
SC_SUBCORES = 16
SC_GATHER_CHUNK = 64

TOKEN_TILE = 512
MOE_BLOCK = 512
WEIGHT_CAST_ROWS = 128
MOE_TOKEN_GROUPS = 2
NA_ROW_UNROLL = 8
FT_N1 = 64
FT_N2 = 128
FT_K1_BLOCK = 8
MASK_VALUE = -1e30


def _params(*semantics):
    return pltpu.CompilerParams(dimension_semantics=semantics, vmem_limit_bytes=VMEM_LIMIT_BYTES)


def _rms_scale(x):
    return x * lax.rsqrt(jnp.mean(x * x, axis=-1, keepdims=True) + EPS)


def _softmax_rows(s):
    p = jnp.exp(s - jnp.max(s, axis=-1, keepdims=True))
    return p / jnp.sum(p, axis=-1, keepdims=True)


def _pack_bf16_pairs(x):
    n = x.shape[1] // 2
    bits = pltpu.bitcast(x.astype(bf16).astype(f32), u32)
    return (bits[:, :n] >> 16) | (bits[:, n:] & jnp.uint32(0xFFFF0000))


def _unpack_bf16_pairs(w):
    lo = pltpu.bitcast(w << 16, f32)
    hi = pltpu.bitcast(w & jnp.uint32(0xFFFF0000), f32)
    return jnp.concatenate([lo, hi], axis=1)


def _in_proj_kernel(x_ref, g_ref, w_ref, qna_ref, kna_ref, vna_ref, uft_ref, qmem_ref):
    h = _rms_scale(x_ref[...]) * g_ref[...]
    proj = jnp.dot(h.astype(bf16), w_ref[...], preferred_element_type=f32)
    o = NA_WIDTH
    qna_ref[...] = (proj[:, :o] * (NA_HEAD_DIM ** -0.5)).astype(bf16)
    kna_ref[...] = proj[:, o:2 * o].astype(bf16)
    vna_ref[...] = proj[:, 2 * o:3 * o].astype(bf16)
    uft_ref[...] = proj[:, 3 * o:3 * o + FT_WIDTH]
    qmem_ref[...] = proj[:, 3 * o + FT_WIDTH:].astype(bf16)


def _in_proj(x2d, g_mix, w_in_bf16):
    t, d = x2d.shape
    tm = TOKEN_TILE
    row = lambda w: pl.BlockSpec((tm, w), lambda i: (i, 0))
    return pl.pallas_call(
        _in_proj_kernel,
        grid=(t // tm,),
        in_specs=[row(d), pl.BlockSpec((1, d), lambda i: (0, 0)),
                  pl.BlockSpec(w_in_bf16.shape, lambda i: (0, 0))],
        out_specs=[row(NA_WIDTH), row(NA_WIDTH), row(NA_WIDTH), row(FT_WIDTH), row(MEM_WIDTH)],
        out_shape=[jax.ShapeDtypeStruct((t, NA_WIDTH), bf16)] * 3
        + [jax.ShapeDtypeStruct((t, FT_WIDTH), f32), jax.ShapeDtypeStruct((t, MEM_WIDTH), bf16)],
        compiler_params=_params("parallel"),
        name="in_proj",
    )(x2d, g_mix.reshape(1, d), w_in_bf16)


def _na_bias_table(rel_bias):
    s = np.arange(NA_WIN_ROWS)[:, None]
    j = np.arange(NA_WIN_ROWS)[None, :]
    dr_idx = (j - s) + (NA_WIN_ROWS - 1)
    c = np.arange(GRID_W)
    dc_idx = np.clip(c[None, :] - c[:, None], -(NA_WIN_COLS - 1), NA_WIN_COLS - 1) + (NA_WIN_COLS - 1)
    col_start = np.clip(c - NA_WIN_COLS // 2, 0, GRID_W - NA_WIN_COLS)
    col_in = (c[None, :] >= col_start[:, None]) & (c[None, :] < col_start[:, None] + NA_WIN_COLS)
    pick_r = jnp.asarray(dr_idx[:, :, None] == np.arange(2 * NA_WIN_ROWS - 1), f32)
    pick_c = jnp.asarray(dc_idx[:, :, None] == np.arange(2 * NA_WIN_COLS - 1), f32)
    tab = jnp.einsum("hab,sja,qcb->hsqjc", rel_bias.astype(f32), pick_r, pick_c, precision=lax.Precision.HIGHEST)
    tab = jnp.where(col_in[None, None, :, None, :], tab, MASK_VALUE)
    tab = tab.reshape(NA_HEADS // 2, 2, NA_WIN_ROWS, GRID_W, NA_WIN_ROWS * GRID_W)
    return tab.transpose(2, 0, 1, 3, 4).reshape(NA_WIN_ROWS, NA_HEADS // 2, 2 * GRID_W, NA_WIN_ROWS * GRID_W)


def _na_kernel(q_ref, k_ref, v_ref, bias_ref, o_ref):
    rows = q_ref.shape[1] // GRID_W
    win = NA_WIN_ROWS * GRID_W
    first_head = lax.broadcasted_iota(i32, (GRID_W, 2 * NA_HEAD_DIM), 1) < NA_HEAD_DIM

    def body(it, carry):
        scores, values, q_offsets = [], [], []
        for u in range(NA_ROW_UNROLL):
            r = it * NA_ROW_UNROLL + u
            row_start = jnp.clip(r - NA_WIN_ROWS // 2, 0, rows - NA_WIN_ROWS)
            q0 = pl.multiple_of(r * GRID_W, GRID_W)
            k0 = pl.multiple_of(row_start * GRID_W, GRID_W)
            q = q_ref[0, pl.ds(q0, GRID_W), :]
            zero = jnp.zeros_like(q)
            qm = jnp.concatenate([jnp.where(first_head, q, zero), jnp.where(first_head, zero, q)], axis=0)
            s = lax.dot_general(qm, k_ref[0, pl.ds(k0, win), :], (((1,), (1,)), ((), ())),
                                preferred_element_type=f32)
            scores.append(s + bias_ref[r - row_start, 0])
            values.append(v_ref[0, pl.ds(k0, win), :])
            q_offsets.append(q0)
        s = jnp.concatenate(scores, axis=0)
        p = jnp.exp(s - jnp.max(s, axis=-1, keepdims=True))
        inv_den = 1.0 / jnp.sum(p, axis=-1, keepdims=True)
        p = p.astype(bf16)
        for u in range(NA_ROW_UNROLL):
            sl = slice(u * 2 * GRID_W, (u + 1) * 2 * GRID_W)
            o = jnp.dot(p[sl], values[u], preferred_element_type=f32) * inv_den[sl]
            o_ref[0, pl.ds(q_offsets[u], GRID_W), :] = jnp.where(
                first_head, o[:GRID_W], o[GRID_W:]).astype(o_ref.dtype)
        return carry

    lax.fori_loop(0, rows // NA_ROW_UNROLL, body, 0)


def _neighbourhood_attention(q, k, v, bias_tab):
    b, s, _ = q.shape
    pair = 2 * NA_HEAD_DIM
    qkv_spec = pl.BlockSpec((1, s, pair), lambda bi, hp: (bi, 0, hp))
    return pl.pallas_call(
        _na_kernel,
        grid=(b, NA_HEADS // 2),
        in_specs=[qkv_spec, qkv_spec, qkv_spec,
                  pl.BlockSpec((NA_WIN_ROWS, 1, 2 * GRID_W, NA_WIN_ROWS * GRID_W), lambda bi, hp: (0, hp, 0, 0))],
        out_specs=qkv_spec,
        out_shape=jax.ShapeDtypeStruct((b, s, NA_WIDTH), bf16),
        compiler_params=_params("parallel", "parallel"),
        name="neighbourhood_attention",
    )(q, k, v, bias_tab)


def _ft_tables(seq):
    assert seq == FT_N1 * FT_N2
    n_blk = FT_N2 // SUBLANES
    k1 = np.arange(FT_N1)[:, None, None, None]
    sr = np.arange(SUBLANES)[None, :, None, None]
    n1 = np.arange(FT_N1)[None, None, :, None]
    sc = np.arange(SUBLANES)[None, None, None, :]
    stage1 = np.zeros((n_blk, 2, FT_N1, SUBLANES, FT_N1, SUBLANES), np.float64)
    for blk in range(n_blk):
        n = FT_N2 * n1 + SUBLANES * blk + sr
        ang = 2.0 * np.pi * ((k1 * n) % seq) / seq
        eye = (sr == sc)
        stage1[blk, 0] = np.cos(ang) * eye
        stage1[blk, 1] = -np.sin(ang) * eye
    stage1 = stage1.reshape(n_blk, 2 * FT_N1 * SUBLANES, FT_N1 * SUBLANES)
    a = np.arange(FT_N2)
    ang2 = 2.0 * np.pi * ((a[:, None] * a[None, :]) % FT_N2) / FT_N2
    c2, s2 = np.cos(ang2), np.sin(ang2)
    stage2 = np.block([[c2, s2], [-s2, c2]])
    g = np.arange(FT_GROUP_DIM)
    angc = 2.0 * np.pi * ((g[:, None] * g[None, :]) % FT_GROUP_DIM) / FT_GROUP_DIM
    norm = 1.0 / np.sqrt(seq * FT_GROUP_DIM)
    chan = np.concatenate([np.cos(angc), np.sin(angc)], axis=0) * norm
    return (jnp.asarray(stage1, bf16), jnp.asarray(stage2, bf16), jnp.asarray(chan, bf16))


def _ft_stage1_kernel(u_ref, m_ref, z_ref):
    rows = FT_N1 * SUBLANES
    u = u_ref[0].reshape(rows, FT_WIDTH).astype(bf16)
    z = jnp.dot(m_ref[0], u, preferred_element_type=f32)
    z_ref[0] = z.reshape(2, FT_N1, SUBLANES, FT_WIDTH)


def _ft_stage2_kernel(z_ref, s2_ref, cs_ref, y_ref):
    gd = FT_GROUP_DIM
    for kk in range(FT_K1_BLOCK):
        zz = jnp.concatenate([z_ref[0, 0, kk], z_ref[0, 1, kk]], axis=0).astype(bf16)
        x = jnp.dot(s2_ref[...], zz, preferred_element_type=f32)
        outs = []
        for g in range(FT_GROUPS):
            xg = jnp.concatenate([x[:FT_N2, g * gd:(g + 1) * gd], x[FT_N2:, g * gd:(g + 1) * gd]], axis=1)
            outs.append(jnp.dot(xg.astype(bf16), cs_ref[...], preferred_element_type=f32))
        y_ref[0, kk] = jnp.concatenate(outs, axis=1).astype(y_ref.dtype)


def _fourier_mix(u, tables):
    b, s, c = u.shape
    stage1, stage2, chan = tables
    n_blk = FT_N2 // SUBLANES
    z = pl.pallas_call(
        _ft_stage1_kernel,
        grid=(n_blk, b),
        in_specs=[pl.BlockSpec((1, FT_N1, SUBLANES, c), lambda j, bi: (bi, 0, j, 0)),
                  pl.BlockSpec((1,) + stage1.shape[1:], lambda j, bi: (j, 0, 0))],
        out_specs=pl.BlockSpec((1, 2, FT_N1, SUBLANES, c), lambda j, bi: (bi, 0, 0, j, 0)),
        out_shape=jax.ShapeDtypeStruct((b, 2, FT_N1, FT_N2, c), f32),
        compiler_params=_params("parallel", "parallel"),
        name="fourier_stage1",
    )(u.reshape(b, FT_N1, FT_N2, c), stage1)
    y = pl.pallas_call(
        _ft_stage2_kernel,
        grid=(b, FT_N1 // FT_K1_BLOCK),
        in_specs=[pl.BlockSpec((1, 2, FT_K1_BLOCK, FT_N2, c), lambda bi, kb: (bi, 0, kb, 0, 0)),
                  pl.BlockSpec(stage2.shape, lambda bi, kb: (0, 0)),
                  pl.BlockSpec(chan.shape, lambda bi, kb: (0, 0))],
        out_specs=pl.BlockSpec((1, FT_K1_BLOCK, FT_N2, c), lambda bi, kb: (bi, kb, 0, 0)),
        out_shape=jax.ShapeDtypeStruct((b, FT_N1, FT_N2, c), bf16),
        compiler_params=_params("parallel", "parallel"),
        name="fourier_stage2",
    )(z, stage2, chan)
    return y.transpose(0, 2, 1, 3).reshape(b, s, c)


def _mem_kv_kernel(mem_ref, g_ref, w_ref, k_ref, v_ref):
    mn = _rms_scale(mem_ref[0]) * g_ref[...]
    kv = jnp.dot(mn.astype(bf16), w_ref[...], preferred_element_type=f32)
    k_ref[0] = kv[:, :MEM_WIDTH].astype(bf16)
    v_ref[0] = kv[:, MEM_WIDTH:].astype(bf16)


def _mem_kv(mem, g_mem, w_kv_bf16):
    b, m, d = mem.shape
    kv_spec = pl.BlockSpec((1, m, MEM_WIDTH), lambda bi: (bi, 0, 0))
    return pl.pallas_call(
        _mem_kv_kernel,
        grid=(b,),
        in_specs=[pl.BlockSpec((1, m, d), lambda bi: (bi, 0, 0)), pl.BlockSpec((1, d), lambda bi: (0, 0)),
                  pl.BlockSpec(w_kv_bf16.shape, lambda bi: (0, 0))],
        out_specs=[kv_spec, kv_spec],
        out_shape=[jax.ShapeDtypeStruct((b, m, MEM_WIDTH), bf16)] * 2,
        compiler_params=_params("parallel"),
        name="mem_kv",
    )(mem, g_mem.reshape(1, d), w_kv_bf16)


def _mix_out_kernel(x_ref, yna_ref, yft_ref, qm_ref, km_ref, vm_ref, ggrp_ref, wout_ref, gffn_ref, rw_ref,
                    rb_ref, x1_ref, h2p_ref, eidx_ref, gate_ref, rank_ref, cnt_ref, carry_ref):
    tm = x_ref.shape[0]

    @pl.when(pl.program_id(0) == 0)
    def _():
        carry_ref[...] = jnp.zeros_like(carry_ref)

    q = qm_ref[...]
    km = km_ref[0]
    vm = vm_ref[0]
    heads = []
    for h in range(MEM_HEADS):
        sl = slice(h * MEM_HEAD_DIM, (h + 1) * MEM_HEAD_DIM)
        s = lax.dot_general(q[:, sl], km[:, sl], (((1,), (1,)), ((), ())), preferred_element_type=f32)
        p = _softmax_rows(s * (MEM_HEAD_DIM ** -0.5))
        heads.append(jnp.dot(p.astype(bf16), vm[:, sl], preferred_element_type=f32))
    ymem = jnp.concatenate(heads, axis=1)

    g = ggrp_ref[...]
    a, c = NA_WIDTH, NA_WIDTH + FT_WIDTH
    y = jnp.concatenate([_rms_scale(yna_ref[...].astype(f32)) * g[:, :a],
                         _rms_scale(yft_ref[...].astype(f32)) * g[:, a:c],
                         _rms_scale(ymem) * g[:, c:]], axis=1)
    x1 = x_ref[...] + jnp.dot(y.astype(bf16), wout_ref[...], preferred_element_type=f32)
    x1_ref[...] = x1
    h2 = _rms_scale(x1) * gffn_ref[...]
    h2p_ref[...] = _pack_bf16_pairs(h2)

    h_hi = h2.astype(bf16)
    h_lo = (h2 - h_hi.astype(f32)).astype(bf16)
    logits = (jnp.dot(h_hi, rw_ref[0], preferred_element_type=f32)
              + jnp.dot(h_hi, rw_ref[1], preferred_element_type=f32)
              + jnp.dot(h_lo, rw_ref[0], preferred_element_type=f32)) + rb_ref[...]
    lane = lax.broadcasted_iota(i32, (tm, N_EXPERTS), 1).astype(f32)
    vals, idxs, sels = [], [], []
    l = logits
    for _ in range(TOP_K):
        m = jnp.max(l, axis=1, keepdims=True)
        idx = jnp.min(jnp.where(l == m, lane, float(N_EXPERTS)), axis=1, keepdims=True)
        sel = lane == idx
        vals.append(m)
        idxs.append(idx)
        sels.append(sel)
        l = jnp.where(sel, -jnp.inf, l)
    ex = [jnp.exp(v - vals[0]) for v in vals]
    den = ex[0] + ex[1] + ex[2] + ex[3]

    onehot = (sels[0] | sels[1] | sels[2] | sels[3]).astype(f32)
    lower = (lax.broadcasted_iota(i32, (tm, tm), 0) > lax.broadcasted_iota(i32, (tm, tm), 1)).astype(bf16)
    before = jnp.dot(lower, onehot.astype(bf16), preferred_element_type=f32) + carry_ref[...]
    ranks = [jnp.sum(jnp.where(sel, before, 0.0), axis=1, keepdims=True) for sel in sels]
    carry_ref[...] = carry_ref[...] + jnp.sum(onehot, axis=0, keepdims=True)
    cnt_ref[...] = carry_ref[...].astype(i32)

    k_lane = lax.broadcasted_iota(i32, (tm, TOP_K), 1)

    def columns(cols):
        out = jnp.broadcast_to(cols[TOP_K - 1], (tm, TOP_K))
        for k in range(TOP_K - 2, -1, -1):
            out = jnp.where(k_lane == k, cols[k], out)
        return out

    eidx_ref[...] = columns(idxs).astype(i32)
    gate_ref[...] = columns([e / den for e in ex])
    rank_ref[...] = columns(ranks).astype(i32)


def _mix_out(x2d, y_na, y_ft, q_mem, k_mem, v_mem, g_grp, w_out_bf16, g_ffn, router_w2, router_b, seq,
             tile0, n_tiles):
    d = x2d.shape[1]
    tm = TOKEN_TILE
    t = n_tiles * tm
    steps_per_batch = seq // tm
    m = k_mem.shape[1]
    row_in = lambda w: pl.BlockSpec((tm, w), lambda i: (i + tile0, 0))
    row_out = lambda w: pl.BlockSpec((tm, w), lambda i: (i, 0))
    full = lambda a: pl.BlockSpec(a.shape, lambda i: (0,) * a.ndim)
    kv_spec = pl.BlockSpec((1, m, MEM_WIDTH), lambda i: ((i + tile0) // steps_per_batch, 0, 0))
    g_grp2, g_ffn2, rb2 = g_grp.reshape(1, -1), g_ffn.reshape(1, d), router_b.reshape(1, N_EXPERTS)
    return pl.pallas_call(
        _mix_out_kernel,
        grid=(n_tiles,),
        in_specs=[row_in(d), row_in(NA_WIDTH), row_in(FT_WIDTH), row_in(MEM_WIDTH), kv_spec, kv_spec,
                  full(g_grp2), full(w_out_bf16), full(g_ffn2), full(router_w2), full(rb2)],
        out_specs=[row_out(d), row_out(d // 2), row_out(TOP_K), row_out(TOP_K), row_out(TOP_K),
                   pl.BlockSpec((1, N_EXPERTS), lambda i: (0, 0))],
        out_shape=[jax.ShapeDtypeStruct((t, d), f32), jax.ShapeDtypeStruct((t, d // 2), u32),
                   jax.ShapeDtypeStruct((t, TOP_K), i32), jax.ShapeDtypeStruct((t, TOP_K), f32),
                   jax.ShapeDtypeStruct((t, TOP_K), i32), jax.ShapeDtypeStruct((1, N_EXPERTS), i32)],
        scratch_shapes=[pltpu.VMEM((1, N_EXPERTS), f32)],
        compiler_params=_params("arbitrary"),
        name="mix_out_router",
    )(x2d, y_na, y_ft, q_mem, k_mem, v_mem, g_grp2, w_out_bf16, g_ffn2, router_w2, rb2)


def _sc_mesh():
    return plsc.VectorSubcoreMesh(core_axis_name="c", subcore_axis_name="s",
                                  num_cores=SC_CORES, num_subcores=SC_SUBCORES)


def _dispatch(h2p, dest, n_slots):
    t, w = h2p.shape
    workers = SC_CORES * SC_SUBCORES
    chunk = SC_GATHER_CHUNK
    per = t // workers
    steps = per // chunk
    assert per * workers == t and steps * chunk == per and steps % 2 == 0
    idx = dest.reshape(workers, steps, chunk, TOP_K).transpose(0, 3, 1, 2)

    def body(h_hbm, idx_hbm, out_hbm, idx_v, rows_v, rsem, ssem):
        wid = lax.axis_index("s") * SC_CORES + lax.axis_index("c")
        base = wid * per
        pltpu.sync_copy(idx_hbm.at[wid], idx_v)

        def read(c, slot):
            return pltpu.make_async_copy(h_hbm.at[pl.ds(base + c * chunk, chunk)], rows_v.at[slot], rsem.at[slot])

        def scatters(c, slot):
            return [pltpu.make_async_copy(rows_v.at[slot], out_hbm.at[idx_v.at[j, c]], ssem.at[slot])
                    for j in range(TOP_K)]

        read(0, 0).start()

        @pl.loop(0, steps, step=2)
        def _(c0):
            for slot in range(2):
                c = c0 + slot
                read(c, slot).wait()
                for cp in scatters(c, slot):
                    cp.start()

                @pl.when(c >= 1)
                def _():
                    for cp in scatters(c - 1, 1 - slot):
                        cp.wait()

                @pl.when(c + 1 < steps)
                def _():
                    read(c + 1, 1 - slot).start()

        for cp in scatters(steps - 1, 1):
            cp.wait()

    return pl.kernel(
        body,
        out_type=jax.ShapeDtypeStruct((n_slots, w), h2p.dtype),
        mesh=_sc_mesh(),
        scratch_types=[pltpu.VMEM((TOP_K, steps, chunk), i32), pltpu.VMEM((2, chunk, w), h2p.dtype),
                       pltpu.SemaphoreType.DMA((2,)), pltpu.SemaphoreType.DMA((2,))],
        name="sc_dispatch_rows",
    )(h2p, idx)


def _expert_kernel(blk_e_ref, blk_cnt_ref, xs_ref, wgu_ref, bgu_ref, wd_ref, bd_ref, y_ref, wgu_bf, wd_bf):
    b = pl.program_id(0)
    cnt = blk_cnt_ref[b]
    bm = xs_ref.shape[0]
    de = wd_ref.shape[1]

    @pl.when(jnp.logical_or(b == 0, blk_e_ref[b] != blk_e_ref[jnp.maximum(b - 1, 0)]))
    def _():
        def convert(i, carry):
            rows = pl.ds(pl.multiple_of(i * WEIGHT_CAST_ROWS, WEIGHT_CAST_ROWS), WEIGHT_CAST_ROWS)
            wgu_bf[rows, :] = wgu_ref[0, rows, :].astype(bf16)
            wd_bf[rows, :] = wd_ref[0, rows, :].astype(bf16)
            return carry

        lax.fori_loop(0, wgu_ref.shape[1] // WEIGHT_CAST_ROWS, convert, 0)

    @pl.when(cnt > 0)
    def _():
        valid = lax.broadcasted_iota(i32, (bm, 1), 0) < cnt
        x = jnp.where(valid, _unpack_bf16_pairs(xs_ref[...]), 0.0).astype(bf16)
        gu = jnp.dot(x, wgu_bf[...], preferred_element_type=f32) + bgu_ref[0]
        x_glu = jnp.minimum(gu[:, :de], SWIGLU_LIMIT)
        x_lin = jnp.clip(gu[:, de:], -SWIGLU_LIMIT, SWIGLU_LIMIT)
        act = x_glu * (1.0 / (1.0 + jnp.exp(-SWIGLU_ALPHA * x_glu))) * (x_lin + 1.0)
        y = jnp.dot(act.astype(bf16), wd_bf[...], preferred_element_type=f32) + bd_ref[0]
        y_ref[...] = _pack_bf16_pairs(y)

    @pl.when(cnt == 0)
    def _():
        y_ref[...] = jnp.zeros_like(y_ref)


def _experts(xs, blk_e, blk_cnt, w_gu, b_gu, w_down, b_down):
    n_slots, w = xs.shape
    bm = MOE_BLOCK
    e, d, de2 = w_gu.shape
    de = w_down.shape[1]
    assert de == d, "one row loop converts both weight matrices"
    grid_spec = pltpu.PrefetchScalarGridSpec(
        num_scalar_prefetch=2,
        grid=(n_slots // bm,),
        in_specs=[pl.BlockSpec((bm, w), lambda b, be, bc: (b, 0)),
                  pl.BlockSpec((1, d, de2), lambda b, be, bc: (be[b], 0, 0)),
                  pl.BlockSpec((1, 1, de2), lambda b, be, bc: (be[b], 0, 0)),
                  pl.BlockSpec((1, de, d), lambda b, be, bc: (be[b], 0, 0)),
                  pl.BlockSpec((1, 1, d), lambda b, be, bc: (be[b], 0, 0))],
        out_specs=pl.BlockSpec((bm, w), lambda b, be, bc: (b, 0)),
        scratch_shapes=[pltpu.VMEM((d, de2), bf16), pltpu.VMEM((de, d), bf16)],
    )
    return pl.pallas_call(
        _expert_kernel,
        grid_spec=grid_spec,
        out_shape=jax.ShapeDtypeStruct((n_slots, w), u32),
        compiler_params=_params("arbitrary"),
        name="moe_experts",
    )(blk_e, blk_cnt, xs, w_gu, b_gu.reshape(e, 1, de2), w_down, b_down.reshape(e, 1, d))


def _sc_gather_rows(table, idx):
    n, w = idx.shape[0], table.shape[1]
    workers = SC_CORES * SC_SUBCORES
    chunk = SC_GATHER_CHUNK
    per = n // workers
    steps = per // chunk
    assert per * workers == n and steps * chunk == per and steps % 2 == 0

    def body(table_hbm, idx_hbm, out_hbm, idx_v, rows_v, gsem, wsem):
        base = (lax.axis_index("s") * SC_CORES + lax.axis_index("c")) * per
        pltpu.sync_copy(idx_hbm.at[pl.ds(base, per)], idx_v)

        def gather(c, slot):
            return pltpu.make_async_copy(table_hbm.at[idx_v.at[pl.ds(c * chunk, chunk)]], rows_v.at[slot],
                                         gsem.at[slot])

        def write(c, slot):
            return pltpu.make_async_copy(rows_v.at[slot], out_hbm.at[pl.ds(base + c * chunk, chunk)],
                                         wsem.at[slot])

        gather(0, 0).start()

        @pl.loop(0, steps, step=2)
        def _(c0):
            for slot in range(2):
                c = c0 + slot
                gather(c, slot).wait()
                write(c, slot).start()

                @pl.when(c >= 1)
                def _():
                    write(c - 1, 1 - slot).wait()

                @pl.when(c + 1 < steps)
                def _():
                    gather(c + 1, 1 - slot).start()

        write(steps - 1, 1).wait()

    return pl.kernel(
        body,
        out_type=jax.ShapeDtypeStruct((n, w), table.dtype),
        mesh=_sc_mesh(),
        scratch_types=[pltpu.VMEM((per,), i32), pltpu.VMEM((2, chunk, w), table.dtype),
                       pltpu.SemaphoreType.DMA((2,)), pltpu.SemaphoreType.DMA((2,))],
        name="sc_gather_rows",
    )(table, idx)


def _combine_kernel(x1_ref, gate_ref, gfin_ref, yg_ref, *rest):
    o_ref = rest[-1]
    acc = x1_ref[...]
    gates = gate_ref[...]
    for j in range(TOP_K):
        acc = acc + gates[:, j:j + 1] * _unpack_bf16_pairs(yg_ref[j])
    o_ref[...] = _rms_scale(acc) * gfin_ref[...]


def _combine(x1, gates, dest, y_slots, g_final, out_prev, tile0, total_tokens):
    t, d = x1.shape
    tm = TOKEN_TILE
    w = y_slots.shape[1]
    yg = _sc_gather_rows(y_slots, dest.reshape(t, TOP_K).T.reshape(-1)).reshape(TOP_K, t, w)
    in_specs = [pl.BlockSpec((tm, d), lambda i: (i, 0)),
                pl.BlockSpec((tm, TOP_K), lambda i: (i, 0)),
                pl.BlockSpec((1, d), lambda i: (0, 0)),
                pl.BlockSpec((TOP_K, tm, w), lambda i: (0, i, 0))]
    args = [x1, gates, g_final.reshape(1, d), yg]
    aliases = {}
    if out_prev is not None:
        in_specs.append(pl.BlockSpec(memory_space=pl.ANY))
        args.append(out_prev)
        aliases = {len(args) - 1: 0}
    return pl.pallas_call(
        _combine_kernel,
        grid=(t // tm,),
        in_specs=in_specs,
        out_specs=pl.BlockSpec((tm, d), lambda i: (i + tile0, 0)),
        out_shape=jax.ShapeDtypeStruct((total_tokens, d), f32),
        input_output_aliases=aliases,
        compiler_params=_params("parallel"),
        name="moe_combine",
    )(*args)


def _slot_layout(counts, eidx, rank, n_blocks):
    bm = MOE_BLOCK
    padded = (counts + bm - 1) // bm * bm
    padded_end = jnp.cumsum(padded)
    start = padded_end - padded
    experts = jnp.arange(N_EXPERTS, dtype=i32)
    lookup = lambda table, idx: jnp.sum(jnp.where(idx[..., None] == experts, table, 0), axis=-1)
    dest = (lookup(start, eidx) + rank).reshape(-1).astype(i32)
    blk_row = jnp.arange(n_blocks, dtype=i32) * bm
    blk_e = jnp.minimum(jnp.sum((padded_end[None, :] <= blk_row[:, None]).astype(i32), axis=1), N_EXPERTS - 1)
    blk_cnt = jnp.clip(lookup(counts, blk_e) - (blk_row - lookup(start, blk_e)), 0, bm).astype(i32)
    return dest, blk_e, blk_cnt


def _layer_and_final_norm(x2d, mem, seq, g_mix, g_mem, w_in, w_mem_kv, na_rel_bias, g_grp, w_out, g_ffn,
                          router_w, router_b, w_gu, b_gu, w_down, b_down, g_final):
    t, d = x2d.shape
    b = t // seq
    q_na, k_na, v_na, u_ft, q_mem = _in_proj(x2d, g_mix, w_in.astype(bf16))
    k_mem, v_mem = _mem_kv(mem, g_mem, w_mem_kv.astype(bf16))
    shape3 = lambda a: a.reshape(b, seq, a.shape[-1])
    y_na = _neighbourhood_attention(shape3(q_na), shape3(k_na), shape3(v_na), _na_bias_table(na_rel_bias))
    y_ft = _fourier_mix(shape3(u_ft), _ft_tables(seq))
    y_na, y_ft = y_na.reshape(t, -1), y_ft.reshape(t, -1)
    w_out_bf16 = w_out.astype(bf16)
    rw_hi = router_w.astype(bf16)
    router_w2 = jnp.stack([rw_hi, (router_w - rw_hi.astype(f32)).astype(bf16)])

    tiles = t // TOKEN_TILE
    assert tiles % MOE_TOKEN_GROUPS == 0
    group_tiles = tiles // MOE_TOKEN_GROUPS
    group_tokens = group_tiles * TOKEN_TILE
    n_blocks = (group_tokens * TOP_K) // MOE_BLOCK + N_EXPERTS
    out = None
    for grp in range(MOE_TOKEN_GROUPS):
        tile0 = grp * group_tiles
        x1, h2p, eidx, gates, rank, counts = _mix_out(
            x2d, y_na, y_ft, q_mem, k_mem, v_mem, g_grp, w_out_bf16, g_ffn, router_w2, router_b, seq,
            tile0, group_tiles)
        dest, blk_e, blk_cnt = _slot_layout(counts[0], eidx, rank, n_blocks)
        xs = _dispatch(h2p, dest, n_blocks * MOE_BLOCK)
        y_slots = _experts(xs, blk_e, blk_cnt, w_gu, b_gu, w_down, b_down)
        out = _combine(x1, gates, dest, y_slots, g_final, out, tile0, t)
    return out


def kernel(x, mem, g_mix, g_mem, w_in, w_mem_kv, na_rel_bias, g_grp, w_out, g_ffn, router_w, router_b,
           w_gu, b_gu, w_down, b_down, g_final):
    b, seq, d = x.shape
    depth = w_in.shape[0]
    assert depth == 1, "the final norm is fused into the single layer's combine step"
    out = _layer_and_final_norm(
        x.reshape(b * seq, d), mem, seq, g_mix[0], g_mem[0], w_in[0], w_mem_kv[0], na_rel_bias[0], g_grp[0],
        w_out[0], g_ffn[0], router_w[0], router_b[0], w_gu[0], b_gu[0], w_down[0], b_down[0], g_final)
    return out.reshape(b, seq, d)
```

```python
import functools

import numpy as np
import jax
import jax.numpy as jnp
from jax import lax
from jax.experimental import pallas as pl
from jax.experimental.pallas import tpu as pltpu
from jax.experimental.pallas import tpu_sc as plsc

f32 = jnp.float32
bf16 = jnp.bfloat16
u32 = jnp.uint32
i32 = jnp.int32

GRID_W = 64
NA_HEADS = 8
NA_HEAD_DIM = 64
NA_WIN_ROWS = 8
NA_WIN_COLS = 16
FT_GROUPS = 4
FT_GROUP_DIM = 128
MEM_HEADS = 4
MEM_HEAD_DIM = 128
NA_WIDTH = NA_HEADS * NA_HEAD_DIM
FT_WIDTH = FT_GROUPS * FT_GROUP_DIM
MEM_WIDTH = MEM_HEADS * MEM_HEAD_DIM
N_EXPERTS = 32
TOP_K = 4
SWIGLU_LIMIT = 7.0
SWIGLU_ALPHA = 1.702
EPS = 1e-6

LANES = 128
SUBLANES = 8
VMEM_LIMIT_BYTES = 56 * 1024 * 1024
SC_CORES = 2
SC_SUBCORES = 16
SC_GATHER_CHUNK = 64

TOKEN_TILE = 512
MOE_BLOCK = 512
WEIGHT_CAST_ROWS = 128
MOE_TOKEN_GROUPS = 2
NA_ROW_UNROLL = 8
FT_N1 = 64
FT_N2 = 128
FT_K1_BLOCK = 8
MASK_VALUE = -1e30


def _params(*semantics):
    return pltpu.CompilerParams(dimension_semantics=semantics, vmem_limit_bytes=VMEM_LIMIT_BYTES)


def _rms_scale(x):
    return x * lax.rsqrt(jnp.mean(x * x, axis=-1, keepdims=True) + EPS)


def _softmax_rows(s):
    p = jnp.exp(s - jnp.max(s, axis=-1, keepdims=True))
    return p / jnp.sum(p, axis=-1, keepdims=True)


def _pack_bf16_pairs(x):
    n = x.shape[1] // 2
    bits = pltpu.bitcast(x.astype(bf16).astype(f32), u32)
    return (bits[:, :n] >> 16) | (bits[:, n:] & jnp.uint32(0xFFFF0000))


def _unpack_bf16_pairs(w):
    lo = pltpu.bitcast(w << 16, f32)
    hi = pltpu.bitcast(w & jnp.uint32(0xFFFF0000), f32)
    return jnp.concatenate([lo, hi], axis=1)


def _in_proj_kernel(x_ref, g_ref, w_ref, qna_ref, kna_ref, vna_ref, uft_ref, qmem_ref):
    h = _rms_scale(x_ref[...]) * g_ref[...]
    proj = jnp.dot(h.astype(bf16), w_ref[...], preferred_element_type=f32)
    o = NA_WIDTH
    qna_ref[...] = (proj[:, :o] * (NA_HEAD_DIM ** -0.5)).astype(bf16)
    kna_ref[...] = proj[:, o:2 * o].astype(bf16)
    vna_ref[...] = proj[:, 2 * o:3 * o].astype(bf16)
    uft_ref[...] = proj[:, 3 * o:3 * o + FT_WIDTH]
    qmem_ref[...] = proj[:, 3 * o + FT_WIDTH:].astype(bf16)


def _in_proj(x2d, g_mix, w_in_bf16):
    t, d = x2d.shape
    tm = TOKEN_TILE
    row = lambda w: pl.BlockSpec((tm, w), lambda i: (i, 0))
    return pl.pallas_call(
        _in_proj_kernel,
        grid=(t // tm,),
        in_specs=[row(d), pl.BlockSpec((1, d), lambda i: (0, 0)),
                  pl.BlockSpec(w_in_bf16.shape, lambda i: (0, 0))],
        out_specs=[row(NA_WIDTH), row(NA_WIDTH), row(NA_WIDTH), row(FT_WIDTH), row(MEM_WIDTH)],
        out_shape=[jax.ShapeDtypeStruct((t, NA_WIDTH), bf16)] * 3
        + [jax.ShapeDtypeStruct((t, FT_WIDTH), f32), jax.ShapeDtypeStruct((t, MEM_WIDTH), bf16)],
        compiler_params=_params("parallel"),
        name="in_proj",
    )(x2d, g_mix.reshape(1, d), w_in_bf16)


def _na_bias_table(rel_bias):
    s = np.arange(NA_WIN_ROWS)[:, None]
    j = np.arange(NA_WIN_ROWS)[None, :]
    dr_idx = (j - s) + (NA_WIN_ROWS - 1)
    c = np.arange(GRID_W)
    dc_idx = np.clip(c[None, :] - c[:, None], -(NA_WIN_COLS - 1), NA_WIN_COLS - 1) + (NA_WIN_COLS - 1)
    col_start = np.clip(c - NA_WIN_COLS // 2, 0, GRID_W - NA_WIN_COLS)
    col_in = (c[None, :] >= col_start[:, None]) & (c[None, :] < col_start[:, None] + NA_WIN_COLS)
    pick_r = jnp.asarray(dr_idx[:, :, None] == np.arange(2 * NA_WIN_ROWS - 1), f32)
    pick_c = jnp.asarray(dc_idx[:, :, None] == np.arange(2 * NA_WIN_COLS - 1), f32)
    tab = jnp.einsum("hab,sja,qcb->hsqjc", rel_bias.astype(f32), pick_r, pick_c, precision=lax.Precision.HIGHEST)
    tab = jnp.where(col_in[None, None, :, None, :], tab, MASK_VALUE)
    tab = tab.reshape(NA_HEADS // 2, 2, NA_WIN_ROWS, GRID_W, NA_WIN_ROWS * GRID_W)
    return tab.transpose(2, 0, 1, 3, 4).reshape(NA_WIN_ROWS, NA_HEADS // 2, 2 * GRID_W, NA_WIN_ROWS * GRID_W)


def _na_kernel(q_ref, k_ref, v_ref, bias_ref, o_ref):
    rows = q_ref.shape[1] // GRID_W
    win = NA_WIN_ROWS * GRID_W
    first_head = lax.broadcasted_iota(i32, (GRID_W, 2 * NA_HEAD_DIM), 1) < NA_HEAD_DIM

    def body(it, carry):
        scores, values, q_offsets = [], [], []
        for u in range(NA_ROW_UNROLL):
            r = it * NA_ROW_UNROLL + u
            row_start = jnp.clip(r - NA_WIN_ROWS // 2, 0, rows - NA_WIN_ROWS)
            q0 = pl.multiple_of(r * GRID_W, GRID_W)
            k0 = pl.multiple_of(row_start * GRID_W, GRID_W)
            q = q_ref[0, pl.ds(q0, GRID_W), :]
            zero = jnp.zeros_like(q)
            qm = jnp.concatenate([jnp.where(first_head, q, zero), jnp.where(first_head, zero, q)], axis=0)
            s = lax.dot_general(qm, k_ref[0, pl.ds(k0, win), :], (((1,), (1,)), ((), ())),
                                preferred_element_type=f32)
            scores.append(s + bias_ref[r - row_start, 0])
            values.append(v_ref[0, pl.ds(k0, win), :])
            q_offsets.append(q0)
        s = jnp.concatenate(scores, axis=0)
        p = jnp.exp(s - jnp.max(s, axis=-1, keepdims=True))
        inv_den = 1.0 / jnp.sum(p, axis=-1, keepdims=True)
        p = p.astype(bf16)
        for u in range(NA_ROW_UNROLL):
            sl = slice(u * 2 * GRID_W, (u + 1) * 2 * GRID_W)
            o = jnp.dot(p[sl], values[u], preferred_element_type=f32) * inv_den[sl]
            o_ref[0, pl.ds(q_offsets[u], GRID_W), :] = jnp.where(
                first_head, o[:GRID_W], o[GRID_W:]).astype(o_ref.dtype)
        return carry

    lax.fori_loop(0, rows // NA_ROW_UNROLL, body, 0)


def _neighbourhood_attention(q, k, v, bias_tab):
    b, s, _ = q.shape
    pair = 2 * NA_HEAD_DIM
    qkv_spec = pl.BlockSpec((1, s, pair), lambda bi, hp: (bi, 0, hp))
    return pl.pallas_call(
        _na_kernel,
        grid=(b, NA_HEADS // 2),
        in_specs=[qkv_spec, qkv_spec, qkv_spec,
                  pl.BlockSpec((NA_WIN_ROWS, 1, 2 * GRID_W, NA_WIN_ROWS * GRID_W), lambda bi, hp: (0, hp, 0, 0))],
        out_specs=qkv_spec,
        out_shape=jax.ShapeDtypeStruct((b, s, NA_WIDTH), bf16),
        compiler_params=_params("parallel", "parallel"),
        name="neighbourhood_attention",
    )(q, k, v, bias_tab)


def _ft_tables(seq):
    assert seq == FT_N1 * FT_N2
    n_blk = FT_N2 // SUBLANES
    k1 = np.arange(FT_N1)[:, None, None, None]
    sr = np.arange(SUBLANES)[None, :, None, None]
    n1 = np.arange(FT_N1)[None, None, :, None]
    sc = np.arange(SUBLANES)[None, None, None, :]
    stage1 = np.zeros((n_blk, 2, FT_N1, SUBLANES, FT_N1, SUBLANES), np.float64)
    for blk in range(n_blk):
        n = FT_N2 * n1 + SUBLANES * blk + sr
        ang = 2.0 * np.pi * ((k1 * n) % seq) / seq
        eye = (sr == sc)
        stage1[blk, 0] = np.cos(ang) * eye
        stage1[blk, 1] = -np.sin(ang) * eye
    stage1 = stage1.reshape(n_blk, 2 * FT_N1 * SUBLANES, FT_N1 * SUBLANES)
    a = np.arange(FT_N2)
    ang2 = 2.0 * np.pi * ((a[:, None] * a[None, :]) % FT_N2) / FT_N2
    c2, s2 = np.cos(ang2), np.sin(ang2)
    stage2 = np.block([[c2, s2], [-s2, c2]])
    g = np.arange(FT_GROUP_DIM)
    angc = 2.0 * np.pi * ((g[:, None] * g[None, :]) % FT_GROUP_DIM) / FT_GROUP_DIM
    norm = 1.0 / np.sqrt(seq * FT_GROUP_DIM)
    chan = np.concatenate([np.cos(angc), np.sin(angc)], axis=0) * norm
    return (jnp.asarray(stage1, bf16), jnp.asarray(stage2, bf16), jnp.asarray(chan, bf16))


def _ft_stage1_kernel(u_ref, m_ref, z_ref):
    rows = FT_N1 * SUBLANES
    u = u_ref[0].reshape(rows, FT_WIDTH).astype(bf16)
    z = jnp.dot(m_ref[0], u, preferred_element_type=f32)
    z_ref[0] = z.reshape(2, FT_N1, SUBLANES, FT_WIDTH)


def _ft_stage2_kernel(z_ref, s2_ref, cs_ref, y_ref):
    gd = FT_GROUP_DIM
    for kk in range(FT_K1_BLOCK):
        zz = jnp.concatenate([z_ref[0, 0, kk], z_ref[0, 1, kk]], axis=0).astype(bf16)
        x = jnp.dot(s2_ref[...], zz, preferred_element_type=f32)
        outs = []
        for g in range(FT_GROUPS):
            xg = jnp.concatenate([x[:FT_N2, g * gd:(g + 1) * gd], x[FT_N2:, g * gd:(g + 1) * gd]], axis=1)
            outs.append(jnp.dot(xg.astype(bf16), cs_ref[...], preferred_element_type=f32))
        y_ref[0, kk] = jnp.concatenate(outs, axis=1).astype(y_ref.dtype)


def _fourier_mix(u, tables):
    b, s, c = u.shape
    stage1, stage2, chan = tables
    n_blk = FT_N2 // SUBLANES
    z = pl.pallas_call(
        _ft_stage1_kernel,
        grid=(n_blk, b),
        in_specs=[pl.BlockSpec((1, FT_N1, SUBLANES, c), lambda j, bi: (bi, 0, j, 0)),
                  pl.BlockSpec((1,) + stage1.shape[1:], lambda j, bi: (j, 0, 0))],
        out_specs=pl.BlockSpec((1, 2, FT_N1, SUBLANES, c), lambda j, bi: (bi, 0, 0, j, 0)),
        out_shape=jax.ShapeDtypeStruct((b, 2, FT_N1, FT_N2, c), f32),
        compiler_params=_params("parallel", "parallel"),
        name="fourier_stage1",
    )(u.reshape(b, FT_N1, FT_N2, c), stage1)
    y = pl.pallas_call(
        _ft_stage2_kernel,
        grid=(b, FT_N1 // FT_K1_BLOCK),
        in_specs=[pl.BlockSpec((1, 2, FT_K1_BLOCK, FT_N2, c), lambda bi, kb: (bi, 0, kb, 0, 0)),
                  pl.BlockSpec(stage2.shape, lambda bi, kb: (0, 0)),
                  pl.BlockSpec(chan.shape, lambda bi, kb: (0, 0))],
        out_specs=pl.BlockSpec((1, FT_K1_BLOCK, FT_N2, c), lambda bi, kb: (bi, kb, 0, 0)),
        out_shape=jax.ShapeDtypeStruct((b, FT_N1, FT_N2, c), bf16),
        compiler_params=_params("parallel", "parallel"),
        name="fourier_stage2",
    )(z, stage2, chan)
    return y.transpose(0, 2, 1, 3).reshape(b, s, c)


def _mem_kv_kernel(mem_ref, g_ref, w_ref, k_ref, v_ref):
    mn = _rms_scale(mem_ref[0]) * g_ref[...]
    kv = jnp.dot(mn.astype(bf16), w_ref[...], preferred_element_type=f32)
    k_ref[0] = kv[:, :MEM_WIDTH].astype(bf16)
    v_ref[0] = kv[:, MEM_WIDTH:].astype(bf16)


def _mem_kv(mem, g_mem, w_kv_bf16):
    b, m, d = mem.shape
    kv_spec = pl.BlockSpec((1, m, MEM_WIDTH), lambda bi: (bi, 0, 0))
    return pl.pallas_call(
        _mem_kv_kernel,
        grid=(b,),
        in_specs=[pl.BlockSpec((1, m, d), lambda bi: (bi, 0, 0)), pl.BlockSpec((1, d), lambda bi: (0, 0)),
                  pl.BlockSpec(w_kv_bf16.shape, lambda bi: (0, 0))],
        out_specs=[kv_spec, kv_spec],
        out_shape=[jax.ShapeDtypeStruct((b, m, MEM_WIDTH), bf16)] * 2,
        compiler_params=_params("parallel"),
        name="mem_kv",
    )(mem, g_mem.reshape(1, d), w_kv_bf16)


def _mix_out_kernel(x_ref, yna_ref, yft_ref, qm_ref, km_ref, vm_ref, ggrp_ref, wout_ref, gffn_ref, rw_ref,
                    rb_ref, x1_ref, h2p_ref, eidx_ref, gate_ref, rank_ref, cnt_ref, carry_ref):
    tm = x_ref.shape[0]

    @pl.when(pl.program_id(0) == 0)
    def _():
        carry_ref[...] = jnp.zeros_like(carry_ref)

    q = qm_ref[...]
    km = km_ref[0]
    vm = vm_ref[0]
    heads = []
    for h in range(MEM_HEADS):
        sl = slice(h * MEM_HEAD_DIM, (h + 1) * MEM_HEAD_DIM)
        s = lax.dot_general(q[:, sl], km[:, sl], (((1,), (1,)), ((), ())), preferred_element_type=f32)
        p = _softmax_rows(s * (MEM_HEAD_DIM ** -0.5))
        heads.append(jnp.dot(p.astype(bf16), vm[:, sl], preferred_element_type=f32))
    ymem = jnp.concatenate(heads, axis=1)

    g = ggrp_ref[...]
    a, c = NA_WIDTH, NA_WIDTH + FT_WIDTH
    y = jnp.concatenate([_rms_scale(yna_ref[...].astype(f32)) * g[:, :a],
                         _rms_scale(yft_ref[...].astype(f32)) * g[:, a:c],
                         _rms_scale(ymem) * g[:, c:]], axis=1)
    x1 = x_ref[...] + jnp.dot(y.astype(bf16), wout_ref[...], preferred_element_type=f32)
    x1_ref[...] = x1
    h2 = _rms_scale(x1) * gffn_ref[...]
    h2p_ref[...] = _pack_bf16_pairs(h2)

    h_hi = h2.astype(bf16)
    h_lo = (h2 - h_hi.astype(f32)).astype(bf16)
    logits = (jnp.dot(h_hi, rw_ref[0], preferred_element_type=f32)
              + jnp.dot(h_hi, rw_ref[1], preferred_element_type=f32)
              + jnp.dot(h_lo, rw_ref[0], preferred_element_type=f32)) + rb_ref[...]
    lane = lax.broadcasted_iota(i32, (tm, N_EXPERTS), 1).astype(f32)
    vals, idxs, sels = [], [], []
    l = logits
    for _ in range(TOP_K):
        m = jnp.max(l, axis=1, keepdims=True)
        idx = jnp.min(jnp.where(l == m, lane, float(N_EXPERTS)), axis=1, keepdims=True)
        sel = lane == idx
        vals.append(m)
        idxs.append(idx)
        sels.append(sel)
        l = jnp.where(sel, -jnp.inf, l)
    ex = [jnp.exp(v - vals[0]) for v in vals]
    den = ex[0] + ex[1] + ex[2] + ex[3]

    onehot = (sels[0] | sels[1] | sels[2] | sels[3]).astype(f32)
    lower = (lax.broadcasted_iota(i32, (tm, tm), 0) > lax.broadcasted_iota(i32, (tm, tm), 1)).astype(bf16)
    before = jnp.dot(lower, onehot.astype(bf16), preferred_element_type=f32) + carry_ref[...]
    ranks = [jnp.sum(jnp.where(sel, before, 0.0), axis=1, keepdims=True) for sel in sels]
    carry_ref[...] = carry_ref[...] + jnp.sum(onehot, axis=0, keepdims=True)
    cnt_ref[...] = carry_ref[...].astype(i32)

    k_lane = lax.broadcasted_iota(i32, (tm, TOP_K), 1)

    def columns(cols):
        out = jnp.broadcast_to(cols[TOP_K - 1], (tm, TOP_K))
        for k in range(TOP_K - 2, -1, -1):
            out = jnp.where(k_lane == k, cols[k], out)
        return out

    eidx_ref[...] = columns(idxs).astype(i32)
    gate_ref[...] = columns([e / den for e in ex])
    rank_ref[...] = columns(ranks).astype(i32)


def _mix_out(x2d, y_na, y_ft, q_mem, k_mem, v_mem, g_grp, w_out_bf16, g_ffn, router_w2, router_b, seq,
             tile0, n_tiles):
    d = x2d.shape[1]
    tm = TOKEN_TILE
    t = n_tiles * tm
    steps_per_batch = seq // tm
    m = k_mem.shape[1]
    row_in = lambda w: pl.BlockSpec((tm, w), lambda i: (i + tile0, 0))
    row_out = lambda w: pl.BlockSpec((tm, w), lambda i: (i, 0))
    full = lambda a: pl.BlockSpec(a.shape, lambda i: (0,) * a.ndim)
    kv_spec = pl.BlockSpec((1, m, MEM_WIDTH), lambda i: ((i + tile0) // steps_per_batch, 0, 0))
    g_grp2, g_ffn2, rb2 = g_grp.reshape(1, -1), g_ffn.reshape(1, d), router_b.reshape(1, N_EXPERTS)
    return pl.pallas_call(
        _mix_out_kernel,
        grid=(n_tiles,),
        in_specs=[row_in(d), row_in(NA_WIDTH), row_in(FT_WIDTH), row_in(MEM_WIDTH), kv_spec, kv_spec,
                  full(g_grp2), full(w_out_bf16), full(g_ffn2), full(router_w2), full(rb2)],
        out_specs=[row_out(d), row_out(d // 2), row_out(TOP_K), row_out(TOP_K), row_out(TOP_K),
                   pl.BlockSpec((1, N_EXPERTS), lambda i: (0, 0))],
        out_shape=[jax.ShapeDtypeStruct((t, d), f32), jax.ShapeDtypeStruct((t, d // 2), u32),
                   jax.ShapeDtypeStruct((t, TOP_K), i32), jax.ShapeDtypeStruct((t, TOP_K), f32),
                   jax.ShapeDtypeStruct((t, TOP_K), i32), jax.ShapeDtypeStruct((1, N_EXPERTS), i32)],
        scratch_shapes=[pltpu.VMEM((1, N_EXPERTS), f32)],
        compiler_params=_params("arbitrary"),
        name="mix_out_router",
    )(x2d, y_na, y_ft, q_mem, k_mem, v_mem, g_grp2, w_out_bf16, g_ffn2, router_w2, rb2)


def _sc_mesh():
    return plsc.VectorSubcoreMesh(core_axis_name="c", subcore_axis_name="s",
                                  num_cores=SC_CORES, num_subcores=SC_SUBCORES)


def _dispatch(h2p, dest, n_slots):
    t, w = h2p.shape
    workers = SC_CORES * SC_SUBCORES
    chunk = SC_GATHER_CHUNK
    per = t // workers
    steps = per // chunk
    assert per * workers == t and steps * chunk == per and steps % 2 == 0
    idx = dest.reshape(workers, steps, chunk, TOP_K).transpose(0, 3, 1, 2)

    def body(h_hbm, idx_hbm, out_hbm, idx_v, rows_v, rsem, ssem):
        wid = lax.axis_index("s") * SC_CORES + lax.axis_index("c")
        base = wid * per
        pltpu.sync_copy(idx_hbm.at[wid], idx_v)

        def read(c, slot):
            return pltpu.make_async_copy(h_hbm.at[pl.ds(base + c * chunk, chunk)], rows_v.at[slot], rsem.at[slot])

        def scatters(c, slot):
            return [pltpu.make_async_copy(rows_v.at[slot], out_hbm.at[idx_v.at[j, c]], ssem.at[slot])
                    for j in range(TOP_K)]

        read(0, 0).start()

        @pl.loop(0, steps, step=2)
        def _(c0):
            for slot in range(2):
                c = c0 + slot
                read(c, slot).wait()
                for cp in scatters(c, slot):
                    cp.start()

                @pl.when(c >= 1)
                def _():
                    for cp in scatters(c - 1, 1 - slot):
                        cp.wait()

                @pl.when(c + 1 < steps)
                def _():
                    read(c + 1, 1 - slot).start()

        for cp in scatters(steps - 1, 1):
            cp.wait()

    return pl.kernel(
        body,
        out_type=jax.ShapeDtypeStruct((n_slots, w), h2p.dtype),
        mesh=_sc_mesh(),
        scratch_types=[pltpu.VMEM((TOP_K, steps, chunk), i32), pltpu.VMEM((2, chunk, w), h2p.dtype),
                       pltpu.SemaphoreType.DMA((2,)), pltpu.SemaphoreType.DMA((2,))],
        name="sc_dispatch_rows",
    )(h2p, idx)


def _expert_kernel(blk_e_ref, blk_cnt_ref, nxt_e_ref, xs_ref, wgu_hbm, bgu_ref, wd_hbm, bd_ref, y_ref,
                   wgu_f32, wd_f32, wgu_bf, wd_bf, sem):
    b = pl.program_id(0)
    e = blk_e_ref[b]
    cnt = blk_cnt_ref[b]
    bm = xs_ref.shape[0]
    de = wd_f32.shape[0]

    def fetch(expert):
        return (pltpu.make_async_copy(wgu_hbm.at[expert], wgu_f32, sem.at[0]),
                pltpu.make_async_copy(wd_hbm.at[expert], wd_f32, sem.at[1]))

    @pl.when(b == 0)
    def _():
        for cp in fetch(e):
            cp.start()

    @pl.when(jnp.logical_or(b == 0, e != blk_e_ref[jnp.maximum(b - 1, 0)]))
    def _():
        for cp in fetch(e):
            cp.wait()

        def convert(i, carry):
            rows = pl.ds(pl.multiple_of(i * WEIGHT_CAST_ROWS, WEIGHT_CAST_ROWS), WEIGHT_CAST_ROWS)
            wgu_bf[rows, :] = wgu_f32[rows, :].astype(bf16)
            wd_bf[rows, :] = wd_f32[rows, :].astype(bf16)
            return carry

        lax.fori_loop(0, wgu_f32.shape[0] // WEIGHT_CAST_ROWS, convert, 0)

        @pl.when(nxt_e_ref[b] >= 0)
        def _():
            for cp in fetch(nxt_e_ref[b]):
                cp.start()

    @pl.when(cnt > 0)
    def _():
        valid = lax.broadcasted_iota(i32, (bm, 1), 0) < cnt
        x = jnp.where(valid, _unpack_bf16_pairs(xs_ref[...]), 0.0).astype(bf16)
        gu = jnp.dot(x, wgu_bf[...], preferred_element_type=f32) + bgu_ref[0]
        x_glu = jnp.minimum(gu[:, :de], SWIGLU_LIMIT)
        x_lin = jnp.clip(gu[:, de:], -SWIGLU_LIMIT, SWIGLU_LIMIT)
        act = x_glu * (1.0 / (1.0 + jnp.exp(-SWIGLU_ALPHA * x_glu))) * (x_lin + 1.0)
        y = jnp.dot(act.astype(bf16), wd_bf[...], preferred_element_type=f32) + bd_ref[0]
        y_ref[...] = _pack_bf16_pairs(y)

    @pl.when(cnt == 0)
    def _():
        y_ref[...] = jnp.zeros_like(y_ref)


def _experts(xs, blk_e, blk_cnt, nxt_e, w_gu, b_gu, w_down, b_down):
    n_slots, w = xs.shape
    bm = MOE_BLOCK
    e, d, de2 = w_gu.shape
    de = w_down.shape[1]
    assert de == d, "one row loop converts both weight matrices"
    grid_spec = pltpu.PrefetchScalarGridSpec(
        num_scalar_prefetch=3,
        grid=(n_slots // bm,),
        in_specs=[pl.BlockSpec((bm, w), lambda b, be, bc, ne: (b, 0)),
                  pl.BlockSpec(memory_space=pl.ANY),
                  pl.BlockSpec((1, 1, de2), lambda b, be, bc, ne: (be[b], 0, 0)),
                  pl.BlockSpec(memory_space=pl.ANY),
                  pl.BlockSpec((1, 1, d), lambda b, be, bc, ne: (be[b], 0, 0))],
        out_specs=pl.BlockSpec((bm, w), lambda b, be, bc, ne: (b, 0)),
        scratch_shapes=[pltpu.VMEM((d, de2), f32), pltpu.VMEM((de, d), f32),
                        pltpu.VMEM((d, de2), bf16), pltpu.VMEM((de, d), bf16),
                        pltpu.SemaphoreType.DMA((2,))],
    )
    return pl.pallas_call(
        _expert_kernel,
        grid_spec=grid_spec,
        out_shape=jax.ShapeDtypeStruct((n_slots, w), u32),
        compiler_params=_params("arbitrary"),
        name="moe_experts",
    )(blk_e, blk_cnt, nxt_e, xs, w_gu, b_gu.reshape(e, 1, de2), w_down, b_down.reshape(e, 1, d))


def _sc_gather_rows(table, idx):
    n, w = idx.shape[0], table.shape[1]
    workers = SC_CORES * SC_SUBCORES
    chunk = SC_GATHER_CHUNK
    per = n // workers
    steps = per // chunk
    assert per * workers == n and steps * chunk == per and steps % 2 == 0

    def body(table_hbm, idx_hbm, out_hbm, idx_v, rows_v, gsem, wsem):
        base = (lax.axis_index("s") * SC_CORES + lax.axis_index("c")) * per
        pltpu.sync_copy(idx_hbm.at[pl.ds(base, per)], idx_v)

        def gather(c, slot):
            return pltpu.make_async_copy(table_hbm.at[idx_v.at[pl.ds(c * chunk, chunk)]], rows_v.at[slot],
                                         gsem.at[slot])

        def write(c, slot):
            return pltpu.make_async_copy(rows_v.at[slot], out_hbm.at[pl.ds(base + c * chunk, chunk)],
                                         wsem.at[slot])

        gather(0, 0).start()

        @pl.loop(0, steps, step=2)
        def _(c0):
            for slot in range(2):
                c = c0 + slot
                gather(c, slot).wait()
                write(c, slot).start()

                @pl.when(c >= 1)
                def _():
                    write(c - 1, 1 - slot).wait()

                @pl.when(c + 1 < steps)
                def _():
                    gather(c + 1, 1 - slot).start()

        write(steps - 1, 1).wait()

    return pl.kernel(
        body,
        out_type=jax.ShapeDtypeStruct((n, w), table.dtype),
        mesh=_sc_mesh(),
        scratch_types=[pltpu.VMEM((per,), i32), pltpu.VMEM((2, chunk, w), table.dtype),
                       pltpu.SemaphoreType.DMA((2,)), pltpu.SemaphoreType.DMA((2,))],
        name="sc_gather_rows",
    )(table, idx)


def _combine_kernel(x1_ref, gate_ref, gfin_ref, yg_ref, *rest):
    o_ref = rest[-1]
    acc = x1_ref[...]
    gates = gate_ref[...]
    for j in range(TOP_K):
        acc = acc + gates[:, j:j + 1] * _unpack_bf16_pairs(yg_ref[j])
    o_ref[...] = _rms_scale(acc) * gfin_ref[...]


def _combine(x1, gates, dest, y_slots, g_final, out_prev, tile0, total_tokens):
    t, d = x1.shape
    tm = TOKEN_TILE
    w = y_slots.shape[1]
    yg = _sc_gather_rows(y_slots, dest.reshape(t, TOP_K).T.reshape(-1)).reshape(TOP_K, t, w)
    in_specs = [pl.BlockSpec((tm, d), lambda i: (i, 0)),
                pl.BlockSpec((tm, TOP_K), lambda i: (i, 0)),
                pl.BlockSpec((1, d), lambda i: (0, 0)),
                pl.BlockSpec((TOP_K, tm, w), lambda i: (0, i, 0))]
    args = [x1, gates, g_final.reshape(1, d), yg]
    aliases = {}
    if out_prev is not None:
        in_specs.append(pl.BlockSpec(memory_space=pl.ANY))
        args.append(out_prev)
        aliases = {len(args) - 1: 0}
    return pl.pallas_call(
        _combine_kernel,
        grid=(t // tm,),
        in_specs=in_specs,
        out_specs=pl.BlockSpec((tm, d), lambda i: (i + tile0, 0)),
        out_shape=jax.ShapeDtypeStruct((total_tokens, d), f32),
        input_output_aliases=aliases,
        compiler_params=_params("parallel"),
        name="moe_combine",
    )(*args)


def _slot_layout(counts, eidx, rank, n_blocks):
    bm = MOE_BLOCK
    padded = (counts + bm - 1) // bm * bm
    padded_end = jnp.cumsum(padded)
    start = padded_end - padded
    experts = jnp.arange(N_EXPERTS, dtype=i32)
    lookup = lambda table, idx: jnp.sum(jnp.where(idx[..., None] == experts, table, 0), axis=-1)
    dest = (lookup(start, eidx) + rank).reshape(-1).astype(i32)
    blk_row = jnp.arange(n_blocks, dtype=i32) * bm
    blk_e = jnp.minimum(jnp.sum((padded_end[None, :] <= blk_row[:, None]).astype(i32), axis=1), N_EXPERTS - 1)
    blk_cnt = jnp.clip(lookup(counts, blk_e) - (blk_row - lookup(start, blk_e)), 0, bm).astype(i32)
    none = jnp.int32(N_EXPERTS)
    nxt_e = jnp.min(jnp.where(blk_e[None, :] > blk_e[:, None], blk_e[None, :], none), axis=1)
    nxt_e = jnp.where(nxt_e == none, -1, nxt_e).astype(i32)
    return dest, blk_e, blk_cnt, nxt_e


def _layer_and_final_norm(x2d, mem, seq, g_mix, g_mem, w_in, w_mem_kv, na_rel_bias, g_grp, w_out, g_ffn,
                          router_w, router_b, w_gu, b_gu, w_down, b_down, g_final):
    t, d = x2d.shape
    b = t // seq
    q_na, k_na, v_na, u_ft, q_mem = _in_proj(x2d, g_mix, w_in.astype(bf16))
    k_mem, v_mem = _mem_kv(mem, g_mem, w_mem_kv.astype(bf16))
    shape3 = lambda a: a.reshape(b, seq, a.shape[-1])
    y_na = _neighbourhood_attention(shape3(q_na), shape3(k_na), shape3(v_na), _na_bias_table(na_rel_bias))
    y_ft = _fourier_mix(shape3(u_ft), _ft_tables(seq))
    y_na, y_ft = y_na.reshape(t, -1), y_ft.reshape(t, -1)
    w_out_bf16 = w_out.astype(bf16)
    rw_hi = router_w.astype(bf16)
    router_w2 = jnp.stack([rw_hi, (router_w - rw_hi.astype(f32)).astype(bf16)])

    tiles = t // TOKEN_TILE
    assert tiles % MOE_TOKEN_GROUPS == 0
    group_tiles = tiles // MOE_TOKEN_GROUPS
    group_tokens = group_tiles * TOKEN_TILE
    n_blocks = (group_tokens * TOP_K) // MOE_BLOCK + N_EXPERTS
    out = None
    for grp in range(MOE_TOKEN_GROUPS):
        tile0 = grp * group_tiles
        x1, h2p, eidx, gates, rank, counts = _mix_out(
            x2d, y_na, y_ft, q_mem, k_mem, v_mem, g_grp, w_out_bf16, g_ffn, router_w2, router_b, seq,
            tile0, group_tiles)
        dest, blk_e, blk_cnt, nxt_e = _slot_layout(counts[0], eidx, rank, n_blocks)
        xs = _dispatch(h2p, dest, n_blocks * MOE_BLOCK)
        y_slots = _experts(xs, blk_e, blk_cnt, nxt_e, w_gu, b_gu, w_down, b_down)
        out = _combine(x1, gates, dest, y_slots, g_final, out, tile0, t)
    return out


def kernel(x, mem, g_mix, g_mem, w_in, w_mem_kv, na_rel_bias, g_grp, w_out, g_ffn, router_w, router_b,
           w_gu, b_gu, w_down, b_down, g_final):
    b, seq, d = x.shape
    depth = w_in.shape[0]
    assert depth == 1, "the final norm is fused into the single layer's combine step"
    out = _layer_and_final_norm(
        x.reshape(b * seq, d), mem, seq, g_mix[0], g_mem[0], w_in[0], w_mem_kv[0], na_rel_bias[0], g_grp[0],
        w_out[0], g_ffn[0], router_w[0], router_b[0], w_gu[0], b_gu[0], w_down[0], b_down[0], g_final)
    return out.reshape(b, seq, d)
```

```python
import functools

import numpy as np
import jax
import jax.numpy as jnp
from jax import lax
from jax.experimental import pallas as pl
from jax.experimental.pallas import tpu as pltpu
from jax.experimental.pallas import tpu_sc as plsc

f32 = jnp.float32
bf16 = jnp.bfloat16
u32 = jnp.uint32
i32 = jnp.int32

GRID_W = 64
NA_HEADS = 8
NA_HEAD_DIM = 64
NA_WIN_ROWS = 8
NA_WIN_COLS = 16
FT_GROUPS = 4
FT_GROUP_DIM = 128
MEM_HEADS = 4
MEM_HEAD_DIM = 128
NA_WIDTH = NA_HEADS * NA_HEAD_DIM
FT_WIDTH = FT_GROUPS * FT_GROUP_DIM
MEM_WIDTH = MEM_HEADS * MEM_HEAD_DIM
N_EXPERTS = 32
TOP_K = 4
SWIGLU_LIMIT = 7.0
SWIGLU_ALPHA = 1.702
EPS = 1e-6

LANES = 128
SUBLANES = 8
VMEM_LIMIT_BYTES = 56 * 1024 * 1024
SC_CORES = 2
SC_SUBCORES = 16
SC_GATHER_CHUNK = 64

TOKEN_TILE = 512
MOE_BLOCK = 512
WEIGHT_CAST_ROWS = 128
MOE_TOKEN_GROUPS = 2
NA_ROW_UNROLL = 8
FT_N1 = 64
FT_N2 = 128
FT_K1_BLOCK = 8
MASK_VALUE = -1e30


def _params(*semantics):
    return pltpu.CompilerParams(dimension_semantics=semantics, vmem_limit_bytes=VMEM_LIMIT_BYTES)


def _rms_scale(x):
    return x * lax.rsqrt(jnp.mean(x * x, axis=-1, keepdims=True) + EPS)


def _softmax_rows(s):
    p = jnp.exp(s - jnp.max(s, axis=-1, keepdims=True))
    return p / jnp.sum(p, axis=-1, keepdims=True)


def _bf16_bits(x):
    return pltpu.bitcast(x.astype(bf16).astype(f32), u32)


def _pack2(lo, hi):
    return (_bf16_bits(lo) >> 16) | (_bf16_bits(hi) & jnp.uint32(0xFFFF0000))


def _unpack2(w):
    return pltpu.bitcast(w << 16, f32), pltpu.bitcast(w & jnp.uint32(0xFFFF0000), f32)


def _pack_bf16_pairs(x):
    n = x.shape[1] // 2
    return _pack2(x[:, :n], x[:, n:])


def _unpack_bf16_pairs(w):
    return jnp.concatenate(_unpack2(w), axis=1)


def _in_proj_kernel(x_ref, g_ref, w_ref, qna_ref, kna_ref, vna_ref, uft_ref, qmem_ref):
    h = _rms_scale(x_ref[...]) * g_ref[...]
    proj = jnp.dot(h.astype(bf16), w_ref[...], preferred_element_type=f32)
    o = NA_WIDTH
    qna_ref[...] = (proj[:, :o] * (NA_HEAD_DIM ** -0.5)).astype(bf16)
    kna_ref[...] = proj[:, o:2 * o].astype(bf16)
    vna_ref[...] = proj[:, 2 * o:3 * o].astype(bf16)
    uft_ref[...] = _pack_bf16_pairs(proj[:, 3 * o:3 * o + FT_WIDTH])
    qmem_ref[...] = proj[:, 3 * o + FT_WIDTH:].astype(bf16)


def _in_proj(x2d, g_mix, w_in_bf16):
    t, d = x2d.shape
    tm = TOKEN_TILE
    row = lambda w: pl.BlockSpec((tm, w), lambda i: (i, 0))
    return pl.pallas_call(
        _in_proj_kernel,
        grid=(t // tm,),
        in_specs=[row(d), pl.BlockSpec((1, d), lambda i: (0, 0)),
                  pl.BlockSpec(w_in_bf16.shape, lambda i: (0, 0))],
        out_specs=[row(NA_WIDTH), row(NA_WIDTH), row(NA_WIDTH), row(FT_WIDTH // 2), row(MEM_WIDTH)],
        out_shape=[jax.ShapeDtypeStruct((t, NA_WIDTH), bf16)] * 3
        + [jax.ShapeDtypeStruct((t, FT_WIDTH // 2), u32), jax.ShapeDtypeStruct((t, MEM_WIDTH), bf16)],
        compiler_params=_params("parallel"),
        name="in_proj",
    )(x2d, g_mix.reshape(1, d), w_in_bf16)


def _na_bias_table(rel_bias):
    s = np.arange(NA_WIN_ROWS)[:, None]
    j = np.arange(NA_WIN_ROWS)[None, :]
    dr_idx = (j - s) + (NA_WIN_ROWS - 1)
    c = np.arange(GRID_W)
    dc_idx = np.clip(c[None, :] - c[:, None], -(NA_WIN_COLS - 1), NA_WIN_COLS - 1) + (NA_WIN_COLS - 1)
    col_start = np.clip(c - NA_WIN_COLS // 2, 0, GRID_W - NA_WIN_COLS)
    col_in = (c[None, :] >= col_start[:, None]) & (c[None, :] < col_start[:, None] + NA_WIN_COLS)
    pick_r = jnp.asarray(dr_idx[:, :, None] == np.arange(2 * NA_WIN_ROWS - 1), f32)
    pick_c = jnp.asarray(dc_idx[:, :, None] == np.arange(2 * NA_WIN_COLS - 1), f32)
    tab = jnp.einsum("hab,sja,qcb->hsqjc", rel_bias.astype(f32), pick_r, pick_c, precision=lax.Precision.HIGHEST)
    tab = jnp.where(col_in[None, None, :, None, :], tab, MASK_VALUE)
    tab = tab.reshape(NA_HEADS // 2, 2, NA_WIN_ROWS, GRID_W, NA_WIN_ROWS * GRID_W)
    return tab.transpose(2, 0, 1, 3, 4).reshape(NA_WIN_ROWS, NA_HEADS // 2, 2 * GRID_W, NA_WIN_ROWS * GRID_W)


def _na_kernel(q_ref, k_ref, v_ref, bias_ref, o_ref):
    rows = q_ref.shape[1] // GRID_W
    win = NA_WIN_ROWS * GRID_W
    first_head = lax.broadcasted_iota(i32, (GRID_W, 2 * NA_HEAD_DIM), 1) < NA_HEAD_DIM

    def body(it, carry):
        scores, values, q_offsets = [], [], []
        for u in range(NA_ROW_UNROLL):
            r = it * NA_ROW_UNROLL + u
            row_start = jnp.clip(r - NA_WIN_ROWS // 2, 0, rows - NA_WIN_ROWS)
            q0 = pl.multiple_of(r * GRID_W, GRID_W)
            k0 = pl.multiple_of(row_start * GRID_W, GRID_W)
            q = q_ref[0, pl.ds(q0, GRID_W), :]
            zero = jnp.zeros_like(q)
            qm = jnp.concatenate([jnp.where(first_head, q, zero), jnp.where(first_head, zero, q)], axis=0)
            s = lax.dot_general(qm, k_ref[0, pl.ds(k0, win), :], (((1,), (1,)), ((), ())),
                                preferred_element_type=f32)
            scores.append(s + bias_ref[r - row_start, 0])
            values.append(v_ref[0, pl.ds(k0, win), :])
            q_offsets.append(q0)
        s = jnp.concatenate(scores, axis=0)
        p = jnp.exp(s - jnp.max(s, axis=-1, keepdims=True))
        inv_den = 1.0 / jnp.sum(p, axis=-1, keepdims=True)
        p = p.astype(bf16)
        for u in range(NA_ROW_UNROLL):
            sl = slice(u * 2 * GRID_W, (u + 1) * 2 * GRID_W)
            o = jnp.dot(p[sl], values[u], preferred_element_type=f32) * inv_den[sl]
            o_ref[0, pl.ds(q_offsets[u], GRID_W), :] = jnp.where(
                first_head, o[:GRID_W], o[GRID_W:]).astype(o_ref.dtype)
        return carry

    lax.fori_loop(0, rows // NA_ROW_UNROLL, body, 0)


def _neighbourhood_attention(q, k, v, bias_tab):
    b, s, _ = q.shape
    pair = 2 * NA_HEAD_DIM
    qkv_spec = pl.BlockSpec((1, s, pair), lambda bi, hp: (bi, 0, hp))
    return pl.pallas_call(
        _na_kernel,
        grid=(b, NA_HEADS // 2),
        in_specs=[qkv_spec, qkv_spec, qkv_spec,
                  pl.BlockSpec((NA_WIN_ROWS, 1, 2 * GRID_W, NA_WIN_ROWS * GRID_W), lambda bi, hp: (0, hp, 0, 0))],
        out_specs=qkv_spec,
        out_shape=jax.ShapeDtypeStruct((b, s, NA_WIDTH), bf16),
        compiler_params=_params("parallel", "parallel"),
        name="neighbourhood_attention",
    )(q, k, v, bias_tab)


def _ft_tables(seq):
    assert seq == FT_N1 * FT_N2
    n_blk = FT_N2 // SUBLANES
    k1 = np.arange(FT_N1)[:, None, None, None]
    sr = np.arange(SUBLANES)[None, :, None, None]
    n1 = np.arange(FT_N1)[None, None, :, None]
    sc = np.arange(SUBLANES)[None, None, None, :]
    stage1 = np.zeros((n_blk, 2, FT_N1, SUBLANES, FT_N1, SUBLANES), np.float64)
    for blk in range(n_blk):
        n = FT_N2 * n1 + SUBLANES * blk + sr
        ang = 2.0 * np.pi * ((k1 * n) % seq) / seq
        eye = (sr == sc)
        stage1[blk, 0] = np.cos(ang) * eye
        stage1[blk, 1] = -np.sin(ang) * eye
    stage1 = stage1.reshape(n_blk, 2 * FT_N1 * SUBLANES, FT_N1 * SUBLANES)
    a = np.arange(FT_N2)
    ang2 = 2.0 * np.pi * ((a[:, None] * a[None, :]) % FT_N2) / FT_N2
    c2, s2 = np.cos(ang2), np.sin(ang2)
    stage2 = np.block([[c2, s2], [-s2, c2]])
    g = np.arange(FT_GROUP_DIM)
    angc = 2.0 * np.pi * ((g[:, None] * g[None, :]) % FT_GROUP_DIM) / FT_GROUP_DIM
    norm = 1.0 / np.sqrt(seq * FT_GROUP_DIM)
    chan = np.concatenate([np.cos(angc), np.sin(angc)], axis=0) * norm
    return (jnp.asarray(stage1, bf16), jnp.asarray(stage2, bf16), jnp.asarray(chan, bf16))


def _ft_stage1_kernel(u_ref, m_ref, z_ref):
    rows = FT_N1 * SUBLANES
    u = _unpack_bf16_pairs(u_ref[0].reshape(rows, FT_WIDTH // 2)).astype(bf16)
    z = jnp.dot(m_ref[0], u, preferred_element_type=f32)
    z_ref[0] = _pack2(z[:rows], z[rows:]).reshape(FT_N1, SUBLANES, FT_WIDTH)


def _ft_stage2_kernel(z_ref, s2_ref, cs_ref, y_ref):
    gd = FT_GROUP_DIM
    for kk in range(FT_K1_BLOCK):
        zz = jnp.concatenate(_unpack2(z_ref[0, kk]), axis=0).astype(bf16)
        x = jnp.dot(s2_ref[...], zz, preferred_element_type=f32)
        outs = []
        for g in range(FT_GROUPS):
            xg = jnp.concatenate([x[:FT_N2, g * gd:(g + 1) * gd], x[FT_N2:, g * gd:(g + 1) * gd]], axis=1)
            outs.append(jnp.dot(xg.astype(bf16), cs_ref[...], preferred_element_type=f32))
        y_ref[0, kk] = jnp.concatenate(outs, axis=1).astype(y_ref.dtype)


def _fourier_mix(u_packed, tables):
    u = u_packed
    b, s, _ = u.shape
    c = FT_WIDTH
    stage1, stage2, chan = tables
    n_blk = FT_N2 // SUBLANES
    z = pl.pallas_call(
        _ft_stage1_kernel,
        grid=(n_blk, b),
        in_specs=[pl.BlockSpec((1, FT_N1, SUBLANES, c // 2), lambda j, bi: (bi, 0, j, 0)),
                  pl.BlockSpec((1,) + stage1.shape[1:], lambda j, bi: (j, 0, 0))],
        out_specs=pl.BlockSpec((1, FT_N1, SUBLANES, c), lambda j, bi: (bi, 0, j, 0)),
        out_shape=jax.ShapeDtypeStruct((b, FT_N1, FT_N2, c), u32),
        compiler_params=_params("parallel", "parallel"),
        name="fourier_stage1",
    )(u.reshape(b, FT_N1, FT_N2, c // 2), stage1)
    y = pl.pallas_call(
        _ft_stage2_kernel,
        grid=(b, FT_N1 // FT_K1_BLOCK),
        in_specs=[pl.BlockSpec((1, FT_K1_BLOCK, FT_N2, c), lambda bi, kb: (bi, kb, 0, 0)),
                  pl.BlockSpec(stage2.shape, lambda bi, kb: (0, 0)),
                  pl.BlockSpec(chan.shape, lambda bi, kb: (0, 0))],
        out_specs=pl.BlockSpec((1, FT_K1_BLOCK, FT_N2, c), lambda bi, kb: (bi, kb, 0, 0)),
        out_shape=jax.ShapeDtypeStruct((b, FT_N1, FT_N2, c), bf16),
        compiler_params=_params("parallel", "parallel"),
        name="fourier_stage2",
    )(z, stage2, chan)
    return y.transpose(0, 2, 1, 3).reshape(b, s, c)


def _mem_kv_kernel(mem_ref, g_ref, w_ref, k_ref, v_ref):
    mn = _rms_scale(mem_ref[0]) * g_ref[...]
    kv = jnp.dot(mn.astype(bf16), w_ref[...], preferred_element_type=f32)
    k_ref[0] = kv[:, :MEM_WIDTH].astype(bf16)
    v_ref[0] = kv[:, MEM_WIDTH:].astype(bf16)


def _mem_kv(mem, g_mem, w_kv_bf16):
    b, m, d = mem.shape
    kv_spec = pl.BlockSpec((1, m, MEM_WIDTH), lambda bi: (bi, 0, 0))
    return pl.pallas_call(
        _mem_kv_kernel,
        grid=(b,),
        in_specs=[pl.BlockSpec((1, m, d), lambda bi: (bi, 0, 0)), pl.BlockSpec((1, d), lambda bi: (0, 0)),
                  pl.BlockSpec(w_kv_bf16.shape, lambda bi: (0, 0))],
        out_specs=[kv_spec, kv_spec],
        out_shape=[jax.ShapeDtypeStruct((b, m, MEM_WIDTH), bf16)] * 2,
        compiler_params=_params("parallel"),
        name="mem_kv",
    )(mem, g_mem.reshape(1, d), w_kv_bf16)


def _mix_out_kernel(x_ref, yna_ref, yft_ref, qm_ref, km_ref, vm_ref, ggrp_ref, wout_ref, gffn_ref, rw_ref,
                    rb_ref, x1_ref, h2p_ref, eidx_ref, gate_ref, rank_ref, cnt_ref, carry_ref):
    tm = x_ref.shape[0]

    @pl.when(pl.program_id(0) == 0)
    def _():
        carry_ref[...] = jnp.zeros_like(carry_ref)

    q = qm_ref[...]
    km = km_ref[0]
    vm = vm_ref[0]
    heads = []
    for h in range(MEM_HEADS):
        sl = slice(h * MEM_HEAD_DIM, (h + 1) * MEM_HEAD_DIM)
        s = lax.dot_general(q[:, sl], km[:, sl], (((1,), (1,)), ((), ())), preferred_element_type=f32)
        p = _softmax_rows(s * (MEM_HEAD_DIM ** -0.5))
        heads.append(jnp.dot(p.astype(bf16), vm[:, sl], preferred_element_type=f32))
    ymem = jnp.concatenate(heads, axis=1)

    g = ggrp_ref[...]
    a, c = NA_WIDTH, NA_WIDTH + FT_WIDTH
    y = jnp.concatenate([_rms_scale(yna_ref[...].astype(f32)) * g[:, :a],
                         _rms_scale(yft_ref[...].astype(f32)) * g[:, a:c],
                         _rms_scale(ymem) * g[:, c:]], axis=1)
    x1 = x_ref[...] + jnp.dot(y.astype(bf16), wout_ref[...], preferred_element_type=f32)
    x1_ref[...] = x1
    h2 = _rms_scale(x1) * gffn_ref[...]
    h2p_ref[...] = _pack_bf16_pairs(h2)

    h_hi = h2.astype(bf16)
    h_lo = (h2 - h_hi.astype(f32)).astype(bf16)
    logits = (jnp.dot(h_hi, rw_ref[0], preferred_element_type=f32)
              + jnp.dot(h_hi, rw_ref[1], preferred_element_type=f32)
              + jnp.dot(h_lo, rw_ref[0], preferred_element_type=f32)) + rb_ref[...]
    lane = lax.broadcasted_iota(i32, (tm, N_EXPERTS), 1).astype(f32)
    vals, idxs, sels = [], [], []
    l = logits
    for _ in range(TOP_K):
        m = jnp.max(l, axis=1, keepdims=True)
        idx = jnp.min(jnp.where(l == m, lane, float(N_EXPERTS)), axis=1, keepdims=True)
        sel = lane == idx
        vals.append(m)
        idxs.append(idx)
        sels.append(sel)
        l = jnp.where(sel, -jnp.inf, l)
    ex = [jnp.exp(v - vals[0]) for v in vals]
    den = ex[0] + ex[1] + ex[2] + ex[3]

    onehot = (sels[0] | sels[1] | sels[2] | sels[3]).astype(f32)
    lower = (lax.broadcasted_iota(i32, (tm, tm), 0) > lax.broadcasted_iota(i32, (tm, tm), 1)).astype(bf16)
    before = jnp.dot(lower, onehot.astype(bf16), preferred_element_type=f32) + carry_ref[...]
    ranks = [jnp.sum(jnp.where(sel, before, 0.0), axis=1, keepdims=True) for sel in sels]
    carry_ref[...] = carry_ref[...] + jnp.sum(onehot, axis=0, keepdims=True)
    cnt_ref[...] = carry_ref[...].astype(i32)

    k_lane = lax.broadcasted_iota(i32, (tm, TOP_K), 1)

    def columns(cols):
        out = jnp.broadcast_to(cols[TOP_K - 1], (tm, TOP_K))
        for k in range(TOP_K - 2, -1, -1):
            out = jnp.where(k_lane == k, cols[k], out)
        return out

    eidx_ref[...] = columns(idxs).astype(i32)
    gate_ref[...] = columns([e / den for e in ex])
    rank_ref[...] = columns(ranks).astype(i32)


def _mix_out(x2d, y_na, y_ft, q_mem, k_mem, v_mem, g_grp, w_out_bf16, g_ffn, router_w2, router_b, seq,
             tile0, n_tiles):
    d = x2d.shape[1]
    tm = TOKEN_TILE
    t = n_tiles * tm
    steps_per_batch = seq // tm
    m = k_mem.shape[1]
    row_in = lambda w: pl.BlockSpec((tm, w), lambda i: (i + tile0, 0))
    row_out = lambda w: pl.BlockSpec((tm, w), lambda i: (i, 0))
    full = lambda a: pl.BlockSpec(a.shape, lambda i: (0,) * a.ndim)
    kv_spec = pl.BlockSpec((1, m, MEM_WIDTH), lambda i: ((i + tile0) // steps_per_batch, 0, 0))
    g_grp2, g_ffn2, rb2 = g_grp.reshape(1, -1), g_ffn.reshape(1, d), router_b.reshape(1, N_EXPERTS)
    return pl.pallas_call(
        _mix_out_kernel,
        grid=(n_tiles,),
        in_specs=[row_in(d), row_in(NA_WIDTH), row_in(FT_WIDTH), row_in(MEM_WIDTH), kv_spec, kv_spec,
                  full(g_grp2), full(w_out_bf16), full(g_ffn2), full(router_w2), full(rb2)],
        out_specs=[row_out(d), row_out(d // 2), row_out(TOP_K), row_out(TOP_K), row_out(TOP_K),
                   pl.BlockSpec((1, N_EXPERTS), lambda i: (0, 0))],
        out_shape=[jax.ShapeDtypeStruct((t, d), f32), jax.ShapeDtypeStruct((t, d // 2), u32),
                   jax.ShapeDtypeStruct((t, TOP_K), i32), jax.ShapeDtypeStruct((t, TOP_K), f32),
                   jax.ShapeDtypeStruct((t, TOP_K), i32), jax.ShapeDtypeStruct((1, N_EXPERTS), i32)],
        scratch_shapes=[pltpu.VMEM((1, N_EXPERTS), f32)],
        compiler_params=_params("arbitrary"),
        name="mix_out_router",
    )(x2d, y_na, y_ft, q_mem, k_mem, v_mem, g_grp2, w_out_bf16, g_ffn2, router_w2, rb2)


def _sc_mesh():
    return plsc.VectorSubcoreMesh(core_axis_name="c", subcore_axis_name="s",
                                  num_cores=SC_CORES, num_subcores=SC_SUBCORES)


def _dispatch(h2p, dest, n_slots):
    t, w = h2p.shape
    workers = SC_CORES * SC_SUBCORES
    chunk = SC_GATHER_CHUNK
    per = t // workers
    steps = per // chunk
    assert per * workers == t and steps * chunk == per and steps % 2 == 0
    idx = dest.reshape(workers, steps, chunk, TOP_K).transpose(0, 3, 1, 2)

    def body(h_hbm, idx_hbm, out_hbm, idx_v, rows_v, rsem, ssem):
        wid = lax.axis_index("s") * SC_CORES + lax.axis_index("c")
        base = wid * per
        pltpu.sync_copy(idx_hbm.at[wid], idx_v)

        def read(c, slot):
            return pltpu.make_async_copy(h_hbm.at[pl.ds(base + c * chunk, chunk)], rows_v.at[slot], rsem.at[slot])

        def scatters(c, slot):
            return [pltpu.make_async_copy(rows_v.at[slot], out_hbm.at[idx_v.at[j, c]], ssem.at[slot])
                    for j in range(TOP_K)]

        read(0, 0).start()

        @pl.loop(0, steps, step=2)
        def _(c0):
            for slot in range(2):
                c = c0 + slot
                read(c, slot).wait()
                for cp in scatters(c, slot):
                    cp.start()

                @pl.when(c >= 1)
                def _():
                    for cp in scatters(c - 1, 1 - slot):
                        cp.wait()

                @pl.when(c + 1 < steps)
                def _():
                    read(c + 1, 1 - slot).start()

        for cp in scatters(steps - 1, 1):
            cp.wait()

    return pl.kernel(
        body,
        out_type=jax.ShapeDtypeStruct((n_slots, w), h2p.dtype),
        mesh=_sc_mesh(),
        scratch_types=[pltpu.VMEM((TOP_K, steps, chunk), i32), pltpu.VMEM((2, chunk, w), h2p.dtype),
                       pltpu.SemaphoreType.DMA((2,)), pltpu.SemaphoreType.DMA((2,))],
        name="sc_dispatch_rows",
    )(h2p, idx)


def _expert_kernel(blk_e_ref, blk_cnt_ref, nxt_e_ref, xs_ref, wgu_hbm, bgu_ref, wd_hbm, bd_ref, y_ref,
                   wgu_f32, wd_f32, wgu_bf, wd_bf, sem):
    b = pl.program_id(0)
    e = blk_e_ref[b]
    cnt = blk_cnt_ref[b]
    bm = xs_ref.shape[0]
    de = wd_f32.shape[0]

    def fetch(expert):
        return (pltpu.make_async_copy(wgu_hbm.at[expert], wgu_f32, sem.at[0]),
                pltpu.make_async_copy(wd_hbm.at[expert], wd_f32, sem.at[1]))

    @pl.when(b == 0)
    def _():
        for cp in fetch(e):
            cp.start()

    @pl.when(jnp.logical_or(b == 0, e != blk_e_ref[jnp.maximum(b - 1, 0)]))
    def _():
        for cp in fetch(e):
            cp.wait()

        def convert(i, carry):
            rows = pl.ds(pl.multiple_of(i * WEIGHT_CAST_ROWS, WEIGHT_CAST_ROWS), WEIGHT_CAST_ROWS)
            wgu_bf[rows, :] = wgu_f32[rows, :].astype(bf16)
            wd_bf[rows, :] = wd_f32[rows, :].astype(bf16)
            return carry

        lax.fori_loop(0, wgu_f32.shape[0] // WEIGHT_CAST_ROWS, convert, 0)

        @pl.when(nxt_e_ref[b] >= 0)
        def _():
            for cp in fetch(nxt_e_ref[b]):
                cp.start()

    @pl.when(cnt > 0)
    def _():
        valid = lax.broadcasted_iota(i32, (bm, 1), 0) < cnt
        x = jnp.where(valid, _unpack_bf16_pairs(xs_ref[...]), 0.0).astype(bf16)
        gu = jnp.dot(x, wgu_bf[...], preferred_element_type=f32) + bgu_ref[0]
        x_glu = jnp.minimum(gu[:, :de], SWIGLU_LIMIT)
        x_lin = jnp.clip(gu[:, de:], -SWIGLU_LIMIT, SWIGLU_LIMIT)
        act = x_glu * (1.0 / (1.0 + jnp.exp(-SWIGLU_ALPHA * x_glu))) * (x_lin + 1.0)
        y = jnp.dot(act.astype(bf16), wd_bf[...], preferred_element_type=f32) + bd_ref[0]
        y_ref[...] = _pack_bf16_pairs(y)

    @pl.when(cnt == 0)
    def _():
        y_ref[...] = jnp.zeros_like(y_ref)


def _experts(xs, blk_e, blk_cnt, nxt_e, w_gu, b_gu, w_down, b_down):
    n_slots, w = xs.shape
    bm = MOE_BLOCK
    e, d, de2 = w_gu.shape
    de = w_down.shape[1]
    assert de == d, "one row loop converts both weight matrices"
    grid_spec = pltpu.PrefetchScalarGridSpec(
        num_scalar_prefetch=3,
        grid=(n_slots // bm,),
        in_specs=[pl.BlockSpec((bm, w), lambda b, be, bc, ne: (b, 0)),
                  pl.BlockSpec(memory_space=pl.ANY),
                  pl.BlockSpec((1, 1, de2), lambda b, be, bc, ne: (be[b], 0, 0)),
                  pl.BlockSpec(memory_space=pl.ANY),
                  pl.BlockSpec((1, 1, d), lambda b, be, bc, ne: (be[b], 0, 0))],
        out_specs=pl.BlockSpec((bm, w), lambda b, be, bc, ne: (b, 0)),
        scratch_shapes=[pltpu.VMEM((d, de2), f32), pltpu.VMEM((de, d), f32),
                        pltpu.VMEM((d, de2), bf16), pltpu.VMEM((de, d), bf16),
                        pltpu.SemaphoreType.DMA((2,))],
    )
    return pl.pallas_call(
        _expert_kernel,
        grid_spec=grid_spec,
        out_shape=jax.ShapeDtypeStruct((n_slots, w), u32),
        compiler_params=_params("arbitrary"),
        name="moe_experts",
    )(blk_e, blk_cnt, nxt_e, xs, w_gu, b_gu.reshape(e, 1, de2), w_down, b_down.reshape(e, 1, d))


def _sc_gather_rows(table, idx):
    n, w = idx.shape[0], table.shape[1]
    workers = SC_CORES * SC_SUBCORES
    chunk = SC_GATHER_CHUNK
    per = n // workers
    steps = per // chunk
    assert per * workers == n and steps * chunk == per and steps % 2 == 0

    def body(table_hbm, idx_hbm, out_hbm, idx_v, rows_v, gsem, wsem):
        base = (lax.axis_index("s") * SC_CORES + lax.axis_index("c")) * per
        pltpu.sync_copy(idx_hbm.at[pl.ds(base, per)], idx_v)

        def gather(c, slot):
            return pltpu.make_async_copy(table_hbm.at[idx_v.at[pl.ds(c * chunk, chunk)]], rows_v.at[slot],
                                         gsem.at[slot])

        def write(c, slot):
            return pltpu.make_async_copy(rows_v.at[slot], out_hbm.at[pl.ds(base + c * chunk, chunk)],
                                         wsem.at[slot])

        gather(0, 0).start()

        @pl.loop(0, steps, step=2)
        def _(c0):
            for slot in range(2):
                c = c0 + slot
                gather(c, slot).wait()
                write(c, slot).start()

                @pl.when(c >= 1)
                def _():
                    write(c - 1, 1 - slot).wait()

                @pl.when(c + 1 < steps)
                def _():
                    gather(c + 1, 1 - slot).start()

        write(steps - 1, 1).wait()

    return pl.kernel(
        body,
        out_type=jax.ShapeDtypeStruct((n, w), table.dtype),
        mesh=_sc_mesh(),
        scratch_types=[pltpu.VMEM((per,), i32), pltpu.VMEM((2, chunk, w), table.dtype),
                       pltpu.SemaphoreType.DMA((2,)), pltpu.SemaphoreType.DMA((2,))],
        name="sc_gather_rows",
    )(table, idx)


def _combine_kernel(x1_ref, gate_ref, gfin_ref, yg_ref, *rest):
    o_ref = rest[-1]
    acc = x1_ref[...]
    gates = gate_ref[...]
    for j in range(TOP_K):
        acc = acc + gates[:, j:j + 1] * _unpack_bf16_pairs(yg_ref[j])
    o_ref[...] = _rms_scale(acc) * gfin_ref[...]


def _combine(x1, gates, dest, y_slots, g_final, out_prev, tile0, total_tokens):
    t, d = x1.shape
    tm = TOKEN_TILE
    w = y_slots.shape[1]
    yg = _sc_gather_rows(y_slots, dest.reshape(t, TOP_K).T.reshape(-1)).reshape(TOP_K, t, w)
    in_specs = [pl.BlockSpec((tm, d), lambda i: (i, 0)),
                pl.BlockSpec((tm, TOP_K), lambda i: (i, 0)),
                pl.BlockSpec((1, d), lambda i: (0, 0)),
                pl.BlockSpec((TOP_K, tm, w), lambda i: (0, i, 0))]
    args = [x1, gates, g_final.reshape(1, d), yg]
    aliases = {}
    if out_prev is not None:
        in_specs.append(pl.BlockSpec(memory_space=pl.ANY))
        args.append(out_prev)
        aliases = {len(args) - 1: 0}
    return pl.pallas_call(
        _combine_kernel,
        grid=(t // tm,),
        in_specs=in_specs,
        out_specs=pl.BlockSpec((tm, d), lambda i: (i + tile0, 0)),
        out_shape=jax.ShapeDtypeStruct((total_tokens, d), f32),
        input_output_aliases=aliases,
        compiler_params=_params("parallel"),
        name="moe_combine",
    )(*args)


def _slot_layout(counts, eidx, rank, n_blocks):
    bm = MOE_BLOCK
    padded = (counts + bm - 1) // bm * bm
    padded_end = jnp.cumsum(padded)
    start = padded_end - padded
    experts = jnp.arange(N_EXPERTS, dtype=i32)
    lookup = lambda table, idx: jnp.sum(jnp.where(idx[..., None] == experts, table, 0), axis=-1)
    dest = (lookup(start, eidx) + rank).reshape(-1).astype(i32)
    blk_row = jnp.arange(n_blocks, dtype=i32) * bm
    blk_e = jnp.minimum(jnp.sum((padded_end[None, :] <= blk_row[:, None]).astype(i32), axis=1), N_EXPERTS - 1)
    blk_cnt = jnp.clip(lookup(counts, blk_e) - (blk_row - lookup(start, blk_e)), 0, bm).astype(i32)
    none = jnp.int32(N_EXPERTS)
    nxt_e = jnp.min(jnp.where(blk_e[None, :] > blk_e[:, None], blk_e[None, :], none), axis=1)
    nxt_e = jnp.where(nxt_e == none, -1, nxt_e).astype(i32)
    return dest, blk_e, blk_cnt, nxt_e


def _layer_and_final_norm(x2d, mem, seq, g_mix, g_mem, w_in, w_mem_kv, na_rel_bias, g_grp, w_out, g_ffn,
                          router_w, router_b, w_gu, b_gu, w_down, b_down, g_final):
    t, d = x2d.shape
    b = t // seq
    q_na, k_na, v_na, u_ft, q_mem = _in_proj(x2d, g_mix, w_in.astype(bf16))
    k_mem, v_mem = _mem_kv(mem, g_mem, w_mem_kv.astype(bf16))
    shape3 = lambda a: a.reshape(b, seq, a.shape[-1])
    y_na = _neighbourhood_attention(shape3(q_na), shape3(k_na), shape3(v_na), _na_bias_table(na_rel_bias))
    y_ft = _fourier_mix(shape3(u_ft), _ft_tables(seq))
    y_na, y_ft = y_na.reshape(t, -1), y_ft.reshape(t, -1)
    w_out_bf16 = w_out.astype(bf16)
    rw_hi = router_w.astype(bf16)
    router_w2 = jnp.stack([rw_hi, (router_w - rw_hi.astype(f32)).astype(bf16)])

    tiles = t // TOKEN_TILE
    assert tiles % MOE_TOKEN_GROUPS == 0
    group_tiles = tiles // MOE_TOKEN_GROUPS
    group_tokens = group_tiles * TOKEN_TILE
    n_blocks = (group_tokens * TOP_K) // MOE_BLOCK + N_EXPERTS
    out = None
    for grp in range(MOE_TOKEN_GROUPS):
        tile0 = grp * group_tiles
        x1, h2p, eidx, gates, rank, counts = _mix_out(
            x2d, y_na, y_ft, q_mem, k_mem, v_mem, g_grp, w_out_bf16, g_ffn, router_w2, router_b, seq,
            tile0, group_tiles)
        dest, blk_e, blk_cnt, nxt_e = _slot_layout(counts[0], eidx, rank, n_blocks)
        xs = _dispatch(h2p, dest, n_blocks * MOE_BLOCK)
        y_slots = _experts(xs, blk_e, blk_cnt, nxt_e, w_gu, b_gu, w_down, b_down)
        out = _combine(x1, gates, dest, y_slots, g_final, out, tile0, t)
    return out


def kernel(x, mem, g_mix, g_mem, w_in, w_mem_kv, na_rel_bias, g_grp, w_out, g_ffn, router_w, router_b,
           w_gu, b_gu, w_down, b_down, g_final):
    b, seq, d = x.shape
    depth = w_in.shape[0]
    assert depth == 1, "the final norm is fused into the single layer's combine step"
    out = _layer_and_final_norm(
        x.reshape(b * seq, d), mem, seq, g_mix[0], g_mem[0], w_in[0], w_mem_kv[0], na_rel_bias[0], g_grp[0],
        w_out[0], g_ffn[0], router_w[0], router_b[0], w_gu[0], b_gu[0], w_down[0], b_down[0], g_final)
    return out.reshape(b, seq, d)
```

```python
import functools

import numpy as np
import jax
import jax.numpy as jnp
from jax import lax
from jax.experimental import pallas as pl
from jax.experimental.pallas import tpu as pltpu
from jax.experimental.pallas import tpu_sc as plsc

f32 = jnp.float32
bf16 = jnp.bfloat16
u32 = jnp.uint32
i32 = jnp.int32

GRID_W = 64
NA_HEADS = 8
NA_HEAD_DIM = 64
NA_WIN_ROWS = 8
NA_WIN_COLS = 16
FT_GROUPS = 4
FT_GROUP_DIM = 128
MEM_HEADS = 4
MEM_HEAD_DIM = 128
NA_WIDTH = NA_HEADS * NA_HEAD_DIM
FT_WIDTH = FT_GROUPS * FT_GROUP_DIM
MEM_WIDTH = MEM_HEADS * MEM_HEAD_DIM
N_EXPERTS = 32
TOP_K = 4
SWIGLU_LIMIT = 7.0
SWIGLU_ALPHA = 1.702
EPS = 1e-6

LANES = 128
SUBLANES = 8
VMEM_LIMIT_BYTES = 56 * 1024 * 1024
SC_CORES = 2
SC_SUBCORES = 16
SC_GATHER_CHUNK = 64

TOKEN_TILE = 512
MOE_BLOCK = 512
WEIGHT_CAST_ROWS = 128
MOE_TOKEN_GROUPS = 2
NA_ROW_UNROLL = 16
FT_N1 = 64
FT_N2 = 128
FT_K1_BLOCK = 8
MASK_VALUE = -1e30


def _params(*semantics):
    return pltpu.CompilerParams(dimension_semantics=semantics, vmem_limit_bytes=VMEM_LIMIT_BYTES)


def _rms_scale(x):
    return x * lax.rsqrt(jnp.mean(x * x, axis=-1, keepdims=True) + EPS)


def _softmax_rows(s):
    p = jnp.exp(s - jnp.max(s, axis=-1, keepdims=True))
    return p / jnp.sum(p, axis=-1, keepdims=True)


def _bf16_bits(x):
    return pltpu.bitcast(x.astype(bf16).astype(f32), u32)


def _pack2(lo, hi):
    return (_bf16_bits(lo) >> 16) | (_bf16_bits(hi) & jnp.uint32(0xFFFF0000))


def _unpack2(w):
    return pltpu.bitcast(w << 16, f32), pltpu.bitcast(w & jnp.uint32(0xFFFF0000), f32)


def _pack_bf16_pairs(x):
    n = x.shape[1] // 2
    return _pack2(x[:, :n], x[:, n:])


def _unpack_bf16_pairs(w):
    return jnp.concatenate(_unpack2(w), axis=1)


def _in_proj_kernel(x_ref, g_ref, w_ref, qna_ref, kna_ref, vna_ref, uft_ref, qmem_ref):
    h = _rms_scale(x_ref[...]) * g_ref[...]
    proj = jnp.dot(h.astype(bf16), w_ref[...], preferred_element_type=f32)
    o = NA_WIDTH
    qna_ref[...] = (proj[:, :o] * (NA_HEAD_DIM ** -0.5)).astype(bf16)
    kna_ref[...] = proj[:, o:2 * o].astype(bf16)
    vna_ref[...] = proj[:, 2 * o:3 * o].astype(bf16)
    uft_ref[...] = _pack_bf16_pairs(proj[:, 3 * o:3 * o + FT_WIDTH])
    qmem_ref[...] = proj[:, 3 * o + FT_WIDTH:].astype(bf16)


def _in_proj(x2d, g_mix, w_in_bf16):
    t, d = x2d.shape
    tm = TOKEN_TILE
    row = lambda w: pl.BlockSpec((tm, w), lambda i: (i, 0))
    return pl.pallas_call(
        _in_proj_kernel,
        grid=(t // tm,),
        in_specs=[row(d), pl.BlockSpec((1, d), lambda i: (0, 0)),
                  pl.BlockSpec(w_in_bf16.shape, lambda i: (0, 0))],
        out_specs=[row(NA_WIDTH), row(NA_WIDTH), row(NA_WIDTH), row(FT_WIDTH // 2), row(MEM_WIDTH)],
        out_shape=[jax.ShapeDtypeStruct((t, NA_WIDTH), bf16)] * 3
        + [jax.ShapeDtypeStruct((t, FT_WIDTH // 2), u32), jax.ShapeDtypeStruct((t, MEM_WIDTH), bf16)],
        compiler_params=_params("parallel"),
        name="in_proj",
    )(x2d, g_mix.reshape(1, d), w_in_bf16)


def _na_bias_table(rel_bias):
    s = np.arange(NA_WIN_ROWS)[:, None]
    j = np.arange(NA_WIN_ROWS)[None, :]
    dr_idx = (j - s) + (NA_WIN_ROWS - 1)
    c = np.arange(GRID_W)
    dc_idx = np.clip(c[None, :] - c[:, None], -(NA_WIN_COLS - 1), NA_WIN_COLS - 1) + (NA_WIN_COLS - 1)
    col_start = np.clip(c - NA_WIN_COLS // 2, 0, GRID_W - NA_WIN_COLS)
    col_in = (c[None, :] >= col_start[:, None]) & (c[None, :] < col_start[:, None] + NA_WIN_COLS)
    pick_r = jnp.asarray(dr_idx[:, :, None] == np.arange(2 * NA_WIN_ROWS - 1), f32)
    pick_c = jnp.asarray(dc_idx[:, :, None] == np.arange(2 * NA_WIN_COLS - 1), f32)
    tab = jnp.einsum("hab,sja,qcb->shqjc", rel_bias.astype(f32), pick_r, pick_c, precision=lax.Precision.HIGHEST)
    tab = jnp.where(col_in[None, None, :, None, :], tab, MASK_VALUE)
    return tab.reshape(NA_WIN_ROWS, NA_HEADS // 2, 2 * GRID_W, NA_WIN_ROWS * GRID_W)


def _na_kernel(q_ref, k_ref, v_ref, bias_ref, o_ref):
    rows = q_ref.shape[1] // GRID_W
    win = NA_WIN_ROWS * GRID_W
    first_head = lax.broadcasted_iota(i32, (GRID_W, 2 * NA_HEAD_DIM), 1) < NA_HEAD_DIM

    def body(it, carry):
        scores, values, q_offsets = [], [], []
        for u in range(NA_ROW_UNROLL):
            r = it * NA_ROW_UNROLL + u
            row_start = jnp.clip(r - NA_WIN_ROWS // 2, 0, rows - NA_WIN_ROWS)
            q0 = pl.multiple_of(r * GRID_W, GRID_W)
            k0 = pl.multiple_of(row_start * GRID_W, GRID_W)
            q = q_ref[0, pl.ds(q0, GRID_W), :]
            zero = jnp.zeros_like(q)
            qm = jnp.concatenate([jnp.where(first_head, q, zero), jnp.where(first_head, zero, q)], axis=0)
            s = lax.dot_general(qm, k_ref[0, pl.ds(k0, win), :], (((1,), (1,)), ((), ())),
                                preferred_element_type=f32)
            scores.append(s + bias_ref[r - row_start, 0])
            values.append(v_ref[0, pl.ds(k0, win), :])
            q_offsets.append(q0)
        s = jnp.concatenate(scores, axis=0)
        p = jnp.exp(s - jnp.max(s, axis=-1, keepdims=True))
        inv_den = 1.0 / jnp.sum(p, axis=-1, keepdims=True)
        p = p.astype(bf16)
        for u in range(NA_ROW_UNROLL):
            sl = slice(u * 2 * GRID_W, (u + 1) * 2 * GRID_W)
            o = jnp.dot(p[sl], values[u], preferred_element_type=f32) * inv_den[sl]
            o_ref[0, pl.ds(q_offsets[u], GRID_W), :] = jnp.where(
                first_head, o[:GRID_W], o[GRID_W:]).astype(o_ref.dtype)
        return carry

    lax.fori_loop(0, rows // NA_ROW_UNROLL, body, 0)


def _neighbourhood_attention(q, k, v, bias_tab):
    b, s, _ = q.shape
    pair = 2 * NA_HEAD_DIM
    qkv_spec = pl.BlockSpec((1, s, pair), lambda bi, hp: (bi, 0, hp))
    return pl.pallas_call(
        _na_kernel,
        grid=(b, NA_HEADS // 2),
        in_specs=[qkv_spec, qkv_spec, qkv_spec,
                  pl.BlockSpec((NA_WIN_ROWS, 1, 2 * GRID_W, NA_WIN_ROWS * GRID_W), lambda bi, hp: (0, hp, 0, 0))],
        out_specs=qkv_spec,
        out_shape=jax.ShapeDtypeStruct((b, s, NA_WIDTH), bf16),
        compiler_params=_params("parallel", "parallel"),
        name="neighbourhood_attention",
    )(q, k, v, bias_tab)


def _ft_tables(seq):
    assert seq == FT_N1 * FT_N2
    n_blk = FT_N2 // SUBLANES
    k1 = np.arange(FT_N1)[:, None, None, None]
    sr = np.arange(SUBLANES)[None, :, None, None]
    n1 = np.arange(FT_N1)[None, None, :, None]
    sc = np.arange(SUBLANES)[None, None, None, :]
    stage1 = np.zeros((n_blk, 2, FT_N1, SUBLANES, FT_N1, SUBLANES), np.float64)
    for blk in range(n_blk):
        n = FT_N2 * n1 + SUBLANES * blk + sr
        ang = 2.0 * np.pi * ((k1 * n) % seq) / seq
        eye = (sr == sc)
        stage1[blk, 0] = np.cos(ang) * eye
        stage1[blk, 1] = -np.sin(ang) * eye
    stage1 = stage1.reshape(n_blk, 2 * FT_N1 * SUBLANES, FT_N1 * SUBLANES)
    a = np.arange(FT_N2)
    ang2 = 2.0 * np.pi * ((a[:, None] * a[None, :]) % FT_N2) / FT_N2
    c2, s2 = np.cos(ang2), np.sin(ang2)
    stage2 = np.block([[c2, s2], [-s2, c2]])
    g = np.arange(FT_GROUP_DIM)
    angc = 2.0 * np.pi * ((g[:, None] * g[None, :]) % FT_GROUP_DIM) / FT_GROUP_DIM
    norm = 1.0 / np.sqrt(seq * FT_GROUP_DIM)
    chan = np.concatenate([np.cos(angc), np.sin(angc)], axis=0) * norm
    return (jnp.asarray(stage1, bf16), jnp.asarray(stage2, bf16), jnp.asarray(chan, bf16))


def _ft_stage1_kernel(u_ref, m_ref, z_ref):
    rows = FT_N1 * SUBLANES
    u = _unpack_bf16_pairs(u_ref[0].reshape(rows, FT_WIDTH // 2)).astype(bf16)
    z = jnp.dot(m_ref[0], u, preferred_element_type=f32)
    z_ref[0] = _pack2(z[:rows], z[rows:]).reshape(FT_N1, SUBLANES, FT_WIDTH)


def _ft_stage2_kernel(z_ref, s2_ref, cs_ref, y_ref):
    gd = FT_GROUP_DIM
    for kk in range(FT_K1_BLOCK):
        zz = jnp.concatenate(_unpack2(z_ref[0, kk]), axis=0).astype(bf16)
        x = jnp.dot(s2_ref[...], zz, preferred_element_type=f32)
        outs = []
        for g in range(FT_GROUPS):
            xg = jnp.concatenate([x[:FT_N2, g * gd:(g + 1) * gd], x[FT_N2:, g * gd:(g + 1) * gd]], axis=1)
            outs.append(jnp.dot(xg.astype(bf16), cs_ref[...], preferred_element_type=f32))
        y_ref[0, kk] = jnp.concatenate(outs, axis=1).astype(y_ref.dtype)


def _fourier_mix(u_packed, tables):
    u = u_packed
    b, s, _ = u.shape
    c = FT_WIDTH
    stage1, stage2, chan = tables
    n_blk = FT_N2 // SUBLANES
    z = pl.pallas_call(
        _ft_stage1_kernel,
        grid=(n_blk, b),
        in_specs=[pl.BlockSpec((1, FT_N1, SUBLANES, c // 2), lambda j, bi: (bi, 0, j, 0)),
                  pl.BlockSpec((1,) + stage1.shape[1:], lambda j, bi: (j, 0, 0))],
        out_specs=pl.BlockSpec((1, FT_N1, SUBLANES, c), lambda j, bi: (bi, 0, j, 0)),
        out_shape=jax.ShapeDtypeStruct((b, FT_N1, FT_N2, c), u32),
        compiler_params=_params("parallel", "parallel"),
        name="fourier_stage1",
    )(u.reshape(b, FT_N1, FT_N2, c // 2), stage1)
    y = pl.pallas_call(
        _ft_stage2_kernel,
        grid=(b, FT_N1 // FT_K1_BLOCK),
        in_specs=[pl.BlockSpec((1, FT_K1_BLOCK, FT_N2, c), lambda bi, kb: (bi, kb, 0, 0)),
                  pl.BlockSpec(stage2.shape, lambda bi, kb: (0, 0)),
                  pl.BlockSpec(chan.shape, lambda bi, kb: (0, 0))],
        out_specs=pl.BlockSpec((1, FT_K1_BLOCK, FT_N2, c), lambda bi, kb: (bi, kb, 0, 0)),
        out_shape=jax.ShapeDtypeStruct((b, FT_N1, FT_N2, c), bf16),
        compiler_params=_params("parallel", "parallel"),
        name="fourier_stage2",
    )(z, stage2, chan)
    return y.transpose(0, 2, 1, 3).reshape(b, s, c)


def _mem_kv_kernel(mem_ref, g_ref, w_ref, k_ref, v_ref):
    mn = _rms_scale(mem_ref[0]) * g_ref[...]
    kv = jnp.dot(mn.astype(bf16), w_ref[...], preferred_element_type=f32)
    k_ref[0] = kv[:, :MEM_WIDTH].astype(bf16)
    v_ref[0] = kv[:, MEM_WIDTH:].astype(bf16)


def _mem_kv(mem, g_mem, w_kv_bf16):
    b, m, d = mem.shape
    kv_spec = pl.BlockSpec((1, m, MEM_WIDTH), lambda bi: (bi, 0, 0))
    return pl.pallas_call(
        _mem_kv_kernel,
        grid=(b,),
        in_specs=[pl.BlockSpec((1, m, d), lambda bi: (bi, 0, 0)), pl.BlockSpec((1, d), lambda bi: (0, 0)),
                  pl.BlockSpec(w_kv_bf16.shape, lambda bi: (0, 0))],
        out_specs=[kv_spec, kv_spec],
        out_shape=[jax.ShapeDtypeStruct((b, m, MEM_WIDTH), bf16)] * 2,
        compiler_params=_params("parallel"),
        name="mem_kv",
    )(mem, g_mem.reshape(1, d), w_kv_bf16)


def _mix_out_kernel(x_ref, yna_ref, yft_ref, qm_ref, km_ref, vm_ref, ggrp_ref, wout_ref, gffn_ref, rw_ref,
                    rb_ref, x1_ref, h2p_ref, eidx_ref, gate_ref, rank_ref, cnt_ref, carry_ref):
    tm = x_ref.shape[0]

    @pl.when(pl.program_id(0) == 0)
    def _():
        carry_ref[...] = jnp.zeros_like(carry_ref)

    q = qm_ref[...]
    km = km_ref[0]
    vm = vm_ref[0]
    heads = []
    for h in range(MEM_HEADS):
        sl = slice(h * MEM_HEAD_DIM, (h + 1) * MEM_HEAD_DIM)
        s = lax.dot_general(q[:, sl], km[:, sl], (((1,), (1,)), ((), ())), preferred_element_type=f32)
        p = _softmax_rows(s * (MEM_HEAD_DIM ** -0.5))
        heads.append(jnp.dot(p.astype(bf16), vm[:, sl], preferred_element_type=f32))
    ymem = jnp.concatenate(heads, axis=1)

    g = ggrp_ref[...]
    a, c = NA_WIDTH, NA_WIDTH + FT_WIDTH
    y = jnp.concatenate([_rms_scale(yna_ref[...].astype(f32)) * g[:, :a],
                         _rms_scale(yft_ref[...].astype(f32)) * g[:, a:c],
                         _rms_scale(ymem) * g[:, c:]], axis=1)
    x1 = x_ref[...] + jnp.dot(y.astype(bf16), wout_ref[...], preferred_element_type=f32)
    x1_ref[...] = x1
    h2 = _rms_scale(x1) * gffn_ref[...]
    h2p_ref[...] = _pack_bf16_pairs(h2)

    h_hi = h2.astype(bf16)
    h_lo = (h2 - h_hi.astype(f32)).astype(bf16)
    hh = jnp.dot(h_hi, rw_ref[...], preferred_element_type=f32)
    logits = (hh[:, :LANES] + hh[:, LANES:]
              + jnp.dot(h_lo, rw_ref[:, :LANES], preferred_element_type=f32)) + rb_ref[...]
    l = logits.T[:N_EXPERTS]
    row = lax.broadcasted_iota(i32, (N_EXPERTS, tm), 0).astype(f32)
    vals, idxs, sels = [], [], []
    for _ in range(TOP_K):
        m = jnp.max(l, axis=0, keepdims=True)
        idx = jnp.min(jnp.where(l == m, row, float(N_EXPERTS)), axis=0, keepdims=True)
        sel = row == idx
        vals.append(m)
        idxs.append(idx)
        sels.append(sel)
        l = jnp.where(sel, -jnp.inf, l)
    ex = [jnp.exp(v - vals[0]) for v in vals]
    den = ex[0] + ex[1] + ex[2] + ex[3]

    onehot = (sels[0] | sels[1] | sels[2] | sels[3]).astype(f32)
    earlier = (lax.broadcasted_iota(i32, (tm, tm), 0) < lax.broadcasted_iota(i32, (tm, tm), 1)).astype(bf16)
    before = jnp.dot(onehot.astype(bf16), earlier, preferred_element_type=f32) + carry_ref[...]
    ranks = [jnp.sum(jnp.where(sel, before, 0.0), axis=0, keepdims=True) for sel in sels]
    carry_ref[...] = carry_ref[...] + jnp.sum(onehot, axis=1, keepdims=True)
    cnt_ref[...] = carry_ref[...].astype(i32)

    eidx_ref[...] = jnp.concatenate(idxs, axis=0).astype(i32)
    rank_ref[...] = jnp.concatenate(ranks, axis=0).astype(i32)
    gates_t = jnp.concatenate([e / den for e in ex] + [jnp.zeros((LANES - TOP_K, tm), f32)], axis=0)
    gate_ref[...] = gates_t.T[:, :TOP_K]


def _mix_out(x2d, y_na, y_ft, q_mem, k_mem, v_mem, g_grp, w_out_bf16, g_ffn, router_w2, router_b, seq,
             tile0, n_tiles):
    d = x2d.shape[1]
    tm = TOKEN_TILE
    t = n_tiles * tm
    steps_per_batch = seq // tm
    m = k_mem.shape[1]
    row_in = lambda w: pl.BlockSpec((tm, w), lambda i: (i + tile0, 0))
    row_out = lambda w: pl.BlockSpec((tm, w), lambda i: (i, 0))
    full = lambda a: pl.BlockSpec(a.shape, lambda i: (0,) * a.ndim)
    kv_spec = pl.BlockSpec((1, m, MEM_WIDTH), lambda i: ((i + tile0) // steps_per_batch, 0, 0))
    g_grp2, g_ffn2 = g_grp.reshape(1, -1), g_ffn.reshape(1, d)
    rb2 = jnp.pad(router_b.reshape(1, N_EXPERTS), ((0, 0), (0, LANES - N_EXPERTS)))
    router_w2 = jnp.pad(router_w2, ((0, 0), (0, 0), (0, LANES - N_EXPERTS)))
    router_w2 = jnp.concatenate([router_w2[0], router_w2[1]], axis=1)
    col_out = pl.BlockSpec((TOP_K, tm), lambda i: (0, i))
    return pl.pallas_call(
        _mix_out_kernel,
        grid=(n_tiles,),
        in_specs=[row_in(d), row_in(NA_WIDTH), row_in(FT_WIDTH), row_in(MEM_WIDTH), kv_spec, kv_spec,
                  full(g_grp2), full(w_out_bf16), full(g_ffn2), full(router_w2), full(rb2)],
        out_specs=[row_out(d), row_out(d // 2), col_out, row_out(TOP_K), col_out,
                   pl.BlockSpec((N_EXPERTS, 1), lambda i: (0, 0))],
        out_shape=[jax.ShapeDtypeStruct((t, d), f32), jax.ShapeDtypeStruct((t, d // 2), u32),
                   jax.ShapeDtypeStruct((TOP_K, t), i32), jax.ShapeDtypeStruct((t, TOP_K), f32),
                   jax.ShapeDtypeStruct((TOP_K, t), i32), jax.ShapeDtypeStruct((N_EXPERTS, 1), i32)],
        scratch_shapes=[pltpu.VMEM((N_EXPERTS, 1), f32)],
        compiler_params=_params("arbitrary"),
        name="mix_out_router",
    )(x2d, y_na, y_ft, q_mem, k_mem, v_mem, g_grp2, w_out_bf16, g_ffn2, router_w2, rb2)


def _sc_mesh():
    return plsc.VectorSubcoreMesh(core_axis_name="c", subcore_axis_name="s",
                                  num_cores=SC_CORES, num_subcores=SC_SUBCORES)


def _dispatch(h2p, dest, n_slots):
    t, w = h2p.shape
    workers = SC_CORES * SC_SUBCORES
    chunk = SC_GATHER_CHUNK
    per = t // workers
    steps = per // chunk
    assert per * workers == t and steps * chunk == per and steps % 2 == 0
    idx = dest.reshape(TOP_K, workers, steps, chunk)

    def body(h_hbm, idx_hbm, out_hbm, idx_v, rows_v, rsem, ssem):
        wid = lax.axis_index("s") * SC_CORES + lax.axis_index("c")
        base = wid * per
        for j in range(TOP_K):
            pltpu.sync_copy(idx_hbm.at[j, wid], idx_v.at[j])

        def read(c, slot):
            return pltpu.make_async_copy(h_hbm.at[pl.ds(base + c * chunk, chunk)], rows_v.at[slot], rsem.at[slot])

        def scatters(c, slot):
            return [pltpu.make_async_copy(rows_v.at[slot], out_hbm.at[idx_v.at[j, c]], ssem.at[slot])
                    for j in range(TOP_K)]

        read(0, 0).start()

        @pl.loop(0, steps, step=2)
        def _(c0):
            for slot in range(2):
                c = c0 + slot
                read(c, slot).wait()
                for cp in scatters(c, slot):
                    cp.start()

                @pl.when(c >= 1)
                def _():
                    for cp in scatters(c - 1, 1 - slot):
                        cp.wait()

                @pl.when(c + 1 < steps)
                def _():
                    read(c + 1, 1 - slot).start()

        for cp in scatters(steps - 1, 1):
            cp.wait()

    return pl.kernel(
        body,
        out_type=jax.ShapeDtypeStruct((n_slots, w), h2p.dtype),
        mesh=_sc_mesh(),
        scratch_types=[pltpu.VMEM((TOP_K, steps, chunk), i32), pltpu.VMEM((2, chunk, w), h2p.dtype),
                       pltpu.SemaphoreType.DMA((2,)), pltpu.SemaphoreType.DMA((2,))],
        name="sc_dispatch_rows",
    )(h2p, idx)


def _expert_kernel(blk_e_ref, blk_cnt_ref, nxt_e_ref, xs_ref, wgu_hbm, bgu_ref, wd_hbm, bd_ref, y_ref,
                   wgu_f32, wd_f32, wgu_bf, wd_bf, sem):
    b = pl.program_id(0)
    e = blk_e_ref[b]
    cnt = blk_cnt_ref[b]
    bm = xs_ref.shape[0]
    de = wd_f32.shape[0]

    def fetch(expert):
        return (pltpu.make_async_copy(wgu_hbm.at[expert], wgu_f32, sem.at[0]),
                pltpu.make_async_copy(wd_hbm.at[expert], wd_f32, sem.at[1]))

    @pl.when(b == 0)
    def _():
        for cp in fetch(e):
            cp.start()

    @pl.when(jnp.logical_or(b == 0, e != blk_e_ref[jnp.maximum(b - 1, 0)]))
    def _():
        for cp in fetch(e):
            cp.wait()

        def convert(i, carry):
            rows = pl.ds(pl.multiple_of(i * WEIGHT_CAST_ROWS, WEIGHT_CAST_ROWS), WEIGHT_CAST_ROWS)
            wgu_bf[rows, :] = wgu_f32[rows, :].astype(bf16)
            wd_bf[rows, :] = wd_f32[rows, :].astype(bf16)
            return carry

        lax.fori_loop(0, wgu_f32.shape[0] // WEIGHT_CAST_ROWS, convert, 0)

        @pl.when(nxt_e_ref[b] >= 0)
        def _():
            for cp in fetch(nxt_e_ref[b]):
                cp.start()

    @pl.when(cnt > 0)
    def _():
        valid = lax.broadcasted_iota(i32, (bm, 1), 0) < cnt
        x = jnp.where(valid, _unpack_bf16_pairs(xs_ref[...]), 0.0).astype(bf16)
        gu = jnp.dot(x, wgu_bf[...], preferred_element_type=f32) + bgu_ref[0]
        x_glu = jnp.minimum(gu[:, :de], SWIGLU_LIMIT)
        x_lin = jnp.clip(gu[:, de:], -SWIGLU_LIMIT, SWIGLU_LIMIT)
        act = x_glu * (1.0 / (1.0 + jnp.exp(-SWIGLU_ALPHA * x_glu))) * (x_lin + 1.0)
        y = jnp.dot(act.astype(bf16), wd_bf[...], preferred_element_type=f32) + bd_ref[0]
        y_ref[...] = _pack_bf16_pairs(y)

    @pl.when(cnt == 0)
    def _():
        y_ref[...] = jnp.zeros_like(y_ref)


def _experts(xs, blk_e, blk_cnt, nxt_e, w_gu, b_gu, w_down, b_down):
    n_slots, w = xs.shape
    bm = MOE_BLOCK
    e, d, de2 = w_gu.shape
    de = w_down.shape[1]
    assert de == d, "one row loop converts both weight matrices"
    grid_spec = pltpu.PrefetchScalarGridSpec(
        num_scalar_prefetch=3,
        grid=(n_slots // bm,),
        in_specs=[pl.BlockSpec((bm, w), lambda b, be, bc, ne: (b, 0)),
                  pl.BlockSpec(memory_space=pl.ANY),
                  pl.BlockSpec((1, 1, de2), lambda b, be, bc, ne: (be[b], 0, 0)),
                  pl.BlockSpec(memory_space=pl.ANY),
                  pl.BlockSpec((1, 1, d), lambda b, be, bc, ne: (be[b], 0, 0))],
        out_specs=pl.BlockSpec((bm, w), lambda b, be, bc, ne: (b, 0)),
        scratch_shapes=[pltpu.VMEM((d, de2), f32), pltpu.VMEM((de, d), f32),
                        pltpu.VMEM((d, de2), bf16), pltpu.VMEM((de, d), bf16),
                        pltpu.SemaphoreType.DMA((2,))],
    )
    return pl.pallas_call(
        _expert_kernel,
        grid_spec=grid_spec,
        out_shape=jax.ShapeDtypeStruct((n_slots, w), u32),
        compiler_params=_params("arbitrary"),
        name="moe_experts",
    )(blk_e, blk_cnt, nxt_e, xs, w_gu, b_gu.reshape(e, 1, de2), w_down, b_down.reshape(e, 1, d))


def _sc_gather_rows(table, idx):
    n, w = idx.shape[0], table.shape[1]
    workers = SC_CORES * SC_SUBCORES
    chunk = SC_GATHER_CHUNK
    per = n // workers
    steps = per // chunk
    assert per * workers == n and steps * chunk == per and steps % 2 == 0

    def body(table_hbm, idx_hbm, out_hbm, idx_v, rows_v, gsem, wsem):
        base = (lax.axis_index("s") * SC_CORES + lax.axis_index("c")) * per
        pltpu.sync_copy(idx_hbm.at[pl.ds(base, per)], idx_v)

        def gather(c, slot):
            return pltpu.make_async_copy(table_hbm.at[idx_v.at[pl.ds(c * chunk, chunk)]], rows_v.at[slot],
                                         gsem.at[slot])

        def write(c, slot):
            return pltpu.make_async_copy(rows_v.at[slot], out_hbm.at[pl.ds(base + c * chunk, chunk)],
                                         wsem.at[slot])

        gather(0, 0).start()

        @pl.loop(0, steps, step=2)
        def _(c0):
            for slot in range(2):
                c = c0 + slot
                gather(c, slot).wait()
                write(c, slot).start()

                @pl.when(c >= 1)
                def _():
                    write(c - 1, 1 - slot).wait()

                @pl.when(c + 1 < steps)
                def _():
                    gather(c + 1, 1 - slot).start()

        write(steps - 1, 1).wait()

    return pl.kernel(
        body,
        out_type=jax.ShapeDtypeStruct((n, w), table.dtype),
        mesh=_sc_mesh(),
        scratch_types=[pltpu.VMEM((per,), i32), pltpu.VMEM((2, chunk, w), table.dtype),
                       pltpu.SemaphoreType.DMA((2,)), pltpu.SemaphoreType.DMA((2,))],
        name="sc_gather_rows",
    )(table, idx)


def _combine_kernel(x1_ref, gate_ref, gfin_ref, yg_ref, *rest):
    o_ref = rest[-1]
    acc = x1_ref[...]
    gates = gate_ref[...]
    for j in range(TOP_K):
        acc = acc + gates[:, j:j + 1] * _unpack_bf16_pairs(yg_ref[j])
    o_ref[...] = _rms_scale(acc) * gfin_ref[...]


def _combine(x1, gates, dest, y_slots, g_final, out_prev, tile0, total_tokens):
    t, d = x1.shape
    tm = TOKEN_TILE
    w = y_slots.shape[1]
    yg = _sc_gather_rows(y_slots, dest.reshape(-1)).reshape(TOP_K, t, w)
    in_specs = [pl.BlockSpec((tm, d), lambda i: (i, 0)),
                pl.BlockSpec((tm, TOP_K), lambda i: (i, 0)),
                pl.BlockSpec((1, d), lambda i: (0, 0)),
                pl.BlockSpec((TOP_K, tm, w), lambda i: (0, i, 0))]
    args = [x1, gates, g_final.reshape(1, d), yg]
    aliases = {}
    if out_prev is not None:
        in_specs.append(pl.BlockSpec(memory_space=pl.ANY))
        args.append(out_prev)
        aliases = {len(args) - 1: 0}
    return pl.pallas_call(
        _combine_kernel,
        grid=(t // tm,),
        in_specs=in_specs,
        out_specs=pl.BlockSpec((tm, d), lambda i: (i + tile0, 0)),
        out_shape=jax.ShapeDtypeStruct((total_tokens, d), f32),
        input_output_aliases=aliases,
        compiler_params=_params("parallel"),
        name="moe_combine",
    )(*args)


def _slot_layout(counts, eidx, rank, n_blocks):
    bm = MOE_BLOCK
    padded = (counts + bm - 1) // bm * bm
    padded_end = jnp.cumsum(padded)
    start = padded_end - padded
    experts = jnp.arange(N_EXPERTS, dtype=i32)
    lookup = lambda table, idx: jnp.sum(jnp.where(idx[..., None] == experts, table, 0), axis=-1)
    dest = rank
    for e in range(N_EXPERTS):
        dest = dest + jnp.where(eidx == e, start[e], 0)
    dest = dest.astype(i32)
    blk_row = jnp.arange(n_blocks, dtype=i32) * bm
    blk_e = jnp.minimum(jnp.sum((padded_end[None, :] <= blk_row[:, None]).astype(i32), axis=1), N_EXPERTS - 1)
    blk_cnt = jnp.clip(lookup(counts, blk_e) - (blk_row - lookup(start, blk_e)), 0, bm).astype(i32)
    none = jnp.int32(N_EXPERTS)
    nxt_e = jnp.min(jnp.where(blk_e[None, :] > blk_e[:, None], blk_e[None, :], none), axis=1)
    nxt_e = jnp.where(nxt_e == none, -1, nxt_e).astype(i32)
    return dest, blk_e, blk_cnt, nxt_e


def _layer_and_final_norm(x2d, mem, seq, g_mix, g_mem, w_in, w_mem_kv, na_rel_bias, g_grp, w_out, g_ffn,
                          router_w, router_b, w_gu, b_gu, w_down, b_down, g_final):
    t, d = x2d.shape
    b = t // seq
    q_na, k_na, v_na, u_ft, q_mem = _in_proj(x2d, g_mix, w_in.astype(bf16))
    k_mem, v_mem = _mem_kv(mem, g_mem, w_mem_kv.astype(bf16))
    shape3 = lambda a: a.reshape(b, seq, a.shape[-1])
    y_na = _neighbourhood_attention(shape3(q_na), shape3(k_na), shape3(v_na), _na_bias_table(na_rel_bias))
    y_ft = _fourier_mix(shape3(u_ft), _ft_tables(seq))
    y_na, y_ft = y_na.reshape(t, -1), y_ft.reshape(t, -1)
    w_out_bf16 = w_out.astype(bf16)
    rw_hi = router_w.astype(bf16)
    router_w2 = jnp.stack([rw_hi, (router_w - rw_hi.astype(f32)).astype(bf16)])

    tiles = t // TOKEN_TILE
    assert tiles % MOE_TOKEN_GROUPS == 0
    group_tiles = tiles // MOE_TOKEN_GROUPS
    group_tokens = group_tiles * TOKEN_TILE
    n_blocks = (group_tokens * TOP_K) // MOE_BLOCK + N_EXPERTS
    out = None
    for grp in range(MOE_TOKEN_GROUPS):
        tile0 = grp * group_tiles
        x1, h2p, eidx, gates, rank, counts = _mix_out(
            x2d, y_na, y_ft, q_mem, k_mem, v_mem, g_grp, w_out_bf16, g_ffn, router_w2, router_b, seq,
            tile0, group_tiles)
        dest, blk_e, blk_cnt, nxt_e = _slot_layout(counts[:, 0], eidx, rank, n_blocks)
        xs = _dispatch(h2p, dest, n_blocks * MOE_BLOCK)
        y_slots = _experts(xs, blk_e, blk_cnt, nxt_e, w_gu, b_gu, w_down, b_down)
        out = _combine(x1, gates, dest, y_slots, g_final, out, tile0, t)
    return out


def kernel(x, mem, g_mix, g_mem, w_in, w_mem_kv, na_rel_bias, g_grp, w_out, g_ffn, router_w, router_b,
           w_gu, b_gu, w_down, b_down, g_final):
    b, seq, d = x.shape
    depth = w_in.shape[0]
    assert depth == 1, "the final norm is fused into the single layer's combine step"
    out = _layer_and_final_norm(
        x.reshape(b * seq, d), mem, seq, g_mix[0], g_mem[0], w_in[0], w_mem_kv[0], na_rel_bias[0], g_grp[0],
        w_out[0], g_ffn[0], router_w[0], router_b[0], w_gu[0], b_gu[0], w_down[0], b_down[0], g_final)
    return out.reshape(b, seq, d)
```

```python
import functools

import numpy as np
import jax
import jax.numpy as jnp
from jax import lax
from jax.experimental import pallas as pl
from jax.experimental.pallas import tpu as pltpu
from jax.experimental.pallas import tpu_sc as plsc

f32 = jnp.float32
bf16 = jnp.bfloat16
u32 = jnp.uint32
i32 = jnp.int32

GRID_W = 64
NA_HEADS = 8
NA_HEAD_DIM = 64
NA_WIN_ROWS = 8
NA_WIN_COLS = 16
FT_GROUPS = 4
FT_GROUP_DIM = 128
MEM_HEADS = 4
MEM_HEAD_DIM = 128
NA_WIDTH = NA_HEADS * NA_HEAD_DIM
FT_WIDTH = FT_GROUPS * FT_GROUP_DIM
MEM_WIDTH = MEM_HEADS * MEM_HEAD_DIM
N_EXPERTS = 32
TOP_K = 4
SWIGLU_LIMIT = 7.0
SWIGLU_ALPHA = 1.702
EPS = 1e-6

LANES = 128
SUBLANES = 8
VMEM_LIMIT_BYTES = 56 * 1024 * 1024
SC_CORES = 2
SC_SUBCORES = 16
SC_GATHER_CHUNK = 64

TOKEN_TILE = 512
MOE_BLOCK = 512
MOE_SUB_BLOCK = 128
WEIGHT_CAST_ROWS = 128
MOE_TOKEN_GROUPS = 2
NA_ROW_UNROLL = 16
FT_N1 = 64
FT_N2 = 128
FT_K1_BLOCK = 8
MASK_VALUE = -1e30


def _params(*semantics):
    return pltpu.CompilerParams(dimension_semantics=semantics, vmem_limit_bytes=VMEM_LIMIT_BYTES)


def _rms_scale(x):
    return x * lax.rsqrt(jnp.mean(x * x, axis=-1, keepdims=True) + EPS)


def _softmax_rows(s):
    p = jnp.exp(s - jnp.max(s, axis=-1, keepdims=True))
    return p / jnp.sum(p, axis=-1, keepdims=True)


def _bf16_bits(x):
    return pltpu.bitcast(x.astype(bf16).astype(f32), u32)


def _pack2(lo, hi):
    return (_bf16_bits(lo) >> 16) | (_bf16_bits(hi) & jnp.uint32(0xFFFF0000))


def _unpack2(w):
    return pltpu.bitcast(w << 16, f32), pltpu.bitcast(w & jnp.uint32(0xFFFF0000), f32)


def _pack_bf16_pairs(x):
    n = x.shape[1] // 2
    return _pack2(x[:, :n], x[:, n:])


def _unpack_bf16_pairs(w):
    return jnp.concatenate(_unpack2(w), axis=1)


def _in_proj_kernel(x_ref, g_ref, w_ref, qna_ref, kna_ref, vna_ref, uft_ref, qmem_ref):
    h = _rms_scale(x_ref[...]) * g_ref[...]
    proj = jnp.dot(h.astype(bf16), w_ref[...], preferred_element_type=f32)
    o = NA_WIDTH
    qna_ref[...] = (proj[:, :o] * (NA_HEAD_DIM ** -0.5)).astype(bf16)
    kna_ref[...] = proj[:, o:2 * o].astype(bf16)
    vna_ref[...] = proj[:, 2 * o:3 * o].astype(bf16)
    uft_ref[...] = _pack_bf16_pairs(proj[:, 3 * o:3 * o + FT_WIDTH])
    qmem_ref[...] = proj[:, 3 * o + FT_WIDTH:].astype(bf16)


def _in_proj(x2d, g_mix, w_in_bf16):
    t, d = x2d.shape
    tm = TOKEN_TILE
    row = lambda w: pl.BlockSpec((tm, w), lambda i: (i, 0))
    return pl.pallas_call(
        _in_proj_kernel,
        grid=(t // tm,),
        in_specs=[row(d), pl.BlockSpec((1, d), lambda i: (0, 0)),
                  pl.BlockSpec(w_in_bf16.shape, lambda i: (0, 0))],
        out_specs=[row(NA_WIDTH), row(NA_WIDTH), row(NA_WIDTH), row(FT_WIDTH // 2), row(MEM_WIDTH)],
        out_shape=[jax.ShapeDtypeStruct((t, NA_WIDTH), bf16)] * 3
        + [jax.ShapeDtypeStruct((t, FT_WIDTH // 2), u32), jax.ShapeDtypeStruct((t, MEM_WIDTH), bf16)],
        compiler_params=_params("parallel"),
        name="in_proj",
    )(x2d, g_mix.reshape(1, d), w_in_bf16)


def _na_bias_table(rel_bias):
    s = np.arange(NA_WIN_ROWS)[:, None]
    j = np.arange(NA_WIN_ROWS)[None, :]
    dr_idx = (j - s) + (NA_WIN_ROWS - 1)
    c = np.arange(GRID_W)
    dc_idx = np.clip(c[None, :] - c[:, None], -(NA_WIN_COLS - 1), NA_WIN_COLS - 1) + (NA_WIN_COLS - 1)
    col_start = np.clip(c - NA_WIN_COLS // 2, 0, GRID_W - NA_WIN_COLS)
    col_in = (c[None, :] >= col_start[:, None]) & (c[None, :] < col_start[:, None] + NA_WIN_COLS)
    pick_r = jnp.asarray(dr_idx[:, :, None] == np.arange(2 * NA_WIN_ROWS - 1), f32)
    pick_c = jnp.asarray(dc_idx[:, :, None] == np.arange(2 * NA_WIN_COLS - 1), f32)
    tab = jnp.einsum("hab,sja,qcb->shqjc", rel_bias.astype(f32), pick_r, pick_c, precision=lax.Precision.HIGHEST)
    tab = jnp.where(col_in[None, None, :, None, :], tab, MASK_VALUE)
    return tab.reshape(NA_WIN_ROWS, NA_HEADS // 2, 2 * GRID_W, NA_WIN_ROWS * GRID_W)


def _na_kernel(q_ref, k_ref, v_ref, bias_ref, o_ref):
    rows = q_ref.shape[1] // GRID_W
    win = NA_WIN_ROWS * GRID_W
    first_head = lax.broadcasted_iota(i32, (GRID_W, 2 * NA_HEAD_DIM), 1) < NA_HEAD_DIM

    def body(it, carry):
        scores, values, q_offsets = [], [], []
        for u in range(NA_ROW_UNROLL):
            r = it * NA_ROW_UNROLL + u
            row_start = jnp.clip(r - NA_WIN_ROWS // 2, 0, rows - NA_WIN_ROWS)
            q0 = pl.multiple_of(r * GRID_W, GRID_W)
            k0 = pl.multiple_of(row_start * GRID_W, GRID_W)
            q = q_ref[0, pl.ds(q0, GRID_W), :]
            zero = jnp.zeros_like(q)
            qm = jnp.concatenate([jnp.where(first_head, q, zero), jnp.where(first_head, zero, q)], axis=0)
            s = lax.dot_general(qm, k_ref[0, pl.ds(k0, win), :], (((1,), (1,)), ((), ())),
                                preferred_element_type=f32)
            scores.append(s + bias_ref[r - row_start, 0])
            values.append(v_ref[0, pl.ds(k0, win), :])
            q_offsets.append(q0)
        s = jnp.concatenate(scores, axis=0)
        p = jnp.exp(s - jnp.max(s, axis=-1, keepdims=True))
        inv_den = 1.0 / jnp.sum(p, axis=-1, keepdims=True)
        p = p.astype(bf16)
        for u in range(NA_ROW_UNROLL):
            sl = slice(u * 2 * GRID_W, (u + 1) * 2 * GRID_W)
            o = jnp.dot(p[sl], values[u], preferred_element_type=f32) * inv_den[sl]
            o_ref[0, pl.ds(q_offsets[u], GRID_W), :] = jnp.where(
                first_head, o[:GRID_W], o[GRID_W:]).astype(o_ref.dtype)
        return carry

    lax.fori_loop(0, rows // NA_ROW_UNROLL, body, 0)


def _neighbourhood_attention(q, k, v, bias_tab):
    b, s, _ = q.shape
    pair = 2 * NA_HEAD_DIM
    qkv_spec = pl.BlockSpec((1, s, pair), lambda bi, hp: (bi, 0, hp))
    return pl.pallas_call(
        _na_kernel,
        grid=(b, NA_HEADS // 2),
        in_specs=[qkv_spec, qkv_spec, qkv_spec,
                  pl.BlockSpec((NA_WIN_ROWS, 1, 2 * GRID_W, NA_WIN_ROWS * GRID_W), lambda bi, hp: (0, hp, 0, 0))],
        out_specs=qkv_spec,
        out_shape=jax.ShapeDtypeStruct((b, s, NA_WIDTH), bf16),
        compiler_params=_params("parallel", "parallel"),
        name="neighbourhood_attention",
    )(q, k, v, bias_tab)


def _ft_tables(seq):
    assert seq == FT_N1 * FT_N2
    n_blk = FT_N2 // SUBLANES
    k1 = np.arange(FT_N1)[:, None, None, None]
    sr = np.arange(SUBLANES)[None, :, None, None]
    n1 = np.arange(FT_N1)[None, None, :, None]
    sc = np.arange(SUBLANES)[None, None, None, :]
    stage1 = np.zeros((n_blk, 2, FT_N1, SUBLANES, FT_N1, SUBLANES), np.float64)
    for blk in range(n_blk):
        n = FT_N2 * n1 + SUBLANES * blk + sr
        ang = 2.0 * np.pi * ((k1 * n) % seq) / seq
        eye = (sr == sc)
        stage1[blk, 0] = np.cos(ang) * eye
        stage1[blk, 1] = -np.sin(ang) * eye
    stage1 = stage1.reshape(n_blk, 2 * FT_N1 * SUBLANES, FT_N1 * SUBLANES)
    a = np.arange(FT_N2)
    ang2 = 2.0 * np.pi * ((a[:, None] * a[None, :]) % FT_N2) / FT_N2
    c2, s2 = np.cos(ang2), np.sin(ang2)
    stage2 = np.block([[c2, s2], [-s2, c2]])
    g = np.arange(FT_GROUP_DIM)
    angc = 2.0 * np.pi * ((g[:, None] * g[None, :]) % FT_GROUP_DIM) / FT_GROUP_DIM
    norm = 1.0 / np.sqrt(seq * FT_GROUP_DIM)
    chan = np.concatenate([np.cos(angc), np.sin(angc)], axis=0) * norm
    return (jnp.asarray(stage1, bf16), jnp.asarray(stage2, bf16), jnp.asarray(chan, bf16))


def _ft_stage1_kernel(u_ref, m_ref, z_ref):
    rows = FT_N1 * SUBLANES
    u = _unpack_bf16_pairs(u_ref[0].reshape(rows, FT_WIDTH // 2)).astype(bf16)
    z = jnp.dot(m_ref[0], u, preferred_element_type=f32)
    z_ref[0] = _pack2(z[:rows], z[rows:]).reshape(FT_N1, SUBLANES, FT_WIDTH)


def _ft_stage2_kernel(z_ref, s2_ref, cs_ref, y_ref):
    gd = FT_GROUP_DIM
    for kk in range(FT_K1_BLOCK):
        zz = jnp.concatenate(_unpack2(z_ref[0, kk]), axis=0).astype(bf16)
        x = jnp.dot(s2_ref[...], zz, preferred_element_type=f32)
        outs = []
        for g in range(FT_GROUPS):
            xg = jnp.concatenate([x[:FT_N2, g * gd:(g + 1) * gd], x[FT_N2:, g * gd:(g + 1) * gd]], axis=1)
            outs.append(jnp.dot(xg.astype(bf16), cs_ref[...], preferred_element_type=f32))
        y_ref[0, kk] = jnp.concatenate(outs, axis=1).astype(y_ref.dtype)


def _fourier_mix(u_packed, tables):
    u = u_packed
    b, s, _ = u.shape
    c = FT_WIDTH
    stage1, stage2, chan = tables
    n_blk = FT_N2 // SUBLANES
    z = pl.pallas_call(
        _ft_stage1_kernel,
        grid=(n_blk, b),
        in_specs=[pl.BlockSpec((1, FT_N1, SUBLANES, c // 2), lambda j, bi: (bi, 0, j, 0)),
                  pl.BlockSpec((1,) + stage1.shape[1:], lambda j, bi: (j, 0, 0))],
        out_specs=pl.BlockSpec((1, FT_N1, SUBLANES, c), lambda j, bi: (bi, 0, j, 0)),
        out_shape=jax.ShapeDtypeStruct((b, FT_N1, FT_N2, c), u32),
        compiler_params=_params("parallel", "parallel"),
        name="fourier_stage1",
    )(u.reshape(b, FT_N1, FT_N2, c // 2), stage1)
    y = pl.pallas_call(
        _ft_stage2_kernel,
        grid=(b, FT_N1 // FT_K1_BLOCK),
        in_specs=[pl.BlockSpec((1, FT_K1_BLOCK, FT_N2, c), lambda bi, kb: (bi, kb, 0, 0)),
                  pl.BlockSpec(stage2.shape, lambda bi, kb: (0, 0)),
                  pl.BlockSpec(chan.shape, lambda bi, kb: (0, 0))],
        out_specs=pl.BlockSpec((1, FT_K1_BLOCK, FT_N2, c), lambda bi, kb: (bi, kb, 0, 0)),
        out_shape=jax.ShapeDtypeStruct((b, FT_N1, FT_N2, c), bf16),
        compiler_params=_params("parallel", "parallel"),
        name="fourier_stage2",
    )(z, stage2, chan)
    return y.transpose(0, 2, 1, 3).reshape(b, s, c)


def _mem_kv_kernel(mem_ref, g_ref, w_ref, k_ref, v_ref):
    mn = _rms_scale(mem_ref[0]) * g_ref[...]
    kv = jnp.dot(mn.astype(bf16), w_ref[...], preferred_element_type=f32)
    k_ref[0] = kv[:, :MEM_WIDTH].astype(bf16)
    v_ref[0] = kv[:, MEM_WIDTH:].astype(bf16)


def _mem_kv(mem, g_mem, w_kv_bf16):
    b, m, d = mem.shape
    kv_spec = pl.BlockSpec((1, m, MEM_WIDTH), lambda bi: (bi, 0, 0))
    return pl.pallas_call(
        _mem_kv_kernel,
        grid=(b,),
        in_specs=[pl.BlockSpec((1, m, d), lambda bi: (bi, 0, 0)), pl.BlockSpec((1, d), lambda bi: (0, 0)),
                  pl.BlockSpec(w_kv_bf16.shape, lambda bi: (0, 0))],
        out_specs=[kv_spec, kv_spec],
        out_shape=[jax.ShapeDtypeStruct((b, m, MEM_WIDTH), bf16)] * 2,
        compiler_params=_params("parallel"),
        name="mem_kv",
    )(mem, g_mem.reshape(1, d), w_kv_bf16)


def _mix_out_kernel(x_ref, yna_ref, yft_ref, qm_ref, km_ref, vm_ref, ggrp_ref, wout_ref, gffn_ref, rw_ref,
                    rb_ref, x1_ref, h2p_ref, eidx_ref, gate_ref, rank_ref, cnt_ref, carry_ref):
    tm = x_ref.shape[0]

    @pl.when(pl.program_id(0) == 0)
    def _():
        carry_ref[...] = jnp.zeros_like(carry_ref)

    q = qm_ref[...]
    km = km_ref[0]
    vm = vm_ref[0]
    heads = []
    for h in range(MEM_HEADS):
        sl = slice(h * MEM_HEAD_DIM, (h + 1) * MEM_HEAD_DIM)
        s = lax.dot_general(q[:, sl], km[:, sl], (((1,), (1,)), ((), ())), preferred_element_type=f32)
        p = _softmax_rows(s * (MEM_HEAD_DIM ** -0.5))
        heads.append(jnp.dot(p.astype(bf16), vm[:, sl], preferred_element_type=f32))
    ymem = jnp.concatenate(heads, axis=1)

    g = ggrp_ref[...]
    a, c = NA_WIDTH, NA_WIDTH + FT_WIDTH
    y = jnp.concatenate([_rms_scale(yna_ref[...].astype(f32)) * g[:, :a],
                         _rms_scale(yft_ref[...].astype(f32)) * g[:, a:c],
                         _rms_scale(ymem) * g[:, c:]], axis=1)
    x1 = x_ref[...] + jnp.dot(y.astype(bf16), wout_ref[...], preferred_element_type=f32)
    x1_ref[...] = x1
    h2 = _rms_scale(x1) * gffn_ref[...]
    h2p_ref[...] = _pack_bf16_pairs(h2)

    h_hi = h2.astype(bf16)
    h_lo = (h2 - h_hi.astype(f32)).astype(bf16)
    hh = jnp.dot(h_hi, rw_ref[...], preferred_element_type=f32)
    logits = (hh[:, :LANES] + hh[:, LANES:]
              + jnp.dot(h_lo, rw_ref[:, :LANES], preferred_element_type=f32)) + rb_ref[...]
    l = logits.T[:N_EXPERTS]
    row = lax.broadcasted_iota(i32, (N_EXPERTS, tm), 0).astype(f32)
    vals, idxs, sels = [], [], []
    for _ in range(TOP_K):
        m = jnp.max(l, axis=0, keepdims=True)
        idx = jnp.min(jnp.where(l == m, row, float(N_EXPERTS)), axis=0, keepdims=True)
        sel = row == idx
        vals.append(m)
        idxs.append(idx)
        sels.append(sel)
        l = jnp.where(sel, -jnp.inf, l)
    ex = [jnp.exp(v - vals[0]) for v in vals]
    den = ex[0] + ex[1] + ex[2] + ex[3]

    onehot = (sels[0] | sels[1] | sels[2] | sels[3]).astype(f32)
    earlier = (lax.broadcasted_iota(i32, (tm, tm), 0) < lax.broadcasted_iota(i32, (tm, tm), 1)).astype(bf16)
    before = jnp.dot(onehot.astype(bf16), earlier, preferred_element_type=f32) + carry_ref[...]
    ranks = [jnp.sum(jnp.where(sel, before, 0.0), axis=0, keepdims=True) for sel in sels]
    carry_ref[...] = carry_ref[...] + jnp.sum(onehot, axis=1, keepdims=True)
    cnt_ref[...] = carry_ref[...].astype(i32)

    eidx_ref[...] = jnp.concatenate(idxs, axis=0).astype(i32)
    rank_ref[...] = jnp.concatenate(ranks, axis=0).astype(i32)
    gates_t = jnp.concatenate([e / den for e in ex] + [jnp.zeros((LANES - TOP_K, tm), f32)], axis=0)
    gate_ref[...] = gates_t.T[:, :TOP_K]


def _mix_out(x2d, y_na, y_ft, q_mem, k_mem, v_mem, g_grp, w_out_bf16, g_ffn, router_w2, router_b, seq,
             tile0, n_tiles):
    d = x2d.shape[1]
    tm = TOKEN_TILE
    t = n_tiles * tm
    steps_per_batch = seq // tm
    m = k_mem.shape[1]
    row_in = lambda w: pl.BlockSpec((tm, w), lambda i: (i + tile0, 0))
    row_out = lambda w: pl.BlockSpec((tm, w), lambda i: (i, 0))
    full = lambda a: pl.BlockSpec(a.shape, lambda i: (0,) * a.ndim)
    kv_spec = pl.BlockSpec((1, m, MEM_WIDTH), lambda i: ((i + tile0) // steps_per_batch, 0, 0))
    g_grp2, g_ffn2 = g_grp.reshape(1, -1), g_ffn.reshape(1, d)
    rb2 = jnp.pad(router_b.reshape(1, N_EXPERTS), ((0, 0), (0, LANES - N_EXPERTS)))
    router_w2 = jnp.pad(router_w2, ((0, 0), (0, 0), (0, LANES - N_EXPERTS)))
    router_w2 = jnp.concatenate([router_w2[0], router_w2[1]], axis=1)
    col_out = pl.BlockSpec((TOP_K, tm), lambda i: (0, i))
    return pl.pallas_call(
        _mix_out_kernel,
        grid=(n_tiles,),
        in_specs=[row_in(d), row_in(NA_WIDTH), row_in(FT_WIDTH), row_in(MEM_WIDTH), kv_spec, kv_spec,
                  full(g_grp2), full(w_out_bf16), full(g_ffn2), full(router_w2), full(rb2)],
        out_specs=[row_out(d), row_out(d // 2), col_out, row_out(TOP_K), col_out,
                   pl.BlockSpec((N_EXPERTS, 1), lambda i: (0, 0))],
        out_shape=[jax.ShapeDtypeStruct((t, d), f32), jax.ShapeDtypeStruct((t, d // 2), u32),
                   jax.ShapeDtypeStruct((TOP_K, t), i32), jax.ShapeDtypeStruct((t, TOP_K), f32),
                   jax.ShapeDtypeStruct((TOP_K, t), i32), jax.ShapeDtypeStruct((N_EXPERTS, 1), i32)],
        scratch_shapes=[pltpu.VMEM((N_EXPERTS, 1), f32)],
        compiler_params=_params("arbitrary"),
        name="mix_out_router",
    )(x2d, y_na, y_ft, q_mem, k_mem, v_mem, g_grp2, w_out_bf16, g_ffn2, router_w2, rb2)


def _sc_mesh():
    return plsc.VectorSubcoreMesh(core_axis_name="c", subcore_axis_name="s",
                                  num_cores=SC_CORES, num_subcores=SC_SUBCORES)


def _dispatch(h2p, dest, n_slots):
    t, w = h2p.shape
    workers = SC_CORES * SC_SUBCORES
    chunk = SC_GATHER_CHUNK
    per = t // workers
    steps = per // chunk
    assert per * workers == t and steps * chunk == per and steps % 2 == 0
    idx = dest.reshape(TOP_K, workers, steps, chunk)

    def body(h_hbm, idx_hbm, out_hbm, idx_v, rows_v, rsem, ssem):
        wid = lax.axis_index("s") * SC_CORES + lax.axis_index("c")
        base = wid * per
        for j in range(TOP_K):
            pltpu.sync_copy(idx_hbm.at[j, wid], idx_v.at[j])

        def read(c, slot):
            return pltpu.make_async_copy(h_hbm.at[pl.ds(base + c * chunk, chunk)], rows_v.at[slot], rsem.at[slot])

        def scatters(c, slot):
            return [pltpu.make_async_copy(rows_v.at[slot], out_hbm.at[idx_v.at[j, c]], ssem.at[slot])
                    for j in range(TOP_K)]

        read(0, 0).start()

        @pl.loop(0, steps, step=2)
        def _(c0):
            for slot in range(2):
                c = c0 + slot
                read(c, slot).wait()
                for cp in scatters(c, slot):
                    cp.start()

                @pl.when(c >= 1)
                def _():
                    for cp in scatters(c - 1, 1 - slot):
                        cp.wait()

                @pl.when(c + 1 < steps)
                def _():
                    read(c + 1, 1 - slot).start()

        for cp in scatters(steps - 1, 1):
            cp.wait()

    return pl.kernel(
        body,
        out_type=jax.ShapeDtypeStruct((n_slots, w), h2p.dtype),
        mesh=_sc_mesh(),
        scratch_types=[pltpu.VMEM((TOP_K, steps, chunk), i32), pltpu.VMEM((2, chunk, w), h2p.dtype),
                       pltpu.SemaphoreType.DMA((2,)), pltpu.SemaphoreType.DMA((2,))],
        name="sc_dispatch_rows",
    )(h2p, idx)


def _expert_kernel(blk_e_ref, blk_cnt_ref, nxt_e_ref, xs_ref, wgu_hbm, bgu_ref, wd_hbm, bd_ref, y_ref,
                   wgu_f32, wd_f32, wgu_bf, wd_bf, sem):
    b = pl.program_id(0)
    e = blk_e_ref[b]
    cnt = blk_cnt_ref[b]
    bm = xs_ref.shape[0]
    de = wd_f32.shape[0]

    def fetch(expert):
        return (pltpu.make_async_copy(wgu_hbm.at[expert], wgu_f32, sem.at[0]),
                pltpu.make_async_copy(wd_hbm.at[expert], wd_f32, sem.at[1]))

    @pl.when(b == 0)
    def _():
        for cp in fetch(e):
            cp.start()

    @pl.when(jnp.logical_or(b == 0, e != blk_e_ref[jnp.maximum(b - 1, 0)]))
    def _():
        for cp in fetch(e):
            cp.wait()

        def convert(i, carry):
            rows = pl.ds(pl.multiple_of(i * WEIGHT_CAST_ROWS, WEIGHT_CAST_ROWS), WEIGHT_CAST_ROWS)
            wgu_bf[rows, :] = wgu_f32[rows, :].astype(bf16)
            wd_bf[rows, :] = wd_f32[rows, :].astype(bf16)
            return carry

        lax.fori_loop(0, wgu_f32.shape[0] // WEIGHT_CAST_ROWS, convert, 0)

        @pl.when(nxt_e_ref[b] >= 0)
        def _():
            for cp in fetch(nxt_e_ref[b]):
                cp.start()

    def ffn_rows(r0, n):
        rows = pl.ds(r0, n)
        valid = r0 + lax.broadcasted_iota(i32, (n, 1), 0) < cnt
        x = jnp.where(valid, _unpack_bf16_pairs(xs_ref[rows, :]), 0.0).astype(bf16)
        gu = jnp.dot(x, wgu_bf[...], preferred_element_type=f32) + bgu_ref[0]
        x_glu = jnp.minimum(gu[:, :de], SWIGLU_LIMIT)
        x_lin = jnp.clip(gu[:, de:], -SWIGLU_LIMIT, SWIGLU_LIMIT)
        act = x_glu * (1.0 / (1.0 + jnp.exp(-SWIGLU_ALPHA * x_glu))) * (x_lin + 1.0)
        y = jnp.dot(act.astype(bf16), wd_bf[...], preferred_element_type=f32) + bd_ref[0]
        y_ref[rows, :] = _pack_bf16_pairs(y)

    @pl.when(cnt == bm)
    def _():
        ffn_rows(0, bm)

    @pl.when(cnt < bm)
    def _():
        y_ref[...] = jnp.zeros_like(y_ref)

        def piece(i, carry):
            ffn_rows(pl.multiple_of(i * MOE_SUB_BLOCK, MOE_SUB_BLOCK), MOE_SUB_BLOCK)
            return carry

        lax.fori_loop(0, (cnt + MOE_SUB_BLOCK - 1) // MOE_SUB_BLOCK, piece, 0)


def _experts(xs, blk_e, blk_cnt, nxt_e, w_gu, b_gu, w_down, b_down):
    n_slots, w = xs.shape
    bm = MOE_BLOCK
    e, d, de2 = w_gu.shape
    de = w_down.shape[1]
    assert de == d, "one row loop converts both weight matrices"
    grid_spec = pltpu.PrefetchScalarGridSpec(
        num_scalar_prefetch=3,
        grid=(n_slots // bm,),
        in_specs=[pl.BlockSpec((bm, w), lambda b, be, bc, ne: (b, 0)),
                  pl.BlockSpec(memory_space=pl.ANY),
                  pl.BlockSpec((1, 1, de2), lambda b, be, bc, ne: (be[b], 0, 0)),
                  pl.BlockSpec(memory_space=pl.ANY),
                  pl.BlockSpec((1, 1, d), lambda b, be, bc, ne: (be[b], 0, 0))],
        out_specs=pl.BlockSpec((bm, w), lambda b, be, bc, ne: (b, 0)),
        scratch_shapes=[pltpu.VMEM((d, de2), f32), pltpu.VMEM((de, d), f32),
                        pltpu.VMEM((d, de2), bf16), pltpu.VMEM((de, d), bf16),
                        pltpu.SemaphoreType.DMA((2,))],
    )
    return pl.pallas_call(
        _expert_kernel,
        grid_spec=grid_spec,
        out_shape=jax.ShapeDtypeStruct((n_slots, w), u32),
        compiler_params=_params("arbitrary"),
        name="moe_experts",
    )(blk_e, blk_cnt, nxt_e, xs, w_gu, b_gu.reshape(e, 1, de2), w_down, b_down.reshape(e, 1, d))


def _sc_gather_rows(table, idx):
    n, w = idx.shape[0], table.shape[1]
    workers = SC_CORES * SC_SUBCORES
    chunk = SC_GATHER_CHUNK
    per = n // workers
    steps = per // chunk
    assert per * workers == n and steps * chunk == per and steps % 2 == 0

    def body(table_hbm, idx_hbm, out_hbm, idx_v, rows_v, gsem, wsem):
        base = (lax.axis_index("s") * SC_CORES + lax.axis_index("c")) * per
        pltpu.sync_copy(idx_hbm.at[pl.ds(base, per)], idx_v)

        def gather(c, slot):
            return pltpu.make_async_copy(table_hbm.at[idx_v.at[pl.ds(c * chunk, chunk)]], rows_v.at[slot],
                                         gsem.at[slot])

        def write(c, slot):
            return pltpu.make_async_copy(rows_v.at[slot], out_hbm.at[pl.ds(base + c * chunk, chunk)],
                                         wsem.at[slot])

        gather(0, 0).start()

        @pl.loop(0, steps, step=2)
        def _(c0):
            for slot in range(2):
                c = c0 + slot
                gather(c, slot).wait()
                write(c, slot).start()

                @pl.when(c >= 1)
                def _():
                    write(c - 1, 1 - slot).wait()

                @pl.when(c + 1 < steps)
                def _():
                    gather(c + 1, 1 - slot).start()

        write(steps - 1, 1).wait()

    return pl.kernel(
        body,
        out_type=jax.ShapeDtypeStruct((n, w), table.dtype),
        mesh=_sc_mesh(),
        scratch_types=[pltpu.VMEM((per,), i32), pltpu.VMEM((2, chunk, w), table.dtype),
                       pltpu.SemaphoreType.DMA((2,)), pltpu.SemaphoreType.DMA((2,))],
        name="sc_gather_rows",
    )(table, idx)


def _combine_kernel(x1_ref, gate_ref, gfin_ref, yg_ref, *rest):
    o_ref = rest[-1]
    acc = x1_ref[...]
    gates = gate_ref[...]
    for j in range(TOP_K):
        acc = acc + gates[:, j:j + 1] * _unpack_bf16_pairs(yg_ref[j])
    o_ref[...] = _rms_scale(acc) * gfin_ref[...]


def _combine(x1, gates, dest, y_slots, g_final, out_prev, tile0, total_tokens):
    t, d = x1.shape
    tm = TOKEN_TILE
    w = y_slots.shape[1]
    yg = _sc_gather_rows(y_slots, dest.reshape(-1)).reshape(TOP_K, t, w)
    in_specs = [pl.BlockSpec((tm, d), lambda i: (i, 0)),
                pl.BlockSpec((tm, TOP_K), lambda i: (i, 0)),
                pl.BlockSpec((1, d), lambda i: (0, 0)),
                pl.BlockSpec((TOP_K, tm, w), lambda i: (0, i, 0))]
    args = [x1, gates, g_final.reshape(1, d), yg]
    aliases = {}
    if out_prev is not None:
        in_specs.append(pl.BlockSpec(memory_space=pl.ANY))
        args.append(out_prev)
        aliases = {len(args) - 1: 0}
    return pl.pallas_call(
        _combine_kernel,
        grid=(t // tm,),
        in_specs=in_specs,
        out_specs=pl.BlockSpec((tm, d), lambda i: (i + tile0, 0)),
        out_shape=jax.ShapeDtypeStruct((total_tokens, d), f32),
        input_output_aliases=aliases,
        compiler_params=_params("parallel"),
        name="moe_combine",
    )(*args)


def _slot_layout(counts, eidx, rank, n_blocks):
    bm = MOE_BLOCK
    padded = (counts + bm - 1) // bm * bm
    padded_end = jnp.cumsum(padded)
    start = padded_end - padded
    experts = jnp.arange(N_EXPERTS, dtype=i32)
    lookup = lambda table, idx: jnp.sum(jnp.where(idx[..., None] == experts, table, 0), axis=-1)
    shape = eidx.shape
    eidx, dest = eidx.reshape(-1, LANES), rank.reshape(-1, LANES)
    for e in range(N_EXPERTS):
        dest = dest + jnp.where(eidx == e, start[e], 0)
    dest = dest.astype(i32).reshape(shape)
    blk_row = jnp.arange(n_blocks, dtype=i32) * bm
    blk_e = jnp.minimum(jnp.sum((padded_end[None, :] <= blk_row[:, None]).astype(i32), axis=1), N_EXPERTS - 1)
    blk_cnt = jnp.clip(lookup(counts, blk_e) - (blk_row - lookup(start, blk_e)), 0, bm).astype(i32)
    none = jnp.int32(N_EXPERTS)
    nxt_e = jnp.min(jnp.where(blk_e[None, :] > blk_e[:, None], blk_e[None, :], none), axis=1)
    nxt_e = jnp.where(nxt_e == none, -1, nxt_e).astype(i32)
    return dest, blk_e, blk_cnt, nxt_e


def _layer_and_final_norm(x2d, mem, seq, g_mix, g_mem, w_in, w_mem_kv, na_rel_bias, g_grp, w_out, g_ffn,
                          router_w, router_b, w_gu, b_gu, w_down, b_down, g_final):
    t, d = x2d.shape
    b = t // seq
    q_na, k_na, v_na, u_ft, q_mem = _in_proj(x2d, g_mix, w_in.astype(bf16))
    k_mem, v_mem = _mem_kv(mem, g_mem, w_mem_kv.astype(bf16))
    shape3 = lambda a: a.reshape(b, seq, a.shape[-1])
    y_na = _neighbourhood_attention(shape3(q_na), shape3(k_na), shape3(v_na), _na_bias_table(na_rel_bias))
    y_ft = _fourier_mix(shape3(u_ft), _ft_tables(seq))
    y_na, y_ft = y_na.reshape(t, -1), y_ft.reshape(t, -1)
    w_out_bf16 = w_out.astype(bf16)
    rw_hi = router_w.astype(bf16)
    router_w2 = jnp.stack([rw_hi, (router_w - rw_hi.astype(f32)).astype(bf16)])

    tiles = t // TOKEN_TILE
    assert tiles % MOE_TOKEN_GROUPS == 0
    group_tiles = tiles // MOE_TOKEN_GROUPS
    group_tokens = group_tiles * TOKEN_TILE
    n_blocks = (group_tokens * TOP_K) // MOE_BLOCK + N_EXPERTS
    out = None
    for grp in range(MOE_TOKEN_GROUPS):
        tile0 = grp * group_tiles
        x1, h2p, eidx, gates, rank, counts = _mix_out(
            x2d, y_na, y_ft, q_mem, k_mem, v_mem, g_grp, w_out_bf16, g_ffn, router_w2, router_b, seq,
            tile0, group_tiles)
        dest, blk_e, blk_cnt, nxt_e = _slot_layout(counts[:, 0], eidx, rank, n_blocks)
        xs = _dispatch(h2p, dest, n_blocks * MOE_BLOCK)
        y_slots = _experts(xs, blk_e, blk_cnt, nxt_e, w_gu, b_gu, w_down, b_down)
        out = _combine(x1, gates, dest, y_slots, g_final, out, tile0, t)
    return out


def kernel(x, mem, g_mix, g_mem, w_in, w_mem_kv, na_rel_bias, g_grp, w_out, g_ffn, router_w, router_b,
           w_gu, b_gu, w_down, b_down, g_final):
    b, seq, d = x.shape
    depth = w_in.shape[0]
    assert depth == 1, "the final norm is fused into the single layer's combine step"
    out = _layer_and_final_norm(
        x.reshape(b * seq, d), mem, seq, g_mix[0], g_mem[0], w_in[0], w_mem_kv[0], na_rel_bias[0], g_grp[0],
        w_out[0], g_ffn[0], router_w[0], router_b[0], w_gu[0], b_gu[0], w_down[0], b_down[0], g_final)
    return out.reshape(b, seq, d)
```

```python
import functools

import numpy as np
import jax
import jax.numpy as jnp
from jax import lax
from jax.experimental import pallas as pl
from jax.experimental.pallas import tpu as pltpu
from jax.experimental.pallas import tpu_sc as plsc

f32 = jnp.float32
bf16 = jnp.bfloat16
u32 = jnp.uint32
i32 = jnp.int32

GRID_W = 64
NA_HEADS = 8
NA_HEAD_DIM = 64
NA_WIN_ROWS = 8
NA_WIN_COLS = 16
FT_GROUPS = 4
FT_GROUP_DIM = 128
MEM_HEADS = 4
MEM_HEAD_DIM = 128
NA_WIDTH = NA_HEADS * NA_HEAD_DIM
FT_WIDTH = FT_GROUPS * FT_GROUP_DIM
MEM_WIDTH = MEM_HEADS * MEM_HEAD_DIM
N_EXPERTS = 32
TOP_K = 4
SWIGLU_LIMIT = 7.0
SWIGLU_ALPHA = 1.702
EPS = 1e-6

LANES = 128
SUBLANES = 8
VMEM_LIMIT_BYTES = 56 * 1024 * 1024
SC_CORES = 2
SC_SUBCORES = 16
SC_GATHER_CHUNK = 64

TOKEN_TILE = 512
MOE_BLOCK = 512
MOE_SUB_BLOCK = 128
WEIGHT_CAST_ROWS = 128
MOE_TOKEN_GROUPS = 2
NA_ROW_UNROLL = 16
FT_N1 = 64
FT_N2 = 128
FT_K1_BLOCK = 8
MASK_VALUE = -1e30


def _params(*semantics):
    return pltpu.CompilerParams(dimension_semantics=semantics, vmem_limit_bytes=VMEM_LIMIT_BYTES)


def _rms_scale(x):
    return x * lax.rsqrt(jnp.mean(x * x, axis=-1, keepdims=True) + EPS)


def _softmax_rows(s):
    p = jnp.exp(s - jnp.max(s, axis=-1, keepdims=True))
    return p / jnp.sum(p, axis=-1, keepdims=True)


def _bf16_bits(x):
    return pltpu.bitcast(x.astype(bf16).astype(f32), u32)


def _pack2(lo, hi):
    return (_bf16_bits(lo) >> 16) | (_bf16_bits(hi) & jnp.uint32(0xFFFF0000))


def _unpack2(w):
    return pltpu.bitcast(w << 16, f32), pltpu.bitcast(w & jnp.uint32(0xFFFF0000), f32)


def _pack_bf16_pairs(x):
    n = x.shape[1] // 2
    return _pack2(x[:, :n], x[:, n:])


def _unpack_bf16_pairs(w):
    return jnp.concatenate(_unpack2(w), axis=1)


def _in_proj_kernel(x_ref, g_ref, w_ref, qna_ref, kna_ref, vna_ref, uft_ref, qmem_ref):
    h = _rms_scale(x_ref[...]) * g_ref[...]
    proj = jnp.dot(h.astype(bf16), w_ref[...], preferred_element_type=f32)
    o = NA_WIDTH
    qna_ref[...] = (proj[:, :o] * (NA_HEAD_DIM ** -0.5)).astype(bf16)
    kna_ref[...] = proj[:, o:2 * o].astype(bf16)
    vna_ref[...] = proj[:, 2 * o:3 * o].astype(bf16)
    uft_ref[...] = _pack_bf16_pairs(proj[:, 3 * o:3 * o + FT_WIDTH])
    qmem_ref[...] = proj[:, 3 * o + FT_WIDTH:].astype(bf16)


def _in_proj(x2d, g_mix, w_in_bf16):
    t, d = x2d.shape
    tm = TOKEN_TILE
    row = lambda w: pl.BlockSpec((tm, w), lambda i: (i, 0))
    return pl.pallas_call(
        _in_proj_kernel,
        grid=(t // tm,),
        in_specs=[row(d), pl.BlockSpec((1, d), lambda i: (0, 0)),
                  pl.BlockSpec(w_in_bf16.shape, lambda i: (0, 0))],
        out_specs=[row(NA_WIDTH), row(NA_WIDTH), row(NA_WIDTH), row(FT_WIDTH // 2), row(MEM_WIDTH)],
        out_shape=[jax.ShapeDtypeStruct((t, NA_WIDTH), bf16)] * 3
        + [jax.ShapeDtypeStruct((t, FT_WIDTH // 2), u32), jax.ShapeDtypeStruct((t, MEM_WIDTH), bf16)],
        compiler_params=_params("parallel"),
        name="in_proj",
    )(x2d, g_mix.reshape(1, d), w_in_bf16)


def _na_bias_table(rel_bias):
    c = np.arange(GRID_W)
    dc_idx = np.clip(c[None, :] - c[:, None], -(NA_WIN_COLS - 1), NA_WIN_COLS - 1) + (NA_WIN_COLS - 1)
    col_start = np.clip(c - NA_WIN_COLS // 2, 0, GRID_W - NA_WIN_COLS)
    col_in = (c[None, :] >= col_start[:, None]) & (c[None, :] < col_start[:, None] + NA_WIN_COLS)
    pick_c = jnp.asarray(dc_idx[:, :, None] == np.arange(2 * NA_WIN_COLS - 1), f32)
    cols = jnp.einsum("hab,qcb->haqc", rel_bias.astype(f32), pick_c, precision=lax.Precision.HIGHEST)
    cols = jnp.where(col_in[None, None], cols, MASK_VALUE)
    tab = jnp.stack([jnp.concatenate([cols[:, j - s + NA_WIN_ROWS - 1] for j in range(NA_WIN_ROWS)], axis=-1)
                     for s in range(NA_WIN_ROWS)])
    return tab.reshape(NA_WIN_ROWS, NA_HEADS // 2, 2 * GRID_W, NA_WIN_ROWS * GRID_W)


def _na_kernel(q_ref, k_ref, v_ref, bias_ref, o_ref):
    rows = q_ref.shape[1] // GRID_W
    win = NA_WIN_ROWS * GRID_W
    first_head = lax.broadcasted_iota(i32, (GRID_W, 2 * NA_HEAD_DIM), 1) < NA_HEAD_DIM

    def body(it, carry):
        scores, values, q_offsets = [], [], []
        for u in range(NA_ROW_UNROLL):
            r = it * NA_ROW_UNROLL + u
            row_start = jnp.clip(r - NA_WIN_ROWS // 2, 0, rows - NA_WIN_ROWS)
            q0 = pl.multiple_of(r * GRID_W, GRID_W)
            k0 = pl.multiple_of(row_start * GRID_W, GRID_W)
            q = q_ref[0, pl.ds(q0, GRID_W), :]
            zero = jnp.zeros_like(q)
            qm = jnp.concatenate([jnp.where(first_head, q, zero), jnp.where(first_head, zero, q)], axis=0)
            s = lax.dot_general(qm, k_ref[0, pl.ds(k0, win), :], (((1,), (1,)), ((), ())),
                                preferred_element_type=f32)
            scores.append(s + bias_ref[r - row_start, 0])
            values.append(v_ref[0, pl.ds(k0, win), :])
            q_offsets.append(q0)
        s = jnp.concatenate(scores, axis=0)
        p = jnp.exp(s - jnp.max(s, axis=-1, keepdims=True))
        inv_den = 1.0 / jnp.sum(p, axis=-1, keepdims=True)
        p = p.astype(bf16)
        for u in range(NA_ROW_UNROLL):
            sl = slice(u * 2 * GRID_W, (u + 1) * 2 * GRID_W)
            o = jnp.dot(p[sl], values[u], preferred_element_type=f32) * inv_den[sl]
            o_ref[0, pl.ds(q_offsets[u], GRID_W), :] = jnp.where(
                first_head, o[:GRID_W], o[GRID_W:]).astype(o_ref.dtype)
        return carry

    lax.fori_loop(0, rows // NA_ROW_UNROLL, body, 0)


def _neighbourhood_attention(q, k, v, bias_tab):
    b, s, _ = q.shape
    pair = 2 * NA_HEAD_DIM
    qkv_spec = pl.BlockSpec((1, s, pair), lambda bi, hp: (bi, 0, hp))
    return pl.pallas_call(
        _na_kernel,
        grid=(b, NA_HEADS // 2),
        in_specs=[qkv_spec, qkv_spec, qkv_spec,
                  pl.BlockSpec((NA_WIN_ROWS, 1, 2 * GRID_W, NA_WIN_ROWS * GRID_W), lambda bi, hp: (0, hp, 0, 0))],
        out_specs=qkv_spec,
        out_shape=jax.ShapeDtypeStruct((b, s, NA_WIDTH), bf16),
        compiler_params=_params("parallel", "parallel"),
        name="neighbourhood_attention",
    )(q, k, v, bias_tab)


def _ft_tables(seq):
    assert seq == FT_N1 * FT_N2
    n_blk = FT_N2 // SUBLANES
    k1 = np.arange(FT_N1)[:, None, None, None]
    sr = np.arange(SUBLANES)[None, :, None, None]
    n1 = np.arange(FT_N1)[None, None, :, None]
    sc = np.arange(SUBLANES)[None, None, None, :]
    stage1 = np.zeros((n_blk, 2, FT_N1, SUBLANES, FT_N1, SUBLANES), np.float64)
    for blk in range(n_blk):
        n = FT_N2 * n1 + SUBLANES * blk + sr
        ang = 2.0 * np.pi * ((k1 * n) % seq) / seq
        eye = (sr == sc)
        stage1[blk, 0] = np.cos(ang) * eye
        stage1[blk, 1] = -np.sin(ang) * eye
    stage1 = stage1.reshape(n_blk, 2 * FT_N1 * SUBLANES, FT_N1 * SUBLANES)
    a = np.arange(FT_N2)
    ang2 = 2.0 * np.pi * ((a[:, None] * a[None, :]) % FT_N2) / FT_N2
    c2, s2 = np.cos(ang2), np.sin(ang2)
    stage2 = np.block([[c2, s2], [-s2, c2]])
    g = np.arange(FT_GROUP_DIM)
    angc = 2.0 * np.pi * ((g[:, None] * g[None, :]) % FT_GROUP_DIM) / FT_GROUP_DIM
    norm = 1.0 / np.sqrt(seq * FT_GROUP_DIM)
    chan = np.concatenate([np.cos(angc), np.sin(angc)], axis=0) * norm
    return (jnp.asarray(stage1, bf16), jnp.asarray(stage2, bf16), jnp.asarray(chan, bf16))


def _ft_stage1_kernel(u_ref, m_ref, z_ref):
    rows = FT_N1 * SUBLANES
    u = _unpack_bf16_pairs(u_ref[0].reshape(rows, FT_WIDTH // 2)).astype(bf16)
    z = jnp.dot(m_ref[0], u, preferred_element_type=f32)
    z_ref[0] = _pack2(z[:rows], z[rows:]).reshape(FT_N1, SUBLANES, FT_WIDTH)


def _ft_stage2_kernel(z_ref, s2_ref, cs_ref, y_ref):
    gd = FT_GROUP_DIM
    for kk in range(FT_K1_BLOCK):
        zz = jnp.concatenate(_unpack2(z_ref[0, kk]), axis=0).astype(bf16)
        x = jnp.dot(s2_ref[...], zz, preferred_element_type=f32)
        outs = []
        for g in range(FT_GROUPS):
            xg = jnp.concatenate([x[:FT_N2, g * gd:(g + 1) * gd], x[FT_N2:, g * gd:(g + 1) * gd]], axis=1)
            outs.append(jnp.dot(xg.astype(bf16), cs_ref[...], preferred_element_type=f32))
        y_ref[0, kk] = jnp.concatenate(outs, axis=1).astype(y_ref.dtype)


def _fourier_mix(u_packed, tables):
    u = u_packed
    b, s, _ = u.shape
    c = FT_WIDTH
    stage1, stage2, chan = tables
    n_blk = FT_N2 // SUBLANES
    z = pl.pallas_call(
        _ft_stage1_kernel,
        grid=(n_blk, b),
        in_specs=[pl.BlockSpec((1, FT_N1, SUBLANES, c // 2), lambda j, bi: (bi, 0, j, 0)),
                  pl.BlockSpec((1,) + stage1.shape[1:], lambda j, bi: (j, 0, 0))],
        out_specs=pl.BlockSpec((1, FT_N1, SUBLANES, c), lambda j, bi: (bi, 0, j, 0)),
        out_shape=jax.ShapeDtypeStruct((b, FT_N1, FT_N2, c), u32),
        compiler_params=_params("parallel", "parallel"),
        name="fourier_stage1",
    )(u.reshape(b, FT_N1, FT_N2, c // 2), stage1)
    y = pl.pallas_call(
        _ft_stage2_kernel,
        grid=(b, FT_N1 // FT_K1_BLOCK),
        in_specs=[pl.BlockSpec((1, FT_K1_BLOCK, FT_N2, c), lambda bi, kb: (bi, kb, 0, 0)),
                  pl.BlockSpec(stage2.shape, lambda bi, kb: (0, 0)),
                  pl.BlockSpec(chan.shape, lambda bi, kb: (0, 0))],
        out_specs=pl.BlockSpec((1, FT_K1_BLOCK, FT_N2, c), lambda bi, kb: (bi, kb, 0, 0)),
        out_shape=jax.ShapeDtypeStruct((b, FT_N1, FT_N2, c), bf16),
        compiler_params=_params("parallel", "parallel"),
        name="fourier_stage2",
    )(z, stage2, chan)
    return y.transpose(0, 2, 1, 3).reshape(b, s, c)


def _mem_kv_kernel(mem_ref, g_ref, w_ref, k_ref, v_ref):
    mn = _rms_scale(mem_ref[0]) * g_ref[...]
    kv = jnp.dot(mn.astype(bf16), w_ref[...], preferred_element_type=f32)
    k_ref[0] = kv[:, :MEM_WIDTH].astype(bf16)
    v_ref[0] = kv[:, MEM_WIDTH:].astype(bf16)


def _mem_kv(mem, g_mem, w_kv_bf16):
    b, m, d = mem.shape
    kv_spec = pl.BlockSpec((1, m, MEM_WIDTH), lambda bi: (bi, 0, 0))
    return pl.pallas_call(
        _mem_kv_kernel,
        grid=(b,),
        in_specs=[pl.BlockSpec((1, m, d), lambda bi: (bi, 0, 0)), pl.BlockSpec((1, d), lambda bi: (0, 0)),
                  pl.BlockSpec(w_kv_bf16.shape, lambda bi: (0, 0))],
        out_specs=[kv_spec, kv_spec],
        out_shape=[jax.ShapeDtypeStruct((b, m, MEM_WIDTH), bf16)] * 2,
        compiler_params=_params("parallel"),
        name="mem_kv",
    )(mem, g_mem.reshape(1, d), w_kv_bf16)


def _mix_out_kernel(x_ref, yna_ref, yft_ref, qm_ref, km_ref, vm_ref, ggrp_ref, wout_ref, gffn_ref, rw_ref,
                    rb_ref, x1_ref, h2p_ref, eidx_ref, gate_ref, rank_ref, cnt_ref, carry_ref):
    tm = x_ref.shape[0]

    @pl.when(pl.program_id(0) == 0)
    def _():
        carry_ref[...] = jnp.zeros_like(carry_ref)

    q = qm_ref[...]
    km = km_ref[0]
    vm = vm_ref[0]
    heads = []
    for h in range(MEM_HEADS):
        sl = slice(h * MEM_HEAD_DIM, (h + 1) * MEM_HEAD_DIM)
        s = lax.dot_general(q[:, sl], km[:, sl], (((1,), (1,)), ((), ())), preferred_element_type=f32)
        p = _softmax_rows(s * (MEM_HEAD_DIM ** -0.5))
        heads.append(jnp.dot(p.astype(bf16), vm[:, sl], preferred_element_type=f32))
    ymem = jnp.concatenate(heads, axis=1)

    g = ggrp_ref[...]
    a, c = NA_WIDTH, NA_WIDTH + FT_WIDTH
    y = jnp.concatenate([_rms_scale(yna_ref[...].astype(f32)) * g[:, :a],
                         _rms_scale(yft_ref[...].astype(f32)) * g[:, a:c],
                         _rms_scale(ymem) * g[:, c:]], axis=1)
    x1 = x_ref[...] + jnp.dot(y.astype(bf16), wout_ref[...], preferred_element_type=f32)
    x1_ref[...] = x1
    h2 = _rms_scale(x1) * gffn_ref[...]
    h2p_ref[...] = _pack_bf16_pairs(h2)

    h_hi = h2.astype(bf16)
    h_lo = (h2 - h_hi.astype(f32)).astype(bf16)
    hh = jnp.dot(h_hi, rw_ref[...], preferred_element_type=f32)
    logits = (hh[:, :LANES] + hh[:, LANES:]
              + jnp.dot(h_lo, rw_ref[:, :LANES], preferred_element_type=f32)) + rb_ref[...]
    l = logits.T[:N_EXPERTS]
    row = lax.broadcasted_iota(i32, (N_EXPERTS, tm), 0).astype(f32)
    vals, idxs, sels = [], [], []
    for _ in range(TOP_K):
        m = jnp.max(l, axis=0, keepdims=True)
        idx = jnp.min(jnp.where(l == m, row, float(N_EXPERTS)), axis=0, keepdims=True)
        sel = row == idx
        vals.append(m)
        idxs.append(idx)
        sels.append(sel)
        l = jnp.where(sel, -jnp.inf, l)
    ex = [jnp.exp(v - vals[0]) for v in vals]
    den = ex[0] + ex[1] + ex[2] + ex[3]

    onehot = (sels[0] | sels[1] | sels[2] | sels[3]).astype(f32)
    earlier = (lax.broadcasted_iota(i32, (tm, tm), 0) < lax.broadcasted_iota(i32, (tm, tm), 1)).astype(bf16)
    before = jnp.dot(onehot.astype(bf16), earlier, preferred_element_type=f32) + carry_ref[...]
    ranks = [jnp.sum(jnp.where(sel, before, 0.0), axis=0, keepdims=True) for sel in sels]
    carry_ref[...] = carry_ref[...] + jnp.sum(onehot, axis=1, keepdims=True)
    cnt_ref[...] = carry_ref[...].astype(i32)

    eidx_ref[...] = jnp.concatenate(idxs, axis=0).astype(i32)
    rank_ref[...] = jnp.concatenate(ranks, axis=0).astype(i32)
    gates_t = jnp.concatenate([e / den for e in ex] + [jnp.zeros((LANES - TOP_K, tm), f32)], axis=0)
    gate_ref[...] = gates_t.T[:, :TOP_K]


def _mix_out(x2d, y_na, y_ft, q_mem, k_mem, v_mem, g_grp, w_out_bf16, g_ffn, router_w2, router_b, seq,
             tile0, n_tiles):
    d = x2d.shape[1]
    tm = TOKEN_TILE
    t = n_tiles * tm
    steps_per_batch = seq // tm
    m = k_mem.shape[1]
    row_in = lambda w: pl.BlockSpec((tm, w), lambda i: (i + tile0, 0))
    row_out = lambda w: pl.BlockSpec((tm, w), lambda i: (i, 0))
    full = lambda a: pl.BlockSpec(a.shape, lambda i: (0,) * a.ndim)
    kv_spec = pl.BlockSpec((1, m, MEM_WIDTH), lambda i: ((i + tile0) // steps_per_batch, 0, 0))
    g_grp2, g_ffn2 = g_grp.reshape(1, -1), g_ffn.reshape(1, d)
    rb2 = jnp.pad(router_b.reshape(1, N_EXPERTS), ((0, 0), (0, LANES - N_EXPERTS)))
    router_w2 = jnp.pad(router_w2, ((0, 0), (0, 0), (0, LANES - N_EXPERTS)))
    router_w2 = jnp.concatenate([router_w2[0], router_w2[1]], axis=1)
    col_out = pl.BlockSpec((TOP_K, tm), lambda i: (0, i))
    return pl.pallas_call(
        _mix_out_kernel,
        grid=(n_tiles,),
        in_specs=[row_in(d), row_in(NA_WIDTH), row_in(FT_WIDTH), row_in(MEM_WIDTH), kv_spec, kv_spec,
                  full(g_grp2), full(w_out_bf16), full(g_ffn2), full(router_w2), full(rb2)],
        out_specs=[row_out(d), row_out(d // 2), col_out, row_out(TOP_K), col_out,
                   pl.BlockSpec((N_EXPERTS, 1), lambda i: (0, 0))],
        out_shape=[jax.ShapeDtypeStruct((t, d), f32), jax.ShapeDtypeStruct((t, d // 2), u32),
                   jax.ShapeDtypeStruct((TOP_K, t), i32), jax.ShapeDtypeStruct((t, TOP_K), f32),
                   jax.ShapeDtypeStruct((TOP_K, t), i32), jax.ShapeDtypeStruct((N_EXPERTS, 1), i32)],
        scratch_shapes=[pltpu.VMEM((N_EXPERTS, 1), f32)],
        compiler_params=_params("arbitrary"),
        name="mix_out_router",
    )(x2d, y_na, y_ft, q_mem, k_mem, v_mem, g_grp2, w_out_bf16, g_ffn2, router_w2, rb2)


def _sc_mesh():
    return plsc.VectorSubcoreMesh(core_axis_name="c", subcore_axis_name="s",
                                  num_cores=SC_CORES, num_subcores=SC_SUBCORES)


def _dispatch(h2p, dest, n_slots):
    t, w = h2p.shape
    workers = SC_CORES * SC_SUBCORES
    chunk = SC_GATHER_CHUNK
    per = t // workers
    steps = per // chunk
    assert per * workers == t and steps * chunk == per and steps % 2 == 0
    idx = dest.reshape(TOP_K, workers, steps, chunk)

    def body(h_hbm, idx_hbm, out_hbm, idx_v, rows_v, rsem, ssem):
        wid = lax.axis_index("s") * SC_CORES + lax.axis_index("c")
        base = wid * per
        for j in range(TOP_K):
            pltpu.sync_copy(idx_hbm.at[j, wid], idx_v.at[j])

        def read(c, slot):
            return pltpu.make_async_copy(h_hbm.at[pl.ds(base + c * chunk, chunk)], rows_v.at[slot], rsem.at[slot])

        def scatters(c, slot):
            return [pltpu.make_async_copy(rows_v.at[slot], out_hbm.at[idx_v.at[j, c]], ssem.at[slot])
                    for j in range(TOP_K)]

        read(0, 0).start()

        @pl.loop(0, steps, step=2)
        def _(c0):
            for slot in range(2):
                c = c0 + slot
                read(c, slot).wait()
                for cp in scatters(c, slot):
                    cp.start()

                @pl.when(c >= 1)
                def _():
                    for cp in scatters(c - 1, 1 - slot):
                        cp.wait()

                @pl.when(c + 1 < steps)
                def _():
                    read(c + 1, 1 - slot).start()

        for cp in scatters(steps - 1, 1):
            cp.wait()

    return pl.kernel(
        body,
        out_type=jax.ShapeDtypeStruct((n_slots, w), h2p.dtype),
        mesh=_sc_mesh(),
        scratch_types=[pltpu.VMEM((TOP_K, steps, chunk), i32), pltpu.VMEM((2, chunk, w), h2p.dtype),
                       pltpu.SemaphoreType.DMA((2,)), pltpu.SemaphoreType.DMA((2,))],
        name="sc_dispatch_rows",
    )(h2p, idx)


def _expert_kernel(blk_e_ref, blk_cnt_ref, nxt_e_ref, xs_ref, wgu_hbm, bgu_ref, wd_hbm, bd_ref, y_ref,
                   wgu_f32, wd_f32, wgu_bf, wd_bf, sem):
    b = pl.program_id(0)
    e = blk_e_ref[b]
    cnt = blk_cnt_ref[b]
    bm = xs_ref.shape[0]
    de = wd_f32.shape[0]

    def fetch(expert):
        return (pltpu.make_async_copy(wgu_hbm.at[expert], wgu_f32, sem.at[0]),
                pltpu.make_async_copy(wd_hbm.at[expert], wd_f32, sem.at[1]))

    @pl.when(b == 0)
    def _():
        for cp in fetch(e):
            cp.start()

    @pl.when(jnp.logical_or(b == 0, e != blk_e_ref[jnp.maximum(b - 1, 0)]))
    def _():
        for cp in fetch(e):
            cp.wait()

        def convert(i, carry):
            rows = pl.ds(pl.multiple_of(i * WEIGHT_CAST_ROWS, WEIGHT_CAST_ROWS), WEIGHT_CAST_ROWS)
            wgu_bf[rows, :] = wgu_f32[rows, :].astype(bf16)
            wd_bf[rows, :] = wd_f32[rows, :].astype(bf16)
            return carry

        lax.fori_loop(0, wgu_f32.shape[0] // WEIGHT_CAST_ROWS, convert, 0)

        @pl.when(nxt_e_ref[b] >= 0)
        def _():
            for cp in fetch(nxt_e_ref[b]):
                cp.start()

    def ffn_rows(r0, n):
        rows = pl.ds(r0, n)
        valid = r0 + lax.broadcasted_iota(i32, (n, 1), 0) < cnt
        x = jnp.where(valid, _unpack_bf16_pairs(xs_ref[rows, :]), 0.0).astype(bf16)
        gu = jnp.dot(x, wgu_bf[...], preferred_element_type=f32) + bgu_ref[0]
        x_glu = jnp.minimum(gu[:, :de], SWIGLU_LIMIT)
        x_lin = jnp.clip(gu[:, de:], -SWIGLU_LIMIT, SWIGLU_LIMIT)
        act = x_glu * (1.0 / (1.0 + jnp.exp(-SWIGLU_ALPHA * x_glu))) * (x_lin + 1.0)
        y = jnp.dot(act.astype(bf16), wd_bf[...], preferred_element_type=f32) + bd_ref[0]
        y_ref[rows, :] = _pack_bf16_pairs(y)

    @pl.when(cnt == bm)
    def _():
        ffn_rows(0, bm)

    @pl.when(cnt < bm)
    def _():
        y_ref[...] = jnp.zeros_like(y_ref)

        def piece(i, carry):
            ffn_rows(pl.multiple_of(i * MOE_SUB_BLOCK, MOE_SUB_BLOCK), MOE_SUB_BLOCK)
            return carry

        lax.fori_loop(0, (cnt + MOE_SUB_BLOCK - 1) // MOE_SUB_BLOCK, piece, 0)


def _experts(xs, blk_e, blk_cnt, nxt_e, w_gu, b_gu, w_down, b_down):
    n_slots, w = xs.shape
    bm = MOE_BLOCK
    e, d, de2 = w_gu.shape
    de = w_down.shape[1]
    assert de == d, "one row loop converts both weight matrices"
    grid_spec = pltpu.PrefetchScalarGridSpec(
        num_scalar_prefetch=3,
        grid=(n_slots // bm,),
        in_specs=[pl.BlockSpec((bm, w), lambda b, be, bc, ne: (b, 0)),
                  pl.BlockSpec(memory_space=pl.ANY),
                  pl.BlockSpec((1, 1, de2), lambda b, be, bc, ne: (be[b], 0, 0)),
                  pl.BlockSpec(memory_space=pl.ANY),
                  pl.BlockSpec((1, 1, d), lambda b, be, bc, ne: (be[b], 0, 0))],
        out_specs=pl.BlockSpec((bm, w), lambda b, be, bc, ne: (b, 0)),
        scratch_shapes=[pltpu.VMEM((d, de2), f32), pltpu.VMEM((de, d), f32),
                        pltpu.VMEM((d, de2), bf16), pltpu.VMEM((de, d), bf16),
                        pltpu.SemaphoreType.DMA((2,))],
    )
    return pl.pallas_call(
        _expert_kernel,
        grid_spec=grid_spec,
        out_shape=jax.ShapeDtypeStruct((n_slots, w), u32),
        compiler_params=_params("arbitrary"),
        name="moe_experts",
    )(blk_e, blk_cnt, nxt_e, xs, w_gu, b_gu.reshape(e, 1, de2), w_down, b_down.reshape(e, 1, d))


def _sc_gather_rows(table, idx):
    n, w = idx.shape[0], table.shape[1]
    workers = SC_CORES * SC_SUBCORES
    chunk = SC_GATHER_CHUNK
    per = n // workers
    steps = per // chunk
    assert per * workers == n and steps * chunk == per and steps % 2 == 0

    def body(table_hbm, idx_hbm, out_hbm, idx_v, rows_v, gsem, wsem):
        base = (lax.axis_index("s") * SC_CORES + lax.axis_index("c")) * per
        pltpu.sync_copy(idx_hbm.at[pl.ds(base, per)], idx_v)

        def gather(c, slot):
            return pltpu.make_async_copy(table_hbm.at[idx_v.at[pl.ds(c * chunk, chunk)]], rows_v.at[slot],
                                         gsem.at[slot])

        def write(c, slot):
            return pltpu.make_async_copy(rows_v.at[slot], out_hbm.at[pl.ds(base + c * chunk, chunk)],
                                         wsem.at[slot])

        gather(0, 0).start()

        @pl.loop(0, steps, step=2)
        def _(c0):
            for slot in range(2):
                c = c0 + slot
                gather(c, slot).wait()
                write(c, slot).start()

                @pl.when(c >= 1)
                def _():
                    write(c - 1, 1 - slot).wait()

                @pl.when(c + 1 < steps)
                def _():
                    gather(c + 1, 1 - slot).start()

        write(steps - 1, 1).wait()

    return pl.kernel(
        body,
        out_type=jax.ShapeDtypeStruct((n, w), table.dtype),
        mesh=_sc_mesh(),
        scratch_types=[pltpu.VMEM((per,), i32), pltpu.VMEM((2, chunk, w), table.dtype),
                       pltpu.SemaphoreType.DMA((2,)), pltpu.SemaphoreType.DMA((2,))],
        name="sc_gather_rows",
    )(table, idx)


def _combine_kernel(x1_ref, gate_ref, gfin_ref, yg_ref, *rest):
    o_ref = rest[-1]
    acc = x1_ref[...]
    gates = gate_ref[...]
    for j in range(TOP_K):
        acc = acc + gates[:, j:j + 1] * _unpack_bf16_pairs(yg_ref[j])
    o_ref[...] = _rms_scale(acc) * gfin_ref[...]


def _combine(x1, gates, dest, y_slots, g_final, out_prev, tile0, total_tokens):
    t, d = x1.shape
    tm = TOKEN_TILE
    w = y_slots.shape[1]
    yg = _sc_gather_rows(y_slots, dest.reshape(-1)).reshape(TOP_K, t, w)
    in_specs = [pl.BlockSpec((tm, d), lambda i: (i, 0)),
                pl.BlockSpec((tm, TOP_K), lambda i: (i, 0)),
                pl.BlockSpec((1, d), lambda i: (0, 0)),
                pl.BlockSpec((TOP_K, tm, w), lambda i: (0, i, 0))]
    args = [x1, gates, g_final.reshape(1, d), yg]
    aliases = {}
    if out_prev is not None:
        in_specs.append(pl.BlockSpec(memory_space=pl.ANY))
        args.append(out_prev)
        aliases = {len(args) - 1: 0}
    return pl.pallas_call(
        _combine_kernel,
        grid=(t // tm,),
        in_specs=in_specs,
        out_specs=pl.BlockSpec((tm, d), lambda i: (i + tile0, 0)),
        out_shape=jax.ShapeDtypeStruct((total_tokens, d), f32),
        input_output_aliases=aliases,
        compiler_params=_params("parallel"),
        name="moe_combine",
    )(*args)


def _dest_kernel(start_ref, eidx_ref, rank_ref, dest_ref):
    eidx = eidx_ref[...]
    dest = rank_ref[...]
    for e in range(N_EXPERTS):
        dest = dest + jnp.where(eidx == e, start_ref[e], 0)
    dest_ref[...] = dest


def _slot_layout(counts, eidx, rank, n_blocks):
    bm = MOE_BLOCK
    padded = (counts + bm - 1) // bm * bm
    padded_end = jnp.cumsum(padded)
    start = padded_end - padded
    experts = jnp.arange(N_EXPERTS, dtype=i32)
    lookup = lambda table, idx: jnp.sum(jnp.where(idx[..., None] == experts, table, 0), axis=-1)
    dest = pl.pallas_call(
        _dest_kernel,
        grid_spec=pltpu.PrefetchScalarGridSpec(
            num_scalar_prefetch=1, grid=(1,),
            in_specs=[pl.BlockSpec(eidx.shape, lambda i, st: (0, 0)), pl.BlockSpec(rank.shape, lambda i, st: (0, 0))],
            out_specs=pl.BlockSpec(rank.shape, lambda i, st: (0, 0))),
        out_shape=jax.ShapeDtypeStruct(rank.shape, i32),
        name="moe_dest",
    )(start.astype(i32), eidx, rank)
    blk_row = jnp.arange(n_blocks, dtype=i32) * bm
    blk_e = jnp.minimum(jnp.sum((padded_end[None, :] <= blk_row[:, None]).astype(i32), axis=1), N_EXPERTS - 1)
    blk_cnt = jnp.clip(lookup(counts, blk_e) - (blk_row - lookup(start, blk_e)), 0, bm).astype(i32)
    none = jnp.int32(N_EXPERTS)
    nxt_e = jnp.min(jnp.where(blk_e[None, :] > blk_e[:, None], blk_e[None, :], none), axis=1)
    nxt_e = jnp.where(nxt_e == none, -1, nxt_e).astype(i32)
    return dest, blk_e, blk_cnt, nxt_e


def _layer_and_final_norm(x2d, mem, seq, g_mix, g_mem, w_in, w_mem_kv, na_rel_bias, g_grp, w_out, g_ffn,
                          router_w, router_b, w_gu, b_gu, w_down, b_down, g_final):
    t, d = x2d.shape
    b = t // seq
    q_na, k_na, v_na, u_ft, q_mem = _in_proj(x2d, g_mix, w_in.astype(bf16))
    k_mem, v_mem = _mem_kv(mem, g_mem, w_mem_kv.astype(bf16))
    shape3 = lambda a: a.reshape(b, seq, a.shape[-1])
    y_na = _neighbourhood_attention(shape3(q_na), shape3(k_na), shape3(v_na), _na_bias_table(na_rel_bias))
    y_ft = _fourier_mix(shape3(u_ft), _ft_tables(seq))
    y_na, y_ft = y_na.reshape(t, -1), y_ft.reshape(t, -1)
    w_out_bf16 = w_out.astype(bf16)
    rw_hi = router_w.astype(bf16)
    router_w2 = jnp.stack([rw_hi, (router_w - rw_hi.astype(f32)).astype(bf16)])

    tiles = t // TOKEN_TILE
    assert tiles % MOE_TOKEN_GROUPS == 0
    group_tiles = tiles // MOE_TOKEN_GROUPS
    group_tokens = group_tiles * TOKEN_TILE
    n_blocks = (group_tokens * TOP_K) // MOE_BLOCK + N_EXPERTS
    out = None
    for grp in range(MOE_TOKEN_GROUPS):
        tile0 = grp * group_tiles
        x1, h2p, eidx, gates, rank, counts = _mix_out(
            x2d, y_na, y_ft, q_mem, k_mem, v_mem, g_grp, w_out_bf16, g_ffn, router_w2, router_b, seq,
            tile0, group_tiles)
        dest, blk_e, blk_cnt, nxt_e = _slot_layout(counts[:, 0], eidx, rank, n_blocks)
        xs = _dispatch(h2p, dest, n_blocks * MOE_BLOCK)
        y_slots = _experts(xs, blk_e, blk_cnt, nxt_e, w_gu, b_gu, w_down, b_down)
        out = _combine(x1, gates, dest, y_slots, g_final, out, tile0, t)
    return out


def kernel(x, mem, g_mix, g_mem, w_in, w_mem_kv, na_rel_bias, g_grp, w_out, g_ffn, router_w, router_b,
           w_gu, b_gu, w_down, b_down, g_final):
    b, seq, d = x.shape
    depth = w_in.shape[0]
    assert depth == 1, "the final norm is fused into the single layer's combine step"
    out = _layer_and_final_norm(
        x.reshape(b * seq, d), mem, seq, g_mix[0], g_mem[0], w_in[0], w_mem_kv[0], na_rel_bias[0], g_grp[0],
        w_out[0], g_ffn[0], router_w[0], router_b[0], w_gu[0], b_gu[0], w_down[0], b_down[0], g_final)
    return out.reshape(b, seq, d)
```

```python
import functools

import numpy as np
import jax
import jax.numpy as jnp
from jax import lax
from jax.experimental import pallas as pl
from jax.experimental.pallas import tpu as pltpu
from jax.experimental.pallas import tpu_sc as plsc

f32 = jnp.float32
bf16 = jnp.bfloat16
u32 = jnp.uint32
i32 = jnp.int32

GRID_W = 64
NA_HEADS = 8
NA_HEAD_DIM = 64
NA_WIN_ROWS = 8
NA_WIN_COLS = 16
FT_GROUPS = 4
FT_GROUP_DIM = 128
MEM_HEADS = 4
MEM_HEAD_DIM = 128
NA_WIDTH = NA_HEADS * NA_HEAD_DIM
FT_WIDTH = FT_GROUPS * FT_GROUP_DIM
MEM_WIDTH = MEM_HEADS * MEM_HEAD_DIM
N_EXPERTS = 32
TOP_K = 4
SWIGLU_LIMIT = 7.0
SWIGLU_ALPHA = 1.702
EPS = 1e-6

LANES = 128
SUBLANES = 8
VMEM_LIMIT_BYTES = 56 * 1024 * 1024
SC_CORES = 2
SC_SUBCORES = 16
SC_GATHER_CHUNK = 64

TOKEN_TILE = 512
MOE_BLOCK = 512
MOE_SUB_BLOCK = 128
WEIGHT_CAST_ROWS = 128
MOE_GROUP_SHARES = (3, 5)
NA_ROW_UNROLL = 16
FT_N1 = 64
FT_N2 = 128
FT_K1_BLOCK = 8
MASK_VALUE = -1e30


def _params(*semantics):
    return pltpu.CompilerParams(dimension_semantics=semantics, vmem_limit_bytes=VMEM_LIMIT_BYTES)


def _rms_scale(x):
    return x * lax.rsqrt(jnp.mean(x * x, axis=-1, keepdims=True) + EPS)


def _softmax_rows(s):
    p = jnp.exp(s - jnp.max(s, axis=-1, keepdims=True))
    return p / jnp.sum(p, axis=-1, keepdims=True)


def _bf16_bits(x):
    return pltpu.bitcast(x.astype(bf16).astype(f32), u32)


def _pack2(lo, hi):
    return (_bf16_bits(lo) >> 16) | (_bf16_bits(hi) & jnp.uint32(0xFFFF0000))


def _unpack2(w):
    return pltpu.bitcast(w << 16, f32), pltpu.bitcast(w & jnp.uint32(0xFFFF0000), f32)


def _pack_bf16_pairs(x):
    n = x.shape[1] // 2
    return _pack2(x[:, :n], x[:, n:])


def _unpack_bf16_pairs(w):
    return jnp.concatenate(_unpack2(w), axis=1)


def _in_proj_kernel(x_ref, g_ref, w_ref, qna_ref, kna_ref, vna_ref, uft_ref, qmem_ref):
    h = _rms_scale(x_ref[...]) * g_ref[...]
    proj = jnp.dot(h.astype(bf16), w_ref[...], preferred_element_type=f32)
    o = NA_WIDTH
    qna_ref[...] = (proj[:, :o] * (NA_HEAD_DIM ** -0.5)).astype(bf16)
    kna_ref[...] = proj[:, o:2 * o].astype(bf16)
    vna_ref[...] = proj[:, 2 * o:3 * o].astype(bf16)
    uft_ref[...] = _pack_bf16_pairs(proj[:, 3 * o:3 * o + FT_WIDTH])
    qmem_ref[...] = proj[:, 3 * o + FT_WIDTH:].astype(bf16)


def _in_proj(x2d, g_mix, w_in_bf16):
    t, d = x2d.shape
    tm = TOKEN_TILE
    row = lambda w: pl.BlockSpec((tm, w), lambda i: (i, 0))
    return pl.pallas_call(
        _in_proj_kernel,
        grid=(t // tm,),
        in_specs=[row(d), pl.BlockSpec((1, d), lambda i: (0, 0)),
                  pl.BlockSpec(w_in_bf16.shape, lambda i: (0, 0))],
        out_specs=[row(NA_WIDTH), row(NA_WIDTH), row(NA_WIDTH), row(FT_WIDTH // 2), row(MEM_WIDTH)],
        out_shape=[jax.ShapeDtypeStruct((t, NA_WIDTH), bf16)] * 3
        + [jax.ShapeDtypeStruct((t, FT_WIDTH // 2), u32), jax.ShapeDtypeStruct((t, MEM_WIDTH), bf16)],
        compiler_params=_params("parallel"),
        name="in_proj",
    )(x2d, g_mix.reshape(1, d), w_in_bf16)


def _na_bias_table(rel_bias):
    c = np.arange(GRID_W)
    dc_idx = np.clip(c[None, :] - c[:, None], -(NA_WIN_COLS - 1), NA_WIN_COLS - 1) + (NA_WIN_COLS - 1)
    col_start = np.clip(c - NA_WIN_COLS // 2, 0, GRID_W - NA_WIN_COLS)
    col_in = (c[None, :] >= col_start[:, None]) & (c[None, :] < col_start[:, None] + NA_WIN_COLS)
    pick_c = jnp.asarray(dc_idx[:, :, None] == np.arange(2 * NA_WIN_COLS - 1), f32)
    cols = jnp.einsum("hab,qcb->haqc", rel_bias.astype(f32), pick_c, precision=lax.Precision.HIGHEST)
    cols = jnp.where(col_in[None, None], cols, MASK_VALUE)
    tab = jnp.stack([jnp.concatenate([cols[:, j - s + NA_WIN_ROWS - 1] for j in range(NA_WIN_ROWS)], axis=-1)
                     for s in range(NA_WIN_ROWS)])
    return tab.reshape(NA_WIN_ROWS, NA_HEADS // 2, 2 * GRID_W, NA_WIN_ROWS * GRID_W)


def _na_kernel(q_ref, k_ref, v_ref, bias_ref, o_ref):
    rows = q_ref.shape[1] // GRID_W
    win = NA_WIN_ROWS * GRID_W
    first_head = lax.broadcasted_iota(i32, (GRID_W, 2 * NA_HEAD_DIM), 1) < NA_HEAD_DIM

    def body(it, carry):
        scores, values, q_offsets = [], [], []
        for u in range(NA_ROW_UNROLL):
            r = it * NA_ROW_UNROLL + u
            row_start = jnp.clip(r - NA_WIN_ROWS // 2, 0, rows - NA_WIN_ROWS)
            q0 = pl.multiple_of(r * GRID_W, GRID_W)
            k0 = pl.multiple_of(row_start * GRID_W, GRID_W)
            q = q_ref[0, pl.ds(q0, GRID_W), :]
            zero = jnp.zeros_like(q)
            qm = jnp.concatenate([jnp.where(first_head, q, zero), jnp.where(first_head, zero, q)], axis=0)
            s = lax.dot_general(qm, k_ref[0, pl.ds(k0, win), :], (((1,), (1,)), ((), ())),
                                preferred_element_type=f32)
            scores.append(s + bias_ref[r - row_start, 0])
            values.append(v_ref[0, pl.ds(k0, win), :])
            q_offsets.append(q0)
        s = jnp.concatenate(scores, axis=0)
        p = jnp.exp(s - jnp.max(s, axis=-1, keepdims=True))
        inv_den = 1.0 / jnp.sum(p, axis=-1, keepdims=True)
        p = p.astype(bf16)
        for u in range(NA_ROW_UNROLL):
            sl = slice(u * 2 * GRID_W, (u + 1) * 2 * GRID_W)
            o = jnp.dot(p[sl], values[u], preferred_element_type=f32) * inv_den[sl]
            o_ref[0, pl.ds(q_offsets[u], GRID_W), :] = jnp.where(
                first_head, o[:GRID_W], o[GRID_W:]).astype(o_ref.dtype)
        return carry

    lax.fori_loop(0, rows // NA_ROW_UNROLL, body, 0)


def _neighbourhood_attention(q, k, v, bias_tab):
    b, s, _ = q.shape
    pair = 2 * NA_HEAD_DIM
    qkv_spec = pl.BlockSpec((1, s, pair), lambda bi, hp: (bi, 0, hp))
    return pl.pallas_call(
        _na_kernel,
        grid=(b, NA_HEADS // 2),
        in_specs=[qkv_spec, qkv_spec, qkv_spec,
                  pl.BlockSpec((NA_WIN_ROWS, 1, 2 * GRID_W, NA_WIN_ROWS * GRID_W), lambda bi, hp: (0, hp, 0, 0))],
        out_specs=qkv_spec,
        out_shape=jax.ShapeDtypeStruct((b, s, NA_WIDTH), bf16),
        compiler_params=_params("parallel", "parallel"),
        name="neighbourhood_attention",
    )(q, k, v, bias_tab)


def _ft_tables(seq):
    assert seq == FT_N1 * FT_N2
    n_blk = FT_N2 // SUBLANES
    k1 = np.arange(FT_N1)[:, None, None, None]
    sr = np.arange(SUBLANES)[None, :, None, None]
    n1 = np.arange(FT_N1)[None, None, :, None]
    sc = np.arange(SUBLANES)[None, None, None, :]
    stage1 = np.zeros((n_blk, 2, FT_N1, SUBLANES, FT_N1, SUBLANES), np.float64)
    for blk in range(n_blk):
        n = FT_N2 * n1 + SUBLANES * blk + sr
        ang = 2.0 * np.pi * ((k1 * n) % seq) / seq
        eye = (sr == sc)
        stage1[blk, 0] = np.cos(ang) * eye
        stage1[blk, 1] = -np.sin(ang) * eye
    stage1 = stage1.reshape(n_blk, 2 * FT_N1 * SUBLANES, FT_N1 * SUBLANES)
    a = np.arange(FT_N2)
    ang2 = 2.0 * np.pi * ((a[:, None] * a[None, :]) % FT_N2) / FT_N2
    c2, s2 = np.cos(ang2), np.sin(ang2)
    stage2 = np.block([[c2, s2], [-s2, c2]])
    g = np.arange(FT_GROUP_DIM)
    angc = 2.0 * np.pi * ((g[:, None] * g[None, :]) % FT_GROUP_DIM) / FT_GROUP_DIM
    norm = 1.0 / np.sqrt(seq * FT_GROUP_DIM)
    chan = np.concatenate([np.cos(angc), np.sin(angc)], axis=0) * norm
    return (jnp.asarray(stage1, bf16), jnp.asarray(stage2, bf16), jnp.asarray(chan, bf16))


def _ft_stage1_kernel(u_ref, m_ref, z_ref):
    rows = FT_N1 * SUBLANES
    u = _unpack_bf16_pairs(u_ref[0].reshape(rows, FT_WIDTH // 2)).astype(bf16)
    z = jnp.dot(m_ref[0], u, preferred_element_type=f32)
    z_ref[0] = _pack2(z[:rows], z[rows:]).reshape(FT_N1, SUBLANES, FT_WIDTH)


def _ft_stage2_kernel(z_ref, s2_ref, cs_ref, y_ref):
    gd = FT_GROUP_DIM
    for kk in range(FT_K1_BLOCK):
        zz = jnp.concatenate(_unpack2(z_ref[0, kk]), axis=0).astype(bf16)
        x = jnp.dot(s2_ref[...], zz, preferred_element_type=f32)
        outs = []
        for g in range(FT_GROUPS):
            xg = jnp.concatenate([x[:FT_N2, g * gd:(g + 1) * gd], x[FT_N2:, g * gd:(g + 1) * gd]], axis=1)
            outs.append(jnp.dot(xg.astype(bf16), cs_ref[...], preferred_element_type=f32))
        y_ref[0, kk] = jnp.concatenate(outs, axis=1).astype(y_ref.dtype)


def _fourier_mix(u_packed, tables):
    u = u_packed
    b, s, _ = u.shape
    c = FT_WIDTH
    stage1, stage2, chan = tables
    n_blk = FT_N2 // SUBLANES
    z = pl.pallas_call(
        _ft_stage1_kernel,
        grid=(n_blk, b),
        in_specs=[pl.BlockSpec((1, FT_N1, SUBLANES, c // 2), lambda j, bi: (bi, 0, j, 0)),
                  pl.BlockSpec((1,) + stage1.shape[1:], lambda j, bi: (j, 0, 0))],
        out_specs=pl.BlockSpec((1, FT_N1, SUBLANES, c), lambda j, bi: (bi, 0, j, 0)),
        out_shape=jax.ShapeDtypeStruct((b, FT_N1, FT_N2, c), u32),
        compiler_params=_params("parallel", "parallel"),
        name="fourier_stage1",
    )(u.reshape(b, FT_N1, FT_N2, c // 2), stage1)
    y = pl.pallas_call(
        _ft_stage2_kernel,
        grid=(b, FT_N1 // FT_K1_BLOCK),
        in_specs=[pl.BlockSpec((1, FT_K1_BLOCK, FT_N2, c), lambda bi, kb: (bi, kb, 0, 0)),
                  pl.BlockSpec(stage2.shape, lambda bi, kb: (0, 0)),
                  pl.BlockSpec(chan.shape, lambda bi, kb: (0, 0))],
        out_specs=pl.BlockSpec((1, FT_K1_BLOCK, FT_N2, c), lambda bi, kb: (bi, kb, 0, 0)),
        out_shape=jax.ShapeDtypeStruct((b, FT_N1, FT_N2, c), bf16),
        compiler_params=_params("parallel", "parallel"),
        name="fourier_stage2",
    )(z, stage2, chan)
    return y.transpose(0, 2, 1, 3).reshape(b, s, c)


def _mem_kv_kernel(mem_ref, g_ref, w_ref, k_ref, v_ref):
    mn = _rms_scale(mem_ref[0]) * g_ref[...]
    kv = jnp.dot(mn.astype(bf16), w_ref[...], preferred_element_type=f32)
    k_ref[0] = kv[:, :MEM_WIDTH].astype(bf16)
    v_ref[0] = kv[:, MEM_WIDTH:].astype(bf16)


def _mem_kv(mem, g_mem, w_kv_bf16):
    b, m, d = mem.shape
    kv_spec = pl.BlockSpec((1, m, MEM_WIDTH), lambda bi: (bi, 0, 0))
    return pl.pallas_call(
        _mem_kv_kernel,
        grid=(b,),
        in_specs=[pl.BlockSpec((1, m, d), lambda bi: (bi, 0, 0)), pl.BlockSpec((1, d), lambda bi: (0, 0)),
                  pl.BlockSpec(w_kv_bf16.shape, lambda bi: (0, 0))],
        out_specs=[kv_spec, kv_spec],
        out_shape=[jax.ShapeDtypeStruct((b, m, MEM_WIDTH), bf16)] * 2,
        compiler_params=_params("parallel"),
        name="mem_kv",
    )(mem, g_mem.reshape(1, d), w_kv_bf16)


def _mix_out_kernel(x_ref, yna_ref, yft_ref, qm_ref, km_ref, vm_ref, ggrp_ref, wout_ref, gffn_ref, rw_ref,
                    rb_ref, x1_ref, h2p_ref, eidx_ref, gate_ref, rank_ref, cnt_ref, carry_ref):
    tm = x_ref.shape[0]

    @pl.when(pl.program_id(0) == 0)
    def _():
        carry_ref[...] = jnp.zeros_like(carry_ref)

    q = qm_ref[...]
    km = km_ref[0]
    vm = vm_ref[0]
    heads = []
    for h in range(MEM_HEADS):
        sl = slice(h * MEM_HEAD_DIM, (h + 1) * MEM_HEAD_DIM)
        s = lax.dot_general(q[:, sl], km[:, sl], (((1,), (1,)), ((), ())), preferred_element_type=f32)
        p = _softmax_rows(s * (MEM_HEAD_DIM ** -0.5))
        heads.append(jnp.dot(p.astype(bf16), vm[:, sl], preferred_element_type=f32))
    ymem = jnp.concatenate(heads, axis=1)

    g = ggrp_ref[...]
    a, c = NA_WIDTH, NA_WIDTH + FT_WIDTH
    y = jnp.concatenate([_rms_scale(yna_ref[...].astype(f32)) * g[:, :a],
                         _rms_scale(yft_ref[...].astype(f32)) * g[:, a:c],
                         _rms_scale(ymem) * g[:, c:]], axis=1)
    x1 = x_ref[...] + jnp.dot(y.astype(bf16), wout_ref[...], preferred_element_type=f32)
    x1_ref[...] = x1
    h2 = _rms_scale(x1) * gffn_ref[...]
    h2p_ref[...] = _pack_bf16_pairs(h2)

    h_hi = h2.astype(bf16)
    h_lo = (h2 - h_hi.astype(f32)).astype(bf16)
    hh = jnp.dot(h_hi, rw_ref[...], preferred_element_type=f32)
    logits = (hh[:, :LANES] + hh[:, LANES:]
              + jnp.dot(h_lo, rw_ref[:, :LANES], preferred_element_type=f32)) + rb_ref[...]
    l = logits.T[:N_EXPERTS]
    row = lax.broadcasted_iota(i32, (N_EXPERTS, tm), 0).astype(f32)
    vals, idxs, sels = [], [], []
    for _ in range(TOP_K):
        m = jnp.max(l, axis=0, keepdims=True)
        idx = jnp.min(jnp.where(l == m, row, float(N_EXPERTS)), axis=0, keepdims=True)
        sel = row == idx
        vals.append(m)
        idxs.append(idx)
        sels.append(sel)
        l = jnp.where(sel, -jnp.inf, l)
    ex = [jnp.exp(v - vals[0]) for v in vals]
    den = ex[0] + ex[1] + ex[2] + ex[3]

    onehot = (sels[0] | sels[1] | sels[2] | sels[3]).astype(f32)
    earlier = (lax.broadcasted_iota(i32, (tm, tm), 0) < lax.broadcasted_iota(i32, (tm, tm), 1)).astype(bf16)
    before = jnp.dot(onehot.astype(bf16), earlier, preferred_element_type=f32) + carry_ref[...]
    ranks = [jnp.sum(jnp.where(sel, before, 0.0), axis=0, keepdims=True) for sel in sels]
    carry_ref[...] = carry_ref[...] + jnp.sum(onehot, axis=1, keepdims=True)
    cnt_ref[...] = carry_ref[...].astype(i32)

    eidx_ref[...] = jnp.concatenate(idxs, axis=0).astype(i32)
    rank_ref[...] = jnp.concatenate(ranks, axis=0).astype(i32)
    gates_t = jnp.concatenate([e / den for e in ex] + [jnp.zeros((LANES - TOP_K, tm), f32)], axis=0)
    gate_ref[...] = gates_t.T[:, :TOP_K]


def _mix_out(x2d, y_na, y_ft, q_mem, k_mem, v_mem, g_grp, w_out_bf16, g_ffn, router_w2, router_b, seq,
             tile0, n_tiles):
    d = x2d.shape[1]
    tm = TOKEN_TILE
    t = n_tiles * tm
    steps_per_batch = seq // tm
    m = k_mem.shape[1]
    row_in = lambda w: pl.BlockSpec((tm, w), lambda i: (i + tile0, 0))
    row_out = lambda w: pl.BlockSpec((tm, w), lambda i: (i, 0))
    full = lambda a: pl.BlockSpec(a.shape, lambda i: (0,) * a.ndim)
    kv_spec = pl.BlockSpec((1, m, MEM_WIDTH), lambda i: ((i + tile0) // steps_per_batch, 0, 0))
    g_grp2, g_ffn2 = g_grp.reshape(1, -1), g_ffn.reshape(1, d)
    rb2 = jnp.pad(router_b.reshape(1, N_EXPERTS), ((0, 0), (0, LANES - N_EXPERTS)))
    router_w2 = jnp.pad(router_w2, ((0, 0), (0, 0), (0, LANES - N_EXPERTS)))
    router_w2 = jnp.concatenate([router_w2[0], router_w2[1]], axis=1)
    col_out = pl.BlockSpec((TOP_K, tm), lambda i: (0, i))
    return pl.pallas_call(
        _mix_out_kernel,
        grid=(n_tiles,),
        in_specs=[row_in(d), row_in(NA_WIDTH), row_in(FT_WIDTH), row_in(MEM_WIDTH), kv_spec, kv_spec,
                  full(g_grp2), full(w_out_bf16), full(g_ffn2), full(router_w2), full(rb2)],
        out_specs=[row_out(d), row_out(d // 2), col_out, row_out(TOP_K), col_out,
                   pl.BlockSpec((N_EXPERTS, 1), lambda i: (0, 0))],
        out_shape=[jax.ShapeDtypeStruct((t, d), f32), jax.ShapeDtypeStruct((t, d // 2), u32),
                   jax.ShapeDtypeStruct((TOP_K, t), i32), jax.ShapeDtypeStruct((t, TOP_K), f32),
                   jax.ShapeDtypeStruct((TOP_K, t), i32), jax.ShapeDtypeStruct((N_EXPERTS, 1), i32)],
        scratch_shapes=[pltpu.VMEM((N_EXPERTS, 1), f32)],
        compiler_params=_params("arbitrary"),
        name="mix_out_router",
    )(x2d, y_na, y_ft, q_mem, k_mem, v_mem, g_grp2, w_out_bf16, g_ffn2, router_w2, rb2)


def _sc_mesh():
    return plsc.VectorSubcoreMesh(core_axis_name="c", subcore_axis_name="s",
                                  num_cores=SC_CORES, num_subcores=SC_SUBCORES)


def _dispatch(h2p, dest, n_slots):
    t, w = h2p.shape
    workers = SC_CORES * SC_SUBCORES
    chunk = SC_GATHER_CHUNK
    per = t // workers
    steps = per // chunk
    assert per * workers == t and steps * chunk == per and steps % 2 == 0
    idx = dest.reshape(TOP_K, workers, steps, chunk)

    def body(h_hbm, idx_hbm, out_hbm, idx_v, rows_v, rsem, ssem):
        wid = lax.axis_index("s") * SC_CORES + lax.axis_index("c")
        base = wid * per
        for j in range(TOP_K):
            pltpu.sync_copy(idx_hbm.at[j, wid], idx_v.at[j])

        def read(c, slot):
            return pltpu.make_async_copy(h_hbm.at[pl.ds(base + c * chunk, chunk)], rows_v.at[slot], rsem.at[slot])

        def scatters(c, slot):
            return [pltpu.make_async_copy(rows_v.at[slot], out_hbm.at[idx_v.at[j, c]], ssem.at[slot])
                    for j in range(TOP_K)]

        read(0, 0).start()

        @pl.loop(0, steps, step=2)
        def _(c0):
            for slot in range(2):
                c = c0 + slot
                read(c, slot).wait()
                for cp in scatters(c, slot):
                    cp.start()

                @pl.when(c >= 1)
                def _():
                    for cp in scatters(c - 1, 1 - slot):
                        cp.wait()

                @pl.when(c + 1 < steps)
                def _():
                    read(c + 1, 1 - slot).start()

        for cp in scatters(steps - 1, 1):
            cp.wait()

    return pl.kernel(
        body,
        out_type=jax.ShapeDtypeStruct((n_slots, w), h2p.dtype),
        mesh=_sc_mesh(),
        scratch_types=[pltpu.VMEM((TOP_K, steps, chunk), i32), pltpu.VMEM((2, chunk, w), h2p.dtype),
                       pltpu.SemaphoreType.DMA((2,)), pltpu.SemaphoreType.DMA((2,))],
        name="sc_dispatch_rows",
    )(h2p, idx)


def _expert_kernel(blk_e_ref, blk_cnt_ref, nxt_e_ref, xs_ref, wgu_hbm, bgu_ref, wd_hbm, bd_ref, y_ref,
                   wgu_f32, wd_f32, wgu_bf, wd_bf, sem):
    b = pl.program_id(0)
    e = blk_e_ref[b]
    cnt = blk_cnt_ref[b]
    bm = xs_ref.shape[0]
    de = wd_f32.shape[0]

    def fetch(expert):
        return (pltpu.make_async_copy(wgu_hbm.at[expert], wgu_f32, sem.at[0]),
                pltpu.make_async_copy(wd_hbm.at[expert], wd_f32, sem.at[1]))

    @pl.when(b == 0)
    def _():
        for cp in fetch(e):
            cp.start()

    @pl.when(jnp.logical_or(b == 0, e != blk_e_ref[jnp.maximum(b - 1, 0)]))
    def _():
        for cp in fetch(e):
            cp.wait()

        def convert(i, carry):
            rows = pl.ds(pl.multiple_of(i * WEIGHT_CAST_ROWS, WEIGHT_CAST_ROWS), WEIGHT_CAST_ROWS)
            wgu_bf[rows, :] = wgu_f32[rows, :].astype(bf16)
            wd_bf[rows, :] = wd_f32[rows, :].astype(bf16)
            return carry

        lax.fori_loop(0, wgu_f32.shape[0] // WEIGHT_CAST_ROWS, convert, 0)

        @pl.when(nxt_e_ref[b] >= 0)
        def _():
            for cp in fetch(nxt_e_ref[b]):
                cp.start()

    def ffn_rows(r0, n):
        rows = pl.ds(r0, n)
        valid = r0 + lax.broadcasted_iota(i32, (n, 1), 0) < cnt
        x = jnp.where(valid, _unpack_bf16_pairs(xs_ref[rows, :]), 0.0).astype(bf16)
        gu = jnp.dot(x, wgu_bf[...], preferred_element_type=f32) + bgu_ref[0]
        x_glu = jnp.minimum(gu[:, :de], SWIGLU_LIMIT)
        x_lin = jnp.clip(gu[:, de:], -SWIGLU_LIMIT, SWIGLU_LIMIT)
        act = x_glu * (1.0 / (1.0 + jnp.exp(-SWIGLU_ALPHA * x_glu))) * (x_lin + 1.0)
        y = jnp.dot(act.astype(bf16), wd_bf[...], preferred_element_type=f32) + bd_ref[0]
        y_ref[rows, :] = _pack_bf16_pairs(y)

    @pl.when(cnt == bm)
    def _():
        ffn_rows(0, bm)

    @pl.when(cnt < bm)
    def _():
        y_ref[...] = jnp.zeros_like(y_ref)

        def piece(i, carry):
            ffn_rows(pl.multiple_of(i * MOE_SUB_BLOCK, MOE_SUB_BLOCK), MOE_SUB_BLOCK)
            return carry

        lax.fori_loop(0, (cnt + MOE_SUB_BLOCK - 1) // MOE_SUB_BLOCK, piece, 0)


def _experts(xs, blk_e, blk_cnt, nxt_e, w_gu, b_gu, w_down, b_down):
    n_slots, w = xs.shape
    bm = MOE_BLOCK
    e, d, de2 = w_gu.shape
    de = w_down.shape[1]
    assert de == d, "one row loop converts both weight matrices"
    grid_spec = pltpu.PrefetchScalarGridSpec(
        num_scalar_prefetch=3,
        grid=(n_slots // bm,),
        in_specs=[pl.BlockSpec((bm, w), lambda b, be, bc, ne: (b, 0)),
                  pl.BlockSpec(memory_space=pl.ANY),
                  pl.BlockSpec((1, 1, de2), lambda b, be, bc, ne: (be[b], 0, 0)),
                  pl.BlockSpec(memory_space=pl.ANY),
                  pl.BlockSpec((1, 1, d), lambda b, be, bc, ne: (be[b], 0, 0))],
        out_specs=pl.BlockSpec((bm, w), lambda b, be, bc, ne: (b, 0)),
        scratch_shapes=[pltpu.VMEM((d, de2), f32), pltpu.VMEM((de, d), f32),
                        pltpu.VMEM((d, de2), bf16), pltpu.VMEM((de, d), bf16),
                        pltpu.SemaphoreType.DMA((2,))],
    )
    return pl.pallas_call(
        _expert_kernel,
        grid_spec=grid_spec,
        out_shape=jax.ShapeDtypeStruct((n_slots, w), u32),
        compiler_params=_params("arbitrary"),
        name="moe_experts",
    )(blk_e, blk_cnt, nxt_e, xs, w_gu, b_gu.reshape(e, 1, de2), w_down, b_down.reshape(e, 1, d))


def _sc_gather_rows(table, idx):
    n, w = idx.shape[0], table.shape[1]
    workers = SC_CORES * SC_SUBCORES
    chunk = SC_GATHER_CHUNK
    per = n // workers
    steps = per // chunk
    assert per * workers == n and steps * chunk == per and steps % 2 == 0

    def body(table_hbm, idx_hbm, out_hbm, idx_v, rows_v, gsem, wsem):
        base = (lax.axis_index("s") * SC_CORES + lax.axis_index("c")) * per
        pltpu.sync_copy(idx_hbm.at[pl.ds(base, per)], idx_v)

        def gather(c, slot):
            return pltpu.make_async_copy(table_hbm.at[idx_v.at[pl.ds(c * chunk, chunk)]], rows_v.at[slot],
                                         gsem.at[slot])

        def write(c, slot):
            return pltpu.make_async_copy(rows_v.at[slot], out_hbm.at[pl.ds(base + c * chunk, chunk)],
                                         wsem.at[slot])

        gather(0, 0).start()

        @pl.loop(0, steps, step=2)
        def _(c0):
            for slot in range(2):
                c = c0 + slot
                gather(c, slot).wait()
                write(c, slot).start()

                @pl.when(c >= 1)
                def _():
                    write(c - 1, 1 - slot).wait()

                @pl.when(c + 1 < steps)
                def _():
                    gather(c + 1, 1 - slot).start()

        write(steps - 1, 1).wait()

    return pl.kernel(
        body,
        out_type=jax.ShapeDtypeStruct((n, w), table.dtype),
        mesh=_sc_mesh(),
        scratch_types=[pltpu.VMEM((per,), i32), pltpu.VMEM((2, chunk, w), table.dtype),
                       pltpu.SemaphoreType.DMA((2,)), pltpu.SemaphoreType.DMA((2,))],
        name="sc_gather_rows",
    )(table, idx)


def _combine_kernel(x1_ref, gate_ref, gfin_ref, yg_ref, *rest):
    o_ref = rest[-1]
    acc = x1_ref[...]
    gates = gate_ref[...]
    for j in range(TOP_K):
        acc = acc + gates[:, j:j + 1] * _unpack_bf16_pairs(yg_ref[j])
    o_ref[...] = _rms_scale(acc) * gfin_ref[...]


def _combine(x1, gates, dest, y_slots, g_final, out_prev, tile0, total_tokens):
    t, d = x1.shape
    tm = TOKEN_TILE
    w = y_slots.shape[1]
    yg = _sc_gather_rows(y_slots, dest.reshape(-1)).reshape(TOP_K, t, w)
    in_specs = [pl.BlockSpec((tm, d), lambda i: (i, 0)),
                pl.BlockSpec((tm, TOP_K), lambda i: (i, 0)),
                pl.BlockSpec((1, d), lambda i: (0, 0)),
                pl.BlockSpec((TOP_K, tm, w), lambda i: (0, i, 0))]
    args = [x1, gates, g_final.reshape(1, d), yg]
    aliases = {}
    if out_prev is not None:
        in_specs.append(pl.BlockSpec(memory_space=pl.ANY))
        args.append(out_prev)
        aliases = {len(args) - 1: 0}
    return pl.pallas_call(
        _combine_kernel,
        grid=(t // tm,),
        in_specs=in_specs,
        out_specs=pl.BlockSpec((tm, d), lambda i: (i + tile0, 0)),
        out_shape=jax.ShapeDtypeStruct((total_tokens, d), f32),
        input_output_aliases=aliases,
        compiler_params=_params("parallel"),
        name="moe_combine",
    )(*args)


def _dest_kernel(start_ref, eidx_ref, rank_ref, dest_ref):
    eidx = eidx_ref[...]
    dest = rank_ref[...]
    for e in range(N_EXPERTS):
        dest = dest + jnp.where(eidx == e, start_ref[e], 0)
    dest_ref[...] = dest


def _slot_layout(counts, eidx, rank, n_blocks):
    bm = MOE_BLOCK
    padded = (counts + bm - 1) // bm * bm
    padded_end = jnp.cumsum(padded)
    start = padded_end - padded
    experts = jnp.arange(N_EXPERTS, dtype=i32)
    lookup = lambda table, idx: jnp.sum(jnp.where(idx[..., None] == experts, table, 0), axis=-1)
    dest = pl.pallas_call(
        _dest_kernel,
        grid_spec=pltpu.PrefetchScalarGridSpec(
            num_scalar_prefetch=1, grid=(1,),
            in_specs=[pl.BlockSpec(eidx.shape, lambda i, st: (0, 0)), pl.BlockSpec(rank.shape, lambda i, st: (0, 0))],
            out_specs=pl.BlockSpec(rank.shape, lambda i, st: (0, 0))),
        out_shape=jax.ShapeDtypeStruct(rank.shape, i32),
        name="moe_dest",
    )(start.astype(i32), eidx, rank)
    blk_row = jnp.arange(n_blocks, dtype=i32) * bm
    blk_e = jnp.minimum(jnp.sum((padded_end[None, :] <= blk_row[:, None]).astype(i32), axis=1), N_EXPERTS - 1)
    blk_cnt = jnp.clip(lookup(counts, blk_e) - (blk_row - lookup(start, blk_e)), 0, bm).astype(i32)
    none = jnp.int32(N_EXPERTS)
    nxt_e = jnp.min(jnp.where(blk_e[None, :] > blk_e[:, None], blk_e[None, :], none), axis=1)
    nxt_e = jnp.where(nxt_e == none, -1, nxt_e).astype(i32)
    return dest, blk_e, blk_cnt, nxt_e


def _layer_and_final_norm(x2d, mem, seq, g_mix, g_mem, w_in, w_mem_kv, na_rel_bias, g_grp, w_out, g_ffn,
                          router_w, router_b, w_gu, b_gu, w_down, b_down, g_final):
    t, d = x2d.shape
    b = t // seq
    q_na, k_na, v_na, u_ft, q_mem = _in_proj(x2d, g_mix, w_in.astype(bf16))
    k_mem, v_mem = _mem_kv(mem, g_mem, w_mem_kv.astype(bf16))
    shape3 = lambda a: a.reshape(b, seq, a.shape[-1])
    y_na = _neighbourhood_attention(shape3(q_na), shape3(k_na), shape3(v_na), _na_bias_table(na_rel_bias))
    y_ft = _fourier_mix(shape3(u_ft), _ft_tables(seq))
    y_na, y_ft = y_na.reshape(t, -1), y_ft.reshape(t, -1)
    w_out_bf16 = w_out.astype(bf16)
    rw_hi = router_w.astype(bf16)
    router_w2 = jnp.stack([rw_hi, (router_w - rw_hi.astype(f32)).astype(bf16)])

    tiles = t // TOKEN_TILE
    unit = tiles // sum(MOE_GROUP_SHARES)
    assert unit * sum(MOE_GROUP_SHARES) == tiles
    out = None
    tile0 = 0
    for share in MOE_GROUP_SHARES:
        group_tiles = share * unit
        n_blocks = (group_tiles * TOKEN_TILE * TOP_K) // MOE_BLOCK + N_EXPERTS
        x1, h2p, eidx, gates, rank, counts = _mix_out(
            x2d, y_na, y_ft, q_mem, k_mem, v_mem, g_grp, w_out_bf16, g_ffn, router_w2, router_b, seq,
            tile0, group_tiles)
        dest, blk_e, blk_cnt, nxt_e = _slot_layout(counts[:, 0], eidx, rank, n_blocks)
        xs = _dispatch(h2p, dest, n_blocks * MOE_BLOCK)
        y_slots = _experts(xs, blk_e, blk_cnt, nxt_e, w_gu, b_gu, w_down, b_down)
        out = _combine(x1, gates, dest, y_slots, g_final, out, tile0, t)
        tile0 += group_tiles
    return out


def kernel(x, mem, g_mix, g_mem, w_in, w_mem_kv, na_rel_bias, g_grp, w_out, g_ffn, router_w, router_b,
           w_gu, b_gu, w_down, b_down, g_final):
    b, seq, d = x.shape
    depth = w_in.shape[0]
    assert depth == 1, "the final norm is fused into the single layer's combine step"
    out = _layer_and_final_norm(
        x.reshape(b * seq, d), mem, seq, g_mix[0], g_mem[0], w_in[0], w_mem_kv[0], na_rel_bias[0], g_grp[0],
        w_out[0], g_ffn[0], router_w[0], router_b[0], w_gu[0], b_gu[0], w_down[0], b_down[0], g_final)
    return out.reshape(b, seq, d)
```

```python
import functools

import numpy as np
import jax
import jax.numpy as jnp
from jax import lax
from jax.experimental import pallas as pl
from jax.experimental.pallas import tpu as pltpu
from jax.experimental.pallas import tpu_sc as plsc

f32 = jnp.float32
bf16 = jnp.bfloat16
u32 = jnp.uint32
i32 = jnp.int32

GRID_W = 64
NA_HEADS = 8
NA_HEAD_DIM = 64
NA_WIN_ROWS = 8
NA_WIN_COLS = 16
FT_GROUPS = 4
FT_GROUP_DIM = 128
MEM_HEADS = 4
MEM_HEAD_DIM = 128
NA_WIDTH = NA_HEADS * NA_HEAD_DIM
FT_WIDTH = FT_GROUPS * FT_GROUP_DIM
MEM_WIDTH = MEM_HEADS * MEM_HEAD_DIM
N_EXPERTS = 32
TOP_K = 4
SWIGLU_LIMIT = 7.0
SWIGLU_ALPHA = 1.702
EPS = 1e-6

LANES = 128
SUBLANES = 8
VMEM_LIMIT_BYTES = 56 * 1024 * 1024
SC_CORES = 2
SC_SUBCORES = 16
SC_GATHER_CHUNK = 64

TOKEN_TILE = 512
MOE_BLOCK = 512
MOE_SUB_BLOCK = 128
WEIGHT_CAST_ROWS = 128
MOE_GROUP_SHARES = (1, 1)
NA_ROW_UNROLL = 16
FT_N1 = 64
FT_N2 = 128
FT_K1_BLOCK = 8
MASK_VALUE = -1e30


def _params(*semantics):
    return pltpu.CompilerParams(dimension_semantics=semantics, vmem_limit_bytes=VMEM_LIMIT_BYTES)


def _rms_scale(x):
    return x * lax.rsqrt(jnp.mean(x * x, axis=-1, keepdims=True) + EPS)


def _softmax_rows(s):
    p = jnp.exp(s - jnp.max(s, axis=-1, keepdims=True))
    return p / jnp.sum(p, axis=-1, keepdims=True)


def _bf16_bits(x):
    return pltpu.bitcast(x.astype(bf16).astype(f32), u32)


def _pack2(lo, hi):
    return (_bf16_bits(lo) >> 16) | (_bf16_bits(hi) & jnp.uint32(0xFFFF0000))


def _unpack2(w):
    return pltpu.bitcast(w << 16, f32), pltpu.bitcast(w & jnp.uint32(0xFFFF0000), f32)


def _pack_bf16_pairs(x):
    n = x.shape[1] // 2
    return _pack2(x[:, :n], x[:, n:])


def _unpack_bf16_pairs(w):
    return jnp.concatenate(_unpack2(w), axis=1)


def _in_proj_kernel(x_ref, g_ref, w_ref, qna_ref, kna_ref, vna_ref, uft_ref, qmem_ref):
    h = _rms_scale(x_ref[...]) * g_ref[...]
    proj = jnp.dot(h.astype(bf16), w_ref[...], preferred_element_type=f32)
    o = NA_WIDTH
    qna_ref[...] = (proj[:, :o] * (NA_HEAD_DIM ** -0.5)).astype(bf16)
    kna_ref[...] = proj[:, o:2 * o].astype(bf16)
    vna_ref[...] = proj[:, 2 * o:3 * o].astype(bf16)
    uft_ref[...] = _pack_bf16_pairs(proj[:, 3 * o:3 * o + FT_WIDTH])
    qmem_ref[...] = proj[:, 3 * o + FT_WIDTH:].astype(bf16)


def _in_proj(x2d, g_mix, w_in_bf16):
    t, d = x2d.shape
    tm = TOKEN_TILE
    row = lambda w: pl.BlockSpec((tm, w), lambda i: (i, 0))
    return pl.pallas_call(
        _in_proj_kernel,
        grid=(t // tm,),
        in_specs=[row(d), pl.BlockSpec((1, d), lambda i: (0, 0)),
                  pl.BlockSpec(w_in_bf16.shape, lambda i: (0, 0))],
        out_specs=[row(NA_WIDTH), row(NA_WIDTH), row(NA_WIDTH), row(FT_WIDTH // 2), row(MEM_WIDTH)],
        out_shape=[jax.ShapeDtypeStruct((t, NA_WIDTH), bf16)] * 3
        + [jax.ShapeDtypeStruct((t, FT_WIDTH // 2), u32), jax.ShapeDtypeStruct((t, MEM_WIDTH), bf16)],
        compiler_params=_params("parallel"),
        name="in_proj",
    )(x2d, g_mix.reshape(1, d), w_in_bf16)


def _na_bias_table(rel_bias):
    c = np.arange(GRID_W)
    dc_idx = np.clip(c[None, :] - c[:, None], -(NA_WIN_COLS - 1), NA_WIN_COLS - 1) + (NA_WIN_COLS - 1)
    col_start = np.clip(c - NA_WIN_COLS // 2, 0, GRID_W - NA_WIN_COLS)
    col_in = (c[None, :] >= col_start[:, None]) & (c[None, :] < col_start[:, None] + NA_WIN_COLS)
    pick_c = jnp.asarray(dc_idx[:, :, None] == np.arange(2 * NA_WIN_COLS - 1), f32)
    cols = jnp.einsum("hab,qcb->haqc", rel_bias.astype(f32), pick_c, precision=lax.Precision.HIGHEST)
    cols = jnp.where(col_in[None, None], cols, MASK_VALUE)
    tab = jnp.stack([jnp.concatenate([cols[:, j - s + NA_WIN_ROWS - 1] for j in range(NA_WIN_ROWS)], axis=-1)
                     for s in range(NA_WIN_ROWS)])
    return tab.reshape(NA_WIN_ROWS, NA_HEADS // 2, 2 * GRID_W, NA_WIN_ROWS * GRID_W)


def _na_kernel(q_ref, k_ref, v_ref, bias_ref, o_ref):
    rows = q_ref.shape[1] // GRID_W
    win = NA_WIN_ROWS * GRID_W
    first_head = lax.broadcasted_iota(i32, (GRID_W, 2 * NA_HEAD_DIM), 1) < NA_HEAD_DIM

    def body(it, carry):
        scores, values, q_offsets = [], [], []
        for u in range(NA_ROW_UNROLL):
            r = it * NA_ROW_UNROLL + u
            row_start = jnp.clip(r - NA_WIN_ROWS // 2, 0, rows - NA_WIN_ROWS)
            q0 = pl.multiple_of(r * GRID_W, GRID_W)
            k0 = pl.multiple_of(row_start * GRID_W, GRID_W)
            q = q_ref[0, pl.ds(q0, GRID_W), :]
            zero = jnp.zeros_like(q)
            qm = jnp.concatenate([jnp.where(first_head, q, zero), jnp.where(first_head, zero, q)], axis=0)
            s = lax.dot_general(qm, k_ref[0, pl.ds(k0, win), :], (((1,), (1,)), ((), ())),
                                preferred_element_type=f32)
            scores.append(s + bias_ref[r - row_start, 0])
            values.append(v_ref[0, pl.ds(k0, win), :])
            q_offsets.append(q0)
        s = jnp.concatenate(scores, axis=0)
        p = jnp.exp(s - jnp.max(s, axis=-1, keepdims=True))
        inv_den = 1.0 / jnp.sum(p, axis=-1, keepdims=True)
        p = p.astype(bf16)
        for u in range(NA_ROW_UNROLL):
            sl = slice(u * 2 * GRID_W, (u + 1) * 2 * GRID_W)
            o = jnp.dot(p[sl], values[u], preferred_element_type=f32) * inv_den[sl]
            o_ref[0, pl.ds(q_offsets[u], GRID_W), :] = jnp.where(
                first_head, o[:GRID_W], o[GRID_W:]).astype(o_ref.dtype)
        return carry

    lax.fori_loop(0, rows // NA_ROW_UNROLL, body, 0)


def _neighbourhood_attention(q, k, v, bias_tab):
    b, s, _ = q.shape
    pair = 2 * NA_HEAD_DIM
    qkv_spec = pl.BlockSpec((1, s, pair), lambda bi, hp: (bi, 0, hp))
    return pl.pallas_call(
        _na_kernel,
        grid=(b, NA_HEADS // 2),
        in_specs=[qkv_spec, qkv_spec, qkv_spec,
                  pl.BlockSpec((NA_WIN_ROWS, 1, 2 * GRID_W, NA_WIN_ROWS * GRID_W), lambda bi, hp: (0, hp, 0, 0))],
        out_specs=qkv_spec,
        out_shape=jax.ShapeDtypeStruct((b, s, NA_WIDTH), bf16),
        compiler_params=_params("parallel", "parallel"),
        name="neighbourhood_attention",
    )(q, k, v, bias_tab)


def _ft_tables(seq):
    assert seq == FT_N1 * FT_N2
    n_blk = FT_N2 // SUBLANES
    k1 = np.arange(FT_N1)[:, None, None, None]
    sr = np.arange(SUBLANES)[None, :, None, None]
    n1 = np.arange(FT_N1)[None, None, :, None]
    sc = np.arange(SUBLANES)[None, None, None, :]
    stage1 = np.zeros((n_blk, 2, FT_N1, SUBLANES, FT_N1, SUBLANES), np.float64)
    for blk in range(n_blk):
        n = FT_N2 * n1 + SUBLANES * blk + sr
        ang = 2.0 * np.pi * ((k1 * n) % seq) / seq
        eye = (sr == sc)
        stage1[blk, 0] = np.cos(ang) * eye
        stage1[blk, 1] = -np.sin(ang) * eye
    stage1 = stage1.reshape(n_blk, 2 * FT_N1 * SUBLANES, FT_N1 * SUBLANES)
    a = np.arange(FT_N2)
    ang2 = 2.0 * np.pi * ((a[:, None] * a[None, :]) % FT_N2) / FT_N2
    c2, s2 = np.cos(ang2), np.sin(ang2)
    stage2 = np.block([[c2, s2], [-s2, c2]])
    g = np.arange(FT_GROUP_DIM)
    angc = 2.0 * np.pi * ((g[:, None] * g[None, :]) % FT_GROUP_DIM) / FT_GROUP_DIM
    norm = 1.0 / np.sqrt(seq * FT_GROUP_DIM)
    chan = np.concatenate([np.cos(angc), np.sin(angc)], axis=0) * norm
    return (jnp.asarray(stage1, bf16), jnp.asarray(stage2, bf16), jnp.asarray(chan, bf16))


def _ft_stage1_kernel(u_ref, m_ref, z_ref):
    rows = FT_N1 * SUBLANES
    u = _unpack_bf16_pairs(u_ref[0].reshape(rows, FT_WIDTH // 2)).astype(bf16)
    z = jnp.dot(m_ref[0], u, preferred_element_type=f32)
    z_ref[0] = _pack2(z[:rows], z[rows:]).reshape(FT_N1, SUBLANES, FT_WIDTH)


def _ft_stage2_kernel(z_ref, s2_ref, cs_ref, y_ref):
    gd = FT_GROUP_DIM
    for kk in range(FT_K1_BLOCK):
        zz = jnp.concatenate(_unpack2(z_ref[0, kk]), axis=0).astype(bf16)
        x = jnp.dot(s2_ref[...], zz, preferred_element_type=f32)
        outs = []
        for g in range(FT_GROUPS):
            xg = jnp.concatenate([x[:FT_N2, g * gd:(g + 1) * gd], x[FT_N2:, g * gd:(g + 1) * gd]], axis=1)
            outs.append(jnp.dot(xg.astype(bf16), cs_ref[...], preferred_element_type=f32))
        y_ref[0, kk] = jnp.concatenate(outs, axis=1).astype(y_ref.dtype)


def _fourier_mix(u_packed, tables):
    u = u_packed
    b, s, _ = u.shape
    c = FT_WIDTH
    stage1, stage2, chan = tables
    n_blk = FT_N2 // SUBLANES
    z = pl.pallas_call(
        _ft_stage1_kernel,
        grid=(n_blk, b),
        in_specs=[pl.BlockSpec((1, FT_N1, SUBLANES, c // 2), lambda j, bi: (bi, 0, j, 0)),
                  pl.BlockSpec((1,) + stage1.shape[1:], lambda j, bi: (j, 0, 0))],
        out_specs=pl.BlockSpec((1, FT_N1, SUBLANES, c), lambda j, bi: (bi, 0, j, 0)),
        out_shape=jax.ShapeDtypeStruct((b, FT_N1, FT_N2, c), u32),
        compiler_params=_params("parallel", "parallel"),
        name="fourier_stage1",
    )(u.reshape(b, FT_N1, FT_N2, c // 2), stage1)
    y = pl.pallas_call(
        _ft_stage2_kernel,
        grid=(b, FT_N1 // FT_K1_BLOCK),
        in_specs=[pl.BlockSpec((1, FT_K1_BLOCK, FT_N2, c), lambda bi, kb: (bi, kb, 0, 0)),
                  pl.BlockSpec(stage2.shape, lambda bi, kb: (0, 0)),
                  pl.BlockSpec(chan.shape, lambda bi, kb: (0, 0))],
        out_specs=pl.BlockSpec((1, FT_K1_BLOCK, FT_N2, c), lambda bi, kb: (bi, kb, 0, 0)),
        out_shape=jax.ShapeDtypeStruct((b, FT_N1, FT_N2, c), bf16),
        compiler_params=_params("parallel", "parallel"),
        name="fourier_stage2",
    )(z, stage2, chan)
    return y.transpose(0, 2, 1, 3).reshape(b, s, c)


def _mem_kv_kernel(mem_ref, g_ref, w_ref, k_ref, v_ref):
    mn = _rms_scale(mem_ref[0]) * g_ref[...]
    kv = jnp.dot(mn.astype(bf16), w_ref[...], preferred_element_type=f32)
    k_ref[0] = kv[:, :MEM_WIDTH].astype(bf16)
    v_ref[0] = kv[:, MEM_WIDTH:].astype(bf16)


def _mem_kv(mem, g_mem, w_kv_bf16):
    b, m, d = mem.shape
    kv_spec = pl.BlockSpec((1, m, MEM_WIDTH), lambda bi: (bi, 0, 0))
    return pl.pallas_call(
        _mem_kv_kernel,
        grid=(b,),
        in_specs=[pl.BlockSpec((1, m, d), lambda bi: (bi, 0, 0)), pl.BlockSpec((1, d), lambda bi: (0, 0)),
                  pl.BlockSpec(w_kv_bf16.shape, lambda bi: (0, 0))],
        out_specs=[kv_spec, kv_spec],
        out_shape=[jax.ShapeDtypeStruct((b, m, MEM_WIDTH), bf16)] * 2,
        compiler_params=_params("parallel"),
        name="mem_kv",
    )(mem, g_mem.reshape(1, d), w_kv_bf16)


def _mix_out_kernel(x_ref, yna_ref, yft_ref, qm_ref, km_ref, vm_ref, ggrp_ref, wout_ref, gffn_ref, rw_ref,
                    rb_ref, _order_ref, x1_ref, h2p_ref, eidx_ref, gate_ref, rank_ref, cnt_ref, carry_ref):
    tm = x_ref.shape[0]

    @pl.when(pl.program_id(0) == 0)
    def _():
        carry_ref[...] = jnp.zeros_like(carry_ref)

    q = qm_ref[...]
    km = km_ref[0]
    vm = vm_ref[0]
    heads = []
    for h in range(MEM_HEADS):
        sl = slice(h * MEM_HEAD_DIM, (h + 1) * MEM_HEAD_DIM)
        s = lax.dot_general(q[:, sl], km[:, sl], (((1,), (1,)), ((), ())), preferred_element_type=f32)
        p = _softmax_rows(s * (MEM_HEAD_DIM ** -0.5))
        heads.append(jnp.dot(p.astype(bf16), vm[:, sl], preferred_element_type=f32))
    ymem = jnp.concatenate(heads, axis=1)

    g = ggrp_ref[...]
    a, c = NA_WIDTH, NA_WIDTH + FT_WIDTH
    y = jnp.concatenate([_rms_scale(yna_ref[...].astype(f32)) * g[:, :a],
                         _rms_scale(yft_ref[...].astype(f32)) * g[:, a:c],
                         _rms_scale(ymem) * g[:, c:]], axis=1)
    x1 = x_ref[...] + jnp.dot(y.astype(bf16), wout_ref[...], preferred_element_type=f32)
    x1_ref[...] = x1
    h2 = _rms_scale(x1) * gffn_ref[...]
    h2p_ref[...] = _pack_bf16_pairs(h2)

    h_hi = h2.astype(bf16)
    h_lo = (h2 - h_hi.astype(f32)).astype(bf16)
    hh = jnp.dot(h_hi, rw_ref[...], preferred_element_type=f32)
    logits = (hh[:, :LANES] + hh[:, LANES:]
              + jnp.dot(h_lo, rw_ref[:, :LANES], preferred_element_type=f32)) + rb_ref[...]
    l = logits.T[:N_EXPERTS]
    row = lax.broadcasted_iota(i32, (N_EXPERTS, tm), 0).astype(f32)
    vals, idxs, sels = [], [], []
    for _ in range(TOP_K):
        m = jnp.max(l, axis=0, keepdims=True)
        idx = jnp.min(jnp.where(l == m, row, float(N_EXPERTS)), axis=0, keepdims=True)
        sel = row == idx
        vals.append(m)
        idxs.append(idx)
        sels.append(sel)
        l = jnp.where(sel, -jnp.inf, l)
    ex = [jnp.exp(v - vals[0]) for v in vals]
    den = ex[0] + ex[1] + ex[2] + ex[3]

    onehot = (sels[0] | sels[1] | sels[2] | sels[3]).astype(f32)
    earlier = (lax.broadcasted_iota(i32, (tm, tm), 0) < lax.broadcasted_iota(i32, (tm, tm), 1)).astype(bf16)
    before = jnp.dot(onehot.astype(bf16), earlier, preferred_element_type=f32) + carry_ref[...]
    ranks = [jnp.sum(jnp.where(sel, before, 0.0), axis=0, keepdims=True) for sel in sels]
    carry_ref[...] = carry_ref[...] + jnp.sum(onehot, axis=1, keepdims=True)
    cnt_ref[...] = carry_ref[...].astype(i32)

    eidx_ref[...] = jnp.concatenate(idxs, axis=0).astype(i32)
    rank_ref[...] = jnp.concatenate(ranks, axis=0).astype(i32)
    gates_t = jnp.concatenate([e / den for e in ex] + [jnp.zeros((LANES - TOP_K, tm), f32)], axis=0)
    gate_ref[...] = gates_t.T[:, :TOP_K]


def _mix_out(x2d, y_na, y_ft, q_mem, k_mem, v_mem, g_grp, w_out_bf16, g_ffn, router_w2, router_b, seq,
             tile0, n_tiles, order_after):
    d = x2d.shape[1]
    tm = TOKEN_TILE
    t = n_tiles * tm
    steps_per_batch = seq // tm
    m = k_mem.shape[1]
    row_in = lambda w: pl.BlockSpec((tm, w), lambda i: (i + tile0, 0))
    row_out = lambda w: pl.BlockSpec((tm, w), lambda i: (i, 0))
    full = lambda a: pl.BlockSpec(a.shape, lambda i: (0,) * a.ndim)
    kv_spec = pl.BlockSpec((1, m, MEM_WIDTH), lambda i: ((i + tile0) // steps_per_batch, 0, 0))
    g_grp2, g_ffn2 = g_grp.reshape(1, -1), g_ffn.reshape(1, d)
    rb2 = jnp.pad(router_b.reshape(1, N_EXPERTS), ((0, 0), (0, LANES - N_EXPERTS)))
    router_w2 = jnp.pad(router_w2, ((0, 0), (0, 0), (0, LANES - N_EXPERTS)))
    router_w2 = jnp.concatenate([router_w2[0], router_w2[1]], axis=1)
    col_out = pl.BlockSpec((TOP_K, tm), lambda i: (0, i))
    return pl.pallas_call(
        _mix_out_kernel,
        grid=(n_tiles,),
        in_specs=[row_in(d), row_in(NA_WIDTH), row_in(FT_WIDTH), row_in(MEM_WIDTH), kv_spec, kv_spec,
                  full(g_grp2), full(w_out_bf16), full(g_ffn2), full(router_w2), full(rb2),
                  pl.BlockSpec(memory_space=pl.ANY)],
        out_specs=[row_out(d), row_out(d // 2), col_out, row_out(TOP_K), col_out,
                   pl.BlockSpec((N_EXPERTS, 1), lambda i: (0, 0))],
        out_shape=[jax.ShapeDtypeStruct((t, d), f32), jax.ShapeDtypeStruct((t, d // 2), u32),
                   jax.ShapeDtypeStruct((TOP_K, t), i32), jax.ShapeDtypeStruct((t, TOP_K), f32),
                   jax.ShapeDtypeStruct((TOP_K, t), i32), jax.ShapeDtypeStruct((N_EXPERTS, 1), i32)],
        scratch_shapes=[pltpu.VMEM((N_EXPERTS, 1), f32)],
        compiler_params=_params("arbitrary"),
        name="mix_out_router",
    )(x2d, y_na, y_ft, q_mem, k_mem, v_mem, g_grp2, w_out_bf16, g_ffn2, router_w2, rb2, order_after)


def _sc_mesh():
    return plsc.VectorSubcoreMesh(core_axis_name="c", subcore_axis_name="s",
                                  num_cores=SC_CORES, num_subcores=SC_SUBCORES)


def _dispatch(h2p, dest, n_slots):
    t, w = h2p.shape
    workers = SC_CORES * SC_SUBCORES
    chunk = SC_GATHER_CHUNK
    per = t // workers
    steps = per // chunk
    assert per * workers == t and steps * chunk == per and steps % 2 == 0
    idx = dest.reshape(TOP_K, workers, steps, chunk)

    def body(h_hbm, idx_hbm, out_hbm, idx_v, rows_v, rsem, ssem):
        wid = lax.axis_index("s") * SC_CORES + lax.axis_index("c")
        base = wid * per
        for j in range(TOP_K):
            pltpu.sync_copy(idx_hbm.at[j, wid], idx_v.at[j])

        def read(c, slot):
            return pltpu.make_async_copy(h_hbm.at[pl.ds(base + c * chunk, chunk)], rows_v.at[slot], rsem.at[slot])

        def scatters(c, slot):
            return [pltpu.make_async_copy(rows_v.at[slot], out_hbm.at[idx_v.at[j, c]], ssem.at[slot])
                    for j in range(TOP_K)]

        read(0, 0).start()

        @pl.loop(0, steps, step=2)
        def _(c0):
            for slot in range(2):
                c = c0 + slot
                read(c, slot).wait()
                for cp in scatters(c, slot):
                    cp.start()

                @pl.when(c >= 1)
                def _():
                    for cp in scatters(c - 1, 1 - slot):
                        cp.wait()

                @pl.when(c + 1 < steps)
                def _():
                    read(c + 1, 1 - slot).start()

        for cp in scatters(steps - 1, 1):
            cp.wait()

    return pl.kernel(
        body,
        out_type=jax.ShapeDtypeStruct((n_slots, w), h2p.dtype),
        mesh=_sc_mesh(),
        scratch_types=[pltpu.VMEM((TOP_K, steps, chunk), i32), pltpu.VMEM((2, chunk, w), h2p.dtype),
                       pltpu.SemaphoreType.DMA((2,)), pltpu.SemaphoreType.DMA((2,))],
        name="sc_dispatch_rows",
    )(h2p, idx)


def _expert_kernel(blk_e_ref, blk_cnt_ref, nxt_e_ref, xs_ref, wgu_hbm, bgu_ref, wd_hbm, bd_ref, y_ref,
                   wgu_f32, wd_f32, wgu_bf, wd_bf, sem):
    b = pl.program_id(0)
    e = blk_e_ref[b]
    cnt = blk_cnt_ref[b]
    bm = xs_ref.shape[0]
    de = wd_f32.shape[0]

    def fetch(expert):
        return (pltpu.make_async_copy(wgu_hbm.at[expert], wgu_f32, sem.at[0]),
                pltpu.make_async_copy(wd_hbm.at[expert], wd_f32, sem.at[1]))

    @pl.when(b == 0)
    def _():
        for cp in fetch(e):
            cp.start()

    @pl.when(jnp.logical_or(b == 0, e != blk_e_ref[jnp.maximum(b - 1, 0)]))
    def _():
        for cp in fetch(e):
            cp.wait()

        def convert(i, carry):
            rows = pl.ds(pl.multiple_of(i * WEIGHT_CAST_ROWS, WEIGHT_CAST_ROWS), WEIGHT_CAST_ROWS)
            wgu_bf[rows, :] = wgu_f32[rows, :].astype(bf16)
            wd_bf[rows, :] = wd_f32[rows, :].astype(bf16)
            return carry

        lax.fori_loop(0, wgu_f32.shape[0] // WEIGHT_CAST_ROWS, convert, 0)

        @pl.when(nxt_e_ref[b] >= 0)
        def _():
            for cp in fetch(nxt_e_ref[b]):
                cp.start()

    def ffn_rows(r0, n):
        rows = pl.ds(r0, n)
        valid = r0 + lax.broadcasted_iota(i32, (n, 1), 0) < cnt
        x = jnp.where(valid, _unpack_bf16_pairs(xs_ref[rows, :]), 0.0).astype(bf16)
        gu = jnp.dot(x, wgu_bf[...], preferred_element_type=f32) + bgu_ref[0]
        x_glu = jnp.minimum(gu[:, :de], SWIGLU_LIMIT)
        x_lin = jnp.clip(gu[:, de:], -SWIGLU_LIMIT, SWIGLU_LIMIT)
        act = x_glu * (1.0 / (1.0 + jnp.exp(-SWIGLU_ALPHA * x_glu))) * (x_lin + 1.0)
        y = jnp.dot(act.astype(bf16), wd_bf[...], preferred_element_type=f32) + bd_ref[0]
        y_ref[rows, :] = _pack_bf16_pairs(y)

    @pl.when(cnt == bm)
    def _():
        ffn_rows(0, bm)

    @pl.when(cnt < bm)
    def _():
        y_ref[...] = jnp.zeros_like(y_ref)

        def piece(i, carry):
            ffn_rows(pl.multiple_of(i * MOE_SUB_BLOCK, MOE_SUB_BLOCK), MOE_SUB_BLOCK)
            return carry

        lax.fori_loop(0, (cnt + MOE_SUB_BLOCK - 1) // MOE_SUB_BLOCK, piece, 0)


def _experts(xs, blk_e, blk_cnt, nxt_e, w_gu, b_gu, w_down, b_down):
    n_slots, w = xs.shape
    bm = MOE_BLOCK
    e, d, de2 = w_gu.shape
    de = w_down.shape[1]
    assert de == d, "one row loop converts both weight matrices"
    grid_spec = pltpu.PrefetchScalarGridSpec(
        num_scalar_prefetch=3,
        grid=(n_slots // bm,),
        in_specs=[pl.BlockSpec((bm, w), lambda b, be, bc, ne: (b, 0)),
                  pl.BlockSpec(memory_space=pl.ANY),
                  pl.BlockSpec((1, 1, de2), lambda b, be, bc, ne: (be[b], 0, 0)),
                  pl.BlockSpec(memory_space=pl.ANY),
                  pl.BlockSpec((1, 1, d), lambda b, be, bc, ne: (be[b], 0, 0))],
        out_specs=pl.BlockSpec((bm, w), lambda b, be, bc, ne: (b, 0)),
        scratch_shapes=[pltpu.VMEM((d, de2), f32), pltpu.VMEM((de, d), f32),
                        pltpu.VMEM((d, de2), bf16), pltpu.VMEM((de, d), bf16),
                        pltpu.SemaphoreType.DMA((2,))],
    )
    return pl.pallas_call(
        _expert_kernel,
        grid_spec=grid_spec,
        out_shape=jax.ShapeDtypeStruct((n_slots, w), u32),
        compiler_params=_params("arbitrary"),
        name="moe_experts",
    )(blk_e, blk_cnt, nxt_e, xs, w_gu, b_gu.reshape(e, 1, de2), w_down, b_down.reshape(e, 1, d))


def _sc_gather_rows(table, idx):
    n, w = idx.shape[0], table.shape[1]
    workers = SC_CORES * SC_SUBCORES
    chunk = SC_GATHER_CHUNK
    per = n // workers
    steps = per // chunk
    assert per * workers == n and steps * chunk == per and steps % 2 == 0

    def body(table_hbm, idx_hbm, out_hbm, idx_v, rows_v, gsem, wsem):
        base = (lax.axis_index("s") * SC_CORES + lax.axis_index("c")) * per
        pltpu.sync_copy(idx_hbm.at[pl.ds(base, per)], idx_v)

        def gather(c, slot):
            return pltpu.make_async_copy(table_hbm.at[idx_v.at[pl.ds(c * chunk, chunk)]], rows_v.at[slot],
                                         gsem.at[slot])

        def write(c, slot):
            return pltpu.make_async_copy(rows_v.at[slot], out_hbm.at[pl.ds(base + c * chunk, chunk)],
                                         wsem.at[slot])

        gather(0, 0).start()

        @pl.loop(0, steps, step=2)
        def _(c0):
            for slot in range(2):
                c = c0 + slot
                gather(c, slot).wait()
                write(c, slot).start()

                @pl.when(c >= 1)
                def _():
                    write(c - 1, 1 - slot).wait()

                @pl.when(c + 1 < steps)
                def _():
                    gather(c + 1, 1 - slot).start()

        write(steps - 1, 1).wait()

    return pl.kernel(
        body,
        out_type=jax.ShapeDtypeStruct((n, w), table.dtype),
        mesh=_sc_mesh(),
        scratch_types=[pltpu.VMEM((per,), i32), pltpu.VMEM((2, chunk, w), table.dtype),
                       pltpu.SemaphoreType.DMA((2,)), pltpu.SemaphoreType.DMA((2,))],
        name="sc_gather_rows",
    )(table, idx)


def _combine_kernel(x1_ref, gate_ref, gfin_ref, yg_ref, *rest):
    o_ref = rest[-1]
    acc = x1_ref[...]
    gates = gate_ref[...]
    for j in range(TOP_K):
        acc = acc + gates[:, j:j + 1] * _unpack_bf16_pairs(yg_ref[j])
    o_ref[...] = _rms_scale(acc) * gfin_ref[...]


def _combine(x1, gates, dest, y_slots, g_final, out_prev, tile0, total_tokens):
    t, d = x1.shape
    tm = TOKEN_TILE
    w = y_slots.shape[1]
    yg = _sc_gather_rows(y_slots, dest.reshape(-1)).reshape(TOP_K, t, w)
    in_specs = [pl.BlockSpec((tm, d), lambda i: (i, 0)),
                pl.BlockSpec((tm, TOP_K), lambda i: (i, 0)),
                pl.BlockSpec((1, d), lambda i: (0, 0)),
                pl.BlockSpec((TOP_K, tm, w), lambda i: (0, i, 0))]
    args = [x1, gates, g_final.reshape(1, d), yg]
    aliases = {}
    if out_prev is not None:
        in_specs.append(pl.BlockSpec(memory_space=pl.ANY))
        args.append(out_prev)
        aliases = {len(args) - 1: 0}
    return pl.pallas_call(
        _combine_kernel,
        grid=(t // tm,),
        in_specs=in_specs,
        out_specs=pl.BlockSpec((tm, d), lambda i: (i + tile0, 0)),
        out_shape=jax.ShapeDtypeStruct((total_tokens, d), f32),
        input_output_aliases=aliases,
        compiler_params=_params("parallel"),
        name="moe_combine",
    )(*args)


def _dest_kernel(start_ref, eidx_ref, rank_ref, dest_ref):
    eidx = eidx_ref[...]
    dest = rank_ref[...]
    for e in range(N_EXPERTS):
        dest = dest + jnp.where(eidx == e, start_ref[e], 0)
    dest_ref[...] = dest


def _slot_layout(counts, eidx, rank, n_blocks):
    bm = MOE_BLOCK
    padded = (counts + bm - 1) // bm * bm
    padded_end = jnp.cumsum(padded)
    start = padded_end - padded
    experts = jnp.arange(N_EXPERTS, dtype=i32)
    lookup = lambda table, idx: jnp.sum(jnp.where(idx[..., None] == experts, table, 0), axis=-1)
    dest = pl.pallas_call(
        _dest_kernel,
        grid_spec=pltpu.PrefetchScalarGridSpec(
            num_scalar_prefetch=1, grid=(1,),
            in_specs=[pl.BlockSpec(eidx.shape, lambda i, st: (0, 0)), pl.BlockSpec(rank.shape, lambda i, st: (0, 0))],
            out_specs=pl.BlockSpec(rank.shape, lambda i, st: (0, 0))),
        out_shape=jax.ShapeDtypeStruct(rank.shape, i32),
        name="moe_dest",
    )(start.astype(i32), eidx, rank)
    blk_row = jnp.arange(n_blocks, dtype=i32) * bm
    blk_e = jnp.minimum(jnp.sum((padded_end[None, :] <= blk_row[:, None]).astype(i32), axis=1), N_EXPERTS - 1)
    blk_cnt = jnp.clip(lookup(counts, blk_e) - (blk_row - lookup(start, blk_e)), 0, bm).astype(i32)
    none = jnp.int32(N_EXPERTS)
    nxt_e = jnp.min(jnp.where(blk_e[None, :] > blk_e[:, None], blk_e[None, :], none), axis=1)
    nxt_e = jnp.where(nxt_e == none, -1, nxt_e).astype(i32)
    return dest, blk_e, blk_cnt, nxt_e


def _layer_and_final_norm(x2d, mem, seq, g_mix, g_mem, w_in, w_mem_kv, na_rel_bias, g_grp, w_out, g_ffn,
                          router_w, router_b, w_gu, b_gu, w_down, b_down, g_final):
    t, d = x2d.shape
    b = t // seq
    q_na, k_na, v_na, u_ft, q_mem = _in_proj(x2d, g_mix, w_in.astype(bf16))
    k_mem, v_mem = _mem_kv(mem, g_mem, w_mem_kv.astype(bf16))
    shape3 = lambda a: a.reshape(b, seq, a.shape[-1])
    y_na = _neighbourhood_attention(shape3(q_na), shape3(k_na), shape3(v_na), _na_bias_table(na_rel_bias))
    y_ft = _fourier_mix(shape3(u_ft), _ft_tables(seq))
    y_na, y_ft = y_na.reshape(t, -1), y_ft.reshape(t, -1)
    w_out_bf16 = w_out.astype(bf16)
    rw_hi = router_w.astype(bf16)
    router_w2 = jnp.stack([rw_hi, (router_w - rw_hi.astype(f32)).astype(bf16)])

    tiles = t // TOKEN_TILE
    unit = tiles // sum(MOE_GROUP_SHARES)
    assert unit * sum(MOE_GROUP_SHARES) == tiles
    out = None
    tile0 = 0
    dest = router_b
    for share in MOE_GROUP_SHARES:
        group_tiles = share * unit
        n_blocks = (group_tiles * TOKEN_TILE * TOP_K) // MOE_BLOCK + N_EXPERTS
        x1, h2p, eidx, gates, rank, counts = _mix_out(
            x2d, y_na, y_ft, q_mem, k_mem, v_mem, g_grp, w_out_bf16, g_ffn, router_w2, router_b, seq,
            tile0, group_tiles, dest)
        dest, blk_e, blk_cnt, nxt_e = _slot_layout(counts[:, 0], eidx, rank, n_blocks)
        xs = _dispatch(h2p, dest, n_blocks * MOE_BLOCK)
        y_slots = _experts(xs, blk_e, blk_cnt, nxt_e, w_gu, b_gu, w_down, b_down)
        out = _combine(x1, gates, dest, y_slots, g_final, out, tile0, t)
        tile0 += group_tiles
    return out


def kernel(x, mem, g_mix, g_mem, w_in, w_mem_kv, na_rel_bias, g_grp, w_out, g_ffn, router_w, router_b,
           w_gu, b_gu, w_down, b_down, g_final):
    b, seq, d = x.shape
    depth = w_in.shape[0]
    assert depth == 1, "the final norm is fused into the single layer's combine step"
    out = _layer_and_final_norm(
        x.reshape(b * seq, d), mem, seq, g_mix[0], g_mem[0], w_in[0], w_mem_kv[0], na_rel_bias[0], g_grp[0],
        w_out[0], g_ffn[0], router_w[0], router_b[0], w_gu[0], b_gu[0], w_down[0], b_down[0], g_final)
    return out.reshape(b, seq, d)
```

```python
import functools

import numpy as np
import jax
import jax.numpy as jnp
from jax import lax
from jax.experimental import pallas as pl
from jax.experimental.pallas import tpu as pltpu
from jax.experimental.pallas import tpu_sc as plsc

f32 = jnp.float32
bf16 = jnp.bfloat16
u32 = jnp.uint32
i32 = jnp.int32

GRID_W = 64
NA_HEADS = 8
NA_HEAD_DIM = 64
NA_WIN_ROWS = 8
NA_WIN_COLS = 16
FT_GROUPS = 4
FT_GROUP_DIM = 128
MEM_HEADS = 4
MEM_HEAD_DIM = 128
NA_WIDTH = NA_HEADS * NA_HEAD_DIM
FT_WIDTH = FT_GROUPS * FT_GROUP_DIM
MEM_WIDTH = MEM_HEADS * MEM_HEAD_DIM
N_EXPERTS = 32
TOP_K = 4
SWIGLU_LIMIT = 7.0
SWIGLU_ALPHA = 1.702
EPS = 1e-6

LANES = 128
SUBLANES = 8
VMEM_LIMIT_BYTES = 56 * 1024 * 1024
SC_CORES = 2
SC_SUBCORES = 16
SC_GATHER_CHUNK = 64

TOKEN_TILE = 512
MOE_BLOCK = 512
MOE_SUB_BLOCK = 128
WEIGHT_CAST_ROWS = 128
MOE_GROUP_SHARES = (5, 3)
NA_ROW_UNROLL = 16
FT_N1 = 64
FT_N2 = 128
FT_K1_BLOCK = 8
MASK_VALUE = -1e30


def _params(*semantics):
    return pltpu.CompilerParams(dimension_semantics=semantics, vmem_limit_bytes=VMEM_LIMIT_BYTES)


def _rms_scale(x):
    return x * lax.rsqrt(jnp.mean(x * x, axis=-1, keepdims=True) + EPS)


def _softmax_rows(s):
    p = jnp.exp(s - jnp.max(s, axis=-1, keepdims=True))
    return p / jnp.sum(p, axis=-1, keepdims=True)


def _bf16_bits(x):
    return pltpu.bitcast(x.astype(bf16).astype(f32), u32)


def _pack2(lo, hi):
    return (_bf16_bits(lo) >> 16) | (_bf16_bits(hi) & jnp.uint32(0xFFFF0000))


def _unpack2(w):
    return pltpu.bitcast(w << 16, f32), pltpu.bitcast(w & jnp.uint32(0xFFFF0000), f32)


def _pack_bf16_pairs(x):
    n = x.shape[1] // 2
    return _pack2(x[:, :n], x[:, n:])


def _unpack_bf16_pairs(w):
    return jnp.concatenate(_unpack2(w), axis=1)


def _in_proj_kernel(x_ref, g_ref, w_ref, qna_ref, kna_ref, vna_ref, uft_ref, qmem_ref):
    h = _rms_scale(x_ref[...]) * g_ref[...]
    proj = jnp.dot(h.astype(bf16), w_ref[...], preferred_element_type=f32)
    o = NA_WIDTH
    qna_ref[...] = (proj[:, :o] * (NA_HEAD_DIM ** -0.5)).astype(bf16)
    kna_ref[...] = proj[:, o:2 * o].astype(bf16)
    vna_ref[...] = proj[:, 2 * o:3 * o].astype(bf16)
    uft_ref[...] = _pack_bf16_pairs(proj[:, 3 * o:3 * o + FT_WIDTH])
    qmem_ref[...] = proj[:, 3 * o + FT_WIDTH:].astype(bf16)


def _in_proj(x2d, g_mix, w_in_bf16):
    t, d = x2d.shape
    tm = TOKEN_TILE
    row = lambda w: pl.BlockSpec((tm, w), lambda i: (i, 0))
    return pl.pallas_call(
        _in_proj_kernel,
        grid=(t // tm,),
        in_specs=[row(d), pl.BlockSpec((1, d), lambda i: (0, 0)),
                  pl.BlockSpec(w_in_bf16.shape, lambda i: (0, 0))],
        out_specs=[row(NA_WIDTH), row(NA_WIDTH), row(NA_WIDTH), row(FT_WIDTH // 2), row(MEM_WIDTH)],
        out_shape=[jax.ShapeDtypeStruct((t, NA_WIDTH), bf16)] * 3
        + [jax.ShapeDtypeStruct((t, FT_WIDTH // 2), u32), jax.ShapeDtypeStruct((t, MEM_WIDTH), bf16)],
        compiler_params=_params("parallel"),
        name="in_proj",
    )(x2d, g_mix.reshape(1, d), w_in_bf16)


def _na_bias_table(rel_bias):
    c = np.arange(GRID_W)
    dc_idx = np.clip(c[None, :] - c[:, None], -(NA_WIN_COLS - 1), NA_WIN_COLS - 1) + (NA_WIN_COLS - 1)
    col_start = np.clip(c - NA_WIN_COLS // 2, 0, GRID_W - NA_WIN_COLS)
    col_in = (c[None, :] >= col_start[:, None]) & (c[None, :] < col_start[:, None] + NA_WIN_COLS)
    pick_c = jnp.asarray(dc_idx[:, :, None] == np.arange(2 * NA_WIN_COLS - 1), f32)
    cols = jnp.einsum("hab,qcb->haqc", rel_bias.astype(f32), pick_c, precision=lax.Precision.HIGHEST)
    cols = jnp.where(col_in[None, None], cols, MASK_VALUE)
    tab = jnp.stack([jnp.concatenate([cols[:, j - s + NA_WIN_ROWS - 1] for j in range(NA_WIN_ROWS)], axis=-1)
                     for s in range(NA_WIN_ROWS)])
    return tab.reshape(NA_WIN_ROWS, NA_HEADS // 2, 2 * GRID_W, NA_WIN_ROWS * GRID_W)


def _na_kernel(q_ref, k_ref, v_ref, bias_ref, o_ref):
    rows = q_ref.shape[1] // GRID_W
    win = NA_WIN_ROWS * GRID_W
    first_head = lax.broadcasted_iota(i32, (GRID_W, 2 * NA_HEAD_DIM), 1) < NA_HEAD_DIM

    def body(it, carry):
        scores, values, q_offsets = [], [], []
        for u in range(NA_ROW_UNROLL):
            r = it * NA_ROW_UNROLL + u
            row_start = jnp.clip(r - NA_WIN_ROWS // 2, 0, rows - NA_WIN_ROWS)
            q0 = pl.multiple_of(r * GRID_W, GRID_W)
            k0 = pl.multiple_of(row_start * GRID_W, GRID_W)
            q = q_ref[0, pl.ds(q0, GRID_W), :]
            zero = jnp.zeros_like(q)
            qm = jnp.concatenate([jnp.where(first_head, q, zero), jnp.where(first_head, zero, q)], axis=0)
            s = lax.dot_general(qm, k_ref[0, pl.ds(k0, win), :], (((1,), (1,)), ((), ())),
                                preferred_element_type=f32)
            scores.append(s + bias_ref[r - row_start, 0])
            values.append(v_ref[0, pl.ds(k0, win), :])
            q_offsets.append(q0)
        s = jnp.concatenate(scores, axis=0)
        p = jnp.exp(s - jnp.max(s, axis=-1, keepdims=True))
        inv_den = 1.0 / jnp.sum(p, axis=-1, keepdims=True)
        p = p.astype(bf16)
        for u in range(NA_ROW_UNROLL):
            sl = slice(u * 2 * GRID_W, (u + 1) * 2 * GRID_W)
            o = jnp.dot(p[sl], values[u], preferred_element_type=f32) * inv_den[sl]
            o_ref[0, pl.ds(q_offsets[u], GRID_W), :] = jnp.where(
                first_head, o[:GRID_W], o[GRID_W:]).astype(o_ref.dtype)
        return carry

    lax.fori_loop(0, rows // NA_ROW_UNROLL, body, 0)


def _neighbourhood_attention(q, k, v, bias_tab):
    b, s, _ = q.shape
    pair = 2 * NA_HEAD_DIM
    qkv_spec = pl.BlockSpec((1, s, pair), lambda bi, hp: (bi, 0, hp))
    return pl.pallas_call(
        _na_kernel,
        grid=(b, NA_HEADS // 2),
        in_specs=[qkv_spec, qkv_spec, qkv_spec,
                  pl.BlockSpec((NA_WIN_ROWS, 1, 2 * GRID_W, NA_WIN_ROWS * GRID_W), lambda bi, hp: (0, hp, 0, 0))],
        out_specs=qkv_spec,
        out_shape=jax.ShapeDtypeStruct((b, s, NA_WIDTH), bf16),
        compiler_params=_params("parallel", "parallel"),
        name="neighbourhood_attention",
    )(q, k, v, bias_tab)


def _ft_tables(seq):
    assert seq == FT_N1 * FT_N2
    n_blk = FT_N2 // SUBLANES
    k1 = np.arange(FT_N1)[:, None, None, None]
    sr = np.arange(SUBLANES)[None, :, None, None]
    n1 = np.arange(FT_N1)[None, None, :, None]
    sc = np.arange(SUBLANES)[None, None, None, :]
    stage1 = np.zeros((n_blk, 2, FT_N1, SUBLANES, FT_N1, SUBLANES), np.float64)
    for blk in range(n_blk):
        n = FT_N2 * n1 + SUBLANES * blk + sr
        ang = 2.0 * np.pi * ((k1 * n) % seq) / seq
        eye = (sr == sc)
        stage1[blk, 0] = np.cos(ang) * eye
        stage1[blk, 1] = -np.sin(ang) * eye
    stage1 = stage1.reshape(n_blk, 2 * FT_N1 * SUBLANES, FT_N1 * SUBLANES)
    a = np.arange(FT_N2)
    ang2 = 2.0 * np.pi * ((a[:, None] * a[None, :]) % FT_N2) / FT_N2
    c2, s2 = np.cos(ang2), np.sin(ang2)
    stage2 = np.block([[c2, s2], [-s2, c2]])
    g = np.arange(FT_GROUP_DIM)
    angc = 2.0 * np.pi * ((g[:, None] * g[None, :]) % FT_GROUP_DIM) / FT_GROUP_DIM
    norm = 1.0 / np.sqrt(seq * FT_GROUP_DIM)
    chan = np.concatenate([np.cos(angc), np.sin(angc)], axis=0) * norm
    return (jnp.asarray(stage1, bf16), jnp.asarray(stage2, bf16), jnp.asarray(chan, bf16))


def _ft_stage1_kernel(u_ref, m_ref, z_ref):
    rows = FT_N1 * SUBLANES
    u = _unpack_bf16_pairs(u_ref[0].reshape(rows, FT_WIDTH // 2)).astype(bf16)
    z = jnp.dot(m_ref[0], u, preferred_element_type=f32)
    z_ref[0] = _pack2(z[:rows], z[rows:]).reshape(FT_N1, SUBLANES, FT_WIDTH)


def _ft_stage2_kernel(z_ref, s2_ref, cs_ref, y_ref):
    gd = FT_GROUP_DIM
    for kk in range(FT_K1_BLOCK):
        zz = jnp.concatenate(_unpack2(z_ref[0, kk]), axis=0).astype(bf16)
        x = jnp.dot(s2_ref[...], zz, preferred_element_type=f32)
        outs = []
        for g in range(FT_GROUPS):
            xg = jnp.concatenate([x[:FT_N2, g * gd:(g + 1) * gd], x[FT_N2:, g * gd:(g + 1) * gd]], axis=1)
            outs.append(jnp.dot(xg.astype(bf16), cs_ref[...], preferred_element_type=f32))
        y_ref[0, kk] = jnp.concatenate(outs, axis=1).astype(y_ref.dtype)


def _fourier_mix(u_packed, tables):
    u = u_packed
    b, s, _ = u.shape
    c = FT_WIDTH
    stage1, stage2, chan = tables
    n_blk = FT_N2 // SUBLANES
    z = pl.pallas_call(
        _ft_stage1_kernel,
        grid=(n_blk, b),
        in_specs=[pl.BlockSpec((1, FT_N1, SUBLANES, c // 2), lambda j, bi: (bi, 0, j, 0)),
                  pl.BlockSpec((1,) + stage1.shape[1:], lambda j, bi: (j, 0, 0))],
        out_specs=pl.BlockSpec((1, FT_N1, SUBLANES, c), lambda j, bi: (bi, 0, j, 0)),
        out_shape=jax.ShapeDtypeStruct((b, FT_N1, FT_N2, c), u32),
        compiler_params=_params("parallel", "parallel"),
        name="fourier_stage1",
    )(u.reshape(b, FT_N1, FT_N2, c // 2), stage1)
    y = pl.pallas_call(
        _ft_stage2_kernel,
        grid=(b, FT_N1 // FT_K1_BLOCK),
        in_specs=[pl.BlockSpec((1, FT_K1_BLOCK, FT_N2, c), lambda bi, kb: (bi, kb, 0, 0)),
                  pl.BlockSpec(stage2.shape, lambda bi, kb: (0, 0)),
                  pl.BlockSpec(chan.shape, lambda bi, kb: (0, 0))],
        out_specs=pl.BlockSpec((1, FT_K1_BLOCK, FT_N2, c), lambda bi, kb: (bi, kb, 0, 0)),
        out_shape=jax.ShapeDtypeStruct((b, FT_N1, FT_N2, c), bf16),
        compiler_params=_params("parallel", "parallel"),
        name="fourier_stage2",
    )(z, stage2, chan)
    return y.transpose(0, 2, 1, 3).reshape(b, s, c)


def _mem_kv_kernel(mem_ref, g_ref, w_ref, k_ref, v_ref):
    mn = _rms_scale(mem_ref[0]) * g_ref[...]
    kv = jnp.dot(mn.astype(bf16), w_ref[...], preferred_element_type=f32)
    k_ref[0] = kv[:, :MEM_WIDTH].astype(bf16)
    v_ref[0] = kv[:, MEM_WIDTH:].astype(bf16)


def _mem_kv(mem, g_mem, w_kv_bf16):
    b, m, d = mem.shape
    kv_spec = pl.BlockSpec((1, m, MEM_WIDTH), lambda bi: (bi, 0, 0))
    return pl.pallas_call(
        _mem_kv_kernel,
        grid=(b,),
        in_specs=[pl.BlockSpec((1, m, d), lambda bi: (bi, 0, 0)), pl.BlockSpec((1, d), lambda bi: (0, 0)),
                  pl.BlockSpec(w_kv_bf16.shape, lambda bi: (0, 0))],
        out_specs=[kv_spec, kv_spec],
        out_shape=[jax.ShapeDtypeStruct((b, m, MEM_WIDTH), bf16)] * 2,
        compiler_params=_params("parallel"),
        name="mem_kv",
    )(mem, g_mem.reshape(1, d), w_kv_bf16)


def _mix_out_kernel(x_ref, yna_ref, yft_ref, qm_ref, km_ref, vm_ref, ggrp_ref, wout_ref, gffn_ref, rw_ref,
                    rb_ref, _order_ref, x1_ref, h2p_ref, eidx_ref, gate_ref, rank_ref, cnt_ref, carry_ref):
    tm = x_ref.shape[0]

    @pl.when(pl.program_id(0) == 0)
    def _():
        carry_ref[...] = jnp.zeros_like(carry_ref)

    q = qm_ref[...]
    km = km_ref[0]
    vm = vm_ref[0]
    heads = []
    for h in range(MEM_HEADS):
        sl = slice(h * MEM_HEAD_DIM, (h + 1) * MEM_HEAD_DIM)
        s = lax.dot_general(q[:, sl], km[:, sl], (((1,), (1,)), ((), ())), preferred_element_type=f32)
        p = _softmax_rows(s * (MEM_HEAD_DIM ** -0.5))
        heads.append(jnp.dot(p.astype(bf16), vm[:, sl], preferred_element_type=f32))
    ymem = jnp.concatenate(heads, axis=1)

    g = ggrp_ref[...]
    a, c = NA_WIDTH, NA_WIDTH + FT_WIDTH
    y = jnp.concatenate([_rms_scale(yna_ref[...].astype(f32)) * g[:, :a],
                         _rms_scale(yft_ref[...].astype(f32)) * g[:, a:c],
                         _rms_scale(ymem) * g[:, c:]], axis=1)
    x1 = x_ref[...] + jnp.dot(y.astype(bf16), wout_ref[...], preferred_element_type=f32)
    x1_ref[...] = x1
    h2 = _rms_scale(x1) * gffn_ref[...]
    h2p_ref[...] = _pack_bf16_pairs(h2)

    h_hi = h2.astype(bf16)
    h_lo = (h2 - h_hi.astype(f32)).astype(bf16)
    hh = jnp.dot(h_hi, rw_ref[...], preferred_element_type=f32)
    logits = (hh[:, :LANES] + hh[:, LANES:]
              + jnp.dot(h_lo, rw_ref[:, :LANES], preferred_element_type=f32)) + rb_ref[...]
    l = logits.T[:N_EXPERTS]
    row = lax.broadcasted_iota(i32, (N_EXPERTS, tm), 0).astype(f32)
    vals, idxs, sels = [], [], []
    for _ in range(TOP_K):
        m = jnp.max(l, axis=0, keepdims=True)
        idx = jnp.min(jnp.where(l == m, row, float(N_EXPERTS)), axis=0, keepdims=True)
        sel = row == idx
        vals.append(m)
        idxs.append(idx)
        sels.append(sel)
        l = jnp.where(sel, -jnp.inf, l)
    ex = [jnp.exp(v - vals[0]) for v in vals]
    den = ex[0] + ex[1] + ex[2] + ex[3]

    onehot = (sels[0] | sels[1] | sels[2] | sels[3]).astype(f32)
    earlier = (lax.broadcasted_iota(i32, (tm, tm), 0) < lax.broadcasted_iota(i32, (tm, tm), 1)).astype(bf16)
    before = jnp.dot(onehot.astype(bf16), earlier, preferred_element_type=f32) + carry_ref[...]
    ranks = [jnp.sum(jnp.where(sel, before, 0.0), axis=0, keepdims=True) for sel in sels]
    carry_ref[...] = carry_ref[...] + jnp.sum(onehot, axis=1, keepdims=True)
    cnt_ref[...] = carry_ref[...].astype(i32)

    eidx_ref[...] = jnp.concatenate(idxs, axis=0).astype(i32)
    rank_ref[...] = jnp.concatenate(ranks, axis=0).astype(i32)
    gates_t = jnp.concatenate([e / den for e in ex] + [jnp.zeros((LANES - TOP_K, tm), f32)], axis=0)
    gate_ref[...] = gates_t.T[:, :TOP_K]


def _mix_out(x2d, y_na, y_ft, q_mem, k_mem, v_mem, g_grp, w_out_bf16, g_ffn, router_w2, router_b, seq,
             tile0, n_tiles, order_after):
    d = x2d.shape[1]
    tm = TOKEN_TILE
    t = n_tiles * tm
    steps_per_batch = seq // tm
    m = k_mem.shape[1]
    row_in = lambda w: pl.BlockSpec((tm, w), lambda i: (i + tile0, 0))
    row_out = lambda w: pl.BlockSpec((tm, w), lambda i: (i, 0))
    full = lambda a: pl.BlockSpec(a.shape, lambda i: (0,) * a.ndim)
    kv_spec = pl.BlockSpec((1, m, MEM_WIDTH), lambda i: ((i + tile0) // steps_per_batch, 0, 0))
    g_grp2, g_ffn2 = g_grp.reshape(1, -1), g_ffn.reshape(1, d)
    rb2 = jnp.pad(router_b.reshape(1, N_EXPERTS), ((0, 0), (0, LANES - N_EXPERTS)))
    router_w2 = jnp.pad(router_w2, ((0, 0), (0, 0), (0, LANES - N_EXPERTS)))
    router_w2 = jnp.concatenate([router_w2[0], router_w2[1]], axis=1)
    col_out = pl.BlockSpec((TOP_K, tm), lambda i: (0, i))
    return pl.pallas_call(
        _mix_out_kernel,
        grid=(n_tiles,),
        in_specs=[row_in(d), row_in(NA_WIDTH), row_in(FT_WIDTH), row_in(MEM_WIDTH), kv_spec, kv_spec,
                  full(g_grp2), full(w_out_bf16), full(g_ffn2), full(router_w2), full(rb2),
                  pl.BlockSpec(memory_space=pl.ANY)],
        out_specs=[row_out(d), row_out(d // 2), col_out, row_out(TOP_K), col_out,
                   pl.BlockSpec((N_EXPERTS, 1), lambda i: (0, 0))],
        out_shape=[jax.ShapeDtypeStruct((t, d), f32), jax.ShapeDtypeStruct((t, d // 2), u32),
                   jax.ShapeDtypeStruct((TOP_K, t), i32), jax.ShapeDtypeStruct((t, TOP_K), f32),
                   jax.ShapeDtypeStruct((TOP_K, t), i32), jax.ShapeDtypeStruct((N_EXPERTS, 1), i32)],
        scratch_shapes=[pltpu.VMEM((N_EXPERTS, 1), f32)],
        compiler_params=_params("arbitrary"),
        name="mix_out_router",
    )(x2d, y_na, y_ft, q_mem, k_mem, v_mem, g_grp2, w_out_bf16, g_ffn2, router_w2, rb2, order_after)


def _sc_mesh():
    return plsc.VectorSubcoreMesh(core_axis_name="c", subcore_axis_name="s",
                                  num_cores=SC_CORES, num_subcores=SC_SUBCORES)


def _dispatch(h2p, dest, n_slots):
    t, w = h2p.shape
    workers = SC_CORES * SC_SUBCORES
    chunk = SC_GATHER_CHUNK
    per = t // workers
    steps = per // chunk
    assert per * workers == t and steps * chunk == per and steps % 2 == 0
    idx = dest.reshape(TOP_K, workers, steps, chunk)

    def body(h_hbm, idx_hbm, out_hbm, idx_v, rows_v, rsem, ssem):
        wid = lax.axis_index("s") * SC_CORES + lax.axis_index("c")
        base = wid * per
        for j in range(TOP_K):
            pltpu.sync_copy(idx_hbm.at[j, wid], idx_v.at[j])

        def read(c, slot):
            return pltpu.make_async_copy(h_hbm.at[pl.ds(base + c * chunk, chunk)], rows_v.at[slot], rsem.at[slot])

        def scatters(c, slot):
            return [pltpu.make_async_copy(rows_v.at[slot], out_hbm.at[idx_v.at[j, c]], ssem.at[slot])
                    for j in range(TOP_K)]

        read(0, 0).start()

        @pl.loop(0, steps, step=2)
        def _(c0):
            for slot in range(2):
                c = c0 + slot
                read(c, slot).wait()
                for cp in scatters(c, slot):
                    cp.start()

                @pl.when(c >= 1)
                def _():
                    for cp in scatters(c - 1, 1 - slot):
                        cp.wait()

                @pl.when(c + 1 < steps)
                def _():
                    read(c + 1, 1 - slot).start()

        for cp in scatters(steps - 1, 1):
            cp.wait()

    return pl.kernel(
        body,
        out_type=jax.ShapeDtypeStruct((n_slots, w), h2p.dtype),
        mesh=_sc_mesh(),
        scratch_types=[pltpu.VMEM((TOP_K, steps, chunk), i32), pltpu.VMEM((2, chunk, w), h2p.dtype),
                       pltpu.SemaphoreType.DMA((2,)), pltpu.SemaphoreType.DMA((2,))],
        name="sc_dispatch_rows",
    )(h2p, idx)


def _expert_kernel(blk_e_ref, blk_cnt_ref, nxt_e_ref, xs_ref, wgu_hbm, bgu_ref, wd_hbm, bd_ref, y_ref,
                   wgu_f32, wd_f32, wgu_bf, wd_bf, sem):
    b = pl.program_id(0)
    e = blk_e_ref[b]
    cnt = blk_cnt_ref[b]
    bm = xs_ref.shape[0]
    de = wd_f32.shape[0]

    def fetch(expert):
        return (pltpu.make_async_copy(wgu_hbm.at[expert], wgu_f32, sem.at[0]),
                pltpu.make_async_copy(wd_hbm.at[expert], wd_f32, sem.at[1]))

    @pl.when(b == 0)
    def _():
        for cp in fetch(e):
            cp.start()

    @pl.when(jnp.logical_or(b == 0, e != blk_e_ref[jnp.maximum(b - 1, 0)]))
    def _():
        for cp in fetch(e):
            cp.wait()

        def convert(i, carry):
            rows = pl.ds(pl.multiple_of(i * WEIGHT_CAST_ROWS, WEIGHT_CAST_ROWS), WEIGHT_CAST_ROWS)
            wgu_bf[rows, :] = wgu_f32[rows, :].astype(bf16)
            wd_bf[rows, :] = wd_f32[rows, :].astype(bf16)
            return carry

        lax.fori_loop(0, wgu_f32.shape[0] // WEIGHT_CAST_ROWS, convert, 0)

        @pl.when(nxt_e_ref[b] >= 0)
        def _():
            for cp in fetch(nxt_e_ref[b]):
                cp.start()

    def ffn_rows(r0, n):
        rows = pl.ds(r0, n)
        valid = r0 + lax.broadcasted_iota(i32, (n, 1), 0) < cnt
        x = jnp.where(valid, _unpack_bf16_pairs(xs_ref[rows, :]), 0.0).astype(bf16)
        gu = jnp.dot(x, wgu_bf[...], preferred_element_type=f32) + bgu_ref[0]
        x_glu = jnp.minimum(gu[:, :de], SWIGLU_LIMIT)
        x_lin = jnp.clip(gu[:, de:], -SWIGLU_LIMIT, SWIGLU_LIMIT)
        act = x_glu * (1.0 / (1.0 + jnp.exp(-SWIGLU_ALPHA * x_glu))) * (x_lin + 1.0)
        y = jnp.dot(act.astype(bf16), wd_bf[...], preferred_element_type=f32) + bd_ref[0]
        y_ref[rows, :] = _pack_bf16_pairs(y)

    @pl.when(cnt == bm)
    def _():
        ffn_rows(0, bm)

    @pl.when(cnt < bm)
    def _():
        y_ref[...] = jnp.zeros_like(y_ref)

        def piece(i, carry):
            ffn_rows(pl.multiple_of(i * MOE_SUB_BLOCK, MOE_SUB_BLOCK), MOE_SUB_BLOCK)
            return carry

        lax.fori_loop(0, (cnt + MOE_SUB_BLOCK - 1) // MOE_SUB_BLOCK, piece, 0)


def _experts(xs, blk_e, blk_cnt, nxt_e, w_gu, b_gu, w_down, b_down):
    n_slots, w = xs.shape
    bm = MOE_BLOCK
    e, d, de2 = w_gu.shape
    de = w_down.shape[1]
    assert de == d, "one row loop converts both weight matrices"
    last = n_slots // bm - 1

    def rows_of_block(b, be, bc, ne):
        return (jnp.where(bc[b] > 0, b, last), 0)

    grid_spec = pltpu.PrefetchScalarGridSpec(
        num_scalar_prefetch=3,
        grid=(n_slots // bm,),
        in_specs=[pl.BlockSpec((bm, w), rows_of_block),
                  pl.BlockSpec(memory_space=pl.ANY),
                  pl.BlockSpec((1, 1, de2), lambda b, be, bc, ne: (be[b], 0, 0)),
                  pl.BlockSpec(memory_space=pl.ANY),
                  pl.BlockSpec((1, 1, d), lambda b, be, bc, ne: (be[b], 0, 0))],
        out_specs=pl.BlockSpec((bm, w), rows_of_block),
        scratch_shapes=[pltpu.VMEM((d, de2), f32), pltpu.VMEM((de, d), f32),
                        pltpu.VMEM((d, de2), bf16), pltpu.VMEM((de, d), bf16),
                        pltpu.SemaphoreType.DMA((2,))],
    )
    return pl.pallas_call(
        _expert_kernel,
        grid_spec=grid_spec,
        out_shape=jax.ShapeDtypeStruct((n_slots, w), u32),
        compiler_params=_params("arbitrary"),
        name="moe_experts",
    )(blk_e, blk_cnt, nxt_e, xs, w_gu, b_gu.reshape(e, 1, de2), w_down, b_down.reshape(e, 1, d))


def _sc_gather_rows(table, idx):
    n, w = idx.shape[0], table.shape[1]
    workers = SC_CORES * SC_SUBCORES
    chunk = SC_GATHER_CHUNK
    per = n // workers
    steps = per // chunk
    assert per * workers == n and steps * chunk == per and steps % 2 == 0

    def body(table_hbm, idx_hbm, out_hbm, idx_v, rows_v, gsem, wsem):
        base = (lax.axis_index("s") * SC_CORES + lax.axis_index("c")) * per
        pltpu.sync_copy(idx_hbm.at[pl.ds(base, per)], idx_v)

        def gather(c, slot):
            return pltpu.make_async_copy(table_hbm.at[idx_v.at[pl.ds(c * chunk, chunk)]], rows_v.at[slot],
                                         gsem.at[slot])

        def write(c, slot):
            return pltpu.make_async_copy(rows_v.at[slot], out_hbm.at[pl.ds(base + c * chunk, chunk)],
                                         wsem.at[slot])

        gather(0, 0).start()

        @pl.loop(0, steps, step=2)
        def _(c0):
            for slot in range(2):
                c = c0 + slot
                gather(c, slot).wait()
                write(c, slot).start()

                @pl.when(c >= 1)
                def _():
                    write(c - 1, 1 - slot).wait()

                @pl.when(c + 1 < steps)
                def _():
                    gather(c + 1, 1 - slot).start()

        write(steps - 1, 1).wait()

    return pl.kernel(
        body,
        out_type=jax.ShapeDtypeStruct((n, w), table.dtype),
        mesh=_sc_mesh(),
        scratch_types=[pltpu.VMEM((per,), i32), pltpu.VMEM((2, chunk, w), table.dtype),
                       pltpu.SemaphoreType.DMA((2,)), pltpu.SemaphoreType.DMA((2,))],
        name="sc_gather_rows",
    )(table, idx)


def _combine_kernel(x1_ref, gate_ref, gfin_ref, yg_ref, *rest):
    o_ref = rest[-1]
    acc = x1_ref[...]
    gates = gate_ref[...]
    for j in range(TOP_K):
        acc = acc + gates[:, j:j + 1] * _unpack_bf16_pairs(yg_ref[j])
    o_ref[...] = _rms_scale(acc) * gfin_ref[...]


def _combine(x1, gates, dest, y_slots, g_final, out_prev, tile0, total_tokens):
    t, d = x1.shape
    tm = TOKEN_TILE
    w = y_slots.shape[1]
    yg = _sc_gather_rows(y_slots, dest.reshape(-1)).reshape(TOP_K, t, w)
    in_specs = [pl.BlockSpec((tm, d), lambda i: (i, 0)),
                pl.BlockSpec((tm, TOP_K), lambda i: (i, 0)),
                pl.BlockSpec((1, d), lambda i: (0, 0)),
                pl.BlockSpec((TOP_K, tm, w), lambda i: (0, i, 0))]
    args = [x1, gates, g_final.reshape(1, d), yg]
    aliases = {}
    if out_prev is not None:
        in_specs.append(pl.BlockSpec(memory_space=pl.ANY))
        args.append(out_prev)
        aliases = {len(args) - 1: 0}
    return pl.pallas_call(
        _combine_kernel,
        grid=(t // tm,),
        in_specs=in_specs,
        out_specs=pl.BlockSpec((tm, d), lambda i: (i + tile0, 0)),
        out_shape=jax.ShapeDtypeStruct((total_tokens, d), f32),
        input_output_aliases=aliases,
        compiler_params=_params("parallel"),
        name="moe_combine",
    )(*args)


def _dest_kernel(start_ref, eidx_ref, rank_ref, dest_ref):
    eidx = eidx_ref[...]
    dest = rank_ref[...]
    for e in range(N_EXPERTS):
        dest = dest + jnp.where(eidx == e, start_ref[e], 0)
    dest_ref[...] = dest


def _slot_layout(counts, eidx, rank, n_blocks):
    bm = MOE_BLOCK
    padded = (counts + bm - 1) // bm * bm
    padded_end = jnp.cumsum(padded)
    start = padded_end - padded
    experts = jnp.arange(N_EXPERTS, dtype=i32)
    lookup = lambda table, idx: jnp.sum(jnp.where(idx[..., None] == experts, table, 0), axis=-1)
    dest = pl.pallas_call(
        _dest_kernel,
        grid_spec=pltpu.PrefetchScalarGridSpec(
            num_scalar_prefetch=1, grid=(1,),
            in_specs=[pl.BlockSpec(eidx.shape, lambda i, st: (0, 0)), pl.BlockSpec(rank.shape, lambda i, st: (0, 0))],
            out_specs=pl.BlockSpec(rank.shape, lambda i, st: (0, 0))),
        out_shape=jax.ShapeDtypeStruct(rank.shape, i32),
        name="moe_dest",
    )(start.astype(i32), eidx, rank)
    blk_row = jnp.arange(n_blocks, dtype=i32) * bm
    blk_e = jnp.minimum(jnp.sum((padded_end[None, :] <= blk_row[:, None]).astype(i32), axis=1), N_EXPERTS - 1)
    blk_cnt = jnp.clip(lookup(counts, blk_e) - (blk_row - lookup(start, blk_e)), 0, bm).astype(i32)
    none = jnp.int32(N_EXPERTS)
    nxt_e = jnp.min(jnp.where(blk_e[None, :] > blk_e[:, None], blk_e[None, :], none), axis=1)
    nxt_e = jnp.where(nxt_e == none, -1, nxt_e).astype(i32)
    return dest, blk_e, blk_cnt, nxt_e


def _layer_and_final_norm(x2d, mem, seq, g_mix, g_mem, w_in, w_mem_kv, na_rel_bias, g_grp, w_out, g_ffn,
                          router_w, router_b, w_gu, b_gu, w_down, b_down, g_final):
    t, d = x2d.shape
    b = t // seq
    q_na, k_na, v_na, u_ft, q_mem = _in_proj(x2d, g_mix, w_in.astype(bf16))
    k_mem, v_mem = _mem_kv(mem, g_mem, w_mem_kv.astype(bf16))
    shape3 = lambda a: a.reshape(b, seq, a.shape[-1])
    y_na = _neighbourhood_attention(shape3(q_na), shape3(k_na), shape3(v_na), _na_bias_table(na_rel_bias))
    y_ft = _fourier_mix(shape3(u_ft), _ft_tables(seq))
    y_na, y_ft = y_na.reshape(t, -1), y_ft.reshape(t, -1)
    w_out_bf16 = w_out.astype(bf16)
    rw_hi = router_w.astype(bf16)
    router_w2 = jnp.stack([rw_hi, (router_w - rw_hi.astype(f32)).astype(bf16)])

    tiles = t // TOKEN_TILE
    unit = tiles // sum(MOE_GROUP_SHARES)
    assert unit * sum(MOE_GROUP_SHARES) == tiles
    out = None
    tile0 = 0
    dest = router_b
    for share in MOE_GROUP_SHARES:
        group_tiles = share * unit
        n_blocks = (group_tiles * TOKEN_TILE * TOP_K) // MOE_BLOCK + N_EXPERTS
        x1, h2p, eidx, gates, rank, counts = _mix_out(
            x2d, y_na, y_ft, q_mem, k_mem, v_mem, g_grp, w_out_bf16, g_ffn, router_w2, router_b, seq,
            tile0, group_tiles, dest)
        dest, blk_e, blk_cnt, nxt_e = _slot_layout(counts[:, 0], eidx, rank, n_blocks)
        xs = _dispatch(h2p, dest, n_blocks * MOE_BLOCK)
        y_slots = _experts(xs, blk_e, blk_cnt, nxt_e, w_gu, b_gu, w_down, b_down)
        out = _combine(x1, gates, dest, y_slots, g_final, out, tile0, t)
        tile0 += group_tiles
    return out


def kernel(x, mem, g_mix, g_mem, w_in, w_mem_kv, na_rel_bias, g_grp, w_out, g_ffn, router_w, router_b,
           w_gu, b_gu, w_down, b_down, g_final):
    b, seq, d = x.shape
    depth = w_in.shape[0]
    assert depth == 1, "the final norm is fused into the single layer's combine step"
    out = _layer_and_final_norm(
        x.reshape(b * seq, d), mem, seq, g_mix[0], g_mem[0], w_in[0], w_mem_kv[0], na_rel_bias[0], g_grp[0],
        w_out[0], g_ffn[0], router_w[0], router_b[0], w_gu[0], b_gu[0], w_down[0], b_down[0], g_final)
    return out.reshape(b, seq, d)
```

```python
import functools

import numpy as np
import jax
import jax.numpy as jnp
from jax import lax
from jax.experimental import pallas as pl
from jax.experimental.pallas import tpu as pltpu
from jax.experimental.pallas import tpu_sc as plsc

f32 = jnp.float32
bf16 = jnp.bfloat16
u32 = jnp.uint32
i32 = jnp.int32

GRID_W = 64
NA_HEADS = 8
NA_HEAD_DIM = 64
NA_WIN_ROWS = 8
NA_WIN_COLS = 16
FT_GROUPS = 4
FT_GROUP_DIM = 128
MEM_HEADS = 4
MEM_HEAD_DIM = 128
NA_WIDTH = NA_HEADS * NA_HEAD_DIM
FT_WIDTH = FT_GROUPS * FT_GROUP_DIM
MEM_WIDTH = MEM_HEADS * MEM_HEAD_DIM
N_EXPERTS = 32
TOP_K = 4
SWIGLU_LIMIT = 7.0
SWIGLU_ALPHA = 1.702
EPS = 1e-6

LANES = 128
SUBLANES = 8
VMEM_LIMIT_BYTES = 56 * 1024 * 1024
SC_CORES = 2
SC_SUBCORES = 16
SC_GATHER_CHUNK = 64

TOKEN_TILE = 512
MOE_BLOCK = 512
MOE_SUB_BLOCK = 128
WEIGHT_CAST_ROWS = 128
MOE_GROUP_SHARES = (5, 3)
NA_ROW_UNROLL = 16
FT_N1 = 64
FT_N2 = 128
FT_K1_BLOCK = 8
LOG2_E = 1.4426950408889634
MASK_VALUE = -jnp.inf


def _params(*semantics):
    return pltpu.CompilerParams(dimension_semantics=semantics, vmem_limit_bytes=VMEM_LIMIT_BYTES)


def _rms_scale(x):
    return x * lax.rsqrt(jnp.mean(x * x, axis=-1, keepdims=True) + EPS)


def _softmax_rows(s):
    p = jnp.exp(s - jnp.max(s, axis=-1, keepdims=True))
    return p / jnp.sum(p, axis=-1, keepdims=True)


def _bf16_bits(x):
    return pltpu.bitcast(x.astype(bf16).astype(f32), u32)


def _pack2(lo, hi):
    return (_bf16_bits(lo) >> 16) | (_bf16_bits(hi) & jnp.uint32(0xFFFF0000))


def _unpack2(w):
    return pltpu.bitcast(w << 16, f32), pltpu.bitcast(w & jnp.uint32(0xFFFF0000), f32)


def _pack_bf16_pairs(x):
    n = x.shape[1] // 2
    return _pack2(x[:, :n], x[:, n:])


def _unpack_bf16_pairs(w):
    return jnp.concatenate(_unpack2(w), axis=1)


def _in_proj_kernel(x_ref, g_ref, w_ref, qna_ref, kna_ref, vna_ref, uft_ref, qmem_ref):
    h = _rms_scale(x_ref[...]) * g_ref[...]
    proj = jnp.dot(h.astype(bf16), w_ref[...], preferred_element_type=f32)
    o = NA_WIDTH
    qna_ref[...] = (proj[:, :o] * (NA_HEAD_DIM ** -0.5 * LOG2_E)).astype(bf16)
    kna_ref[...] = proj[:, o:2 * o].astype(bf16)
    vna_ref[...] = proj[:, 2 * o:3 * o].astype(bf16)
    uft_ref[...] = _pack_bf16_pairs(proj[:, 3 * o:3 * o + FT_WIDTH])
    qmem_ref[...] = proj[:, 3 * o + FT_WIDTH:].astype(bf16)


def _in_proj(x2d, g_mix, w_in_bf16):
    t, d = x2d.shape
    tm = TOKEN_TILE
    row = lambda w: pl.BlockSpec((tm, w), lambda i: (i, 0))
    return pl.pallas_call(
        _in_proj_kernel,
        grid=(t // tm,),
        in_specs=[row(d), pl.BlockSpec((1, d), lambda i: (0, 0)),
                  pl.BlockSpec(w_in_bf16.shape, lambda i: (0, 0))],
        out_specs=[row(NA_WIDTH), row(NA_WIDTH), row(NA_WIDTH), row(FT_WIDTH // 2), row(MEM_WIDTH)],
        out_shape=[jax.ShapeDtypeStruct((t, NA_WIDTH), bf16)] * 3
        + [jax.ShapeDtypeStruct((t, FT_WIDTH // 2), u32), jax.ShapeDtypeStruct((t, MEM_WIDTH), bf16)],
        compiler_params=_params("parallel"),
        name="in_proj",
    )(x2d, g_mix.reshape(1, d), w_in_bf16)


def _na_bias_table(rel_bias):
    c = np.arange(GRID_W)
    dc_idx = np.clip(c[None, :] - c[:, None], -(NA_WIN_COLS - 1), NA_WIN_COLS - 1) + (NA_WIN_COLS - 1)
    col_start = np.clip(c - NA_WIN_COLS // 2, 0, GRID_W - NA_WIN_COLS)
    col_in = (c[None, :] >= col_start[:, None]) & (c[None, :] < col_start[:, None] + NA_WIN_COLS)
    pick_c = jnp.asarray(dc_idx[:, :, None] == np.arange(2 * NA_WIN_COLS - 1), f32)
    cols = jnp.einsum("hab,qcb->haqc", rel_bias.astype(f32), pick_c, precision=lax.Precision.HIGHEST)
    cols = jnp.where(col_in[None, None], cols * LOG2_E, MASK_VALUE)
    tab = jnp.stack([jnp.concatenate([cols[:, j - s + NA_WIN_ROWS - 1] for j in range(NA_WIN_ROWS)], axis=-1)
                     for s in range(NA_WIN_ROWS)])
    return tab.reshape(NA_WIN_ROWS, NA_HEADS // 2, 2 * GRID_W, NA_WIN_ROWS * GRID_W)


def _na_kernel(q_ref, k_ref, v_ref, bias_ref, o_ref):
    rows = q_ref.shape[1] // GRID_W
    win = NA_WIN_ROWS * GRID_W
    first_head = lax.broadcasted_iota(i32, (GRID_W, 2 * NA_HEAD_DIM), 1) < NA_HEAD_DIM

    def body(it, carry):
        scores, values, q_offsets = [], [], []
        for u in range(NA_ROW_UNROLL):
            r = it * NA_ROW_UNROLL + u
            row_start = jnp.clip(r - NA_WIN_ROWS // 2, 0, rows - NA_WIN_ROWS)
            q0 = pl.multiple_of(r * GRID_W, GRID_W)
            k0 = pl.multiple_of(row_start * GRID_W, GRID_W)
            q = q_ref[0, pl.ds(q0, GRID_W), :]
            zero = jnp.zeros_like(q)
            qm = jnp.concatenate([jnp.where(first_head, q, zero), jnp.where(first_head, zero, q)], axis=0)
            s = lax.dot_general(qm, k_ref[0, pl.ds(k0, win), :], (((1,), (1,)), ((), ())),
                                preferred_element_type=f32)
            scores.append(s + bias_ref[r - row_start, 0])
            values.append(v_ref[0, pl.ds(k0, win), :])
            q_offsets.append(q0)
        s = jnp.concatenate(scores, axis=0)
        p = jnp.exp2(s - jnp.max(s, axis=-1, keepdims=True))
        inv_den = 1.0 / jnp.sum(p, axis=-1, keepdims=True)
        p = p.astype(bf16)
        for u in range(NA_ROW_UNROLL):
            sl = slice(u * 2 * GRID_W, (u + 1) * 2 * GRID_W)
            o = jnp.dot(p[sl], values[u], preferred_element_type=f32) * inv_den[sl]
            o_ref[0, pl.ds(q_offsets[u], GRID_W), :] = jnp.where(
                first_head, o[:GRID_W], o[GRID_W:]).astype(o_ref.dtype)
        return carry

    lax.fori_loop(0, rows // NA_ROW_UNROLL, body, 0)


def _neighbourhood_attention(q, k, v, bias_tab):
    b, s, _ = q.shape
    pair = 2 * NA_HEAD_DIM
    qkv_spec = pl.BlockSpec((1, s, pair), lambda bi, hp: (bi, 0, hp))
    return pl.pallas_call(
        _na_kernel,
        grid=(b, NA_HEADS // 2),
        in_specs=[qkv_spec, qkv_spec, qkv_spec,
                  pl.BlockSpec((NA_WIN_ROWS, 1, 2 * GRID_W, NA_WIN_ROWS * GRID_W), lambda bi, hp: (0, hp, 0, 0))],
        out_specs=qkv_spec,
        out_shape=jax.ShapeDtypeStruct((b, s, NA_WIDTH), bf16),
        compiler_params=_params("parallel", "parallel"),
        name="neighbourhood_attention",
    )(q, k, v, bias_tab)


def _ft_tables(seq):
    assert seq == FT_N1 * FT_N2
    n_blk = FT_N2 // SUBLANES
    k1 = np.arange(FT_N1)[:, None, None, None]
    sr = np.arange(SUBLANES)[None, :, None, None]
    n1 = np.arange(FT_N1)[None, None, :, None]
    sc = np.arange(SUBLANES)[None, None, None, :]
    stage1 = np.zeros((n_blk, 2, FT_N1, SUBLANES, FT_N1, SUBLANES), np.float64)
    for blk in range(n_blk):
        n = FT_N2 * n1 + SUBLANES * blk + sr
        ang = 2.0 * np.pi * ((k1 * n) % seq) / seq
        eye = (sr == sc)
        stage1[blk, 0] = np.cos(ang) * eye
        stage1[blk, 1] = -np.sin(ang) * eye
    stage1 = stage1.reshape(n_blk, 2 * FT_N1 * SUBLANES, FT_N1 * SUBLANES)
    a = np.arange(FT_N2)
    ang2 = 2.0 * np.pi * ((a[:, None] * a[None, :]) % FT_N2) / FT_N2
    c2, s2 = np.cos(ang2), np.sin(ang2)
    stage2 = np.block([[c2, s2], [-s2, c2]])
    g = np.arange(FT_GROUP_DIM)
    angc = 2.0 * np.pi * ((g[:, None] * g[None, :]) % FT_GROUP_DIM) / FT_GROUP_DIM
    norm = 1.0 / np.sqrt(seq * FT_GROUP_DIM)
    chan = np.concatenate([np.cos(angc), np.sin(angc)], axis=0) * norm
    return (jnp.asarray(stage1, bf16), jnp.asarray(stage2, bf16), jnp.asarray(chan, bf16))


def _ft_stage1_kernel(u_ref, m_ref, z_ref):
    rows = FT_N1 * SUBLANES
    u = _unpack_bf16_pairs(u_ref[0].reshape(rows, FT_WIDTH // 2)).astype(bf16)
    z = jnp.dot(m_ref[0], u, preferred_element_type=f32)
    z_ref[0] = _pack2(z[:rows], z[rows:]).reshape(FT_N1, SUBLANES, FT_WIDTH)


def _ft_stage2_kernel(z_ref, s2_ref, cs_ref, y_ref):
    gd = FT_GROUP_DIM
    for kk in range(FT_K1_BLOCK):
        zz = jnp.concatenate(_unpack2(z_ref[0, kk]), axis=0).astype(bf16)
        x = jnp.dot(s2_ref[...], zz, preferred_element_type=f32)
        outs = []
        for g in range(FT_GROUPS):
            xg = jnp.concatenate([x[:FT_N2, g * gd:(g + 1) * gd], x[FT_N2:, g * gd:(g + 1) * gd]], axis=1)
            outs.append(jnp.dot(xg.astype(bf16), cs_ref[...], preferred_element_type=f32))
        y_ref[0, kk] = jnp.concatenate(outs, axis=1).astype(y_ref.dtype)


def _fourier_mix(u_packed, tables):
    u = u_packed
    b, s, _ = u.shape
    c = FT_WIDTH
    stage1, stage2, chan = tables
    n_blk = FT_N2 // SUBLANES
    z = pl.pallas_call(
        _ft_stage1_kernel,
        grid=(n_blk, b),
        in_specs=[pl.BlockSpec((1, FT_N1, SUBLANES, c // 2), lambda j, bi: (bi, 0, j, 0)),
                  pl.BlockSpec((1,) + stage1.shape[1:], lambda j, bi: (j, 0, 0))],
        out_specs=pl.BlockSpec((1, FT_N1, SUBLANES, c), lambda j, bi: (bi, 0, j, 0)),
        out_shape=jax.ShapeDtypeStruct((b, FT_N1, FT_N2, c), u32),
        compiler_params=_params("parallel", "parallel"),
        name="fourier_stage1",
    )(u.reshape(b, FT_N1, FT_N2, c // 2), stage1)
    y = pl.pallas_call(
        _ft_stage2_kernel,
        grid=(b, FT_N1 // FT_K1_BLOCK),
        in_specs=[pl.BlockSpec((1, FT_K1_BLOCK, FT_N2, c), lambda bi, kb: (bi, kb, 0, 0)),
                  pl.BlockSpec(stage2.shape, lambda bi, kb: (0, 0)),
                  pl.BlockSpec(chan.shape, lambda bi, kb: (0, 0))],
        out_specs=pl.BlockSpec((1, FT_K1_BLOCK, FT_N2, c), lambda bi, kb: (bi, kb, 0, 0)),
        out_shape=jax.ShapeDtypeStruct((b, FT_N1, FT_N2, c), bf16),
        compiler_params=_params("parallel", "parallel"),
        name="fourier_stage2",
    )(z, stage2, chan)
    return y.transpose(0, 2, 1, 3).reshape(b, s, c)


def _mem_kv_kernel(mem_ref, g_ref, w_ref, k_ref, v_ref):
    mn = _rms_scale(mem_ref[0]) * g_ref[...]
    kv = jnp.dot(mn.astype(bf16), w_ref[...], preferred_element_type=f32)
    k_ref[0] = kv[:, :MEM_WIDTH].astype(bf16)
    v_ref[0] = kv[:, MEM_WIDTH:].astype(bf16)


def _mem_kv(mem, g_mem, w_kv_bf16):
    b, m, d = mem.shape
    kv_spec = pl.BlockSpec((1, m, MEM_WIDTH), lambda bi: (bi, 0, 0))
    return pl.pallas_call(
        _mem_kv_kernel,
        grid=(b,),
        in_specs=[pl.BlockSpec((1, m, d), lambda bi: (bi, 0, 0)), pl.BlockSpec((1, d), lambda bi: (0, 0)),
                  pl.BlockSpec(w_kv_bf16.shape, lambda bi: (0, 0))],
        out_specs=[kv_spec, kv_spec],
        out_shape=[jax.ShapeDtypeStruct((b, m, MEM_WIDTH), bf16)] * 2,
        compiler_params=_params("parallel"),
        name="mem_kv",
    )(mem, g_mem.reshape(1, d), w_kv_bf16)


def _mix_out_kernel(x_ref, yna_ref, yft_ref, qm_ref, km_ref, vm_ref, ggrp_ref, wout_ref, gffn_ref, rw_ref,
                    rb_ref, _order_ref, x1_ref, h2p_ref, eidx_ref, gate_ref, rank_ref, cnt_ref, carry_ref):
    tm = x_ref.shape[0]

    @pl.when(pl.program_id(0) == 0)
    def _():
        carry_ref[...] = jnp.zeros_like(carry_ref)

    q = qm_ref[...]
    km = km_ref[0]
    vm = vm_ref[0]
    heads = []
    for h in range(MEM_HEADS):
        sl = slice(h * MEM_HEAD_DIM, (h + 1) * MEM_HEAD_DIM)
        s = lax.dot_general(q[:, sl], km[:, sl], (((1,), (1,)), ((), ())), preferred_element_type=f32)
        p = _softmax_rows(s * (MEM_HEAD_DIM ** -0.5))
        heads.append(jnp.dot(p.astype(bf16), vm[:, sl], preferred_element_type=f32))
    ymem = jnp.concatenate(heads, axis=1)

    g = ggrp_ref[...]
    a, c = NA_WIDTH, NA_WIDTH + FT_WIDTH
    y = jnp.concatenate([_rms_scale(yna_ref[...].astype(f32)) * g[:, :a],
                         _rms_scale(yft_ref[...].astype(f32)) * g[:, a:c],
                         _rms_scale(ymem) * g[:, c:]], axis=1)
    x1 = x_ref[...] + jnp.dot(y.astype(bf16), wout_ref[...], preferred_element_type=f32)
    x1_ref[...] = x1
    h2 = _rms_scale(x1) * gffn_ref[...]
    h2p_ref[...] = _pack_bf16_pairs(h2)

    h_hi = h2.astype(bf16)
    h_lo = (h2 - h_hi.astype(f32)).astype(bf16)
    hh = jnp.dot(h_hi, rw_ref[...], preferred_element_type=f32)
    logits = (hh[:, :LANES] + hh[:, LANES:]
              + jnp.dot(h_lo, rw_ref[:, :LANES], preferred_element_type=f32)) + rb_ref[...]
    l = logits.T[:N_EXPERTS]
    row = lax.broadcasted_iota(i32, (N_EXPERTS, tm), 0).astype(f32)
    vals, idxs, sels = [], [], []
    for _ in range(TOP_K):
        m = jnp.max(l, axis=0, keepdims=True)
        idx = jnp.min(jnp.where(l == m, row, float(N_EXPERTS)), axis=0, keepdims=True)
        sel = row == idx
        vals.append(m)
        idxs.append(idx)
        sels.append(sel)
        l = jnp.where(sel, -jnp.inf, l)
    ex = [jnp.exp(v - vals[0]) for v in vals]
    den = ex[0] + ex[1] + ex[2] + ex[3]

    onehot = (sels[0] | sels[1] | sels[2] | sels[3]).astype(f32)
    earlier = (lax.broadcasted_iota(i32, (tm, tm), 0) < lax.broadcasted_iota(i32, (tm, tm), 1)).astype(bf16)
    before = jnp.dot(onehot.astype(bf16), earlier, preferred_element_type=f32) + carry_ref[...]
    ranks = [jnp.sum(jnp.where(sel, before, 0.0), axis=0, keepdims=True) for sel in sels]
    carry_ref[...] = carry_ref[...] + jnp.sum(onehot, axis=1, keepdims=True)
    cnt_ref[...] = carry_ref[...].astype(i32)

    eidx_ref[...] = jnp.concatenate(idxs, axis=0).astype(i32)
    rank_ref[...] = jnp.concatenate(ranks, axis=0).astype(i32)
    gates_t = jnp.concatenate([e / den for e in ex] + [jnp.zeros((LANES - TOP_K, tm), f32)], axis=0)
    gate_ref[...] = gates_t.T[:, :TOP_K]


def _mix_out(x2d, y_na, y_ft, q_mem, k_mem, v_mem, g_grp, w_out_bf16, g_ffn, router_w2, router_b, seq,
             tile0, n_tiles, order_after):
    d = x2d.shape[1]
    tm = TOKEN_TILE
    t = n_tiles * tm
    steps_per_batch = seq // tm
    m = k_mem.shape[1]
    row_in = lambda w: pl.BlockSpec((tm, w), lambda i: (i + tile0, 0))
    row_out = lambda w: pl.BlockSpec((tm, w), lambda i: (i, 0))
    full = lambda a: pl.BlockSpec(a.shape, lambda i: (0,) * a.ndim)
    kv_spec = pl.BlockSpec((1, m, MEM_WIDTH), lambda i: ((i + tile0) // steps_per_batch, 0, 0))
    g_grp2, g_ffn2 = g_grp.reshape(1, -1), g_ffn.reshape(1, d)
    rb2 = jnp.pad(router_b.reshape(1, N_EXPERTS), ((0, 0), (0, LANES - N_EXPERTS)))
    router_w2 = jnp.pad(router_w2, ((0, 0), (0, 0), (0, LANES - N_EXPERTS)))
    router_w2 = jnp.concatenate([router_w2[0], router_w2[1]], axis=1)
    col_out = pl.BlockSpec((TOP_K, tm), lambda i: (0, i))
    return pl.pallas_call(
        _mix_out_kernel,
        grid=(n_tiles,),
        in_specs=[row_in(d), row_in(NA_WIDTH), row_in(FT_WIDTH), row_in(MEM_WIDTH), kv_spec, kv_spec,
                  full(g_grp2), full(w_out_bf16), full(g_ffn2), full(router_w2), full(rb2),
                  pl.BlockSpec(memory_space=pl.ANY)],
        out_specs=[row_out(d), row_out(d // 2), col_out, row_out(TOP_K), col_out,
                   pl.BlockSpec((N_EXPERTS, 1), lambda i: (0, 0))],
        out_shape=[jax.ShapeDtypeStruct((t, d), f32), jax.ShapeDtypeStruct((t, d // 2), u32),
                   jax.ShapeDtypeStruct((TOP_K, t), i32), jax.ShapeDtypeStruct((t, TOP_K), f32),
                   jax.ShapeDtypeStruct((TOP_K, t), i32), jax.ShapeDtypeStruct((N_EXPERTS, 1), i32)],
        scratch_shapes=[pltpu.VMEM((N_EXPERTS, 1), f32)],
        compiler_params=_params("arbitrary"),
        name="mix_out_router",
    )(x2d, y_na, y_ft, q_mem, k_mem, v_mem, g_grp2, w_out_bf16, g_ffn2, router_w2, rb2, order_after)


def _sc_mesh():
    return plsc.VectorSubcoreMesh(core_axis_name="c", subcore_axis_name="s",
                                  num_cores=SC_CORES, num_subcores=SC_SUBCORES)


def _dispatch(h2p, dest, n_slots):
    t, w = h2p.shape
    workers = SC_CORES * SC_SUBCORES
    chunk = SC_GATHER_CHUNK
    per = t // workers
    steps = per // chunk
    assert per * workers == t and steps * chunk == per and steps % 2 == 0
    idx = dest.reshape(TOP_K, workers, steps, chunk)

    def body(h_hbm, idx_hbm, out_hbm, idx_v, rows_v, rsem, ssem):
        wid = lax.axis_index("s") * SC_CORES + lax.axis_index("c")
        base = wid * per
        for j in range(TOP_K):
            pltpu.sync_copy(idx_hbm.at[j, wid], idx_v.at[j])

        def read(c, slot):
            return pltpu.make_async_copy(h_hbm.at[pl.ds(base + c * chunk, chunk)], rows_v.at[slot], rsem.at[slot])

        def scatters(c, slot):
            return [pltpu.make_async_copy(rows_v.at[slot], out_hbm.at[idx_v.at[j, c]], ssem.at[slot])
                    for j in range(TOP_K)]

        read(0, 0).start()

        @pl.loop(0, steps, step=2)
        def _(c0):
            for slot in range(2):
                c = c0 + slot
                read(c, slot).wait()
                for cp in scatters(c, slot):
                    cp.start()

                @pl.when(c >= 1)
                def _():
                    for cp in scatters(c - 1, 1 - slot):
                        cp.wait()

                @pl.when(c + 1 < steps)
                def _():
                    read(c + 1, 1 - slot).start()

        for cp in scatters(steps - 1, 1):
            cp.wait()

    return pl.kernel(
        body,
        out_type=jax.ShapeDtypeStruct((n_slots, w), h2p.dtype),
        mesh=_sc_mesh(),
        scratch_types=[pltpu.VMEM((TOP_K, steps, chunk), i32), pltpu.VMEM((2, chunk, w), h2p.dtype),
                       pltpu.SemaphoreType.DMA((2,)), pltpu.SemaphoreType.DMA((2,))],
        name="sc_dispatch_rows",
    )(h2p, idx)


def _expert_kernel(blk_e_ref, blk_cnt_ref, nxt_e_ref, xs_ref, wgu_hbm, bgu_ref, wd_hbm, bd_ref, y_ref,
                   wgu_f32, wd_f32, wgu_bf, wd_bf, sem):
    b = pl.program_id(0)
    e = blk_e_ref[b]
    cnt = blk_cnt_ref[b]
    bm = xs_ref.shape[0]
    de = wd_f32.shape[0]

    def fetch(expert):
        return (pltpu.make_async_copy(wgu_hbm.at[expert], wgu_f32, sem.at[0]),
                pltpu.make_async_copy(wd_hbm.at[expert], wd_f32, sem.at[1]))

    @pl.when(b == 0)
    def _():
        for cp in fetch(e):
            cp.start()

    @pl.when(jnp.logical_or(b == 0, e != blk_e_ref[jnp.maximum(b - 1, 0)]))
    def _():
        for cp in fetch(e):
            cp.wait()

        def convert(i, carry):
            rows = pl.ds(pl.multiple_of(i * WEIGHT_CAST_ROWS, WEIGHT_CAST_ROWS), WEIGHT_CAST_ROWS)
            wgu_bf[rows, :] = wgu_f32[rows, :].astype(bf16)
            wd_bf[rows, :] = wd_f32[rows, :].astype(bf16)
            return carry

        lax.fori_loop(0, wgu_f32.shape[0] // WEIGHT_CAST_ROWS, convert, 0)

        @pl.when(nxt_e_ref[b] >= 0)
        def _():
            for cp in fetch(nxt_e_ref[b]):
                cp.start()

    def ffn_rows(r0, n):
        rows = pl.ds(r0, n)
        valid = r0 + lax.broadcasted_iota(i32, (n, 1), 0) < cnt
        x = jnp.where(valid, _unpack_bf16_pairs(xs_ref[rows, :]), 0.0).astype(bf16)
        gu = jnp.dot(x, wgu_bf[...], preferred_element_type=f32) + bgu_ref[0]
        x_glu = jnp.minimum(gu[:, :de], SWIGLU_LIMIT)
        x_lin = jnp.clip(gu[:, de:], -SWIGLU_LIMIT, SWIGLU_LIMIT)
        act = x_glu * (1.0 / (1.0 + jnp.exp(-SWIGLU_ALPHA * x_glu))) * (x_lin + 1.0)
        y = jnp.dot(act.astype(bf16), wd_bf[...], preferred_element_type=f32) + bd_ref[0]
        y_ref[rows, :] = _pack_bf16_pairs(y)

    @pl.when(cnt == bm)
    def _():
        ffn_rows(0, bm)

    @pl.when(cnt < bm)
    def _():
        y_ref[...] = jnp.zeros_like(y_ref)

        def piece(i, carry):
            ffn_rows(pl.multiple_of(i * MOE_SUB_BLOCK, MOE_SUB_BLOCK), MOE_SUB_BLOCK)
            return carry

        lax.fori_loop(0, (cnt + MOE_SUB_BLOCK - 1) // MOE_SUB_BLOCK, piece, 0)


def _experts(xs, blk_e, blk_cnt, nxt_e, w_gu, b_gu, w_down, b_down):
    n_slots, w = xs.shape
    bm = MOE_BLOCK
    e, d, de2 = w_gu.shape
    de = w_down.shape[1]
    assert de == d, "one row loop converts both weight matrices"
    last = n_slots // bm - 1

    def rows_of_block(b, be, bc, ne):
        return (jnp.where(bc[b] > 0, b, last), 0)

    grid_spec = pltpu.PrefetchScalarGridSpec(
        num_scalar_prefetch=3,
        grid=(n_slots // bm,),
        in_specs=[pl.BlockSpec((bm, w), rows_of_block),
                  pl.BlockSpec(memory_space=pl.ANY),
                  pl.BlockSpec((1, 1, de2), lambda b, be, bc, ne: (be[b], 0, 0)),
                  pl.BlockSpec(memory_space=pl.ANY),
                  pl.BlockSpec((1, 1, d), lambda b, be, bc, ne: (be[b], 0, 0))],
        out_specs=pl.BlockSpec((bm, w), rows_of_block),
        scratch_shapes=[pltpu.VMEM((d, de2), f32), pltpu.VMEM((de, d), f32),
                        pltpu.VMEM((d, de2), bf16), pltpu.VMEM((de, d), bf16),
                        pltpu.SemaphoreType.DMA((2,))],
    )
    return pl.pallas_call(
        _expert_kernel,
        grid_spec=grid_spec,
        out_shape=jax.ShapeDtypeStruct((n_slots, w), u32),
        compiler_params=_params("arbitrary"),
        name="moe_experts",
    )(blk_e, blk_cnt, nxt_e, xs, w_gu, b_gu.reshape(e, 1, de2), w_down, b_down.reshape(e, 1, d))


def _sc_gather_rows(table, idx):
    n, w = idx.shape[0], table.shape[1]
    workers = SC_CORES * SC_SUBCORES
    chunk = SC_GATHER_CHUNK
    per = n // workers
    steps = per // chunk
    assert per * workers == n and steps * chunk == per and steps % 2 == 0

    def body(table_hbm, idx_hbm, out_hbm, idx_v, rows_v, gsem, wsem):
        base = (lax.axis_index("s") * SC_CORES + lax.axis_index("c")) * per
        pltpu.sync_copy(idx_hbm.at[pl.ds(base, per)], idx_v)

        def gather(c, slot):
            return pltpu.make_async_copy(table_hbm.at[idx_v.at[pl.ds(c * chunk, chunk)]], rows_v.at[slot],
                                         gsem.at[slot])

        def write(c, slot):
            return pltpu.make_async_copy(rows_v.at[slot], out_hbm.at[pl.ds(base + c * chunk, chunk)],
                                         wsem.at[slot])

        gather(0, 0).start()

        @pl.loop(0, steps, step=2)
        def _(c0):
            for slot in range(2):
                c = c0 + slot
                gather(c, slot).wait()
                write(c, slot).start()

                @pl.when(c >= 1)
                def _():
                    write(c - 1, 1 - slot).wait()

                @pl.when(c + 1 < steps)
                def _():
                    gather(c + 1, 1 - slot).start()

        write(steps - 1, 1).wait()

    return pl.kernel(
        body,
        out_type=jax.ShapeDtypeStruct((n, w), table.dtype),
        mesh=_sc_mesh(),
        scratch_types=[pltpu.VMEM((per,), i32), pltpu.VMEM((2, chunk, w), table.dtype),
                       pltpu.SemaphoreType.DMA((2,)), pltpu.SemaphoreType.DMA((2,))],
        name="sc_gather_rows",
    )(table, idx)


def _combine_kernel(x1_ref, gate_ref, gfin_ref, yg_ref, *rest):
    o_ref = rest[-1]
    acc = x1_ref[...]
    gates = gate_ref[...]
    for j in range(TOP_K):
        acc = acc + gates[:, j:j + 1] * _unpack_bf16_pairs(yg_ref[j])
    o_ref[...] = _rms_scale(acc) * gfin_ref[...]


def _combine(x1, gates, dest, y_slots, g_final, out_prev, tile0, total_tokens):
    t, d = x1.shape
    tm = TOKEN_TILE
    w = y_slots.shape[1]
    yg = _sc_gather_rows(y_slots, dest.reshape(-1)).reshape(TOP_K, t, w)
    in_specs = [pl.BlockSpec((tm, d), lambda i: (i, 0)),
                pl.BlockSpec((tm, TOP_K), lambda i: (i, 0)),
                pl.BlockSpec((1, d), lambda i: (0, 0)),
                pl.BlockSpec((TOP_K, tm, w), lambda i: (0, i, 0))]
    args = [x1, gates, g_final.reshape(1, d), yg]
    aliases = {}
    if out_prev is not None:
        in_specs.append(pl.BlockSpec(memory_space=pl.ANY))
        args.append(out_prev)
        aliases = {len(args) - 1: 0}
    return pl.pallas_call(
        _combine_kernel,
        grid=(t // tm,),
        in_specs=in_specs,
        out_specs=pl.BlockSpec((tm, d), lambda i: (i + tile0, 0)),
        out_shape=jax.ShapeDtypeStruct((total_tokens, d), f32),
        input_output_aliases=aliases,
        compiler_params=_params("parallel"),
        name="moe_combine",
    )(*args)


def _dest_kernel(start_ref, eidx_ref, rank_ref, dest_ref):
    eidx = eidx_ref[...]
    dest = rank_ref[...]
    for e in range(N_EXPERTS):
        dest = dest + jnp.where(eidx == e, start_ref[e], 0)
    dest_ref[...] = dest


def _slot_layout(counts, eidx, rank, n_blocks):
    bm = MOE_BLOCK
    padded = (counts + bm - 1) // bm * bm
    padded_end = jnp.cumsum(padded)
    start = padded_end - padded
    experts = jnp.arange(N_EXPERTS, dtype=i32)
    lookup = lambda table, idx: jnp.sum(jnp.where(idx[..., None] == experts, table, 0), axis=-1)
    dest = pl.pallas_call(
        _dest_kernel,
        grid_spec=pltpu.PrefetchScalarGridSpec(
            num_scalar_prefetch=1, grid=(1,),
            in_specs=[pl.BlockSpec(eidx.shape, lambda i, st: (0, 0)), pl.BlockSpec(rank.shape, lambda i, st: (0, 0))],
            out_specs=pl.BlockSpec(rank.shape, lambda i, st: (0, 0))),
        out_shape=jax.ShapeDtypeStruct(rank.shape, i32),
        name="moe_dest",
    )(start.astype(i32), eidx, rank)
    blk_row = jnp.arange(n_blocks, dtype=i32) * bm
    blk_e = jnp.minimum(jnp.sum((padded_end[None, :] <= blk_row[:, None]).astype(i32), axis=1), N_EXPERTS - 1)
    blk_cnt = jnp.clip(lookup(counts, blk_e) - (blk_row - lookup(start, blk_e)), 0, bm).astype(i32)
    none = jnp.int32(N_EXPERTS)
    nxt_e = jnp.min(jnp.where(blk_e[None, :] > blk_e[:, None], blk_e[None, :], none), axis=1)
    nxt_e = jnp.where(nxt_e == none, -1, nxt_e).astype(i32)
    return dest, blk_e, blk_cnt, nxt_e


def _layer_and_final_norm(x2d, mem, seq, g_mix, g_mem, w_in, w_mem_kv, na_rel_bias, g_grp, w_out, g_ffn,
                          router_w, router_b, w_gu, b_gu, w_down, b_down, g_final):
    t, d = x2d.shape
    b = t // seq
    q_na, k_na, v_na, u_ft, q_mem = _in_proj(x2d, g_mix, w_in.astype(bf16))
    k_mem, v_mem = _mem_kv(mem, g_mem, w_mem_kv.astype(bf16))
    shape3 = lambda a: a.reshape(b, seq, a.shape[-1])
    y_na = _neighbourhood_attention(shape3(q_na), shape3(k_na), shape3(v_na), _na_bias_table(na_rel_bias))
    y_ft = _fourier_mix(shape3(u_ft), _ft_tables(seq))
    y_na, y_ft = y_na.reshape(t, -1), y_ft.reshape(t, -1)
    w_out_bf16 = w_out.astype(bf16)
    rw_hi = router_w.astype(bf16)
    router_w2 = jnp.stack([rw_hi, (router_w - rw_hi.astype(f32)).astype(bf16)])

    tiles = t // TOKEN_TILE
    unit = tiles // sum(MOE_GROUP_SHARES)
    assert unit * sum(MOE_GROUP_SHARES) == tiles
    out = None
    tile0 = 0
    dest = router_b
    for share in MOE_GROUP_SHARES:
        group_tiles = share * unit
        n_blocks = (group_tiles * TOKEN_TILE * TOP_K) // MOE_BLOCK + N_EXPERTS
        x1, h2p, eidx, gates, rank, counts = _mix_out(
            x2d, y_na, y_ft, q_mem, k_mem, v_mem, g_grp, w_out_bf16, g_ffn, router_w2, router_b, seq,
            tile0, group_tiles, dest)
        dest, blk_e, blk_cnt, nxt_e = _slot_layout(counts[:, 0], eidx, rank, n_blocks)
        xs = _dispatch(h2p, dest, n_blocks * MOE_BLOCK)
        y_slots = _experts(xs, blk_e, blk_cnt, nxt_e, w_gu, b_gu, w_down, b_down)
        out = _combine(x1, gates, dest, y_slots, g_final, out, tile0, t)
        tile0 += group_tiles
    return out


def kernel(x, mem, g_mix, g_mem, w_in, w_mem_kv, na_rel_bias, g_grp, w_out, g_ffn, router_w, router_b,
           w_gu, b_gu, w_down, b_down, g_final):
    b, seq, d = x.shape
    depth = w_in.shape[0]
    assert depth == 1, "the final norm is fused into the single layer's combine step"
    out = _layer_and_final_norm(
        x.reshape(b * seq, d), mem, seq, g_mix[0], g_mem[0], w_in[0], w_mem_kv[0], na_rel_bias[0], g_grp[0],
        w_out[0], g_ffn[0], router_w[0], router_b[0], w_gu[0], b_gu[0], w_down[0], b_down[0], g_final)
    return out.reshape(b, seq, d)
```

```python
import numpy as np
import jax
import jax.numpy as jnp
from jax import lax
from jax.experimental import pallas as pl
from jax.experimental.pallas import tpu as pltpu
from jax.experimental.pallas import tpu_sc as plsc

f32 = jnp.float32
bf16 = jnp.bfloat16
u32 = jnp.uint32
i32 = jnp.int32

GRID_W = 64
NA_HEADS = 8
NA_HEAD_DIM = 64
NA_WIN_ROWS = 8
NA_WIN_COLS = 16
FT_GROUPS = 4
FT_GROUP_DIM = 128
MEM_HEADS = 4
MEM_HEAD_DIM = 128
NA_WIDTH = NA_HEADS * NA_HEAD_DIM
FT_WIDTH = FT_GROUPS * FT_GROUP_DIM
MEM_WIDTH = MEM_HEADS * MEM_HEAD_DIM
N_EXPERTS = 32
TOP_K = 4
SWIGLU_LIMIT = 7.0
SWIGLU_ALPHA = 1.702
EPS = 1e-6

LANES = 128
SUBLANES = 8
VMEM_LIMIT_BYTES = 56 * 1024 * 1024
SC_CORES = 2
SC_SUBCORES = 16
SC_GATHER_CHUNK = 64

TOKEN_TILE = 512
IN_PROJ_TILE = 1024
MOE_BLOCK = 512
MOE_SUB_BLOCK = 128
WEIGHT_CAST_ROWS = 128
MOE_GROUP_SHARES = (5, 3)
NA_ROW_UNROLL = 16
FT_N1 = 64
FT_N2 = 128
FT_K1_BLOCK = 8
LOG2_E = 1.4426950408889634
MASK_VALUE = -jnp.inf


def _params(*semantics):
    return pltpu.CompilerParams(dimension_semantics=semantics, vmem_limit_bytes=VMEM_LIMIT_BYTES)


def _rms_scale(x):
    return x * lax.rsqrt(jnp.mean(x * x, axis=-1, keepdims=True) + EPS)


def _softmax_rows(s):
    p = jnp.exp(s - jnp.max(s, axis=-1, keepdims=True))
    return p / jnp.sum(p, axis=-1, keepdims=True)


def _bf16_bits(x):
    return pltpu.bitcast(x.astype(bf16).astype(f32), u32)


def _pack2(lo, hi):
    return (_bf16_bits(lo) >> 16) | (_bf16_bits(hi) & jnp.uint32(0xFFFF0000))


def _unpack2(w):
    return pltpu.bitcast(w << 16, f32), pltpu.bitcast(w & jnp.uint32(0xFFFF0000), f32)


def _pack_bf16_pairs(x):
    n = x.shape[1] // 2
    return _pack2(x[:, :n], x[:, n:])


def _unpack_bf16_pairs(w):
    return jnp.concatenate(_unpack2(w), axis=1)


def _in_proj_kernel(x_ref, g_ref, w_ref, qna_ref, kna_ref, vna_ref, uft_ref, qmem_ref):
    h = _rms_scale(x_ref[...]) * g_ref[...]
    proj = jnp.dot(h.astype(bf16), w_ref[...], preferred_element_type=f32)
    o = NA_WIDTH
    qna_ref[...] = (proj[:, :o] * (NA_HEAD_DIM ** -0.5 * LOG2_E)).astype(bf16)
    kna_ref[...] = proj[:, o:2 * o].astype(bf16)
    vna_ref[...] = proj[:, 2 * o:3 * o].astype(bf16)
    uft_ref[...] = _pack_bf16_pairs(proj[:, 3 * o:3 * o + FT_WIDTH])
    qmem_ref[...] = proj[:, 3 * o + FT_WIDTH:].astype(bf16)


def _in_proj(x2d, g_mix, w_in_bf16):
    t, d = x2d.shape
    tm = IN_PROJ_TILE
    row = lambda w: pl.BlockSpec((tm, w), lambda i: (i, 0))
    return pl.pallas_call(
        _in_proj_kernel,
        grid=(t // tm,),
        in_specs=[row(d), pl.BlockSpec((1, d), lambda i: (0, 0)),
                  pl.BlockSpec(w_in_bf16.shape, lambda i: (0, 0))],
        out_specs=[row(NA_WIDTH), row(NA_WIDTH), row(NA_WIDTH), row(FT_WIDTH // 2), row(MEM_WIDTH)],
        out_shape=[jax.ShapeDtypeStruct((t, NA_WIDTH), bf16)] * 3
        + [jax.ShapeDtypeStruct((t, FT_WIDTH // 2), u32), jax.ShapeDtypeStruct((t, MEM_WIDTH), bf16)],
        compiler_params=_params("parallel"),
        name="in_proj",
    )(x2d, g_mix.reshape(1, d), w_in_bf16)


def _na_bias_table(rel_bias):
    c = np.arange(GRID_W)
    dc_idx = np.clip(c[None, :] - c[:, None], -(NA_WIN_COLS - 1), NA_WIN_COLS - 1) + (NA_WIN_COLS - 1)
    col_start = np.clip(c - NA_WIN_COLS // 2, 0, GRID_W - NA_WIN_COLS)
    col_in = (c[None, :] >= col_start[:, None]) & (c[None, :] < col_start[:, None] + NA_WIN_COLS)
    pick_c = jnp.asarray(dc_idx[:, :, None] == np.arange(2 * NA_WIN_COLS - 1), f32)
    cols = jnp.einsum("hab,qcb->haqc", rel_bias.astype(f32), pick_c, precision=lax.Precision.HIGHEST)
    cols = jnp.where(col_in[None, None], cols * LOG2_E, MASK_VALUE)
    tab = jnp.stack([jnp.concatenate([cols[:, j - s + NA_WIN_ROWS - 1] for j in range(NA_WIN_ROWS)], axis=-1)
                     for s in range(NA_WIN_ROWS)])
    return tab.reshape(NA_WIN_ROWS, NA_HEADS // 2, 2 * GRID_W, NA_WIN_ROWS * GRID_W)


def _na_kernel(q_ref, k_ref, v_ref, bias_ref, o_ref):
    rows = q_ref.shape[1] // GRID_W
    win = NA_WIN_ROWS * GRID_W
    first_head = lax.broadcasted_iota(i32, (GRID_W, 2 * NA_HEAD_DIM), 1) < NA_HEAD_DIM

    def body(it, carry):
        scores, values, q_offsets = [], [], []
        for u in range(NA_ROW_UNROLL):
            r = it * NA_ROW_UNROLL + u
            row_start = jnp.clip(r - NA_WIN_ROWS // 2, 0, rows - NA_WIN_ROWS)
            q0 = pl.multiple_of(r * GRID_W, GRID_W)
            k0 = pl.multiple_of(row_start * GRID_W, GRID_W)
            q = q_ref[0, pl.ds(q0, GRID_W), :]
            zero = jnp.zeros_like(q)
            qm = jnp.concatenate([jnp.where(first_head, q, zero), jnp.where(first_head, zero, q)], axis=0)
            s = lax.dot_general(qm, k_ref[0, pl.ds(k0, win), :], (((1,), (1,)), ((), ())),
                                preferred_element_type=f32)
            scores.append(s + bias_ref[r - row_start, 0])
            values.append(v_ref[0, pl.ds(k0, win), :])
            q_offsets.append(q0)
        s = jnp.concatenate(scores, axis=0)
        p = jnp.exp2(s - jnp.max(s, axis=-1, keepdims=True))
        inv_den = 1.0 / jnp.sum(p, axis=-1, keepdims=True)
        p = p.astype(bf16)
        for u in range(NA_ROW_UNROLL):
            sl = slice(u * 2 * GRID_W, (u + 1) * 2 * GRID_W)
            o = jnp.dot(p[sl], values[u], preferred_element_type=f32) * inv_den[sl]
            o_ref[0, pl.ds(q_offsets[u], GRID_W), :] = jnp.where(
                first_head, o[:GRID_W], o[GRID_W:]).astype(o_ref.dtype)
        return carry

    lax.fori_loop(0, rows // NA_ROW_UNROLL, body, 0)


def _neighbourhood_attention(q, k, v, bias_tab):
    b, s, _ = q.shape
    pair = 2 * NA_HEAD_DIM
    qkv_spec = pl.BlockSpec((1, s, pair), lambda bi, hp: (bi, 0, hp))
    return pl.pallas_call(
        _na_kernel,
        grid=(b, NA_HEADS // 2),
        in_specs=[qkv_spec, qkv_spec, qkv_spec,
                  pl.BlockSpec((NA_WIN_ROWS, 1, 2 * GRID_W, NA_WIN_ROWS * GRID_W), lambda bi, hp: (0, hp, 0, 0))],
        out_specs=qkv_spec,
        out_shape=jax.ShapeDtypeStruct((b, s, NA_WIDTH), bf16),
        compiler_params=_params("parallel", "parallel"),
        name="neighbourhood_attention",
    )(q, k, v, bias_tab)


def _ft_tables(seq):
    assert seq == FT_N1 * FT_N2
    n_blk = FT_N2 // SUBLANES
    k1 = np.arange(FT_N1)[:, None, None, None]
    sr = np.arange(SUBLANES)[None, :, None, None]
    n1 = np.arange(FT_N1)[None, None, :, None]
    sc = np.arange(SUBLANES)[None, None, None, :]
    stage1 = np.zeros((n_blk, 2, FT_N1, SUBLANES, FT_N1, SUBLANES), np.float64)
    for blk in range(n_blk):
        n = FT_N2 * n1 + SUBLANES * blk + sr
        ang = 2.0 * np.pi * ((k1 * n) % seq) / seq
        eye = (sr == sc)
        stage1[blk, 0] = np.cos(ang) * eye
        stage1[blk, 1] = -np.sin(ang) * eye
    stage1 = stage1.reshape(n_blk, 2 * FT_N1 * SUBLANES, FT_N1 * SUBLANES)
    a = np.arange(FT_N2)
    ang2 = 2.0 * np.pi * ((a[:, None] * a[None, :]) % FT_N2) / FT_N2
    c2, s2 = np.cos(ang2), np.sin(ang2)
    stage2 = np.block([[c2, s2], [-s2, c2]])
    g = np.arange(FT_GROUP_DIM)
    angc = 2.0 * np.pi * ((g[:, None] * g[None, :]) % FT_GROUP_DIM) / FT_GROUP_DIM
    norm = 1.0 / np.sqrt(seq * FT_GROUP_DIM)
    chan = np.concatenate([np.cos(angc), np.sin(angc)], axis=0) * norm
    return (jnp.asarray(stage1, bf16), jnp.asarray(stage2, bf16), jnp.asarray(chan, bf16))


def _ft_stage1_kernel(u_ref, m_ref, z_ref):
    rows = FT_N1 * SUBLANES
    u = _unpack_bf16_pairs(u_ref[0].reshape(rows, FT_WIDTH // 2)).astype(bf16)
    z = jnp.dot(m_ref[0], u, preferred_element_type=f32)
    z_ref[0] = _pack2(z[:rows], z[rows:]).reshape(FT_N1, SUBLANES, FT_WIDTH)


def _ft_stage2_kernel(z_ref, s2_ref, cs_ref, y_ref):
    gd = FT_GROUP_DIM
    for kk in range(FT_K1_BLOCK):
        zz = jnp.concatenate(_unpack2(z_ref[0, kk]), axis=0).astype(bf16)
        x = jnp.dot(s2_ref[...], zz, preferred_element_type=f32)
        outs = []
        for g in range(FT_GROUPS):
            xg = jnp.concatenate([x[:FT_N2, g * gd:(g + 1) * gd], x[FT_N2:, g * gd:(g + 1) * gd]], axis=1)
            outs.append(jnp.dot(xg.astype(bf16), cs_ref[...], preferred_element_type=f32))
        y_ref[0, kk] = jnp.concatenate(outs, axis=1).astype(y_ref.dtype)


def _fourier_mix(u_packed, tables):
    u = u_packed
    b, s, _ = u.shape
    c = FT_WIDTH
    stage1, stage2, chan = tables
    n_blk = FT_N2 // SUBLANES
    z = pl.pallas_call(
        _ft_stage1_kernel,
        grid=(n_blk, b),
        in_specs=[pl.BlockSpec((1, FT_N1, SUBLANES, c // 2), lambda j, bi: (bi, 0, j, 0)),
                  pl.BlockSpec((1,) + stage1.shape[1:], lambda j, bi: (j, 0, 0))],
        out_specs=pl.BlockSpec((1, FT_N1, SUBLANES, c), lambda j, bi: (bi, 0, j, 0)),
        out_shape=jax.ShapeDtypeStruct((b, FT_N1, FT_N2, c), u32),
        compiler_params=_params("parallel", "parallel"),
        name="fourier_stage1",
    )(u.reshape(b, FT_N1, FT_N2, c // 2), stage1)
    y = pl.pallas_call(
        _ft_stage2_kernel,
        grid=(b, FT_N1 // FT_K1_BLOCK),
        in_specs=[pl.BlockSpec((1, FT_K1_BLOCK, FT_N2, c), lambda bi, kb: (bi, kb, 0, 0)),
                  pl.BlockSpec(stage2.shape, lambda bi, kb: (0, 0)),
                  pl.BlockSpec(chan.shape, lambda bi, kb: (0, 0))],
        out_specs=pl.BlockSpec((1, FT_K1_BLOCK, FT_N2, c), lambda bi, kb: (bi, kb, 0, 0)),
        out_shape=jax.ShapeDtypeStruct((b, FT_N1, FT_N2, c), bf16),
        compiler_params=_params("parallel", "parallel"),
        name="fourier_stage2",
    )(z, stage2, chan)
    return y.transpose(0, 2, 1, 3).reshape(b, s, c)


def _mem_kv_kernel(mem_ref, g_ref, w_ref, k_ref, v_ref):
    mn = _rms_scale(mem_ref[0]) * g_ref[...]
    kv = jnp.dot(mn.astype(bf16), w_ref[...], preferred_element_type=f32)
    k_ref[0] = kv[:, :MEM_WIDTH].astype(bf16)
    v_ref[0] = kv[:, MEM_WIDTH:].astype(bf16)


def _mem_kv(mem, g_mem, w_kv_bf16):
    b, m, d = mem.shape
    kv_spec = pl.BlockSpec((1, m, MEM_WIDTH), lambda bi: (bi, 0, 0))
    return pl.pallas_call(
        _mem_kv_kernel,
        grid=(b,),
        in_specs=[pl.BlockSpec((1, m, d), lambda bi: (bi, 0, 0)), pl.BlockSpec((1, d), lambda bi: (0, 0)),
                  pl.BlockSpec(w_kv_bf16.shape, lambda bi: (0, 0))],
        out_specs=[kv_spec, kv_spec],
        out_shape=[jax.ShapeDtypeStruct((b, m, MEM_WIDTH), bf16)] * 2,
        compiler_params=_params("parallel"),
        name="mem_kv",
    )(mem, g_mem.reshape(1, d), w_kv_bf16)


def _mix_out_kernel(x_ref, yna_ref, yft_ref, qm_ref, km_ref, vm_ref, ggrp_ref, wout_ref, gffn_ref, rw_ref,
                    rb_ref, _order_ref, x1_ref, h2p_ref, eidx_ref, gate_ref, rank_ref, cnt_ref, carry_ref):
    tm = x_ref.shape[0]

    @pl.when(pl.program_id(0) == 0)
    def _():
        carry_ref[...] = jnp.zeros_like(carry_ref)

    q = qm_ref[...]
    km = km_ref[0]
    vm = vm_ref[0]
    heads = []
    for h in range(MEM_HEADS):
        sl = slice(h * MEM_HEAD_DIM, (h + 1) * MEM_HEAD_DIM)
        s = lax.dot_general(q[:, sl], km[:, sl], (((1,), (1,)), ((), ())), preferred_element_type=f32)
        p = _softmax_rows(s * (MEM_HEAD_DIM ** -0.5))
        heads.append(jnp.dot(p.astype(bf16), vm[:, sl], preferred_element_type=f32))
    ymem = jnp.concatenate(heads, axis=1)

    g = ggrp_ref[...]
    a, c = NA_WIDTH, NA_WIDTH + FT_WIDTH
    y = jnp.concatenate([_rms_scale(yna_ref[...].astype(f32)) * g[:, :a],
                         _rms_scale(yft_ref[...].astype(f32)) * g[:, a:c],
                         _rms_scale(ymem) * g[:, c:]], axis=1)
    x1 = x_ref[...] + jnp.dot(y.astype(bf16), wout_ref[...], preferred_element_type=f32)
    x1_ref[...] = x1
    h2 = _rms_scale(x1) * gffn_ref[...]
    h2p_ref[...] = _pack_bf16_pairs(h2)

    h_hi = h2.astype(bf16)
    h_lo = (h2 - h_hi.astype(f32)).astype(bf16)
    hh = jnp.dot(h_hi, rw_ref[...], preferred_element_type=f32)
    logits = (hh[:, :LANES] + hh[:, LANES:]
              + jnp.dot(h_lo, rw_ref[:, :LANES], preferred_element_type=f32)) + rb_ref[...]
    l = logits.T[:N_EXPERTS]
    row = lax.broadcasted_iota(i32, (N_EXPERTS, tm), 0).astype(f32)
    vals, idxs, sels = [], [], []
    for _ in range(TOP_K):
        m = jnp.max(l, axis=0, keepdims=True)
        idx = jnp.min(jnp.where(l == m, row, float(N_EXPERTS)), axis=0, keepdims=True)
        sel = row == idx
        vals.append(m)
        idxs.append(idx)
        sels.append(sel)
        l = jnp.where(sel, -jnp.inf, l)
    ex = [jnp.exp(v - vals[0]) for v in vals]
    den = ex[0] + ex[1] + ex[2] + ex[3]

    onehot = (sels[0] | sels[1] | sels[2] | sels[3]).astype(f32)
    earlier = (lax.broadcasted_iota(i32, (tm, tm), 0) < lax.broadcasted_iota(i32, (tm, tm), 1)).astype(bf16)
    before = jnp.dot(onehot.astype(bf16), earlier, preferred_element_type=f32) + carry_ref[...]
    ranks = [jnp.sum(jnp.where(sel, before, 0.0), axis=0, keepdims=True) for sel in sels]
    carry_ref[...] = carry_ref[...] + jnp.sum(onehot, axis=1, keepdims=True)
    cnt_ref[...] = carry_ref[...].astype(i32)

    eidx_ref[...] = jnp.concatenate(idxs, axis=0).astype(i32)
    rank_ref[...] = jnp.concatenate(ranks, axis=0).astype(i32)
    gates_t = jnp.concatenate([e / den for e in ex] + [jnp.zeros((LANES - TOP_K, tm), f32)], axis=0)
    gate_ref[...] = gates_t.T[:, :TOP_K]


def _mix_out(x2d, y_na, y_ft, q_mem, k_mem, v_mem, g_grp, w_out_bf16, g_ffn, router_w2, router_b, seq,
             tile0, n_tiles, order_after):
    d = x2d.shape[1]
    tm = TOKEN_TILE
    t = n_tiles * tm
    steps_per_batch = seq // tm
    m = k_mem.shape[1]
    row_in = lambda w: pl.BlockSpec((tm, w), lambda i: (i + tile0, 0))
    row_out = lambda w: pl.BlockSpec((tm, w), lambda i: (i, 0))
    full = lambda a: pl.BlockSpec(a.shape, lambda i: (0,) * a.ndim)
    kv_spec = pl.BlockSpec((1, m, MEM_WIDTH), lambda i: ((i + tile0) // steps_per_batch, 0, 0))
    g_grp2, g_ffn2 = g_grp.reshape(1, -1), g_ffn.reshape(1, d)
    rb2 = jnp.pad(router_b.reshape(1, N_EXPERTS), ((0, 0), (0, LANES - N_EXPERTS)))
    router_w2 = jnp.pad(router_w2, ((0, 0), (0, 0), (0, LANES - N_EXPERTS)))
    router_w2 = jnp.concatenate([router_w2[0], router_w2[1]], axis=1)
    col_out = pl.BlockSpec((TOP_K, tm), lambda i: (0, i))
    return pl.pallas_call(
        _mix_out_kernel,
        grid=(n_tiles,),
        in_specs=[row_in(d), row_in(NA_WIDTH), row_in(FT_WIDTH), row_in(MEM_WIDTH), kv_spec, kv_spec,
                  full(g_grp2), full(w_out_bf16), full(g_ffn2), full(router_w2), full(rb2),
                  pl.BlockSpec(memory_space=pl.ANY)],
        out_specs=[row_out(d), row_out(d // 2), col_out, row_out(TOP_K), col_out,
                   pl.BlockSpec((N_EXPERTS, 1), lambda i: (0, 0))],
        out_shape=[jax.ShapeDtypeStruct((t, d), f32), jax.ShapeDtypeStruct((t, d // 2), u32),
                   jax.ShapeDtypeStruct((TOP_K, t), i32), jax.ShapeDtypeStruct((t, TOP_K), f32),
                   jax.ShapeDtypeStruct((TOP_K, t), i32), jax.ShapeDtypeStruct((N_EXPERTS, 1), i32)],
        scratch_shapes=[pltpu.VMEM((N_EXPERTS, 1), f32)],
        compiler_params=_params("arbitrary"),
        name="mix_out_router",
    )(x2d, y_na, y_ft, q_mem, k_mem, v_mem, g_grp2, w_out_bf16, g_ffn2, router_w2, rb2, order_after)


def _sc_mesh():
    return plsc.VectorSubcoreMesh(core_axis_name="c", subcore_axis_name="s",
                                  num_cores=SC_CORES, num_subcores=SC_SUBCORES)


def _dispatch(h2p, dest, n_slots):
    t, w = h2p.shape
    workers = SC_CORES * SC_SUBCORES
    chunk = SC_GATHER_CHUNK
    per = t // workers
    steps = per // chunk
    assert per * workers == t and steps * chunk == per and steps % 2 == 0
    idx = dest.reshape(TOP_K, workers, steps, chunk)

    def body(h_hbm, idx_hbm, out_hbm, idx_v, rows_v, rsem, ssem):
        wid = lax.axis_index("s") * SC_CORES + lax.axis_index("c")
        base = wid * per
        for j in range(TOP_K):
            pltpu.sync_copy(idx_hbm.at[j, wid], idx_v.at[j])

        def read(c, slot):
            return pltpu.make_async_copy(h_hbm.at[pl.ds(base + c * chunk, chunk)], rows_v.at[slot], rsem.at[slot])

        def scatters(c, slot):
            return [pltpu.make_async_copy(rows_v.at[slot], out_hbm.at[idx_v.at[j, c]], ssem.at[slot])
                    for j in range(TOP_K)]

        read(0, 0).start()

        @pl.loop(0, steps, step=2)
        def _(c0):
            for slot in range(2):
                c = c0 + slot
                read(c, slot).wait()
                for cp in scatters(c, slot):
                    cp.start()

                @pl.when(c >= 1)
                def _():
                    for cp in scatters(c - 1, 1 - slot):
                        cp.wait()

                @pl.when(c + 1 < steps)
                def _():
                    read(c + 1, 1 - slot).start()

        for cp in scatters(steps - 1, 1):
            cp.wait()

    return pl.kernel(
        body,
        out_type=jax.ShapeDtypeStruct((n_slots, w), h2p.dtype),
        mesh=_sc_mesh(),
        scratch_types=[pltpu.VMEM((TOP_K, steps, chunk), i32), pltpu.VMEM((2, chunk, w), h2p.dtype),
                       pltpu.SemaphoreType.DMA((2,)), pltpu.SemaphoreType.DMA((2,))],
        name="sc_dispatch_rows",
    )(h2p, idx)


def _expert_kernel(blk_e_ref, blk_cnt_ref, nxt_e_ref, xs_ref, wgu_hbm, bgu_ref, wd_hbm, bd_ref, y_ref,
                   wgu_f32, wd_f32, wgu_bf, wd_bf, sem):
    b = pl.program_id(0)
    e = blk_e_ref[b]
    cnt = blk_cnt_ref[b]
    bm = xs_ref.shape[0]
    de = wd_f32.shape[0]

    def fetch(expert):
        return (pltpu.make_async_copy(wgu_hbm.at[expert], wgu_f32, sem.at[0]),
                pltpu.make_async_copy(wd_hbm.at[expert], wd_f32, sem.at[1]))

    @pl.when(b == 0)
    def _():
        for cp in fetch(e):
            cp.start()

    @pl.when(jnp.logical_or(b == 0, e != blk_e_ref[jnp.maximum(b - 1, 0)]))
    def _():
        for cp in fetch(e):
            cp.wait()

        def convert(i, carry):
            rows = pl.ds(pl.multiple_of(i * WEIGHT_CAST_ROWS, WEIGHT_CAST_ROWS), WEIGHT_CAST_ROWS)
            wgu_bf[rows, :] = wgu_f32[rows, :].astype(bf16)
            wd_bf[rows, :] = wd_f32[rows, :].astype(bf16)
            return carry

        lax.fori_loop(0, wgu_f32.shape[0] // WEIGHT_CAST_ROWS, convert, 0)

        @pl.when(nxt_e_ref[b] >= 0)
        def _():
            for cp in fetch(nxt_e_ref[b]):
                cp.start()

    def ffn_rows(r0, n):
        rows = pl.ds(r0, n)
        valid = r0 + lax.broadcasted_iota(i32, (n, 1), 0) < cnt
        x = jnp.where(valid, _unpack_bf16_pairs(xs_ref[rows, :]), 0.0).astype(bf16)
        gu = jnp.dot(x, wgu_bf[...], preferred_element_type=f32) + bgu_ref[0]
        x_glu = jnp.minimum(gu[:, :de], SWIGLU_LIMIT)
        x_lin = jnp.clip(gu[:, de:], -SWIGLU_LIMIT, SWIGLU_LIMIT)
        act = x_glu * (1.0 / (1.0 + jnp.exp(-SWIGLU_ALPHA * x_glu))) * (x_lin + 1.0)
        y = jnp.dot(act.astype(bf16), wd_bf[...], preferred_element_type=f32) + bd_ref[0]
        y_ref[rows, :] = _pack_bf16_pairs(y)

    @pl.when(cnt == bm)
    def _():
        ffn_rows(0, bm)

    @pl.when(cnt < bm)
    def _():
        y_ref[...] = jnp.zeros_like(y_ref)

        def piece(i, carry):
            ffn_rows(pl.multiple_of(i * MOE_SUB_BLOCK, MOE_SUB_BLOCK), MOE_SUB_BLOCK)
            return carry

        lax.fori_loop(0, (cnt + MOE_SUB_BLOCK - 1) // MOE_SUB_BLOCK, piece, 0)


def _experts(xs, blk_e, blk_cnt, nxt_e, w_gu, b_gu, w_down, b_down):
    n_slots, w = xs.shape
    bm = MOE_BLOCK
    e, d, de2 = w_gu.shape
    de = w_down.shape[1]
    assert de == d, "one row loop converts both weight matrices"
    last = n_slots // bm - 1

    def rows_of_block(b, be, bc, ne):
        return (jnp.where(bc[b] > 0, b, last), 0)

    grid_spec = pltpu.PrefetchScalarGridSpec(
        num_scalar_prefetch=3,
        grid=(n_slots // bm,),
        in_specs=[pl.BlockSpec((bm, w), rows_of_block),
                  pl.BlockSpec(memory_space=pl.ANY),
                  pl.BlockSpec((1, 1, de2), lambda b, be, bc, ne: (be[b], 0, 0)),
                  pl.BlockSpec(memory_space=pl.ANY),
                  pl.BlockSpec((1, 1, d), lambda b, be, bc, ne: (be[b], 0, 0))],
        out_specs=pl.BlockSpec((bm, w), rows_of_block),
        scratch_shapes=[pltpu.VMEM((d, de2), f32), pltpu.VMEM((de, d), f32),
                        pltpu.VMEM((d, de2), bf16), pltpu.VMEM((de, d), bf16),
                        pltpu.SemaphoreType.DMA((2,))],
    )
    return pl.pallas_call(
        _expert_kernel,
        grid_spec=grid_spec,
        out_shape=jax.ShapeDtypeStruct((n_slots, w), u32),
        compiler_params=_params("arbitrary"),
        name="moe_experts",
    )(blk_e, blk_cnt, nxt_e, xs, w_gu, b_gu.reshape(e, 1, de2), w_down, b_down.reshape(e, 1, d))


def _sc_gather_rows(table, idx):
    n, w = idx.shape[0], table.shape[1]
    workers = SC_CORES * SC_SUBCORES
    chunk = SC_GATHER_CHUNK
    per = n // workers
    steps = per // chunk
    assert per * workers == n and steps * chunk == per and steps % 2 == 0

    def body(table_hbm, idx_hbm, out_hbm, idx_v, rows_v, gsem, wsem):
        base = (lax.axis_index("s") * SC_CORES + lax.axis_index("c")) * per
        pltpu.sync_copy(idx_hbm.at[pl.ds(base, per)], idx_v)

        def gather(c, slot):
            return pltpu.make_async_copy(table_hbm.at[idx_v.at[pl.ds(c * chunk, chunk)]], rows_v.at[slot],
                                         gsem.at[slot])

        def write(c, slot):
            return pltpu.make_async_copy(rows_v.at[slot], out_hbm.at[pl.ds(base + c * chunk, chunk)],
                                         wsem.at[slot])

        gather(0, 0).start()

        @pl.loop(0, steps, step=2)
        def _(c0):
            for slot in range(2):
                c = c0 + slot
                gather(c, slot).wait()
                write(c, slot).start()

                @pl.when(c >= 1)
                def _():
                    write(c - 1, 1 - slot).wait()

                @pl.when(c + 1 < steps)
                def _():
                    gather(c + 1, 1 - slot).start()

        write(steps - 1, 1).wait()

    return pl.kernel(
        body,
        out_type=jax.ShapeDtypeStruct((n, w), table.dtype),
        mesh=_sc_mesh(),
        scratch_types=[pltpu.VMEM((per,), i32), pltpu.VMEM((2, chunk, w), table.dtype),
                       pltpu.SemaphoreType.DMA((2,)), pltpu.SemaphoreType.DMA((2,))],
        name="sc_gather_rows",
    )(table, idx)


def _combine_kernel(x1_ref, gate_ref, gfin_ref, yg_ref, *rest):
    o_ref = rest[-1]
    acc = x1_ref[...]
    gates = gate_ref[...]
    for j in range(TOP_K):
        acc = acc + gates[:, j:j + 1] * _unpack_bf16_pairs(yg_ref[j])
    o_ref[...] = _rms_scale(acc) * gfin_ref[...]


def _combine(x1, gates, dest, y_slots, g_final, out_prev, tile0, total_tokens):
    t, d = x1.shape
    tm = TOKEN_TILE
    w = y_slots.shape[1]
    yg = _sc_gather_rows(y_slots, dest.reshape(-1)).reshape(TOP_K, t, w)
    in_specs = [pl.BlockSpec((tm, d), lambda i: (i, 0)),
                pl.BlockSpec((tm, TOP_K), lambda i: (i, 0)),
                pl.BlockSpec((1, d), lambda i: (0, 0)),
                pl.BlockSpec((TOP_K, tm, w), lambda i: (0, i, 0))]
    args = [x1, gates, g_final.reshape(1, d), yg]
    aliases = {}
    if out_prev is not None:
        in_specs.append(pl.BlockSpec(memory_space=pl.ANY))
        args.append(out_prev)
        aliases = {len(args) - 1: 0}
    return pl.pallas_call(
        _combine_kernel,
        grid=(t // tm,),
        in_specs=in_specs,
        out_specs=pl.BlockSpec((tm, d), lambda i: (i + tile0, 0)),
        out_shape=jax.ShapeDtypeStruct((total_tokens, d), f32),
        input_output_aliases=aliases,
        compiler_params=_params("parallel"),
        name="moe_combine",
    )(*args)


def _dest_kernel(start_ref, eidx_ref, rank_ref, dest_ref):
    eidx = eidx_ref[...]
    dest = rank_ref[...]
    for e in range(N_EXPERTS):
        dest = dest + jnp.where(eidx == e, start_ref[e], 0)
    dest_ref[...] = dest


def _slot_layout(counts, eidx, rank, n_blocks):
    bm = MOE_BLOCK
    padded = (counts + bm - 1) // bm * bm
    padded_end = jnp.cumsum(padded)
    start = padded_end - padded
    experts = jnp.arange(N_EXPERTS, dtype=i32)
    lookup = lambda table, idx: jnp.sum(jnp.where(idx[..., None] == experts, table, 0), axis=-1)
    dest = pl.pallas_call(
        _dest_kernel,
        grid_spec=pltpu.PrefetchScalarGridSpec(
            num_scalar_prefetch=1, grid=(1,),
            in_specs=[pl.BlockSpec(eidx.shape, lambda i, st: (0, 0)), pl.BlockSpec(rank.shape, lambda i, st: (0, 0))],
            out_specs=pl.BlockSpec(rank.shape, lambda i, st: (0, 0))),
        out_shape=jax.ShapeDtypeStruct(rank.shape, i32),
        name="moe_dest",
    )(start.astype(i32), eidx, rank)
    blk_row = jnp.arange(n_blocks, dtype=i32) * bm
    blk_e = jnp.minimum(jnp.sum((padded_end[None, :] <= blk_row[:, None]).astype(i32), axis=1), N_EXPERTS - 1)
    blk_cnt = jnp.clip(lookup(counts, blk_e) - (blk_row - lookup(start, blk_e)), 0, bm).astype(i32)
    none = jnp.int32(N_EXPERTS)
    nxt_e = jnp.min(jnp.where(blk_e[None, :] > blk_e[:, None], blk_e[None, :], none), axis=1)
    nxt_e = jnp.where(nxt_e == none, -1, nxt_e).astype(i32)
    return dest, blk_e, blk_cnt, nxt_e


def _layer_and_final_norm(x2d, mem, seq, g_mix, g_mem, w_in, w_mem_kv, na_rel_bias, g_grp, w_out, g_ffn,
                          router_w, router_b, w_gu, b_gu, w_down, b_down, g_final):
    t, d = x2d.shape
    b = t // seq
    q_na, k_na, v_na, u_ft, q_mem = _in_proj(x2d, g_mix, w_in.astype(bf16))
    k_mem, v_mem = _mem_kv(mem, g_mem, w_mem_kv.astype(bf16))
    shape3 = lambda a: a.reshape(b, seq, a.shape[-1])
    y_na = _neighbourhood_attention(shape3(q_na), shape3(k_na), shape3(v_na), _na_bias_table(na_rel_bias))
    y_ft = _fourier_mix(shape3(u_ft), _ft_tables(seq))
    y_na, y_ft = y_na.reshape(t, -1), y_ft.reshape(t, -1)
    w_out_bf16 = w_out.astype(bf16)
    rw_hi = router_w.astype(bf16)
    router_w2 = jnp.stack([rw_hi, (router_w - rw_hi.astype(f32)).astype(bf16)])

    tiles = t // TOKEN_TILE
    unit = tiles // sum(MOE_GROUP_SHARES)
    assert unit * sum(MOE_GROUP_SHARES) == tiles
    out = None
    tile0 = 0
    dest = router_b
    for share in MOE_GROUP_SHARES:
        group_tiles = share * unit
        n_blocks = (group_tiles * TOKEN_TILE * TOP_K) // MOE_BLOCK + N_EXPERTS
        x1, h2p, eidx, gates, rank, counts = _mix_out(
            x2d, y_na, y_ft, q_mem, k_mem, v_mem, g_grp, w_out_bf16, g_ffn, router_w2, router_b, seq,
            tile0, group_tiles, dest)
        dest, blk_e, blk_cnt, nxt_e = _slot_layout(counts[:, 0], eidx, rank, n_blocks)
        xs = _dispatch(h2p, dest, n_blocks * MOE_BLOCK)
        y_slots = _experts(xs, blk_e, blk_cnt, nxt_e, w_gu, b_gu, w_down, b_down)
        out = _combine(x1, gates, dest, y_slots, g_final, out, tile0, t)
        tile0 += group_tiles
    return out


def kernel(x, mem, g_mix, g_mem, w_in, w_mem_kv, na_rel_bias, g_grp, w_out, g_ffn, router_w, router_b,
           w_gu, b_gu, w_down, b_down, g_final):
    b, seq, d = x.shape
    depth = w_in.shape[0]
    assert depth == 1, "the final norm is fused into the single layer's combine step"
    out = _layer_and_final_norm(
        x.reshape(b * seq, d), mem, seq, g_mix[0], g_mem[0], w_in[0], w_mem_kv[0], na_rel_bias[0], g_grp[0],
        w_out[0], g_ffn[0], router_w[0], router_b[0], w_gu[0], b_gu[0], w_down[0], b_down[0], g_final)
    return out.reshape(b, seq, d)
```

```python
import numpy as np
import jax
import jax.numpy as jnp
from jax import lax
from jax.experimental import pallas as pl
from jax.experimental.pallas import tpu as pltpu
from jax.experimental.pallas import tpu_sc as plsc

f32 = jnp.float32
bf16 = jnp.bfloat16
u32 = jnp.uint32
i32 = jnp.int32

GRID_W = 64
NA_HEADS = 8
NA_HEAD_DIM = 64
NA_WIN_ROWS = 8
NA_WIN_COLS = 16
FT_GROUPS = 4
FT_GROUP_DIM = 128
MEM_HEADS = 4
MEM_HEAD_DIM = 128
NA_WIDTH = NA_HEADS * NA_HEAD_DIM
FT_WIDTH = FT_GROUPS * FT_GROUP_DIM
MEM_WIDTH = MEM_HEADS * MEM_HEAD_DIM
N_EXPERTS = 32
TOP_K = 4
SWIGLU_LIMIT = 7.0
SWIGLU_ALPHA = 1.702
EPS = 1e-6

LANES = 128
SUBLANES = 8
VMEM_LIMIT_BYTES = 56 * 1024 * 1024
SC_CORES = 2
SC_SUBCORES = 16
SC_GATHER_CHUNK = 64

TOKEN_TILE = 512
IN_PROJ_TILE = 1024
MOE_BLOCK = 512
MOE_SUB_BLOCK = 128
WEIGHT_CAST_ROWS = 128
MOE_GROUP_SHARES = (5, 3)
MOE_COMBINE_PARTS = (1, 3)
NA_ROW_UNROLL = 16
FT_N1 = 64
FT_N2 = 128
FT_K1_BLOCK = 8
LOG2_E = 1.4426950408889634
MASK_VALUE = -jnp.inf


def _params(*semantics):
    return pltpu.CompilerParams(dimension_semantics=semantics, vmem_limit_bytes=VMEM_LIMIT_BYTES)


def _rms_scale(x):
    return x * lax.rsqrt(jnp.mean(x * x, axis=-1, keepdims=True) + EPS)


def _softmax_rows(s):
    p = jnp.exp(s - jnp.max(s, axis=-1, keepdims=True))
    return p / jnp.sum(p, axis=-1, keepdims=True)


def _bf16_bits(x):
    return pltpu.bitcast(x.astype(bf16).astype(f32), u32)


def _pack2(lo, hi):
    return (_bf16_bits(lo) >> 16) | (_bf16_bits(hi) & jnp.uint32(0xFFFF0000))


def _unpack2(w):
    return pltpu.bitcast(w << 16, f32), pltpu.bitcast(w & jnp.uint32(0xFFFF0000), f32)


def _pack_bf16_pairs(x):
    n = x.shape[1] // 2
    return _pack2(x[:, :n], x[:, n:])


def _unpack_bf16_pairs(w):
    return jnp.concatenate(_unpack2(w), axis=1)


def _in_proj_kernel(x_ref, g_ref, w_ref, qna_ref, kna_ref, vna_ref, uft_ref, qmem_ref):
    h = _rms_scale(x_ref[...]) * g_ref[...]
    proj = jnp.dot(h.astype(bf16), w_ref[...], preferred_element_type=f32)
    o = NA_WIDTH
    qna_ref[...] = (proj[:, :o] * (NA_HEAD_DIM ** -0.5 * LOG2_E)).astype(bf16)
    kna_ref[...] = proj[:, o:2 * o].astype(bf16)
    vna_ref[...] = proj[:, 2 * o:3 * o].astype(bf16)
    uft_ref[...] = _pack_bf16_pairs(proj[:, 3 * o:3 * o + FT_WIDTH])
    qmem_ref[...] = proj[:, 3 * o + FT_WIDTH:].astype(bf16)


def _in_proj(x2d, g_mix, w_in_bf16):
    t, d = x2d.shape
    tm = IN_PROJ_TILE
    row = lambda w: pl.BlockSpec((tm, w), lambda i: (i, 0))
    return pl.pallas_call(
        _in_proj_kernel,
        grid=(t // tm,),
        in_specs=[row(d), pl.BlockSpec((1, d), lambda i: (0, 0)),
                  pl.BlockSpec(w_in_bf16.shape, lambda i: (0, 0))],
        out_specs=[row(NA_WIDTH), row(NA_WIDTH), row(NA_WIDTH), row(FT_WIDTH // 2), row(MEM_WIDTH)],
        out_shape=[jax.ShapeDtypeStruct((t, NA_WIDTH), bf16)] * 3
        + [jax.ShapeDtypeStruct((t, FT_WIDTH // 2), u32), jax.ShapeDtypeStruct((t, MEM_WIDTH), bf16)],
        compiler_params=_params("parallel"),
        name="in_proj",
    )(x2d, g_mix.reshape(1, d), w_in_bf16)


def _na_bias_table(rel_bias):
    c = np.arange(GRID_W)
    dc_idx = np.clip(c[None, :] - c[:, None], -(NA_WIN_COLS - 1), NA_WIN_COLS - 1) + (NA_WIN_COLS - 1)
    col_start = np.clip(c - NA_WIN_COLS // 2, 0, GRID_W - NA_WIN_COLS)
    col_in = (c[None, :] >= col_start[:, None]) & (c[None, :] < col_start[:, None] + NA_WIN_COLS)
    pick_c = jnp.asarray(dc_idx[:, :, None] == np.arange(2 * NA_WIN_COLS - 1), f32)
    cols = jnp.einsum("hab,qcb->haqc", rel_bias.astype(f32), pick_c, precision=lax.Precision.HIGHEST)
    cols = jnp.where(col_in[None, None], cols * LOG2_E, MASK_VALUE)
    tab = jnp.stack([jnp.concatenate([cols[:, j - s + NA_WIN_ROWS - 1] for j in range(NA_WIN_ROWS)], axis=-1)
                     for s in range(NA_WIN_ROWS)])
    return tab.reshape(NA_WIN_ROWS, NA_HEADS // 2, 2 * GRID_W, NA_WIN_ROWS * GRID_W)


def _na_kernel(q_ref, k_ref, v_ref, bias_ref, o_ref):
    rows = q_ref.shape[1] // GRID_W
    win = NA_WIN_ROWS * GRID_W
    first_head = lax.broadcasted_iota(i32, (GRID_W, 2 * NA_HEAD_DIM), 1) < NA_HEAD_DIM

    def body(it, carry):
        scores, values, q_offsets = [], [], []
        for u in range(NA_ROW_UNROLL):
            r = it * NA_ROW_UNROLL + u
            row_start = jnp.clip(r - NA_WIN_ROWS // 2, 0, rows - NA_WIN_ROWS)
            q0 = pl.multiple_of(r * GRID_W, GRID_W)
            k0 = pl.multiple_of(row_start * GRID_W, GRID_W)
            q = q_ref[0, pl.ds(q0, GRID_W), :]
            zero = jnp.zeros_like(q)
            qm = jnp.concatenate([jnp.where(first_head, q, zero), jnp.where(first_head, zero, q)], axis=0)
            s = lax.dot_general(qm, k_ref[0, pl.ds(k0, win), :], (((1,), (1,)), ((), ())),
                                preferred_element_type=f32)
            scores.append(s + bias_ref[r - row_start, 0])
            values.append(v_ref[0, pl.ds(k0, win), :])
            q_offsets.append(q0)
        s = jnp.concatenate(scores, axis=0)
        p = jnp.exp2(s - jnp.max(s, axis=-1, keepdims=True))
        inv_den = 1.0 / jnp.sum(p, axis=-1, keepdims=True)
        p = p.astype(bf16)
        for u in range(NA_ROW_UNROLL):
            sl = slice(u * 2 * GRID_W, (u + 1) * 2 * GRID_W)
            o = jnp.dot(p[sl], values[u], preferred_element_type=f32) * inv_den[sl]
            o_ref[0, pl.ds(q_offsets[u], GRID_W), :] = jnp.where(
                first_head, o[:GRID_W], o[GRID_W:]).astype(o_ref.dtype)
        return carry

    lax.fori_loop(0, rows // NA_ROW_UNROLL, body, 0)


def _neighbourhood_attention(q, k, v, bias_tab):
    b, s, _ = q.shape
    pair = 2 * NA_HEAD_DIM
    qkv_spec = pl.BlockSpec((1, s, pair), lambda bi, hp: (bi, 0, hp))
    return pl.pallas_call(
        _na_kernel,
        grid=(b, NA_HEADS // 2),
        in_specs=[qkv_spec, qkv_spec, qkv_spec,
                  pl.BlockSpec((NA_WIN_ROWS, 1, 2 * GRID_W, NA_WIN_ROWS * GRID_W), lambda bi, hp: (0, hp, 0, 0))],
        out_specs=qkv_spec,
        out_shape=jax.ShapeDtypeStruct((b, s, NA_WIDTH), bf16),
        compiler_params=_params("parallel", "parallel"),
        name="neighbourhood_attention",
    )(q, k, v, bias_tab)


def _ft_tables(seq):
    assert seq == FT_N1 * FT_N2
    n_blk = FT_N2 // SUBLANES
    k1 = np.arange(FT_N1)[:, None, None, None]
    sr = np.arange(SUBLANES)[None, :, None, None]
    n1 = np.arange(FT_N1)[None, None, :, None]
    sc = np.arange(SUBLANES)[None, None, None, :]
    stage1 = np.zeros((n_blk, 2, FT_N1, SUBLANES, FT_N1, SUBLANES), np.float64)
    for blk in range(n_blk):
        n = FT_N2 * n1 + SUBLANES * blk + sr
        ang = 2.0 * np.pi * ((k1 * n) % seq) / seq
        eye = (sr == sc)
        stage1[blk, 0] = np.cos(ang) * eye
        stage1[blk, 1] = -np.sin(ang) * eye
    stage1 = stage1.reshape(n_blk, 2 * FT_N1 * SUBLANES, FT_N1 * SUBLANES)
    a = np.arange(FT_N2)
    ang2 = 2.0 * np.pi * ((a[:, None] * a[None, :]) % FT_N2) / FT_N2
    c2, s2 = np.cos(ang2), np.sin(ang2)
    stage2 = np.block([[c2, s2], [-s2, c2]])
    g = np.arange(FT_GROUP_DIM)
    angc = 2.0 * np.pi * ((g[:, None] * g[None, :]) % FT_GROUP_DIM) / FT_GROUP_DIM
    norm = 1.0 / np.sqrt(seq * FT_GROUP_DIM)
    chan = np.concatenate([np.cos(angc), np.sin(angc)], axis=0) * norm
    return (jnp.asarray(stage1, bf16), jnp.asarray(stage2, bf16), jnp.asarray(chan, bf16))


def _ft_stage1_kernel(u_ref, m_ref, z_ref):
    rows = FT_N1 * SUBLANES
    u = _unpack_bf16_pairs(u_ref[0].reshape(rows, FT_WIDTH // 2)).astype(bf16)
    z = jnp.dot(m_ref[0], u, preferred_element_type=f32)
    z_ref[0] = _pack2(z[:rows], z[rows:]).reshape(FT_N1, SUBLANES, FT_WIDTH)


def _ft_stage2_kernel(z_ref, s2_ref, cs_ref, y_ref):
    gd = FT_GROUP_DIM
    for kk in range(FT_K1_BLOCK):
        zz = jnp.concatenate(_unpack2(z_ref[0, kk]), axis=0).astype(bf16)
        x = jnp.dot(s2_ref[...], zz, preferred_element_type=f32)
        outs = []
        for g in range(FT_GROUPS):
            xg = jnp.concatenate([x[:FT_N2, g * gd:(g + 1) * gd], x[FT_N2:, g * gd:(g + 1) * gd]], axis=1)
            outs.append(jnp.dot(xg.astype(bf16), cs_ref[...], preferred_element_type=f32))
        y_ref[0, kk] = jnp.concatenate(outs, axis=1).astype(y_ref.dtype)


def _fourier_mix(u_packed, tables):
    u = u_packed
    b, s, _ = u.shape
    c = FT_WIDTH
    stage1, stage2, chan = tables
    n_blk = FT_N2 // SUBLANES
    z = pl.pallas_call(
        _ft_stage1_kernel,
        grid=(n_blk, b),
        in_specs=[pl.BlockSpec((1, FT_N1, SUBLANES, c // 2), lambda j, bi: (bi, 0, j, 0)),
                  pl.BlockSpec((1,) + stage1.shape[1:], lambda j, bi: (j, 0, 0))],
        out_specs=pl.BlockSpec((1, FT_N1, SUBLANES, c), lambda j, bi: (bi, 0, j, 0)),
        out_shape=jax.ShapeDtypeStruct((b, FT_N1, FT_N2, c), u32),
        compiler_params=_params("parallel", "parallel"),
        name="fourier_stage1",
    )(u.reshape(b, FT_N1, FT_N2, c // 2), stage1)
    y = pl.pallas_call(
        _ft_stage2_kernel,
        grid=(b, FT_N1 // FT_K1_BLOCK),
        in_specs=[pl.BlockSpec((1, FT_K1_BLOCK, FT_N2, c), lambda bi, kb: (bi, kb, 0, 0)),
                  pl.BlockSpec(stage2.shape, lambda bi, kb: (0, 0)),
                  pl.BlockSpec(chan.shape, lambda bi, kb: (0, 0))],
        out_specs=pl.BlockSpec((1, FT_K1_BLOCK, FT_N2, c), lambda bi, kb: (bi, kb, 0, 0)),
        out_shape=jax.ShapeDtypeStruct((b, FT_N1, FT_N2, c), bf16),
        compiler_params=_params("parallel", "parallel"),
        name="fourier_stage2",
    )(z, stage2, chan)
    return y.transpose(0, 2, 1, 3).reshape(b, s, c)


def _mem_kv_kernel(mem_ref, g_ref, w_ref, k_ref, v_ref):
    mn = _rms_scale(mem_ref[0]) * g_ref[...]
    kv = jnp.dot(mn.astype(bf16), w_ref[...], preferred_element_type=f32)
    k_ref[0] = kv[:, :MEM_WIDTH].astype(bf16)
    v_ref[0] = kv[:, MEM_WIDTH:].astype(bf16)


def _mem_kv(mem, g_mem, w_kv_bf16):
    b, m, d = mem.shape
    kv_spec = pl.BlockSpec((1, m, MEM_WIDTH), lambda bi: (bi, 0, 0))
    return pl.pallas_call(
        _mem_kv_kernel,
        grid=(b,),
        in_specs=[pl.BlockSpec((1, m, d), lambda bi: (bi, 0, 0)), pl.BlockSpec((1, d), lambda bi: (0, 0)),
                  pl.BlockSpec(w_kv_bf16.shape, lambda bi: (0, 0))],
        out_specs=[kv_spec, kv_spec],
        out_shape=[jax.ShapeDtypeStruct((b, m, MEM_WIDTH), bf16)] * 2,
        compiler_params=_params("parallel"),
        name="mem_kv",
    )(mem, g_mem.reshape(1, d), w_kv_bf16)


def _mix_out_kernel(x_ref, yna_ref, yft_ref, qm_ref, km_ref, vm_ref, ggrp_ref, wout_ref, gffn_ref, rw_ref,
                    rb_ref, _order_ref, x1_ref, h2p_ref, eidx_ref, gate_ref, rank_ref, cnt_ref, carry_ref):
    tm = x_ref.shape[0]

    @pl.when(pl.program_id(0) == 0)
    def _():
        carry_ref[...] = jnp.zeros_like(carry_ref)

    q = qm_ref[...]
    km = km_ref[0]
    vm = vm_ref[0]
    heads = []
    for h in range(MEM_HEADS):
        sl = slice(h * MEM_HEAD_DIM, (h + 1) * MEM_HEAD_DIM)
        s = lax.dot_general(q[:, sl], km[:, sl], (((1,), (1,)), ((), ())), preferred_element_type=f32)
        p = _softmax_rows(s * (MEM_HEAD_DIM ** -0.5))
        heads.append(jnp.dot(p.astype(bf16), vm[:, sl], preferred_element_type=f32))
    ymem = jnp.concatenate(heads, axis=1)

    g = ggrp_ref[...]
    a, c = NA_WIDTH, NA_WIDTH + FT_WIDTH
    y = jnp.concatenate([_rms_scale(yna_ref[...].astype(f32)) * g[:, :a],
                         _rms_scale(yft_ref[...].astype(f32)) * g[:, a:c],
                         _rms_scale(ymem) * g[:, c:]], axis=1)
    x1 = x_ref[...] + jnp.dot(y.astype(bf16), wout_ref[...], preferred_element_type=f32)
    x1_ref[...] = x1
    h2 = _rms_scale(x1) * gffn_ref[...]
    h2p_ref[...] = _pack_bf16_pairs(h2)

    h_hi = h2.astype(bf16)
    h_lo = (h2 - h_hi.astype(f32)).astype(bf16)
    hh = jnp.dot(h_hi, rw_ref[...], preferred_element_type=f32)
    logits = (hh[:, :LANES] + hh[:, LANES:]
              + jnp.dot(h_lo, rw_ref[:, :LANES], preferred_element_type=f32)) + rb_ref[...]
    l = logits.T[:N_EXPERTS]
    row = lax.broadcasted_iota(i32, (N_EXPERTS, tm), 0).astype(f32)
    vals, idxs, sels = [], [], []
    for _ in range(TOP_K):
        m = jnp.max(l, axis=0, keepdims=True)
        idx = jnp.min(jnp.where(l == m, row, float(N_EXPERTS)), axis=0, keepdims=True)
        sel = row == idx
        vals.append(m)
        idxs.append(idx)
        sels.append(sel)
        l = jnp.where(sel, -jnp.inf, l)
    ex = [jnp.exp(v - vals[0]) for v in vals]
    den = ex[0] + ex[1] + ex[2] + ex[3]

    onehot = (sels[0] | sels[1] | sels[2] | sels[3]).astype(f32)
    earlier = (lax.broadcasted_iota(i32, (tm, tm), 0) < lax.broadcasted_iota(i32, (tm, tm), 1)).astype(bf16)
    before = jnp.dot(onehot.astype(bf16), earlier, preferred_element_type=f32) + carry_ref[...]
    ranks = [jnp.sum(jnp.where(sel, before, 0.0), axis=0, keepdims=True) for sel in sels]
    carry_ref[...] = carry_ref[...] + jnp.sum(onehot, axis=1, keepdims=True)
    cnt_ref[...] = carry_ref[...].astype(i32)

    eidx_ref[...] = jnp.concatenate(idxs, axis=0).astype(i32)
    rank_ref[...] = jnp.concatenate(ranks, axis=0).astype(i32)
    gates_t = jnp.concatenate([e / den for e in ex] + [jnp.zeros((LANES - TOP_K, tm), f32)], axis=0)
    gate_ref[...] = gates_t.T[:, :TOP_K]


def _mix_out(x2d, y_na, y_ft, q_mem, k_mem, v_mem, g_grp, w_out_bf16, g_ffn, router_w2, router_b, seq,
             tile0, n_tiles, order_after):
    d = x2d.shape[1]
    tm = TOKEN_TILE
    t = n_tiles * tm
    steps_per_batch = seq // tm
    m = k_mem.shape[1]
    row_in = lambda w: pl.BlockSpec((tm, w), lambda i: (i + tile0, 0))
    row_out = lambda w: pl.BlockSpec((tm, w), lambda i: (i, 0))
    full = lambda a: pl.BlockSpec(a.shape, lambda i: (0,) * a.ndim)
    kv_spec = pl.BlockSpec((1, m, MEM_WIDTH), lambda i: ((i + tile0) // steps_per_batch, 0, 0))
    g_grp2, g_ffn2 = g_grp.reshape(1, -1), g_ffn.reshape(1, d)
    rb2 = jnp.pad(router_b.reshape(1, N_EXPERTS), ((0, 0), (0, LANES - N_EXPERTS)))
    router_w2 = jnp.pad(router_w2, ((0, 0), (0, 0), (0, LANES - N_EXPERTS)))
    router_w2 = jnp.concatenate([router_w2[0], router_w2[1]], axis=1)
    col_out = pl.BlockSpec((TOP_K, tm), lambda i: (0, i))
    return pl.pallas_call(
        _mix_out_kernel,
        grid=(n_tiles,),
        in_specs=[row_in(d), row_in(NA_WIDTH), row_in(FT_WIDTH), row_in(MEM_WIDTH), kv_spec, kv_spec,
                  full(g_grp2), full(w_out_bf16), full(g_ffn2), full(router_w2), full(rb2),
                  pl.BlockSpec(memory_space=pl.ANY)],
        out_specs=[row_out(d), row_out(d // 2), col_out, row_out(TOP_K), col_out,
                   pl.BlockSpec((N_EXPERTS, 1), lambda i: (0, 0))],
        out_shape=[jax.ShapeDtypeStruct((t, d), f32), jax.ShapeDtypeStruct((t, d // 2), u32),
                   jax.ShapeDtypeStruct((TOP_K, t), i32), jax.ShapeDtypeStruct((t, TOP_K), f32),
                   jax.ShapeDtypeStruct((TOP_K, t), i32), jax.ShapeDtypeStruct((N_EXPERTS, 1), i32)],
        scratch_shapes=[pltpu.VMEM((N_EXPERTS, 1), f32)],
        compiler_params=_params("arbitrary"),
        name="mix_out_router",
    )(x2d, y_na, y_ft, q_mem, k_mem, v_mem, g_grp2, w_out_bf16, g_ffn2, router_w2, rb2, order_after)


def _sc_mesh():
    return plsc.VectorSubcoreMesh(core_axis_name="c", subcore_axis_name="s",
                                  num_cores=SC_CORES, num_subcores=SC_SUBCORES)


def _dispatch(h2p, dest, n_slots):
    t, w = h2p.shape
    workers = SC_CORES * SC_SUBCORES
    chunk = SC_GATHER_CHUNK
    per = t // workers
    steps = per // chunk
    assert per * workers == t and steps * chunk == per and steps % 2 == 0
    idx = dest.reshape(TOP_K, workers, steps, chunk)

    def body(h_hbm, idx_hbm, out_hbm, idx_v, rows_v, rsem, ssem):
        wid = lax.axis_index("s") * SC_CORES + lax.axis_index("c")
        base = wid * per
        for j in range(TOP_K):
            pltpu.sync_copy(idx_hbm.at[j, wid], idx_v.at[j])

        def read(c, slot):
            return pltpu.make_async_copy(h_hbm.at[pl.ds(base + c * chunk, chunk)], rows_v.at[slot], rsem.at[slot])

        def scatters(c, slot):
            return [pltpu.make_async_copy(rows_v.at[slot], out_hbm.at[idx_v.at[j, c]], ssem.at[slot])
                    for j in range(TOP_K)]

        read(0, 0).start()

        @pl.loop(0, steps, step=2)
        def _(c0):
            for slot in range(2):
                c = c0 + slot
                read(c, slot).wait()
                for cp in scatters(c, slot):
                    cp.start()

                @pl.when(c >= 1)
                def _():
                    for cp in scatters(c - 1, 1 - slot):
                        cp.wait()

                @pl.when(c + 1 < steps)
                def _():
                    read(c + 1, 1 - slot).start()

        for cp in scatters(steps - 1, 1):
            cp.wait()

    return pl.kernel(
        body,
        out_type=jax.ShapeDtypeStruct((n_slots, w), h2p.dtype),
        mesh=_sc_mesh(),
        scratch_types=[pltpu.VMEM((TOP_K, steps, chunk), i32), pltpu.VMEM((2, chunk, w), h2p.dtype),
                       pltpu.SemaphoreType.DMA((2,)), pltpu.SemaphoreType.DMA((2,))],
        name="sc_dispatch_rows",
    )(h2p, idx)


def _expert_kernel(blk_e_ref, blk_cnt_ref, nxt_e_ref, xs_ref, wgu_hbm, bgu_ref, wd_hbm, bd_ref, y_ref,
                   wgu_f32, wd_f32, wgu_bf, wd_bf, sem):
    b = pl.program_id(0)
    e = blk_e_ref[b]
    cnt = blk_cnt_ref[b]
    bm = xs_ref.shape[0]
    de = wd_f32.shape[0]

    def fetch(expert):
        return (pltpu.make_async_copy(wgu_hbm.at[expert], wgu_f32, sem.at[0]),
                pltpu.make_async_copy(wd_hbm.at[expert], wd_f32, sem.at[1]))

    @pl.when(b == 0)
    def _():
        for cp in fetch(e):
            cp.start()

    @pl.when(jnp.logical_or(b == 0, e != blk_e_ref[jnp.maximum(b - 1, 0)]))
    def _():
        for cp in fetch(e):
            cp.wait()

        def convert(i, carry):
            rows = pl.ds(pl.multiple_of(i * WEIGHT_CAST_ROWS, WEIGHT_CAST_ROWS), WEIGHT_CAST_ROWS)
            wgu_bf[rows, :] = wgu_f32[rows, :].astype(bf16)
            wd_bf[rows, :] = wd_f32[rows, :].astype(bf16)
            return carry

        lax.fori_loop(0, wgu_f32.shape[0] // WEIGHT_CAST_ROWS, convert, 0)

        @pl.when(nxt_e_ref[b] >= 0)
        def _():
            for cp in fetch(nxt_e_ref[b]):
                cp.start()

    def ffn_rows(r0, n):
        rows = pl.ds(r0, n)
        valid = r0 + lax.broadcasted_iota(i32, (n, 1), 0) < cnt
        x = jnp.where(valid, _unpack_bf16_pairs(xs_ref[rows, :]), 0.0).astype(bf16)
        gu = jnp.dot(x, wgu_bf[...], preferred_element_type=f32) + bgu_ref[0]
        x_glu = jnp.minimum(gu[:, :de], SWIGLU_LIMIT)
        x_lin = jnp.clip(gu[:, de:], -SWIGLU_LIMIT, SWIGLU_LIMIT)
        act = x_glu * (1.0 / (1.0 + jnp.exp(-SWIGLU_ALPHA * x_glu))) * (x_lin + 1.0)
        y = jnp.dot(act.astype(bf16), wd_bf[...], preferred_element_type=f32) + bd_ref[0]
        y_ref[rows, :] = _pack_bf16_pairs(y)

    @pl.when(cnt == bm)
    def _():
        ffn_rows(0, bm)

    @pl.when(cnt < bm)
    def _():
        y_ref[...] = jnp.zeros_like(y_ref)

        def piece(i, carry):
            ffn_rows(pl.multiple_of(i * MOE_SUB_BLOCK, MOE_SUB_BLOCK), MOE_SUB_BLOCK)
            return carry

        lax.fori_loop(0, (cnt + MOE_SUB_BLOCK - 1) // MOE_SUB_BLOCK, piece, 0)


def _experts(xs, blk_e, blk_cnt, nxt_e, w_gu, b_gu, w_down, b_down):
    n_slots, w = xs.shape
    bm = MOE_BLOCK
    e, d, de2 = w_gu.shape
    de = w_down.shape[1]
    assert de == d, "one row loop converts both weight matrices"
    last = n_slots // bm - 1

    def rows_of_block(b, be, bc, ne):
        return (jnp.where(bc[b] > 0, b, last), 0)

    grid_spec = pltpu.PrefetchScalarGridSpec(
        num_scalar_prefetch=3,
        grid=(n_slots // bm,),
        in_specs=[pl.BlockSpec((bm, w), rows_of_block),
                  pl.BlockSpec(memory_space=pl.ANY),
                  pl.BlockSpec((1, 1, de2), lambda b, be, bc, ne: (be[b], 0, 0)),
                  pl.BlockSpec(memory_space=pl.ANY),
                  pl.BlockSpec((1, 1, d), lambda b, be, bc, ne: (be[b], 0, 0))],
        out_specs=pl.BlockSpec((bm, w), rows_of_block),
        scratch_shapes=[pltpu.VMEM((d, de2), f32), pltpu.VMEM((de, d), f32),
                        pltpu.VMEM((d, de2), bf16), pltpu.VMEM((de, d), bf16),
                        pltpu.SemaphoreType.DMA((2,))],
    )
    return pl.pallas_call(
        _expert_kernel,
        grid_spec=grid_spec,
        out_shape=jax.ShapeDtypeStruct((n_slots, w), u32),
        compiler_params=_params("arbitrary"),
        name="moe_experts",
    )(blk_e, blk_cnt, nxt_e, xs, w_gu, b_gu.reshape(e, 1, de2), w_down, b_down.reshape(e, 1, d))


def _sc_gather_rows(table, idx):
    n, w = idx.shape[0], table.shape[1]
    workers = SC_CORES * SC_SUBCORES
    chunk = SC_GATHER_CHUNK
    per = n // workers
    steps = per // chunk
    assert per * workers == n and steps * chunk == per and steps % 2 == 0

    def body(table_hbm, idx_hbm, out_hbm, idx_v, rows_v, gsem, wsem):
        base = (lax.axis_index("s") * SC_CORES + lax.axis_index("c")) * per
        pltpu.sync_copy(idx_hbm.at[pl.ds(base, per)], idx_v)

        def gather(c, slot):
            return pltpu.make_async_copy(table_hbm.at[idx_v.at[pl.ds(c * chunk, chunk)]], rows_v.at[slot],
                                         gsem.at[slot])

        def write(c, slot):
            return pltpu.make_async_copy(rows_v.at[slot], out_hbm.at[pl.ds(base + c * chunk, chunk)],
                                         wsem.at[slot])

        gather(0, 0).start()

        @pl.loop(0, steps, step=2)
        def _(c0):
            for slot in range(2):
                c = c0 + slot
                gather(c, slot).wait()
                write(c, slot).start()

                @pl.when(c >= 1)
                def _():
                    write(c - 1, 1 - slot).wait()

                @pl.when(c + 1 < steps)
                def _():
                    gather(c + 1, 1 - slot).start()

        write(steps - 1, 1).wait()

    return pl.kernel(
        body,
        out_type=jax.ShapeDtypeStruct((n, w), table.dtype),
        mesh=_sc_mesh(),
        scratch_types=[pltpu.VMEM((per,), i32), pltpu.VMEM((2, chunk, w), table.dtype),
                       pltpu.SemaphoreType.DMA((2,)), pltpu.SemaphoreType.DMA((2,))],
        name="sc_gather_rows",
    )(table, idx)


def _combine_kernel(x1_ref, gate_ref, gfin_ref, yg_ref, *rest):
    o_ref = rest[-1]
    acc = x1_ref[...]
    gates = gate_ref[...]
    for j in range(TOP_K):
        acc = acc + gates[:, j:j + 1] * _unpack_bf16_pairs(yg_ref[j])
    o_ref[...] = _rms_scale(acc) * gfin_ref[...]


def _combine(x1, gates, dest, y_slots, g_final, out_prev, tile0, total_tokens, parts):
    t, d = x1.shape
    tm = TOKEN_TILE
    w = y_slots.shape[1]
    part_tiles = t // tm // parts
    assert part_tiles * parts * tm == t
    out = out_prev
    for part in range(parts):
        p0 = part * part_tiles
        idx = dest[:, p0 * tm:(p0 + part_tiles) * tm]
        yg = _sc_gather_rows(y_slots, idx.reshape(-1)).reshape(TOP_K, part_tiles * tm, w)
        in_specs = [pl.BlockSpec((tm, d), lambda i, p0=p0: (i + p0, 0)),
                    pl.BlockSpec((tm, TOP_K), lambda i, p0=p0: (i + p0, 0)),
                    pl.BlockSpec((1, d), lambda i: (0, 0)),
                    pl.BlockSpec((TOP_K, tm, w), lambda i: (0, i, 0))]
        args = [x1, gates, g_final.reshape(1, d), yg]
        aliases = {}
        if out is not None:
            in_specs.append(pl.BlockSpec(memory_space=pl.ANY))
            args.append(out)
            aliases = {len(args) - 1: 0}
        out = pl.pallas_call(
            _combine_kernel,
            grid=(part_tiles,),
            in_specs=in_specs,
            out_specs=pl.BlockSpec((tm, d), lambda i, p0=p0: (i + tile0 + p0, 0)),
            out_shape=jax.ShapeDtypeStruct((total_tokens, d), f32),
            input_output_aliases=aliases,
            compiler_params=_params("parallel"),
            name="moe_combine",
        )(*args)
    return out


def _dest_kernel(start_ref, eidx_ref, rank_ref, dest_ref):
    eidx = eidx_ref[...]
    dest = rank_ref[...]
    for e in range(N_EXPERTS):
        dest = dest + jnp.where(eidx == e, start_ref[e], 0)
    dest_ref[...] = dest


def _slot_layout(counts, eidx, rank, n_blocks):
    bm = MOE_BLOCK
    padded = (counts + bm - 1) // bm * bm
    padded_end = jnp.cumsum(padded)
    start = padded_end - padded
    experts = jnp.arange(N_EXPERTS, dtype=i32)
    lookup = lambda table, idx: jnp.sum(jnp.where(idx[..., None] == experts, table, 0), axis=-1)
    dest = pl.pallas_call(
        _dest_kernel,
        grid_spec=pltpu.PrefetchScalarGridSpec(
            num_scalar_prefetch=1, grid=(1,),
            in_specs=[pl.BlockSpec(eidx.shape, lambda i, st: (0, 0)), pl.BlockSpec(rank.shape, lambda i, st: (0, 0))],
            out_specs=pl.BlockSpec(rank.shape, lambda i, st: (0, 0))),
        out_shape=jax.ShapeDtypeStruct(rank.shape, i32),
        name="moe_dest",
    )(start.astype(i32), eidx, rank)
    blk_row = jnp.arange(n_blocks, dtype=i32) * bm
    blk_e = jnp.minimum(jnp.sum((padded_end[None, :] <= blk_row[:, None]).astype(i32), axis=1), N_EXPERTS - 1)
    blk_cnt = jnp.clip(lookup(counts, blk_e) - (blk_row - lookup(start, blk_e)), 0, bm).astype(i32)
    none = jnp.int32(N_EXPERTS)
    nxt_e = jnp.min(jnp.where(blk_e[None, :] > blk_e[:, None], blk_e[None, :], none), axis=1)
    nxt_e = jnp.where(nxt_e == none, -1, nxt_e).astype(i32)
    return dest, blk_e, blk_cnt, nxt_e


def _layer_and_final_norm(x2d, mem, seq, g_mix, g_mem, w_in, w_mem_kv, na_rel_bias, g_grp, w_out, g_ffn,
                          router_w, router_b, w_gu, b_gu, w_down, b_down, g_final):
    t, d = x2d.shape
    b = t // seq
    q_na, k_na, v_na, u_ft, q_mem = _in_proj(x2d, g_mix, w_in.astype(bf16))
    k_mem, v_mem = _mem_kv(mem, g_mem, w_mem_kv.astype(bf16))
    shape3 = lambda a: a.reshape(b, seq, a.shape[-1])
    y_na = _neighbourhood_attention(shape3(q_na), shape3(k_na), shape3(v_na), _na_bias_table(na_rel_bias))
    y_ft = _fourier_mix(shape3(u_ft), _ft_tables(seq))
    y_na, y_ft = y_na.reshape(t, -1), y_ft.reshape(t, -1)
    w_out_bf16 = w_out.astype(bf16)
    rw_hi = router_w.astype(bf16)
    router_w2 = jnp.stack([rw_hi, (router_w - rw_hi.astype(f32)).astype(bf16)])

    tiles = t // TOKEN_TILE
    unit = tiles // sum(MOE_GROUP_SHARES)
    assert unit * sum(MOE_GROUP_SHARES) == tiles
    out = None
    tile0 = 0
    dest = router_b
    for share, parts in zip(MOE_GROUP_SHARES, MOE_COMBINE_PARTS):
        group_tiles = share * unit
        n_blocks = (group_tiles * TOKEN_TILE * TOP_K) // MOE_BLOCK + N_EXPERTS
        x1, h2p, eidx, gates, rank, counts = _mix_out(
            x2d, y_na, y_ft, q_mem, k_mem, v_mem, g_grp, w_out_bf16, g_ffn, router_w2, router_b, seq,
            tile0, group_tiles, dest)
        dest, blk_e, blk_cnt, nxt_e = _slot_layout(counts[:, 0], eidx, rank, n_blocks)
        xs = _dispatch(h2p, dest, n_blocks * MOE_BLOCK)
        y_slots = _experts(xs, blk_e, blk_cnt, nxt_e, w_gu, b_gu, w_down, b_down)
        out = _combine(x1, gates, dest, y_slots, g_final, out, tile0, t, parts)
        tile0 += group_tiles
    return out


def kernel(x, mem, g_mix, g_mem, w_in, w_mem_kv, na_rel_bias, g_grp, w_out, g_ffn, router_w, router_b,
           w_gu, b_gu, w_down, b_down, g_final):
    b, seq, d = x.shape
    depth = w_in.shape[0]
    assert depth == 1, "the final norm is fused into the single layer's combine step"
    out = _layer_and_final_norm(
        x.reshape(b * seq, d), mem, seq, g_mix[0], g_mem[0], w_in[0], w_mem_kv[0], na_rel_bias[0], g_grp[0],
        w_out[0], g_ffn[0], router_w[0], router_b[0], w_gu[0], b_gu[0], w_down[0], b_down[0], g_final)
    return out.reshape(b, seq, d)
```

```python
import numpy as np
import jax
import jax.numpy as jnp
from jax import lax
from jax.experimental import pallas as pl
from jax.experimental.pallas import tpu as pltpu
from jax.experimental.pallas import tpu_sc as plsc

f32 = jnp.float32
bf16 = jnp.bfloat16
u32 = jnp.uint32
i32 = jnp.int32

GRID_W = 64
NA_HEADS = 8
NA_HEAD_DIM = 64
NA_WIN_ROWS = 8
NA_WIN_COLS = 16
FT_GROUPS = 4
FT_GROUP_DIM = 128
MEM_HEADS = 4
MEM_HEAD_DIM = 128
NA_WIDTH = NA_HEADS * NA_HEAD_DIM
FT_WIDTH = FT_GROUPS * FT_GROUP_DIM
MEM_WIDTH = MEM_HEADS * MEM_HEAD_DIM
N_EXPERTS = 32
TOP_K = 4
SWIGLU_LIMIT = 7.0
SWIGLU_ALPHA = 1.702
EPS = 1e-6

LANES = 128
SUBLANES = 8
VMEM_LIMIT_BYTES = 56 * 1024 * 1024
SC_CORES = 2
SC_SUBCORES = 16
SC_GATHER_CHUNK = 64

TOKEN_TILE = 512
IN_PROJ_TILE = 1024
MOE_BLOCK = 512
MOE_SUB_BLOCK = 128
WEIGHT_CAST_ROWS = 128
MOE_GROUP_SHARES = (5, 3)
NA_ROW_UNROLL = 16
FT_N1 = 64
FT_N2 = 128
FT_K1_BLOCK = 8
LOG2_E = 1.4426950408889634
MASK_VALUE = -jnp.inf


def _params(*semantics):
    return pltpu.CompilerParams(dimension_semantics=semantics, vmem_limit_bytes=VMEM_LIMIT_BYTES)


def _rms_scale(x):
    return x * lax.rsqrt(jnp.mean(x * x, axis=-1, keepdims=True) + EPS)


def _softmax_rows(s):
    p = jnp.exp(s - jnp.max(s, axis=-1, keepdims=True))
    return p / jnp.sum(p, axis=-1, keepdims=True)


def _bf16_bits(x):
    return pltpu.bitcast(x.astype(bf16).astype(f32), u32)


def _pack2(lo, hi):
    return (_bf16_bits(lo) >> 16) | (_bf16_bits(hi) & jnp.uint32(0xFFFF0000))


def _unpack2(w):
    return pltpu.bitcast(w << 16, f32), pltpu.bitcast(w & jnp.uint32(0xFFFF0000), f32)


def _pack_bf16_pairs(x):
    n = x.shape[1] // 2
    return _pack2(x[:, :n], x[:, n:])


def _unpack_bf16_pairs(w):
    return jnp.concatenate(_unpack2(w), axis=1)


def _in_proj_kernel(x_ref, g_ref, w_ref, qna_ref, kna_ref, vna_ref, uft_ref, qmem_ref):
    h = _rms_scale(x_ref[...]) * g_ref[...]
    proj = jnp.dot(h.astype(bf16), w_ref[...], preferred_element_type=f32)
    o = NA_WIDTH
    qna_ref[...] = (proj[:, :o] * (NA_HEAD_DIM ** -0.5 * LOG2_E)).astype(bf16)
    kna_ref[...] = proj[:, o:2 * o].astype(bf16)
    vna_ref[...] = proj[:, 2 * o:3 * o].astype(bf16)
    uft_ref[...] = _pack_bf16_pairs(proj[:, 3 * o:3 * o + FT_WIDTH])
    qmem_ref[...] = proj[:, 3 * o + FT_WIDTH:].astype(bf16)


def _in_proj(x2d, g_mix, w_in_bf16):
    t, d = x2d.shape
    tm = IN_PROJ_TILE
    row = lambda w: pl.BlockSpec((tm, w), lambda i: (i, 0))
    return pl.pallas_call(
        _in_proj_kernel,
        grid=(t // tm,),
        in_specs=[row(d), pl.BlockSpec((1, d), lambda i: (0, 0)),
                  pl.BlockSpec(w_in_bf16.shape, lambda i: (0, 0))],
        out_specs=[row(NA_WIDTH), row(NA_WIDTH), row(NA_WIDTH), row(FT_WIDTH // 2), row(MEM_WIDTH)],
        out_shape=[jax.ShapeDtypeStruct((t, NA_WIDTH), bf16)] * 3
        + [jax.ShapeDtypeStruct((t, FT_WIDTH // 2), u32), jax.ShapeDtypeStruct((t, MEM_WIDTH), bf16)],
        compiler_params=_params("parallel"),
        name="in_proj",
    )(x2d, g_mix.reshape(1, d), w_in_bf16)


def _na_bias_table(rel_bias):
    c = np.arange(GRID_W)
    dc_idx = np.clip(c[None, :] - c[:, None], -(NA_WIN_COLS - 1), NA_WIN_COLS - 1) + (NA_WIN_COLS - 1)
    col_start = np.clip(c - NA_WIN_COLS // 2, 0, GRID_W - NA_WIN_COLS)
    col_in = (c[None, :] >= col_start[:, None]) & (c[None, :] < col_start[:, None] + NA_WIN_COLS)
    pick_c = jnp.asarray(dc_idx[:, :, None] == np.arange(2 * NA_WIN_COLS - 1), f32)
    cols = jnp.einsum("hab,qcb->haqc", rel_bias.astype(f32), pick_c, precision=lax.Precision.HIGHEST)
    cols = jnp.where(col_in[None, None], cols * LOG2_E, MASK_VALUE)
    tab = jnp.stack([jnp.concatenate([cols[:, j - s + NA_WIN_ROWS - 1] for j in range(NA_WIN_ROWS)], axis=-1)
                     for s in range(NA_WIN_ROWS)])
    return tab.reshape(NA_WIN_ROWS, NA_HEADS // 2, 2 * GRID_W, NA_WIN_ROWS * GRID_W)


def _na_kernel(q_ref, k_ref, v_ref, bias_ref, o_ref):
    rows = q_ref.shape[1] // GRID_W
    win = NA_WIN_ROWS * GRID_W
    first_head = lax.broadcasted_iota(i32, (GRID_W, 2 * NA_HEAD_DIM), 1) < NA_HEAD_DIM

    def body(it, carry):
        scores, values, q_offsets = [], [], []
        for u in range(NA_ROW_UNROLL):
            r = it * NA_ROW_UNROLL + u
            row_start = jnp.clip(r - NA_WIN_ROWS // 2, 0, rows - NA_WIN_ROWS)
            q0 = pl.multiple_of(r * GRID_W, GRID_W)
            k0 = pl.multiple_of(row_start * GRID_W, GRID_W)
            q = q_ref[0, pl.ds(q0, GRID_W), :]
            zero = jnp.zeros_like(q)
            qm = jnp.concatenate([jnp.where(first_head, q, zero), jnp.where(first_head, zero, q)], axis=0)
            s = lax.dot_general(qm, k_ref[0, pl.ds(k0, win), :], (((1,), (1,)), ((), ())),
                                preferred_element_type=f32)
            scores.append(s + bias_ref[r - row_start, 0])
            values.append(v_ref[0, pl.ds(k0, win), :])
            q_offsets.append(q0)
        s = jnp.concatenate(scores, axis=0)
        p = jnp.exp2(s - jnp.max(s, axis=-1, keepdims=True))
        inv_den = 1.0 / jnp.sum(p, axis=-1, keepdims=True)
        p = p.astype(bf16)
        for u in range(NA_ROW_UNROLL):
            sl = slice(u * 2 * GRID_W, (u + 1) * 2 * GRID_W)
            o = jnp.dot(p[sl], values[u], preferred_element_type=f32) * inv_den[sl]
            o_ref[0, pl.ds(q_offsets[u], GRID_W), :] = jnp.where(
                first_head, o[:GRID_W], o[GRID_W:]).astype(o_ref.dtype)
        return carry

    lax.fori_loop(0, rows // NA_ROW_UNROLL, body, 0)


def _neighbourhood_attention(q, k, v, bias_tab):
    b, s, _ = q.shape
    pair = 2 * NA_HEAD_DIM
    qkv_spec = pl.BlockSpec((1, s, pair), lambda bi, hp: (bi, 0, hp))
    return pl.pallas_call(
        _na_kernel,
        grid=(b, NA_HEADS // 2),
        in_specs=[qkv_spec, qkv_spec, qkv_spec,
                  pl.BlockSpec((NA_WIN_ROWS, 1, 2 * GRID_W, NA_WIN_ROWS * GRID_W), lambda bi, hp: (0, hp, 0, 0))],
        out_specs=qkv_spec,
        out_shape=jax.ShapeDtypeStruct((b, s, NA_WIDTH), bf16),
        compiler_params=_params("parallel", "parallel"),
        name="neighbourhood_attention",
    )(q, k, v, bias_tab)


def _ft_tables(seq):
    assert seq == FT_N1 * FT_N2
    n_blk = FT_N2 // SUBLANES
    k1 = np.arange(FT_N1)[:, None, None, None]
    sr = np.arange(SUBLANES)[None, :, None, None]
    n1 = np.arange(FT_N1)[None, None, :, None]
    sc = np.arange(SUBLANES)[None, None, None, :]
    stage1 = np.zeros((n_blk, 2, FT_N1, SUBLANES, FT_N1, SUBLANES), np.float64)
    for blk in range(n_blk):
        n = FT_N2 * n1 + SUBLANES * blk + sr
        ang = 2.0 * np.pi * ((k1 * n) % seq) / seq
        eye = (sr == sc)
        stage1[blk, 0] = np.cos(ang) * eye
        stage1[blk, 1] = -np.sin(ang) * eye
    stage1 = stage1.reshape(n_blk, 2 * FT_N1 * SUBLANES, FT_N1 * SUBLANES)
    a = np.arange(FT_N2)
    ang2 = 2.0 * np.pi * ((a[:, None] * a[None, :]) % FT_N2) / FT_N2
    c2, s2 = np.cos(ang2), np.sin(ang2)
    stage2 = np.block([[c2, s2], [-s2, c2]])
    g = np.arange(FT_GROUP_DIM)
    angc = 2.0 * np.pi * ((g[:, None] * g[None, :]) % FT_GROUP_DIM) / FT_GROUP_DIM
    norm = 1.0 / np.sqrt(seq * FT_GROUP_DIM)
    chan = np.concatenate([np.cos(angc), np.sin(angc)], axis=0) * norm
    return (jnp.asarray(stage1, bf16), jnp.asarray(stage2, bf16), jnp.asarray(chan, bf16))


def _ft_stage1_kernel(u_ref, m_ref, z_ref):
    rows = FT_N1 * SUBLANES
    u = _unpack_bf16_pairs(u_ref[0].reshape(rows, FT_WIDTH // 2)).astype(bf16)
    z = jnp.dot(m_ref[0], u, preferred_element_type=f32)
    z_ref[0] = _pack2(z[:rows], z[rows:]).reshape(FT_N1, SUBLANES, FT_WIDTH)


def _ft_stage2_kernel(z_ref, s2_ref, cs_ref, y_ref):
    gd = FT_GROUP_DIM
    for kk in range(FT_K1_BLOCK):
        zz = jnp.concatenate(_unpack2(z_ref[0, kk]), axis=0).astype(bf16)
        x = jnp.dot(s2_ref[...], zz, preferred_element_type=f32)
        outs = []
        for g in range(FT_GROUPS):
            xg = jnp.concatenate([x[:FT_N2, g * gd:(g + 1) * gd], x[FT_N2:, g * gd:(g + 1) * gd]], axis=1)
            outs.append(jnp.dot(xg.astype(bf16), cs_ref[...], preferred_element_type=f32))
        y_ref[0, kk] = jnp.concatenate(outs, axis=1).astype(y_ref.dtype)


def _fourier_mix(u_packed, tables):
    u = u_packed
    b, s, _ = u.shape
    c = FT_WIDTH
    stage1, stage2, chan = tables
    n_blk = FT_N2 // SUBLANES
    z = pl.pallas_call(
        _ft_stage1_kernel,
        grid=(n_blk, b),
        in_specs=[pl.BlockSpec((1, FT_N1, SUBLANES, c // 2), lambda j, bi: (bi, 0, j, 0)),
                  pl.BlockSpec((1,) + stage1.shape[1:], lambda j, bi: (j, 0, 0))],
        out_specs=pl.BlockSpec((1, FT_N1, SUBLANES, c), lambda j, bi: (bi, 0, j, 0)),
        out_shape=jax.ShapeDtypeStruct((b, FT_N1, FT_N2, c), u32),
        compiler_params=_params("parallel", "parallel"),
        name="fourier_stage1",
    )(u.reshape(b, FT_N1, FT_N2, c // 2), stage1)
    y = pl.pallas_call(
        _ft_stage2_kernel,
        grid=(b, FT_N1 // FT_K1_BLOCK),
        in_specs=[pl.BlockSpec((1, FT_K1_BLOCK, FT_N2, c), lambda bi, kb: (bi, kb, 0, 0)),
                  pl.BlockSpec(stage2.shape, lambda bi, kb: (0, 0)),
                  pl.BlockSpec(chan.shape, lambda bi, kb: (0, 0))],
        out_specs=pl.BlockSpec((1, FT_K1_BLOCK, FT_N2, c), lambda bi, kb: (bi, kb, 0, 0)),
        out_shape=jax.ShapeDtypeStruct((b, FT_N1, FT_N2, c), bf16),
        compiler_params=_params("parallel", "parallel"),
        name="fourier_stage2",
    )(z, stage2, chan)
    return y.transpose(0, 2, 1, 3).reshape(b, s, c)


def _mem_kv_kernel(mem_ref, g_ref, w_ref, k_ref, v_ref):
    mn = _rms_scale(mem_ref[0]) * g_ref[...]
    kv = jnp.dot(mn.astype(bf16), w_ref[...], preferred_element_type=f32)
    k_ref[0] = kv[:, :MEM_WIDTH].astype(bf16)
    v_ref[0] = kv[:, MEM_WIDTH:].astype(bf16)


def _mem_kv(mem, g_mem, w_kv_bf16):
    b, m, d = mem.shape
    kv_spec = pl.BlockSpec((1, m, MEM_WIDTH), lambda bi: (bi, 0, 0))
    return pl.pallas_call(
        _mem_kv_kernel,
        grid=(b,),
        in_specs=[pl.BlockSpec((1, m, d), lambda bi: (bi, 0, 0)), pl.BlockSpec((1, d), lambda bi: (0, 0)),
                  pl.BlockSpec(w_kv_bf16.shape, lambda bi: (0, 0))],
        out_specs=[kv_spec, kv_spec],
        out_shape=[jax.ShapeDtypeStruct((b, m, MEM_WIDTH), bf16)] * 2,
        compiler_params=_params("parallel"),
        name="mem_kv",
    )(mem, g_mem.reshape(1, d), w_kv_bf16)


def _mix_out_kernel(x_ref, yna_ref, yft_ref, qm_ref, km_ref, vm_ref, ggrp_ref, wout_ref, gffn_ref, rw_ref,
                    rb_ref, _order_ref, x1_ref, h2p_ref, eidx_ref, gate_ref, rank_ref, cnt_ref, carry_ref):
    tm = x_ref.shape[0]

    @pl.when(pl.program_id(0) == 0)
    def _():
        carry_ref[...] = jnp.zeros_like(carry_ref)

    q = qm_ref[...]
    km = km_ref[0]
    vm = vm_ref[0]
    heads = []
    for h in range(MEM_HEADS):
        sl = slice(h * MEM_HEAD_DIM, (h + 1) * MEM_HEAD_DIM)
        s = lax.dot_general(q[:, sl], km[:, sl], (((1,), (1,)), ((), ())), preferred_element_type=f32)
        p = _softmax_rows(s * (MEM_HEAD_DIM ** -0.5))
        heads.append(jnp.dot(p.astype(bf16), vm[:, sl], preferred_element_type=f32))
    ymem = jnp.concatenate(heads, axis=1)

    g = ggrp_ref[...]
    a, c = NA_WIDTH, NA_WIDTH + FT_WIDTH
    y = jnp.concatenate([_rms_scale(yna_ref[...].astype(f32)) * g[:, :a],
                         _rms_scale(yft_ref[...].astype(f32)) * g[:, a:c],
                         _rms_scale(ymem) * g[:, c:]], axis=1)
    x1 = x_ref[...] + jnp.dot(y.astype(bf16), wout_ref[...], preferred_element_type=f32)
    x1_ref[...] = x1
    h2 = _rms_scale(x1) * gffn_ref[...]
    h2p_ref[...] = _pack_bf16_pairs(h2)

    h_hi = h2.astype(bf16)
    h_lo = (h2 - h_hi.astype(f32)).astype(bf16)
    hh = jnp.dot(h_hi, rw_ref[...], preferred_element_type=f32)
    logits = (hh[:, :LANES] + hh[:, LANES:]
              + jnp.dot(h_lo, rw_ref[:, :LANES], preferred_element_type=f32)) + rb_ref[...]
    l = logits.T[:N_EXPERTS]
    row = lax.broadcasted_iota(i32, (N_EXPERTS, tm), 0).astype(f32)
    vals, idxs, sels = [], [], []
    for _ in range(TOP_K):
        m = jnp.max(l, axis=0, keepdims=True)
        idx = jnp.min(jnp.where(l == m, row, float(N_EXPERTS)), axis=0, keepdims=True)
        sel = row == idx
        vals.append(m)
        idxs.append(idx)
        sels.append(sel)
        l = jnp.where(sel, -jnp.inf, l)
    ex = [jnp.exp(v - vals[0]) for v in vals]
    den = ex[0] + ex[1] + ex[2] + ex[3]

    onehot = (sels[0] | sels[1] | sels[2] | sels[3]).astype(f32)
    earlier = (lax.broadcasted_iota(i32, (tm, tm), 0) < lax.broadcasted_iota(i32, (tm, tm), 1)).astype(bf16)
    before = jnp.dot(onehot.astype(bf16), earlier, preferred_element_type=f32) + carry_ref[...]
    ranks = [jnp.sum(jnp.where(sel, before, 0.0), axis=0, keepdims=True) for sel in sels]
    carry_ref[...] = carry_ref[...] + jnp.sum(onehot, axis=1, keepdims=True)
    cnt_ref[...] = carry_ref[...].astype(i32)

    eidx_ref[...] = jnp.concatenate(idxs, axis=0).astype(i32)
    rank_ref[...] = jnp.concatenate(ranks, axis=0).astype(i32)
    gates_t = jnp.concatenate([e / den for e in ex] + [jnp.zeros((LANES - TOP_K, tm), f32)], axis=0)
    gate_ref[...] = gates_t.T[:, :TOP_K]


def _mix_out(x2d, y_na, y_ft, q_mem, k_mem, v_mem, g_grp, w_out_bf16, g_ffn, router_w2, router_b, seq,
             tile0, n_tiles, order_after):
    d = x2d.shape[1]
    tm = TOKEN_TILE
    t = n_tiles * tm
    steps_per_batch = seq // tm
    m = k_mem.shape[1]
    row_in = lambda w: pl.BlockSpec((tm, w), lambda i: (i + tile0, 0))
    row_out = lambda w: pl.BlockSpec((tm, w), lambda i: (i, 0))
    full = lambda a: pl.BlockSpec(a.shape, lambda i: (0,) * a.ndim)
    kv_spec = pl.BlockSpec((1, m, MEM_WIDTH), lambda i: ((i + tile0) // steps_per_batch, 0, 0))
    g_grp2, g_ffn2 = g_grp.reshape(1, -1), g_ffn.reshape(1, d)
    rb2 = jnp.pad(router_b.reshape(1, N_EXPERTS), ((0, 0), (0, LANES - N_EXPERTS)))
    router_w2 = jnp.pad(router_w2, ((0, 0), (0, 0), (0, LANES - N_EXPERTS)))
    router_w2 = jnp.concatenate([router_w2[0], router_w2[1]], axis=1)
    col_out = pl.BlockSpec((TOP_K, tm), lambda i: (0, i))
    return pl.pallas_call(
        _mix_out_kernel,
        grid=(n_tiles,),
        in_specs=[row_in(d), row_in(NA_WIDTH), row_in(FT_WIDTH), row_in(MEM_WIDTH), kv_spec, kv_spec,
                  full(g_grp2), full(w_out_bf16), full(g_ffn2), full(router_w2), full(rb2),
                  pl.BlockSpec(memory_space=pl.ANY)],
        out_specs=[row_out(d), row_out(d // 2), col_out, row_out(TOP_K), col_out,
                   pl.BlockSpec((N_EXPERTS, 1), lambda i: (0, 0))],
        out_shape=[jax.ShapeDtypeStruct((t, d), f32), jax.ShapeDtypeStruct((t, d // 2), u32),
                   jax.ShapeDtypeStruct((TOP_K, t), i32), jax.ShapeDtypeStruct((t, TOP_K), f32),
                   jax.ShapeDtypeStruct((TOP_K, t), i32), jax.ShapeDtypeStruct((N_EXPERTS, 1), i32)],
        scratch_shapes=[pltpu.VMEM((N_EXPERTS, 1), f32)],
        compiler_params=_params("arbitrary"),
        name="mix_out_router",
    )(x2d, y_na, y_ft, q_mem, k_mem, v_mem, g_grp2, w_out_bf16, g_ffn2, router_w2, rb2, order_after)


def _sc_mesh():
    return plsc.VectorSubcoreMesh(core_axis_name="c", subcore_axis_name="s",
                                  num_cores=SC_CORES, num_subcores=SC_SUBCORES)


def _dispatch(h2p, dest, n_slots):
    t, w = h2p.shape
    workers = SC_CORES * SC_SUBCORES
    chunk = SC_GATHER_CHUNK
    per = t // workers
    steps = per // chunk
    assert per * workers == t and steps * chunk == per and steps % 2 == 0
    idx = dest.reshape(TOP_K, workers, steps, chunk)

    def body(h_hbm, idx_hbm, out_hbm, idx_v, rows_v, rsem, ssem):
        wid = lax.axis_index("s") * SC_CORES + lax.axis_index("c")
        base = wid * per
        for j in range(TOP_K):
            pltpu.sync_copy(idx_hbm.at[j, wid], idx_v.at[j])

        def read(c, slot):
            return pltpu.make_async_copy(h_hbm.at[pl.ds(base + c * chunk, chunk)], rows_v.at[slot], rsem.at[slot])

        def scatters(c, slot):
            return [pltpu.make_async_copy(rows_v.at[slot], out_hbm.at[idx_v.at[j, c]], ssem.at[slot])
                    for j in range(TOP_K)]

        read(0, 0).start()

        @pl.loop(0, steps, step=2)
        def _(c0):
            for slot in range(2):
                c = c0 + slot
                read(c, slot).wait()
                for cp in scatters(c, slot):
                    cp.start()

                @pl.when(c >= 1)
                def _():
                    for cp in scatters(c - 1, 1 - slot):
                        cp.wait()

                @pl.when(c + 1 < steps)
                def _():
                    read(c + 1, 1 - slot).start()

        for cp in scatters(steps - 1, 1):
            cp.wait()

    return pl.kernel(
        body,
        out_type=jax.ShapeDtypeStruct((n_slots, w), h2p.dtype),
        mesh=_sc_mesh(),
        scratch_types=[pltpu.VMEM((TOP_K, steps, chunk), i32), pltpu.VMEM((2, chunk, w), h2p.dtype),
                       pltpu.SemaphoreType.DMA((2,)), pltpu.SemaphoreType.DMA((2,))],
        name="sc_dispatch_rows",
    )(h2p, idx)


def _expert_kernel(blk_e_ref, blk_cnt_ref, nxt_e_ref, xs_ref, wgu_hbm, bgu_ref, wd_hbm, bd_ref, y_ref,
                   wgu_f32, wd_f32, wgu_bf, wd_bf, sem):
    b = pl.program_id(0)
    e = blk_e_ref[b]
    cnt = blk_cnt_ref[b]
    bm = xs_ref.shape[0]
    de = wd_f32.shape[0]

    def fetch(expert):
        return (pltpu.make_async_copy(wgu_hbm.at[expert], wgu_f32, sem.at[0]),
                pltpu.make_async_copy(wd_hbm.at[expert], wd_f32, sem.at[1]))

    @pl.when(b == 0)
    def _():
        for cp in fetch(e):
            cp.start()

    @pl.when(jnp.logical_or(b == 0, e != blk_e_ref[jnp.maximum(b - 1, 0)]))
    def _():
        for cp in fetch(e):
            cp.wait()

        def convert(i, carry):
            rows = pl.ds(pl.multiple_of(i * WEIGHT_CAST_ROWS, WEIGHT_CAST_ROWS), WEIGHT_CAST_ROWS)
            wgu_bf[rows, :] = wgu_f32[rows, :].astype(bf16)
            wd_bf[rows, :] = wd_f32[rows, :].astype(bf16)
            return carry

        lax.fori_loop(0, wgu_f32.shape[0] // WEIGHT_CAST_ROWS, convert, 0)

        @pl.when(nxt_e_ref[b] >= 0)
        def _():
            for cp in fetch(nxt_e_ref[b]):
                cp.start()

    def ffn_rows(r0, n):
        rows = pl.ds(r0, n)
        valid = r0 + lax.broadcasted_iota(i32, (n, 1), 0) < cnt
        x = jnp.where(valid, _unpack_bf16_pairs(xs_ref[rows, :]), 0.0).astype(bf16)
        gu = jnp.dot(x, wgu_bf[...], preferred_element_type=f32) + bgu_ref[0]
        x_glu = jnp.minimum(gu[:, :de], SWIGLU_LIMIT)
        x_lin = jnp.clip(gu[:, de:], -SWIGLU_LIMIT, SWIGLU_LIMIT)
        act = x_glu * (1.0 / (1.0 + jnp.exp(-SWIGLU_ALPHA * x_glu))) * (x_lin + 1.0)
        y = jnp.dot(act.astype(bf16), wd_bf[...], preferred_element_type=f32) + bd_ref[0]
        y_ref[rows, :] = _pack_bf16_pairs(y)

    @pl.when(cnt == bm)
    def _():
        ffn_rows(0, bm)

    @pl.when(cnt < bm)
    def _():
        y_ref[...] = jnp.zeros_like(y_ref)

        def piece(i, carry):
            ffn_rows(pl.multiple_of(i * MOE_SUB_BLOCK, MOE_SUB_BLOCK), MOE_SUB_BLOCK)
            return carry

        lax.fori_loop(0, (cnt + MOE_SUB_BLOCK - 1) // MOE_SUB_BLOCK, piece, 0)


def _experts(xs, blk_e, blk_cnt, nxt_e, w_gu, b_gu, w_down, b_down):
    n_slots, w = xs.shape
    bm = MOE_BLOCK
    e, d, de2 = w_gu.shape
    de = w_down.shape[1]
    assert de == d, "one row loop converts both weight matrices"
    last = n_slots // bm - 1

    def rows_of_block(b, be, bc, ne):
        return (jnp.where(bc[b] > 0, b, last), 0)

    grid_spec = pltpu.PrefetchScalarGridSpec(
        num_scalar_prefetch=3,
        grid=(n_slots // bm,),
        in_specs=[pl.BlockSpec((bm, w), rows_of_block),
                  pl.BlockSpec(memory_space=pl.ANY),
                  pl.BlockSpec((1, 1, de2), lambda b, be, bc, ne: (be[b], 0, 0)),
                  pl.BlockSpec(memory_space=pl.ANY),
                  pl.BlockSpec((1, 1, d), lambda b, be, bc, ne: (be[b], 0, 0))],
        out_specs=pl.BlockSpec((bm, w), rows_of_block),
        scratch_shapes=[pltpu.VMEM((d, de2), f32), pltpu.VMEM((de, d), f32),
                        pltpu.VMEM((d, de2), bf16), pltpu.VMEM((de, d), bf16),
                        pltpu.SemaphoreType.DMA((2,))],
    )
    return pl.pallas_call(
        _expert_kernel,
        grid_spec=grid_spec,
        out_shape=jax.ShapeDtypeStruct((n_slots, w), u32),
        compiler_params=_params("arbitrary"),
        name="moe_experts",
    )(blk_e, blk_cnt, nxt_e, xs, w_gu, b_gu.reshape(e, 1, de2), w_down, b_down.reshape(e, 1, d))


def _sc_gather_rows(table, idx):
    n, w = idx.shape[0], table.shape[1]
    workers = SC_CORES * SC_SUBCORES
    chunk = SC_GATHER_CHUNK
    per = n // workers
    steps = per // chunk
    assert per * workers == n and steps * chunk == per and steps % 2 == 0

    def body(table_hbm, idx_hbm, out_hbm, idx_v, rows_v, gsem, wsem):
        base = (lax.axis_index("s") * SC_CORES + lax.axis_index("c")) * per
        pltpu.sync_copy(idx_hbm.at[pl.ds(base, per)], idx_v)

        def gather(c, slot):
            return pltpu.make_async_copy(table_hbm.at[idx_v.at[pl.ds(c * chunk, chunk)]], rows_v.at[slot],
                                         gsem.at[slot])

        def write(c, slot):
            return pltpu.make_async_copy(rows_v.at[slot], out_hbm.at[pl.ds(base + c * chunk, chunk)],
                                         wsem.at[slot])

        gather(0, 0).start()

        @pl.loop(0, steps, step=2)
        def _(c0):
            for slot in range(2):
                c = c0 + slot
                gather(c, slot).wait()
                write(c, slot).start()

                @pl.when(c >= 1)
                def _():
                    write(c - 1, 1 - slot).wait()

                @pl.when(c + 1 < steps)
                def _():
                    gather(c + 1, 1 - slot).start()

        write(steps - 1, 1).wait()

    return pl.kernel(
        body,
        out_type=jax.ShapeDtypeStruct((n, w), table.dtype),
        mesh=_sc_mesh(),
        scratch_types=[pltpu.VMEM((per,), i32), pltpu.VMEM((2, chunk, w), table.dtype),
                       pltpu.SemaphoreType.DMA((2,)), pltpu.SemaphoreType.DMA((2,))],
        name="sc_gather_rows",
    )(table, idx)


def _combine_kernel(x1_ref, gate_ref, gfin_ref, yg_ref, *rest):
    o_ref = rest[-1]
    acc = x1_ref[...]
    gates = gate_ref[...]
    for j in range(TOP_K):
        acc = acc + gates[:, j:j + 1] * _unpack_bf16_pairs(yg_ref[j])
    o_ref[...] = _rms_scale(acc) * gfin_ref[...]


def _combine(x1, gates, dest, y_slots, g_final, out_prev, tile0, total_tokens):
    t, d = x1.shape
    tm = TOKEN_TILE
    w = y_slots.shape[1]
    yg = _sc_gather_rows(y_slots, dest.reshape(-1)).reshape(TOP_K, t, w)
    in_specs = [pl.BlockSpec((tm, d), lambda i: (i, 0)),
                pl.BlockSpec((tm, TOP_K), lambda i: (i, 0)),
                pl.BlockSpec((1, d), lambda i: (0, 0)),
                pl.BlockSpec((TOP_K, tm, w), lambda i: (0, i, 0))]
    args = [x1, gates, g_final.reshape(1, d), yg]
    aliases = {}
    if out_prev is not None:
        in_specs.append(pl.BlockSpec(memory_space=pl.ANY))
        args.append(out_prev)
        aliases = {len(args) - 1: 0}
    return pl.pallas_call(
        _combine_kernel,
        grid=(t // tm,),
        in_specs=in_specs,
        out_specs=pl.BlockSpec((tm, d), lambda i: (i + tile0, 0)),
        out_shape=jax.ShapeDtypeStruct((total_tokens, d), f32),
        input_output_aliases=aliases,
        compiler_params=_params("parallel"),
        name="moe_combine",
    )(*args)


def _dest_kernel(start_ref, eidx_ref, rank_ref, dest_ref):
    eidx = eidx_ref[...]
    dest = rank_ref[...]
    for e in range(N_EXPERTS):
        dest = dest + jnp.where(eidx == e, start_ref[e], 0)
    dest_ref[...] = dest


def _slot_layout(counts, eidx, rank, n_blocks):
    bm = MOE_BLOCK
    padded = (counts + bm - 1) // bm * bm
    padded_end = jnp.cumsum(padded)
    start = padded_end - padded
    experts = jnp.arange(N_EXPERTS, dtype=i32)
    lookup = lambda table, idx: jnp.sum(jnp.where(idx[..., None] == experts, table, 0), axis=-1)
    dest = pl.pallas_call(
        _dest_kernel,
        grid_spec=pltpu.PrefetchScalarGridSpec(
            num_scalar_prefetch=1, grid=(1,),
            in_specs=[pl.BlockSpec(eidx.shape, lambda i, st: (0, 0)), pl.BlockSpec(rank.shape, lambda i, st: (0, 0))],
            out_specs=pl.BlockSpec(rank.shape, lambda i, st: (0, 0))),
        out_shape=jax.ShapeDtypeStruct(rank.shape, i32),
        name="moe_dest",
    )(start.astype(i32), eidx, rank)
    blk_row = jnp.arange(n_blocks, dtype=i32) * bm
    blk_e = jnp.minimum(jnp.sum((padded_end[None, :] <= blk_row[:, None]).astype(i32), axis=1), N_EXPERTS - 1)
    blk_cnt = jnp.clip(lookup(counts, blk_e) - (blk_row - lookup(start, blk_e)), 0, bm).astype(i32)
    none = jnp.int32(N_EXPERTS)
    nxt_e = jnp.min(jnp.where(blk_e[None, :] > blk_e[:, None], blk_e[None, :], none), axis=1)
    nxt_e = jnp.where(nxt_e == none, -1, nxt_e).astype(i32)
    return dest, blk_e, blk_cnt, nxt_e


def _layer_and_final_norm(x2d, mem, seq, g_mix, g_mem, w_in, w_mem_kv, na_rel_bias, g_grp, w_out, g_ffn,
                          router_w, router_b, w_gu, b_gu, w_down, b_down, g_final):
    t, d = x2d.shape
    b = t // seq
    q_na, k_na, v_na, u_ft, q_mem = _in_proj(x2d, g_mix, w_in.astype(bf16))
    k_mem, v_mem = _mem_kv(mem, g_mem, w_mem_kv.astype(bf16))
    shape3 = lambda a: a.reshape(b, seq, a.shape[-1])
    y_na = _neighbourhood_attention(shape3(q_na), shape3(k_na), shape3(v_na), _na_bias_table(na_rel_bias))
    y_ft = _fourier_mix(shape3(u_ft), _ft_tables(seq))
    y_na, y_ft = y_na.reshape(t, -1), y_ft.reshape(t, -1)
    w_out_bf16 = w_out.astype(bf16)
    rw_hi = router_w.astype(bf16)
    router_w2 = jnp.stack([rw_hi, (router_w - rw_hi.astype(f32)).astype(bf16)])

    tiles = t // TOKEN_TILE
    unit = tiles // sum(MOE_GROUP_SHARES)
    assert unit * sum(MOE_GROUP_SHARES) == tiles
    out = None
    tile0 = 0
    dest = router_b
    for share in MOE_GROUP_SHARES:
        group_tiles = share * unit
        n_blocks = (group_tiles * TOKEN_TILE * TOP_K) // MOE_BLOCK + N_EXPERTS
        x1, h2p, eidx, gates, rank, counts = _mix_out(
            x2d, y_na, y_ft, q_mem, k_mem, v_mem, g_grp, w_out_bf16, g_ffn, router_w2, router_b, seq,
            tile0, group_tiles, dest)
        dest, blk_e, blk_cnt, nxt_e = _slot_layout(counts[:, 0], eidx, rank, n_blocks)
        xs = _dispatch(h2p, dest, n_blocks * MOE_BLOCK)
        y_slots = _experts(xs, blk_e, blk_cnt, nxt_e, w_gu, b_gu, w_down, b_down)
        out = _combine(x1, gates, dest, y_slots, g_final, out, tile0, t)
        tile0 += group_tiles
    return out


def kernel(x, mem, g_mix, g_mem, w_in, w_mem_kv, na_rel_bias, g_grp, w_out, g_ffn, router_w, router_b,
           w_gu, b_gu, w_down, b_down, g_final):
    b, seq, d = x.shape
    depth = w_in.shape[0]
    assert depth == 1, "the final norm is fused into the single layer's combine step"
    out = _layer_and_final_norm(
        x.reshape(b * seq, d), mem, seq, g_mix[0], g_mem[0], w_in[0], w_mem_kv[0], na_rel_bias[0], g_grp[0],
        w_out[0], g_ffn[0], router_w[0], router_b[0], w_gu[0], b_gu[0], w_down[0], b_down[0], g_final)
    return out.reshape(b, seq, d)
```

```python
import numpy as np
import jax
import jax.numpy as jnp
from jax import lax
from jax.experimental import pallas as pl
from jax.experimental.pallas import tpu as pltpu
from jax.experimental.pallas import tpu_sc as plsc

f32 = jnp.float32
bf16 = jnp.bfloat16
u32 = jnp.uint32
i32 = jnp.int32

GRID_W = 64
NA_HEADS = 8
NA_HEAD_DIM = 64
NA_WIN_ROWS = 8
NA_WIN_COLS = 16
FT_GROUPS = 4
FT_GROUP_DIM = 128
MEM_HEADS = 4
MEM_HEAD_DIM = 128
NA_WIDTH = NA_HEADS * NA_HEAD_DIM
FT_WIDTH = FT_GROUPS * FT_GROUP_DIM
MEM_WIDTH = MEM_HEADS * MEM_HEAD_DIM
N_EXPERTS = 32
TOP_K = 4
SWIGLU_LIMIT = 7.0
SWIGLU_ALPHA = 1.702
EPS = 1e-6

LANES = 128
SUBLANES = 8
VMEM_LIMIT_BYTES = 56 * 1024 * 1024
SC_CORES = 2
SC_SUBCORES = 16
SC_GATHER_CHUNK = 64

TOKEN_TILE = 1024
IN_PROJ_TILE = 1024
MOE_BLOCK = 512
MOE_SUB_BLOCK = 128
WEIGHT_CAST_ROWS = 128
MOE_GROUP_SHARES = (5, 3)
NA_ROW_UNROLL = 16
FT_N1 = 64
FT_N2 = 128
FT_K1_BLOCK = 8
LOG2_E = 1.4426950408889634
MASK_VALUE = -jnp.inf


def _params(*semantics):
    return pltpu.CompilerParams(dimension_semantics=semantics, vmem_limit_bytes=VMEM_LIMIT_BYTES)


def _rms_scale(x):
    return x * lax.rsqrt(jnp.mean(x * x, axis=-1, keepdims=True) + EPS)


def _softmax_rows(s):
    p = jnp.exp(s - jnp.max(s, axis=-1, keepdims=True))
    return p / jnp.sum(p, axis=-1, keepdims=True)


def _bf16_bits(x):
    return pltpu.bitcast(x.astype(bf16).astype(f32), u32)


def _pack2(lo, hi):
    return (_bf16_bits(lo) >> 16) | (_bf16_bits(hi) & jnp.uint32(0xFFFF0000))


def _unpack2(w):
    return pltpu.bitcast(w << 16, f32), pltpu.bitcast(w & jnp.uint32(0xFFFF0000), f32)


def _pack_bf16_pairs(x):
    n = x.shape[1] // 2
    return _pack2(x[:, :n], x[:, n:])


def _unpack_bf16_pairs(w):
    return jnp.concatenate(_unpack2(w), axis=1)


def _in_proj_kernel(x_ref, g_ref, w_ref, qna_ref, kna_ref, vna_ref, uft_ref, qmem_ref):
    h = _rms_scale(x_ref[...]) * g_ref[...]
    proj = jnp.dot(h.astype(bf16), w_ref[...], preferred_element_type=f32)
    o = NA_WIDTH
    qna_ref[...] = (proj[:, :o] * (NA_HEAD_DIM ** -0.5 * LOG2_E)).astype(bf16)
    kna_ref[...] = proj[:, o:2 * o].astype(bf16)
    vna_ref[...] = proj[:, 2 * o:3 * o].astype(bf16)
    uft_ref[...] = _pack_bf16_pairs(proj[:, 3 * o:3 * o + FT_WIDTH])
    qmem_ref[...] = proj[:, 3 * o + FT_WIDTH:].astype(bf16)


def _in_proj(x2d, g_mix, w_in_bf16):
    t, d = x2d.shape
    tm = IN_PROJ_TILE
    row = lambda w: pl.BlockSpec((tm, w), lambda i: (i, 0))
    return pl.pallas_call(
        _in_proj_kernel,
        grid=(t // tm,),
        in_specs=[row(d), pl.BlockSpec((1, d), lambda i: (0, 0)),
                  pl.BlockSpec(w_in_bf16.shape, lambda i: (0, 0))],
        out_specs=[row(NA_WIDTH), row(NA_WIDTH), row(NA_WIDTH), row(FT_WIDTH // 2), row(MEM_WIDTH)],
        out_shape=[jax.ShapeDtypeStruct((t, NA_WIDTH), bf16)] * 3
        + [jax.ShapeDtypeStruct((t, FT_WIDTH // 2), u32), jax.ShapeDtypeStruct((t, MEM_WIDTH), bf16)],
        compiler_params=_params("parallel"),
        name="in_proj",
    )(x2d, g_mix.reshape(1, d), w_in_bf16)


def _na_bias_table(rel_bias):
    c = np.arange(GRID_W)
    dc_idx = np.clip(c[None, :] - c[:, None], -(NA_WIN_COLS - 1), NA_WIN_COLS - 1) + (NA_WIN_COLS - 1)
    col_start = np.clip(c - NA_WIN_COLS // 2, 0, GRID_W - NA_WIN_COLS)
    col_in = (c[None, :] >= col_start[:, None]) & (c[None, :] < col_start[:, None] + NA_WIN_COLS)
    pick_c = jnp.asarray(dc_idx[:, :, None] == np.arange(2 * NA_WIN_COLS - 1), f32)
    cols = jnp.einsum("hab,qcb->haqc", rel_bias.astype(f32), pick_c, precision=lax.Precision.HIGHEST)
    cols = jnp.where(col_in[None, None], cols * LOG2_E, MASK_VALUE)
    tab = jnp.stack([jnp.concatenate([cols[:, j - s + NA_WIN_ROWS - 1] for j in range(NA_WIN_ROWS)], axis=-1)
                     for s in range(NA_WIN_ROWS)])
    return tab.reshape(NA_WIN_ROWS, NA_HEADS // 2, 2 * GRID_W, NA_WIN_ROWS * GRID_W)


def _na_kernel(q_ref, k_ref, v_ref, bias_ref, o_ref):
    rows = q_ref.shape[1] // GRID_W
    win = NA_WIN_ROWS * GRID_W
    first_head = lax.broadcasted_iota(i32, (GRID_W, 2 * NA_HEAD_DIM), 1) < NA_HEAD_DIM

    def body(it, carry):
        scores, values, q_offsets = [], [], []
        for u in range(NA_ROW_UNROLL):
            r = it * NA_ROW_UNROLL + u
            row_start = jnp.clip(r - NA_WIN_ROWS // 2, 0, rows - NA_WIN_ROWS)
            q0 = pl.multiple_of(r * GRID_W, GRID_W)
            k0 = pl.multiple_of(row_start * GRID_W, GRID_W)
            q = q_ref[0, pl.ds(q0, GRID_W), :]
            zero = jnp.zeros_like(q)
            qm = jnp.concatenate([jnp.where(first_head, q, zero), jnp.where(first_head, zero, q)], axis=0)
            s = lax.dot_general(qm, k_ref[0, pl.ds(k0, win), :], (((1,), (1,)), ((), ())),
                                preferred_element_type=f32)
            scores.append(s + bias_ref[r - row_start, 0])
            values.append(v_ref[0, pl.ds(k0, win), :])
            q_offsets.append(q0)
        s = jnp.concatenate(scores, axis=0)
        p = jnp.exp2(s - jnp.max(s, axis=-1, keepdims=True))
        inv_den = 1.0 / jnp.sum(p, axis=-1, keepdims=True)
        p = p.astype(bf16)
        for u in range(NA_ROW_UNROLL):
            sl = slice(u * 2 * GRID_W, (u + 1) * 2 * GRID_W)
            o = jnp.dot(p[sl], values[u], preferred_element_type=f32) * inv_den[sl]
            o_ref[0, pl.ds(q_offsets[u], GRID_W), :] = jnp.where(
                first_head, o[:GRID_W], o[GRID_W:]).astype(o_ref.dtype)
        return carry

    lax.fori_loop(0, rows // NA_ROW_UNROLL, body, 0)


def _neighbourhood_attention(q, k, v, bias_tab):
    b, s, _ = q.shape
    pair = 2 * NA_HEAD_DIM
    qkv_spec = pl.BlockSpec((1, s, pair), lambda bi, hp: (bi, 0, hp))
    return pl.pallas_call(
        _na_kernel,
        grid=(b, NA_HEADS // 2),
        in_specs=[qkv_spec, qkv_spec, qkv_spec,
                  pl.BlockSpec((NA_WIN_ROWS, 1, 2 * GRID_W, NA_WIN_ROWS * GRID_W), lambda bi, hp: (0, hp, 0, 0))],
        out_specs=qkv_spec,
        out_shape=jax.ShapeDtypeStruct((b, s, NA_WIDTH), bf16),
        compiler_params=_params("parallel", "parallel"),
        name="neighbourhood_attention",
    )(q, k, v, bias_tab)


def _ft_tables(seq):
    assert seq == FT_N1 * FT_N2
    n_blk = FT_N2 // SUBLANES
    k1 = np.arange(FT_N1)[:, None, None, None]
    sr = np.arange(SUBLANES)[None, :, None, None]
    n1 = np.arange(FT_N1)[None, None, :, None]
    sc = np.arange(SUBLANES)[None, None, None, :]
    stage1 = np.zeros((n_blk, 2, FT_N1, SUBLANES, FT_N1, SUBLANES), np.float64)
    for blk in range(n_blk):
        n = FT_N2 * n1 + SUBLANES * blk + sr
        ang = 2.0 * np.pi * ((k1 * n) % seq) / seq
        eye = (sr == sc)
        stage1[blk, 0] = np.cos(ang) * eye
        stage1[blk, 1] = -np.sin(ang) * eye
    stage1 = stage1.reshape(n_blk, 2 * FT_N1 * SUBLANES, FT_N1 * SUBLANES)
    a = np.arange(FT_N2)
    ang2 = 2.0 * np.pi * ((a[:, None] * a[None, :]) % FT_N2) / FT_N2
    c2, s2 = np.cos(ang2), np.sin(ang2)
    stage2 = np.block([[c2, s2], [-s2, c2]])
    g = np.arange(FT_GROUP_DIM)
    angc = 2.0 * np.pi * ((g[:, None] * g[None, :]) % FT_GROUP_DIM) / FT_GROUP_DIM
    norm = 1.0 / np.sqrt(seq * FT_GROUP_DIM)
    chan = np.concatenate([np.cos(angc), np.sin(angc)], axis=0) * norm
    return (jnp.asarray(stage1, bf16), jnp.asarray(stage2, bf16), jnp.asarray(chan, bf16))


def _ft_stage1_kernel(u_ref, m_ref, z_ref):
    rows = FT_N1 * SUBLANES
    u = _unpack_bf16_pairs(u_ref[0].reshape(rows, FT_WIDTH // 2)).astype(bf16)
    z = jnp.dot(m_ref[0], u, preferred_element_type=f32)
    z_ref[0] = _pack2(z[:rows], z[rows:]).reshape(FT_N1, SUBLANES, FT_WIDTH)


def _ft_stage2_kernel(z_ref, s2_ref, cs_ref, y_ref):
    gd = FT_GROUP_DIM
    for kk in range(FT_K1_BLOCK):
        zz = jnp.concatenate(_unpack2(z_ref[0, kk]), axis=0).astype(bf16)
        x = jnp.dot(s2_ref[...], zz, preferred_element_type=f32)
        outs = []
        for g in range(FT_GROUPS):
            xg = jnp.concatenate([x[:FT_N2, g * gd:(g + 1) * gd], x[FT_N2:, g * gd:(g + 1) * gd]], axis=1)
            outs.append(jnp.dot(xg.astype(bf16), cs_ref[...], preferred_element_type=f32))
        y_ref[0, kk] = jnp.concatenate(outs, axis=1).astype(y_ref.dtype)


def _fourier_mix(u_packed, tables):
    u = u_packed
    b, s, _ = u.shape
    c = FT_WIDTH
    stage1, stage2, chan = tables
    n_blk = FT_N2 // SUBLANES
    z = pl.pallas_call(
        _ft_stage1_kernel,
        grid=(n_blk, b),
        in_specs=[pl.BlockSpec((1, FT_N1, SUBLANES, c // 2), lambda j, bi: (bi, 0, j, 0)),
                  pl.BlockSpec((1,) + stage1.shape[1:], lambda j, bi: (j, 0, 0))],
        out_specs=pl.BlockSpec((1, FT_N1, SUBLANES, c), lambda j, bi: (bi, 0, j, 0)),
        out_shape=jax.ShapeDtypeStruct((b, FT_N1, FT_N2, c), u32),
        compiler_params=_params("parallel", "parallel"),
        name="fourier_stage1",
    )(u.reshape(b, FT_N1, FT_N2, c // 2), stage1)
    y = pl.pallas_call(
        _ft_stage2_kernel,
        grid=(b, FT_N1 // FT_K1_BLOCK),
        in_specs=[pl.BlockSpec((1, FT_K1_BLOCK, FT_N2, c), lambda bi, kb: (bi, kb, 0, 0)),
                  pl.BlockSpec(stage2.shape, lambda bi, kb: (0, 0)),
                  pl.BlockSpec(chan.shape, lambda bi, kb: (0, 0))],
        out_specs=pl.BlockSpec((1, FT_K1_BLOCK, FT_N2, c), lambda bi, kb: (bi, kb, 0, 0)),
        out_shape=jax.ShapeDtypeStruct((b, FT_N1, FT_N2, c), bf16),
        compiler_params=_params("parallel", "parallel"),
        name="fourier_stage2",
    )(z, stage2, chan)
    return y.transpose(0, 2, 1, 3).reshape(b, s, c)


def _mem_kv_kernel(mem_ref, g_ref, w_ref, k_ref, v_ref):
    mn = _rms_scale(mem_ref[0]) * g_ref[...]
    kv = jnp.dot(mn.astype(bf16), w_ref[...], preferred_element_type=f32)
    k_ref[0] = kv[:, :MEM_WIDTH].astype(bf16)
    v_ref[0] = kv[:, MEM_WIDTH:].astype(bf16)


def _mem_kv(mem, g_mem, w_kv_bf16):
    b, m, d = mem.shape
    kv_spec = pl.BlockSpec((1, m, MEM_WIDTH), lambda bi: (bi, 0, 0))
    return pl.pallas_call(
        _mem_kv_kernel,
        grid=(b,),
        in_specs=[pl.BlockSpec((1, m, d), lambda bi: (bi, 0, 0)), pl.BlockSpec((1, d), lambda bi: (0, 0)),
                  pl.BlockSpec(w_kv_bf16.shape, lambda bi: (0, 0))],
        out_specs=[kv_spec, kv_spec],
        out_shape=[jax.ShapeDtypeStruct((b, m, MEM_WIDTH), bf16)] * 2,
        compiler_params=_params("parallel"),
        name="mem_kv",
    )(mem, g_mem.reshape(1, d), w_kv_bf16)


def _mix_out_kernel(x_ref, yna_ref, yft_ref, qm_ref, km_ref, vm_ref, ggrp_ref, wout_ref, gffn_ref, rw_ref,
                    rb_ref, _order_ref, x1_ref, h2p_ref, eidx_ref, gate_ref, rank_ref, cnt_ref, carry_ref):
    tm = x_ref.shape[0]

    @pl.when(pl.program_id(0) == 0)
    def _():
        carry_ref[...] = jnp.zeros_like(carry_ref)

    q = qm_ref[...]
    km = km_ref[0]
    vm = vm_ref[0]
    heads = []
    for h in range(MEM_HEADS):
        sl = slice(h * MEM_HEAD_DIM, (h + 1) * MEM_HEAD_DIM)
        s = lax.dot_general(q[:, sl], km[:, sl], (((1,), (1,)), ((), ())), preferred_element_type=f32)
        p = _softmax_rows(s * (MEM_HEAD_DIM ** -0.5))
        heads.append(jnp.dot(p.astype(bf16), vm[:, sl], preferred_element_type=f32))
    ymem = jnp.concatenate(heads, axis=1)

    g = ggrp_ref[...]
    a, c = NA_WIDTH, NA_WIDTH + FT_WIDTH
    y = jnp.concatenate([_rms_scale(yna_ref[...].astype(f32)) * g[:, :a],
                         _rms_scale(yft_ref[...].astype(f32)) * g[:, a:c],
                         _rms_scale(ymem) * g[:, c:]], axis=1)
    x1 = x_ref[...] + jnp.dot(y.astype(bf16), wout_ref[...], preferred_element_type=f32)
    x1_ref[...] = x1
    h2 = _rms_scale(x1) * gffn_ref[...]
    h2p_ref[...] = _pack_bf16_pairs(h2)

    h_hi = h2.astype(bf16)
    h_lo = (h2 - h_hi.astype(f32)).astype(bf16)
    hh = jnp.dot(h_hi, rw_ref[...], preferred_element_type=f32)
    logits = (hh[:, :LANES] + hh[:, LANES:]
              + jnp.dot(h_lo, rw_ref[:, :LANES], preferred_element_type=f32)) + rb_ref[...]
    l = logits.T[:N_EXPERTS]
    row = lax.broadcasted_iota(i32, (N_EXPERTS, tm), 0).astype(f32)
    vals, idxs, sels = [], [], []
    for _ in range(TOP_K):
        m = jnp.max(l, axis=0, keepdims=True)
        idx = jnp.min(jnp.where(l == m, row, float(N_EXPERTS)), axis=0, keepdims=True)
        sel = row == idx
        vals.append(m)
        idxs.append(idx)
        sels.append(sel)
        l = jnp.where(sel, -jnp.inf, l)
    ex = [jnp.exp(v - vals[0]) for v in vals]
    den = ex[0] + ex[1] + ex[2] + ex[3]

    onehot = (sels[0] | sels[1] | sels[2] | sels[3]).astype(f32)
    earlier = (lax.broadcasted_iota(i32, (tm, tm), 0) < lax.broadcasted_iota(i32, (tm, tm), 1)).astype(bf16)
    before = jnp.dot(onehot.astype(bf16), earlier, preferred_element_type=f32) + carry_ref[...]
    ranks = [jnp.sum(jnp.where(sel, before, 0.0), axis=0, keepdims=True) for sel in sels]
    carry_ref[...] = carry_ref[...] + jnp.sum(onehot, axis=1, keepdims=True)
    cnt_ref[...] = carry_ref[...].astype(i32)

    eidx_ref[...] = jnp.concatenate(idxs, axis=0).astype(i32)
    rank_ref[...] = jnp.concatenate(ranks, axis=0).astype(i32)
    gates_t = jnp.concatenate([e / den for e in ex] + [jnp.zeros((LANES - TOP_K, tm), f32)], axis=0)
    gate_ref[...] = gates_t.T[:, :TOP_K]


def _mix_out(x2d, y_na, y_ft, q_mem, k_mem, v_mem, g_grp, w_out_bf16, g_ffn, router_w2, router_b, seq,
             tile0, n_tiles, order_after):
    d = x2d.shape[1]
    tm = TOKEN_TILE
    t = n_tiles * tm
    steps_per_batch = seq // tm
    m = k_mem.shape[1]
    row_in = lambda w: pl.BlockSpec((tm, w), lambda i: (i + tile0, 0))
    row_out = lambda w: pl.BlockSpec((tm, w), lambda i: (i, 0))
    full = lambda a: pl.BlockSpec(a.shape, lambda i: (0,) * a.ndim)
    kv_spec = pl.BlockSpec((1, m, MEM_WIDTH), lambda i: ((i + tile0) // steps_per_batch, 0, 0))
    g_grp2, g_ffn2 = g_grp.reshape(1, -1), g_ffn.reshape(1, d)
    rb2 = jnp.pad(router_b.reshape(1, N_EXPERTS), ((0, 0), (0, LANES - N_EXPERTS)))
    router_w2 = jnp.pad(router_w2, ((0, 0), (0, 0), (0, LANES - N_EXPERTS)))
    router_w2 = jnp.concatenate([router_w2[0], router_w2[1]], axis=1)
    col_out = pl.BlockSpec((TOP_K, tm), lambda i: (0, i))
    return pl.pallas_call(
        _mix_out_kernel,
        grid=(n_tiles,),
        in_specs=[row_in(d), row_in(NA_WIDTH), row_in(FT_WIDTH), row_in(MEM_WIDTH), kv_spec, kv_spec,
                  full(g_grp2), full(w_out_bf16), full(g_ffn2), full(router_w2), full(rb2),
                  pl.BlockSpec(memory_space=pl.ANY)],
        out_specs=[row_out(d), row_out(d // 2), col_out, row_out(TOP_K), col_out,
                   pl.BlockSpec((N_EXPERTS, 1), lambda i: (0, 0))],
        out_shape=[jax.ShapeDtypeStruct((t, d), f32), jax.ShapeDtypeStruct((t, d // 2), u32),
                   jax.ShapeDtypeStruct((TOP_K, t), i32), jax.ShapeDtypeStruct((t, TOP_K), f32),
                   jax.ShapeDtypeStruct((TOP_K, t), i32), jax.ShapeDtypeStruct((N_EXPERTS, 1), i32)],
        scratch_shapes=[pltpu.VMEM((N_EXPERTS, 1), f32)],
        compiler_params=_params("arbitrary"),
        name="mix_out_router",
    )(x2d, y_na, y_ft, q_mem, k_mem, v_mem, g_grp2, w_out_bf16, g_ffn2, router_w2, rb2, order_after)


def _sc_mesh():
    return plsc.VectorSubcoreMesh(core_axis_name="c", subcore_axis_name="s",
                                  num_cores=SC_CORES, num_subcores=SC_SUBCORES)


def _dispatch(h2p, dest, n_slots):
    t, w = h2p.shape
    workers = SC_CORES * SC_SUBCORES
    chunk = SC_GATHER_CHUNK
    per = t // workers
    steps = per // chunk
    assert per * workers == t and steps * chunk == per and steps % 2 == 0
    idx = dest.reshape(TOP_K, workers, steps, chunk)

    def body(h_hbm, idx_hbm, out_hbm, idx_v, rows_v, rsem, ssem):
        wid = lax.axis_index("s") * SC_CORES + lax.axis_index("c")
        base = wid * per
        for j in range(TOP_K):
            pltpu.sync_copy(idx_hbm.at[j, wid], idx_v.at[j])

        def read(c, slot):
            return pltpu.make_async_copy(h_hbm.at[pl.ds(base + c * chunk, chunk)], rows_v.at[slot], rsem.at[slot])

        def scatters(c, slot):
            return [pltpu.make_async_copy(rows_v.at[slot], out_hbm.at[idx_v.at[j, c]], ssem.at[slot])
                    for j in range(TOP_K)]

        read(0, 0).start()

        @pl.loop(0, steps, step=2)
        def _(c0):
            for slot in range(2):
                c = c0 + slot
                read(c, slot).wait()
                for cp in scatters(c, slot):
                    cp.start()

                @pl.when(c >= 1)
                def _():
                    for cp in scatters(c - 1, 1 - slot):
                        cp.wait()

                @pl.when(c + 1 < steps)
                def _():
                    read(c + 1, 1 - slot).start()

        for cp in scatters(steps - 1, 1):
            cp.wait()

    return pl.kernel(
        body,
        out_type=jax.ShapeDtypeStruct((n_slots, w), h2p.dtype),
        mesh=_sc_mesh(),
        scratch_types=[pltpu.VMEM((TOP_K, steps, chunk), i32), pltpu.VMEM((2, chunk, w), h2p.dtype),
                       pltpu.SemaphoreType.DMA((2,)), pltpu.SemaphoreType.DMA((2,))],
        name="sc_dispatch_rows",
    )(h2p, idx)


def _expert_kernel(blk_e_ref, blk_cnt_ref, nxt_e_ref, xs_ref, wgu_hbm, bgu_ref, wd_hbm, bd_ref, y_ref,
                   wgu_f32, wd_f32, wgu_bf, wd_bf, sem):
    b = pl.program_id(0)
    e = blk_e_ref[b]
    cnt = blk_cnt_ref[b]
    bm = xs_ref.shape[0]
    de = wd_f32.shape[0]

    def fetch(expert):
        return (pltpu.make_async_copy(wgu_hbm.at[expert], wgu_f32, sem.at[0]),
                pltpu.make_async_copy(wd_hbm.at[expert], wd_f32, sem.at[1]))

    @pl.when(b == 0)
    def _():
        for cp in fetch(e):
            cp.start()

    @pl.when(jnp.logical_or(b == 0, e != blk_e_ref[jnp.maximum(b - 1, 0)]))
    def _():
        for cp in fetch(e):
            cp.wait()

        def convert(i, carry):
            rows = pl.ds(pl.multiple_of(i * WEIGHT_CAST_ROWS, WEIGHT_CAST_ROWS), WEIGHT_CAST_ROWS)
            wgu_bf[rows, :] = wgu_f32[rows, :].astype(bf16)
            wd_bf[rows, :] = wd_f32[rows, :].astype(bf16)
            return carry

        lax.fori_loop(0, wgu_f32.shape[0] // WEIGHT_CAST_ROWS, convert, 0)

        @pl.when(nxt_e_ref[b] >= 0)
        def _():
            for cp in fetch(nxt_e_ref[b]):
                cp.start()

    def ffn_rows(r0, n):
        rows = pl.ds(r0, n)
        valid = r0 + lax.broadcasted_iota(i32, (n, 1), 0) < cnt
        x = jnp.where(valid, _unpack_bf16_pairs(xs_ref[rows, :]), 0.0).astype(bf16)
        gu = jnp.dot(x, wgu_bf[...], preferred_element_type=f32) + bgu_ref[0]
        x_glu = jnp.minimum(gu[:, :de], SWIGLU_LIMIT)
        x_lin = jnp.clip(gu[:, de:], -SWIGLU_LIMIT, SWIGLU_LIMIT)
        act = x_glu * (1.0 / (1.0 + jnp.exp(-SWIGLU_ALPHA * x_glu))) * (x_lin + 1.0)
        y = jnp.dot(act.astype(bf16), wd_bf[...], preferred_element_type=f32) + bd_ref[0]
        y_ref[rows, :] = _pack_bf16_pairs(y)

    @pl.when(cnt == bm)
    def _():
        ffn_rows(0, bm)

    @pl.when(cnt < bm)
    def _():
        y_ref[...] = jnp.zeros_like(y_ref)

        def piece(i, carry):
            ffn_rows(pl.multiple_of(i * MOE_SUB_BLOCK, MOE_SUB_BLOCK), MOE_SUB_BLOCK)
            return carry

        lax.fori_loop(0, (cnt + MOE_SUB_BLOCK - 1) // MOE_SUB_BLOCK, piece, 0)


def _experts(xs, blk_e, blk_cnt, nxt_e, w_gu, b_gu, w_down, b_down):
    n_slots, w = xs.shape
    bm = MOE_BLOCK
    e, d, de2 = w_gu.shape
    de = w_down.shape[1]
    assert de == d, "one row loop converts both weight matrices"
    last = n_slots // bm - 1

    def rows_of_block(b, be, bc, ne):
        return (jnp.where(bc[b] > 0, b, last), 0)

    grid_spec = pltpu.PrefetchScalarGridSpec(
        num_scalar_prefetch=3,
        grid=(n_slots // bm,),
        in_specs=[pl.BlockSpec((bm, w), rows_of_block),
                  pl.BlockSpec(memory_space=pl.ANY),
                  pl.BlockSpec((1, 1, de2), lambda b, be, bc, ne: (be[b], 0, 0)),
                  pl.BlockSpec(memory_space=pl.ANY),
                  pl.BlockSpec((1, 1, d), lambda b, be, bc, ne: (be[b], 0, 0))],
        out_specs=pl.BlockSpec((bm, w), rows_of_block),
        scratch_shapes=[pltpu.VMEM((d, de2), f32), pltpu.VMEM((de, d), f32),
                        pltpu.VMEM((d, de2), bf16), pltpu.VMEM((de, d), bf16),
                        pltpu.SemaphoreType.DMA((2,))],
    )
    return pl.pallas_call(
        _expert_kernel,
        grid_spec=grid_spec,
        out_shape=jax.ShapeDtypeStruct((n_slots, w), u32),
        compiler_params=_params("arbitrary"),
        name="moe_experts",
    )(blk_e, blk_cnt, nxt_e, xs, w_gu, b_gu.reshape(e, 1, de2), w_down, b_down.reshape(e, 1, d))


def _sc_gather_rows(table, idx):
    n, w = idx.shape[0], table.shape[1]
    workers = SC_CORES * SC_SUBCORES
    chunk = SC_GATHER_CHUNK
    per = n // workers
    steps = per // chunk
    assert per * workers == n and steps * chunk == per and steps % 2 == 0

    def body(table_hbm, idx_hbm, out_hbm, idx_v, rows_v, gsem, wsem):
        base = (lax.axis_index("s") * SC_CORES + lax.axis_index("c")) * per
        pltpu.sync_copy(idx_hbm.at[pl.ds(base, per)], idx_v)

        def gather(c, slot):
            return pltpu.make_async_copy(table_hbm.at[idx_v.at[pl.ds(c * chunk, chunk)]], rows_v.at[slot],
                                         gsem.at[slot])

        def write(c, slot):
            return pltpu.make_async_copy(rows_v.at[slot], out_hbm.at[pl.ds(base + c * chunk, chunk)],
                                         wsem.at[slot])

        gather(0, 0).start()

        @pl.loop(0, steps, step=2)
        def _(c0):
            for slot in range(2):
                c = c0 + slot
                gather(c, slot).wait()
                write(c, slot).start()

                @pl.when(c >= 1)
                def _():
                    write(c - 1, 1 - slot).wait()

                @pl.when(c + 1 < steps)
                def _():
                    gather(c + 1, 1 - slot).start()

        write(steps - 1, 1).wait()

    return pl.kernel(
        body,
        out_type=jax.ShapeDtypeStruct((n, w), table.dtype),
        mesh=_sc_mesh(),
        scratch_types=[pltpu.VMEM((per,), i32), pltpu.VMEM((2, chunk, w), table.dtype),
                       pltpu.SemaphoreType.DMA((2,)), pltpu.SemaphoreType.DMA((2,))],
        name="sc_gather_rows",
    )(table, idx)


def _combine_kernel(x1_ref, gate_ref, gfin_ref, yg_ref, *rest):
    o_ref = rest[-1]
    acc = x1_ref[...]
    gates = gate_ref[...]
    for j in range(TOP_K):
        acc = acc + gates[:, j:j + 1] * _unpack_bf16_pairs(yg_ref[j])
    o_ref[...] = _rms_scale(acc) * gfin_ref[...]


def _combine(x1, gates, dest, y_slots, g_final, out_prev, tile0, total_tokens):
    t, d = x1.shape
    tm = TOKEN_TILE
    w = y_slots.shape[1]
    yg = _sc_gather_rows(y_slots, dest.reshape(-1)).reshape(TOP_K, t, w)
    in_specs = [pl.BlockSpec((tm, d), lambda i: (i, 0)),
                pl.BlockSpec((tm, TOP_K), lambda i: (i, 0)),
                pl.BlockSpec((1, d), lambda i: (0, 0)),
                pl.BlockSpec((TOP_K, tm, w), lambda i: (0, i, 0))]
    args = [x1, gates, g_final.reshape(1, d), yg]
    aliases = {}
    if out_prev is not None:
        in_specs.append(pl.BlockSpec(memory_space=pl.ANY))
        args.append(out_prev)
        aliases = {len(args) - 1: 0}
    return pl.pallas_call(
        _combine_kernel,
        grid=(t // tm,),
        in_specs=in_specs,
        out_specs=pl.BlockSpec((tm, d), lambda i: (i + tile0, 0)),
        out_shape=jax.ShapeDtypeStruct((total_tokens, d), f32),
        input_output_aliases=aliases,
        compiler_params=_params("parallel"),
        name="moe_combine",
    )(*args)


def _dest_kernel(start_ref, eidx_ref, rank_ref, dest_ref):
    eidx = eidx_ref[...]
    dest = rank_ref[...]
    for e in range(N_EXPERTS):
        dest = dest + jnp.where(eidx == e, start_ref[e], 0)
    dest_ref[...] = dest


def _slot_layout(counts, eidx, rank, n_blocks):
    bm = MOE_BLOCK
    padded = (counts + bm - 1) // bm * bm
    padded_end = jnp.cumsum(padded)
    start = padded_end - padded
    experts = jnp.arange(N_EXPERTS, dtype=i32)
    lookup = lambda table, idx: jnp.sum(jnp.where(idx[..., None] == experts, table, 0), axis=-1)
    dest = pl.pallas_call(
        _dest_kernel,
        grid_spec=pltpu.PrefetchScalarGridSpec(
            num_scalar_prefetch=1, grid=(1,),
            in_specs=[pl.BlockSpec(eidx.shape, lambda i, st: (0, 0)), pl.BlockSpec(rank.shape, lambda i, st: (0, 0))],
            out_specs=pl.BlockSpec(rank.shape, lambda i, st: (0, 0))),
        out_shape=jax.ShapeDtypeStruct(rank.shape, i32),
        name="moe_dest",
    )(start.astype(i32), eidx, rank)
    blk_row = jnp.arange(n_blocks, dtype=i32) * bm
    blk_e = jnp.minimum(jnp.sum((padded_end[None, :] <= blk_row[:, None]).astype(i32), axis=1), N_EXPERTS - 1)
    blk_cnt = jnp.clip(lookup(counts, blk_e) - (blk_row - lookup(start, blk_e)), 0, bm).astype(i32)
    none = jnp.int32(N_EXPERTS)
    nxt_e = jnp.min(jnp.where(blk_e[None, :] > blk_e[:, None], blk_e[None, :], none), axis=1)
    nxt_e = jnp.where(nxt_e == none, -1, nxt_e).astype(i32)
    return dest, blk_e, blk_cnt, nxt_e


def _layer_and_final_norm(x2d, mem, seq, g_mix, g_mem, w_in, w_mem_kv, na_rel_bias, g_grp, w_out, g_ffn,
                          router_w, router_b, w_gu, b_gu, w_down, b_down, g_final):
    t, d = x2d.shape
    b = t // seq
    q_na, k_na, v_na, u_ft, q_mem = _in_proj(x2d, g_mix, w_in.astype(bf16))
    k_mem, v_mem = _mem_kv(mem, g_mem, w_mem_kv.astype(bf16))
    shape3 = lambda a: a.reshape(b, seq, a.shape[-1])
    y_na = _neighbourhood_attention(shape3(q_na), shape3(k_na), shape3(v_na), _na_bias_table(na_rel_bias))
    y_ft = _fourier_mix(shape3(u_ft), _ft_tables(seq))
    y_na, y_ft = y_na.reshape(t, -1), y_ft.reshape(t, -1)
    w_out_bf16 = w_out.astype(bf16)
    rw_hi = router_w.astype(bf16)
    router_w2 = jnp.stack([rw_hi, (router_w - rw_hi.astype(f32)).astype(bf16)])

    tiles = t // TOKEN_TILE
    unit = tiles // sum(MOE_GROUP_SHARES)
    assert unit * sum(MOE_GROUP_SHARES) == tiles
    out = None
    tile0 = 0
    dest = router_b
    for share in MOE_GROUP_SHARES:
        group_tiles = share * unit
        n_blocks = (group_tiles * TOKEN_TILE * TOP_K) // MOE_BLOCK + N_EXPERTS
        x1, h2p, eidx, gates, rank, counts = _mix_out(
            x2d, y_na, y_ft, q_mem, k_mem, v_mem, g_grp, w_out_bf16, g_ffn, router_w2, router_b, seq,
            tile0, group_tiles, dest)
        dest, blk_e, blk_cnt, nxt_e = _slot_layout(counts[:, 0], eidx, rank, n_blocks)
        xs = _dispatch(h2p, dest, n_blocks * MOE_BLOCK)
        y_slots = _experts(xs, blk_e, blk_cnt, nxt_e, w_gu, b_gu, w_down, b_down)
        out = _combine(x1, gates, dest, y_slots, g_final, out, tile0, t)
        tile0 += group_tiles
    return out


def kernel(x, mem, g_mix, g_mem, w_in, w_mem_kv, na_rel_bias, g_grp, w_out, g_ffn, router_w, router_b,
           w_gu, b_gu, w_down, b_down, g_final):
    b, seq, d = x.shape
    depth = w_in.shape[0]
    assert depth == 1, "the final norm is fused into the single layer's combine step"
    out = _layer_and_final_norm(
        x.reshape(b * seq, d), mem, seq, g_mix[0], g_mem[0], w_in[0], w_mem_kv[0], na_rel_bias[0], g_grp[0],
        w_out[0], g_ffn[0], router_w[0], router_b[0], w_gu[0], b_gu[0], w_down[0], b_down[0], g_final)
    return out.reshape(b, seq, d)
```

```python
import numpy as np
import jax
import jax.numpy as jnp
from jax import lax
from jax.experimental import pallas as pl
from jax.experimental.pallas import tpu as pltpu
from jax.experimental.pallas import tpu_sc as plsc

f32 = jnp.float32
bf16 = jnp.bfloat16
u32 = jnp.uint32
i32 = jnp.int32

GRID_W = 64
NA_HEADS = 8
NA_HEAD_DIM = 64
NA_WIN_ROWS = 8
NA_WIN_COLS = 16
FT_GROUPS = 4
FT_GROUP_DIM = 128
MEM_HEADS = 4
MEM_HEAD_DIM = 128
NA_WIDTH = NA_HEADS * NA_HEAD_DIM
FT_WIDTH = FT_GROUPS * FT_GROUP_DIM
MEM_WIDTH = MEM_HEADS * MEM_HEAD_DIM
N_EXPERTS = 32
TOP_K = 4
SWIGLU_LIMIT = 7.0
SWIGLU_ALPHA = 1.702
EPS = 1e-6

LANES = 128
SUBLANES = 8
VMEM_LIMIT_BYTES = 56 * 1024 * 1024
SC_CORES = 2
SC_SUBCORES = 16
SC_GATHER_CHUNK = 64

TOKEN_TILE = 1024
IN_PROJ_TILE = 1024
MOE_BLOCK = 512
MOE_SUB_BLOCK = 128
WEIGHT_CAST_ROWS = 128
MOE_GROUP_SHARES = (5, 3)
NA_ROW_UNROLL = 16
FT_N1 = 64
FT_N2 = 128
FT_K1_BLOCK = 16
FT_BATCH_BLOCK = 2
LOG2_E = 1.4426950408889634
MASK_VALUE = -jnp.inf


def _params(*semantics):
    return pltpu.CompilerParams(dimension_semantics=semantics, vmem_limit_bytes=VMEM_LIMIT_BYTES)


def _rms_scale(x):
    return x * lax.rsqrt(jnp.mean(x * x, axis=-1, keepdims=True) + EPS)


def _softmax_rows(s):
    p = jnp.exp(s - jnp.max(s, axis=-1, keepdims=True))
    return p / jnp.sum(p, axis=-1, keepdims=True)


def _bf16_bits(x):
    return pltpu.bitcast(x.astype(bf16).astype(f32), u32)


def _pack2(lo, hi):
    return (_bf16_bits(lo) >> 16) | (_bf16_bits(hi) & jnp.uint32(0xFFFF0000))


def _unpack2(w):
    return pltpu.bitcast(w << 16, f32), pltpu.bitcast(w & jnp.uint32(0xFFFF0000), f32)


def _pack_bf16_pairs(x):
    n = x.shape[1] // 2
    return _pack2(x[:, :n], x[:, n:])


def _unpack_bf16_pairs(w):
    return jnp.concatenate(_unpack2(w), axis=1)


def _in_proj_kernel(x_ref, g_ref, w_ref, qna_ref, kna_ref, vna_ref, uft_ref, qmem_ref):
    h = _rms_scale(x_ref[...]) * g_ref[...]
    proj = jnp.dot(h.astype(bf16), w_ref[...], preferred_element_type=f32)
    o = NA_WIDTH
    qna_ref[...] = (proj[:, :o] * (NA_HEAD_DIM ** -0.5 * LOG2_E)).astype(bf16)
    kna_ref[...] = proj[:, o:2 * o].astype(bf16)
    vna_ref[...] = proj[:, 2 * o:3 * o].astype(bf16)
    uft_ref[...] = _pack_bf16_pairs(proj[:, 3 * o:3 * o + FT_WIDTH])
    qmem_ref[...] = proj[:, 3 * o + FT_WIDTH:].astype(bf16)


def _in_proj(x2d, g_mix, w_in_bf16):
    t, d = x2d.shape
    tm = IN_PROJ_TILE
    row = lambda w: pl.BlockSpec((tm, w), lambda i: (i, 0))
    return pl.pallas_call(
        _in_proj_kernel,
        grid=(t // tm,),
        in_specs=[row(d), pl.BlockSpec((1, d), lambda i: (0, 0)),
                  pl.BlockSpec(w_in_bf16.shape, lambda i: (0, 0))],
        out_specs=[row(NA_WIDTH), row(NA_WIDTH), row(NA_WIDTH), row(FT_WIDTH // 2), row(MEM_WIDTH)],
        out_shape=[jax.ShapeDtypeStruct((t, NA_WIDTH), bf16)] * 3
        + [jax.ShapeDtypeStruct((t, FT_WIDTH // 2), u32), jax.ShapeDtypeStruct((t, MEM_WIDTH), bf16)],
        compiler_params=_params("parallel"),
        name="in_proj",
    )(x2d, g_mix.reshape(1, d), w_in_bf16)


def _na_bias_table(rel_bias):
    c = np.arange(GRID_W)
    dc_idx = np.clip(c[None, :] - c[:, None], -(NA_WIN_COLS - 1), NA_WIN_COLS - 1) + (NA_WIN_COLS - 1)
    col_start = np.clip(c - NA_WIN_COLS // 2, 0, GRID_W - NA_WIN_COLS)
    col_in = (c[None, :] >= col_start[:, None]) & (c[None, :] < col_start[:, None] + NA_WIN_COLS)
    pick_c = jnp.asarray(dc_idx[:, :, None] == np.arange(2 * NA_WIN_COLS - 1), f32)
    cols = jnp.einsum("hab,qcb->haqc", rel_bias.astype(f32), pick_c, precision=lax.Precision.HIGHEST)
    cols = jnp.where(col_in[None, None], cols * LOG2_E, MASK_VALUE)
    tab = jnp.stack([jnp.concatenate([cols[:, j - s + NA_WIN_ROWS - 1] for j in range(NA_WIN_ROWS)], axis=-1)
                     for s in range(NA_WIN_ROWS)])
    return tab.reshape(NA_WIN_ROWS, NA_HEADS // 2, 2 * GRID_W, NA_WIN_ROWS * GRID_W)


def _na_kernel(q_ref, k_ref, v_ref, bias_ref, o_ref):
    rows = q_ref.shape[1] // GRID_W
    win = NA_WIN_ROWS * GRID_W
    first_head = lax.broadcasted_iota(i32, (GRID_W, 2 * NA_HEAD_DIM), 1) < NA_HEAD_DIM

    def body(it, carry):
        scores, values, q_offsets = [], [], []
        for u in range(NA_ROW_UNROLL):
            r = it * NA_ROW_UNROLL + u
            row_start = jnp.clip(r - NA_WIN_ROWS // 2, 0, rows - NA_WIN_ROWS)
            q0 = pl.multiple_of(r * GRID_W, GRID_W)
            k0 = pl.multiple_of(row_start * GRID_W, GRID_W)
            q = q_ref[0, pl.ds(q0, GRID_W), :]
            zero = jnp.zeros_like(q)
            qm = jnp.concatenate([jnp.where(first_head, q, zero), jnp.where(first_head, zero, q)], axis=0)
            s = lax.dot_general(qm, k_ref[0, pl.ds(k0, win), :], (((1,), (1,)), ((), ())),
                                preferred_element_type=f32)
            scores.append(s + bias_ref[r - row_start, 0])
            values.append(v_ref[0, pl.ds(k0, win), :])
            q_offsets.append(q0)
        s = jnp.concatenate(scores, axis=0)
        p = jnp.exp2(s - jnp.max(s, axis=-1, keepdims=True))
        inv_den = 1.0 / jnp.sum(p, axis=-1, keepdims=True)
        p = p.astype(bf16)
        for u in range(NA_ROW_UNROLL):
            sl = slice(u * 2 * GRID_W, (u + 1) * 2 * GRID_W)
            o = jnp.dot(p[sl], values[u], preferred_element_type=f32) * inv_den[sl]
            o_ref[0, pl.ds(q_offsets[u], GRID_W), :] = jnp.where(
                first_head, o[:GRID_W], o[GRID_W:]).astype(o_ref.dtype)
        return carry

    lax.fori_loop(0, rows // NA_ROW_UNROLL, body, 0)


def _neighbourhood_attention(q, k, v, bias_tab):
    b, s, _ = q.shape
    pair = 2 * NA_HEAD_DIM
    qkv_spec = pl.BlockSpec((1, s, pair), lambda bi, hp: (bi, 0, hp))
    return pl.pallas_call(
        _na_kernel,
        grid=(b, NA_HEADS // 2),
        in_specs=[qkv_spec, qkv_spec, qkv_spec,
                  pl.BlockSpec((NA_WIN_ROWS, 1, 2 * GRID_W, NA_WIN_ROWS * GRID_W), lambda bi, hp: (0, hp, 0, 0))],
        out_specs=qkv_spec,
        out_shape=jax.ShapeDtypeStruct((b, s, NA_WIDTH), bf16),
        compiler_params=_params("parallel", "parallel"),
        name="neighbourhood_attention",
    )(q, k, v, bias_tab)


def _ft_tables(seq):
    assert seq == FT_N1 * FT_N2
    n_blk = FT_N2 // SUBLANES
    k1 = np.arange(FT_N1)[:, None, None, None]
    sr = np.arange(SUBLANES)[None, :, None, None]
    n1 = np.arange(FT_N1)[None, None, :, None]
    sc = np.arange(SUBLANES)[None, None, None, :]
    stage1 = np.zeros((n_blk, 2, FT_N1, SUBLANES, FT_N1, SUBLANES), np.float64)
    for blk in range(n_blk):
        n = FT_N2 * n1 + SUBLANES * blk + sr
        ang = 2.0 * np.pi * ((k1 * n) % seq) / seq
        eye = (sr == sc)
        stage1[blk, 0] = np.cos(ang) * eye
        stage1[blk, 1] = -np.sin(ang) * eye
    stage1 = stage1.reshape(n_blk, 2 * FT_N1 * SUBLANES, FT_N1 * SUBLANES)
    a = np.arange(FT_N2)
    ang2 = 2.0 * np.pi * ((a[:, None] * a[None, :]) % FT_N2) / FT_N2
    c2, s2 = np.cos(ang2), np.sin(ang2)
    stage2 = np.block([[c2, s2], [-s2, c2]])
    g = np.arange(FT_GROUP_DIM)
    angc = 2.0 * np.pi * ((g[:, None] * g[None, :]) % FT_GROUP_DIM) / FT_GROUP_DIM
    norm = 1.0 / np.sqrt(seq * FT_GROUP_DIM)
    chan = np.concatenate([np.cos(angc), np.sin(angc)], axis=0) * norm
    return (jnp.asarray(stage1, bf16), jnp.asarray(stage2, bf16), jnp.asarray(chan, bf16))


def _ft_stage1_kernel(u_ref, m_ref, z_ref):
    rows = FT_N1 * SUBLANES
    nb = u_ref.shape[0]
    u = jnp.concatenate([_unpack_bf16_pairs(u_ref[bb].reshape(rows, FT_WIDTH // 2)) for bb in range(nb)],
                        axis=1).astype(bf16)
    z = jnp.dot(m_ref[0], u, preferred_element_type=f32)
    for bb in range(nb):
        zb = z[:, bb * FT_WIDTH:(bb + 1) * FT_WIDTH]
        z_ref[bb] = _pack2(zb[:rows], zb[rows:]).reshape(FT_N1, SUBLANES, FT_WIDTH)


def _ft_stage2_kernel(z_ref, s2_ref, cs_ref, y_ref):
    gd = FT_GROUP_DIM
    for kk in range(FT_K1_BLOCK):
        zz = jnp.concatenate(_unpack2(z_ref[0, kk]), axis=0).astype(bf16)
        x = jnp.dot(s2_ref[...], zz, preferred_element_type=f32)
        outs = []
        for g in range(FT_GROUPS):
            xg = jnp.concatenate([x[:FT_N2, g * gd:(g + 1) * gd], x[FT_N2:, g * gd:(g + 1) * gd]], axis=1)
            outs.append(jnp.dot(xg.astype(bf16), cs_ref[...], preferred_element_type=f32))
        y_ref[0, kk] = jnp.concatenate(outs, axis=1).astype(y_ref.dtype)


def _fourier_mix(u_packed, tables):
    u = u_packed
    b, s, _ = u.shape
    c = FT_WIDTH
    stage1, stage2, chan = tables
    n_blk = FT_N2 // SUBLANES
    z = pl.pallas_call(
        _ft_stage1_kernel,
        grid=(n_blk, b // FT_BATCH_BLOCK),
        in_specs=[pl.BlockSpec((FT_BATCH_BLOCK, FT_N1, SUBLANES, c // 2), lambda j, bi: (bi, 0, j, 0)),
                  pl.BlockSpec((1,) + stage1.shape[1:], lambda j, bi: (j, 0, 0))],
        out_specs=pl.BlockSpec((FT_BATCH_BLOCK, FT_N1, SUBLANES, c), lambda j, bi: (bi, 0, j, 0)),
        out_shape=jax.ShapeDtypeStruct((b, FT_N1, FT_N2, c), u32),
        compiler_params=_params("parallel", "parallel"),
        name="fourier_stage1",
    )(u.reshape(b, FT_N1, FT_N2, c // 2), stage1)
    y = pl.pallas_call(
        _ft_stage2_kernel,
        grid=(b, FT_N1 // FT_K1_BLOCK),
        in_specs=[pl.BlockSpec((1, FT_K1_BLOCK, FT_N2, c), lambda bi, kb: (bi, kb, 0, 0)),
                  pl.BlockSpec(stage2.shape, lambda bi, kb: (0, 0)),
                  pl.BlockSpec(chan.shape, lambda bi, kb: (0, 0))],
        out_specs=pl.BlockSpec((1, FT_K1_BLOCK, FT_N2, c), lambda bi, kb: (bi, kb, 0, 0)),
        out_shape=jax.ShapeDtypeStruct((b, FT_N1, FT_N2, c), bf16),
        compiler_params=_params("parallel", "parallel"),
        name="fourier_stage2",
    )(z, stage2, chan)
    return y.transpose(0, 2, 1, 3).reshape(b, s, c)


def _mem_kv_kernel(mem_ref, g_ref, w_ref, k_ref, v_ref):
    mn = _rms_scale(mem_ref[0]) * g_ref[...]
    kv = jnp.dot(mn.astype(bf16), w_ref[...], preferred_element_type=f32)
    k_ref[0] = kv[:, :MEM_WIDTH].astype(bf16)
    v_ref[0] = kv[:, MEM_WIDTH:].astype(bf16)


def _mem_kv(mem, g_mem, w_kv_bf16):
    b, m, d = mem.shape
    kv_spec = pl.BlockSpec((1, m, MEM_WIDTH), lambda bi: (bi, 0, 0))
    return pl.pallas_call(
        _mem_kv_kernel,
        grid=(b,),
        in_specs=[pl.BlockSpec((1, m, d), lambda bi: (bi, 0, 0)), pl.BlockSpec((1, d), lambda bi: (0, 0)),
                  pl.BlockSpec(w_kv_bf16.shape, lambda bi: (0, 0))],
        out_specs=[kv_spec, kv_spec],
        out_shape=[jax.ShapeDtypeStruct((b, m, MEM_WIDTH), bf16)] * 2,
        compiler_params=_params("parallel"),
        name="mem_kv",
    )(mem, g_mem.reshape(1, d), w_kv_bf16)


def _mix_out_kernel(x_ref, yna_ref, yft_ref, qm_ref, km_ref, vm_ref, ggrp_ref, wout_ref, gffn_ref, rw_ref,
                    rb_ref, _order_ref, x1_ref, h2p_ref, eidx_ref, gate_ref, rank_ref, cnt_ref, carry_ref):
    tm = x_ref.shape[0]

    @pl.when(pl.program_id(0) == 0)
    def _():
        carry_ref[...] = jnp.zeros_like(carry_ref)

    q = qm_ref[...]
    km = km_ref[0]
    vm = vm_ref[0]
    heads = []
    for h in range(MEM_HEADS):
        sl = slice(h * MEM_HEAD_DIM, (h + 1) * MEM_HEAD_DIM)
        s = lax.dot_general(q[:, sl], km[:, sl], (((1,), (1,)), ((), ())), preferred_element_type=f32)
        p = _softmax_rows(s * (MEM_HEAD_DIM ** -0.5))
        heads.append(jnp.dot(p.astype(bf16), vm[:, sl], preferred_element_type=f32))
    ymem = jnp.concatenate(heads, axis=1)

    g = ggrp_ref[...]
    a, c = NA_WIDTH, NA_WIDTH + FT_WIDTH
    y = jnp.concatenate([_rms_scale(yna_ref[...].astype(f32)) * g[:, :a],
                         _rms_scale(yft_ref[...].astype(f32)) * g[:, a:c],
                         _rms_scale(ymem) * g[:, c:]], axis=1)
    x1 = x_ref[...] + jnp.dot(y.astype(bf16), wout_ref[...], preferred_element_type=f32)
    x1_ref[...] = x1
    h2 = _rms_scale(x1) * gffn_ref[...]
    h2p_ref[...] = _pack_bf16_pairs(h2)

    h_hi = h2.astype(bf16)
    h_lo = (h2 - h_hi.astype(f32)).astype(bf16)
    hh = jnp.dot(h_hi, rw_ref[...], preferred_element_type=f32)
    logits = (hh[:, :LANES] + hh[:, LANES:]
              + jnp.dot(h_lo, rw_ref[:, :LANES], preferred_element_type=f32)) + rb_ref[...]
    l = logits.T[:N_EXPERTS]
    row = lax.broadcasted_iota(i32, (N_EXPERTS, tm), 0).astype(f32)
    vals, idxs, sels = [], [], []
    for _ in range(TOP_K):
        m = jnp.max(l, axis=0, keepdims=True)
        idx = jnp.min(jnp.where(l == m, row, float(N_EXPERTS)), axis=0, keepdims=True)
        sel = row == idx
        vals.append(m)
        idxs.append(idx)
        sels.append(sel)
        l = jnp.where(sel, -jnp.inf, l)
    ex = [jnp.exp(v - vals[0]) for v in vals]
    den = ex[0] + ex[1] + ex[2] + ex[3]

    onehot = (sels[0] | sels[1] | sels[2] | sels[3]).astype(f32)
    earlier = (lax.broadcasted_iota(i32, (tm, tm), 0) < lax.broadcasted_iota(i32, (tm, tm), 1)).astype(bf16)
    before = jnp.dot(onehot.astype(bf16), earlier, preferred_element_type=f32) + carry_ref[...]
    ranks = [jnp.sum(jnp.where(sel, before, 0.0), axis=0, keepdims=True) for sel in sels]
    carry_ref[...] = carry_ref[...] + jnp.sum(onehot, axis=1, keepdims=True)
    cnt_ref[...] = carry_ref[...].astype(i32)

    eidx_ref[...] = jnp.concatenate(idxs, axis=0).astype(i32)
    rank_ref[...] = jnp.concatenate(ranks, axis=0).astype(i32)
    gates_t = jnp.concatenate([e / den for e in ex] + [jnp.zeros((LANES - TOP_K, tm), f32)], axis=0)
    gate_ref[...] = gates_t.T[:, :TOP_K]


def _mix_out(x2d, y_na, y_ft, q_mem, k_mem, v_mem, g_grp, w_out_bf16, g_ffn, router_w2, router_b, seq,
             tile0, n_tiles, order_after):
    d = x2d.shape[1]
    tm = TOKEN_TILE
    t = n_tiles * tm
    steps_per_batch = seq // tm
    m = k_mem.shape[1]
    row_in = lambda w: pl.BlockSpec((tm, w), lambda i: (i + tile0, 0))
    row_out = lambda w: pl.BlockSpec((tm, w), lambda i: (i, 0))
    full = lambda a: pl.BlockSpec(a.shape, lambda i: (0,) * a.ndim)
    kv_spec = pl.BlockSpec((1, m, MEM_WIDTH), lambda i: ((i + tile0) // steps_per_batch, 0, 0))
    g_grp2, g_ffn2 = g_grp.reshape(1, -1), g_ffn.reshape(1, d)
    rb2 = jnp.pad(router_b.reshape(1, N_EXPERTS), ((0, 0), (0, LANES - N_EXPERTS)))
    router_w2 = jnp.pad(router_w2, ((0, 0), (0, 0), (0, LANES - N_EXPERTS)))
    router_w2 = jnp.concatenate([router_w2[0], router_w2[1]], axis=1)
    col_out = pl.BlockSpec((TOP_K, tm), lambda i: (0, i))
    return pl.pallas_call(
        _mix_out_kernel,
        grid=(n_tiles,),
        in_specs=[row_in(d), row_in(NA_WIDTH), row_in(FT_WIDTH), row_in(MEM_WIDTH), kv_spec, kv_spec,
                  full(g_grp2), full(w_out_bf16), full(g_ffn2), full(router_w2), full(rb2),
                  pl.BlockSpec(memory_space=pl.ANY)],
        out_specs=[row_out(d), row_out(d // 2), col_out, row_out(TOP_K), col_out,
                   pl.BlockSpec((N_EXPERTS, 1), lambda i: (0, 0))],
        out_shape=[jax.ShapeDtypeStruct((t, d), f32), jax.ShapeDtypeStruct((t, d // 2), u32),
                   jax.ShapeDtypeStruct((TOP_K, t), i32), jax.ShapeDtypeStruct((t, TOP_K), f32),
                   jax.ShapeDtypeStruct((TOP_K, t), i32), jax.ShapeDtypeStruct((N_EXPERTS, 1), i32)],
        scratch_shapes=[pltpu.VMEM((N_EXPERTS, 1), f32)],
        compiler_params=_params("arbitrary"),
        name="mix_out_router",
    )(x2d, y_na, y_ft, q_mem, k_mem, v_mem, g_grp2, w_out_bf16, g_ffn2, router_w2, rb2, order_after)


def _sc_mesh():
    return plsc.VectorSubcoreMesh(core_axis_name="c", subcore_axis_name="s",
                                  num_cores=SC_CORES, num_subcores=SC_SUBCORES)


def _dispatch(h2p, dest, n_slots):
    t, w = h2p.shape
    workers = SC_CORES * SC_SUBCORES
    chunk = SC_GATHER_CHUNK
    per = t // workers
    steps = per // chunk
    assert per * workers == t and steps * chunk == per and steps % 2 == 0
    idx = dest.reshape(TOP_K, workers, steps, chunk)

    def body(h_hbm, idx_hbm, out_hbm, idx_v, rows_v, rsem, ssem):
        wid = lax.axis_index("s") * SC_CORES + lax.axis_index("c")
        base = wid * per
        for j in range(TOP_K):
            pltpu.sync_copy(idx_hbm.at[j, wid], idx_v.at[j])

        def read(c, slot):
            return pltpu.make_async_copy(h_hbm.at[pl.ds(base + c * chunk, chunk)], rows_v.at[slot], rsem.at[slot])

        def scatters(c, slot):
            return [pltpu.make_async_copy(rows_v.at[slot], out_hbm.at[idx_v.at[j, c]], ssem.at[slot])
                    for j in range(TOP_K)]

        read(0, 0).start()

        @pl.loop(0, steps, step=2)
        def _(c0):
            for slot in range(2):
                c = c0 + slot
                read(c, slot).wait()
                for cp in scatters(c, slot):
                    cp.start()

                @pl.when(c >= 1)
                def _():
                    for cp in scatters(c - 1, 1 - slot):
                        cp.wait()

                @pl.when(c + 1 < steps)
                def _():
                    read(c + 1, 1 - slot).start()

        for cp in scatters(steps - 1, 1):
            cp.wait()

    return pl.kernel(
        body,
        out_type=jax.ShapeDtypeStruct((n_slots, w), h2p.dtype),
        mesh=_sc_mesh(),
        scratch_types=[pltpu.VMEM((TOP_K, steps, chunk), i32), pltpu.VMEM((2, chunk, w), h2p.dtype),
                       pltpu.SemaphoreType.DMA((2,)), pltpu.SemaphoreType.DMA((2,))],
        name="sc_dispatch_rows",
    )(h2p, idx)


def _expert_kernel(blk_e_ref, blk_cnt_ref, nxt_e_ref, xs_ref, wgu_hbm, bgu_ref, wd_hbm, bd_ref, y_ref,
                   wgu_f32, wd_f32, wgu_bf, wd_bf, sem):
    b = pl.program_id(0)
    e = blk_e_ref[b]
    cnt = blk_cnt_ref[b]
    bm = xs_ref.shape[0]
    de = wd_f32.shape[0]

    def fetch(expert):
        return (pltpu.make_async_copy(wgu_hbm.at[expert], wgu_f32, sem.at[0]),
                pltpu.make_async_copy(wd_hbm.at[expert], wd_f32, sem.at[1]))

    @pl.when(b == 0)
    def _():
        for cp in fetch(e):
            cp.start()

    @pl.when(jnp.logical_or(b == 0, e != blk_e_ref[jnp.maximum(b - 1, 0)]))
    def _():
        for cp in fetch(e):
            cp.wait()

        def convert(i, carry):
            rows = pl.ds(pl.multiple_of(i * WEIGHT_CAST_ROWS, WEIGHT_CAST_ROWS), WEIGHT_CAST_ROWS)
            wgu_bf[rows, :] = wgu_f32[rows, :].astype(bf16)
            wd_bf[rows, :] = wd_f32[rows, :].astype(bf16)
            return carry

        lax.fori_loop(0, wgu_f32.shape[0] // WEIGHT_CAST_ROWS, convert, 0)

        @pl.when(nxt_e_ref[b] >= 0)
        def _():
            for cp in fetch(nxt_e_ref[b]):
                cp.start()

    def ffn_rows(r0, n):
        rows = pl.ds(r0, n)
        valid = r0 + lax.broadcasted_iota(i32, (n, 1), 0) < cnt
        x = jnp.where(valid, _unpack_bf16_pairs(xs_ref[rows, :]), 0.0).astype(bf16)
        gu = jnp.dot(x, wgu_bf[...], preferred_element_type=f32) + bgu_ref[0]
        x_glu = jnp.minimum(gu[:, :de], SWIGLU_LIMIT)
        x_lin = jnp.clip(gu[:, de:], -SWIGLU_LIMIT, SWIGLU_LIMIT)
        act = x_glu * (1.0 / (1.0 + jnp.exp(-SWIGLU_ALPHA * x_glu))) * (x_lin + 1.0)
        y = jnp.dot(act.astype(bf16), wd_bf[...], preferred_element_type=f32) + bd_ref[0]
        y_ref[rows, :] = _pack_bf16_pairs(y)

    @pl.when(cnt == bm)
    def _():
        ffn_rows(0, bm)

    @pl.when(cnt < bm)
    def _():
        y_ref[...] = jnp.zeros_like(y_ref)

        def piece(i, carry):
            ffn_rows(pl.multiple_of(i * MOE_SUB_BLOCK, MOE_SUB_BLOCK), MOE_SUB_BLOCK)
            return carry

        lax.fori_loop(0, (cnt + MOE_SUB_BLOCK - 1) // MOE_SUB_BLOCK, piece, 0)


def _experts(xs, blk_e, blk_cnt, nxt_e, w_gu, b_gu, w_down, b_down):
    n_slots, w = xs.shape
    bm = MOE_BLOCK
    e, d, de2 = w_gu.shape
    de = w_down.shape[1]
    assert de == d, "one row loop converts both weight matrices"
    last = n_slots // bm - 1

    def rows_of_block(b, be, bc, ne):
        return (jnp.where(bc[b] > 0, b, last), 0)

    grid_spec = pltpu.PrefetchScalarGridSpec(
        num_scalar_prefetch=3,
        grid=(n_slots // bm,),
        in_specs=[pl.BlockSpec((bm, w), rows_of_block),
                  pl.BlockSpec(memory_space=pl.ANY),
                  pl.BlockSpec((1, 1, de2), lambda b, be, bc, ne: (be[b], 0, 0)),
                  pl.BlockSpec(memory_space=pl.ANY),
                  pl.BlockSpec((1, 1, d), lambda b, be, bc, ne: (be[b], 0, 0))],
        out_specs=pl.BlockSpec((bm, w), rows_of_block),
        scratch_shapes=[pltpu.VMEM((d, de2), f32), pltpu.VMEM((de, d), f32),
                        pltpu.VMEM((d, de2), bf16), pltpu.VMEM((de, d), bf16),
                        pltpu.SemaphoreType.DMA((2,))],
    )
    return pl.pallas_call(
        _expert_kernel,
        grid_spec=grid_spec,
        out_shape=jax.ShapeDtypeStruct((n_slots, w), u32),
        compiler_params=_params("arbitrary"),
        name="moe_experts",
    )(blk_e, blk_cnt, nxt_e, xs, w_gu, b_gu.reshape(e, 1, de2), w_down, b_down.reshape(e, 1, d))


def _sc_gather_rows(table, idx):
    n, w = idx.shape[0], table.shape[1]
    workers = SC_CORES * SC_SUBCORES
    chunk = SC_GATHER_CHUNK
    per = n // workers
    steps = per // chunk
    assert per * workers == n and steps * chunk == per and steps % 2 == 0

    def body(table_hbm, idx_hbm, out_hbm, idx_v, rows_v, gsem, wsem):
        base = (lax.axis_index("s") * SC_CORES + lax.axis_index("c")) * per
        pltpu.sync_copy(idx_hbm.at[pl.ds(base, per)], idx_v)

        def gather(c, slot):
            return pltpu.make_async_copy(table_hbm.at[idx_v.at[pl.ds(c * chunk, chunk)]], rows_v.at[slot],
                                         gsem.at[slot])

        def write(c, slot):
            return pltpu.make_async_copy(rows_v.at[slot], out_hbm.at[pl.ds(base + c * chunk, chunk)],
                                         wsem.at[slot])

        gather(0, 0).start()

        @pl.loop(0, steps, step=2)
        def _(c0):
            for slot in range(2):
                c = c0 + slot
                gather(c, slot).wait()
                write(c, slot).start()

                @pl.when(c >= 1)
                def _():
                    write(c - 1, 1 - slot).wait()

                @pl.when(c + 1 < steps)
                def _():
                    gather(c + 1, 1 - slot).start()

        write(steps - 1, 1).wait()

    return pl.kernel(
        body,
        out_type=jax.ShapeDtypeStruct((n, w), table.dtype),
        mesh=_sc_mesh(),
        scratch_types=[pltpu.VMEM((per,), i32), pltpu.VMEM((2, chunk, w), table.dtype),
                       pltpu.SemaphoreType.DMA((2,)), pltpu.SemaphoreType.DMA((2,))],
        name="sc_gather_rows",
    )(table, idx)


def _combine_kernel(x1_ref, gate_ref, gfin_ref, yg_ref, *rest):
    o_ref = rest[-1]
    acc = x1_ref[...]
    gates = gate_ref[...]
    for j in range(TOP_K):
        acc = acc + gates[:, j:j + 1] * _unpack_bf16_pairs(yg_ref[j])
    o_ref[...] = _rms_scale(acc) * gfin_ref[...]


def _combine(x1, gates, dest, y_slots, g_final, out_prev, tile0, total_tokens):
    t, d = x1.shape
    tm = TOKEN_TILE
    w = y_slots.shape[1]
    yg = _sc_gather_rows(y_slots, dest.reshape(-1)).reshape(TOP_K, t, w)
    in_specs = [pl.BlockSpec((tm, d), lambda i: (i, 0)),
                pl.BlockSpec((tm, TOP_K), lambda i: (i, 0)),
                pl.BlockSpec((1, d), lambda i: (0, 0)),
                pl.BlockSpec((TOP_K, tm, w), lambda i: (0, i, 0))]
    args = [x1, gates, g_final.reshape(1, d), yg]
    aliases = {}
    if out_prev is not None:
        in_specs.append(pl.BlockSpec(memory_space=pl.ANY))
        args.append(out_prev)
        aliases = {len(args) - 1: 0}
    return pl.pallas_call(
        _combine_kernel,
        grid=(t // tm,),
        in_specs=in_specs,
        out_specs=pl.BlockSpec((tm, d), lambda i: (i + tile0, 0)),
        out_shape=jax.ShapeDtypeStruct((total_tokens, d), f32),
        input_output_aliases=aliases,
        compiler_params=_params("parallel"),
        name="moe_combine",
    )(*args)


def _dest_kernel(start_ref, eidx_ref, rank_ref, dest_ref):
    eidx = eidx_ref[...]
    dest = rank_ref[...]
    for e in range(N_EXPERTS):
        dest = dest + jnp.where(eidx == e, start_ref[e], 0)
    dest_ref[...] = dest


def _slot_layout(counts, eidx, rank, n_blocks):
    bm = MOE_BLOCK
    padded = (counts + bm - 1) // bm * bm
    padded_end = jnp.cumsum(padded)
    start = padded_end - padded
    experts = jnp.arange(N_EXPERTS, dtype=i32)
    lookup = lambda table, idx: jnp.sum(jnp.where(idx[..., None] == experts, table, 0), axis=-1)
    dest = pl.pallas_call(
        _dest_kernel,
        grid_spec=pltpu.PrefetchScalarGridSpec(
            num_scalar_prefetch=1, grid=(1,),
            in_specs=[pl.BlockSpec(eidx.shape, lambda i, st: (0, 0)), pl.BlockSpec(rank.shape, lambda i, st: (0, 0))],
            out_specs=pl.BlockSpec(rank.shape, lambda i, st: (0, 0))),
        out_shape=jax.ShapeDtypeStruct(rank.shape, i32),
        name="moe_dest",
    )(start.astype(i32), eidx, rank)
    blk_row = jnp.arange(n_blocks, dtype=i32) * bm
    blk_e = jnp.minimum(jnp.sum((padded_end[None, :] <= blk_row[:, None]).astype(i32), axis=1), N_EXPERTS - 1)
    blk_cnt = jnp.clip(lookup(counts, blk_e) - (blk_row - lookup(start, blk_e)), 0, bm).astype(i32)
    none = jnp.int32(N_EXPERTS)
    nxt_e = jnp.min(jnp.where(blk_e[None, :] > blk_e[:, None], blk_e[None, :], none), axis=1)
    nxt_e = jnp.where(nxt_e == none, -1, nxt_e).astype(i32)
    return dest, blk_e, blk_cnt, nxt_e


def _layer_and_final_norm(x2d, mem, seq, g_mix, g_mem, w_in, w_mem_kv, na_rel_bias, g_grp, w_out, g_ffn,
                          router_w, router_b, w_gu, b_gu, w_down, b_down, g_final):
    t, d = x2d.shape
    b = t // seq
    q_na, k_na, v_na, u_ft, q_mem = _in_proj(x2d, g_mix, w_in.astype(bf16))
    k_mem, v_mem = _mem_kv(mem, g_mem, w_mem_kv.astype(bf16))
    shape3 = lambda a: a.reshape(b, seq, a.shape[-1])
    y_na = _neighbourhood_attention(shape3(q_na), shape3(k_na), shape3(v_na), _na_bias_table(na_rel_bias))
    y_ft = _fourier_mix(shape3(u_ft), _ft_tables(seq))
    y_na, y_ft = y_na.reshape(t, -1), y_ft.reshape(t, -1)
    w_out_bf16 = w_out.astype(bf16)
    rw_hi = router_w.astype(bf16)
    router_w2 = jnp.stack([rw_hi, (router_w - rw_hi.astype(f32)).astype(bf16)])

    tiles = t // TOKEN_TILE
    unit = tiles // sum(MOE_GROUP_SHARES)
    assert unit * sum(MOE_GROUP_SHARES) == tiles
    out = None
    tile0 = 0
    dest = router_b
    for share in MOE_GROUP_SHARES:
        group_tiles = share * unit
        n_blocks = (group_tiles * TOKEN_TILE * TOP_K) // MOE_BLOCK + N_EXPERTS
        x1, h2p, eidx, gates, rank, counts = _mix_out(
            x2d, y_na, y_ft, q_mem, k_mem, v_mem, g_grp, w_out_bf16, g_ffn, router_w2, router_b, seq,
            tile0, group_tiles, dest)
        dest, blk_e, blk_cnt, nxt_e = _slot_layout(counts[:, 0], eidx, rank, n_blocks)
        xs = _dispatch(h2p, dest, n_blocks * MOE_BLOCK)
        y_slots = _experts(xs, blk_e, blk_cnt, nxt_e, w_gu, b_gu, w_down, b_down)
        out = _combine(x1, gates, dest, y_slots, g_final, out, tile0, t)
        tile0 += group_tiles
    return out


def kernel(x, mem, g_mix, g_mem, w_in, w_mem_kv, na_rel_bias, g_grp, w_out, g_ffn, router_w, router_b,
           w_gu, b_gu, w_down, b_down, g_final):
    b, seq, d = x.shape
    depth = w_in.shape[0]
    assert depth == 1, "the final norm is fused into the single layer's combine step"
    out = _layer_and_final_norm(
        x.reshape(b * seq, d), mem, seq, g_mix[0], g_mem[0], w_in[0], w_mem_kv[0], na_rel_bias[0], g_grp[0],
        w_out[0], g_ffn[0], router_w[0], router_b[0], w_gu[0], b_gu[0], w_down[0], b_down[0], g_final)
    return out.reshape(b, seq, d)
```

```python
import numpy as np
import jax
import jax.numpy as jnp
from jax import lax
from jax.experimental import pallas as pl
from jax.experimental.pallas import tpu as pltpu
from jax.experimental.pallas import tpu_sc as plsc

f32 = jnp.float32
bf16 = jnp.bfloat16
u32 = jnp.uint32
i32 = jnp.int32

GRID_W = 64
NA_HEADS = 8
NA_HEAD_DIM = 64
NA_WIN_ROWS = 8
NA_WIN_COLS = 16
FT_GROUPS = 4
FT_GROUP_DIM = 128
MEM_HEADS = 4
MEM_HEAD_DIM = 128
NA_WIDTH = NA_HEADS * NA_HEAD_DIM
FT_WIDTH = FT_GROUPS * FT_GROUP_DIM
MEM_WIDTH = MEM_HEADS * MEM_HEAD_DIM
N_EXPERTS = 32
TOP_K = 4
SWIGLU_LIMIT = 7.0
SWIGLU_ALPHA = 1.702
EPS = 1e-6

LANES = 128
SUBLANES = 8
VMEM_LIMIT_BYTES = 56 * 1024 * 1024
SC_CORES = 2
SC_SUBCORES = 16
SC_GATHER_CHUNK = 64

TOKEN_TILE = 1024
IN_PROJ_TILE = 1024
MOE_BLOCK = 512
MOE_SUB_BLOCK = 128
WEIGHT_CAST_ROWS = 128
MOE_GROUP_SHARES = (5, 3)
NA_ROW_UNROLL = 16
FT_N1 = 64
FT_N2 = 128
FT_K1_BLOCK = 16
FT_BATCH_BLOCK = 4
LOG2_E = 1.4426950408889634
MASK_VALUE = -jnp.inf


def _params(*semantics):
    return pltpu.CompilerParams(dimension_semantics=semantics, vmem_limit_bytes=VMEM_LIMIT_BYTES)


def _rms_scale(x):
    return x * lax.rsqrt(jnp.mean(x * x, axis=-1, keepdims=True) + EPS)


def _softmax_rows(s):
    p = jnp.exp(s - jnp.max(s, axis=-1, keepdims=True))
    return p / jnp.sum(p, axis=-1, keepdims=True)


def _bf16_bits(x):
    return pltpu.bitcast(x.astype(bf16).astype(f32), u32)


def _pack2(lo, hi):
    return (_bf16_bits(lo) >> 16) | (_bf16_bits(hi) & jnp.uint32(0xFFFF0000))


def _unpack2(w):
    return pltpu.bitcast(w << 16, f32), pltpu.bitcast(w & jnp.uint32(0xFFFF0000), f32)


def _pack_bf16_pairs(x):
    n = x.shape[1] // 2
    return _pack2(x[:, :n], x[:, n:])


def _unpack_bf16_pairs(w):
    return jnp.concatenate(_unpack2(w), axis=1)


def _in_proj_kernel(x_ref, g_ref, w_ref, qna_ref, kna_ref, vna_ref, uft_ref, qmem_ref):
    h = _rms_scale(x_ref[...]) * g_ref[...]
    proj = jnp.dot(h.astype(bf16), w_ref[...], preferred_element_type=f32)
    o = NA_WIDTH
    qna_ref[...] = (proj[:, :o] * (NA_HEAD_DIM ** -0.5 * LOG2_E)).astype(bf16)
    kna_ref[...] = proj[:, o:2 * o].astype(bf16)
    vna_ref[...] = proj[:, 2 * o:3 * o].astype(bf16)
    uft_ref[...] = _pack_bf16_pairs(proj[:, 3 * o:3 * o + FT_WIDTH])
    qmem_ref[...] = proj[:, 3 * o + FT_WIDTH:].astype(bf16)


def _in_proj(x2d, g_mix, w_in_bf16):
    t, d = x2d.shape
    tm = IN_PROJ_TILE
    row = lambda w: pl.BlockSpec((tm, w), lambda i: (i, 0))
    return pl.pallas_call(
        _in_proj_kernel,
        grid=(t // tm,),
        in_specs=[row(d), pl.BlockSpec((1, d), lambda i: (0, 0)),
                  pl.BlockSpec(w_in_bf16.shape, lambda i: (0, 0))],
        out_specs=[row(NA_WIDTH), row(NA_WIDTH), row(NA_WIDTH), row(FT_WIDTH // 2), row(MEM_WIDTH)],
        out_shape=[jax.ShapeDtypeStruct((t, NA_WIDTH), bf16)] * 3
        + [jax.ShapeDtypeStruct((t, FT_WIDTH // 2), u32), jax.ShapeDtypeStruct((t, MEM_WIDTH), bf16)],
        compiler_params=_params("parallel"),
        name="in_proj",
    )(x2d, g_mix.reshape(1, d), w_in_bf16)


def _na_bias_table(rel_bias):
    c = np.arange(GRID_W)
    dc_idx = np.clip(c[None, :] - c[:, None], -(NA_WIN_COLS - 1), NA_WIN_COLS - 1) + (NA_WIN_COLS - 1)
    col_start = np.clip(c - NA_WIN_COLS // 2, 0, GRID_W - NA_WIN_COLS)
    col_in = (c[None, :] >= col_start[:, None]) & (c[None, :] < col_start[:, None] + NA_WIN_COLS)
    pick_c = jnp.asarray(dc_idx[:, :, None] == np.arange(2 * NA_WIN_COLS - 1), f32)
    cols = jnp.einsum("hab,qcb->haqc", rel_bias.astype(f32), pick_c, precision=lax.Precision.HIGHEST)
    cols = jnp.where(col_in[None, None], cols * LOG2_E, MASK_VALUE)
    tab = jnp.stack([jnp.concatenate([cols[:, j - s + NA_WIN_ROWS - 1] for j in range(NA_WIN_ROWS)], axis=-1)
                     for s in range(NA_WIN_ROWS)])
    return tab.reshape(NA_WIN_ROWS, NA_HEADS // 2, 2 * GRID_W, NA_WIN_ROWS * GRID_W)


def _na_kernel(q_ref, k_ref, v_ref, bias_ref, o_ref):
    rows = q_ref.shape[1] // GRID_W
    win = NA_WIN_ROWS * GRID_W
    first_head = lax.broadcasted_iota(i32, (GRID_W, 2 * NA_HEAD_DIM), 1) < NA_HEAD_DIM

    def body(it, carry):
        scores, values, q_offsets = [], [], []
        for u in range(NA_ROW_UNROLL):
            r = it * NA_ROW_UNROLL + u
            row_start = jnp.clip(r - NA_WIN_ROWS // 2, 0, rows - NA_WIN_ROWS)
            q0 = pl.multiple_of(r * GRID_W, GRID_W)
            k0 = pl.multiple_of(row_start * GRID_W, GRID_W)
            q = q_ref[0, pl.ds(q0, GRID_W), :]
            zero = jnp.zeros_like(q)
            qm = jnp.concatenate([jnp.where(first_head, q, zero), jnp.where(first_head, zero, q)], axis=0)
            s = lax.dot_general(qm, k_ref[0, pl.ds(k0, win), :], (((1,), (1,)), ((), ())),
                                preferred_element_type=f32)
            scores.append(s + bias_ref[r - row_start, 0])
            values.append(v_ref[0, pl.ds(k0, win), :])
            q_offsets.append(q0)
        s = jnp.concatenate(scores, axis=0)
        p = jnp.exp2(s - jnp.max(s, axis=-1, keepdims=True))
        inv_den = 1.0 / jnp.sum(p, axis=-1, keepdims=True)
        p = p.astype(bf16)
        for u in range(NA_ROW_UNROLL):
            sl = slice(u * 2 * GRID_W, (u + 1) * 2 * GRID_W)
            o = jnp.dot(p[sl], values[u], preferred_element_type=f32) * inv_den[sl]
            o_ref[0, pl.ds(q_offsets[u], GRID_W), :] = jnp.where(
                first_head, o[:GRID_W], o[GRID_W:]).astype(o_ref.dtype)
        return carry

    lax.fori_loop(0, rows // NA_ROW_UNROLL, body, 0)


def _neighbourhood_attention(q, k, v, bias_tab):
    b, s, _ = q.shape
    pair = 2 * NA_HEAD_DIM
    qkv_spec = pl.BlockSpec((1, s, pair), lambda bi, hp: (bi, 0, hp))
    return pl.pallas_call(
        _na_kernel,
        grid=(b, NA_HEADS // 2),
        in_specs=[qkv_spec, qkv_spec, qkv_spec,
                  pl.BlockSpec((NA_WIN_ROWS, 1, 2 * GRID_W, NA_WIN_ROWS * GRID_W), lambda bi, hp: (0, hp, 0, 0))],
        out_specs=qkv_spec,
        out_shape=jax.ShapeDtypeStruct((b, s, NA_WIDTH), bf16),
        compiler_params=_params("parallel", "parallel"),
        name="neighbourhood_attention",
    )(q, k, v, bias_tab)


def _ft_tables(seq):
    assert seq == FT_N1 * FT_N2
    n_blk = FT_N2 // SUBLANES
    k1 = np.arange(FT_N1)[:, None, None, None]
    sr = np.arange(SUBLANES)[None, :, None, None]
    n1 = np.arange(FT_N1)[None, None, :, None]
    sc = np.arange(SUBLANES)[None, None, None, :]
    stage1 = np.zeros((n_blk, 2, FT_N1, SUBLANES, FT_N1, SUBLANES), np.float64)
    for blk in range(n_blk):
        n = FT_N2 * n1 + SUBLANES * blk + sr
        ang = 2.0 * np.pi * ((k1 * n) % seq) / seq
        eye = (sr == sc)
        stage1[blk, 0] = np.cos(ang) * eye
        stage1[blk, 1] = -np.sin(ang) * eye
    stage1 = stage1.reshape(n_blk, 2 * FT_N1 * SUBLANES, FT_N1 * SUBLANES)
    a = np.arange(FT_N2)
    ang2 = 2.0 * np.pi * ((a[:, None] * a[None, :]) % FT_N2) / FT_N2
    c2, s2 = np.cos(ang2), np.sin(ang2)
    stage2 = np.block([[c2, s2], [-s2, c2]])
    g = np.arange(FT_GROUP_DIM)
    angc = 2.0 * np.pi * ((g[:, None] * g[None, :]) % FT_GROUP_DIM) / FT_GROUP_DIM
    norm = 1.0 / np.sqrt(seq * FT_GROUP_DIM)
    chan = np.concatenate([np.cos(angc), np.sin(angc)], axis=0) * norm
    return (jnp.asarray(stage1, bf16), jnp.asarray(stage2, bf16), jnp.asarray(chan, bf16))


def _ft_stage1_kernel(u_ref, m_ref, z_ref):
    rows = FT_N1 * SUBLANES
    nb = u_ref.shape[0]
    u = jnp.concatenate([_unpack_bf16_pairs(u_ref[bb].reshape(rows, FT_WIDTH // 2)) for bb in range(nb)],
                        axis=1).astype(bf16)
    z = jnp.dot(m_ref[0], u, preferred_element_type=f32)
    for bb in range(nb):
        zb = z[:, bb * FT_WIDTH:(bb + 1) * FT_WIDTH]
        z_ref[bb] = _pack2(zb[:rows], zb[rows:]).reshape(FT_N1, SUBLANES, FT_WIDTH)


def _ft_stage2_kernel(z_ref, s2_ref, cs_ref, y_ref):
    gd = FT_GROUP_DIM
    xs = []
    for kk in range(FT_K1_BLOCK):
        zz = jnp.concatenate(_unpack2(z_ref[0, kk]), axis=0).astype(bf16)
        xs.append(jnp.dot(s2_ref[...], zz, preferred_element_type=f32))
    for g in range(FT_GROUPS):
        cols = slice(g * gd, (g + 1) * gd)
        xg = jnp.concatenate([jnp.concatenate([x[:FT_N2, cols], x[FT_N2:, cols]], axis=1) for x in xs], axis=0)
        og = jnp.dot(xg.astype(bf16), cs_ref[...], preferred_element_type=f32)
        for kk in range(FT_K1_BLOCK):
            y_ref[0, kk, :, cols] = og[kk * FT_N2:(kk + 1) * FT_N2].astype(y_ref.dtype)


def _fourier_mix(u_packed, tables):
    u = u_packed
    b, s, _ = u.shape
    c = FT_WIDTH
    stage1, stage2, chan = tables
    n_blk = FT_N2 // SUBLANES
    z = pl.pallas_call(
        _ft_stage1_kernel,
        grid=(n_blk, b // FT_BATCH_BLOCK),
        in_specs=[pl.BlockSpec((FT_BATCH_BLOCK, FT_N1, SUBLANES, c // 2), lambda j, bi: (bi, 0, j, 0)),
                  pl.BlockSpec((1,) + stage1.shape[1:], lambda j, bi: (j, 0, 0))],
        out_specs=pl.BlockSpec((FT_BATCH_BLOCK, FT_N1, SUBLANES, c), lambda j, bi: (bi, 0, j, 0)),
        out_shape=jax.ShapeDtypeStruct((b, FT_N1, FT_N2, c), u32),
        compiler_params=_params("parallel", "parallel"),
        name="fourier_stage1",
    )(u.reshape(b, FT_N1, FT_N2, c // 2), stage1)
    y = pl.pallas_call(
        _ft_stage2_kernel,
        grid=(b, FT_N1 // FT_K1_BLOCK),
        in_specs=[pl.BlockSpec((1, FT_K1_BLOCK, FT_N2, c), lambda bi, kb: (bi, kb, 0, 0)),
                  pl.BlockSpec(stage2.shape, lambda bi, kb: (0, 0)),
                  pl.BlockSpec(chan.shape, lambda bi, kb: (0, 0))],
        out_specs=pl.BlockSpec((1, FT_K1_BLOCK, FT_N2, c), lambda bi, kb: (bi, kb, 0, 0)),
        out_shape=jax.ShapeDtypeStruct((b, FT_N1, FT_N2, c), bf16),
        compiler_params=_params("parallel", "parallel"),
        name="fourier_stage2",
    )(z, stage2, chan)
    return y.transpose(0, 2, 1, 3).reshape(b, s, c)


def _mem_kv_kernel(mem_ref, g_ref, w_ref, k_ref, v_ref):
    mn = _rms_scale(mem_ref[0]) * g_ref[...]
    kv = jnp.dot(mn.astype(bf16), w_ref[...], preferred_element_type=f32)
    k_ref[0] = kv[:, :MEM_WIDTH].astype(bf16)
    v_ref[0] = kv[:, MEM_WIDTH:].astype(bf16)


def _mem_kv(mem, g_mem, w_kv_bf16):
    b, m, d = mem.shape
    kv_spec = pl.BlockSpec((1, m, MEM_WIDTH), lambda bi: (bi, 0, 0))
    return pl.pallas_call(
        _mem_kv_kernel,
        grid=(b,),
        in_specs=[pl.BlockSpec((1, m, d), lambda bi: (bi, 0, 0)), pl.BlockSpec((1, d), lambda bi: (0, 0)),
                  pl.BlockSpec(w_kv_bf16.shape, lambda bi: (0, 0))],
        out_specs=[kv_spec, kv_spec],
        out_shape=[jax.ShapeDtypeStruct((b, m, MEM_WIDTH), bf16)] * 2,
        compiler_params=_params("parallel"),
        name="mem_kv",
    )(mem, g_mem.reshape(1, d), w_kv_bf16)


def _mix_out_kernel(x_ref, yna_ref, yft_ref, qm_ref, km_ref, vm_ref, ggrp_ref, wout_ref, gffn_ref, rw_ref,
                    rb_ref, _order_ref, x1_ref, h2p_ref, eidx_ref, gate_ref, rank_ref, cnt_ref, carry_ref):
    tm = x_ref.shape[0]

    @pl.when(pl.program_id(0) == 0)
    def _():
        carry_ref[...] = jnp.zeros_like(carry_ref)

    q = qm_ref[...]
    km = km_ref[0]
    vm = vm_ref[0]
    heads = []
    for h in range(MEM_HEADS):
        sl = slice(h * MEM_HEAD_DIM, (h + 1) * MEM_HEAD_DIM)
        s = lax.dot_general(q[:, sl], km[:, sl], (((1,), (1,)), ((), ())), preferred_element_type=f32)
        p = _softmax_rows(s * (MEM_HEAD_DIM ** -0.5))
        heads.append(jnp.dot(p.astype(bf16), vm[:, sl], preferred_element_type=f32))
    ymem = jnp.concatenate(heads, axis=1)

    g = ggrp_ref[...]
    a, c = NA_WIDTH, NA_WIDTH + FT_WIDTH
    y = jnp.concatenate([_rms_scale(yna_ref[...].astype(f32)) * g[:, :a],
                         _rms_scale(yft_ref[...].astype(f32)) * g[:, a:c],
                         _rms_scale(ymem) * g[:, c:]], axis=1)
    x1 = x_ref[...] + jnp.dot(y.astype(bf16), wout_ref[...], preferred_element_type=f32)
    x1_ref[...] = x1
    h2 = _rms_scale(x1) * gffn_ref[...]
    h2p_ref[...] = _pack_bf16_pairs(h2)

    h_hi = h2.astype(bf16)
    h_lo = (h2 - h_hi.astype(f32)).astype(bf16)
    hh = jnp.dot(h_hi, rw_ref[...], preferred_element_type=f32)
    logits = (hh[:, :LANES] + hh[:, LANES:]
              + jnp.dot(h_lo, rw_ref[:, :LANES], preferred_element_type=f32)) + rb_ref[...]
    l = logits.T[:N_EXPERTS]
    row = lax.broadcasted_iota(i32, (N_EXPERTS, tm), 0).astype(f32)
    vals, idxs, sels = [], [], []
    for _ in range(TOP_K):
        m = jnp.max(l, axis=0, keepdims=True)
        idx = jnp.min(jnp.where(l == m, row, float(N_EXPERTS)), axis=0, keepdims=True)
        sel = row == idx
        vals.append(m)
        idxs.append(idx)
        sels.append(sel)
        l = jnp.where(sel, -jnp.inf, l)
    ex = [jnp.exp(v - vals[0]) for v in vals]
    den = ex[0] + ex[1] + ex[2] + ex[3]

    onehot = (sels[0] | sels[1] | sels[2] | sels[3]).astype(f32)
    earlier = (lax.broadcasted_iota(i32, (tm, tm), 0) < lax.broadcasted_iota(i32, (tm, tm), 1)).astype(bf16)
    before = jnp.dot(onehot.astype(bf16), earlier, preferred_element_type=f32) + carry_ref[...]
    ranks = [jnp.sum(jnp.where(sel, before, 0.0), axis=0, keepdims=True) for sel in sels]
    carry_ref[...] = carry_ref[...] + jnp.sum(onehot, axis=1, keepdims=True)
    cnt_ref[...] = carry_ref[...].astype(i32)

    eidx_ref[...] = jnp.concatenate(idxs, axis=0).astype(i32)
    rank_ref[...] = jnp.concatenate(ranks, axis=0).astype(i32)
    gates_t = jnp.concatenate([e / den for e in ex] + [jnp.zeros((LANES - TOP_K, tm), f32)], axis=0)
    gate_ref[...] = gates_t.T[:, :TOP_K]


def _mix_out(x2d, y_na, y_ft, q_mem, k_mem, v_mem, g_grp, w_out_bf16, g_ffn, router_w2, router_b, seq,
             tile0, n_tiles, order_after):
    d = x2d.shape[1]
    tm = TOKEN_TILE
    t = n_tiles * tm
    steps_per_batch = seq // tm
    m = k_mem.shape[1]
    row_in = lambda w: pl.BlockSpec((tm, w), lambda i: (i + tile0, 0))
    row_out = lambda w: pl.BlockSpec((tm, w), lambda i: (i, 0))
    full = lambda a: pl.BlockSpec(a.shape, lambda i: (0,) * a.ndim)
    kv_spec = pl.BlockSpec((1, m, MEM_WIDTH), lambda i: ((i + tile0) // steps_per_batch, 0, 0))
    g_grp2, g_ffn2 = g_grp.reshape(1, -1), g_ffn.reshape(1, d)
    rb2 = jnp.pad(router_b.reshape(1, N_EXPERTS), ((0, 0), (0, LANES - N_EXPERTS)))
    router_w2 = jnp.pad(router_w2, ((0, 0), (0, 0), (0, LANES - N_EXPERTS)))
    router_w2 = jnp.concatenate([router_w2[0], router_w2[1]], axis=1)
    col_out = pl.BlockSpec((TOP_K, tm), lambda i: (0, i))
    return pl.pallas_call(
        _mix_out_kernel,
        grid=(n_tiles,),
        in_specs=[row_in(d), row_in(NA_WIDTH), row_in(FT_WIDTH), row_in(MEM_WIDTH), kv_spec, kv_spec,
                  full(g_grp2), full(w_out_bf16), full(g_ffn2), full(router_w2), full(rb2),
                  pl.BlockSpec(memory_space=pl.ANY)],
        out_specs=[row_out(d), row_out(d // 2), col_out, row_out(TOP_K), col_out,
                   pl.BlockSpec((N_EXPERTS, 1), lambda i: (0, 0))],
        out_shape=[jax.ShapeDtypeStruct((t, d), f32), jax.ShapeDtypeStruct((t, d // 2), u32),
                   jax.ShapeDtypeStruct((TOP_K, t), i32), jax.ShapeDtypeStruct((t, TOP_K), f32),
                   jax.ShapeDtypeStruct((TOP_K, t), i32), jax.ShapeDtypeStruct((N_EXPERTS, 1), i32)],
        scratch_shapes=[pltpu.VMEM((N_EXPERTS, 1), f32)],
        compiler_params=_params("arbitrary"),
        name="mix_out_router",
    )(x2d, y_na, y_ft, q_mem, k_mem, v_mem, g_grp2, w_out_bf16, g_ffn2, router_w2, rb2, order_after)


def _sc_mesh():
    return plsc.VectorSubcoreMesh(core_axis_name="c", subcore_axis_name="s",
                                  num_cores=SC_CORES, num_subcores=SC_SUBCORES)


def _dispatch(h2p, dest, n_slots):
    t, w = h2p.shape
    workers = SC_CORES * SC_SUBCORES
    chunk = SC_GATHER_CHUNK
    per = t // workers
    steps = per // chunk
    assert per * workers == t and steps * chunk == per and steps % 2 == 0
    idx = dest.reshape(TOP_K, workers, steps, chunk)

    def body(h_hbm, idx_hbm, out_hbm, idx_v, rows_v, rsem, ssem):
        wid = lax.axis_index("s") * SC_CORES + lax.axis_index("c")
        base = wid * per
        for j in range(TOP_K):
            pltpu.sync_copy(idx_hbm.at[j, wid], idx_v.at[j])

        def read(c, slot):
            return pltpu.make_async_copy(h_hbm.at[pl.ds(base + c * chunk, chunk)], rows_v.at[slot], rsem.at[slot])

        def scatters(c, slot):
            return [pltpu.make_async_copy(rows_v.at[slot], out_hbm.at[idx_v.at[j, c]], ssem.at[slot])
                    for j in range(TOP_K)]

        read(0, 0).start()

        @pl.loop(0, steps, step=2)
        def _(c0):
            for slot in range(2):
                c = c0 + slot
                read(c, slot).wait()
                for cp in scatters(c, slot):
                    cp.start()

                @pl.when(c >= 1)
                def _():
                    for cp in scatters(c - 1, 1 - slot):
                        cp.wait()

                @pl.when(c + 1 < steps)
                def _():
                    read(c + 1, 1 - slot).start()

        for cp in scatters(steps - 1, 1):
            cp.wait()

    return pl.kernel(
        body,
        out_type=jax.ShapeDtypeStruct((n_slots, w), h2p.dtype),
        mesh=_sc_mesh(),
        scratch_types=[pltpu.VMEM((TOP_K, steps, chunk), i32), pltpu.VMEM((2, chunk, w), h2p.dtype),
                       pltpu.SemaphoreType.DMA((2,)), pltpu.SemaphoreType.DMA((2,))],
        name="sc_dispatch_rows",
    )(h2p, idx)


def _expert_kernel(blk_e_ref, blk_cnt_ref, nxt_e_ref, xs_ref, wgu_hbm, bgu_ref, wd_hbm, bd_ref, y_ref,
                   wgu_f32, wd_f32, wgu_bf, wd_bf, sem):
    b = pl.program_id(0)
    e = blk_e_ref[b]
    cnt = blk_cnt_ref[b]
    bm = xs_ref.shape[0]
    de = wd_f32.shape[0]

    def fetch(expert):
        return (pltpu.make_async_copy(wgu_hbm.at[expert], wgu_f32, sem.at[0]),
                pltpu.make_async_copy(wd_hbm.at[expert], wd_f32, sem.at[1]))

    @pl.when(b == 0)
    def _():
        for cp in fetch(e):
            cp.start()

    @pl.when(jnp.logical_or(b == 0, e != blk_e_ref[jnp.maximum(b - 1, 0)]))
    def _():
        for cp in fetch(e):
            cp.wait()

        def convert(i, carry):
            rows = pl.ds(pl.multiple_of(i * WEIGHT_CAST_ROWS, WEIGHT_CAST_ROWS), WEIGHT_CAST_ROWS)
            wgu_bf[rows, :] = wgu_f32[rows, :].astype(bf16)
            wd_bf[rows, :] = wd_f32[rows, :].astype(bf16)
            return carry

        lax.fori_loop(0, wgu_f32.shape[0] // WEIGHT_CAST_ROWS, convert, 0)

        @pl.when(nxt_e_ref[b] >= 0)
        def _():
            for cp in fetch(nxt_e_ref[b]):
                cp.start()

    def ffn_rows(r0, n):
        rows = pl.ds(r0, n)
        valid = r0 + lax.broadcasted_iota(i32, (n, 1), 0) < cnt
        x = jnp.where(valid, _unpack_bf16_pairs(xs_ref[rows, :]), 0.0).astype(bf16)
        gu = jnp.dot(x, wgu_bf[...], preferred_element_type=f32) + bgu_ref[0]
        x_glu = jnp.minimum(gu[:, :de], SWIGLU_LIMIT)
        x_lin = jnp.clip(gu[:, de:], -SWIGLU_LIMIT, SWIGLU_LIMIT)
        act = x_glu * (1.0 / (1.0 + jnp.exp(-SWIGLU_ALPHA * x_glu))) * (x_lin + 1.0)
        y = jnp.dot(act.astype(bf16), wd_bf[...], preferred_element_type=f32) + bd_ref[0]
        y_ref[rows, :] = _pack_bf16_pairs(y)

    @pl.when(cnt == bm)
    def _():
        ffn_rows(0, bm)

    @pl.when(cnt < bm)
    def _():
        y_ref[...] = jnp.zeros_like(y_ref)

        def piece(i, carry):
            ffn_rows(pl.multiple_of(i * MOE_SUB_BLOCK, MOE_SUB_BLOCK), MOE_SUB_BLOCK)
            return carry

        lax.fori_loop(0, (cnt + MOE_SUB_BLOCK - 1) // MOE_SUB_BLOCK, piece, 0)


def _experts(xs, blk_e, blk_cnt, nxt_e, w_gu, b_gu, w_down, b_down):
    n_slots, w = xs.shape
    bm = MOE_BLOCK
    e, d, de2 = w_gu.shape
    de = w_down.shape[1]
    assert de == d, "one row loop converts both weight matrices"
    last = n_slots // bm - 1

    def rows_of_block(b, be, bc, ne):
        return (jnp.where(bc[b] > 0, b, last), 0)

    grid_spec = pltpu.PrefetchScalarGridSpec(
        num_scalar_prefetch=3,
        grid=(n_slots // bm,),
        in_specs=[pl.BlockSpec((bm, w), rows_of_block),
                  pl.BlockSpec(memory_space=pl.ANY),
                  pl.BlockSpec((1, 1, de2), lambda b, be, bc, ne: (be[b], 0, 0)),
                  pl.BlockSpec(memory_space=pl.ANY),
                  pl.BlockSpec((1, 1, d), lambda b, be, bc, ne: (be[b], 0, 0))],
        out_specs=pl.BlockSpec((bm, w), rows_of_block),
        scratch_shapes=[pltpu.VMEM((d, de2), f32), pltpu.VMEM((de, d), f32),
                        pltpu.VMEM((d, de2), bf16), pltpu.VMEM((de, d), bf16),
                        pltpu.SemaphoreType.DMA((2,))],
    )
    return pl.pallas_call(
        _expert_kernel,
        grid_spec=grid_spec,
        out_shape=jax.ShapeDtypeStruct((n_slots, w), u32),
        compiler_params=_params("arbitrary"),
        name="moe_experts",
    )(blk_e, blk_cnt, nxt_e, xs, w_gu, b_gu.reshape(e, 1, de2), w_down, b_down.reshape(e, 1, d))


def _sc_gather_rows(table, idx):
    n, w = idx.shape[0], table.shape[1]
    workers = SC_CORES * SC_SUBCORES
    chunk = SC_GATHER_CHUNK
    per = n // workers
    steps = per // chunk
    assert per * workers == n and steps * chunk == per and steps % 2 == 0

    def body(table_hbm, idx_hbm, out_hbm, idx_v, rows_v, gsem, wsem):
        base = (lax.axis_index("s") * SC_CORES + lax.axis_index("c")) * per
        pltpu.sync_copy(idx_hbm.at[pl.ds(base, per)], idx_v)

        def gather(c, slot):
            return pltpu.make_async_copy(table_hbm.at[idx_v.at[pl.ds(c * chunk, chunk)]], rows_v.at[slot],
                                         gsem.at[slot])

        def write(c, slot):
            return pltpu.make_async_copy(rows_v.at[slot], out_hbm.at[pl.ds(base + c * chunk, chunk)],
                                         wsem.at[slot])

        gather(0, 0).start()

        @pl.loop(0, steps, step=2)
        def _(c0):
            for slot in range(2):
                c = c0 + slot
                gather(c, slot).wait()
                write(c, slot).start()

                @pl.when(c >= 1)
                def _():
                    write(c - 1, 1 - slot).wait()

                @pl.when(c + 1 < steps)
                def _():
                    gather(c + 1, 1 - slot).start()

        write(steps - 1, 1).wait()

    return pl.kernel(
        body,
        out_type=jax.ShapeDtypeStruct((n, w), table.dtype),
        mesh=_sc_mesh(),
        scratch_types=[pltpu.VMEM((per,), i32), pltpu.VMEM((2, chunk, w), table.dtype),
                       pltpu.SemaphoreType.DMA((2,)), pltpu.SemaphoreType.DMA((2,))],
        name="sc_gather_rows",
    )(table, idx)


def _combine_kernel(x1_ref, gate_ref, gfin_ref, yg_ref, *rest):
    o_ref = rest[-1]
    acc = x1_ref[...]
    gates = gate_ref[...]
    for j in range(TOP_K):
        acc = acc + gates[:, j:j + 1] * _unpack_bf16_pairs(yg_ref[j])
    o_ref[...] = _rms_scale(acc) * gfin_ref[...]


def _combine(x1, gates, dest, y_slots, g_final, out_prev, tile0, total_tokens):
    t, d = x1.shape
    tm = TOKEN_TILE
    w = y_slots.shape[1]
    yg = _sc_gather_rows(y_slots, dest.reshape(-1)).reshape(TOP_K, t, w)
    in_specs = [pl.BlockSpec((tm, d), lambda i: (i, 0)),
                pl.BlockSpec((tm, TOP_K), lambda i: (i, 0)),
                pl.BlockSpec((1, d), lambda i: (0, 0)),
                pl.BlockSpec((TOP_K, tm, w), lambda i: (0, i, 0))]
    args = [x1, gates, g_final.reshape(1, d), yg]
    aliases = {}
    if out_prev is not None:
        in_specs.append(pl.BlockSpec(memory_space=pl.ANY))
        args.append(out_prev)
        aliases = {len(args) - 1: 0}
    return pl.pallas_call(
        _combine_kernel,
        grid=(t // tm,),
        in_specs=in_specs,
        out_specs=pl.BlockSpec((tm, d), lambda i: (i + tile0, 0)),
        out_shape=jax.ShapeDtypeStruct((total_tokens, d), f32),
        input_output_aliases=aliases,
        compiler_params=_params("parallel"),
        name="moe_combine",
    )(*args)


def _dest_kernel(start_ref, eidx_ref, rank_ref, dest_ref):
    eidx = eidx_ref[...]
    dest = rank_ref[...]
    for e in range(N_EXPERTS):
        dest = dest + jnp.where(eidx == e, start_ref[e], 0)
    dest_ref[...] = dest


def _slot_layout(counts, eidx, rank, n_blocks):
    bm = MOE_BLOCK
    padded = (counts + bm - 1) // bm * bm
    padded_end = jnp.cumsum(padded)
    start = padded_end - padded
    experts = jnp.arange(N_EXPERTS, dtype=i32)
    lookup = lambda table, idx: jnp.sum(jnp.where(idx[..., None] == experts, table, 0), axis=-1)
    dest = pl.pallas_call(
        _dest_kernel,
        grid_spec=pltpu.PrefetchScalarGridSpec(
            num_scalar_prefetch=1, grid=(1,),
            in_specs=[pl.BlockSpec(eidx.shape, lambda i, st: (0, 0)), pl.BlockSpec(rank.shape, lambda i, st: (0, 0))],
            out_specs=pl.BlockSpec(rank.shape, lambda i, st: (0, 0))),
        out_shape=jax.ShapeDtypeStruct(rank.shape, i32),
        name="moe_dest",
    )(start.astype(i32), eidx, rank)
    blk_row = jnp.arange(n_blocks, dtype=i32) * bm
    blk_e = jnp.minimum(jnp.sum((padded_end[None, :] <= blk_row[:, None]).astype(i32), axis=1), N_EXPERTS - 1)
    blk_cnt = jnp.clip(lookup(counts, blk_e) - (blk_row - lookup(start, blk_e)), 0, bm).astype(i32)
    none = jnp.int32(N_EXPERTS)
    nxt_e = jnp.min(jnp.where(blk_e[None, :] > blk_e[:, None], blk_e[None, :], none), axis=1)
    nxt_e = jnp.where(nxt_e == none, -1, nxt_e).astype(i32)
    return dest, blk_e, blk_cnt, nxt_e


def _layer_and_final_norm(x2d, mem, seq, g_mix, g_mem, w_in, w_mem_kv, na_rel_bias, g_grp, w_out, g_ffn,
                          router_w, router_b, w_gu, b_gu, w_down, b_down, g_final):
    t, d = x2d.shape
    b = t // seq
    q_na, k_na, v_na, u_ft, q_mem = _in_proj(x2d, g_mix, w_in.astype(bf16))
    k_mem, v_mem = _mem_kv(mem, g_mem, w_mem_kv.astype(bf16))
    shape3 = lambda a: a.reshape(b, seq, a.shape[-1])
    y_na = _neighbourhood_attention(shape3(q_na), shape3(k_na), shape3(v_na), _na_bias_table(na_rel_bias))
    y_ft = _fourier_mix(shape3(u_ft), _ft_tables(seq))
    y_na, y_ft = y_na.reshape(t, -1), y_ft.reshape(t, -1)
    w_out_bf16 = w_out.astype(bf16)
    rw_hi = router_w.astype(bf16)
    router_w2 = jnp.stack([rw_hi, (router_w - rw_hi.astype(f32)).astype(bf16)])

    tiles = t // TOKEN_TILE
    unit = tiles // sum(MOE_GROUP_SHARES)
    assert unit * sum(MOE_GROUP_SHARES) == tiles
    out = None
    tile0 = 0
    dest = router_b
    for share in MOE_GROUP_SHARES:
        group_tiles = share * unit
        n_blocks = (group_tiles * TOKEN_TILE * TOP_K) // MOE_BLOCK + N_EXPERTS
        x1, h2p, eidx, gates, rank, counts = _mix_out(
            x2d, y_na, y_ft, q_mem, k_mem, v_mem, g_grp, w_out_bf16, g_ffn, router_w2, router_b, seq,
            tile0, group_tiles, dest)
        dest, blk_e, blk_cnt, nxt_e = _slot_layout(counts[:, 0], eidx, rank, n_blocks)
        xs = _dispatch(h2p, dest, n_blocks * MOE_BLOCK)
        y_slots = _experts(xs, blk_e, blk_cnt, nxt_e, w_gu, b_gu, w_down, b_down)
        out = _combine(x1, gates, dest, y_slots, g_final, out, tile0, t)
        tile0 += group_tiles
    return out


def kernel(x, mem, g_mix, g_mem, w_in, w_mem_kv, na_rel_bias, g_grp, w_out, g_ffn, router_w, router_b,
           w_gu, b_gu, w_down, b_down, g_final):
    b, seq, d = x.shape
    depth = w_in.shape[0]
    assert depth == 1, "the final norm is fused into the single layer's combine step"
    out = _layer_and_final_norm(
        x.reshape(b * seq, d), mem, seq, g_mix[0], g_mem[0], w_in[0], w_mem_kv[0], na_rel_bias[0], g_grp[0],
        w_out[0], g_ffn[0], router_w[0], router_b[0], w_gu[0], b_gu[0], w_down[0], b_down[0], g_final)
    return out.reshape(b, seq, d)
```

```python
import numpy as np
import jax
import jax.numpy as jnp
from jax import lax
from jax.experimental import pallas as pl
from jax.experimental.pallas import tpu as pltpu
from jax.experimental.pallas import tpu_sc as plsc

f32 = jnp.float32
bf16 = jnp.bfloat16
u32 = jnp.uint32
i32 = jnp.int32

GRID_W = 64
NA_HEADS = 8
NA_HEAD_DIM = 64
NA_WIN_ROWS = 8
NA_WIN_COLS = 16
FT_GROUPS = 4
FT_GROUP_DIM = 128
MEM_HEADS = 4
MEM_HEAD_DIM = 128
NA_WIDTH = NA_HEADS * NA_HEAD_DIM
FT_WIDTH = FT_GROUPS * FT_GROUP_DIM
MEM_WIDTH = MEM_HEADS * MEM_HEAD_DIM
N_EXPERTS = 32
TOP_K = 4
SWIGLU_LIMIT = 7.0
SWIGLU_ALPHA = 1.702
EPS = 1e-6

LANES = 128
SUBLANES = 8
VMEM_LIMIT_BYTES = 56 * 1024 * 1024
SC_CORES = 2
SC_SUBCORES = 16
SC_GATHER_CHUNK = 64

TOKEN_TILE = 1024
IN_PROJ_TILE = 1024
MOE_BLOCK = 512
MOE_SUB_BLOCK = 128
WEIGHT_CAST_ROWS = 128
MOE_GROUP_SHARES = (5, 3)
NA_ROW_UNROLL = 16
FT_N1 = 64
FT_N2 = 128
FT_K1_BLOCK = 16
FT_BATCH_BLOCK = 4
LOG2_E = 1.4426950408889634
MASK_VALUE = -jnp.inf


def _params(*semantics):
    return pltpu.CompilerParams(dimension_semantics=semantics, vmem_limit_bytes=VMEM_LIMIT_BYTES)


def _rms_scale(x):
    return x * lax.rsqrt(jnp.mean(x * x, axis=-1, keepdims=True) + EPS)


def _softmax_rows(s):
    p = jnp.exp(s - jnp.max(s, axis=-1, keepdims=True))
    return p / jnp.sum(p, axis=-1, keepdims=True)


def _bf16_bits(x):
    return pltpu.bitcast(x.astype(bf16).astype(f32), u32)


def _pack2(lo, hi):
    return (_bf16_bits(lo) >> 16) | (_bf16_bits(hi) & jnp.uint32(0xFFFF0000))


def _unpack2(w):
    return pltpu.bitcast(w << 16, f32), pltpu.bitcast(w & jnp.uint32(0xFFFF0000), f32)


def _pack_bf16_pairs(x):
    n = x.shape[1] // 2
    return _pack2(x[:, :n], x[:, n:])


def _unpack_bf16_pairs(w):
    return jnp.concatenate(_unpack2(w), axis=1)


def _in_proj_kernel(x_ref, g_ref, w_ref, qna_ref, kna_ref, vna_ref, uft_ref, qmem_ref):
    h = _rms_scale(x_ref[...]) * g_ref[...]
    proj = jnp.dot(h.astype(bf16), w_ref[...], preferred_element_type=f32)
    o = NA_WIDTH
    qna_ref[...] = (proj[:, :o] * (NA_HEAD_DIM ** -0.5 * LOG2_E)).astype(bf16)
    kna_ref[...] = proj[:, o:2 * o].astype(bf16)
    vna_ref[...] = proj[:, 2 * o:3 * o].astype(bf16)
    uft_ref[...] = _pack_bf16_pairs(proj[:, 3 * o:3 * o + FT_WIDTH])
    qmem_ref[...] = proj[:, 3 * o + FT_WIDTH:].astype(bf16)


def _in_proj(x2d, g_mix, w_in_bf16):
    t, d = x2d.shape
    tm = IN_PROJ_TILE
    row = lambda w: pl.BlockSpec((tm, w), lambda i: (i, 0))
    return pl.pallas_call(
        _in_proj_kernel,
        grid=(t // tm,),
        in_specs=[row(d), pl.BlockSpec((1, d), lambda i: (0, 0)),
                  pl.BlockSpec(w_in_bf16.shape, lambda i: (0, 0))],
        out_specs=[row(NA_WIDTH), row(NA_WIDTH), row(NA_WIDTH), row(FT_WIDTH // 2), row(MEM_WIDTH)],
        out_shape=[jax.ShapeDtypeStruct((t, NA_WIDTH), bf16)] * 3
        + [jax.ShapeDtypeStruct((t, FT_WIDTH // 2), u32), jax.ShapeDtypeStruct((t, MEM_WIDTH), bf16)],
        compiler_params=_params("parallel"),
        name="in_proj",
    )(x2d, g_mix.reshape(1, d), w_in_bf16)


def _na_bias_table(rel_bias):
    c = np.arange(GRID_W)
    dc_idx = np.clip(c[None, :] - c[:, None], -(NA_WIN_COLS - 1), NA_WIN_COLS - 1) + (NA_WIN_COLS - 1)
    col_start = np.clip(c - NA_WIN_COLS // 2, 0, GRID_W - NA_WIN_COLS)
    col_in = (c[None, :] >= col_start[:, None]) & (c[None, :] < col_start[:, None] + NA_WIN_COLS)
    pick_c = jnp.asarray(dc_idx[:, :, None] == np.arange(2 * NA_WIN_COLS - 1), f32)
    cols = jnp.einsum("hab,qcb->haqc", rel_bias.astype(f32), pick_c, precision=lax.Precision.HIGHEST)
    cols = jnp.where(col_in[None, None], cols * LOG2_E, MASK_VALUE)
    tab = jnp.stack([jnp.concatenate([cols[:, j - s + NA_WIN_ROWS - 1] for j in range(NA_WIN_ROWS)], axis=-1)
                     for s in range(NA_WIN_ROWS)])
    return tab.reshape(NA_WIN_ROWS, NA_HEADS // 2, 2 * GRID_W, NA_WIN_ROWS * GRID_W)


def _na_kernel(q_ref, k_ref, v_ref, bias_ref, o_ref):
    rows = q_ref.shape[1] // GRID_W
    win = NA_WIN_ROWS * GRID_W
    first_head = lax.broadcasted_iota(i32, (GRID_W, 2 * NA_HEAD_DIM), 1) < NA_HEAD_DIM

    def body(it, carry):
        scores, values, q_offsets = [], [], []
        for u in range(NA_ROW_UNROLL):
            r = it * NA_ROW_UNROLL + u
            row_start = jnp.clip(r - NA_WIN_ROWS // 2, 0, rows - NA_WIN_ROWS)
            q0 = pl.multiple_of(r * GRID_W, GRID_W)
            k0 = pl.multiple_of(row_start * GRID_W, GRID_W)
            q = q_ref[0, pl.ds(q0, GRID_W), :]
            zero = jnp.zeros_like(q)
            qm = jnp.concatenate([jnp.where(first_head, q, zero), jnp.where(first_head, zero, q)], axis=0)
            s = lax.dot_general(qm, k_ref[0, pl.ds(k0, win), :], (((1,), (1,)), ((), ())),
                                preferred_element_type=f32)
            scores.append(s + bias_ref[r - row_start, 0])
            values.append(v_ref[0, pl.ds(k0, win), :])
            q_offsets.append(q0)
        s = jnp.concatenate(scores, axis=0)
        p = jnp.exp2(s - jnp.max(s, axis=-1, keepdims=True))
        inv_den = 1.0 / jnp.sum(p, axis=-1, keepdims=True)
        p = p.astype(bf16)
        for u in range(NA_ROW_UNROLL):
            sl = slice(u * 2 * GRID_W, (u + 1) * 2 * GRID_W)
            o = jnp.dot(p[sl], values[u], preferred_element_type=f32) * inv_den[sl]
            o_ref[0, pl.ds(q_offsets[u], GRID_W), :] = jnp.where(
                first_head, o[:GRID_W], o[GRID_W:]).astype(o_ref.dtype)
        return carry

    lax.fori_loop(0, rows // NA_ROW_UNROLL, body, 0)


def _neighbourhood_attention(q, k, v, bias_tab):
    b, s, _ = q.shape
    pair = 2 * NA_HEAD_DIM
    qkv_spec = pl.BlockSpec((1, s, pair), lambda bi, hp: (bi, 0, hp))
    return pl.pallas_call(
        _na_kernel,
        grid=(b, NA_HEADS // 2),
        in_specs=[qkv_spec, qkv_spec, qkv_spec,
                  pl.BlockSpec((NA_WIN_ROWS, 1, 2 * GRID_W, NA_WIN_ROWS * GRID_W), lambda bi, hp: (0, hp, 0, 0))],
        out_specs=qkv_spec,
        out_shape=jax.ShapeDtypeStruct((b, s, NA_WIDTH), bf16),
        compiler_params=_params("parallel", "parallel"),
        name="neighbourhood_attention",
    )(q, k, v, bias_tab)


def _ft_tables(seq):
    assert seq == FT_N1 * FT_N2
    n_blk = FT_N2 // SUBLANES
    k1 = np.arange(FT_N1)[:, None, None, None]
    sr = np.arange(SUBLANES)[None, :, None, None]
    n1 = np.arange(FT_N1)[None, None, :, None]
    sc = np.arange(SUBLANES)[None, None, None, :]
    stage1 = np.zeros((n_blk, 2, FT_N1, SUBLANES, FT_N1, SUBLANES), np.float64)
    for blk in range(n_blk):
        n = FT_N2 * n1 + SUBLANES * blk + sr
        ang = 2.0 * np.pi * ((k1 * n) % seq) / seq
        eye = (sr == sc)
        stage1[blk, 0] = np.cos(ang) * eye
        stage1[blk, 1] = -np.sin(ang) * eye
    stage1 = stage1.reshape(n_blk, 2 * FT_N1 * SUBLANES, FT_N1 * SUBLANES)
    a = np.arange(FT_N2)
    ang2 = 2.0 * np.pi * ((a[:, None] * a[None, :]) % FT_N2) / FT_N2
    c2, s2 = np.cos(ang2), np.sin(ang2)
    stage2 = np.block([[c2, s2], [-s2, c2]])
    g = np.arange(FT_GROUP_DIM)
    angc = 2.0 * np.pi * ((g[:, None] * g[None, :]) % FT_GROUP_DIM) / FT_GROUP_DIM
    norm = 1.0 / np.sqrt(seq * FT_GROUP_DIM)
    chan = np.concatenate([np.cos(angc), np.sin(angc)], axis=0) * norm
    return (jnp.asarray(stage1, bf16), jnp.asarray(stage2, bf16), jnp.asarray(chan, bf16))


def _ft_stage1_kernel(u_ref, m_ref, z_ref):
    rows = FT_N1 * SUBLANES
    nb = u_ref.shape[0]
    u = jnp.concatenate([_unpack_bf16_pairs(u_ref[bb].reshape(rows, FT_WIDTH // 2)) for bb in range(nb)],
                        axis=1).astype(bf16)
    z = jnp.dot(m_ref[0], u, preferred_element_type=f32)
    for bb in range(nb):
        zb = z[:, bb * FT_WIDTH:(bb + 1) * FT_WIDTH]
        z_ref[bb] = _pack2(zb[:rows], zb[rows:]).reshape(FT_N1, SUBLANES, FT_WIDTH)


def _ft_stage2_kernel(z_ref, s2_ref, cs_ref, y_ref):
    gd = FT_GROUP_DIM
    xs = []
    for kk in range(FT_K1_BLOCK):
        zz = jnp.concatenate(_unpack2(z_ref[0, kk]), axis=0).astype(bf16)
        xs.append(jnp.dot(s2_ref[...], zz, preferred_element_type=f32))
    for g in range(FT_GROUPS):
        cols = slice(g * gd, (g + 1) * gd)
        xg = jnp.concatenate([jnp.concatenate([x[:FT_N2, cols], x[FT_N2:, cols]], axis=1) for x in xs], axis=0)
        og = jnp.dot(xg.astype(bf16), cs_ref[...], preferred_element_type=f32)
        for kk in range(FT_K1_BLOCK):
            y_ref[0, kk, :, cols] = og[kk * FT_N2:(kk + 1) * FT_N2].astype(y_ref.dtype)


def _fourier_mix(u_packed, tables):
    u = u_packed
    b, s, _ = u.shape
    c = FT_WIDTH
    assert b % FT_BATCH_BLOCK == 0 and FT_N1 % FT_K1_BLOCK == 0
    stage1, stage2, chan = tables
    n_blk = FT_N2 // SUBLANES
    z = pl.pallas_call(
        _ft_stage1_kernel,
        grid=(n_blk, b // FT_BATCH_BLOCK),
        in_specs=[pl.BlockSpec((FT_BATCH_BLOCK, FT_N1, SUBLANES, c // 2), lambda j, bi: (bi, 0, j, 0)),
                  pl.BlockSpec((1,) + stage1.shape[1:], lambda j, bi: (j, 0, 0))],
        out_specs=pl.BlockSpec((FT_BATCH_BLOCK, FT_N1, SUBLANES, c), lambda j, bi: (bi, 0, j, 0)),
        out_shape=jax.ShapeDtypeStruct((b, FT_N1, FT_N2, c), u32),
        compiler_params=_params("parallel", "parallel"),
        name="fourier_stage1",
    )(u.reshape(b, FT_N1, FT_N2, c // 2), stage1)
    y = pl.pallas_call(
        _ft_stage2_kernel,
        grid=(b, FT_N1 // FT_K1_BLOCK),
        in_specs=[pl.BlockSpec((1, FT_K1_BLOCK, FT_N2, c), lambda bi, kb: (bi, kb, 0, 0)),
                  pl.BlockSpec(stage2.shape, lambda bi, kb: (0, 0)),
                  pl.BlockSpec(chan.shape, lambda bi, kb: (0, 0))],
        out_specs=pl.BlockSpec((1, FT_K1_BLOCK, FT_N2, c), lambda bi, kb: (bi, kb, 0, 0)),
        out_shape=jax.ShapeDtypeStruct((b, FT_N1, FT_N2, c), bf16),
        compiler_params=_params("parallel", "parallel"),
        name="fourier_stage2",
    )(z, stage2, chan)
    return y.transpose(0, 2, 1, 3).reshape(b, s, c)


def _mem_kv_kernel(mem_ref, g_ref, w_ref, k_ref, v_ref):
    mn = _rms_scale(mem_ref[0]) * g_ref[...]
    kv = jnp.dot(mn.astype(bf16), w_ref[...], preferred_element_type=f32)
    k_ref[0] = kv[:, :MEM_WIDTH].astype(bf16)
    v_ref[0] = kv[:, MEM_WIDTH:].astype(bf16)


def _mem_kv(mem, g_mem, w_kv_bf16):
    b, m, d = mem.shape
    kv_spec = pl.BlockSpec((1, m, MEM_WIDTH), lambda bi: (bi, 0, 0))
    return pl.pallas_call(
        _mem_kv_kernel,
        grid=(b,),
        in_specs=[pl.BlockSpec((1, m, d), lambda bi: (bi, 0, 0)), pl.BlockSpec((1, d), lambda bi: (0, 0)),
                  pl.BlockSpec(w_kv_bf16.shape, lambda bi: (0, 0))],
        out_specs=[kv_spec, kv_spec],
        out_shape=[jax.ShapeDtypeStruct((b, m, MEM_WIDTH), bf16)] * 2,
        compiler_params=_params("parallel"),
        name="mem_kv",
    )(mem, g_mem.reshape(1, d), w_kv_bf16)


def _mix_out_kernel(x_ref, yna_ref, yft_ref, qm_ref, km_ref, vm_ref, ggrp_ref, wout_ref, gffn_ref, rw_ref,
                    rb_ref, _order_ref, x1_ref, h2p_ref, eidx_ref, gate_ref, rank_ref, cnt_ref, carry_ref):
    tm = x_ref.shape[0]

    @pl.when(pl.program_id(0) == 0)
    def _():
        carry_ref[...] = jnp.zeros_like(carry_ref)

    q = qm_ref[...]
    km = km_ref[0]
    vm = vm_ref[0]
    heads = []
    for h in range(MEM_HEADS):
        sl = slice(h * MEM_HEAD_DIM, (h + 1) * MEM_HEAD_DIM)
        s = lax.dot_general(q[:, sl], km[:, sl], (((1,), (1,)), ((), ())), preferred_element_type=f32)
        p = _softmax_rows(s * (MEM_HEAD_DIM ** -0.5))
        heads.append(jnp.dot(p.astype(bf16), vm[:, sl], preferred_element_type=f32))
    ymem = jnp.concatenate(heads, axis=1)

    g = ggrp_ref[...]
    a, c = NA_WIDTH, NA_WIDTH + FT_WIDTH
    y = jnp.concatenate([_rms_scale(yna_ref[...].astype(f32)) * g[:, :a],
                         _rms_scale(yft_ref[...].astype(f32)) * g[:, a:c],
                         _rms_scale(ymem) * g[:, c:]], axis=1)
    x1 = x_ref[...] + jnp.dot(y.astype(bf16), wout_ref[...], preferred_element_type=f32)
    x1_ref[...] = x1
    h2 = _rms_scale(x1) * gffn_ref[...]
    h2p_ref[...] = _pack_bf16_pairs(h2)

    h_hi = h2.astype(bf16)
    h_lo = (h2 - h_hi.astype(f32)).astype(bf16)
    hh = jnp.dot(h_hi, rw_ref[...], preferred_element_type=f32)
    logits = (hh[:, :LANES] + hh[:, LANES:]
              + jnp.dot(h_lo, rw_ref[:, :LANES], preferred_element_type=f32)) + rb_ref[...]
    l = logits.T[:N_EXPERTS]
    row = lax.broadcasted_iota(i32, (N_EXPERTS, tm), 0).astype(f32)
    vals, idxs, sels = [], [], []
    for _ in range(TOP_K):
        m = jnp.max(l, axis=0, keepdims=True)
        idx = jnp.min(jnp.where(l == m, row, float(N_EXPERTS)), axis=0, keepdims=True)
        sel = row == idx
        vals.append(m)
        idxs.append(idx)
        sels.append(sel)
        l = jnp.where(sel, -jnp.inf, l)
    ex = [jnp.exp(v - vals[0]) for v in vals]
    den = ex[0] + ex[1] + ex[2] + ex[3]

    onehot = (sels[0] | sels[1] | sels[2] | sels[3]).astype(f32)
    earlier = (lax.broadcasted_iota(i32, (tm, tm), 0) < lax.broadcasted_iota(i32, (tm, tm), 1)).astype(bf16)
    before = jnp.dot(onehot.astype(bf16), earlier, preferred_element_type=f32) + carry_ref[...]
    ranks = [jnp.sum(jnp.where(sel, before, 0.0), axis=0, keepdims=True) for sel in sels]
    carry_ref[...] = carry_ref[...] + jnp.sum(onehot, axis=1, keepdims=True)
    cnt_ref[...] = carry_ref[...].astype(i32)

    eidx_ref[...] = jnp.concatenate(idxs, axis=0).astype(i32)
    rank_ref[...] = jnp.concatenate(ranks, axis=0).astype(i32)
    gates_t = jnp.concatenate([e / den for e in ex] + [jnp.zeros((LANES - TOP_K, tm), f32)], axis=0)
    gate_ref[...] = gates_t.T[:, :TOP_K]


def _mix_out(x2d, y_na, y_ft, q_mem, k_mem, v_mem, g_grp, w_out_bf16, g_ffn, router_w2, router_b, seq,
             tile0, n_tiles, order_after):
    d = x2d.shape[1]
    tm = TOKEN_TILE
    t = n_tiles * tm
    steps_per_batch = seq // tm
    m = k_mem.shape[1]
    row_in = lambda w: pl.BlockSpec((tm, w), lambda i: (i + tile0, 0))
    row_out = lambda w: pl.BlockSpec((tm, w), lambda i: (i, 0))
    full = lambda a: pl.BlockSpec(a.shape, lambda i: (0,) * a.ndim)
    kv_spec = pl.BlockSpec((1, m, MEM_WIDTH), lambda i: ((i + tile0) // steps_per_batch, 0, 0))
    g_grp2, g_ffn2 = g_grp.reshape(1, -1), g_ffn.reshape(1, d)
    rb2 = jnp.pad(router_b.reshape(1, N_EXPERTS), ((0, 0), (0, LANES - N_EXPERTS)))
    router_w2 = jnp.pad(router_w2, ((0, 0), (0, 0), (0, LANES - N_EXPERTS)))
    router_w2 = jnp.concatenate([router_w2[0], router_w2[1]], axis=1)
    col_out = pl.BlockSpec((TOP_K, tm), lambda i: (0, i))
    return pl.pallas_call(
        _mix_out_kernel,
        grid=(n_tiles,),
        in_specs=[row_in(d), row_in(NA_WIDTH), row_in(FT_WIDTH), row_in(MEM_WIDTH), kv_spec, kv_spec,
                  full(g_grp2), full(w_out_bf16), full(g_ffn2), full(router_w2), full(rb2),
                  pl.BlockSpec(memory_space=pl.ANY)],
        out_specs=[row_out(d), row_out(d // 2), col_out, row_out(TOP_K), col_out,
                   pl.BlockSpec((N_EXPERTS, 1), lambda i: (0, 0))],
        out_shape=[jax.ShapeDtypeStruct((t, d), f32), jax.ShapeDtypeStruct((t, d // 2), u32),
                   jax.ShapeDtypeStruct((TOP_K, t), i32), jax.ShapeDtypeStruct((t, TOP_K), f32),
                   jax.ShapeDtypeStruct((TOP_K, t), i32), jax.ShapeDtypeStruct((N_EXPERTS, 1), i32)],
        scratch_shapes=[pltpu.VMEM((N_EXPERTS, 1), f32)],
        compiler_params=_params("arbitrary"),
        name="mix_out_router",
    )(x2d, y_na, y_ft, q_mem, k_mem, v_mem, g_grp2, w_out_bf16, g_ffn2, router_w2, rb2, order_after)


def _sc_mesh():
    return plsc.VectorSubcoreMesh(core_axis_name="c", subcore_axis_name="s",
                                  num_cores=SC_CORES, num_subcores=SC_SUBCORES)


def _dispatch(h2p, dest, n_slots):
    t, w = h2p.shape
    workers = SC_CORES * SC_SUBCORES
    chunk = SC_GATHER_CHUNK
    per = t // workers
    steps = per // chunk
    assert per * workers == t and steps * chunk == per and steps % 2 == 0
    idx = dest.reshape(TOP_K, workers, steps, chunk)

    def body(h_hbm, idx_hbm, out_hbm, idx_v, rows_v, rsem, ssem):
        wid = lax.axis_index("s") * SC_CORES + lax.axis_index("c")
        base = wid * per
        for j in range(TOP_K):
            pltpu.sync_copy(idx_hbm.at[j, wid], idx_v.at[j])

        def read(c, slot):
            return pltpu.make_async_copy(h_hbm.at[pl.ds(base + c * chunk, chunk)], rows_v.at[slot], rsem.at[slot])

        def scatters(c, slot):
            return [pltpu.make_async_copy(rows_v.at[slot], out_hbm.at[idx_v.at[j, c]], ssem.at[slot])
                    for j in range(TOP_K)]

        read(0, 0).start()

        @pl.loop(0, steps, step=2)
        def _(c0):
            for slot in range(2):
                c = c0 + slot
                read(c, slot).wait()
                for cp in scatters(c, slot):
                    cp.start()

                @pl.when(c >= 1)
                def _():
                    for cp in scatters(c - 1, 1 - slot):
                        cp.wait()

                @pl.when(c + 1 < steps)
                def _():
                    read(c + 1, 1 - slot).start()

        for cp in scatters(steps - 1, 1):
            cp.wait()

    return pl.kernel(
        body,
        out_type=jax.ShapeDtypeStruct((n_slots, w), h2p.dtype),
        mesh=_sc_mesh(),
        scratch_types=[pltpu.VMEM((TOP_K, steps, chunk), i32), pltpu.VMEM((2, chunk, w), h2p.dtype),
                       pltpu.SemaphoreType.DMA((2,)), pltpu.SemaphoreType.DMA((2,))],
        name="sc_dispatch_rows",
    )(h2p, idx)


def _expert_kernel(blk_e_ref, blk_cnt_ref, nxt_e_ref, xs_ref, wgu_hbm, bgu_ref, wd_hbm, bd_ref, y_ref,
                   wgu_f32, wd_f32, wgu_bf, wd_bf, sem):
    b = pl.program_id(0)
    e = blk_e_ref[b]
    cnt = blk_cnt_ref[b]
    bm = xs_ref.shape[0]
    de = wd_f32.shape[0]

    def fetch(expert):
        return (pltpu.make_async_copy(wgu_hbm.at[expert], wgu_f32, sem.at[0]),
                pltpu.make_async_copy(wd_hbm.at[expert], wd_f32, sem.at[1]))

    @pl.when(b == 0)
    def _():
        for cp in fetch(e):
            cp.start()

    @pl.when(jnp.logical_or(b == 0, e != blk_e_ref[jnp.maximum(b - 1, 0)]))
    def _():
        for cp in fetch(e):
            cp.wait()

        def convert(i, carry):
            rows = pl.ds(pl.multiple_of(i * WEIGHT_CAST_ROWS, WEIGHT_CAST_ROWS), WEIGHT_CAST_ROWS)
            wgu_bf[rows, :] = wgu_f32[rows, :].astype(bf16)
            wd_bf[rows, :] = wd_f32[rows, :].astype(bf16)
            return carry

        lax.fori_loop(0, wgu_f32.shape[0] // WEIGHT_CAST_ROWS, convert, 0)

        @pl.when(nxt_e_ref[b] >= 0)
        def _():
            for cp in fetch(nxt_e_ref[b]):
                cp.start()

    def ffn_rows(r0, n):
        rows = pl.ds(r0, n)
        valid = r0 + lax.broadcasted_iota(i32, (n, 1), 0) < cnt
        x = jnp.where(valid, _unpack_bf16_pairs(xs_ref[rows, :]), 0.0).astype(bf16)
        gu = jnp.dot(x, wgu_bf[...], preferred_element_type=f32) + bgu_ref[0]
        x_glu = jnp.minimum(gu[:, :de], SWIGLU_LIMIT)
        x_lin = jnp.clip(gu[:, de:], -SWIGLU_LIMIT, SWIGLU_LIMIT)
        act = x_glu * (1.0 / (1.0 + jnp.exp(-SWIGLU_ALPHA * x_glu))) * (x_lin + 1.0)
        y = jnp.dot(act.astype(bf16), wd_bf[...], preferred_element_type=f32) + bd_ref[0]
        y_ref[rows, :] = _pack_bf16_pairs(y)

    @pl.when(cnt == bm)
    def _():
        ffn_rows(0, bm)

    @pl.when(cnt < bm)
    def _():
        y_ref[...] = jnp.zeros_like(y_ref)

        def piece(i, carry):
            ffn_rows(pl.multiple_of(i * MOE_SUB_BLOCK, MOE_SUB_BLOCK), MOE_SUB_BLOCK)
            return carry

        lax.fori_loop(0, (cnt + MOE_SUB_BLOCK - 1) // MOE_SUB_BLOCK, piece, 0)


def _experts(xs, blk_e, blk_cnt, nxt_e, w_gu, b_gu, w_down, b_down):
    n_slots, w = xs.shape
    bm = MOE_BLOCK
    e, d, de2 = w_gu.shape
    de = w_down.shape[1]
    assert de == d, "one row loop converts both weight matrices"
    last = n_slots // bm - 1

    def rows_of_block(b, be, bc, ne):
        return (jnp.where(bc[b] > 0, b, last), 0)

    grid_spec = pltpu.PrefetchScalarGridSpec(
        num_scalar_prefetch=3,
        grid=(n_slots // bm,),
        in_specs=[pl.BlockSpec((bm, w), rows_of_block),
                  pl.BlockSpec(memory_space=pl.ANY),
                  pl.BlockSpec((1, 1, de2), lambda b, be, bc, ne: (be[b], 0, 0)),
                  pl.BlockSpec(memory_space=pl.ANY),
                  pl.BlockSpec((1, 1, d), lambda b, be, bc, ne: (be[b], 0, 0))],
        out_specs=pl.BlockSpec((bm, w), rows_of_block),
        scratch_shapes=[pltpu.VMEM((d, de2), f32), pltpu.VMEM((de, d), f32),
                        pltpu.VMEM((d, de2), bf16), pltpu.VMEM((de, d), bf16),
                        pltpu.SemaphoreType.DMA((2,))],
    )
    return pl.pallas_call(
        _expert_kernel,
        grid_spec=grid_spec,
        out_shape=jax.ShapeDtypeStruct((n_slots, w), u32),
        compiler_params=_params("arbitrary"),
        name="moe_experts",
    )(blk_e, blk_cnt, nxt_e, xs, w_gu, b_gu.reshape(e, 1, de2), w_down, b_down.reshape(e, 1, d))


def _sc_gather_rows(table, idx):
    n, w = idx.shape[0], table.shape[1]
    workers = SC_CORES * SC_SUBCORES
    chunk = SC_GATHER_CHUNK
    per = n // workers
    steps = per // chunk
    assert per * workers == n and steps * chunk == per and steps % 2 == 0

    def body(table_hbm, idx_hbm, out_hbm, idx_v, rows_v, gsem, wsem):
        base = (lax.axis_index("s") * SC_CORES + lax.axis_index("c")) * per
        pltpu.sync_copy(idx_hbm.at[pl.ds(base, per)], idx_v)

        def gather(c, slot):
            return pltpu.make_async_copy(table_hbm.at[idx_v.at[pl.ds(c * chunk, chunk)]], rows_v.at[slot],
                                         gsem.at[slot])

        def write(c, slot):
            return pltpu.make_async_copy(rows_v.at[slot], out_hbm.at[pl.ds(base + c * chunk, chunk)],
                                         wsem.at[slot])

        gather(0, 0).start()

        @pl.loop(0, steps, step=2)
        def _(c0):
            for slot in range(2):
                c = c0 + slot
                gather(c, slot).wait()
                write(c, slot).start()

                @pl.when(c >= 1)
                def _():
                    write(c - 1, 1 - slot).wait()

                @pl.when(c + 1 < steps)
                def _():
                    gather(c + 1, 1 - slot).start()

        write(steps - 1, 1).wait()

    return pl.kernel(
        body,
        out_type=jax.ShapeDtypeStruct((n, w), table.dtype),
        mesh=_sc_mesh(),
        scratch_types=[pltpu.VMEM((per,), i32), pltpu.VMEM((2, chunk, w), table.dtype),
                       pltpu.SemaphoreType.DMA((2,)), pltpu.SemaphoreType.DMA((2,))],
        name="sc_gather_rows",
    )(table, idx)


def _combine_kernel(x1_ref, gate_ref, gfin_ref, yg_ref, *rest):
    o_ref = rest[-1]
    acc = x1_ref[...]
    gates = gate_ref[...]
    for j in range(TOP_K):
        acc = acc + gates[:, j:j + 1] * _unpack_bf16_pairs(yg_ref[j])
    o_ref[...] = _rms_scale(acc) * gfin_ref[...]


def _combine(x1, gates, dest, y_slots, g_final, out_prev, tile0, total_tokens):
    t, d = x1.shape
    tm = TOKEN_TILE
    w = y_slots.shape[1]
    yg = _sc_gather_rows(y_slots, dest.reshape(-1)).reshape(TOP_K, t, w)
    in_specs = [pl.BlockSpec((tm, d), lambda i: (i, 0)),
                pl.BlockSpec((tm, TOP_K), lambda i: (i, 0)),
                pl.BlockSpec((1, d), lambda i: (0, 0)),
                pl.BlockSpec((TOP_K, tm, w), lambda i: (0, i, 0))]
    args = [x1, gates, g_final.reshape(1, d), yg]
    aliases = {}
    if out_prev is not None:
        in_specs.append(pl.BlockSpec(memory_space=pl.ANY))
        args.append(out_prev)
        aliases = {len(args) - 1: 0}
    return pl.pallas_call(
        _combine_kernel,
        grid=(t // tm,),
        in_specs=in_specs,
        out_specs=pl.BlockSpec((tm, d), lambda i: (i + tile0, 0)),
        out_shape=jax.ShapeDtypeStruct((total_tokens, d), f32),
        input_output_aliases=aliases,
        compiler_params=_params("parallel"),
        name="moe_combine",
    )(*args)


def _dest_kernel(start_ref, eidx_ref, rank_ref, dest_ref):
    eidx = eidx_ref[...]
    dest = rank_ref[...]
    for e in range(N_EXPERTS):
        dest = dest + jnp.where(eidx == e, start_ref[e], 0)
    dest_ref[...] = dest


def _slot_layout(counts, eidx, rank, n_blocks):
    bm = MOE_BLOCK
    padded = (counts + bm - 1) // bm * bm
    padded_end = jnp.cumsum(padded)
    start = padded_end - padded
    experts = jnp.arange(N_EXPERTS, dtype=i32)
    lookup = lambda table, idx: jnp.sum(jnp.where(idx[..., None] == experts, table, 0), axis=-1)
    dest = pl.pallas_call(
        _dest_kernel,
        grid_spec=pltpu.PrefetchScalarGridSpec(
            num_scalar_prefetch=1, grid=(1,),
            in_specs=[pl.BlockSpec(eidx.shape, lambda i, st: (0, 0)), pl.BlockSpec(rank.shape, lambda i, st: (0, 0))],
            out_specs=pl.BlockSpec(rank.shape, lambda i, st: (0, 0))),
        out_shape=jax.ShapeDtypeStruct(rank.shape, i32),
        name="moe_dest",
    )(start.astype(i32), eidx, rank)
    blk_row = jnp.arange(n_blocks, dtype=i32) * bm
    blk_e = jnp.minimum(jnp.sum((padded_end[None, :] <= blk_row[:, None]).astype(i32), axis=1), N_EXPERTS - 1)
    blk_cnt = jnp.clip(lookup(counts, blk_e) - (blk_row - lookup(start, blk_e)), 0, bm).astype(i32)
    none = jnp.int32(N_EXPERTS)
    nxt_e = jnp.min(jnp.where(blk_e[None, :] > blk_e[:, None], blk_e[None, :], none), axis=1)
    nxt_e = jnp.where(nxt_e == none, -1, nxt_e).astype(i32)
    return dest, blk_e, blk_cnt, nxt_e


def _layer_and_final_norm(x2d, mem, seq, g_mix, g_mem, w_in, w_mem_kv, na_rel_bias, g_grp, w_out, g_ffn,
                          router_w, router_b, w_gu, b_gu, w_down, b_down, g_final):
    t, d = x2d.shape
    b = t // seq
    q_na, k_na, v_na, u_ft, q_mem = _in_proj(x2d, g_mix, w_in.astype(bf16))
    k_mem, v_mem = _mem_kv(mem, g_mem, w_mem_kv.astype(bf16))
    shape3 = lambda a: a.reshape(b, seq, a.shape[-1])
    y_na = _neighbourhood_attention(shape3(q_na), shape3(k_na), shape3(v_na), _na_bias_table(na_rel_bias))
    y_ft = _fourier_mix(shape3(u_ft), _ft_tables(seq))
    y_na, y_ft = y_na.reshape(t, -1), y_ft.reshape(t, -1)
    w_out_bf16 = w_out.astype(bf16)
    rw_hi = router_w.astype(bf16)
    router_w2 = jnp.stack([rw_hi, (router_w - rw_hi.astype(f32)).astype(bf16)])

    tiles = t // TOKEN_TILE
    unit = tiles // sum(MOE_GROUP_SHARES)
    assert unit * sum(MOE_GROUP_SHARES) == tiles
    out = None
    tile0 = 0
    dest = router_b
    for share in MOE_GROUP_SHARES:
        group_tiles = share * unit
        n_blocks = (group_tiles * TOKEN_TILE * TOP_K) // MOE_BLOCK + N_EXPERTS
        x1, h2p, eidx, gates, rank, counts = _mix_out(
            x2d, y_na, y_ft, q_mem, k_mem, v_mem, g_grp, w_out_bf16, g_ffn, router_w2, router_b, seq,
            tile0, group_tiles, dest)
        dest, blk_e, blk_cnt, nxt_e = _slot_layout(counts[:, 0], eidx, rank, n_blocks)
        xs = _dispatch(h2p, dest, n_blocks * MOE_BLOCK)
        y_slots = _experts(xs, blk_e, blk_cnt, nxt_e, w_gu, b_gu, w_down, b_down)
        out = _combine(x1, gates, dest, y_slots, g_final, out, tile0, t)
        tile0 += group_tiles
    return out


def kernel(x, mem, g_mix, g_mem, w_in, w_mem_kv, na_rel_bias, g_grp, w_out, g_ffn, router_w, router_b,
           w_gu, b_gu, w_down, b_down, g_final):
    b, seq, d = x.shape
    depth = w_in.shape[0]
    assert depth == 1, "the final norm is fused into the single layer's combine step"
    out = _layer_and_final_norm(
        x.reshape(b * seq, d), mem, seq, g_mix[0], g_mem[0], w_in[0], w_mem_kv[0], na_rel_bias[0], g_grp[0],
        w_out[0], g_ffn[0], router_w[0], router_b[0], w_gu[0], b_gu[0], w_down[0], b_down[0], g_final)
    return out.reshape(b, seq, d)
```

```python
import numpy as np
import jax
import jax.numpy as jnp
from jax import lax
from jax.experimental import pallas as pl
from jax.experimental.pallas import tpu as pltpu
from jax.experimental.pallas import tpu_sc as plsc

f32 = jnp.float32
bf16 = jnp.bfloat16
u32 = jnp.uint32
i32 = jnp.int32

GRID_W = 64
NA_HEADS = 8
NA_HEAD_DIM = 64
NA_WIN_ROWS = 8
NA_WIN_COLS = 16
FT_GROUPS = 4
FT_GROUP_DIM = 128
MEM_HEADS = 4
MEM_HEAD_DIM = 128
NA_WIDTH = NA_HEADS * NA_HEAD_DIM
FT_WIDTH = FT_GROUPS * FT_GROUP_DIM
MEM_WIDTH = MEM_HEADS * MEM_HEAD_DIM
N_EXPERTS = 32
TOP_K = 4
SWIGLU_LIMIT = 7.0
SWIGLU_ALPHA = 1.702
EPS = 1e-6

LANES = 128
SUBLANES = 8
VMEM_LIMIT_BYTES = 56 * 1024 * 1024
SC_CORES = 2
SC_SUBCORES = 16
SC_GATHER_CHUNK = 64

TOKEN_TILE = 1024
IN_PROJ_TILE = 1024
MOE_BLOCK = 512
MOE_SUB_BLOCK = 128
WEIGHT_CAST_ROWS = 128
MOE_GROUP_SHARES = (5, 3)
NA_ROW_UNROLL = 16
FT_N1 = 64
FT_N2 = 128
FT_K1_BLOCK = 16
FT_BATCH_BLOCK = 4
LOG2_E = 1.4426950408889634
MASK_VALUE = -jnp.inf


def _params(*semantics):
    return pltpu.CompilerParams(dimension_semantics=semantics, vmem_limit_bytes=VMEM_LIMIT_BYTES)


def _rms_scale(x):
    return x * lax.rsqrt(jnp.mean(x * x, axis=-1, keepdims=True) + EPS)


def _softmax_rows(s):
    p = jnp.exp(s - jnp.max(s, axis=-1, keepdims=True))
    return p / jnp.sum(p, axis=-1, keepdims=True)


def _bf16_bits(x):
    return pltpu.bitcast(x.astype(bf16).astype(f32), u32)


def _pack2(lo, hi):
    return (_bf16_bits(lo) >> 16) | (_bf16_bits(hi) & jnp.uint32(0xFFFF0000))


def _unpack2(w):
    return pltpu.bitcast(w << 16, f32), pltpu.bitcast(w & jnp.uint32(0xFFFF0000), f32)


def _pack_bf16_pairs(x):
    n = x.shape[1] // 2
    return _pack2(x[:, :n], x[:, n:])


def _unpack_bf16_pairs(w):
    return jnp.concatenate(_unpack2(w), axis=1)


def _in_proj_kernel(x_ref, g_ref, w_ref, qna_ref, kna_ref, vna_ref, uft_ref, qmem_ref):
    h = _rms_scale(x_ref[...]) * g_ref[...]
    proj = jnp.dot(h.astype(bf16), w_ref[...], preferred_element_type=f32)
    o = NA_WIDTH
    qna_ref[...] = (proj[:, :o] * (NA_HEAD_DIM ** -0.5 * LOG2_E)).astype(bf16)
    kna_ref[...] = proj[:, o:2 * o].astype(bf16)
    vna_ref[...] = proj[:, 2 * o:3 * o].astype(bf16)
    uft_ref[...] = _pack_bf16_pairs(proj[:, 3 * o:3 * o + FT_WIDTH])
    qmem_ref[...] = proj[:, 3 * o + FT_WIDTH:].astype(bf16)


def _in_proj(x2d, g_mix, w_in_bf16):
    t, d = x2d.shape
    tm = IN_PROJ_TILE
    row = lambda w: pl.BlockSpec((tm, w), lambda i: (i, 0))
    return pl.pallas_call(
        _in_proj_kernel,
        grid=(t // tm,),
        in_specs=[row(d), pl.BlockSpec((1, d), lambda i: (0, 0)),
                  pl.BlockSpec(w_in_bf16.shape, lambda i: (0, 0))],
        out_specs=[row(NA_WIDTH), row(NA_WIDTH), row(NA_WIDTH), row(FT_WIDTH // 2), row(MEM_WIDTH)],
        out_shape=[jax.ShapeDtypeStruct((t, NA_WIDTH), bf16)] * 3
        + [jax.ShapeDtypeStruct((t, FT_WIDTH // 2), u32), jax.ShapeDtypeStruct((t, MEM_WIDTH), bf16)],
        compiler_params=_params("parallel"),
        name="in_proj",
    )(x2d, g_mix.reshape(1, d), w_in_bf16)


def _na_bias_table(rel_bias):
    c = np.arange(GRID_W)
    dc_idx = np.clip(c[None, :] - c[:, None], -(NA_WIN_COLS - 1), NA_WIN_COLS - 1) + (NA_WIN_COLS - 1)
    col_start = np.clip(c - NA_WIN_COLS // 2, 0, GRID_W - NA_WIN_COLS)
    col_in = (c[None, :] >= col_start[:, None]) & (c[None, :] < col_start[:, None] + NA_WIN_COLS)
    pick_c = jnp.asarray(dc_idx[:, :, None] == np.arange(2 * NA_WIN_COLS - 1), f32)
    cols = jnp.einsum("hab,qcb->haqc", rel_bias.astype(f32), pick_c, precision=lax.Precision.HIGHEST)
    cols = jnp.where(col_in[None, None], cols * LOG2_E, MASK_VALUE)
    tab = pl.pallas_call(
        _na_bias_expand_kernel,
        grid=(NA_WIN_ROWS,),
        in_specs=[pl.BlockSpec(cols.shape, lambda s: (0, 0, 0, 0))],
        out_specs=pl.BlockSpec((1, NA_HEADS, GRID_W, NA_WIN_ROWS * GRID_W), lambda s: (s, 0, 0, 0)),
        out_shape=jax.ShapeDtypeStruct((NA_WIN_ROWS, NA_HEADS, GRID_W, NA_WIN_ROWS * GRID_W), f32),
        compiler_params=_params("parallel"),
        name="na_bias_expand",
    )(cols)
    return tab.reshape(NA_WIN_ROWS, NA_HEADS // 2, 2 * GRID_W, NA_WIN_ROWS * GRID_W)


def _na_bias_expand_kernel(cols_ref, tab_ref):
    s = pl.program_id(0)
    for j in range(0, NA_WIN_ROWS, 2):
        a = j - s + (NA_WIN_ROWS - 1)
        pair = jnp.concatenate([cols_ref[:, pl.ds(a, 1)], cols_ref[:, pl.ds(a + 1, 1)]], axis=-1)
        tab_ref[0, :, :, pl.ds(j * GRID_W, 2 * GRID_W)] = pair[:, 0]


def _na_kernel(q_ref, k_ref, v_ref, bias_ref, o_ref):
    rows = q_ref.shape[1] // GRID_W
    win = NA_WIN_ROWS * GRID_W
    first_head = lax.broadcasted_iota(i32, (GRID_W, 2 * NA_HEAD_DIM), 1) < NA_HEAD_DIM

    def body(it, carry):
        scores, values, q_offsets = [], [], []
        for u in range(NA_ROW_UNROLL):
            r = it * NA_ROW_UNROLL + u
            row_start = jnp.clip(r - NA_WIN_ROWS // 2, 0, rows - NA_WIN_ROWS)
            q0 = pl.multiple_of(r * GRID_W, GRID_W)
            k0 = pl.multiple_of(row_start * GRID_W, GRID_W)
            q = q_ref[0, pl.ds(q0, GRID_W), :]
            zero = jnp.zeros_like(q)
            qm = jnp.concatenate([jnp.where(first_head, q, zero), jnp.where(first_head, zero, q)], axis=0)
            s = lax.dot_general(qm, k_ref[0, pl.ds(k0, win), :], (((1,), (1,)), ((), ())),
                                preferred_element_type=f32)
            scores.append(s + bias_ref[r - row_start, 0])
            values.append(v_ref[0, pl.ds(k0, win), :])
            q_offsets.append(q0)
        s = jnp.concatenate(scores, axis=0)
        p = jnp.exp2(s - jnp.max(s, axis=-1, keepdims=True))
        inv_den = 1.0 / jnp.sum(p, axis=-1, keepdims=True)
        p = p.astype(bf16)
        for u in range(NA_ROW_UNROLL):
            sl = slice(u * 2 * GRID_W, (u + 1) * 2 * GRID_W)
            o = jnp.dot(p[sl], values[u], preferred_element_type=f32) * inv_den[sl]
            o_ref[0, pl.ds(q_offsets[u], GRID_W), :] = jnp.where(
                first_head, o[:GRID_W], o[GRID_W:]).astype(o_ref.dtype)
        return carry

    lax.fori_loop(0, rows // NA_ROW_UNROLL, body, 0)


def _neighbourhood_attention(q, k, v, bias_tab):
    b, s, _ = q.shape
    pair = 2 * NA_HEAD_DIM
    qkv_spec = pl.BlockSpec((1, s, pair), lambda bi, hp: (bi, 0, hp))
    return pl.pallas_call(
        _na_kernel,
        grid=(b, NA_HEADS // 2),
        in_specs=[qkv_spec, qkv_spec, qkv_spec,
                  pl.BlockSpec((NA_WIN_ROWS, 1, 2 * GRID_W, NA_WIN_ROWS * GRID_W), lambda bi, hp: (0, hp, 0, 0))],
        out_specs=qkv_spec,
        out_shape=jax.ShapeDtypeStruct((b, s, NA_WIDTH), bf16),
        compiler_params=_params("parallel", "parallel"),
        name="neighbourhood_attention",
    )(q, k, v, bias_tab)


def _ft_tables(seq):
    assert seq == FT_N1 * FT_N2
    n_blk = FT_N2 // SUBLANES
    k1 = np.arange(FT_N1)[:, None, None, None]
    sr = np.arange(SUBLANES)[None, :, None, None]
    n1 = np.arange(FT_N1)[None, None, :, None]
    sc = np.arange(SUBLANES)[None, None, None, :]
    stage1 = np.zeros((n_blk, 2, FT_N1, SUBLANES, FT_N1, SUBLANES), np.float64)
    for blk in range(n_blk):
        n = FT_N2 * n1 + SUBLANES * blk + sr
        ang = 2.0 * np.pi * ((k1 * n) % seq) / seq
        eye = (sr == sc)
        stage1[blk, 0] = np.cos(ang) * eye
        stage1[blk, 1] = -np.sin(ang) * eye
    stage1 = stage1.reshape(n_blk, 2 * FT_N1 * SUBLANES, FT_N1 * SUBLANES)
    a = np.arange(FT_N2)
    ang2 = 2.0 * np.pi * ((a[:, None] * a[None, :]) % FT_N2) / FT_N2
    c2, s2 = np.cos(ang2), np.sin(ang2)
    stage2 = np.block([[c2, s2], [-s2, c2]])
    g = np.arange(FT_GROUP_DIM)
    angc = 2.0 * np.pi * ((g[:, None] * g[None, :]) % FT_GROUP_DIM) / FT_GROUP_DIM
    norm = 1.0 / np.sqrt(seq * FT_GROUP_DIM)
    chan = np.concatenate([np.cos(angc), np.sin(angc)], axis=0) * norm
    return (jnp.asarray(stage1, bf16), jnp.asarray(stage2, bf16), jnp.asarray(chan, bf16))


def _ft_stage1_kernel(u_ref, m_ref, z_ref):
    rows = FT_N1 * SUBLANES
    nb = u_ref.shape[0]
    u = jnp.concatenate([_unpack_bf16_pairs(u_ref[bb].reshape(rows, FT_WIDTH // 2)) for bb in range(nb)],
                        axis=1).astype(bf16)
    z = jnp.dot(m_ref[0], u, preferred_element_type=f32)
    for bb in range(nb):
        zb = z[:, bb * FT_WIDTH:(bb + 1) * FT_WIDTH]
        z_ref[bb] = _pack2(zb[:rows], zb[rows:]).reshape(FT_N1, SUBLANES, FT_WIDTH)


def _ft_stage2_kernel(z_ref, s2_ref, cs_ref, y_ref):
    gd = FT_GROUP_DIM
    xs = []
    for kk in range(FT_K1_BLOCK):
        zz = jnp.concatenate(_unpack2(z_ref[0, kk]), axis=0).astype(bf16)
        xs.append(jnp.dot(s2_ref[...], zz, preferred_element_type=f32))
    for g in range(FT_GROUPS):
        cols = slice(g * gd, (g + 1) * gd)
        xg = jnp.concatenate([jnp.concatenate([x[:FT_N2, cols], x[FT_N2:, cols]], axis=1) for x in xs], axis=0)
        og = jnp.dot(xg.astype(bf16), cs_ref[...], preferred_element_type=f32)
        for kk in range(FT_K1_BLOCK):
            y_ref[0, kk, :, cols] = og[kk * FT_N2:(kk + 1) * FT_N2].astype(y_ref.dtype)


def _fourier_mix(u_packed, tables):
    u = u_packed
    b, s, _ = u.shape
    c = FT_WIDTH
    assert b % FT_BATCH_BLOCK == 0 and FT_N1 % FT_K1_BLOCK == 0
    stage1, stage2, chan = tables
    n_blk = FT_N2 // SUBLANES
    z = pl.pallas_call(
        _ft_stage1_kernel,
        grid=(n_blk, b // FT_BATCH_BLOCK),
        in_specs=[pl.BlockSpec((FT_BATCH_BLOCK, FT_N1, SUBLANES, c // 2), lambda j, bi: (bi, 0, j, 0)),
                  pl.BlockSpec((1,) + stage1.shape[1:], lambda j, bi: (j, 0, 0))],
        out_specs=pl.BlockSpec((FT_BATCH_BLOCK, FT_N1, SUBLANES, c), lambda j, bi: (bi, 0, j, 0)),
        out_shape=jax.ShapeDtypeStruct((b, FT_N1, FT_N2, c), u32),
        compiler_params=_params("parallel", "parallel"),
        name="fourier_stage1",
    )(u.reshape(b, FT_N1, FT_N2, c // 2), stage1)
    y = pl.pallas_call(
        _ft_stage2_kernel,
        grid=(b, FT_N1 // FT_K1_BLOCK),
        in_specs=[pl.BlockSpec((1, FT_K1_BLOCK, FT_N2, c), lambda bi, kb: (bi, kb, 0, 0)),
                  pl.BlockSpec(stage2.shape, lambda bi, kb: (0, 0)),
                  pl.BlockSpec(chan.shape, lambda bi, kb: (0, 0))],
        out_specs=pl.BlockSpec((1, FT_K1_BLOCK, FT_N2, c), lambda bi, kb: (bi, kb, 0, 0)),
        out_shape=jax.ShapeDtypeStruct((b, FT_N1, FT_N2, c), bf16),
        compiler_params=_params("parallel", "parallel"),
        name="fourier_stage2",
    )(z, stage2, chan)
    return y.transpose(0, 2, 1, 3).reshape(b, s, c)


def _mem_kv_kernel(mem_ref, g_ref, w_ref, k_ref, v_ref):
    mn = _rms_scale(mem_ref[0]) * g_ref[...]
    kv = jnp.dot(mn.astype(bf16), w_ref[...], preferred_element_type=f32)
    k_ref[0] = kv[:, :MEM_WIDTH].astype(bf16)
    v_ref[0] = kv[:, MEM_WIDTH:].astype(bf16)


def _mem_kv(mem, g_mem, w_kv_bf16):
    b, m, d = mem.shape
    kv_spec = pl.BlockSpec((1, m, MEM_WIDTH), lambda bi: (bi, 0, 0))
    return pl.pallas_call(
        _mem_kv_kernel,
        grid=(b,),
        in_specs=[pl.BlockSpec((1, m, d), lambda bi: (bi, 0, 0)), pl.BlockSpec((1, d), lambda bi: (0, 0)),
                  pl.BlockSpec(w_kv_bf16.shape, lambda bi: (0, 0))],
        out_specs=[kv_spec, kv_spec],
        out_shape=[jax.ShapeDtypeStruct((b, m, MEM_WIDTH), bf16)] * 2,
        compiler_params=_params("parallel"),
        name="mem_kv",
    )(mem, g_mem.reshape(1, d), w_kv_bf16)


def _mix_out_kernel(x_ref, yna_ref, yft_ref, qm_ref, km_ref, vm_ref, ggrp_ref, wout_ref, gffn_ref, rw_ref,
                    rb_ref, _order_ref, x1_ref, h2p_ref, eidx_ref, gate_ref, rank_ref, cnt_ref, carry_ref):
    tm = x_ref.shape[0]

    @pl.when(pl.program_id(0) == 0)
    def _():
        carry_ref[...] = jnp.zeros_like(carry_ref)

    q = qm_ref[...]
    km = km_ref[0]
    vm = vm_ref[0]
    heads = []
    for h in range(MEM_HEADS):
        sl = slice(h * MEM_HEAD_DIM, (h + 1) * MEM_HEAD_DIM)
        s = lax.dot_general(q[:, sl], km[:, sl], (((1,), (1,)), ((), ())), preferred_element_type=f32)
        p = _softmax_rows(s * (MEM_HEAD_DIM ** -0.5))
        heads.append(jnp.dot(p.astype(bf16), vm[:, sl], preferred_element_type=f32))
    ymem = jnp.concatenate(heads, axis=1)

    g = ggrp_ref[...]
    a, c = NA_WIDTH, NA_WIDTH + FT_WIDTH
    y = jnp.concatenate([_rms_scale(yna_ref[...].astype(f32)) * g[:, :a],
                         _rms_scale(yft_ref[...].astype(f32)) * g[:, a:c],
                         _rms_scale(ymem) * g[:, c:]], axis=1)
    x1 = x_ref[...] + jnp.dot(y.astype(bf16), wout_ref[...], preferred_element_type=f32)
    x1_ref[...] = x1
    h2 = _rms_scale(x1) * gffn_ref[...]
    h2p_ref[...] = _pack_bf16_pairs(h2)

    h_hi = h2.astype(bf16)
    h_lo = (h2 - h_hi.astype(f32)).astype(bf16)
    hh = jnp.dot(h_hi, rw_ref[...], preferred_element_type=f32)
    logits = (hh[:, :LANES] + hh[:, LANES:]
              + jnp.dot(h_lo, rw_ref[:, :LANES], preferred_element_type=f32)) + rb_ref[...]
    l = logits.T[:N_EXPERTS]
    row = lax.broadcasted_iota(i32, (N_EXPERTS, tm), 0).astype(f32)
    vals, idxs, sels = [], [], []
    for _ in range(TOP_K):
        m = jnp.max(l, axis=0, keepdims=True)
        idx = jnp.min(jnp.where(l == m, row, float(N_EXPERTS)), axis=0, keepdims=True)
        sel = row == idx
        vals.append(m)
        idxs.append(idx)
        sels.append(sel)
        l = jnp.where(sel, -jnp.inf, l)
    ex = [jnp.exp(v - vals[0]) for v in vals]
    den = ex[0] + ex[1] + ex[2] + ex[3]

    onehot = (sels[0] | sels[1] | sels[2] | sels[3]).astype(f32)
    earlier = (lax.broadcasted_iota(i32, (tm, tm), 0) < lax.broadcasted_iota(i32, (tm, tm), 1)).astype(bf16)
    before = jnp.dot(onehot.astype(bf16), earlier, preferred_element_type=f32) + carry_ref[...]
    ranks = [jnp.sum(jnp.where(sel, before, 0.0), axis=0, keepdims=True) for sel in sels]
    carry_ref[...] = carry_ref[...] + jnp.sum(onehot, axis=1, keepdims=True)
    cnt_ref[...] = carry_ref[...].astype(i32)

    eidx_ref[...] = jnp.concatenate(idxs, axis=0).astype(i32)
    rank_ref[...] = jnp.concatenate(ranks, axis=0).astype(i32)
    gates_t = jnp.concatenate([e / den for e in ex] + [jnp.zeros((LANES - TOP_K, tm), f32)], axis=0)
    gate_ref[...] = gates_t.T[:, :TOP_K]


def _mix_out(x2d, y_na, y_ft, q_mem, k_mem, v_mem, g_grp, w_out_bf16, g_ffn, router_w2, router_b, seq,
             tile0, n_tiles, order_after):
    d = x2d.shape[1]
    tm = TOKEN_TILE
    t = n_tiles * tm
    steps_per_batch = seq // tm
    m = k_mem.shape[1]
    row_in = lambda w: pl.BlockSpec((tm, w), lambda i: (i + tile0, 0))
    row_out = lambda w: pl.BlockSpec((tm, w), lambda i: (i, 0))
    full = lambda a: pl.BlockSpec(a.shape, lambda i: (0,) * a.ndim)
    kv_spec = pl.BlockSpec((1, m, MEM_WIDTH), lambda i: ((i + tile0) // steps_per_batch, 0, 0))
    g_grp2, g_ffn2 = g_grp.reshape(1, -1), g_ffn.reshape(1, d)
    rb2 = jnp.pad(router_b.reshape(1, N_EXPERTS), ((0, 0), (0, LANES - N_EXPERTS)))
    router_w2 = jnp.pad(router_w2, ((0, 0), (0, 0), (0, LANES - N_EXPERTS)))
    router_w2 = jnp.concatenate([router_w2[0], router_w2[1]], axis=1)
    col_out = pl.BlockSpec((TOP_K, tm), lambda i: (0, i))
    return pl.pallas_call(
        _mix_out_kernel,
        grid=(n_tiles,),
        in_specs=[row_in(d), row_in(NA_WIDTH), row_in(FT_WIDTH), row_in(MEM_WIDTH), kv_spec, kv_spec,
                  full(g_grp2), full(w_out_bf16), full(g_ffn2), full(router_w2), full(rb2),
                  pl.BlockSpec(memory_space=pl.ANY)],
        out_specs=[row_out(d), row_out(d // 2), col_out, row_out(TOP_K), col_out,
                   pl.BlockSpec((N_EXPERTS, 1), lambda i: (0, 0))],
        out_shape=[jax.ShapeDtypeStruct((t, d), f32), jax.ShapeDtypeStruct((t, d // 2), u32),
                   jax.ShapeDtypeStruct((TOP_K, t), i32), jax.ShapeDtypeStruct((t, TOP_K), f32),
                   jax.ShapeDtypeStruct((TOP_K, t), i32), jax.ShapeDtypeStruct((N_EXPERTS, 1), i32)],
        scratch_shapes=[pltpu.VMEM((N_EXPERTS, 1), f32)],
        compiler_params=_params("arbitrary"),
        name="mix_out_router",
    )(x2d, y_na, y_ft, q_mem, k_mem, v_mem, g_grp2, w_out_bf16, g_ffn2, router_w2, rb2, order_after)


def _sc_mesh():
    return plsc.VectorSubcoreMesh(core_axis_name="c", subcore_axis_name="s",
                                  num_cores=SC_CORES, num_subcores=SC_SUBCORES)


def _dispatch(h2p, dest, n_slots):
    t, w = h2p.shape
    workers = SC_CORES * SC_SUBCORES
    chunk = SC_GATHER_CHUNK
    per = t // workers
    steps = per // chunk
    assert per * workers == t and steps * chunk == per and steps % 2 == 0
    idx = dest.reshape(TOP_K, workers, steps, chunk)

    def body(h_hbm, idx_hbm, out_hbm, idx_v, rows_v, rsem, ssem):
        wid = lax.axis_index("s") * SC_CORES + lax.axis_index("c")
        base = wid * per
        for j in range(TOP_K):
            pltpu.sync_copy(idx_hbm.at[j, wid], idx_v.at[j])

        def read(c, slot):
            return pltpu.make_async_copy(h_hbm.at[pl.ds(base + c * chunk, chunk)], rows_v.at[slot], rsem.at[slot])

        def scatters(c, slot):
            return [pltpu.make_async_copy(rows_v.at[slot], out_hbm.at[idx_v.at[j, c]], ssem.at[slot])
                    for j in range(TOP_K)]

        read(0, 0).start()

        @pl.loop(0, steps, step=2)
        def _(c0):
            for slot in range(2):
                c = c0 + slot
                read(c, slot).wait()
                for cp in scatters(c, slot):
                    cp.start()

                @pl.when(c >= 1)
                def _():
                    for cp in scatters(c - 1, 1 - slot):
                        cp.wait()

                @pl.when(c + 1 < steps)
                def _():
                    read(c + 1, 1 - slot).start()

        for cp in scatters(steps - 1, 1):
            cp.wait()

    return pl.kernel(
        body,
        out_type=jax.ShapeDtypeStruct((n_slots, w), h2p.dtype),
        mesh=_sc_mesh(),
        scratch_types=[pltpu.VMEM((TOP_K, steps, chunk), i32), pltpu.VMEM((2, chunk, w), h2p.dtype),
                       pltpu.SemaphoreType.DMA((2,)), pltpu.SemaphoreType.DMA((2,))],
        name="sc_dispatch_rows",
    )(h2p, idx)


def _expert_kernel(blk_e_ref, blk_cnt_ref, nxt_e_ref, xs_ref, wgu_hbm, bgu_ref, wd_hbm, bd_ref, y_ref,
                   wgu_f32, wd_f32, wgu_bf, wd_bf, sem):
    b = pl.program_id(0)
    e = blk_e_ref[b]
    cnt = blk_cnt_ref[b]
    bm = xs_ref.shape[0]
    de = wd_f32.shape[0]

    def fetch(expert):
        return (pltpu.make_async_copy(wgu_hbm.at[expert], wgu_f32, sem.at[0]),
                pltpu.make_async_copy(wd_hbm.at[expert], wd_f32, sem.at[1]))

    @pl.when(b == 0)
    def _():
        for cp in fetch(e):
            cp.start()

    @pl.when(jnp.logical_or(b == 0, e != blk_e_ref[jnp.maximum(b - 1, 0)]))
    def _():
        for cp in fetch(e):
            cp.wait()

        def convert(i, carry):
            rows = pl.ds(pl.multiple_of(i * WEIGHT_CAST_ROWS, WEIGHT_CAST_ROWS), WEIGHT_CAST_ROWS)
            wgu_bf[rows, :] = wgu_f32[rows, :].astype(bf16)
            wd_bf[rows, :] = wd_f32[rows, :].astype(bf16)
            return carry

        lax.fori_loop(0, wgu_f32.shape[0] // WEIGHT_CAST_ROWS, convert, 0)

        @pl.when(nxt_e_ref[b] >= 0)
        def _():
            for cp in fetch(nxt_e_ref[b]):
                cp.start()

    def ffn_rows(r0, n):
        rows = pl.ds(r0, n)
        valid = r0 + lax.broadcasted_iota(i32, (n, 1), 0) < cnt
        x = jnp.where(valid, _unpack_bf16_pairs(xs_ref[rows, :]), 0.0).astype(bf16)
        gu = jnp.dot(x, wgu_bf[...], preferred_element_type=f32) + bgu_ref[0]
        x_glu = jnp.minimum(gu[:, :de], SWIGLU_LIMIT)
        x_lin = jnp.clip(gu[:, de:], -SWIGLU_LIMIT, SWIGLU_LIMIT)
        act = x_glu * (1.0 / (1.0 + jnp.exp(-SWIGLU_ALPHA * x_glu))) * (x_lin + 1.0)
        y = jnp.dot(act.astype(bf16), wd_bf[...], preferred_element_type=f32) + bd_ref[0]
        y_ref[rows, :] = _pack_bf16_pairs(y)

    @pl.when(cnt == bm)
    def _():
        ffn_rows(0, bm)

    @pl.when(cnt < bm)
    def _():
        y_ref[...] = jnp.zeros_like(y_ref)

        def piece(i, carry):
            ffn_rows(pl.multiple_of(i * MOE_SUB_BLOCK, MOE_SUB_BLOCK), MOE_SUB_BLOCK)
            return carry

        lax.fori_loop(0, (cnt + MOE_SUB_BLOCK - 1) // MOE_SUB_BLOCK, piece, 0)


def _experts(xs, blk_e, blk_cnt, nxt_e, w_gu, b_gu, w_down, b_down):
    n_slots, w = xs.shape
    bm = MOE_BLOCK
    e, d, de2 = w_gu.shape
    de = w_down.shape[1]
    assert de == d, "one row loop converts both weight matrices"
    last = n_slots // bm - 1

    def rows_of_block(b, be, bc, ne):
        return (jnp.where(bc[b] > 0, b, last), 0)

    grid_spec = pltpu.PrefetchScalarGridSpec(
        num_scalar_prefetch=3,
        grid=(n_slots // bm,),
        in_specs=[pl.BlockSpec((bm, w), rows_of_block),
                  pl.BlockSpec(memory_space=pl.ANY),
                  pl.BlockSpec((1, 1, de2), lambda b, be, bc, ne: (be[b], 0, 0)),
                  pl.BlockSpec(memory_space=pl.ANY),
                  pl.BlockSpec((1, 1, d), lambda b, be, bc, ne: (be[b], 0, 0))],
        out_specs=pl.BlockSpec((bm, w), rows_of_block),
        scratch_shapes=[pltpu.VMEM((d, de2), f32), pltpu.VMEM((de, d), f32),
                        pltpu.VMEM((d, de2), bf16), pltpu.VMEM((de, d), bf16),
                        pltpu.SemaphoreType.DMA((2,))],
    )
    return pl.pallas_call(
        _expert_kernel,
        grid_spec=grid_spec,
        out_shape=jax.ShapeDtypeStruct((n_slots, w), u32),
        compiler_params=_params("arbitrary"),
        name="moe_experts",
    )(blk_e, blk_cnt, nxt_e, xs, w_gu, b_gu.reshape(e, 1, de2), w_down, b_down.reshape(e, 1, d))


def _sc_gather_rows(table, idx):
    n, w = idx.shape[0], table.shape[1]
    workers = SC_CORES * SC_SUBCORES
    chunk = SC_GATHER_CHUNK
    per = n // workers
    steps = per // chunk
    assert per * workers == n and steps * chunk == per and steps % 2 == 0

    def body(table_hbm, idx_hbm, out_hbm, idx_v, rows_v, gsem, wsem):
        base = (lax.axis_index("s") * SC_CORES + lax.axis_index("c")) * per
        pltpu.sync_copy(idx_hbm.at[pl.ds(base, per)], idx_v)

        def gather(c, slot):
            return pltpu.make_async_copy(table_hbm.at[idx_v.at[pl.ds(c * chunk, chunk)]], rows_v.at[slot],
                                         gsem.at[slot])

        def write(c, slot):
            return pltpu.make_async_copy(rows_v.at[slot], out_hbm.at[pl.ds(base + c * chunk, chunk)],
                                         wsem.at[slot])

        gather(0, 0).start()

        @pl.loop(0, steps, step=2)
        def _(c0):
            for slot in range(2):
                c = c0 + slot
                gather(c, slot).wait()
                write(c, slot).start()

                @pl.when(c >= 1)
                def _():
                    write(c - 1, 1 - slot).wait()

                @pl.when(c + 1 < steps)
                def _():
                    gather(c + 1, 1 - slot).start()

        write(steps - 1, 1).wait()

    return pl.kernel(
        body,
        out_type=jax.ShapeDtypeStruct((n, w), table.dtype),
        mesh=_sc_mesh(),
        scratch_types=[pltpu.VMEM((per,), i32), pltpu.VMEM((2, chunk, w), table.dtype),
                       pltpu.SemaphoreType.DMA((2,)), pltpu.SemaphoreType.DMA((2,))],
        name="sc_gather_rows",
    )(table, idx)


def _combine_kernel(x1_ref, gate_ref, gfin_ref, yg_ref, *rest):
    o_ref = rest[-1]
    acc = x1_ref[...]
    gates = gate_ref[...]
    for j in range(TOP_K):
        acc = acc + gates[:, j:j + 1] * _unpack_bf16_pairs(yg_ref[j])
    o_ref[...] = _rms_scale(acc) * gfin_ref[...]


def _combine(x1, gates, dest, y_slots, g_final, out_prev, tile0, total_tokens):
    t, d = x1.shape
    tm = TOKEN_TILE
    w = y_slots.shape[1]
    yg = _sc_gather_rows(y_slots, dest.reshape(-1)).reshape(TOP_K, t, w)
    in_specs = [pl.BlockSpec((tm, d), lambda i: (i, 0)),
                pl.BlockSpec((tm, TOP_K), lambda i: (i, 0)),
                pl.BlockSpec((1, d), lambda i: (0, 0)),
                pl.BlockSpec((TOP_K, tm, w), lambda i: (0, i, 0))]
    args = [x1, gates, g_final.reshape(1, d), yg]
    aliases = {}
    if out_prev is not None:
        in_specs.append(pl.BlockSpec(memory_space=pl.ANY))
        args.append(out_prev)
        aliases = {len(args) - 1: 0}
    return pl.pallas_call(
        _combine_kernel,
        grid=(t // tm,),
        in_specs=in_specs,
        out_specs=pl.BlockSpec((tm, d), lambda i: (i + tile0, 0)),
        out_shape=jax.ShapeDtypeStruct((total_tokens, d), f32),
        input_output_aliases=aliases,
        compiler_params=_params("parallel"),
        name="moe_combine",
    )(*args)


def _dest_kernel(start_ref, eidx_ref, rank_ref, dest_ref):
    eidx = eidx_ref[...]
    dest = rank_ref[...]
    for e in range(N_EXPERTS):
        dest = dest + jnp.where(eidx == e, start_ref[e], 0)
    dest_ref[...] = dest


def _slot_layout(counts, eidx, rank, n_blocks):
    bm = MOE_BLOCK
    padded = (counts + bm - 1) // bm * bm
    padded_end = jnp.cumsum(padded)
    start = padded_end - padded
    experts = jnp.arange(N_EXPERTS, dtype=i32)
    lookup = lambda table, idx: jnp.sum(jnp.where(idx[..., None] == experts, table, 0), axis=-1)
    dest = pl.pallas_call(
        _dest_kernel,
        grid_spec=pltpu.PrefetchScalarGridSpec(
            num_scalar_prefetch=1, grid=(1,),
            in_specs=[pl.BlockSpec(eidx.shape, lambda i, st: (0, 0)), pl.BlockSpec(rank.shape, lambda i, st: (0, 0))],
            out_specs=pl.BlockSpec(rank.shape, lambda i, st: (0, 0))),
        out_shape=jax.ShapeDtypeStruct(rank.shape, i32),
        name="moe_dest",
    )(start.astype(i32), eidx, rank)
    blk_row = jnp.arange(n_blocks, dtype=i32) * bm
    blk_e = jnp.minimum(jnp.sum((padded_end[None, :] <= blk_row[:, None]).astype(i32), axis=1), N_EXPERTS - 1)
    blk_cnt = jnp.clip(lookup(counts, blk_e) - (blk_row - lookup(start, blk_e)), 0, bm).astype(i32)
    none = jnp.int32(N_EXPERTS)
    nxt_e = jnp.min(jnp.where(blk_e[None, :] > blk_e[:, None], blk_e[None, :], none), axis=1)
    nxt_e = jnp.where(nxt_e == none, -1, nxt_e).astype(i32)
    return dest, blk_e, blk_cnt, nxt_e


def _layer_and_final_norm(x2d, mem, seq, g_mix, g_mem, w_in, w_mem_kv, na_rel_bias, g_grp, w_out, g_ffn,
                          router_w, router_b, w_gu, b_gu, w_down, b_down, g_final):
    t, d = x2d.shape
    b = t // seq
    q_na, k_na, v_na, u_ft, q_mem = _in_proj(x2d, g_mix, w_in.astype(bf16))
    k_mem, v_mem = _mem_kv(mem, g_mem, w_mem_kv.astype(bf16))
    shape3 = lambda a: a.reshape(b, seq, a.shape[-1])
    y_na = _neighbourhood_attention(shape3(q_na), shape3(k_na), shape3(v_na), _na_bias_table(na_rel_bias))
    y_ft = _fourier_mix(shape3(u_ft), _ft_tables(seq))
    y_na, y_ft = y_na.reshape(t, -1), y_ft.reshape(t, -1)
    w_out_bf16 = w_out.astype(bf16)
    rw_hi = router_w.astype(bf16)
    router_w2 = jnp.stack([rw_hi, (router_w - rw_hi.astype(f32)).astype(bf16)])

    tiles = t // TOKEN_TILE
    unit = tiles // sum(MOE_GROUP_SHARES)
    assert unit * sum(MOE_GROUP_SHARES) == tiles
    out = None
    tile0 = 0
    dest = router_b
    for share in MOE_GROUP_SHARES:
        group_tiles = share * unit
        n_blocks = (group_tiles * TOKEN_TILE * TOP_K) // MOE_BLOCK + N_EXPERTS
        x1, h2p, eidx, gates, rank, counts = _mix_out(
            x2d, y_na, y_ft, q_mem, k_mem, v_mem, g_grp, w_out_bf16, g_ffn, router_w2, router_b, seq,
            tile0, group_tiles, dest)
        dest, blk_e, blk_cnt, nxt_e = _slot_layout(counts[:, 0], eidx, rank, n_blocks)
        xs = _dispatch(h2p, dest, n_blocks * MOE_BLOCK)
        y_slots = _experts(xs, blk_e, blk_cnt, nxt_e, w_gu, b_gu, w_down, b_down)
        out = _combine(x1, gates, dest, y_slots, g_final, out, tile0, t)
        tile0 += group_tiles
    return out


def kernel(x, mem, g_mix, g_mem, w_in, w_mem_kv, na_rel_bias, g_grp, w_out, g_ffn, router_w, router_b,
           w_gu, b_gu, w_down, b_down, g_final):
    b, seq, d = x.shape
    depth = w_in.shape[0]
    assert depth == 1, "the final norm is fused into the single layer's combine step"
    out = _layer_and_final_norm(
        x.reshape(b * seq, d), mem, seq, g_mix[0], g_mem[0], w_in[0], w_mem_kv[0], na_rel_bias[0], g_grp[0],
        w_out[0], g_ffn[0], router_w[0], router_b[0], w_gu[0], b_gu[0], w_down[0], b_down[0], g_final)
    return out.reshape(b, seq, d)
```

```python
import numpy as np
import jax
import jax.numpy as jnp
from jax import lax
from jax.experimental import pallas as pl
from jax.experimental.pallas import tpu as pltpu
from jax.experimental.pallas import tpu_sc as plsc

f32 = jnp.float32
bf16 = jnp.bfloat16
u32 = jnp.uint32
i32 = jnp.int32

GRID_W = 64
NA_HEADS = 8
NA_HEAD_DIM = 64
NA_WIN_ROWS = 8
NA_WIN_COLS = 16
FT_GROUPS = 4
FT_GROUP_DIM = 128
MEM_HEADS = 4
MEM_HEAD_DIM = 128
NA_WIDTH = NA_HEADS * NA_HEAD_DIM
FT_WIDTH = FT_GROUPS * FT_GROUP_DIM
MEM_WIDTH = MEM_HEADS * MEM_HEAD_DIM
N_EXPERTS = 32
TOP_K = 4
SWIGLU_LIMIT = 7.0
SWIGLU_ALPHA = 1.702
EPS = 1e-6

LANES = 128
SUBLANES = 8
VMEM_LIMIT_BYTES = 56 * 1024 * 1024
SC_CORES = 2
SC_SUBCORES = 16
SC_GATHER_CHUNK = 64

TOKEN_TILE = 1024
IN_PROJ_TILE = 1024
MOE_BLOCK = 512
MOE_SUB_BLOCK = 128
WEIGHT_CAST_ROWS = 128
MOE_GROUP_SHARES = (5, 3)
NA_ROW_UNROLL = 16
FT_N1 = 64
FT_N2 = 128
FT_K1_BLOCK = 16
FT_BATCH_BLOCK = 4
LOG2_E = 1.4426950408889634
MASK_VALUE = -jnp.inf


def _params(*semantics):
    return pltpu.CompilerParams(dimension_semantics=semantics, vmem_limit_bytes=VMEM_LIMIT_BYTES)


def _rms_scale(x):
    return x * lax.rsqrt(jnp.mean(x * x, axis=-1, keepdims=True) + EPS)


def _softmax_rows(s):
    p = jnp.exp(s - jnp.max(s, axis=-1, keepdims=True))
    return p / jnp.sum(p, axis=-1, keepdims=True)


def _bf16_bits(x):
    return pltpu.bitcast(x.astype(bf16).astype(f32), u32)


def _pack2(lo, hi):
    return (_bf16_bits(lo) >> 16) | (_bf16_bits(hi) & jnp.uint32(0xFFFF0000))


def _unpack2(w):
    return pltpu.bitcast(w << 16, f32), pltpu.bitcast(w & jnp.uint32(0xFFFF0000), f32)


def _pack_bf16_pairs(x):
    n = x.shape[1] // 2
    return _pack2(x[:, :n], x[:, n:])


def _unpack_bf16_pairs(w):
    return jnp.concatenate(_unpack2(w), axis=1)


def _in_proj_kernel(x_ref, g_ref, w_ref, qna_ref, kna_ref, vna_ref, uft_ref, qmem_ref):
    h = _rms_scale(x_ref[...]) * g_ref[...]
    proj = jnp.dot(h.astype(bf16), w_ref[...], preferred_element_type=f32)
    o = NA_WIDTH
    qna_ref[...] = (proj[:, :o] * (NA_HEAD_DIM ** -0.5 * LOG2_E)).astype(bf16)
    kna_ref[...] = proj[:, o:2 * o].astype(bf16)
    vna_ref[...] = proj[:, 2 * o:3 * o].astype(bf16)
    uft_ref[...] = _pack_bf16_pairs(proj[:, 3 * o:3 * o + FT_WIDTH])
    qmem_ref[...] = proj[:, 3 * o + FT_WIDTH:].astype(bf16)


def _in_proj(x2d, g_mix, w_in_bf16):
    t, d = x2d.shape
    tm = IN_PROJ_TILE
    row = lambda w: pl.BlockSpec((tm, w), lambda i: (i, 0))
    return pl.pallas_call(
        _in_proj_kernel,
        grid=(t // tm,),
        in_specs=[row(d), pl.BlockSpec((1, d), lambda i: (0, 0)),
                  pl.BlockSpec(w_in_bf16.shape, lambda i: (0, 0))],
        out_specs=[row(NA_WIDTH), row(NA_WIDTH), row(NA_WIDTH), row(FT_WIDTH // 2), row(MEM_WIDTH)],
        out_shape=[jax.ShapeDtypeStruct((t, NA_WIDTH), bf16)] * 3
        + [jax.ShapeDtypeStruct((t, FT_WIDTH // 2), u32), jax.ShapeDtypeStruct((t, MEM_WIDTH), bf16)],
        compiler_params=_params("parallel"),
        name="in_proj",
    )(x2d, g_mix.reshape(1, d), w_in_bf16)


def _na_bias_table(rel_bias):
    c = np.arange(GRID_W)
    dc_idx = np.clip(c[None, :] - c[:, None], -(NA_WIN_COLS - 1), NA_WIN_COLS - 1) + (NA_WIN_COLS - 1)
    col_start = np.clip(c - NA_WIN_COLS // 2, 0, GRID_W - NA_WIN_COLS)
    col_in = (c[None, :] >= col_start[:, None]) & (c[None, :] < col_start[:, None] + NA_WIN_COLS)
    pick_c = jnp.asarray(dc_idx[:, :, None] == np.arange(2 * NA_WIN_COLS - 1), f32)
    cols = jnp.einsum("hab,qcb->haqc", rel_bias.astype(f32), pick_c, precision=lax.Precision.HIGHEST)
    cols = jnp.where(col_in[None, None], cols * LOG2_E, MASK_VALUE)
    tab = pl.pallas_call(
        _na_bias_expand_kernel,
        grid=(NA_WIN_ROWS,),
        in_specs=[pl.BlockSpec(cols.shape, lambda s: (0, 0, 0, 0))],
        out_specs=pl.BlockSpec((1, NA_HEADS, GRID_W, NA_WIN_ROWS * GRID_W), lambda s: (s, 0, 0, 0)),
        out_shape=jax.ShapeDtypeStruct((NA_WIN_ROWS, NA_HEADS, GRID_W, NA_WIN_ROWS * GRID_W), f32),
        compiler_params=_params("parallel"),
        name="na_bias_expand",
    )(cols)
    return tab.reshape(NA_WIN_ROWS, NA_HEADS // 2, 2 * GRID_W, NA_WIN_ROWS * GRID_W)


def _na_bias_expand_kernel(cols_ref, tab_ref):
    s = pl.program_id(0)
    for j in range(0, NA_WIN_ROWS, 2):
        a = j - s + (NA_WIN_ROWS - 1)
        pair = jnp.concatenate([cols_ref[:, pl.ds(a, 1)], cols_ref[:, pl.ds(a + 1, 1)]], axis=-1)
        tab_ref[0, :, :, pl.ds(j * GRID_W, 2 * GRID_W)] = pair[:, 0]


def _na_kernel(q_ref, k_ref, v_ref, bias_ref, o_ref):
    rows = q_ref.shape[1] // GRID_W
    win = NA_WIN_ROWS * GRID_W
    first_head = lax.broadcasted_iota(i32, (GRID_W, 2 * NA_HEAD_DIM), 1) < NA_HEAD_DIM

    def body(it, carry):
        scores, values, q_offsets = [], [], []
        for u in range(NA_ROW_UNROLL):
            r = it * NA_ROW_UNROLL + u
            row_start = jnp.clip(r - NA_WIN_ROWS // 2, 0, rows - NA_WIN_ROWS)
            q0 = pl.multiple_of(r * GRID_W, GRID_W)
            k0 = pl.multiple_of(row_start * GRID_W, GRID_W)
            q = q_ref[0, pl.ds(q0, GRID_W), :]
            zero = jnp.zeros_like(q)
            qm = jnp.concatenate([jnp.where(first_head, q, zero), jnp.where(first_head, zero, q)], axis=0)
            s = lax.dot_general(qm, k_ref[0, pl.ds(k0, win), :], (((1,), (1,)), ((), ())),
                                preferred_element_type=f32)
            scores.append(s + bias_ref[r - row_start, 0])
            values.append(v_ref[0, pl.ds(k0, win), :])
            q_offsets.append(q0)
        s = jnp.concatenate(scores, axis=0)
        p = jnp.exp2(s - jnp.max(s, axis=-1, keepdims=True))
        inv_den = 1.0 / jnp.sum(p, axis=-1, keepdims=True)
        p = p.astype(bf16)
        for u in range(NA_ROW_UNROLL):
            sl = slice(u * 2 * GRID_W, (u + 1) * 2 * GRID_W)
            o = jnp.dot(p[sl], values[u], preferred_element_type=f32) * inv_den[sl]
            o_ref[0, pl.ds(q_offsets[u], GRID_W), :] = jnp.where(
                first_head, o[:GRID_W], o[GRID_W:]).astype(o_ref.dtype)
        return carry

    lax.fori_loop(0, rows // NA_ROW_UNROLL, body, 0)


def _neighbourhood_attention(q, k, v, bias_tab):
    b, s, _ = q.shape
    pair = 2 * NA_HEAD_DIM
    qkv_spec = pl.BlockSpec((1, s, pair), lambda bi, hp: (bi, 0, hp))
    return pl.pallas_call(
        _na_kernel,
        grid=(b, NA_HEADS // 2),
        in_specs=[qkv_spec, qkv_spec, qkv_spec,
                  pl.BlockSpec((NA_WIN_ROWS, 1, 2 * GRID_W, NA_WIN_ROWS * GRID_W), lambda bi, hp: (0, hp, 0, 0))],
        out_specs=qkv_spec,
        out_shape=jax.ShapeDtypeStruct((b, s, NA_WIDTH), bf16),
        compiler_params=_params("parallel", "parallel"),
        name="neighbourhood_attention",
    )(q, k, v, bias_tab)


def _ft_tables(seq):
    assert seq == FT_N1 * FT_N2
    n_blk = FT_N2 // SUBLANES
    k1 = np.arange(FT_N1)[:, None, None, None]
    sr = np.arange(SUBLANES)[None, :, None, None]
    n1 = np.arange(FT_N1)[None, None, :, None]
    sc = np.arange(SUBLANES)[None, None, None, :]
    stage1 = np.zeros((n_blk, 2, FT_N1, SUBLANES, FT_N1, SUBLANES), np.float64)
    for blk in range(n_blk):
        n = FT_N2 * n1 + SUBLANES * blk + sr
        ang = 2.0 * np.pi * ((k1 * n) % seq) / seq
        eye = (sr == sc)
        stage1[blk, 0] = np.cos(ang) * eye
        stage1[blk, 1] = -np.sin(ang) * eye
    stage1 = stage1.reshape(n_blk, 2 * FT_N1 * SUBLANES, FT_N1 * SUBLANES)
    a = np.arange(FT_N2)
    ang2 = 2.0 * np.pi * ((a[:, None] * a[None, :]) % FT_N2) / FT_N2
    c2, s2 = np.cos(ang2), np.sin(ang2)
    stage2 = np.block([[c2, s2], [-s2, c2]])
    g = np.arange(FT_GROUP_DIM)
    angc = 2.0 * np.pi * ((g[:, None] * g[None, :]) % FT_GROUP_DIM) / FT_GROUP_DIM
    norm = 1.0 / np.sqrt(seq * FT_GROUP_DIM)
    chan = np.concatenate([np.cos(angc), np.sin(angc)], axis=0) * norm
    return (jnp.asarray(stage1, bf16), jnp.asarray(stage2, bf16), jnp.asarray(chan, bf16))


def _ft_stage1_kernel(u_ref, m_ref, z_ref):
    rows = FT_N1 * SUBLANES
    nb = u_ref.shape[0]
    u = jnp.concatenate([_unpack_bf16_pairs(u_ref[bb].reshape(rows, FT_WIDTH // 2)) for bb in range(nb)],
                        axis=1).astype(bf16)
    z = jnp.dot(m_ref[0], u, preferred_element_type=f32)
    for bb in range(nb):
        zb = z[:, bb * FT_WIDTH:(bb + 1) * FT_WIDTH]
        z_ref[bb] = _pack2(zb[:rows], zb[rows:]).reshape(FT_N1, SUBLANES, FT_WIDTH)


def _ft_stage2_kernel(z_ref, s2_ref, cs_ref, y_ref):
    gd = FT_GROUP_DIM
    xs = []
    for kk in range(FT_K1_BLOCK):
        zz = jnp.concatenate(_unpack2(z_ref[0, kk]), axis=0).astype(bf16)
        xs.append(jnp.dot(s2_ref[...], zz, preferred_element_type=f32))
    for g in range(FT_GROUPS):
        cols = slice(g * gd, (g + 1) * gd)
        xg = jnp.concatenate([jnp.concatenate([x[:FT_N2, cols], x[FT_N2:, cols]], axis=1) for x in xs], axis=0)
        og = jnp.dot(xg.astype(bf16), cs_ref[...], preferred_element_type=f32)
        for kk in range(FT_K1_BLOCK):
            y_ref[0, kk, :, cols] = og[kk * FT_N2:(kk + 1) * FT_N2].astype(y_ref.dtype)


def _fourier_mix(u_packed, tables):
    u = u_packed
    b, s, _ = u.shape
    c = FT_WIDTH
    assert b % FT_BATCH_BLOCK == 0 and FT_N1 % FT_K1_BLOCK == 0
    stage1, stage2, chan = tables
    n_blk = FT_N2 // SUBLANES
    z = pl.pallas_call(
        _ft_stage1_kernel,
        grid=(n_blk, b // FT_BATCH_BLOCK),
        in_specs=[pl.BlockSpec((FT_BATCH_BLOCK, FT_N1, SUBLANES, c // 2), lambda j, bi: (bi, 0, j, 0)),
                  pl.BlockSpec((1,) + stage1.shape[1:], lambda j, bi: (j, 0, 0))],
        out_specs=pl.BlockSpec((FT_BATCH_BLOCK, FT_N1, SUBLANES, c), lambda j, bi: (bi, 0, j, 0)),
        out_shape=jax.ShapeDtypeStruct((b, FT_N1, FT_N2, c), u32),
        compiler_params=_params("parallel", "parallel"),
        name="fourier_stage1",
    )(u.reshape(b, FT_N1, FT_N2, c // 2), stage1)
    y = pl.pallas_call(
        _ft_stage2_kernel,
        grid=(b, FT_N1 // FT_K1_BLOCK),
        in_specs=[pl.BlockSpec((1, FT_K1_BLOCK, FT_N2, c), lambda bi, kb: (bi, kb, 0, 0)),
                  pl.BlockSpec(stage2.shape, lambda bi, kb: (0, 0)),
                  pl.BlockSpec(chan.shape, lambda bi, kb: (0, 0))],
        out_specs=pl.BlockSpec((1, FT_K1_BLOCK, FT_N2, c), lambda bi, kb: (bi, kb, 0, 0)),
        out_shape=jax.ShapeDtypeStruct((b, FT_N1, FT_N2, c), bf16),
        compiler_params=_params("parallel", "parallel"),
        name="fourier_stage2",
    )(z, stage2, chan)
    return y.transpose(0, 2, 1, 3).reshape(b, s, c)


def _mem_kv_kernel(mem_ref, g_ref, w_ref, k_ref, v_ref):
    mn = _rms_scale(mem_ref[0]) * g_ref[...]
    kv = jnp.dot(mn.astype(bf16), w_ref[...], preferred_element_type=f32)
    k_ref[0] = kv[:, :MEM_WIDTH].astype(bf16)
    v_ref[0] = kv[:, MEM_WIDTH:].astype(bf16)


def _mem_kv(mem, g_mem, w_kv_bf16):
    b, m, d = mem.shape
    kv_spec = pl.BlockSpec((1, m, MEM_WIDTH), lambda bi: (bi, 0, 0))
    return pl.pallas_call(
        _mem_kv_kernel,
        grid=(b,),
        in_specs=[pl.BlockSpec((1, m, d), lambda bi: (bi, 0, 0)), pl.BlockSpec((1, d), lambda bi: (0, 0)),
                  pl.BlockSpec(w_kv_bf16.shape, lambda bi: (0, 0))],
        out_specs=[kv_spec, kv_spec],
        out_shape=[jax.ShapeDtypeStruct((b, m, MEM_WIDTH), bf16)] * 2,
        compiler_params=_params("parallel"),
        name="mem_kv",
    )(mem, g_mem.reshape(1, d), w_kv_bf16)


def _mix_out_kernel(x_ref, yna_ref, yft_ref, qm_ref, km_ref, vm_ref, ggrp_ref, wout_ref, gffn_ref, rw_ref,
                    rb_ref, _order_ref, x1_ref, h2p_ref, eidx_ref, gate_ref, rank_ref, cnt_ref, carry_ref):
    tm = x_ref.shape[0]

    @pl.when(pl.program_id(0) == 0)
    def _():
        carry_ref[...] = jnp.zeros_like(carry_ref)

    q = qm_ref[...]
    km = km_ref[0]
    vm = vm_ref[0]
    heads = []
    for h in range(MEM_HEADS):
        sl = slice(h * MEM_HEAD_DIM, (h + 1) * MEM_HEAD_DIM)
        s = lax.dot_general(q[:, sl], km[:, sl], (((1,), (1,)), ((), ())), preferred_element_type=f32)
        p = _softmax_rows(s * (MEM_HEAD_DIM ** -0.5))
        heads.append(jnp.dot(p.astype(bf16), vm[:, sl], preferred_element_type=f32))
    ymem = jnp.concatenate(heads, axis=1)

    g = ggrp_ref[...]
    a, c = NA_WIDTH, NA_WIDTH + FT_WIDTH
    y = jnp.concatenate([_rms_scale(yna_ref[...].astype(f32)) * g[:, :a],
                         _rms_scale(yft_ref[...].astype(f32)) * g[:, a:c],
                         _rms_scale(ymem) * g[:, c:]], axis=1)
    x1 = x_ref[...] + jnp.dot(y.astype(bf16), wout_ref[...], preferred_element_type=f32)
    x1_ref[...] = x1
    h2 = _rms_scale(x1) * gffn_ref[...]
    h2p_ref[...] = _pack_bf16_pairs(h2)

    h_hi = h2.astype(bf16)
    h_lo = (h2 - h_hi.astype(f32)).astype(bf16)
    hh = jnp.dot(h_hi, rw_ref[...], preferred_element_type=f32)
    logits = (hh[:, :LANES] + hh[:, LANES:]
              + jnp.dot(h_lo, rw_ref[:, :LANES], preferred_element_type=f32)) + rb_ref[...]
    l = logits.T[:N_EXPERTS]
    row = lax.broadcasted_iota(i32, (N_EXPERTS, tm), 0).astype(f32)
    vals, idxs, sels = [], [], []
    for _ in range(TOP_K):
        m = jnp.max(l, axis=0, keepdims=True)
        idx = jnp.min(jnp.where(l == m, row, float(N_EXPERTS)), axis=0, keepdims=True)
        sel = row == idx
        vals.append(m)
        idxs.append(idx)
        sels.append(sel)
        l = jnp.where(sel, -jnp.inf, l)
    ex = [jnp.exp(v - vals[0]) for v in vals]
    den = ex[0] + ex[1] + ex[2] + ex[3]

    onehot = (sels[0] | sels[1] | sels[2] | sels[3]).astype(f32)
    earlier = (lax.broadcasted_iota(i32, (tm, tm), 0) < lax.broadcasted_iota(i32, (tm, tm), 1)).astype(bf16)
    before = jnp.dot(onehot.astype(bf16), earlier, preferred_element_type=f32) + carry_ref[...]
    ranks = [jnp.sum(jnp.where(sel, before, 0.0), axis=0, keepdims=True) for sel in sels]
    carry_ref[...] = carry_ref[...] + jnp.sum(onehot, axis=1, keepdims=True)
    cnt_ref[...] = carry_ref[...].astype(i32)

    eidx_ref[...] = jnp.concatenate(idxs, axis=0).astype(i32)
    rank_ref[...] = jnp.concatenate(ranks, axis=0).astype(i32)
    gates_t = jnp.concatenate([e / den for e in ex] + [jnp.zeros((LANES - TOP_K, tm), f32)], axis=0)
    gate_ref[...] = gates_t.T[:, :TOP_K]


def _mix_out(x2d, y_na, y_ft, q_mem, k_mem, v_mem, g_grp, w_out_bf16, g_ffn, router_w2, router_b, seq,
             tile0, n_tiles, order_after):
    d = x2d.shape[1]
    tm = TOKEN_TILE
    t = n_tiles * tm
    steps_per_batch = seq // tm
    m = k_mem.shape[1]
    row_in = lambda w: pl.BlockSpec((tm, w), lambda i: (i + tile0, 0))
    row_out = lambda w: pl.BlockSpec((tm, w), lambda i: (i, 0))
    full = lambda a: pl.BlockSpec(a.shape, lambda i: (0,) * a.ndim)
    kv_spec = pl.BlockSpec((1, m, MEM_WIDTH), lambda i: ((i + tile0) // steps_per_batch, 0, 0))
    g_grp2, g_ffn2 = g_grp.reshape(1, -1), g_ffn.reshape(1, d)
    rb2 = jnp.pad(router_b.reshape(1, N_EXPERTS), ((0, 0), (0, LANES - N_EXPERTS)))
    router_w2 = jnp.pad(router_w2, ((0, 0), (0, 0), (0, LANES - N_EXPERTS)))
    router_w2 = jnp.concatenate([router_w2[0], router_w2[1]], axis=1)
    col_out = pl.BlockSpec((TOP_K, tm), lambda i: (0, i))
    return pl.pallas_call(
        _mix_out_kernel,
        grid=(n_tiles,),
        in_specs=[row_in(d), row_in(NA_WIDTH), row_in(FT_WIDTH), row_in(MEM_WIDTH), kv_spec, kv_spec,
                  full(g_grp2), full(w_out_bf16), full(g_ffn2), full(router_w2), full(rb2),
                  pl.BlockSpec(memory_space=pl.ANY)],
        out_specs=[row_out(d), row_out(d // 2), col_out, row_out(TOP_K), col_out,
                   pl.BlockSpec((N_EXPERTS, 1), lambda i: (0, 0))],
        out_shape=[jax.ShapeDtypeStruct((t, d), f32), jax.ShapeDtypeStruct((t, d // 2), u32),
                   jax.ShapeDtypeStruct((TOP_K, t), i32), jax.ShapeDtypeStruct((t, TOP_K), f32),
                   jax.ShapeDtypeStruct((TOP_K, t), i32), jax.ShapeDtypeStruct((N_EXPERTS, 1), i32)],
        scratch_shapes=[pltpu.VMEM((N_EXPERTS, 1), f32)],
        compiler_params=_params("arbitrary"),
        name="mix_out_router",
    )(x2d, y_na, y_ft, q_mem, k_mem, v_mem, g_grp2, w_out_bf16, g_ffn2, router_w2, rb2, order_after)


def _sc_mesh():
    return plsc.VectorSubcoreMesh(core_axis_name="c", subcore_axis_name="s",
                                  num_cores=SC_CORES, num_subcores=SC_SUBCORES)


def _dispatch(h2p, dest, n_slots):
    t, w = h2p.shape
    workers = SC_CORES * SC_SUBCORES
    chunk = SC_GATHER_CHUNK
    per = t // workers
    steps = per // chunk
    assert per * workers == t and steps * chunk == per and steps % 2 == 0
    idx = dest.reshape(TOP_K, workers, steps, chunk)

    def body(h_hbm, idx_hbm, out_hbm, idx_v, rows_v, rsem, ssem):
        wid = lax.axis_index("s") * SC_CORES + lax.axis_index("c")
        base = wid * per
        for j in range(TOP_K):
            pltpu.sync_copy(idx_hbm.at[j, wid], idx_v.at[j])

        def read(c, slot):
            return pltpu.make_async_copy(h_hbm.at[pl.ds(base + c * chunk, chunk)], rows_v.at[slot], rsem.at[slot])

        def scatters(c, slot):
            return [pltpu.make_async_copy(rows_v.at[slot], out_hbm.at[idx_v.at[j, c]], ssem.at[slot])
                    for j in range(TOP_K)]

        read(0, 0).start()

        @pl.loop(0, steps, step=2)
        def _(c0):
            for slot in range(2):
                c = c0 + slot
                read(c, slot).wait()
                for cp in scatters(c, slot):
                    cp.start()

                @pl.when(c >= 1)
                def _():
                    for cp in scatters(c - 1, 1 - slot):
                        cp.wait()

                @pl.when(c + 1 < steps)
                def _():
                    read(c + 1, 1 - slot).start()

        for cp in scatters(steps - 1, 1):
            cp.wait()

    return pl.kernel(
        body,
        out_type=jax.ShapeDtypeStruct((n_slots, w), h2p.dtype),
        mesh=_sc_mesh(),
        scratch_types=[pltpu.VMEM((TOP_K, steps, chunk), i32), pltpu.VMEM((2, chunk, w), h2p.dtype),
                       pltpu.SemaphoreType.DMA((2,)), pltpu.SemaphoreType.DMA((2,))],
        name="sc_dispatch_rows",
    )(h2p, idx)


def _expert_kernel(blk_e_ref, blk_cnt_ref, nxt_e_ref, xs_ref, wgu_hbm, bgu_ref, wd_hbm, bd_ref, y_ref,
                   wgu_f32, wd_f32, wgu_bf, wd_bf, sem):
    b = pl.program_id(0)
    e = blk_e_ref[b]
    cnt = blk_cnt_ref[b]
    bm = xs_ref.shape[0]
    de = wd_f32.shape[0]

    def fetch(expert):
        return (pltpu.make_async_copy(wgu_hbm.at[expert], wgu_f32, sem.at[0]),
                pltpu.make_async_copy(wd_hbm.at[expert], wd_f32, sem.at[1]))

    @pl.when(b == 0)
    def _():
        for cp in fetch(e):
            cp.start()

    @pl.when(jnp.logical_or(b == 0, e != blk_e_ref[jnp.maximum(b - 1, 0)]))
    def _():
        for cp in fetch(e):
            cp.wait()

        def convert(i, carry):
            rows = pl.ds(pl.multiple_of(i * WEIGHT_CAST_ROWS, WEIGHT_CAST_ROWS), WEIGHT_CAST_ROWS)
            wgu_bf[rows, :] = wgu_f32[rows, :].astype(bf16)
            wd_bf[rows, :] = wd_f32[rows, :].astype(bf16)
            return carry

        lax.fori_loop(0, wgu_f32.shape[0] // WEIGHT_CAST_ROWS, convert, 0)

        @pl.when(nxt_e_ref[b] >= 0)
        def _():
            for cp in fetch(nxt_e_ref[b]):
                cp.start()

    def ffn_rows(r0, n):
        rows = pl.ds(r0, n)
        valid = r0 + lax.broadcasted_iota(i32, (n, 1), 0) < cnt
        x = jnp.where(valid, _unpack_bf16_pairs(xs_ref[rows, :]), 0.0).astype(bf16)
        gu = jnp.dot(x, wgu_bf[...], preferred_element_type=f32) + bgu_ref[0]
        x_glu = jnp.minimum(gu[:, :de], SWIGLU_LIMIT)
        x_lin = jnp.clip(gu[:, de:], -SWIGLU_LIMIT, SWIGLU_LIMIT)
        act = x_glu * (1.0 / (1.0 + jnp.exp(-SWIGLU_ALPHA * x_glu))) * (x_lin + 1.0)
        y = jnp.dot(act.astype(bf16), wd_bf[...], preferred_element_type=f32) + bd_ref[0]
        y_ref[rows, :] = _pack_bf16_pairs(y)

    pieces = (cnt + MOE_SUB_BLOCK - 1) // MOE_SUB_BLOCK
    for k in range(bm // MOE_SUB_BLOCK + 1):
        @pl.when(pieces == k)
        def _(k=k):
            used = k * MOE_SUB_BLOCK
            if used:
                ffn_rows(0, used)
            if used < bm:
                y_ref[pl.ds(used, bm - used), :] = jnp.zeros((bm - used, y_ref.shape[1]), y_ref.dtype)


def _experts(xs, blk_e, blk_cnt, nxt_e, w_gu, b_gu, w_down, b_down):
    n_slots, w = xs.shape
    bm = MOE_BLOCK
    e, d, de2 = w_gu.shape
    de = w_down.shape[1]
    assert de == d, "one row loop converts both weight matrices"
    last = n_slots // bm - 1

    def rows_of_block(b, be, bc, ne):
        return (jnp.where(bc[b] > 0, b, last), 0)

    grid_spec = pltpu.PrefetchScalarGridSpec(
        num_scalar_prefetch=3,
        grid=(n_slots // bm,),
        in_specs=[pl.BlockSpec((bm, w), rows_of_block),
                  pl.BlockSpec(memory_space=pl.ANY),
                  pl.BlockSpec((1, 1, de2), lambda b, be, bc, ne: (be[b], 0, 0)),
                  pl.BlockSpec(memory_space=pl.ANY),
                  pl.BlockSpec((1, 1, d), lambda b, be, bc, ne: (be[b], 0, 0))],
        out_specs=pl.BlockSpec((bm, w), rows_of_block),
        scratch_shapes=[pltpu.VMEM((d, de2), f32), pltpu.VMEM((de, d), f32),
                        pltpu.VMEM((d, de2), bf16), pltpu.VMEM((de, d), bf16),
                        pltpu.SemaphoreType.DMA((2,))],
    )
    return pl.pallas_call(
        _expert_kernel,
        grid_spec=grid_spec,
        out_shape=jax.ShapeDtypeStruct((n_slots, w), u32),
        compiler_params=_params("arbitrary"),
        name="moe_experts",
    )(blk_e, blk_cnt, nxt_e, xs, w_gu, b_gu.reshape(e, 1, de2), w_down, b_down.reshape(e, 1, d))


def _sc_gather_rows(table, idx):
    n, w = idx.shape[0], table.shape[1]
    workers = SC_CORES * SC_SUBCORES
    chunk = SC_GATHER_CHUNK
    per = n // workers
    steps = per // chunk
    assert per * workers == n and steps * chunk == per and steps % 2 == 0

    def body(table_hbm, idx_hbm, out_hbm, idx_v, rows_v, gsem, wsem):
        base = (lax.axis_index("s") * SC_CORES + lax.axis_index("c")) * per
        pltpu.sync_copy(idx_hbm.at[pl.ds(base, per)], idx_v)

        def gather(c, slot):
            return pltpu.make_async_copy(table_hbm.at[idx_v.at[pl.ds(c * chunk, chunk)]], rows_v.at[slot],
                                         gsem.at[slot])

        def write(c, slot):
            return pltpu.make_async_copy(rows_v.at[slot], out_hbm.at[pl.ds(base + c * chunk, chunk)],
                                         wsem.at[slot])

        gather(0, 0).start()

        @pl.loop(0, steps, step=2)
        def _(c0):
            for slot in range(2):
                c = c0 + slot
                gather(c, slot).wait()
                write(c, slot).start()

                @pl.when(c >= 1)
                def _():
                    write(c - 1, 1 - slot).wait()

                @pl.when(c + 1 < steps)
                def _():
                    gather(c + 1, 1 - slot).start()

        write(steps - 1, 1).wait()

    return pl.kernel(
        body,
        out_type=jax.ShapeDtypeStruct((n, w), table.dtype),
        mesh=_sc_mesh(),
        scratch_types=[pltpu.VMEM((per,), i32), pltpu.VMEM((2, chunk, w), table.dtype),
                       pltpu.SemaphoreType.DMA((2,)), pltpu.SemaphoreType.DMA((2,))],
        name="sc_gather_rows",
    )(table, idx)


def _combine_kernel(x1_ref, gate_ref, gfin_ref, yg_ref, *rest):
    o_ref = rest[-1]
    acc = x1_ref[...]
    gates = gate_ref[...]
    for j in range(TOP_K):
        acc = acc + gates[:, j:j + 1] * _unpack_bf16_pairs(yg_ref[j])
    o_ref[...] = _rms_scale(acc) * gfin_ref[...]


def _combine(x1, gates, dest, y_slots, g_final, out_prev, tile0, total_tokens):
    t, d = x1.shape
    tm = TOKEN_TILE
    w = y_slots.shape[1]
    yg = _sc_gather_rows(y_slots, dest.reshape(-1)).reshape(TOP_K, t, w)
    in_specs = [pl.BlockSpec((tm, d), lambda i: (i, 0)),
                pl.BlockSpec((tm, TOP_K), lambda i: (i, 0)),
                pl.BlockSpec((1, d), lambda i: (0, 0)),
                pl.BlockSpec((TOP_K, tm, w), lambda i: (0, i, 0))]
    args = [x1, gates, g_final.reshape(1, d), yg]
    aliases = {}
    if out_prev is not None:
        in_specs.append(pl.BlockSpec(memory_space=pl.ANY))
        args.append(out_prev)
        aliases = {len(args) - 1: 0}
    return pl.pallas_call(
        _combine_kernel,
        grid=(t // tm,),
        in_specs=in_specs,
        out_specs=pl.BlockSpec((tm, d), lambda i: (i + tile0, 0)),
        out_shape=jax.ShapeDtypeStruct((total_tokens, d), f32),
        input_output_aliases=aliases,
        compiler_params=_params("parallel"),
        name="moe_combine",
    )(*args)


def _dest_kernel(start_ref, eidx_ref, rank_ref, dest_ref):
    eidx = eidx_ref[...]
    dest = rank_ref[...]
    for e in range(N_EXPERTS):
        dest = dest + jnp.where(eidx == e, start_ref[e], 0)
    dest_ref[...] = dest


def _slot_layout(counts, eidx, rank, n_blocks):
    bm = MOE_BLOCK
    padded = (counts + bm - 1) // bm * bm
    padded_end = jnp.cumsum(padded)
    start = padded_end - padded
    experts = jnp.arange(N_EXPERTS, dtype=i32)
    lookup = lambda table, idx: jnp.sum(jnp.where(idx[..., None] == experts, table, 0), axis=-1)
    dest = pl.pallas_call(
        _dest_kernel,
        grid_spec=pltpu.PrefetchScalarGridSpec(
            num_scalar_prefetch=1, grid=(1,),
            in_specs=[pl.BlockSpec(eidx.shape, lambda i, st: (0, 0)), pl.BlockSpec(rank.shape, lambda i, st: (0, 0))],
            out_specs=pl.BlockSpec(rank.shape, lambda i, st: (0, 0))),
        out_shape=jax.ShapeDtypeStruct(rank.shape, i32),
        name="moe_dest",
    )(start.astype(i32), eidx, rank)
    blk_row = jnp.arange(n_blocks, dtype=i32) * bm
    blk_e = jnp.minimum(jnp.sum((padded_end[None, :] <= blk_row[:, None]).astype(i32), axis=1), N_EXPERTS - 1)
    blk_cnt = jnp.clip(lookup(counts, blk_e) - (blk_row - lookup(start, blk_e)), 0, bm).astype(i32)
    none = jnp.int32(N_EXPERTS)
    nxt_e = jnp.min(jnp.where(blk_e[None, :] > blk_e[:, None], blk_e[None, :], none), axis=1)
    nxt_e = jnp.where(nxt_e == none, -1, nxt_e).astype(i32)
    return dest, blk_e, blk_cnt, nxt_e


def _layer_and_final_norm(x2d, mem, seq, g_mix, g_mem, w_in, w_mem_kv, na_rel_bias, g_grp, w_out, g_ffn,
                          router_w, router_b, w_gu, b_gu, w_down, b_down, g_final):
    t, d = x2d.shape
    b = t // seq
    q_na, k_na, v_na, u_ft, q_mem = _in_proj(x2d, g_mix, w_in.astype(bf16))
    k_mem, v_mem = _mem_kv(mem, g_mem, w_mem_kv.astype(bf16))
    shape3 = lambda a: a.reshape(b, seq, a.shape[-1])
    y_na = _neighbourhood_attention(shape3(q_na), shape3(k_na), shape3(v_na), _na_bias_table(na_rel_bias))
    y_ft = _fourier_mix(shape3(u_ft), _ft_tables(seq))
    y_na, y_ft = y_na.reshape(t, -1), y_ft.reshape(t, -1)
    w_out_bf16 = w_out.astype(bf16)
    rw_hi = router_w.astype(bf16)
    router_w2 = jnp.stack([rw_hi, (router_w - rw_hi.astype(f32)).astype(bf16)])

    tiles = t // TOKEN_TILE
    unit = tiles // sum(MOE_GROUP_SHARES)
    assert unit * sum(MOE_GROUP_SHARES) == tiles
    out = None
    tile0 = 0
    dest = router_b
    for share in MOE_GROUP_SHARES:
        group_tiles = share * unit
        n_blocks = (group_tiles * TOKEN_TILE * TOP_K) // MOE_BLOCK + N_EXPERTS
        x1, h2p, eidx, gates, rank, counts = _mix_out(
            x2d, y_na, y_ft, q_mem, k_mem, v_mem, g_grp, w_out_bf16, g_ffn, router_w2, router_b, seq,
            tile0, group_tiles, dest)
        dest, blk_e, blk_cnt, nxt_e = _slot_layout(counts[:, 0], eidx, rank, n_blocks)
        xs = _dispatch(h2p, dest, n_blocks * MOE_BLOCK)
        y_slots = _experts(xs, blk_e, blk_cnt, nxt_e, w_gu, b_gu, w_down, b_down)
        out = _combine(x1, gates, dest, y_slots, g_final, out, tile0, t)
        tile0 += group_tiles
    return out


def kernel(x, mem, g_mix, g_mem, w_in, w_mem_kv, na_rel_bias, g_grp, w_out, g_ffn, router_w, router_b,
           w_gu, b_gu, w_down, b_down, g_final):
    b, seq, d = x.shape
    depth = w_in.shape[0]
    assert depth == 1, "the final norm is fused into the single layer's combine step"
    out = _layer_and_final_norm(
        x.reshape(b * seq, d), mem, seq, g_mix[0], g_mem[0], w_in[0], w_mem_kv[0], na_rel_bias[0], g_grp[0],
        w_out[0], g_ffn[0], router_w[0], router_b[0], w_gu[0], b_gu[0], w_down[0], b_down[0], g_final)
    return out.reshape(b, seq, d)
```

```python
import numpy as np
import jax
import jax.numpy as jnp
from jax import lax
from jax.experimental import pallas as pl
from jax.experimental.pallas import tpu as pltpu
from jax.experimental.pallas import tpu_sc as plsc

f32 = jnp.float32
bf16 = jnp.bfloat16
u32 = jnp.uint32
i32 = jnp.int32

GRID_W = 64
NA_HEADS = 8
NA_HEAD_DIM = 64
NA_WIN_ROWS = 8
NA_WIN_COLS = 16
FT_GROUPS = 4
FT_GROUP_DIM = 128
MEM_HEADS = 4
MEM_HEAD_DIM = 128
NA_WIDTH = NA_HEADS * NA_HEAD_DIM
FT_WIDTH = FT_GROUPS * FT_GROUP_DIM
MEM_WIDTH = MEM_HEADS * MEM_HEAD_DIM
N_EXPERTS = 32
TOP_K = 4
SWIGLU_LIMIT = 7.0
SWIGLU_ALPHA = 1.702
EPS = 1e-6

LANES = 128
SUBLANES = 8
VMEM_LIMIT_BYTES = 56 * 1024 * 1024
SC_CORES = 2
SC_SUBCORES = 16
SC_GATHER_CHUNK = 64

TOKEN_TILE = 1024
IN_PROJ_TILE = 1024
MOE_BLOCK = 512
MOE_SUB_BLOCK = 64
WEIGHT_CAST_ROWS = 128
MOE_GROUP_SHARES = (5, 3)
NA_ROW_UNROLL = 32
FT_N1 = 64
FT_N2 = 128
FT_K1_BLOCK = 16
FT_BATCH_BLOCK = 4
LOG2_E = 1.4426950408889634
MASK_VALUE = -jnp.inf


def _params(*semantics):
    return pltpu.CompilerParams(dimension_semantics=semantics, vmem_limit_bytes=VMEM_LIMIT_BYTES)


def _rms_scale(x):
    return x * lax.rsqrt(jnp.mean(x * x, axis=-1, keepdims=True) + EPS)


def _softmax_rows(s):
    p = jnp.exp(s - jnp.max(s, axis=-1, keepdims=True))
    return p / jnp.sum(p, axis=-1, keepdims=True)


def _bf16_bits(x):
    return pltpu.bitcast(x.astype(bf16).astype(f32), u32)


def _pack2(lo, hi):
    return (_bf16_bits(lo) >> 16) | (_bf16_bits(hi) & jnp.uint32(0xFFFF0000))


def _unpack2(w):
    return pltpu.bitcast(w << 16, f32), pltpu.bitcast(w & jnp.uint32(0xFFFF0000), f32)


def _pack_bf16_pairs(x):
    n = x.shape[1] // 2
    return _pack2(x[:, :n], x[:, n:])


def _unpack_bf16_pairs(w):
    return jnp.concatenate(_unpack2(w), axis=1)


def _in_proj_kernel(x_ref, g_ref, w_ref, qna_ref, kna_ref, vna_ref, uft_ref, qmem_ref):
    h = _rms_scale(x_ref[...]) * g_ref[...]
    proj = jnp.dot(h.astype(bf16), w_ref[...], preferred_element_type=f32)
    o = NA_WIDTH
    qna_ref[...] = (proj[:, :o] * (NA_HEAD_DIM ** -0.5 * LOG2_E)).astype(bf16)
    kna_ref[...] = proj[:, o:2 * o].astype(bf16)
    vna_ref[...] = proj[:, 2 * o:3 * o].astype(bf16)
    uft_ref[...] = _pack_bf16_pairs(proj[:, 3 * o:3 * o + FT_WIDTH])
    qmem_ref[...] = proj[:, 3 * o + FT_WIDTH:].astype(bf16)


def _in_proj(x2d, g_mix, w_in_bf16):
    t, d = x2d.shape
    tm = IN_PROJ_TILE
    row = lambda w: pl.BlockSpec((tm, w), lambda i: (i, 0))
    return pl.pallas_call(
        _in_proj_kernel,
        grid=(t // tm,),
        in_specs=[row(d), pl.BlockSpec((1, d), lambda i: (0, 0)),
                  pl.BlockSpec(w_in_bf16.shape, lambda i: (0, 0))],
        out_specs=[row(NA_WIDTH), row(NA_WIDTH), row(NA_WIDTH), row(FT_WIDTH // 2), row(MEM_WIDTH)],
        out_shape=[jax.ShapeDtypeStruct((t, NA_WIDTH), bf16)] * 3
        + [jax.ShapeDtypeStruct((t, FT_WIDTH // 2), u32), jax.ShapeDtypeStruct((t, MEM_WIDTH), bf16)],
        compiler_params=_params("parallel"),
        name="in_proj",
    )(x2d, g_mix.reshape(1, d), w_in_bf16)


def _na_bias_table(rel_bias):
    c = np.arange(GRID_W)
    dc_idx = np.clip(c[None, :] - c[:, None], -(NA_WIN_COLS - 1), NA_WIN_COLS - 1) + (NA_WIN_COLS - 1)
    col_start = np.clip(c - NA_WIN_COLS // 2, 0, GRID_W - NA_WIN_COLS)
    col_in = (c[None, :] >= col_start[:, None]) & (c[None, :] < col_start[:, None] + NA_WIN_COLS)
    pick_c = jnp.asarray(dc_idx[:, :, None] == np.arange(2 * NA_WIN_COLS - 1), f32)
    cols = jnp.einsum("hab,qcb->haqc", rel_bias.astype(f32), pick_c, precision=lax.Precision.HIGHEST)
    cols = jnp.where(col_in[None, None], cols * LOG2_E, MASK_VALUE)
    tab = pl.pallas_call(
        _na_bias_expand_kernel,
        grid=(NA_WIN_ROWS,),
        in_specs=[pl.BlockSpec(cols.shape, lambda s: (0, 0, 0, 0))],
        out_specs=pl.BlockSpec((1, NA_HEADS, GRID_W, NA_WIN_ROWS * GRID_W), lambda s: (s, 0, 0, 0)),
        out_shape=jax.ShapeDtypeStruct((NA_WIN_ROWS, NA_HEADS, GRID_W, NA_WIN_ROWS * GRID_W), f32),
        compiler_params=_params("parallel"),
        name="na_bias_expand",
    )(cols)
    return tab.reshape(NA_WIN_ROWS, NA_HEADS // 2, 2 * GRID_W, NA_WIN_ROWS * GRID_W)


def _na_bias_expand_kernel(cols_ref, tab_ref):
    s = pl.program_id(0)
    for j in range(0, NA_WIN_ROWS, 2):
        a = j - s + (NA_WIN_ROWS - 1)
        pair = jnp.concatenate([cols_ref[:, pl.ds(a, 1)], cols_ref[:, pl.ds(a + 1, 1)]], axis=-1)
        tab_ref[0, :, :, pl.ds(j * GRID_W, 2 * GRID_W)] = pair[:, 0]


def _na_kernel(q_ref, k_ref, v_ref, bias_ref, o_ref):
    rows = q_ref.shape[1] // GRID_W
    win = NA_WIN_ROWS * GRID_W
    first_head = lax.broadcasted_iota(i32, (GRID_W, 2 * NA_HEAD_DIM), 1) < NA_HEAD_DIM

    def body(it, carry):
        scores, values, q_offsets = [], [], []
        for u in range(NA_ROW_UNROLL):
            r = it * NA_ROW_UNROLL + u
            row_start = jnp.clip(r - NA_WIN_ROWS // 2, 0, rows - NA_WIN_ROWS)
            q0 = pl.multiple_of(r * GRID_W, GRID_W)
            k0 = pl.multiple_of(row_start * GRID_W, GRID_W)
            q = q_ref[0, pl.ds(q0, GRID_W), :]
            zero = jnp.zeros_like(q)
            qm = jnp.concatenate([jnp.where(first_head, q, zero), jnp.where(first_head, zero, q)], axis=0)
            s = lax.dot_general(qm, k_ref[0, pl.ds(k0, win), :], (((1,), (1,)), ((), ())),
                                preferred_element_type=f32)
            scores.append(s + bias_ref[r - row_start, 0])
            values.append(v_ref[0, pl.ds(k0, win), :])
            q_offsets.append(q0)
        s = jnp.concatenate(scores, axis=0)
        p = jnp.exp2(s - jnp.max(s, axis=-1, keepdims=True))
        inv_den = 1.0 / jnp.sum(p, axis=-1, keepdims=True)
        p = p.astype(bf16)
        for u in range(NA_ROW_UNROLL):
            sl = slice(u * 2 * GRID_W, (u + 1) * 2 * GRID_W)
            o = jnp.dot(p[sl], values[u], preferred_element_type=f32) * inv_den[sl]
            o_ref[0, pl.ds(q_offsets[u], GRID_W), :] = jnp.where(
                first_head, o[:GRID_W], o[GRID_W:]).astype(o_ref.dtype)
        return carry

    lax.fori_loop(0, rows // NA_ROW_UNROLL, body, 0)


def _neighbourhood_attention(q, k, v, bias_tab):
    b, s, _ = q.shape
    pair = 2 * NA_HEAD_DIM
    qkv_spec = pl.BlockSpec((1, s, pair), lambda bi, hp: (bi, 0, hp))
    return pl.pallas_call(
        _na_kernel,
        grid=(b, NA_HEADS // 2),
        in_specs=[qkv_spec, qkv_spec, qkv_spec,
                  pl.BlockSpec((NA_WIN_ROWS, 1, 2 * GRID_W, NA_WIN_ROWS * GRID_W), lambda bi, hp: (0, hp, 0, 0))],
        out_specs=qkv_spec,
        out_shape=jax.ShapeDtypeStruct((b, s, NA_WIDTH), bf16),
        compiler_params=_params("parallel", "parallel"),
        name="neighbourhood_attention",
    )(q, k, v, bias_tab)


def _ft_tables(seq):
    assert seq == FT_N1 * FT_N2
    n_blk = FT_N2 // SUBLANES
    k1 = np.arange(FT_N1)[:, None, None, None]
    sr = np.arange(SUBLANES)[None, :, None, None]
    n1 = np.arange(FT_N1)[None, None, :, None]
    sc = np.arange(SUBLANES)[None, None, None, :]
    stage1 = np.zeros((n_blk, 2, FT_N1, SUBLANES, FT_N1, SUBLANES), np.float64)
    for blk in range(n_blk):
        n = FT_N2 * n1 + SUBLANES * blk + sr
        ang = 2.0 * np.pi * ((k1 * n) % seq) / seq
        eye = (sr == sc)
        stage1[blk, 0] = np.cos(ang) * eye
        stage1[blk, 1] = -np.sin(ang) * eye
    stage1 = stage1.reshape(n_blk, 2 * FT_N1 * SUBLANES, FT_N1 * SUBLANES)
    a = np.arange(FT_N2)
    ang2 = 2.0 * np.pi * ((a[:, None] * a[None, :]) % FT_N2) / FT_N2
    c2, s2 = np.cos(ang2), np.sin(ang2)
    stage2 = np.block([[c2, s2], [-s2, c2]])
    g = np.arange(FT_GROUP_DIM)
    angc = 2.0 * np.pi * ((g[:, None] * g[None, :]) % FT_GROUP_DIM) / FT_GROUP_DIM
    norm = 1.0 / np.sqrt(seq * FT_GROUP_DIM)
    chan = np.concatenate([np.cos(angc), np.sin(angc)], axis=0) * norm
    return (jnp.asarray(stage1, bf16), jnp.asarray(stage2, bf16), jnp.asarray(chan, bf16))


def _ft_stage1_kernel(u_ref, m_ref, z_ref):
    rows = FT_N1 * SUBLANES
    nb = u_ref.shape[0]
    u = jnp.concatenate([_unpack_bf16_pairs(u_ref[bb].reshape(rows, FT_WIDTH // 2)) for bb in range(nb)],
                        axis=1).astype(bf16)
    z = jnp.dot(m_ref[0], u, preferred_element_type=f32)
    for bb in range(nb):
        zb = z[:, bb * FT_WIDTH:(bb + 1) * FT_WIDTH]
        z_ref[bb] = _pack2(zb[:rows], zb[rows:]).reshape(FT_N1, SUBLANES, FT_WIDTH)


def _ft_stage2_kernel(z_ref, s2_ref, cs_ref, y_ref):
    gd = FT_GROUP_DIM
    xs = []
    for kk in range(FT_K1_BLOCK):
        zz = jnp.concatenate(_unpack2(z_ref[0, kk]), axis=0).astype(bf16)
        xs.append(jnp.dot(s2_ref[...], zz, preferred_element_type=f32))
    for g in range(FT_GROUPS):
        cols = slice(g * gd, (g + 1) * gd)
        xg = jnp.concatenate([jnp.concatenate([x[:FT_N2, cols], x[FT_N2:, cols]], axis=1) for x in xs], axis=0)
        og = jnp.dot(xg.astype(bf16), cs_ref[...], preferred_element_type=f32)
        for kk in range(FT_K1_BLOCK):
            y_ref[0, kk, :, cols] = og[kk * FT_N2:(kk + 1) * FT_N2].astype(y_ref.dtype)


def _fourier_mix(u_packed, tables):
    u = u_packed
    b, s, _ = u.shape
    c = FT_WIDTH
    assert b % FT_BATCH_BLOCK == 0 and FT_N1 % FT_K1_BLOCK == 0
    stage1, stage2, chan = tables
    n_blk = FT_N2 // SUBLANES
    z = pl.pallas_call(
        _ft_stage1_kernel,
        grid=(n_blk, b // FT_BATCH_BLOCK),
        in_specs=[pl.BlockSpec((FT_BATCH_BLOCK, FT_N1, SUBLANES, c // 2), lambda j, bi: (bi, 0, j, 0)),
                  pl.BlockSpec((1,) + stage1.shape[1:], lambda j, bi: (j, 0, 0))],
        out_specs=pl.BlockSpec((FT_BATCH_BLOCK, FT_N1, SUBLANES, c), lambda j, bi: (bi, 0, j, 0)),
        out_shape=jax.ShapeDtypeStruct((b, FT_N1, FT_N2, c), u32),
        compiler_params=_params("parallel", "parallel"),
        name="fourier_stage1",
    )(u.reshape(b, FT_N1, FT_N2, c // 2), stage1)
    y = pl.pallas_call(
        _ft_stage2_kernel,
        grid=(b, FT_N1 // FT_K1_BLOCK),
        in_specs=[pl.BlockSpec((1, FT_K1_BLOCK, FT_N2, c), lambda bi, kb: (bi, kb, 0, 0)),
                  pl.BlockSpec(stage2.shape, lambda bi, kb: (0, 0)),
                  pl.BlockSpec(chan.shape, lambda bi, kb: (0, 0))],
        out_specs=pl.BlockSpec((1, FT_K1_BLOCK, FT_N2, c), lambda bi, kb: (bi, kb, 0, 0)),
        out_shape=jax.ShapeDtypeStruct((b, FT_N1, FT_N2, c), bf16),
        compiler_params=_params("parallel", "parallel"),
        name="fourier_stage2",
    )(z, stage2, chan)
    return y.transpose(0, 2, 1, 3).reshape(b, s, c)


def _mem_kv_kernel(mem_ref, g_ref, w_ref, k_ref, v_ref):
    mn = _rms_scale(mem_ref[0]) * g_ref[...]
    kv = jnp.dot(mn.astype(bf16), w_ref[...], preferred_element_type=f32)
    k_ref[0] = kv[:, :MEM_WIDTH].astype(bf16)
    v_ref[0] = kv[:, MEM_WIDTH:].astype(bf16)


def _mem_kv(mem, g_mem, w_kv_bf16):
    b, m, d = mem.shape
    kv_spec = pl.BlockSpec((1, m, MEM_WIDTH), lambda bi: (bi, 0, 0))
    return pl.pallas_call(
        _mem_kv_kernel,
        grid=(b,),
        in_specs=[pl.BlockSpec((1, m, d), lambda bi: (bi, 0, 0)), pl.BlockSpec((1, d), lambda bi: (0, 0)),
                  pl.BlockSpec(w_kv_bf16.shape, lambda bi: (0, 0))],
        out_specs=[kv_spec, kv_spec],
        out_shape=[jax.ShapeDtypeStruct((b, m, MEM_WIDTH), bf16)] * 2,
        compiler_params=_params("parallel"),
        name="mem_kv",
    )(mem, g_mem.reshape(1, d), w_kv_bf16)


def _mix_out_kernel(x_ref, yna_ref, yft_ref, qm_ref, km_ref, vm_ref, ggrp_ref, wout_ref, gffn_ref, rw_ref,
                    rb_ref, _order_ref, x1_ref, h2p_ref, eidx_ref, gate_ref, rank_ref, cnt_ref, carry_ref):
    tm = x_ref.shape[0]

    @pl.when(pl.program_id(0) == 0)
    def _():
        carry_ref[...] = jnp.zeros_like(carry_ref)

    q = qm_ref[...]
    km = km_ref[0]
    vm = vm_ref[0]
    heads = []
    for h in range(MEM_HEADS):
        sl = slice(h * MEM_HEAD_DIM, (h + 1) * MEM_HEAD_DIM)
        s = lax.dot_general(q[:, sl], km[:, sl], (((1,), (1,)), ((), ())), preferred_element_type=f32)
        p = _softmax_rows(s * (MEM_HEAD_DIM ** -0.5))
        heads.append(jnp.dot(p.astype(bf16), vm[:, sl], preferred_element_type=f32))
    ymem = jnp.concatenate(heads, axis=1)

    g = ggrp_ref[...]
    a, c = NA_WIDTH, NA_WIDTH + FT_WIDTH
    y = jnp.concatenate([_rms_scale(yna_ref[...].astype(f32)) * g[:, :a],
                         _rms_scale(yft_ref[...].astype(f32)) * g[:, a:c],
                         _rms_scale(ymem) * g[:, c:]], axis=1)
    x1 = x_ref[...] + jnp.dot(y.astype(bf16), wout_ref[...], preferred_element_type=f32)
    x1_ref[...] = x1
    h2 = _rms_scale(x1) * gffn_ref[...]
    h2p_ref[...] = _pack_bf16_pairs(h2)

    h_hi = h2.astype(bf16)
    h_lo = (h2 - h_hi.astype(f32)).astype(bf16)
    hh = jnp.dot(h_hi, rw_ref[...], preferred_element_type=f32)
    logits = (hh[:, :LANES] + hh[:, LANES:]
              + jnp.dot(h_lo, rw_ref[:, :LANES], preferred_element_type=f32)) + rb_ref[...]
    l = logits.T[:N_EXPERTS]
    row = lax.broadcasted_iota(i32, (N_EXPERTS, tm), 0).astype(f32)
    vals, idxs, sels = [], [], []
    for _ in range(TOP_K):
        m = jnp.max(l, axis=0, keepdims=True)
        idx = jnp.min(jnp.where(l == m, row, float(N_EXPERTS)), axis=0, keepdims=True)
        sel = row == idx
        vals.append(m)
        idxs.append(idx)
        sels.append(sel)
        l = jnp.where(sel, -jnp.inf, l)
    ex = [jnp.exp(v - vals[0]) for v in vals]
    den = ex[0] + ex[1] + ex[2] + ex[3]

    onehot = (sels[0] | sels[1] | sels[2] | sels[3]).astype(f32)
    earlier = (lax.broadcasted_iota(i32, (tm, tm), 0) < lax.broadcasted_iota(i32, (tm, tm), 1)).astype(bf16)
    before = jnp.dot(onehot.astype(bf16), earlier, preferred_element_type=f32) + carry_ref[...]
    ranks = [jnp.sum(jnp.where(sel, before, 0.0), axis=0, keepdims=True) for sel in sels]
    carry_ref[...] = carry_ref[...] + jnp.sum(onehot, axis=1, keepdims=True)
    cnt_ref[...] = carry_ref[...].astype(i32)

    eidx_ref[...] = jnp.concatenate(idxs, axis=0).astype(i32)
    rank_ref[...] = jnp.concatenate(ranks, axis=0).astype(i32)
    gates_t = jnp.concatenate([e / den for e in ex] + [jnp.zeros((LANES - TOP_K, tm), f32)], axis=0)
    gate_ref[...] = gates_t.T[:, :TOP_K]


def _mix_out(x2d, y_na, y_ft, q_mem, k_mem, v_mem, g_grp, w_out_bf16, g_ffn, router_w2, router_b, seq,
             tile0, n_tiles, order_after):
    d = x2d.shape[1]
    tm = TOKEN_TILE
    t = n_tiles * tm
    steps_per_batch = seq // tm
    m = k_mem.shape[1]
    row_in = lambda w: pl.BlockSpec((tm, w), lambda i: (i + tile0, 0))
    row_out = lambda w: pl.BlockSpec((tm, w), lambda i: (i, 0))
    full = lambda a: pl.BlockSpec(a.shape, lambda i: (0,) * a.ndim)
    kv_spec = pl.BlockSpec((1, m, MEM_WIDTH), lambda i: ((i + tile0) // steps_per_batch, 0, 0))
    g_grp2, g_ffn2 = g_grp.reshape(1, -1), g_ffn.reshape(1, d)
    rb2 = jnp.pad(router_b.reshape(1, N_EXPERTS), ((0, 0), (0, LANES - N_EXPERTS)))
    router_w2 = jnp.pad(router_w2, ((0, 0), (0, 0), (0, LANES - N_EXPERTS)))
    router_w2 = jnp.concatenate([router_w2[0], router_w2[1]], axis=1)
    col_out = pl.BlockSpec((TOP_K, tm), lambda i: (0, i))
    return pl.pallas_call(
        _mix_out_kernel,
        grid=(n_tiles,),
        in_specs=[row_in(d), row_in(NA_WIDTH), row_in(FT_WIDTH), row_in(MEM_WIDTH), kv_spec, kv_spec,
                  full(g_grp2), full(w_out_bf16), full(g_ffn2), full(router_w2), full(rb2),
                  pl.BlockSpec(memory_space=pl.ANY)],
        out_specs=[row_out(d), row_out(d // 2), col_out, row_out(TOP_K), col_out,
                   pl.BlockSpec((N_EXPERTS, 1), lambda i: (0, 0))],
        out_shape=[jax.ShapeDtypeStruct((t, d), f32), jax.ShapeDtypeStruct((t, d // 2), u32),
                   jax.ShapeDtypeStruct((TOP_K, t), i32), jax.ShapeDtypeStruct((t, TOP_K), f32),
                   jax.ShapeDtypeStruct((TOP_K, t), i32), jax.ShapeDtypeStruct((N_EXPERTS, 1), i32)],
        scratch_shapes=[pltpu.VMEM((N_EXPERTS, 1), f32)],
        compiler_params=_params("arbitrary"),
        name="mix_out_router",
    )(x2d, y_na, y_ft, q_mem, k_mem, v_mem, g_grp2, w_out_bf16, g_ffn2, router_w2, rb2, order_after)


def _sc_mesh():
    return plsc.VectorSubcoreMesh(core_axis_name="c", subcore_axis_name="s",
                                  num_cores=SC_CORES, num_subcores=SC_SUBCORES)


def _dispatch(h2p, dest, n_slots):
    t, w = h2p.shape
    workers = SC_CORES * SC_SUBCORES
    chunk = SC_GATHER_CHUNK
    per = t // workers
    steps = per // chunk
    assert per * workers == t and steps * chunk == per and steps % 2 == 0
    idx = dest.reshape(TOP_K, workers, steps, chunk)

    def body(h_hbm, idx_hbm, out_hbm, idx_v, rows_v, rsem, ssem):
        wid = lax.axis_index("s") * SC_CORES + lax.axis_index("c")
        base = wid * per
        for j in range(TOP_K):
            pltpu.sync_copy(idx_hbm.at[j, wid], idx_v.at[j])

        def read(c, slot):
            return pltpu.make_async_copy(h_hbm.at[pl.ds(base + c * chunk, chunk)], rows_v.at[slot], rsem.at[slot])

        def scatters(c, slot):
            return [pltpu.make_async_copy(rows_v.at[slot], out_hbm.at[idx_v.at[j, c]], ssem.at[slot])
                    for j in range(TOP_K)]

        read(0, 0).start()

        @pl.loop(0, steps, step=2)
        def _(c0):
            for slot in range(2):
                c = c0 + slot
                read(c, slot).wait()
                for cp in scatters(c, slot):
                    cp.start()

                @pl.when(c >= 1)
                def _():
                    for cp in scatters(c - 1, 1 - slot):
                        cp.wait()

                @pl.when(c + 1 < steps)
                def _():
                    read(c + 1, 1 - slot).start()

        for cp in scatters(steps - 1, 1):
            cp.wait()

    return pl.kernel(
        body,
        out_type=jax.ShapeDtypeStruct((n_slots, w), h2p.dtype),
        mesh=_sc_mesh(),
        scratch_types=[pltpu.VMEM((TOP_K, steps, chunk), i32), pltpu.VMEM((2, chunk, w), h2p.dtype),
                       pltpu.SemaphoreType.DMA((2,)), pltpu.SemaphoreType.DMA((2,))],
        name="sc_dispatch_rows",
    )(h2p, idx)


def _expert_kernel(blk_e_ref, blk_cnt_ref, nxt_e_ref, xs_ref, wgu_hbm, bgu_ref, wd_hbm, bd_ref, y_ref,
                   wgu_f32, wd_f32, wgu_bf, wd_bf, sem):
    b = pl.program_id(0)
    e = blk_e_ref[b]
    cnt = blk_cnt_ref[b]
    bm = xs_ref.shape[0]
    de = wd_f32.shape[0]

    def fetch(expert):
        return (pltpu.make_async_copy(wgu_hbm.at[expert], wgu_f32, sem.at[0]),
                pltpu.make_async_copy(wd_hbm.at[expert], wd_f32, sem.at[1]))

    @pl.when(b == 0)
    def _():
        for cp in fetch(e):
            cp.start()

    @pl.when(jnp.logical_or(b == 0, e != blk_e_ref[jnp.maximum(b - 1, 0)]))
    def _():
        for cp in fetch(e):
            cp.wait()

        def convert(i, carry):
            rows = pl.ds(pl.multiple_of(i * WEIGHT_CAST_ROWS, WEIGHT_CAST_ROWS), WEIGHT_CAST_ROWS)
            wgu_bf[rows, :] = wgu_f32[rows, :].astype(bf16)
            wd_bf[rows, :] = wd_f32[rows, :].astype(bf16)
            return carry

        lax.fori_loop(0, wgu_f32.shape[0] // WEIGHT_CAST_ROWS, convert, 0)

        @pl.when(nxt_e_ref[b] >= 0)
        def _():
            for cp in fetch(nxt_e_ref[b]):
                cp.start()

    def ffn_rows(r0, n):
        rows = pl.ds(r0, n)
        valid = r0 + lax.broadcasted_iota(i32, (n, 1), 0) < cnt
        x = jnp.where(valid, _unpack_bf16_pairs(xs_ref[rows, :]), 0.0).astype(bf16)
        gu = jnp.dot(x, wgu_bf[...], preferred_element_type=f32) + bgu_ref[0]
        x_glu = jnp.minimum(gu[:, :de], SWIGLU_LIMIT)
        x_lin = jnp.clip(gu[:, de:], -SWIGLU_LIMIT, SWIGLU_LIMIT)
        act = x_glu * (1.0 / (1.0 + jnp.exp(-SWIGLU_ALPHA * x_glu))) * (x_lin + 1.0)
        y = jnp.dot(act.astype(bf16), wd_bf[...], preferred_element_type=f32) + bd_ref[0]
        y_ref[rows, :] = _pack_bf16_pairs(y)

    pieces = (cnt + MOE_SUB_BLOCK - 1) // MOE_SUB_BLOCK
    for k in range(bm // MOE_SUB_BLOCK + 1):
        @pl.when(pieces == k)
        def _(k=k):
            used = k * MOE_SUB_BLOCK
            if used:
                ffn_rows(0, used)
            if used < bm:
                y_ref[pl.ds(used, bm - used), :] = jnp.zeros((bm - used, y_ref.shape[1]), y_ref.dtype)


def _experts(xs, blk_e, blk_cnt, nxt_e, w_gu, b_gu, w_down, b_down):
    n_slots, w = xs.shape
    bm = MOE_BLOCK
    e, d, de2 = w_gu.shape
    de = w_down.shape[1]
    assert de == d, "one row loop converts both weight matrices"
    last = n_slots // bm - 1

    def rows_of_block(b, be, bc, ne):
        return (jnp.where(bc[b] > 0, b, last), 0)

    grid_spec = pltpu.PrefetchScalarGridSpec(
        num_scalar_prefetch=3,
        grid=(n_slots // bm,),
        in_specs=[pl.BlockSpec((bm, w), rows_of_block),
                  pl.BlockSpec(memory_space=pl.ANY),
                  pl.BlockSpec((1, 1, de2), lambda b, be, bc, ne: (be[b], 0, 0)),
                  pl.BlockSpec(memory_space=pl.ANY),
                  pl.BlockSpec((1, 1, d), lambda b, be, bc, ne: (be[b], 0, 0))],
        out_specs=pl.BlockSpec((bm, w), rows_of_block),
        scratch_shapes=[pltpu.VMEM((d, de2), f32), pltpu.VMEM((de, d), f32),
                        pltpu.VMEM((d, de2), bf16), pltpu.VMEM((de, d), bf16),
                        pltpu.SemaphoreType.DMA((2,))],
    )
    return pl.pallas_call(
        _expert_kernel,
        grid_spec=grid_spec,
        out_shape=jax.ShapeDtypeStruct((n_slots, w), u32),
        compiler_params=_params("arbitrary"),
        name="moe_experts",
    )(blk_e, blk_cnt, nxt_e, xs, w_gu, b_gu.reshape(e, 1, de2), w_down, b_down.reshape(e, 1, d))


def _sc_gather_rows(table, idx):
    n, w = idx.shape[0], table.shape[1]
    workers = SC_CORES * SC_SUBCORES
    chunk = SC_GATHER_CHUNK
    per = n // workers
    steps = per // chunk
    assert per * workers == n and steps * chunk == per and steps % 2 == 0

    def body(table_hbm, idx_hbm, out_hbm, idx_v, rows_v, gsem, wsem):
        base = (lax.axis_index("s") * SC_CORES + lax.axis_index("c")) * per
        pltpu.sync_copy(idx_hbm.at[pl.ds(base, per)], idx_v)

        def gather(c, slot):
            return pltpu.make_async_copy(table_hbm.at[idx_v.at[pl.ds(c * chunk, chunk)]], rows_v.at[slot],
                                         gsem.at[slot])

        def write(c, slot):
            return pltpu.make_async_copy(rows_v.at[slot], out_hbm.at[pl.ds(base + c * chunk, chunk)],
                                         wsem.at[slot])

        gather(0, 0).start()

        @pl.loop(0, steps, step=2)
        def _(c0):
            for slot in range(2):
                c = c0 + slot
                gather(c, slot).wait()
                write(c, slot).start()

                @pl.when(c >= 1)
                def _():
                    write(c - 1, 1 - slot).wait()

                @pl.when(c + 1 < steps)
                def _():
                    gather(c + 1, 1 - slot).start()

        write(steps - 1, 1).wait()

    return pl.kernel(
        body,
        out_type=jax.ShapeDtypeStruct((n, w), table.dtype),
        mesh=_sc_mesh(),
        scratch_types=[pltpu.VMEM((per,), i32), pltpu.VMEM((2, chunk, w), table.dtype),
                       pltpu.SemaphoreType.DMA((2,)), pltpu.SemaphoreType.DMA((2,))],
        name="sc_gather_rows",
    )(table, idx)


def _combine_kernel(x1_ref, gate_ref, gfin_ref, yg_ref, *rest):
    o_ref = rest[-1]
    acc = x1_ref[...]
    gates = gate_ref[...]
    for j in range(TOP_K):
        acc = acc + gates[:, j:j + 1] * _unpack_bf16_pairs(yg_ref[j])
    o_ref[...] = _rms_scale(acc) * gfin_ref[...]


def _combine(x1, gates, dest, y_slots, g_final, out_prev, tile0, total_tokens):
    t, d = x1.shape
    tm = TOKEN_TILE
    w = y_slots.shape[1]
    yg = _sc_gather_rows(y_slots, dest.reshape(-1)).reshape(TOP_K, t, w)
    in_specs = [pl.BlockSpec((tm, d), lambda i: (i, 0)),
                pl.BlockSpec((tm, TOP_K), lambda i: (i, 0)),
                pl.BlockSpec((1, d), lambda i: (0, 0)),
                pl.BlockSpec((TOP_K, tm, w), lambda i: (0, i, 0))]
    args = [x1, gates, g_final.reshape(1, d), yg]
    aliases = {}
    if out_prev is not None:
        in_specs.append(pl.BlockSpec(memory_space=pl.ANY))
        args.append(out_prev)
        aliases = {len(args) - 1: 0}
    return pl.pallas_call(
        _combine_kernel,
        grid=(t // tm,),
        in_specs=in_specs,
        out_specs=pl.BlockSpec((tm, d), lambda i: (i + tile0, 0)),
        out_shape=jax.ShapeDtypeStruct((total_tokens, d), f32),
        input_output_aliases=aliases,
        compiler_params=_params("parallel"),
        name="moe_combine",
    )(*args)


def _dest_kernel(start_ref, eidx_ref, rank_ref, dest_ref):
    eidx = eidx_ref[...]
    dest = rank_ref[...]
    for e in range(N_EXPERTS):
        dest = dest + jnp.where(eidx == e, start_ref[e], 0)
    dest_ref[...] = dest


def _slot_layout(counts, eidx, rank, n_blocks):
    bm = MOE_BLOCK
    padded = (counts + bm - 1) // bm * bm
    padded_end = jnp.cumsum(padded)
    start = padded_end - padded
    experts = jnp.arange(N_EXPERTS, dtype=i32)
    lookup = lambda table, idx: jnp.sum(jnp.where(idx[..., None] == experts, table, 0), axis=-1)
    dest = pl.pallas_call(
        _dest_kernel,
        grid_spec=pltpu.PrefetchScalarGridSpec(
            num_scalar_prefetch=1, grid=(1,),
            in_specs=[pl.BlockSpec(eidx.shape, lambda i, st: (0, 0)), pl.BlockSpec(rank.shape, lambda i, st: (0, 0))],
            out_specs=pl.BlockSpec(rank.shape, lambda i, st: (0, 0))),
        out_shape=jax.ShapeDtypeStruct(rank.shape, i32),
        name="moe_dest",
    )(start.astype(i32), eidx, rank)
    blk_row = jnp.arange(n_blocks, dtype=i32) * bm
    blk_e = jnp.minimum(jnp.sum((padded_end[None, :] <= blk_row[:, None]).astype(i32), axis=1), N_EXPERTS - 1)
    blk_cnt = jnp.clip(lookup(counts, blk_e) - (blk_row - lookup(start, blk_e)), 0, bm).astype(i32)
    none = jnp.int32(N_EXPERTS)
    nxt_e = jnp.min(jnp.where(blk_e[None, :] > blk_e[:, None], blk_e[None, :], none), axis=1)
    nxt_e = jnp.where(nxt_e == none, -1, nxt_e).astype(i32)
    return dest, blk_e, blk_cnt, nxt_e


def _layer_and_final_norm(x2d, mem, seq, g_mix, g_mem, w_in, w_mem_kv, na_rel_bias, g_grp, w_out, g_ffn,
                          router_w, router_b, w_gu, b_gu, w_down, b_down, g_final):
    t, d = x2d.shape
    b = t // seq
    q_na, k_na, v_na, u_ft, q_mem = _in_proj(x2d, g_mix, w_in.astype(bf16))
    k_mem, v_mem = _mem_kv(mem, g_mem, w_mem_kv.astype(bf16))
    shape3 = lambda a: a.reshape(b, seq, a.shape[-1])
    y_na = _neighbourhood_attention(shape3(q_na), shape3(k_na), shape3(v_na), _na_bias_table(na_rel_bias))
    y_ft = _fourier_mix(shape3(u_ft), _ft_tables(seq))
    y_na, y_ft = y_na.reshape(t, -1), y_ft.reshape(t, -1)
    w_out_bf16 = w_out.astype(bf16)
    rw_hi = router_w.astype(bf16)
    router_w2 = jnp.stack([rw_hi, (router_w - rw_hi.astype(f32)).astype(bf16)])

    tiles = t // TOKEN_TILE
    unit = tiles // sum(MOE_GROUP_SHARES)
    assert unit * sum(MOE_GROUP_SHARES) == tiles
    out = None
    tile0 = 0
    dest = router_b
    for share in MOE_GROUP_SHARES:
        group_tiles = share * unit
        n_blocks = (group_tiles * TOKEN_TILE * TOP_K) // MOE_BLOCK + N_EXPERTS
        x1, h2p, eidx, gates, rank, counts = _mix_out(
            x2d, y_na, y_ft, q_mem, k_mem, v_mem, g_grp, w_out_bf16, g_ffn, router_w2, router_b, seq,
            tile0, group_tiles, dest)
        dest, blk_e, blk_cnt, nxt_e = _slot_layout(counts[:, 0], eidx, rank, n_blocks)
        xs = _dispatch(h2p, dest, n_blocks * MOE_BLOCK)
        y_slots = _experts(xs, blk_e, blk_cnt, nxt_e, w_gu, b_gu, w_down, b_down)
        out = _combine(x1, gates, dest, y_slots, g_final, out, tile0, t)
        tile0 += group_tiles
    return out


def kernel(x, mem, g_mix, g_mem, w_in, w_mem_kv, na_rel_bias, g_grp, w_out, g_ffn, router_w, router_b,
           w_gu, b_gu, w_down, b_down, g_final):
    b, seq, d = x.shape
    depth = w_in.shape[0]
    assert depth == 1, "the final norm is fused into the single layer's combine step"
    out = _layer_and_final_norm(
        x.reshape(b * seq, d), mem, seq, g_mix[0], g_mem[0], w_in[0], w_mem_kv[0], na_rel_bias[0], g_grp[0],
        w_out[0], g_ffn[0], router_w[0], router_b[0], w_gu[0], b_gu[0], w_down[0], b_down[0], g_final)
    return out.reshape(b, seq, d)
```

```python
import numpy as np
import jax
import jax.numpy as jnp
from jax import lax
from jax.experimental import pallas as pl
from jax.experimental.pallas import tpu as pltpu
from jax.experimental.pallas import tpu_sc as plsc

f32 = jnp.float32
bf16 = jnp.bfloat16
u32 = jnp.uint32
i32 = jnp.int32

GRID_W = 64
NA_HEADS = 8
NA_HEAD_DIM = 64
NA_WIN_ROWS = 8
NA_WIN_COLS = 16
FT_GROUPS = 4
FT_GROUP_DIM = 128
MEM_HEADS = 4
MEM_HEAD_DIM = 128
NA_WIDTH = NA_HEADS * NA_HEAD_DIM
FT_WIDTH = FT_GROUPS * FT_GROUP_DIM
MEM_WIDTH = MEM_HEADS * MEM_HEAD_DIM
N_EXPERTS = 32
TOP_K = 4
SWIGLU_LIMIT = 7.0
SWIGLU_ALPHA = 1.702
EPS = 1e-6

LANES = 128
SUBLANES = 8
VMEM_LIMIT_BYTES = 56 * 1024 * 1024
SC_CORES = 2
SC_SUBCORES = 16
SC_GATHER_CHUNK = 64

TOKEN_TILE = 1024
IN_PROJ_TILE = 1024
MOE_BLOCK = 512
MOE_SUB_BLOCK = 64
WEIGHT_CAST_ROWS = 128
MOE_GROUP_SHARES = (5, 3)
NA_ROW_UNROLL = 32
FT_N1 = 64
FT_N2 = 128
FT_K1_BLOCK = 32
FT_BATCH_BLOCK = 4
LOG2_E = 1.4426950408889634
MASK_VALUE = -jnp.inf


def _params(*semantics):
    return pltpu.CompilerParams(dimension_semantics=semantics, vmem_limit_bytes=VMEM_LIMIT_BYTES)


def _rms_scale(x):
    return x * lax.rsqrt(jnp.mean(x * x, axis=-1, keepdims=True) + EPS)


def _softmax_rows(s):
    p = jnp.exp(s - jnp.max(s, axis=-1, keepdims=True))
    return p / jnp.sum(p, axis=-1, keepdims=True)


def _bf16_bits(x):
    return pltpu.bitcast(x.astype(bf16).astype(f32), u32)


def _pack2(lo, hi):
    return (_bf16_bits(lo) >> 16) | (_bf16_bits(hi) & jnp.uint32(0xFFFF0000))


def _unpack2(w):
    return pltpu.bitcast(w << 16, f32), pltpu.bitcast(w & jnp.uint32(0xFFFF0000), f32)


def _pack_bf16_pairs(x):
    n = x.shape[1] // 2
    return _pack2(x[:, :n], x[:, n:])


def _unpack_bf16_pairs(w):
    return jnp.concatenate(_unpack2(w), axis=1)


def _in_proj_kernel(x_ref, g_ref, w_ref, qna_ref, kna_ref, vna_ref, uft_ref, qmem_ref):
    h = _rms_scale(x_ref[...]) * g_ref[...]
    proj = jnp.dot(h.astype(bf16), w_ref[...], preferred_element_type=f32)
    o = NA_WIDTH
    qna_ref[...] = (proj[:, :o] * (NA_HEAD_DIM ** -0.5 * LOG2_E)).astype(bf16)
    kna_ref[...] = proj[:, o:2 * o].astype(bf16)
    vna_ref[...] = proj[:, 2 * o:3 * o].astype(bf16)
    uft_ref[...] = _pack_bf16_pairs(proj[:, 3 * o:3 * o + FT_WIDTH])
    qmem_ref[...] = proj[:, 3 * o + FT_WIDTH:].astype(bf16)


def _in_proj(x2d, g_mix, w_in_bf16):
    t, d = x2d.shape
    tm = IN_PROJ_TILE
    row = lambda w: pl.BlockSpec((tm, w), lambda i: (i, 0))
    return pl.pallas_call(
        _in_proj_kernel,
        grid=(t // tm,),
        in_specs=[row(d), pl.BlockSpec((1, d), lambda i: (0, 0)),
                  pl.BlockSpec(w_in_bf16.shape, lambda i: (0, 0))],
        out_specs=[row(NA_WIDTH), row(NA_WIDTH), row(NA_WIDTH), row(FT_WIDTH // 2), row(MEM_WIDTH)],
        out_shape=[jax.ShapeDtypeStruct((t, NA_WIDTH), bf16)] * 3
        + [jax.ShapeDtypeStruct((t, FT_WIDTH // 2), u32), jax.ShapeDtypeStruct((t, MEM_WIDTH), bf16)],
        compiler_params=_params("parallel"),
        name="in_proj",
    )(x2d, g_mix.reshape(1, d), w_in_bf16)


def _na_bias_table(rel_bias):
    c = np.arange(GRID_W)
    dc_idx = np.clip(c[None, :] - c[:, None], -(NA_WIN_COLS - 1), NA_WIN_COLS - 1) + (NA_WIN_COLS - 1)
    col_start = np.clip(c - NA_WIN_COLS // 2, 0, GRID_W - NA_WIN_COLS)
    col_in = (c[None, :] >= col_start[:, None]) & (c[None, :] < col_start[:, None] + NA_WIN_COLS)
    pick_c = jnp.asarray(dc_idx[:, :, None] == np.arange(2 * NA_WIN_COLS - 1), f32)
    cols = jnp.einsum("hab,qcb->haqc", rel_bias.astype(f32), pick_c, precision=lax.Precision.HIGHEST)
    cols = jnp.where(col_in[None, None], cols * LOG2_E, MASK_VALUE)
    tab = pl.pallas_call(
        _na_bias_expand_kernel,
        grid=(NA_WIN_ROWS,),
        in_specs=[pl.BlockSpec(cols.shape, lambda s: (0, 0, 0, 0))],
        out_specs=pl.BlockSpec((1, NA_HEADS, GRID_W, NA_WIN_ROWS * GRID_W), lambda s: (s, 0, 0, 0)),
        out_shape=jax.ShapeDtypeStruct((NA_WIN_ROWS, NA_HEADS, GRID_W, NA_WIN_ROWS * GRID_W), f32),
        compiler_params=_params("parallel"),
        name="na_bias_expand",
    )(cols)
    return tab.reshape(NA_WIN_ROWS, NA_HEADS // 2, 2 * GRID_W, NA_WIN_ROWS * GRID_W)


def _na_bias_expand_kernel(cols_ref, tab_ref):
    s = pl.program_id(0)
    for j in range(0, NA_WIN_ROWS, 2):
        a = j - s + (NA_WIN_ROWS - 1)
        pair = jnp.concatenate([cols_ref[:, pl.ds(a, 1)], cols_ref[:, pl.ds(a + 1, 1)]], axis=-1)
        tab_ref[0, :, :, pl.ds(j * GRID_W, 2 * GRID_W)] = pair[:, 0]


def _na_kernel(q_ref, k_ref, v_ref, bias_ref, o_ref):
    rows = q_ref.shape[1] // GRID_W
    win = NA_WIN_ROWS * GRID_W
    first_head = lax.broadcasted_iota(i32, (GRID_W, 2 * NA_HEAD_DIM), 1) < NA_HEAD_DIM

    def body(it, carry):
        scores, values, q_offsets = [], [], []
        for u in range(NA_ROW_UNROLL):
            r = it * NA_ROW_UNROLL + u
            row_start = jnp.clip(r - NA_WIN_ROWS // 2, 0, rows - NA_WIN_ROWS)
            q0 = pl.multiple_of(r * GRID_W, GRID_W)
            k0 = pl.multiple_of(row_start * GRID_W, GRID_W)
            q = q_ref[0, pl.ds(q0, GRID_W), :]
            zero = jnp.zeros_like(q)
            qm = jnp.concatenate([jnp.where(first_head, q, zero), jnp.where(first_head, zero, q)], axis=0)
            s = lax.dot_general(qm, k_ref[0, pl.ds(k0, win), :], (((1,), (1,)), ((), ())),
                                preferred_element_type=f32)
            scores.append(s + bias_ref[r - row_start, 0])
            values.append(v_ref[0, pl.ds(k0, win), :])
            q_offsets.append(q0)
        s = jnp.concatenate(scores, axis=0)
        p = jnp.exp2(s - jnp.max(s, axis=-1, keepdims=True))
        inv_den = 1.0 / jnp.sum(p, axis=-1, keepdims=True)
        p = p.astype(bf16)
        for u in range(NA_ROW_UNROLL):
            sl = slice(u * 2 * GRID_W, (u + 1) * 2 * GRID_W)
            o = jnp.dot(p[sl], values[u], preferred_element_type=f32) * inv_den[sl]
            o_ref[0, pl.ds(q_offsets[u], GRID_W), :] = jnp.where(
                first_head, o[:GRID_W], o[GRID_W:]).astype(o_ref.dtype)
        return carry

    lax.fori_loop(0, rows // NA_ROW_UNROLL, body, 0)


def _neighbourhood_attention(q, k, v, bias_tab):
    b, s, _ = q.shape
    pair = 2 * NA_HEAD_DIM
    qkv_spec = pl.BlockSpec((1, s, pair), lambda bi, hp: (bi, 0, hp))
    return pl.pallas_call(
        _na_kernel,
        grid=(b, NA_HEADS // 2),
        in_specs=[qkv_spec, qkv_spec, qkv_spec,
                  pl.BlockSpec((NA_WIN_ROWS, 1, 2 * GRID_W, NA_WIN_ROWS * GRID_W), lambda bi, hp: (0, hp, 0, 0))],
        out_specs=qkv_spec,
        out_shape=jax.ShapeDtypeStruct((b, s, NA_WIDTH), bf16),
        compiler_params=_params("parallel", "parallel"),
        name="neighbourhood_attention",
    )(q, k, v, bias_tab)


def _ft_tables(seq):
    assert seq == FT_N1 * FT_N2
    n_blk = FT_N2 // SUBLANES
    k1 = np.arange(FT_N1)[:, None, None, None]
    sr = np.arange(SUBLANES)[None, :, None, None]
    n1 = np.arange(FT_N1)[None, None, :, None]
    sc = np.arange(SUBLANES)[None, None, None, :]
    stage1 = np.zeros((n_blk, 2, FT_N1, SUBLANES, FT_N1, SUBLANES), np.float64)
    for blk in range(n_blk):
        n = FT_N2 * n1 + SUBLANES * blk + sr
        ang = 2.0 * np.pi * ((k1 * n) % seq) / seq
        eye = (sr == sc)
        stage1[blk, 0] = np.cos(ang) * eye
        stage1[blk, 1] = -np.sin(ang) * eye
    stage1 = stage1.reshape(n_blk, 2 * FT_N1 * SUBLANES, FT_N1 * SUBLANES)
    a = np.arange(FT_N2)
    ang2 = 2.0 * np.pi * ((a[:, None] * a[None, :]) % FT_N2) / FT_N2
    c2, s2 = np.cos(ang2), np.sin(ang2)
    stage2 = np.block([[c2, s2], [-s2, c2]])
    g = np.arange(FT_GROUP_DIM)
    angc = 2.0 * np.pi * ((g[:, None] * g[None, :]) % FT_GROUP_DIM) / FT_GROUP_DIM
    norm = 1.0 / np.sqrt(seq * FT_GROUP_DIM)
    chan = np.concatenate([np.cos(angc), np.sin(angc)], axis=0) * norm
    return (jnp.asarray(stage1, bf16), jnp.asarray(stage2, bf16), jnp.asarray(chan, bf16))


def _ft_stage1_kernel(u_ref, m_ref, z_ref):
    rows = FT_N1 * SUBLANES
    nb = u_ref.shape[0]
    u = jnp.concatenate([_unpack_bf16_pairs(u_ref[bb].reshape(rows, FT_WIDTH // 2)) for bb in range(nb)],
                        axis=1).astype(bf16)
    z = jnp.dot(m_ref[0], u, preferred_element_type=f32)
    for bb in range(nb):
        zb = z[:, bb * FT_WIDTH:(bb + 1) * FT_WIDTH]
        z_ref[bb] = _pack2(zb[:rows], zb[rows:]).reshape(FT_N1, SUBLANES, FT_WIDTH)


def _ft_stage2_kernel(z_ref, s2_ref, cs_ref, y_ref):
    gd = FT_GROUP_DIM
    xs = []
    for kk in range(FT_K1_BLOCK):
        zz = jnp.concatenate(_unpack2(z_ref[0, kk]), axis=0).astype(bf16)
        xs.append(jnp.dot(s2_ref[...], zz, preferred_element_type=f32))
    for g in range(FT_GROUPS):
        cols = slice(g * gd, (g + 1) * gd)
        xg = jnp.concatenate([jnp.concatenate([x[:FT_N2, cols], x[FT_N2:, cols]], axis=1) for x in xs], axis=0)
        og = jnp.dot(xg.astype(bf16), cs_ref[...], preferred_element_type=f32)
        for kk in range(FT_K1_BLOCK):
            y_ref[0, kk, :, cols] = og[kk * FT_N2:(kk + 1) * FT_N2].astype(y_ref.dtype)


def _fourier_mix(u_packed, tables):
    u = u_packed
    b, s, _ = u.shape
    c = FT_WIDTH
    assert b % FT_BATCH_BLOCK == 0 and FT_N1 % FT_K1_BLOCK == 0
    stage1, stage2, chan = tables
    n_blk = FT_N2 // SUBLANES
    z = pl.pallas_call(
        _ft_stage1_kernel,
        grid=(n_blk, b // FT_BATCH_BLOCK),
        in_specs=[pl.BlockSpec((FT_BATCH_BLOCK, FT_N1, SUBLANES, c // 2), lambda j, bi: (bi, 0, j, 0)),
                  pl.BlockSpec((1,) + stage1.shape[1:], lambda j, bi: (j, 0, 0))],
        out_specs=pl.BlockSpec((FT_BATCH_BLOCK, FT_N1, SUBLANES, c), lambda j, bi: (bi, 0, j, 0)),
        out_shape=jax.ShapeDtypeStruct((b, FT_N1, FT_N2, c), u32),
        compiler_params=_params("parallel", "parallel"),
        name="fourier_stage1",
    )(u.reshape(b, FT_N1, FT_N2, c // 2), stage1)
    y = pl.pallas_call(
        _ft_stage2_kernel,
        grid=(b, FT_N1 // FT_K1_BLOCK),
        in_specs=[pl.BlockSpec((1, FT_K1_BLOCK, FT_N2, c), lambda bi, kb: (bi, kb, 0, 0)),
                  pl.BlockSpec(stage2.shape, lambda bi, kb: (0, 0)),
                  pl.BlockSpec(chan.shape, lambda bi, kb: (0, 0))],
        out_specs=pl.BlockSpec((1, FT_K1_BLOCK, FT_N2, c), lambda bi, kb: (bi, kb, 0, 0)),
        out_shape=jax.ShapeDtypeStruct((b, FT_N1, FT_N2, c), bf16),
        compiler_params=_params("parallel", "parallel"),
        name="fourier_stage2",
    )(z, stage2, chan)
    return y.transpose(0, 2, 1, 3).reshape(b, s, c)


def _mem_kv_kernel(mem_ref, g_ref, w_ref, k_ref, v_ref):
    mn = _rms_scale(mem_ref[0]) * g_ref[...]
    kv = jnp.dot(mn.astype(bf16), w_ref[...], preferred_element_type=f32)
    k_ref[0] = kv[:, :MEM_WIDTH].astype(bf16)
    v_ref[0] = kv[:, MEM_WIDTH:].astype(bf16)


def _mem_kv(mem, g_mem, w_kv_bf16):
    b, m, d = mem.shape
    kv_spec = pl.BlockSpec((1, m, MEM_WIDTH), lambda bi: (bi, 0, 0))
    return pl.pallas_call(
        _mem_kv_kernel,
        grid=(b,),
        in_specs=[pl.BlockSpec((1, m, d), lambda bi: (bi, 0, 0)), pl.BlockSpec((1, d), lambda bi: (0, 0)),
                  pl.BlockSpec(w_kv_bf16.shape, lambda bi: (0, 0))],
        out_specs=[kv_spec, kv_spec],
        out_shape=[jax.ShapeDtypeStruct((b, m, MEM_WIDTH), bf16)] * 2,
        compiler_params=_params("parallel"),
        name="mem_kv",
    )(mem, g_mem.reshape(1, d), w_kv_bf16)


def _mix_out_kernel(x_ref, yna_ref, yft_ref, qm_ref, km_ref, vm_ref, ggrp_ref, wout_ref, gffn_ref, rw_ref,
                    rb_ref, _order_ref, x1_ref, h2p_ref, eidx_ref, gate_ref, rank_ref, cnt_ref, carry_ref):
    tm = x_ref.shape[0]

    @pl.when(pl.program_id(0) == 0)
    def _():
        carry_ref[...] = jnp.zeros_like(carry_ref)

    q = qm_ref[...]
    km = km_ref[0]
    vm = vm_ref[0]
    heads = []
    for h in range(MEM_HEADS):
        sl = slice(h * MEM_HEAD_DIM, (h + 1) * MEM_HEAD_DIM)
        s = lax.dot_general(q[:, sl], km[:, sl], (((1,), (1,)), ((), ())), preferred_element_type=f32)
        p = _softmax_rows(s * (MEM_HEAD_DIM ** -0.5))
        heads.append(jnp.dot(p.astype(bf16), vm[:, sl], preferred_element_type=f32))
    ymem = jnp.concatenate(heads, axis=1)

    g = ggrp_ref[...]
    a, c = NA_WIDTH, NA_WIDTH + FT_WIDTH
    y = jnp.concatenate([_rms_scale(yna_ref[...].astype(f32)) * g[:, :a],
                         _rms_scale(yft_ref[...].astype(f32)) * g[:, a:c],
                         _rms_scale(ymem) * g[:, c:]], axis=1)
    x1 = x_ref[...] + jnp.dot(y.astype(bf16), wout_ref[...], preferred_element_type=f32)
    x1_ref[...] = x1
    h2 = _rms_scale(x1) * gffn_ref[...]
    h2p_ref[...] = _pack_bf16_pairs(h2)

    h_hi = h2.astype(bf16)
    h_lo = (h2 - h_hi.astype(f32)).astype(bf16)
    hh = jnp.dot(h_hi, rw_ref[...], preferred_element_type=f32)
    logits = (hh[:, :LANES] + hh[:, LANES:]
              + jnp.dot(h_lo, rw_ref[:, :LANES], preferred_element_type=f32)) + rb_ref[...]
    l = logits.T[:N_EXPERTS]
    row = lax.broadcasted_iota(i32, (N_EXPERTS, tm), 0).astype(f32)
    vals, idxs, sels = [], [], []
    for _ in range(TOP_K):
        m = jnp.max(l, axis=0, keepdims=True)
        idx = jnp.min(jnp.where(l == m, row, float(N_EXPERTS)), axis=0, keepdims=True)
        sel = row == idx
        vals.append(m)
        idxs.append(idx)
        sels.append(sel)
        l = jnp.where(sel, -jnp.inf, l)
    ex = [jnp.exp(v - vals[0]) for v in vals]
    den = ex[0] + ex[1] + ex[2] + ex[3]

    onehot = (sels[0] | sels[1] | sels[2] | sels[3]).astype(f32)
    earlier = (lax.broadcasted_iota(i32, (tm, tm), 0) < lax.broadcasted_iota(i32, (tm, tm), 1)).astype(bf16)
    before = jnp.dot(onehot.astype(bf16), earlier, preferred_element_type=f32) + carry_ref[...]
    ranks = [jnp.sum(jnp.where(sel, before, 0.0), axis=0, keepdims=True) for sel in sels]
    carry_ref[...] = carry_ref[...] + jnp.sum(onehot, axis=1, keepdims=True)
    cnt_ref[...] = carry_ref[...].astype(i32)

    eidx_ref[...] = jnp.concatenate(idxs, axis=0).astype(i32)
    rank_ref[...] = jnp.concatenate(ranks, axis=0).astype(i32)
    gates_t = jnp.concatenate([e / den for e in ex] + [jnp.zeros((LANES - TOP_K, tm), f32)], axis=0)
    gate_ref[...] = gates_t.T[:, :TOP_K]


def _mix_out(x2d, y_na, y_ft, q_mem, k_mem, v_mem, g_grp, w_out_bf16, g_ffn, router_w2, router_b, seq,
             tile0, n_tiles, order_after):
    d = x2d.shape[1]
    tm = TOKEN_TILE
    t = n_tiles * tm
    steps_per_batch = seq // tm
    m = k_mem.shape[1]
    row_in = lambda w: pl.BlockSpec((tm, w), lambda i: (i + tile0, 0))
    row_out = lambda w: pl.BlockSpec((tm, w), lambda i: (i, 0))
    full = lambda a: pl.BlockSpec(a.shape, lambda i: (0,) * a.ndim)
    kv_spec = pl.BlockSpec((1, m, MEM_WIDTH), lambda i: ((i + tile0) // steps_per_batch, 0, 0))
    g_grp2, g_ffn2 = g_grp.reshape(1, -1), g_ffn.reshape(1, d)
    rb2 = jnp.pad(router_b.reshape(1, N_EXPERTS), ((0, 0), (0, LANES - N_EXPERTS)))
    router_w2 = jnp.pad(router_w2, ((0, 0), (0, 0), (0, LANES - N_EXPERTS)))
    router_w2 = jnp.concatenate([router_w2[0], router_w2[1]], axis=1)
    col_out = pl.BlockSpec((TOP_K, tm), lambda i: (0, i))
    return pl.pallas_call(
        _mix_out_kernel,
        grid=(n_tiles,),
        in_specs=[row_in(d), row_in(NA_WIDTH), row_in(FT_WIDTH), row_in(MEM_WIDTH), kv_spec, kv_spec,
                  full(g_grp2), full(w_out_bf16), full(g_ffn2), full(router_w2), full(rb2),
                  pl.BlockSpec(memory_space=pl.ANY)],
        out_specs=[row_out(d), row_out(d // 2), col_out, row_out(TOP_K), col_out,
                   pl.BlockSpec((N_EXPERTS, 1), lambda i: (0, 0))],
        out_shape=[jax.ShapeDtypeStruct((t, d), f32), jax.ShapeDtypeStruct((t, d // 2), u32),
                   jax.ShapeDtypeStruct((TOP_K, t), i32), jax.ShapeDtypeStruct((t, TOP_K), f32),
                   jax.ShapeDtypeStruct((TOP_K, t), i32), jax.ShapeDtypeStruct((N_EXPERTS, 1), i32)],
        scratch_shapes=[pltpu.VMEM((N_EXPERTS, 1), f32)],
        compiler_params=_params("arbitrary"),
        name="mix_out_router",
    )(x2d, y_na, y_ft, q_mem, k_mem, v_mem, g_grp2, w_out_bf16, g_ffn2, router_w2, rb2, order_after)


def _sc_mesh():
    return plsc.VectorSubcoreMesh(core_axis_name="c", subcore_axis_name="s",
                                  num_cores=SC_CORES, num_subcores=SC_SUBCORES)


def _dispatch(h2p, dest, n_slots):
    t, w = h2p.shape
    workers = SC_CORES * SC_SUBCORES
    chunk = SC_GATHER_CHUNK
    per = t // workers
    steps = per // chunk
    assert per * workers == t and steps * chunk == per and steps % 2 == 0
    idx = dest.reshape(TOP_K, workers, steps, chunk)

    def body(h_hbm, idx_hbm, out_hbm, idx_v, rows_v, rsem, ssem):
        wid = lax.axis_index("s") * SC_CORES + lax.axis_index("c")
        base = wid * per
        for j in range(TOP_K):
            pltpu.sync_copy(idx_hbm.at[j, wid], idx_v.at[j])

        def read(c, slot):
            return pltpu.make_async_copy(h_hbm.at[pl.ds(base + c * chunk, chunk)], rows_v.at[slot], rsem.at[slot])

        def scatters(c, slot):
            return [pltpu.make_async_copy(rows_v.at[slot], out_hbm.at[idx_v.at[j, c]], ssem.at[slot])
                    for j in range(TOP_K)]

        read(0, 0).start()

        @pl.loop(0, steps, step=2)
        def _(c0):
            for slot in range(2):
                c = c0 + slot
                read(c, slot).wait()
                for cp in scatters(c, slot):
                    cp.start()

                @pl.when(c >= 1)
                def _():
                    for cp in scatters(c - 1, 1 - slot):
                        cp.wait()

                @pl.when(c + 1 < steps)
                def _():
                    read(c + 1, 1 - slot).start()

        for cp in scatters(steps - 1, 1):
            cp.wait()

    return pl.kernel(
        body,
        out_type=jax.ShapeDtypeStruct((n_slots, w), h2p.dtype),
        mesh=_sc_mesh(),
        scratch_types=[pltpu.VMEM((TOP_K, steps, chunk), i32), pltpu.VMEM((2, chunk, w), h2p.dtype),
                       pltpu.SemaphoreType.DMA((2,)), pltpu.SemaphoreType.DMA((2,))],
        name="sc_dispatch_rows",
    )(h2p, idx)


def _expert_kernel(blk_e_ref, blk_cnt_ref, nxt_e_ref, xs_ref, wgu_hbm, bgu_ref, wd_hbm, bd_ref, y_ref,
                   wgu_f32, wd_f32, wgu_bf, wd_bf, sem):
    b = pl.program_id(0)
    e = blk_e_ref[b]
    cnt = blk_cnt_ref[b]
    bm = xs_ref.shape[0]
    de = wd_f32.shape[0]

    def fetch(expert):
        return (pltpu.make_async_copy(wgu_hbm.at[expert], wgu_f32, sem.at[0]),
                pltpu.make_async_copy(wd_hbm.at[expert], wd_f32, sem.at[1]))

    @pl.when(b == 0)
    def _():
        for cp in fetch(e):
            cp.start()

    @pl.when(jnp.logical_or(b == 0, e != blk_e_ref[jnp.maximum(b - 1, 0)]))
    def _():
        for cp in fetch(e):
            cp.wait()

        def convert(i, carry):
            rows = pl.ds(pl.multiple_of(i * WEIGHT_CAST_ROWS, WEIGHT_CAST_ROWS), WEIGHT_CAST_ROWS)
            wgu_bf[rows, :] = wgu_f32[rows, :].astype(bf16)
            wd_bf[rows, :] = wd_f32[rows, :].astype(bf16)
            return carry

        lax.fori_loop(0, wgu_f32.shape[0] // WEIGHT_CAST_ROWS, convert, 0)

        @pl.when(nxt_e_ref[b] >= 0)
        def _():
            for cp in fetch(nxt_e_ref[b]):
                cp.start()

    def ffn_rows(r0, n):
        rows = pl.ds(r0, n)
        valid = r0 + lax.broadcasted_iota(i32, (n, 1), 0) < cnt
        x = jnp.where(valid, _unpack_bf16_pairs(xs_ref[rows, :]), 0.0).astype(bf16)
        gu = jnp.dot(x, wgu_bf[...], preferred_element_type=f32) + bgu_ref[0]
        x_glu = jnp.minimum(gu[:, :de], SWIGLU_LIMIT)
        x_lin = jnp.clip(gu[:, de:], -SWIGLU_LIMIT, SWIGLU_LIMIT)
        act = x_glu * (1.0 / (1.0 + jnp.exp(-SWIGLU_ALPHA * x_glu))) * (x_lin + 1.0)
        y = jnp.dot(act.astype(bf16), wd_bf[...], preferred_element_type=f32) + bd_ref[0]
        y_ref[rows, :] = _pack_bf16_pairs(y)

    pieces = (cnt + MOE_SUB_BLOCK - 1) // MOE_SUB_BLOCK
    for k in range(bm // MOE_SUB_BLOCK + 1):
        @pl.when(pieces == k)
        def _(k=k):
            used = k * MOE_SUB_BLOCK
            if used:
                ffn_rows(0, used)
            if used < bm:
                y_ref[pl.ds(used, bm - used), :] = jnp.zeros((bm - used, y_ref.shape[1]), y_ref.dtype)


def _experts(xs, blk_e, blk_cnt, nxt_e, w_gu, b_gu, w_down, b_down):
    n_slots, w = xs.shape
    bm = MOE_BLOCK
    e, d, de2 = w_gu.shape
    de = w_down.shape[1]
    assert de == d, "one row loop converts both weight matrices"
    last = n_slots // bm - 1

    def rows_of_block(b, be, bc, ne):
        return (jnp.where(bc[b] > 0, b, last), 0)

    grid_spec = pltpu.PrefetchScalarGridSpec(
        num_scalar_prefetch=3,
        grid=(n_slots // bm,),
        in_specs=[pl.BlockSpec((bm, w), rows_of_block),
                  pl.BlockSpec(memory_space=pl.ANY),
                  pl.BlockSpec((1, 1, de2), lambda b, be, bc, ne: (be[b], 0, 0)),
                  pl.BlockSpec(memory_space=pl.ANY),
                  pl.BlockSpec((1, 1, d), lambda b, be, bc, ne: (be[b], 0, 0))],
        out_specs=pl.BlockSpec((bm, w), rows_of_block),
        scratch_shapes=[pltpu.VMEM((d, de2), f32), pltpu.VMEM((de, d), f32),
                        pltpu.VMEM((d, de2), bf16), pltpu.VMEM((de, d), bf16),
                        pltpu.SemaphoreType.DMA((2,))],
    )
    return pl.pallas_call(
        _expert_kernel,
        grid_spec=grid_spec,
        out_shape=jax.ShapeDtypeStruct((n_slots, w), u32),
        compiler_params=_params("arbitrary"),
        name="moe_experts",
    )(blk_e, blk_cnt, nxt_e, xs, w_gu, b_gu.reshape(e, 1, de2), w_down, b_down.reshape(e, 1, d))


def _sc_gather_rows(table, idx):
    n, w = idx.shape[0], table.shape[1]
    workers = SC_CORES * SC_SUBCORES
    chunk = SC_GATHER_CHUNK
    per = n // workers
    steps = per // chunk
    assert per * workers == n and steps * chunk == per and steps % 2 == 0

    def body(table_hbm, idx_hbm, out_hbm, idx_v, rows_v, gsem, wsem):
        base = (lax.axis_index("s") * SC_CORES + lax.axis_index("c")) * per
        pltpu.sync_copy(idx_hbm.at[pl.ds(base, per)], idx_v)

        def gather(c, slot):
            return pltpu.make_async_copy(table_hbm.at[idx_v.at[pl.ds(c * chunk, chunk)]], rows_v.at[slot],
                                         gsem.at[slot])

        def write(c, slot):
            return pltpu.make_async_copy(rows_v.at[slot], out_hbm.at[pl.ds(base + c * chunk, chunk)],
                                         wsem.at[slot])

        gather(0, 0).start()

        @pl.loop(0, steps, step=2)
        def _(c0):
            for slot in range(2):
                c = c0 + slot
                gather(c, slot).wait()
                write(c, slot).start()

                @pl.when(c >= 1)
                def _():
                    write(c - 1, 1 - slot).wait()

                @pl.when(c + 1 < steps)
                def _():
                    gather(c + 1, 1 - slot).start()

        write(steps - 1, 1).wait()

    return pl.kernel(
        body,
        out_type=jax.ShapeDtypeStruct((n, w), table.dtype),
        mesh=_sc_mesh(),
        scratch_types=[pltpu.VMEM((per,), i32), pltpu.VMEM((2, chunk, w), table.dtype),
                       pltpu.SemaphoreType.DMA((2,)), pltpu.SemaphoreType.DMA((2,))],
        name="sc_gather_rows",
    )(table, idx)


def _combine_kernel(x1_ref, gate_ref, gfin_ref, yg_ref, *rest):
    o_ref = rest[-1]
    acc = x1_ref[...]
    gates = gate_ref[...]
    for j in range(TOP_K):
        acc = acc + gates[:, j:j + 1] * _unpack_bf16_pairs(yg_ref[j])
    o_ref[...] = _rms_scale(acc) * gfin_ref[...]


def _combine(x1, gates, dest, y_slots, g_final, out_prev, tile0, total_tokens):
    t, d = x1.shape
    tm = TOKEN_TILE
    w = y_slots.shape[1]
    yg = _sc_gather_rows(y_slots, dest.reshape(-1)).reshape(TOP_K, t, w)
    in_specs = [pl.BlockSpec((tm, d), lambda i: (i, 0)),
                pl.BlockSpec((tm, TOP_K), lambda i: (i, 0)),
                pl.BlockSpec((1, d), lambda i: (0, 0)),
                pl.BlockSpec((TOP_K, tm, w), lambda i: (0, i, 0))]
    args = [x1, gates, g_final.reshape(1, d), yg]
    aliases = {}
    if out_prev is not None:
        in_specs.append(pl.BlockSpec(memory_space=pl.ANY))
        args.append(out_prev)
        aliases = {len(args) - 1: 0}
    return pl.pallas_call(
        _combine_kernel,
        grid=(t // tm,),
        in_specs=in_specs,
        out_specs=pl.BlockSpec((tm, d), lambda i: (i + tile0, 0)),
        out_shape=jax.ShapeDtypeStruct((total_tokens, d), f32),
        input_output_aliases=aliases,
        compiler_params=_params("parallel"),
        name="moe_combine",
    )(*args)


def _dest_kernel(start_ref, eidx_ref, rank_ref, dest_ref):
    eidx = eidx_ref[...]
    dest = rank_ref[...]
    for e in range(N_EXPERTS):
        dest = dest + jnp.where(eidx == e, start_ref[e], 0)
    dest_ref[...] = dest


def _slot_layout(counts, eidx, rank, n_blocks):
    bm = MOE_BLOCK
    padded = (counts + bm - 1) // bm * bm
    padded_end = jnp.cumsum(padded)
    start = padded_end - padded
    experts = jnp.arange(N_EXPERTS, dtype=i32)
    lookup = lambda table, idx: jnp.sum(jnp.where(idx[..., None] == experts, table, 0), axis=-1)
    dest = pl.pallas_call(
        _dest_kernel,
        grid_spec=pltpu.PrefetchScalarGridSpec(
            num_scalar_prefetch=1, grid=(1,),
            in_specs=[pl.BlockSpec(eidx.shape, lambda i, st: (0, 0)), pl.BlockSpec(rank.shape, lambda i, st: (0, 0))],
            out_specs=pl.BlockSpec(rank.shape, lambda i, st: (0, 0))),
        out_shape=jax.ShapeDtypeStruct(rank.shape, i32),
        name="moe_dest",
    )(start.astype(i32), eidx, rank)
    blk_row = jnp.arange(n_blocks, dtype=i32) * bm
    blk_e = jnp.minimum(jnp.sum((padded_end[None, :] <= blk_row[:, None]).astype(i32), axis=1), N_EXPERTS - 1)
    blk_cnt = jnp.clip(lookup(counts, blk_e) - (blk_row - lookup(start, blk_e)), 0, bm).astype(i32)
    none = jnp.int32(N_EXPERTS)
    nxt_e = jnp.min(jnp.where(blk_e[None, :] > blk_e[:, None], blk_e[None, :], none), axis=1)
    nxt_e = jnp.where(nxt_e == none, -1, nxt_e).astype(i32)
    return dest, blk_e, blk_cnt, nxt_e


def _layer_and_final_norm(x2d, mem, seq, g_mix, g_mem, w_in, w_mem_kv, na_rel_bias, g_grp, w_out, g_ffn,
                          router_w, router_b, w_gu, b_gu, w_down, b_down, g_final):
    t, d = x2d.shape
    b = t // seq
    q_na, k_na, v_na, u_ft, q_mem = _in_proj(x2d, g_mix, w_in.astype(bf16))
    k_mem, v_mem = _mem_kv(mem, g_mem, w_mem_kv.astype(bf16))
    shape3 = lambda a: a.reshape(b, seq, a.shape[-1])
    y_na = _neighbourhood_attention(shape3(q_na), shape3(k_na), shape3(v_na), _na_bias_table(na_rel_bias))
    y_ft = _fourier_mix(shape3(u_ft), _ft_tables(seq))
    y_na, y_ft = y_na.reshape(t, -1), y_ft.reshape(t, -1)
    w_out_bf16 = w_out.astype(bf16)
    rw_hi = router_w.astype(bf16)
    router_w2 = jnp.stack([rw_hi, (router_w - rw_hi.astype(f32)).astype(bf16)])

    tiles = t // TOKEN_TILE
    unit = tiles // sum(MOE_GROUP_SHARES)
    assert unit * sum(MOE_GROUP_SHARES) == tiles
    out = None
    tile0 = 0
    dest = router_b
    for share in MOE_GROUP_SHARES:
        group_tiles = share * unit
        n_blocks = (group_tiles * TOKEN_TILE * TOP_K) // MOE_BLOCK + N_EXPERTS
        x1, h2p, eidx, gates, rank, counts = _mix_out(
            x2d, y_na, y_ft, q_mem, k_mem, v_mem, g_grp, w_out_bf16, g_ffn, router_w2, router_b, seq,
            tile0, group_tiles, dest)
        dest, blk_e, blk_cnt, nxt_e = _slot_layout(counts[:, 0], eidx, rank, n_blocks)
        xs = _dispatch(h2p, dest, n_blocks * MOE_BLOCK)
        y_slots = _experts(xs, blk_e, blk_cnt, nxt_e, w_gu, b_gu, w_down, b_down)
        out = _combine(x1, gates, dest, y_slots, g_final, out, tile0, t)
        tile0 += group_tiles
    return out


def kernel(x, mem, g_mix, g_mem, w_in, w_mem_kv, na_rel_bias, g_grp, w_out, g_ffn, router_w, router_b,
           w_gu, b_gu, w_down, b_down, g_final):
    b, seq, d = x.shape
    depth = w_in.shape[0]
    assert depth == 1, "the final norm is fused into the single layer's combine step"
    out = _layer_and_final_norm(
        x.reshape(b * seq, d), mem, seq, g_mix[0], g_mem[0], w_in[0], w_mem_kv[0], na_rel_bias[0], g_grp[0],
        w_out[0], g_ffn[0], router_w[0], router_b[0], w_gu[0], b_gu[0], w_down[0], b_down[0], g_final)
    return out.reshape(b, seq, d)
```
